```python
import math
import jax, jax.numpy as jnp
from jax import lax
import numpy as np

D_MODEL = 1024
BATCH = 8
SEQ = 8192
DEPTH = 1

MIX_WIDTH = D_MODEL
GLA_WIDTH = MIX_WIDTH // 2
GLA_HEADS = 4
GLA_DV = GLA_WIDTH // GLA_HEADS
GLA_DK = GLA_DV // 2
GLA_GATE_RANK = 16
GLA_GATE_NORMALIZER = 16.0
GLA_CHUNK = 64
SB_WIDTH = MIX_WIDTH - GLA_WIDTH
SB_HEAD_DIM = 64
SB_HEADS = SB_WIDTH // SB_HEAD_DIM
SB_BLOCK = 128
MEM_LEN = 256
MEM_HEADS = 4
MEM_HEAD_DIM = D_MODEL // MEM_HEADS
D_FF = -(-8 * D_MODEL // (3 * 256)) * 256
RMS_EPS = 1e-6
GLA_QK_W = GLA_HEADS * GLA_DK
GLA_V_W = GLA_HEADS * GLA_DV
IN_SIZES = (GLA_QK_W, GLA_QK_W, GLA_V_W, GLA_V_W, GLA_GATE_RANK, SB_WIDTH, SB_WIDTH, SB_WIDTH)
D_IN = sum(IN_SIZES)

kernel_name = "hybrid_gla_stickbreaking_memxattn_swiglu"


def rms_norm(x, w):
    xf = x.astype(jnp.float32)
    y = xf * lax.rsqrt(jnp.mean(xf * xf, axis=-1, keepdims=True) + RMS_EPS)
    return (y * w.astype(jnp.float32)).astype(x.dtype)


def split_points():
    return [int(v) for v in np.cumsum(np.array(IN_SIZES))[:-1]]


def gla_chunked(q, k, v, gk):
    B, T, H, dk = q.shape
    dv = v.shape[-1]
    C = GLA_CHUNK
    N = T // C
    out_dtype = v.dtype

    def to_chunks(a):
        d = a.shape[-1]
        return a.astype(jnp.float32).reshape(B, N, C, H, d).transpose(1, 0, 3, 2, 4)

    qc, kc, vc, gc = to_chunks(q), to_chunks(k), to_chunks(v), to_chunks(gk)
    causal = jnp.tril(jnp.ones((C, C), dtype=bool))[:, :, None]

    def step(S, inp):
        qi, ki, vi, gi = inp
        b = jnp.cumsum(gi, axis=2)
        o_inter = jnp.einsum('bhcd,bhde->bhce', qi * jnp.exp(b), S)
        diff = b[:, :, :, None, :] - b[:, :, None, :, :]
        decay = jnp.where(causal, jnp.exp(jnp.where(causal, diff, 0.0)), 0.0)
        A = jnp.einsum('bhid,bhjd,bhijd->bhij', qi, ki, decay)
        o = o_inter + jnp.einsum('bhij,bhje->bhie', A, vi)
        b_last = b[:, :, -1]
        k_dec = ki * jnp.exp(b_last[:, :, None, :] - b)
        S = jnp.exp(b_last)[..., None] * S + jnp.einsum('bhcd,bhce->bhde', k_dec, vi)
        return S, o

    S0 = jnp.zeros((B, H, dk, dv), jnp.float32)
    _, o = lax.scan(step, S0, (qc, kc, vc, gc))
    return o.transpose(1, 0, 3, 2, 4).reshape(B, T, H, dv).astype(out_dtype)


def stick_breaking_attention(q, k, v):
    B, H, T, d = q.shape
    scale = 1.0 / math.sqrt(d)
    outs = []
    for i in range(T // SB_BLOCK):
        end = (i + 1) * SB_BLOCK
        q_blk = q[:, :, i * SB_BLOCK:end]
        k_blk = k[:, :, :end]
        v_blk = v[:, :, :end]
        z = jnp.einsum('bhqd,bhkd->bhqk', q_blk, k_blk).astype(jnp.float32) * scale
        t_idx = i * SB_BLOCK + jnp.arange(SB_BLOCK)[:, None]
        s_idx = jnp.arange(end)[None, :]
        strict = s_idx < t_idx
        log_beta = jax.nn.log_sigmoid(z)
        log_1m = jnp.where(strict, jax.nn.log_sigmoid(-z), 0.0)
        rev = lax.cumsum(log_1m, axis=3, reverse=True)
        log_A = log_beta + rev - log_1m
        A = jnp.where(strict, jnp.exp(log_A), 0.0)
        outs.append(jnp.einsum('bhqk,bhkd->bhqd', A, v_blk.astype(jnp.float32)).astype(v.dtype))
    return jnp.concatenate(outs, axis=2)


def parallel_mixer(h, w_in, w_gk_up, b_gk, gla_norm_w, sb_norm_w, w_out):
    B, T, _ = h.shape
    proj = h @ w_in
    q_g, k_g, v_g, g_g, gk_lr, q_s, k_s, v_s = jnp.split(proj, split_points(), axis=-1)
    q_g = q_g.reshape(B, T, GLA_HEADS, GLA_DK) * (GLA_DK ** -0.5)
    k_g = k_g.reshape(B, T, GLA_HEADS, GLA_DK)
    v_g = v_g.reshape(B, T, GLA_HEADS, GLA_DV)
    gk = (jax.nn.log_sigmoid((gk_lr @ w_gk_up + b_gk).astype(jnp.float32)) / GLA_GATE_NORMALIZER)
    gk = gk.reshape(B, T, GLA_HEADS, GLA_DK)
    o_g = gla_chunked(q_g, k_g, v_g, gk)
    o_g = rms_norm(o_g, gla_norm_w) * jax.nn.silu(g_g.reshape(B, T, GLA_HEADS, GLA_DV))
    o_g = o_g.reshape(B, T, GLA_V_W)
    def heads(a):
        return a.reshape(B, T, SB_HEADS, SB_HEAD_DIM).transpose(0, 2, 1, 3)
    o_s = stick_breaking_attention(heads(q_s), heads(k_s), heads(v_s)).transpose(0, 2, 1, 3)
    o_s = rms_norm(o_s, sb_norm_w).reshape(B, T, SB_WIDTH)
    return jnp.concatenate([o_g, o_s], axis=-1) @ w_out


def memory_cross_attention(h, mem_n, w_mq, w_mkv, mq_norm_w, mk_norm_w, w_mo):
    B, T, _ = h.shape
    M = mem_n.shape[1]
    q = rms_norm((h @ w_mq).reshape(B, T, MEM_HEADS, MEM_HEAD_DIM), mq_norm_w)
    kv = mem_n @ w_mkv
    k, v = jnp.split(kv, 2, axis=-1)
    k = rms_norm(k.reshape(B, M, MEM_HEADS, MEM_HEAD_DIM), mk_norm_w)
    v = v.reshape(B, M, MEM_HEADS, MEM_HEAD_DIM)
    s = jnp.einsum('bthd,bmhd->bhtm', q, k).astype(jnp.float32) / math.sqrt(MEM_HEAD_DIM)
    p = jax.nn.softmax(s, axis=-1).astype(v.dtype)
    o = jnp.einsum('bhtm,bmhd->bthd', p, v).reshape(B, T, D_MODEL)
    return o @ w_mo


def swiglu(h, w_gate_up, w_down):
    gate, up = jnp.split(h @ w_gate_up, 2, axis=-1)
    return (jax.nn.silu(gate) * up) @ w_down


def _fwd_setup_inputs(seed: int = 0) -> dict:
    key = jax.random.key(seed)
    ks = jax.random.split(key, 24)
    f32 = jnp.float32

    def w(k, shape, fan_in):
        return jax.random.normal(k, (DEPTH,) + shape, f32) * (fan_in ** -0.5)

    def gain(k, n):
        return 1.0 + 0.01 * jax.random.normal(k, (DEPTH, n), f32)

    return {
        "x": jax.random.normal(ks[0], (BATCH, SEQ, D_MODEL), f32),
        "mem": jax.random.normal(ks[1], (BATCH, MEM_LEN, D_MODEL), f32),
        "mix_norm_w": gain(ks[2], D_MODEL),
        "w_in": w(ks[3], (D_MODEL, D_IN), D_MODEL),
        "w_gk_up": w(ks[4], (GLA_GATE_RANK, GLA_QK_W), GLA_GATE_RANK),
        "b_gk": 0.1 * jax.random.normal(ks[5], (DEPTH, GLA_QK_W), f32),
        "gla_norm_w": gain(ks[6], GLA_DV),
        "sb_norm_w": gain(ks[7], SB_HEAD_DIM),
        "w_out": w(ks[8], (MIX_WIDTH, D_MODEL), MIX_WIDTH),
        "xattn_norm_w": gain(ks[9], D_MODEL),
        "mem_norm_w": gain(ks[10], D_MODEL),
        "w_mq": w(ks[11], (D_MODEL, D_MODEL), D_MODEL),
        "w_mkv": w(ks[12], (D_MODEL, 2 * D_MODEL), D_MODEL),
        "mq_norm_w": gain(ks[13], MEM_HEAD_DIM),
        "mk_norm_w": gain(ks[14], MEM_HEAD_DIM),
        "w_mo": w(ks[15], (D_MODEL, D_MODEL), D_MODEL),
        "ffn_norm_w": gain(ks[16], D_MODEL),
        "w_gate_up": w(ks[17], (D_MODEL, 2 * D_FF), D_MODEL),
        "w_down": w(ks[18], (D_FF, D_MODEL), D_FF),
    }


def _fwd_reference(x, mem, mix_norm_w, w_in, w_gk_up, b_gk, gla_norm_w, sb_norm_w, w_out,
              xattn_norm_w, mem_norm_w, w_mq, w_mkv, mq_norm_w, mk_norm_w, w_mo,
              ffn_norm_w, w_gate_up, w_down):
    for l in range(DEPTH):
        h = rms_norm(x, mix_norm_w[l])
        x = x + parallel_mixer(h, w_in[l], w_gk_up[l], b_gk[l], gla_norm_w[l], sb_norm_w[l], w_out[l])
        h = rms_norm(x, xattn_norm_w[l])
        mem_n = rms_norm(mem, mem_norm_w[l])
        x = x + memory_cross_attention(h, mem_n, w_mq[l], w_mkv[l], mq_norm_w[l], mk_norm_w[l], w_mo[l])
        h = rms_norm(x, ffn_norm_w[l])
        x = x + swiglu(h, w_gate_up[l], w_down[l])
    return x


import jax as _jax
import jax.numpy as _jnp

TWIN_FORMAT = 'train_step'
FWD_PARAMS = ['x', 'mem', 'mix_norm_w', 'w_in', 'w_gk_up', 'b_gk', 'gla_norm_w', 'sb_norm_w', 'w_out', 'xattn_norm_w', 'mem_norm_w', 'w_mq', 'w_mkv', 'mq_norm_w', 'mk_norm_w', 'w_mo', 'ffn_norm_w', 'w_gate_up', 'w_down']
TWIN_WEIGHTS = ['mix_norm_w', 'w_in', 'w_gk_up', 'b_gk', 'gla_norm_w', 'sb_norm_w', 'w_out', 'xattn_norm_w', 'mem_norm_w', 'w_mq', 'w_mkv', 'mq_norm_w', 'mk_norm_w', 'w_mo', 'ffn_norm_w', 'w_gate_up', 'w_down']
TWIN_DIFF_INPUT = 'x'
TWIN_INPUTS = ['x', 'mem', 'mix_norm_w', 'w_in', 'w_gk_up', 'b_gk', 'gla_norm_w', 'sb_norm_w', 'w_out', 'xattn_norm_w', 'mem_norm_w', 'w_mq', 'w_mkv', 'mq_norm_w', 'mk_norm_w', 'w_mo', 'ffn_norm_w', 'w_gate_up', 'w_down', 'loss_target', 'm_mix_norm_w', 'm_w_in', 'm_w_gk_up', 'm_b_gk', 'm_gla_norm_w', 'm_sb_norm_w', 'm_w_out', 'm_xattn_norm_w', 'm_mem_norm_w', 'm_w_mq', 'm_w_mkv', 'm_mq_norm_w', 'm_mk_norm_w', 'm_w_mo', 'm_ffn_norm_w', 'm_w_gate_up', 'm_w_down', 'v_mix_norm_w', 'v_w_in', 'v_w_gk_up', 'v_b_gk', 'v_gla_norm_w', 'v_sb_norm_w', 'v_w_out', 'v_xattn_norm_w', 'v_mem_norm_w', 'v_w_mq', 'v_w_mkv', 'v_mq_norm_w', 'v_mk_norm_w', 'v_w_mo', 'v_ffn_norm_w', 'v_w_gate_up', 'v_w_down']
TWIN_OUTPUTS = ['loss', 'grad_x', 'grad_mix_norm_w', 'grad_w_in', 'grad_w_gk_up', 'grad_b_gk', 'grad_gla_norm_w', 'grad_sb_norm_w', 'grad_w_out', 'grad_xattn_norm_w', 'grad_mem_norm_w', 'grad_w_mq', 'grad_w_mkv', 'grad_mq_norm_w', 'grad_mk_norm_w', 'grad_w_mo', 'grad_ffn_norm_w', 'grad_w_gate_up', 'grad_w_down', 'delta_mix_norm_w', 'delta_w_in', 'delta_w_gk_up', 'delta_b_gk', 'delta_gla_norm_w', 'delta_sb_norm_w', 'delta_w_out', 'delta_xattn_norm_w', 'delta_mem_norm_w', 'delta_w_mq', 'delta_w_mkv', 'delta_mq_norm_w', 'delta_mk_norm_w', 'delta_w_mo', 'delta_ffn_norm_w', 'delta_w_gate_up', 'delta_w_down', 'new_m_mix_norm_w', 'new_m_w_in', 'new_m_w_gk_up', 'new_m_b_gk', 'new_m_gla_norm_w', 'new_m_sb_norm_w', 'new_m_w_out', 'new_m_xattn_norm_w', 'new_m_mem_norm_w', 'new_m_w_mq', 'new_m_w_mkv', 'new_m_mq_norm_w', 'new_m_mk_norm_w', 'new_m_w_mo', 'new_m_ffn_norm_w', 'new_m_w_gate_up', 'new_m_w_down', 'new_v_mix_norm_w', 'new_v_w_in', 'new_v_w_gk_up', 'new_v_b_gk', 'new_v_gla_norm_w', 'new_v_sb_norm_w', 'new_v_w_out', 'new_v_xattn_norm_w', 'new_v_mem_norm_w', 'new_v_w_mq', 'new_v_w_mkv', 'new_v_mq_norm_w', 'new_v_mk_norm_w', 'new_v_w_mo', 'new_v_ffn_norm_w', 'new_v_w_gate_up', 'new_v_w_down']
TWIN_LEAF_KINDS = {'loss': 'loss', 'grad_x': 'grad_x', 'grad_mix_norm_w': 'grad_w', 'grad_w_in': 'grad_w', 'grad_w_gk_up': 'grad_w', 'grad_b_gk': 'grad_w', 'grad_gla_norm_w': 'grad_w', 'grad_sb_norm_w': 'grad_w', 'grad_w_out': 'grad_w', 'grad_xattn_norm_w': 'grad_w', 'grad_mem_norm_w': 'grad_w', 'grad_w_mq': 'grad_w', 'grad_w_mkv': 'grad_w', 'grad_mq_norm_w': 'grad_w', 'grad_mk_norm_w': 'grad_w', 'grad_w_mo': 'grad_w', 'grad_ffn_norm_w': 'grad_w', 'grad_w_gate_up': 'grad_w', 'grad_w_down': 'grad_w', 'delta_mix_norm_w': 'delta_w', 'delta_w_in': 'delta_w', 'delta_w_gk_up': 'delta_w', 'delta_b_gk': 'delta_w', 'delta_gla_norm_w': 'delta_w', 'delta_sb_norm_w': 'delta_w', 'delta_w_out': 'delta_w', 'delta_xattn_norm_w': 'delta_w', 'delta_mem_norm_w': 'delta_w', 'delta_w_mq': 'delta_w', 'delta_w_mkv': 'delta_w', 'delta_mq_norm_w': 'delta_w', 'delta_mk_norm_w': 'delta_w', 'delta_w_mo': 'delta_w', 'delta_ffn_norm_w': 'delta_w', 'delta_w_gate_up': 'delta_w', 'delta_w_down': 'delta_w', 'new_m_mix_norm_w': 'new_m', 'new_m_w_in': 'new_m', 'new_m_w_gk_up': 'new_m', 'new_m_b_gk': 'new_m', 'new_m_gla_norm_w': 'new_m', 'new_m_sb_norm_w': 'new_m', 'new_m_w_out': 'new_m', 'new_m_xattn_norm_w': 'new_m', 'new_m_mem_norm_w': 'new_m', 'new_m_w_mq': 'new_m', 'new_m_w_mkv': 'new_m', 'new_m_mq_norm_w': 'new_m', 'new_m_mk_norm_w': 'new_m', 'new_m_w_mo': 'new_m', 'new_m_ffn_norm_w': 'new_m', 'new_m_w_gate_up': 'new_m', 'new_m_w_down': 'new_m', 'new_v_mix_norm_w': 'new_v', 'new_v_w_in': 'new_v', 'new_v_w_gk_up': 'new_v', 'new_v_b_gk': 'new_v', 'new_v_gla_norm_w': 'new_v', 'new_v_sb_norm_w': 'new_v', 'new_v_w_out': 'new_v', 'new_v_xattn_norm_w': 'new_v', 'new_v_mem_norm_w': 'new_v', 'new_v_w_mq': 'new_v', 'new_v_w_mkv': 'new_v', 'new_v_mq_norm_w': 'new_v', 'new_v_mk_norm_w': 'new_v', 'new_v_w_mo': 'new_v', 'new_v_ffn_norm_w': 'new_v', 'new_v_w_gate_up': 'new_v', 'new_v_w_down': 'new_v'}


def _forward(args):
    return _fwd_reference(*[args[k] for k in FWD_PARAMS])


def _output_shape():
    def fwd():
        inp = _fwd_setup_inputs(0)
        return _fwd_reference(*[inp[k] for k in FWD_PARAMS])
    out = _jax.eval_shape(fwd)
    return out.shape, out.dtype

N_MICROBATCH = 1
ADAM_LR = 0.001
ADAM_B1 = 0.9
ADAM_B2 = 0.999
ADAM_EPS = 1e-08
ADAM_WD = 0.01
ADAM_STEP = 10
PER_EXAMPLE_BATCH_AXIS = {'x': 0, 'mem': 0, 'loss_target': 0}
SHARED_INPUTS = []
_WEIGHT_DTYPES = {'mix_norm_w': _jnp.float32, 'w_in': _jnp.float32, 'w_gk_up': _jnp.float32, 'b_gk': _jnp.float32, 'gla_norm_w': _jnp.float32, 'sb_norm_w': _jnp.float32, 'w_out': _jnp.float32, 'xattn_norm_w': _jnp.float32, 'mem_norm_w': _jnp.float32, 'w_mq': _jnp.float32, 'w_mkv': _jnp.float32, 'mq_norm_w': _jnp.float32, 'mk_norm_w': _jnp.float32, 'w_mo': _jnp.float32, 'ffn_norm_w': _jnp.float32, 'w_gate_up': _jnp.float32, 'w_down': _jnp.float32}
MOMENT_SCALE = {'mix_norm_w': 1.290339e+01, 'w_in': 5.474635e-01, 'w_gk_up': 6.562329e-02, 'b_gk': 2.681478e-01, 'gla_norm_w': 9.130861e+01, 'sb_norm_w': 5.103313e+02, 'w_out': 1.311016e+00, 'xattn_norm_w': 7.734702e-02, 'mem_norm_w': 6.951747e-01, 'w_mq': 8.892504e-02, 'w_mkv': 1.392025e-01, 'mq_norm_w': 2.606267e+00, 'mk_norm_w': 2.604617e+00, 'w_mo': 1.733193e-01, 'ffn_norm_w': 4.952910e+01, 'w_gate_up': 3.099322e-01, 'w_down': 5.352867e-01}


def _to_microbatches(a, axis):
    t = _jnp.moveaxis(a, axis, 0)
    t = t.reshape((N_MICROBATCH, t.shape[0] // N_MICROBATCH) + t.shape[1:])
    return _jnp.moveaxis(t, 1, axis + 1)


def setup_inputs(seed: int = 0) -> dict:
    inp = _fwd_setup_inputs(seed)
    key = _jax.random.fold_in(_jax.random.key(seed), 7919)
    shape, _ = _output_shape()
    out = dict(inp)
    out["loss_target"] = _jax.random.normal(_jax.random.fold_in(key, 0), shape, _jnp.float32)
    for i, name in enumerate(TWIN_WEIGHTS):
        w = inp[name].astype(_jnp.float32)
        if MOMENT_SCALE is None:
            s = _jnp.sqrt(_jnp.mean(_jnp.square(w)) + 1e-30)
        else:
            s = MOMENT_SCALE[name]
        km, kv = _jax.random.split(_jax.random.fold_in(key, i + 1))
        out[name] = w
        out["m_" + name] = s * _jax.random.normal(km, w.shape, _jnp.float32)
        out["v_" + name] = (s * s) * _jax.random.uniform(kv, w.shape, _jnp.float32, 0.5, 1.5)
    if N_MICROBATCH > 1:
        for name, axis in PER_EXAMPLE_BATCH_AXIS.items():
            out[name] = _to_microbatches(out[name], axis)
    return {'x': out['x'], 'mem': out['mem'], 'mix_norm_w': out['mix_norm_w'], 'w_in': out['w_in'], 'w_gk_up': out['w_gk_up'], 'b_gk': out['b_gk'], 'gla_norm_w': out['gla_norm_w'], 'sb_norm_w': out['sb_norm_w'], 'w_out': out['w_out'], 'xattn_norm_w': out['xattn_norm_w'], 'mem_norm_w': out['mem_norm_w'], 'w_mq': out['w_mq'], 'w_mkv': out['w_mkv'], 'mq_norm_w': out['mq_norm_w'], 'mk_norm_w': out['mk_norm_w'], 'w_mo': out['w_mo'], 'ffn_norm_w': out['ffn_norm_w'], 'w_gate_up': out['w_gate_up'], 'w_down': out['w_down'], 'loss_target': out['loss_target'], 'm_mix_norm_w': out['m_mix_norm_w'], 'm_w_in': out['m_w_in'], 'm_w_gk_up': out['m_w_gk_up'], 'm_b_gk': out['m_b_gk'], 'm_gla_norm_w': out['m_gla_norm_w'], 'm_sb_norm_w': out['m_sb_norm_w'], 'm_w_out': out['m_w_out'], 'm_xattn_norm_w': out['m_xattn_norm_w'], 'm_mem_norm_w': out['m_mem_norm_w'], 'm_w_mq': out['m_w_mq'], 'm_w_mkv': out['m_w_mkv'], 'm_mq_norm_w': out['m_mq_norm_w'], 'm_mk_norm_w': out['m_mk_norm_w'], 'm_w_mo': out['m_w_mo'], 'm_ffn_norm_w': out['m_ffn_norm_w'], 'm_w_gate_up': out['m_w_gate_up'], 'm_w_down': out['m_w_down'], 'v_mix_norm_w': out['v_mix_norm_w'], 'v_w_in': out['v_w_in'], 'v_w_gk_up': out['v_w_gk_up'], 'v_b_gk': out['v_b_gk'], 'v_gla_norm_w': out['v_gla_norm_w'], 'v_sb_norm_w': out['v_sb_norm_w'], 'v_w_out': out['v_w_out'], 'v_xattn_norm_w': out['v_xattn_norm_w'], 'v_mem_norm_w': out['v_mem_norm_w'], 'v_w_mq': out['v_w_mq'], 'v_w_mkv': out['v_w_mkv'], 'v_mq_norm_w': out['v_mq_norm_w'], 'v_mk_norm_w': out['v_mk_norm_w'], 'v_w_mo': out['v_w_mo'], 'v_ffn_norm_w': out['v_ffn_norm_w'], 'v_w_gate_up': out['v_w_gate_up'], 'v_w_down': out['v_w_down']}


def _loss(weights, diff, rest, loss_target):
    with _jax.named_scope("forward"):
        args = {**rest, TWIN_DIFF_INPUT: diff, **{k: w.astype(_WEIGHT_DTYPES[k]) for k, w in weights.items()}}
        y = _forward(args)
    with _jax.named_scope("loss_head"):
        err = _jnp.square(y.astype(_jnp.float32) - loss_target)
        return 0.5 * _jnp.sum(_jnp.mean(err, axis=-1)) if err.ndim else 0.5 * err


def _adamw(w, g, m, v):
    m = ADAM_B1 * m + (1.0 - ADAM_B1) * g
    v = ADAM_B2 * v + (1.0 - ADAM_B2) * _jnp.square(g)
    m_hat = m / (1.0 - ADAM_B1 ** ADAM_STEP)
    v_hat = v / (1.0 - ADAM_B2 ** ADAM_STEP)
    delta = -ADAM_LR * (m_hat / (_jnp.sqrt(v_hat) + ADAM_EPS) + ADAM_WD * w)
    return delta, m, v


def reference(x, mem, mix_norm_w, w_in, w_gk_up, b_gk, gla_norm_w, sb_norm_w, w_out, xattn_norm_w, mem_norm_w, w_mq, w_mkv, mq_norm_w, mk_norm_w, w_mo, ffn_norm_w, w_gate_up, w_down, loss_target, m_mix_norm_w, m_w_in, m_w_gk_up, m_b_gk, m_gla_norm_w, m_sb_norm_w, m_w_out, m_xattn_norm_w, m_mem_norm_w, m_w_mq, m_w_mkv, m_mq_norm_w, m_mk_norm_w, m_w_mo, m_ffn_norm_w, m_w_gate_up, m_w_down, v_mix_norm_w, v_w_in, v_w_gk_up, v_b_gk, v_gla_norm_w, v_sb_norm_w, v_w_out, v_xattn_norm_w, v_mem_norm_w, v_w_mq, v_w_mkv, v_mq_norm_w, v_mk_norm_w, v_w_mo, v_ffn_norm_w, v_w_gate_up, v_w_down):
    given = dict(x=x, mem=mem, mix_norm_w=mix_norm_w, w_in=w_in, w_gk_up=w_gk_up, b_gk=b_gk, gla_norm_w=gla_norm_w, sb_norm_w=sb_norm_w, w_out=w_out, xattn_norm_w=xattn_norm_w, mem_norm_w=mem_norm_w, w_mq=w_mq, w_mkv=w_mkv, mq_norm_w=mq_norm_w, mk_norm_w=mk_norm_w, w_mo=w_mo, ffn_norm_w=ffn_norm_w, w_gate_up=w_gate_up, w_down=w_down, loss_target=loss_target, m_mix_norm_w=m_mix_norm_w, m_w_in=m_w_in, m_w_gk_up=m_w_gk_up, m_b_gk=m_b_gk, m_gla_norm_w=m_gla_norm_w, m_sb_norm_w=m_sb_norm_w, m_w_out=m_w_out, m_xattn_norm_w=m_xattn_norm_w, m_mem_norm_w=m_mem_norm_w, m_w_mq=m_w_mq, m_w_mkv=m_w_mkv, m_mq_norm_w=m_mq_norm_w, m_mk_norm_w=m_mk_norm_w, m_w_mo=m_w_mo, m_ffn_norm_w=m_ffn_norm_w, m_w_gate_up=m_w_gate_up, m_w_down=m_w_down, v_mix_norm_w=v_mix_norm_w, v_w_in=v_w_in, v_w_gk_up=v_w_gk_up, v_b_gk=v_b_gk, v_gla_norm_w=v_gla_norm_w, v_sb_norm_w=v_sb_norm_w, v_w_out=v_w_out, v_xattn_norm_w=v_xattn_norm_w, v_mem_norm_w=v_mem_norm_w, v_w_mq=v_w_mq, v_w_mkv=v_w_mkv, v_mq_norm_w=v_mq_norm_w, v_mk_norm_w=v_mk_norm_w, v_w_mo=v_w_mo, v_ffn_norm_w=v_ffn_norm_w, v_w_gate_up=v_w_gate_up, v_w_down=v_w_down)
    weights = {n: given[n] for n in TWIN_WEIGHTS}
    shared = {n: given[n] for n in SHARED_INPUTS}
    per_example = {n: given[n] for n in ['x', 'mem']}
    grad_fn = _jax.value_and_grad(_loss, argnums=(0, 1))

    def one_microbatch(ex, loss_target):
        ex = dict(ex)
        diff = ex.pop(TWIN_DIFF_INPUT)
        return grad_fn(weights, diff, {**shared, **ex}, loss_target)

    if N_MICROBATCH == 1:
        loss, (grad_w, grad_x) = one_microbatch(per_example, given["loss_target"])
    else:
        def body(carry, xs):
            loss_sum, grad_sum = carry
            l_k, (gw_k, gx_k) = one_microbatch(xs[0], xs[1])
            with _jax.named_scope("update"):
                return (loss_sum + l_k, _jax.tree.map(_jnp.add, grad_sum, gw_k)), gx_k

        init = (_jnp.zeros((), _jnp.float32), _jax.tree.map(_jnp.zeros_like, weights))
        (loss, grad_w), grad_x = _jax.lax.scan(body, init, (per_example, given["loss_target"]))
    with _jax.named_scope("update"):
        delta_w, new_m, new_v = {}, {}, {}
        for n in TWIN_WEIGHTS:
            delta_w[n], new_m[n], new_v[n] = _adamw(weights[n], grad_w[n], given["m_" + n], given["v_" + n])
    return (loss, grad_x, *[grad_w[n] for n in TWIN_WEIGHTS], *[delta_w[n] for n in TWIN_WEIGHTS],
            *[new_m[n] for n in TWIN_WEIGHTS], *[new_v[n] for n in TWIN_WEIGHTS])
```

```python
import functools
import math

import numpy as np
import jax
import jax.numpy as jnp
from jax import lax
from jax.experimental import pallas as pl
from jax.experimental.pallas import tpu as pltpu

F32 = jnp.float32
BF16 = jnp.bfloat16
SDS = jax.ShapeDtypeStruct
MESH = pl.DeviceIdType.MESH

D = 1024
EPS = 1e-6
D_FF = 2816
GLA_GATE_NORM = 16.0
GLA_C = 64
MEM_HEADS = 4
MEM_HD = 256
C_QG, C_KG, C_VG, C_GG, C_QS, C_KS, C_VS, C_LR = 0, 256, 512, 1024, 1536, 2048, 2560, 3072
DIN = 3088
DIN_P = 3200
TB = 512
SBQ = 256
VMEM_LIMIT = 56 * 1024 * 1024
HIGHEST = lax.Precision.HIGHEST

ADAM_LR, ADAM_B1, ADAM_B2, ADAM_EPS, ADAM_WD, ADAM_STEP = 0.001, 0.9, 0.999, 1e-08, 0.01, 10

BIG = ("w_in", "w_gk_up", "w_out", "w_mq", "w_mkv", "w_mo", "w_gate_up", "w_down")
SMALL = ("mix_norm_w", "b_gk", "gla_norm_w", "sb_norm_w", "xattn_norm_w", "mem_norm_w", "mq_norm_w",
         "mk_norm_w", "ffn_norm_w")
WEIGHTS = ("mix_norm_w", "w_in", "w_gk_up", "b_gk", "gla_norm_w", "sb_norm_w", "w_out", "xattn_norm_w",
           "mem_norm_w", "w_mq", "w_mkv", "mq_norm_w", "mk_norm_w", "w_mo", "ffn_norm_w", "w_gate_up", "w_down")
PACK_ROWS = 4224
PACK_HALF = PACK_ROWS // 2
SMALL_ROWS = 8


def _params(**kw):
    return pltpu.CompilerParams(vmem_limit_bytes=VMEM_LIMIT, **kw)


def _row(c, j=0):
    return pl.BlockSpec((TB, c), lambda i, j=j: (i, j))


def _const(shape):
    return pl.BlockSpec(shape, lambda i: (0,) * len(shape))


def _dot(a, b):
    return lax.dot_general(a, b, (((1,), (0,)), ((), ())), preferred_element_type=F32)


def _dot_nt(a, b):
    return lax.dot_general(a, b, (((1,), (1,)), ((), ())), preferred_element_type=F32)


def _dot_tn(a, b):
    return lax.dot_general(a, b, (((0,), (0,)), ((), ())), preferred_element_type=F32)


def _dot_nt_f32(a, b):
    return lax.dot_general(a, b, (((1,), (1,)), ((), ())), precision=HIGHEST, preferred_element_type=F32)


def _split3(x):
    h = x.astype(BF16)
    r = x - h.astype(F32)
    m = r.astype(BF16)
    l = (r - m.astype(F32)).astype(BF16)
    return h, m, l


def _dot_exact(x, ones_mat):
    h, m, l = _split3(x)
    return _dot(h, ones_mat) + _dot(m, ones_mat) + _dot(l, ones_mat)


def _dot_hilo(x, ones_mat):
    h = x.astype(BF16)
    l = (x - h.astype(F32)).astype(BF16)
    return _dot(h, ones_mat) + _dot(l, ones_mat)


def _softplus(z):
    return jnp.maximum(z, 0.0) + jnp.log1p(jnp.exp(-jnp.abs(z)))


def _rsqrt_ms(x):
    return lax.rsqrt(jnp.mean(x * x, axis=-1, keepdims=True) + EPS)


def _colsum8(x):
    r, c = x.shape
    return jnp.sum(x.reshape(r // 8, 8, c), axis=0)


def _matmul(a, b, *, mode, tm, tn, tk=None, res=None, out_dtype=F32, name):
    if mode == "tn":
        K, M = a.shape
    else:
        M, K = a.shape
    N = b.shape[0] if mode == "nt" else b.shape[1]
    tk = K if tk is None else tk
    assert M % tm == 0 and N % tn == 0 and K % tk == 0, (name, M, N, K, tm, tn, tk)
    nk = K // tk
    if mode == "tn":
        a_spec = pl.BlockSpec((tk, tm), lambda j, i, k: (k, i))
    else:
        a_spec = pl.BlockSpec((tm, tk), lambda j, i, k: (i, k))
    if mode == "nt":
        b_spec = pl.BlockSpec((tn, tk), lambda j, i, k: (j, k))
    else:
        b_spec = pl.BlockSpec((tk, tn), lambda j, i, k: (k, j))
    o_spec = pl.BlockSpec((tm, tn), lambda j, i, k: (i, j))
    dot = {"nn": _dot, "nt": _dot_nt, "tn": _dot_tn}[mode]
    has_res = res is not None

    def body(*refs):
        a_ref, b_ref = refs[0], refs[1]
        res_ref = refs[2] if has_res else None
        o_ref = refs[2 + has_res]
        p = dot(a_ref[...].astype(BF16), b_ref[...].astype(BF16))
        if nk == 1:
            if has_res:
                p = p + res_ref[...]
            o_ref[...] = p.astype(out_dtype)
        else:
            acc_ref = refs[3 + has_res]
            k = pl.program_id(2)

            @pl.when(k == 0)
            def _():
                acc_ref[...] = p

            @pl.when(k > 0)
            def _():
                acc_ref[...] += p

            @pl.when(k == nk - 1)
            def _():
                t = acc_ref[...]
                if has_res:
                    t = t + res_ref[...]
                o_ref[...] = t.astype(out_dtype)

    in_specs = [a_spec, b_spec] + ([o_spec] if has_res else [])
    args = (a, b) + ((res,) if has_res else ())
    return pl.pallas_call(
        body, name=name, grid=(N // tn, M // tm, nk), in_specs=in_specs, out_specs=o_spec,
        out_shape=SDS((M, N), out_dtype),
        scratch_shapes=[pltpu.VMEM((tm, tn), F32)] if nk > 1 else [],
        compiler_params=_params(dimension_semantics=("parallel", "parallel", "arbitrary")),
    )(*args)


def _norm_fwd(x, w, name):
    T, dm = x.shape

    def body(x_ref, w_ref, h_ref):
        xv = x_ref[...]
        h_ref[...] = (xv * _rsqrt_ms(xv) * w_ref[...]).astype(BF16)

    return pl.pallas_call(
        body, name=name, grid=(T // TB,), in_specs=[_row(dm), _const((1, dm))], out_specs=_row(dm),
        out_shape=SDS((T, dm), BF16), compiler_params=_params())(x, w)


def _norm_bwd(dy, x, w, dres, name):
    T, dm = x.shape

    def body(dy_ref, x_ref, w_ref, dres_ref, dx_ref, dw_ref):
        @pl.when(pl.program_id(0) == 0)
        def _():
            dw_ref[...] = jnp.zeros_like(dw_ref)

        xv = x_ref[...]
        r = _rsqrt_ms(xv)
        n = xv * r
        dyv = dy_ref[...]
        dn = dyv * w_ref[...]
        dx_ref[...] = dres_ref[...] + r * (dn - n * jnp.mean(dn * n, axis=-1, keepdims=True))
        dw_ref[...] += _colsum8(dyv * n)

    return pl.pallas_call(
        body, name=name, grid=(T // TB,),
        in_specs=[_row(dm), _row(dm), _const((1, dm)), _row(dm)],
        out_specs=[_row(dm), _const((8, dm))],
        out_shape=[SDS((T, dm), F32), SDS((8, dm), F32)], compiler_params=_params())(dy, x, w, dres)


def _proj_split(proj, wgk, bgk):
    T = proj.shape[0]

    def body(lr_ref, q_ref, k_ref, v_ref, wgk_ref, b_ref, gk_ref, qs_ref, ks_ref, vs_ref):
        u = _dot(lr_ref[...].astype(BF16), wgk_ref[...].astype(BF16)) + b_ref[...]
        gk_ref[...] = -_softplus(-u) / GLA_GATE_NORM
        qs_ref[...] = (q_ref[...] * 0.125).astype(BF16)
        ks_ref[...] = k_ref[...].astype(BF16)
        vs_ref[...] = v_ref[...].astype(BF16)

    return pl.pallas_call(
        body, name="proj_split", grid=(T // TB,),
        in_specs=[_row(128, C_LR // 128), _row(512, C_QS // 512), _row(512, C_KS // 512), _row(512, C_VS // 512),
                  _const((128, 256)), _const((1, 256))],
        out_specs=[_row(256), _row(512), _row(512), _row(512)],
        out_shape=[SDS((T, 256), F32), SDS((T, 512), BF16), SDS((T, 512), BF16), SDS((T, 512), BF16)],
        compiler_params=_params())(proj, proj, proj, proj, wgk, bgk)


def _dproj_assemble(proj, dq_g, dk_g, dv_g, dg_g, dq_s, dk_s, dv_s, dgk, wgk, bgk):
    T = proj.shape[0]

    def body(lr_ref, dqg_ref, dkg_ref, dvg_ref, dgg_ref, dqs_ref, dks_ref, dvs_ref, dgk_ref, wgk_ref, b_ref,
             dp_ref, dwgk_ref, dbgk_ref):
        @pl.when(pl.program_id(0) == 0)
        def _():
            dwgk_ref[...] = jnp.zeros_like(dwgk_ref)
            dbgk_ref[...] = jnp.zeros_like(dbgk_ref)

        lr = lr_ref[...].astype(BF16)
        wg = wgk_ref[...].astype(BF16)
        u = _dot(lr, wg) + b_ref[...]
        du = dgk_ref[...] * (jax.nn.sigmoid(-u) / GLA_GATE_NORM)
        dub = du.astype(BF16)
        dp_ref[:, C_QG:C_KG] = (dqg_ref[...] * 0.125).astype(BF16)
        dp_ref[:, C_KG:C_VG] = dkg_ref[...].astype(BF16)
        dp_ref[:, C_VG:C_GG] = dvg_ref[...].astype(BF16)
        dp_ref[:, C_GG:C_QS] = dgg_ref[...].astype(BF16)
        dp_ref[:, C_QS:C_KS] = (dqs_ref[...] * 0.125).astype(BF16)
        dp_ref[:, C_KS:C_VS] = dks_ref[...].astype(BF16)
        dp_ref[:, C_VS:C_LR] = dvs_ref[...].astype(BF16)
        dp_ref[:, C_LR:DIN_P] = _dot_nt(dub, wg).astype(BF16)
        dwgk_ref[...] += _dot_tn(lr, dub)
        dbgk_ref[...] += _colsum8(du)

    return pl.pallas_call(
        body, name="dproj_assemble", grid=(T // TB,),
        in_specs=[_row(128, C_LR // 128), _row(256), _row(256), _row(512), _row(512), _row(512), _row(512),
                  _row(512), _row(256), _const((128, 256)), _const((1, 256))],
        out_specs=[_row(DIN_P), _const((128, 256)), _const((8, 256))],
        out_shape=[SDS((T, DIN_P), BF16), SDS((128, 256), F32), SDS((8, 256), F32)],
        compiler_params=_params())(proj, dq_g, dk_g, dv_g, dg_g, dq_s, dk_s, dv_s, dgk, wgk, bgk)


def _group_ones(n, g):
    idx = np.arange(n) // g
    return jnp.asarray((idx[:, None] == idx[None, :]).astype(np.float32), dtype=BF16)


def _mix_cat(o_g, proj, o_s, wg512, ws512, bd64):
    T = o_g.shape[0]

    def body(og_ref, gg_ref, os_ref, wg_ref, ws_ref, bd_ref, cat_ref):
        og = og_ref[...]
        gg = gg_ref[...]
        s = gg * jax.nn.sigmoid(gg)
        for h in range(4):
            sl = slice(128 * h, 128 * (h + 1))
            x = og[:, sl]
            cat_ref[:, sl] = (x * _rsqrt_ms(x) * wg_ref[:, sl] * s[:, sl]).astype(BF16)
        osv = os_ref[...]
        ms = _dot_exact(osv * osv, bd_ref[...]) * (1.0 / 64.0)
        cat_ref[:, 512:1024] = (osv * lax.rsqrt(ms + EPS) * ws_ref[...]).astype(BF16)

    return pl.pallas_call(
        body, name="mix_cat", grid=(T // TB,),
        in_specs=[_row(512), _row(512, C_GG // 512), _row(512), _const((1, 512)), _const((1, 512)),
                  _const((512, 512))],
        out_specs=_row(1024), out_shape=SDS((T, 1024), BF16), compiler_params=_params())(
            o_g, proj, o_s, wg512, ws512, bd64)


def _mix_cat_bwd(dcat, o_g, proj, o_s, wg512, ws512, bd64):
    T = o_g.shape[0]

    def body(dc_ref, og_ref, gg_ref, os_ref, wg_ref, ws_ref, bd_ref, dog_ref, dgg_ref, dos_ref, dwg_ref, dws_ref):
        @pl.when(pl.program_id(0) == 0)
        def _():
            dwg_ref[...] = jnp.zeros_like(dwg_ref)
            dws_ref[...] = jnp.zeros_like(dws_ref)

        og = og_ref[...]
        gg = gg_ref[...]
        sg = jax.nn.sigmoid(gg)
        s = gg * sg
        ds = sg * (1.0 + gg * (1.0 - sg))
        for h in range(4):
            sl = slice(128 * h, 128 * (h + 1))
            x = og[:, sl]
            r = _rsqrt_ms(x)
            n = x * r
            w = wg_ref[:, sl]
            dc = dc_ref[:, sl]
            dy = dc * s[:, sl]
            dgg_ref[:, sl] = dc * (n * w) * ds[:, sl]
            dn = dy * w
            dog_ref[:, sl] = r * (dn - n * jnp.mean(dn * n, axis=-1, keepdims=True))
            dwg_ref[:, sl] += _colsum8(dy * n)
        osv = os_ref[...]
        bd = bd_ref[...]
        r = lax.rsqrt(_dot_exact(osv * osv, bd) * (1.0 / 64.0) + EPS)
        n = osv * r
        dc = dc_ref[:, 512:1024]
        dn = dc * ws_ref[...]
        dos_ref[...] = r * (dn - n * (_dot_exact(dn * n, bd) * (1.0 / 64.0)))
        dws_ref[...] += _colsum8(dc * n)

    return pl.pallas_call(
        body, name="mix_cat_bwd", grid=(T // TB,),
        in_specs=[_row(1024), _row(512), _row(512, C_GG // 512), _row(512), _const((1, 512)), _const((1, 512)),
                  _const((512, 512))],
        out_specs=[_row(512), _row(512), _row(512), _const((8, 512)), _const((8, 512))],
        out_shape=[SDS((T, 512), F32), SDS((T, 512), F32), SDS((T, 512), F32), SDS((8, 512), F32),
                   SDS((8, 512), F32)],
        compiler_params=_params())(dcat, o_g, proj, o_s, wg512, ws512, bd64)


def _swiglu_act(gu):
    T = gu.shape[0]

    def body(g_ref, u_ref, a_ref):
        g = g_ref[...]
        a_ref[...] = (g * jax.nn.sigmoid(g) * u_ref[...]).astype(BF16)

    return pl.pallas_call(
        body, name="swiglu_act", grid=(T // TB,), in_specs=[_row(D_FF, 0), _row(D_FF, 1)], out_specs=_row(D_FF),
        out_shape=SDS((T, D_FF), BF16), compiler_params=_params())(gu, gu)


def _swiglu_bwd(da, gu):
    T = gu.shape[0]

    def body(da_ref, g_ref, u_ref, dgu_ref):
        g = g_ref[...]
        sg = jax.nn.sigmoid(g)
        dav = da_ref[...]
        dgu_ref[:, :D_FF] = (dav * u_ref[...] * (sg * (1.0 + g * (1.0 - sg)))).astype(BF16)
        dgu_ref[:, D_FF:] = (dav * (g * sg)).astype(BF16)

    return pl.pallas_call(
        body, name="swiglu_bwd", grid=(T // TB,), in_specs=[_row(D_FF), _row(D_FF, 0), _row(D_FF, 1)],
        out_specs=_row(2 * D_FF), out_shape=SDS((T, 2 * D_FF), BF16), compiler_params=_params())(da, gu, gu)


def _loss_head(y, tgt):
    T = y.shape[0]

    def body(y_ref, t_ref, dy_ref, l_ref):
        @pl.when(pl.program_id(0) == 0)
        def _():
            l_ref[...] = jnp.zeros_like(l_ref)

        e = y_ref[...] - t_ref[...]
        dy_ref[...] = e * (1.0 / D)
        l_ref[...] += _colsum8(e * e) * (0.5 / D)

    return pl.pallas_call(
        body, name="loss_head", grid=(T // TB,), in_specs=[_row(D), _row(D)],
        out_specs=[_row(D), _const((8, D))], out_shape=[SDS((T, D), F32), SDS((8, D), F32)],
        compiler_params=_params())(y, tgt)


def _gla_consts():
    c = GLA_C
    L = np.tril(np.ones((c, c), np.float32))
    blocks = [L, L[(np.arange(c) // 16) * 16]]
    blocks += [np.repeat(L[16 * i:16 * i + 1], c, axis=0) for i in range(4)]
    blocks.append(np.repeat(L[c - 1:c], c, axis=0))
    return jnp.asarray(np.concatenate(blocks, axis=0))


@jax.custom_vjp
def _gla_lin(cm, g):
    y = jnp.dot(cm, g, precision=HIGHEST, preferred_element_type=F32)
    return tuple(y[GLA_C * n:GLA_C * (n + 1)] for n in range(7))


def _gla_lin_fwd(cm, g):
    return _gla_lin(cm, g), cm


def _gla_lin_bwd(cm, cts):
    ct = jnp.concatenate(cts, axis=0)
    dg = lax.dot_general(cm, ct, (((0,), (0,)), ((), ())), precision=HIGHEST, preferred_element_type=F32)
    return None, dg


_gla_lin.defvjp(_gla_lin_fwd, _gla_lin_bwd)


def _gla_chunk_pair(cm, q, k, g, v0, v1, st):
    c = GLA_C
    lane = lax.broadcasted_iota(jnp.int32, (c, 128), 1)
    m0 = (lane < 64).astype(F32)
    m1 = 1.0 - m0
    row = lax.broadcasted_iota(jnp.int32, (c, 128), 0)
    ri = lax.broadcasted_iota(jnp.int32, (c, c), 0)
    ci = lax.broadcasted_iota(jnp.int32, (c, c), 1)
    b, r, r0, r1, r2, r3, bl = _gla_lin(cm, g)
    qs = q * jnp.exp(b - r)
    a0 = jnp.zeros((c, c), F32)
    a1 = jnp.zeros((c, c), F32)
    for blk, rb in enumerate((r0, r1, r2, r3)):
        keep = row < 16 * (blk + 1)
        ksb = jnp.where(keep, k * jnp.exp(jnp.where(keep, rb - b, 0.0)), 0.0)
        qb = jnp.where((row >= 16 * blk) & keep, qs, 0.0)
        a0 = a0 + _dot_nt_f32(qb * m0, ksb)
        a1 = a1 + _dot_nt_f32(qb * m1, ksb)
    causal = ci <= ri
    a0 = jnp.where(causal, a0, 0.0)
    a1 = jnp.where(causal, a1, 0.0)
    qe = q * jnp.exp(b)
    o0 = _dot_nt(qe * m0, st) + _dot(a0, v0)
    o1 = _dot_nt(qe * m1, st) + _dot(a1, v1)
    kd = k * jnp.exp(bl - b)
    m0s = jnp.concatenate([m0, m0], axis=0)
    decay = jnp.exp(jnp.concatenate([bl, bl], axis=0))
    st_new = st * decay + m0s * _dot_tn(v0, kd) + (1.0 - m0s) * _dot_tn(v1, kd)
    return o0, o1, st_new


GLA_TB = 512


def _gla_fwd(proj, gk, cm):
    T = proj.shape[0]
    nc = GLA_TB // GLA_C

    def body(q_ref, k_ref, v_ref, g_ref, cm_ref, o_ref, st_ref, st_scr):
        @pl.when(pl.program_id(0) == 0)
        def _():
            st_scr[...] = jnp.zeros_like(st_scr)

        cmv = cm_ref[...]

        def chunk(ci, carry):
            rs = pl.ds(pl.multiple_of(ci * GLA_C, GLA_C), GLA_C)
            for p in range(2):
                ls = slice(128 * p, 128 * (p + 1))
                st = st_scr[p]
                st_ref[ci, p] = st
                o0, o1, st_new = _gla_chunk_pair(
                    cmv, q_ref[rs, ls] * 0.125, k_ref[rs, ls], g_ref[rs, ls],
                    v_ref[rs, 256 * p:256 * p + 128], v_ref[rs, 256 * p + 128:256 * p + 256], st)
                o_ref[rs, 256 * p:256 * p + 128] = o0
                o_ref[rs, 256 * p + 128:256 * p + 256] = o1
                st_scr[p] = st_new
            return carry

        lax.fori_loop(0, nc, chunk, 0)

    return pl.pallas_call(
        body, name="gla_fwd", grid=(T // GLA_TB,),
        in_specs=[pl.BlockSpec((GLA_TB, 256), lambda i: (i, C_QG // 256)),
                  pl.BlockSpec((GLA_TB, 256), lambda i: (i, C_KG // 256)),
                  pl.BlockSpec((GLA_TB, 512), lambda i: (i, C_VG // 512)),
                  pl.BlockSpec((GLA_TB, 256), lambda i: (i, 0)),
                  pl.BlockSpec((7 * GLA_C, GLA_C), lambda i: (0, 0))],
        out_specs=[pl.BlockSpec((GLA_TB, 512), lambda i: (i, 0)),
                   pl.BlockSpec((nc, 2, 128, 128), lambda i: (i, 0, 0, 0))],
        out_shape=[SDS((T, 512), F32), SDS((T // GLA_C, 2, 128, 128), F32)],
        scratch_shapes=[pltpu.VMEM((2, 128, 128), F32)],
        compiler_params=_params(dimension_semantics=("arbitrary",)))(proj, proj, proj, gk, cm)


def _gla_bwd(proj, gk, cm, states, do):
    T = proj.shape[0]
    nb = T // GLA_TB
    nc = GLA_TB // GLA_C

    def body(q_ref, k_ref, v_ref, g_ref, cm_ref, st_ref, do_ref, dq_ref, dk_ref, dv_ref, dg_ref, dst_scr):
        @pl.when(pl.program_id(0) == 0)
        def _():
            dst_scr[...] = jnp.zeros_like(dst_scr)

        cmv = cm_ref[...]

        def chunk(t, carry):
            ci = nc - 1 - t
            rs = pl.ds(pl.multiple_of(ci * GLA_C, GLA_C), GLA_C)
            for p in range(2):
                ls = slice(128 * p, 128 * (p + 1))
                _, vjp = jax.vjp(
                    functools.partial(_gla_chunk_pair, cmv),
                    q_ref[rs, ls] * 0.125, k_ref[rs, ls], g_ref[rs, ls],
                    v_ref[rs, 256 * p:256 * p + 128], v_ref[rs, 256 * p + 128:256 * p + 256], st_ref[ci, p])
                dq, dk, dg, dv0, dv1, dst = vjp((do_ref[rs, 256 * p:256 * p + 128],
                                                 do_ref[rs, 256 * p + 128:256 * p + 256], dst_scr[p]))
                dq_ref[rs, ls] = dq
                dk_ref[rs, ls] = dk
                dg_ref[rs, ls] = dg
                dv_ref[rs, 256 * p:256 * p + 128] = dv0
                dv_ref[rs, 256 * p + 128:256 * p + 256] = dv1
                dst_scr[p] = dst
            return carry

        lax.fori_loop(0, nc, chunk, 0)

    rev = lambda i: nb - 1 - i
    return pl.pallas_call(
        body, name="gla_bwd", grid=(nb,),
        in_specs=[pl.BlockSpec((GLA_TB, 256), lambda i: (rev(i), C_QG // 256)),
                  pl.BlockSpec((GLA_TB, 256), lambda i: (rev(i), C_KG // 256)),
                  pl.BlockSpec((GLA_TB, 512), lambda i: (rev(i), C_VG // 512)),
                  pl.BlockSpec((GLA_TB, 256), lambda i: (rev(i), 0)),
                  pl.BlockSpec((7 * GLA_C, GLA_C), lambda i: (0, 0)),
                  pl.BlockSpec((nc, 2, 128, 128), lambda i: (rev(i), 0, 0, 0)),
                  pl.BlockSpec((GLA_TB, 512), lambda i: (rev(i), 0))],
        out_specs=[pl.BlockSpec((GLA_TB, 256), lambda i: (rev(i), 0)),
                   pl.BlockSpec((GLA_TB, 256), lambda i: (rev(i), 0)),
                   pl.BlockSpec((GLA_TB, 512), lambda i: (rev(i), 0)),
                   pl.BlockSpec((GLA_TB, 256), lambda i: (rev(i), 0))],
        out_shape=[SDS((T, 256), F32), SDS((T, 256), F32), SDS((T, 512), F32), SDS((T, 256), F32)],
        scratch_shapes=[pltpu.VMEM((2, 128, 128), F32)],
        compiler_params=_params(dimension_semantics=("arbitrary",)))(proj, proj, proj, gk, cm, states, do)


def _sb_tri():
    i = np.arange(SBQ)
    return jnp.asarray((i[:, None] > i[None, :]).astype(np.float32), dtype=BF16)


def _sb_block_fwd(qh, kb, tri, carry, strict):
    z = _dot_nt(qh, kb)
    sp = _softplus(z)
    l1 = -sp
    if strict is not None:
        l1 = jnp.where(strict, l1, 0.0)
    log_a = (z - sp) + _dot_hilo(l1, tri) + carry
    a = jnp.exp(log_a)
    if strict is not None:
        a = jnp.where(strict, a, 0.0)
    return z - sp, l1, a


def _sb_fwd(qs, ks, vs, tri):
    T = qs.shape[0]
    nq = T // SBQ

    def body(q_ref, k_ref, v_ref, tri_ref, o_ref, c_ref):
        i = pl.program_id(1)
        lane = lax.broadcasted_iota(jnp.int32, (1, 128), 1)
        clane = lax.broadcasted_iota(jnp.int32, (SBQ, 128), 1)
        strict = (lax.broadcasted_iota(jnp.int32, (SBQ, SBQ), 1) < lax.broadcasted_iota(jnp.int32, (SBQ, SBQ), 0))
        tri_v = tri_ref[...]
        qv = q_ref[...]
        out = jnp.zeros((SBQ, 128), F32)
        for hh in range(2):
            hm = (lane // 64) == hh
            qh = jnp.where(hm, qv, jnp.zeros_like(qv))

            def block(j, carry, acc, masked):
                rs = pl.ds(pl.multiple_of(j * SBQ, SBQ), SBQ)
                _, l1, a = _sb_block_fwd(qh, k_ref[rs, :], tri_v, carry, strict if masked else None)
                acc = acc + _dot(a.astype(BF16), v_ref[rs, :])
                return carry + jnp.sum(l1, axis=1, keepdims=True), acc

            carry, acc = block(i, jnp.zeros((SBQ, 1), F32), jnp.zeros((SBQ, 128), F32), True)

            def step(jj, state):
                carry, acc, ct = state
                j = i - jj
                ct = jnp.where(clane == j, carry, ct)
                carry, acc = block(j, carry, acc, False)
                return carry, acc, ct

            carry, acc, ct = lax.fori_loop(1, i + 1, step, (carry, acc, jnp.zeros((SBQ, 128), F32)))
            out = out + jnp.where(hm, acc, 0.0)
            c_ref[0, :, 128 * hh:128 * (hh + 1)] = ct
        o_ref[...] = out

    return pl.pallas_call(
        body, name="sb_fwd", grid=(4, nq),
        in_specs=[pl.BlockSpec((SBQ, 128), lambda h, i: (i, h)),
                  pl.BlockSpec((T, 128), lambda h, i: (0, h)),
                  pl.BlockSpec((T, 128), lambda h, i: (0, h)),
                  pl.BlockSpec((SBQ, SBQ), lambda h, i: (0, 0))],
        out_specs=[pl.BlockSpec((SBQ, 128), lambda h, i: (i, h)),
                   pl.BlockSpec((1, SBQ, 256), lambda h, i: (h, i, 0))],
        out_shape=[SDS((T, 512), F32), SDS((4, T, 256), F32)],
        compiler_params=_params(dimension_semantics=("parallel", "arbitrary")))(qs, ks, vs, tri)


def _sb_bwd(qs, ks, vs, do, carries, tri, tri_t):
    T = qs.shape[0]
    nq = T // SBQ

    def body(q_ref, k_ref, v_ref, do_ref, c_ref, tri_ref, trit_ref, dq_ref, dk_ref, dv_ref):
        i = pl.program_id(1)

        @pl.when(i == 0)
        def _():
            dk_ref[...] = jnp.zeros_like(dk_ref)
            dv_ref[...] = jnp.zeros_like(dv_ref)

        lane = lax.broadcasted_iota(jnp.int32, (1, 128), 1)
        clane = lax.broadcasted_iota(jnp.int32, (SBQ, 128), 1)
        strict = (lax.broadcasted_iota(jnp.int32, (SBQ, SBQ), 1) < lax.broadcasted_iota(jnp.int32, (SBQ, SBQ), 0))
        tri_v = tri_ref[...]
        trit_v = trit_ref[...]
        qv = q_ref[...]
        dov = do_ref[...].astype(BF16)
        dq_out = jnp.zeros((SBQ, 128), F32)
        for hh in range(2):
            hm = (lane // 64) == hh
            qh = jnp.where(hm, qv, jnp.zeros_like(qv))
            doh = jnp.where(hm, dov, jnp.zeros_like(dov))
            ct = c_ref[0, :, 128 * hh:128 * (hh + 1)]

            def block(j, pcarry, dq, masked):
                rs = pl.ds(pl.multiple_of(j * SBQ, SBQ), SBQ)
                kb = k_ref[rs, :]
                vb = v_ref[rs, :]
                carry = jnp.sum(jnp.where(clane == j, ct, 0.0), axis=1, keepdims=True)
                lb, _, a = _sb_block_fwd(qh, kb, tri_v, carry, strict if masked else None)
                g = a * _dot_nt(doh, vb)
                p = _dot_hilo(g, trit_v) + pcarry
                dz = g - (g + p) * jnp.exp(lb)
                if masked:
                    dz = jnp.where(strict, dz, 0.0)
                dzb = dz.astype(BF16)
                dk_ref[rs, :] += _dot_tn(dzb, qh)
                dv_ref[rs, :] += _dot_tn(a.astype(BF16), doh)
                return pcarry + jnp.sum(g, axis=1, keepdims=True), dq + _dot(dzb, kb)

            def step(j, state):
                return block(j, state[0], state[1], False)

            pcarry, dq = lax.fori_loop(0, i, step, (jnp.zeros((SBQ, 1), F32), jnp.zeros((SBQ, 128), F32)))
            _, dq = block(i, pcarry, dq, True)
            dq_out = dq_out + jnp.where(hm, dq, 0.0)
        dq_ref[...] = dq_out

    return pl.pallas_call(
        body, name="sb_bwd", grid=(4, nq),
        in_specs=[pl.BlockSpec((SBQ, 128), lambda h, i: (i, h)),
                  pl.BlockSpec((T, 128), lambda h, i: (0, h)),
                  pl.BlockSpec((T, 128), lambda h, i: (0, h)),
                  pl.BlockSpec((SBQ, 128), lambda h, i: (i, h)),
                  pl.BlockSpec((1, SBQ, 256), lambda h, i: (h, i, 0)),
                  pl.BlockSpec((SBQ, SBQ), lambda h, i: (0, 0)),
                  pl.BlockSpec((SBQ, SBQ), lambda h, i: (0, 0))],
        out_specs=[pl.BlockSpec((SBQ, 128), lambda h, i: (i, h)),
                   pl.BlockSpec((T, 128), lambda h, i: (0, h)),
                   pl.BlockSpec((T, 128), lambda h, i: (0, h))],
        out_shape=[SDS((T, 512), F32), SDS((T, 512), F32), SDS((T, 512), F32)],
        compiler_params=_params(dimension_semantics=("parallel", "arbitrary")))(qs, ks, vs, do, carries, tri, tri_t)


def _mem_fwd(mem, mem_norm_w, w_mkv, mk_norm_w):
    M = mem.shape[0]

    def body(mem_ref, wn_ref, w_ref, wk_ref, mn_ref, kraw_ref, k_ref, v_ref):
        mv = mem_ref[...]
        mn = (mv * _rsqrt_ms(mv) * wn_ref[...]).astype(BF16)
        mn_ref[...] = mn
        kv = _dot(mn, w_ref[...])
        kraw_ref[...] = kv[:, :D]
        v_ref[...] = kv[:, D:].astype(BF16)
        for h in range(MEM_HEADS):
            sl = slice(MEM_HD * h, MEM_HD * (h + 1))
            x = kv[:, sl]
            k_ref[:, sl] = (x * _rsqrt_ms(x) * wk_ref[...]).astype(BF16)

    vm = pl.BlockSpec(memory_space=pltpu.VMEM)
    return pl.pallas_call(
        body, name="mem_fwd", in_specs=[vm] * 4, out_specs=[vm] * 4,
        out_shape=[SDS((M, D), BF16), SDS((M, D), F32), SDS((M, D), BF16), SDS((M, D), BF16)],
        compiler_params=_params())(mem, mem_norm_w, w_mkv, mk_norm_w)


def _xattn_fwd(qraw, k, v, wq):
    T = qraw.shape[0]
    M = k.shape[0]

    def body(q_ref, k_ref, v_ref, wq_ref, o_ref):
        for h in range(MEM_HEADS):
            sl = slice(MEM_HD * h, MEM_HD * (h + 1))
            x = q_ref[:, sl]
            q = (x * _rsqrt_ms(x) * wq_ref[...]).astype(BF16)
            s = _dot_nt(q, k_ref[:, sl]) * (1.0 / math.sqrt(MEM_HD))
            s = s - jnp.max(s, axis=-1, keepdims=True)
            e = jnp.exp(s)
            p = e / jnp.sum(e, axis=-1, keepdims=True)
            o_ref[:, sl] = _dot(p.astype(BF16), v_ref[:, sl]).astype(BF16)

    return pl.pallas_call(
        body, name="xattn_fwd", grid=(T // TB,),
        in_specs=[_row(D), _const((M, D)), _const((M, D)), _const((1, MEM_HD))],
        out_specs=_row(D), out_shape=SDS((T, D), BF16), compiler_params=_params())(qraw, k, v, wq)


def _xattn_bwd(qraw, k, v, wq, do):
    T = qraw.shape[0]
    M = k.shape[0]

    def body(q_ref, k_ref, v_ref, wq_ref, do_ref, dq_ref, dk_ref, dv_ref, dw_ref):
        @pl.when(pl.program_id(0) == 0)
        def _():
            dk_ref[...] = jnp.zeros_like(dk_ref)
            dv_ref[...] = jnp.zeros_like(dv_ref)
            dw_ref[...] = jnp.zeros_like(dw_ref)

        w = wq_ref[...]
        for h in range(MEM_HEADS):
            sl = slice(MEM_HD * h, MEM_HD * (h + 1))
            x = q_ref[:, sl]
            r = _rsqrt_ms(x)
            n = x * r
            q = (n * w).astype(BF16)
            kb = k_ref[:, sl]
            s = _dot_nt(q, kb) * (1.0 / math.sqrt(MEM_HD))
            s = s - jnp.max(s, axis=-1, keepdims=True)
            e = jnp.exp(s)
            p = e / jnp.sum(e, axis=-1, keepdims=True)
            dob = do_ref[:, sl].astype(BF16)
            dp = _dot_nt(dob, v_ref[:, sl])
            ds = (p * (dp - jnp.sum(dp * p, axis=-1, keepdims=True)) * (1.0 / math.sqrt(MEM_HD))).astype(BF16)
            dv_ref[:, sl] += _dot_tn(p.astype(BF16), dob)
            dk_ref[:, sl] += _dot_tn(ds, q)
            dqn = _dot(ds, kb)
            dn = dqn * w
            dq_ref[:, sl] = r * (dn - n * jnp.mean(dn * n, axis=-1, keepdims=True))
            dw_ref[...] += _colsum8(dqn * n)

    return pl.pallas_call(
        body, name="xattn_bwd", grid=(T // TB,),
        in_specs=[_row(D), _const((M, D)), _const((M, D)), _const((1, MEM_HD)), _row(D)],
        out_specs=[_row(D), _const((M, D)), _const((M, D)), _const((8, MEM_HD))],
        out_shape=[SDS((T, D), F32), SDS((M, D), F32), SDS((M, D), F32), SDS((8, MEM_HD), F32)],
        compiler_params=_params())(qraw, k, v, wq, do)


def _mem_bwd(mem, mem_norm_w, w_mkv, mk_norm_w, mem_n, k_raw, dk, dv):
    M = mem.shape[0]

    def body(mem_ref, wn_ref, w_ref, wk_ref, mn_ref, kraw_ref, dk_ref, dv_ref, dw_ref, dwn_ref, dwk_ref, dkv_scr):
        wk = wk_ref[...]
        dwk = jnp.zeros((8, MEM_HD), F32)
        for h in range(MEM_HEADS):
            sl = slice(MEM_HD * h, MEM_HD * (h + 1))
            x = kraw_ref[:, sl]
            r = _rsqrt_ms(x)
            n = x * r
            dkh = dk_ref[:, sl]
            dn = dkh * wk
            dkv_scr[:, sl] = (r * (dn - n * jnp.mean(dn * n, axis=-1, keepdims=True))).astype(BF16)
            dwk = dwk + _colsum8(dkh * n)
        dwk_ref[...] = dwk
        dkv_scr[:, D:] = dv_ref[...].astype(BF16)
        dkv = dkv_scr[...]
        dw_ref[...] = _dot_tn(mn_ref[...], dkv)
        dmn = _dot_nt(dkv, w_ref[...])
        mv = mem_ref[...]
        dwn_ref[...] = _colsum8(dmn * (mv * _rsqrt_ms(mv)))

    vm = pl.BlockSpec(memory_space=pltpu.VMEM)
    return pl.pallas_call(
        body, name="mem_bwd", in_specs=[vm] * 8, out_specs=[vm] * 3,
        out_shape=[SDS((D, 2 * D), F32), SDS((8, D), F32), SDS((8, MEM_HD), F32)],
        scratch_shapes=[pltpu.VMEM((M, 2 * D), BF16)],
        compiler_params=_params())(mem, mem_norm_w, w_mkv, mk_norm_w, mem_n, k_raw, dk, dv)


def _local_step(x, mem, tgt, w):
    wgk = jnp.zeros((128, 256), F32).at[:16].set(w["w_gk_up"].astype(F32))
    wg512 = jnp.tile(w["gla_norm_w"], (1, 4))
    ws512 = jnp.tile(w["sb_norm_w"], (1, 8))
    bd64 = _group_ones(512, 64)
    cm = _gla_consts()
    tri = _sb_tri()
    tri_t = tri.T

    h1 = _norm_fwd(x, w["mix_norm_w"], "norm1_fwd")
    proj = _matmul(h1, w["w_in"], mode="nn", tm=TB, tn=DIN_P, name="mm_proj")
    gk, qs, ks, vs = _proj_split(proj, wgk, w["b_gk"])
    o_g, states = _gla_fwd(proj, gk, cm)
    o_s, carries = _sb_fwd(qs, ks, vs, tri)
    cat = _mix_cat(o_g, proj, o_s, wg512, ws512, bd64)
    x1 = _matmul(cat, w["w_out"], mode="nn", tm=TB, tn=D, res=x, name="mm_out")
    h2 = _norm_fwd(x1, w["xattn_norm_w"], "norm2_fwd")
    qraw = _matmul(h2, w["w_mq"], mode="nn", tm=TB, tn=D, name="mm_mq")
    mem_n, k_raw, k_n, v_m = _mem_fwd(mem, w["mem_norm_w"], w["w_mkv"], w["mk_norm_w"])
    om = _xattn_fwd(qraw, k_n, v_m, w["mq_norm_w"])
    x2 = _matmul(om, w["w_mo"], mode="nn", tm=TB, tn=D, res=x1, name="mm_mo")
    h3 = _norm_fwd(x2, w["ffn_norm_w"], "norm3_fwd")
    gu = _matmul(h3, w["w_gate_up"], mode="nn", tm=TB, tn=D_FF, name="mm_gate_up")
    act = _swiglu_act(gu)
    x3 = _matmul(act, w["w_down"], mode="nn", tm=TB, tn=D, res=x2, name="mm_down")
    dx3, loss_rows = _loss_head(x3, tgt)

    g = {}
    g["w_down"] = _matmul(act, dx3, mode="tn", tm=1408, tn=D, tk=TB, name="mm_dw_down")
    da = _matmul(dx3, w["w_down"], mode="nt", tm=TB, tn=1408, name="mm_da")
    dgu = _swiglu_bwd(da, gu)
    g["w_gate_up"] = _matmul(h3, dgu, mode="tn", tm=D, tn=1408, tk=TB, name="mm_dw_gate_up")
    dh3 = _matmul(dgu, w["w_gate_up"], mode="nt", tm=TB, tn=D, name="mm_dh3")
    dx2, g["ffn_norm_w"] = _norm_bwd(dh3, x2, w["ffn_norm_w"], dx3, "norm3_bwd")
    g["w_mo"] = _matmul(om, dx2, mode="tn", tm=D, tn=D, tk=TB, name="mm_dw_mo")
    dom = _matmul(dx2, w["w_mo"], mode="nt", tm=TB, tn=D, name="mm_dom")
    dqraw, dk_n, dv_m, g["mq_norm_w"] = _xattn_bwd(qraw, k_n, v_m, w["mq_norm_w"], dom)
    g["w_mkv"], g["mem_norm_w"], g["mk_norm_w"] = _mem_bwd(
        mem, w["mem_norm_w"], w["w_mkv"], w["mk_norm_w"], mem_n, k_raw, dk_n, dv_m)
    g["w_mq"] = _matmul(h2, dqraw, mode="tn", tm=D, tn=D, tk=TB, name="mm_dw_mq")
    dh2 = _matmul(dqraw, w["w_mq"], mode="nt", tm=TB, tn=D, name="mm_dh2")
    dx1, g["xattn_norm_w"] = _norm_bwd(dh2, x1, w["xattn_norm_w"], dx2, "norm2_bwd")
    g["w_out"] = _matmul(cat, dx1, mode="tn", tm=D, tn=D, tk=TB, name="mm_dw_out")
    dcat = _matmul(dx1, w["w_out"], mode="nt", tm=TB, tn=D, name="mm_dcat")
    do_g, dg_g, do_s, dwg, dws = _mix_cat_bwd(dcat, o_g, proj, o_s, wg512, ws512, bd64)
    dq_s, dk_s, dv_s = _sb_bwd(qs, ks, vs, do_s, carries, tri, tri_t)
    dq_g, dk_g, dv_g, dgk = _gla_bwd(proj, gk, cm, states, do_g)
    dproj, dwgk, g["b_gk"] = _dproj_assemble(proj, dq_g, dk_g, dv_g, dg_g, dq_s, dk_s, dv_s, dgk, wgk, w["b_gk"])
    g["w_in"] = _matmul(h1, dproj, mode="tn", tm=D, tn=640, tk=TB, name="mm_dw_in")
    dh1 = _matmul(dproj, w["w_in"], mode="nt", tm=TB, tn=D, name="mm_dh1")
    grad_x, g["mix_norm_w"] = _norm_bwd(dh1, x, w["mix_norm_w"], dx1, "norm1_bwd")

    g["w_gk_up"] = dwgk[:16]
    g["gla_norm_w"] = dwg.reshape(8, 4, 128).sum(axis=1)
    g["sb_norm_w"] = dws.reshape(8, 8, 64).sum(axis=1)
    for n in SMALL:
        g[n] = jnp.sum(g[n], axis=0, keepdims=True)
    return jnp.sum(loss_rows), grad_x, g


def _mesh_pos():
    return lax.axis_index("x"), lax.axis_index("y"), lax.axis_index("c")


def _other_chips(x, y):
    return [(1 - x, y), (x, 1 - y), (1 - x, 1 - y)]


def _half(c):
    return pl.ds(pl.multiple_of(c * PACK_HALF, 32), PACK_HALF)


def _ag_weights(wp):
    def body(w_ref, out_ref, send_sems, recv_sems, local_sem):
        x, y, c = _mesh_pos()
        me = 2 * x + y
        sibling = (x, y, 1 - c)
        chips = _other_chips(x, y)
        mine, theirs = _half(c), _half(1 - c)
        local = pltpu.make_async_copy(w_ref, out_ref.at[me], local_sem)
        local.start()

        def copy(k, src, dst, to):
            return pltpu.make_async_remote_copy(src_ref=src, dst_ref=dst, send_sem=send_sems.at[k],
                                                recv_sem=recv_sems.at[k], device_id=to, device_id_type=MESH)

        first = [copy(k, w_ref.at[mine], out_ref.at[me, mine], (cx, cy, c)) for k, (cx, cy) in enumerate(chips)]
        for cp in first:
            cp.start()
        passed = []
        for k, (cx, cy) in enumerate(chips):
            slot = out_ref.at[2 * cx + cy, mine]
            copy(k, slot, slot, (cx, cy, c)).wait_recv()
            fwd = copy(3 + k, slot, slot, sibling)
            fwd.start()
            passed.append(fwd)
        for k, (cx, cy) in enumerate(chips):
            slot = out_ref.at[2 * cx + cy, theirs]
            copy(3 + k, slot, slot, sibling).wait_recv()
        for cp in first + passed:
            cp.wait_send()
        local.wait()

    hbm = pl.BlockSpec(memory_space=pl.ANY)
    return pl.pallas_call(
        body, name="ag_weights", in_specs=[hbm], out_specs=hbm,
        out_shape=SDS((4, PACK_ROWS, 1024), BF16),
        scratch_shapes=[pltpu.SemaphoreType.DMA((6,)), pltpu.SemaphoreType.DMA((6,)), pltpu.SemaphoreType.DMA],
        compiler_params=pltpu.CompilerParams(has_side_effects=True))(wp)


def _rs_swap_halves(gp):
    def body(g_ref, out_ref, send_sem, recv_sem):
        x, y, c = _mesh_pos()
        cp = pltpu.make_async_remote_copy(
            src_ref=g_ref.at[:, _half(1 - c)], dst_ref=out_ref, send_sem=send_sem, recv_sem=recv_sem,
            device_id=(x, y, 1 - c), device_id_type=MESH)
        cp.start()
        cp.wait()

    hbm = pl.BlockSpec(memory_space=pl.ANY)
    return pl.pallas_call(
        body, name="rs_swap_halves", in_specs=[hbm], out_specs=hbm,
        out_shape=SDS((4, PACK_HALF, 1024), F32),
        scratch_shapes=[pltpu.SemaphoreType.DMA, pltpu.SemaphoreType.DMA],
        compiler_params=pltpu.CompilerParams(has_side_effects=True))(gp)


RS_TR = 704


def _rs_add_halves(gp, other, c_arr):
    nt = PACK_HALF // RS_TR

    def body(c_ref, a_ref, b_ref, o_ref):
        o_ref[...] = a_ref[...] + b_ref[...]

    return pl.pallas_call(
        body, name="rs_add_halves",
        grid_spec=pltpu.PrefetchScalarGridSpec(
            num_scalar_prefetch=1, grid=(4, nt),
            in_specs=[pl.BlockSpec((1, RS_TR, 1024), lambda s, t, c: (s, c[0] * nt + t, 0)),
                      pl.BlockSpec((1, RS_TR, 1024), lambda s, t, c: (s, t, 0))],
            out_specs=pl.BlockSpec((1, RS_TR, 1024), lambda s, t, c: (s, t, 0))),
        out_shape=SDS((4, PACK_HALF, 1024), F32), compiler_params=_params())(c_arr, gp, other)


def _rs_exchange(part):
    def body(p_ref, out_ref, send_sems, recv_sems, local_sem):
        x, y, c = _mesh_pos()
        me = 2 * x + y
        chips = _other_chips(x, y)
        local = pltpu.make_async_copy(p_ref.at[me], out_ref.at[me], local_sem)
        local.start()
        sends = []
        for k, (cx, cy) in enumerate(chips):
            cp = pltpu.make_async_remote_copy(
                src_ref=p_ref.at[2 * cx + cy], dst_ref=out_ref.at[me], send_sem=send_sems.at[k],
                recv_sem=recv_sems.at[k], device_id=(cx, cy, c), device_id_type=MESH)
            cp.start()
            sends.append(cp)
        for k, (cx, cy) in enumerate(chips):
            slot = out_ref.at[2 * cx + cy]
            pltpu.make_async_remote_copy(
                src_ref=slot, dst_ref=slot, send_sem=send_sems.at[k], recv_sem=recv_sems.at[k],
                device_id=(cx, cy, c), device_id_type=MESH).wait_recv()
        for cp in sends:
            cp.wait_send()
        local.wait()

    hbm = pl.BlockSpec(memory_space=pl.ANY)
    return pl.pallas_call(
        body, name="rs_exchange", in_specs=[hbm], out_specs=hbm,
        out_shape=SDS((4, PACK_HALF, 1024), F32),
        scratch_shapes=[pltpu.SemaphoreType.DMA((3,)), pltpu.SemaphoreType.DMA((3,)), pltpu.SemaphoreType.DMA],
        compiler_params=pltpu.CompilerParams(has_side_effects=True))(part)


def _rs_add_chips(parts):
    def body(p_ref, o_ref):
        o_ref[...] = ((p_ref[0] + p_ref[1]) + p_ref[2]) + p_ref[3]

    return pl.pallas_call(
        body, name="rs_add_chips", grid=(PACK_HALF // RS_TR,),
        in_specs=[pl.BlockSpec((4, RS_TR, 1024), lambda t: (0, t, 0))],
        out_specs=pl.BlockSpec((RS_TR, 1024), lambda t: (t, 0)),
        out_shape=SDS((PACK_HALF, 1024), F32), compiler_params=_params())(parts)


def _rs_share(half_sum):
    def body(h_ref, out_ref, send_sem, recv_sem, local_sem):
        x, y, c = _mesh_pos()
        local = pltpu.make_async_copy(h_ref, out_ref.at[_half(c)], local_sem)
        local.start()
        cp = pltpu.make_async_remote_copy(
            src_ref=h_ref, dst_ref=out_ref.at[_half(c)], send_sem=send_sem, recv_sem=recv_sem,
            device_id=(x, y, 1 - c), device_id_type=MESH)
        cp.start()
        theirs = out_ref.at[_half(1 - c)]
        pltpu.make_async_remote_copy(
            src_ref=theirs, dst_ref=theirs, send_sem=send_sem, recv_sem=recv_sem,
            device_id=(x, y, 1 - c), device_id_type=MESH).wait_recv()
        cp.wait_send()
        local.wait()

    hbm = pl.BlockSpec(memory_space=pl.ANY)
    return pl.pallas_call(
        body, name="rs_share", in_specs=[hbm], out_specs=hbm,
        out_shape=SDS((PACK_ROWS, 1024), F32),
        scratch_shapes=[pltpu.SemaphoreType.DMA, pltpu.SemaphoreType.DMA, pltpu.SemaphoreType.DMA],
        compiler_params=pltpu.CompilerParams(has_side_effects=True))(half_sum)


def _adamw(w, g, m, v, name):
    rows, cols = w.shape
    tr = rows
    for cand in (512, 352, 256):
        if rows > cand and rows % cand == 0:
            tr = cand
            break
    c1 = 1.0 - ADAM_B1 ** ADAM_STEP
    c2 = 1.0 - ADAM_B2 ** ADAM_STEP

    def body(w_ref, g_ref, m_ref, v_ref, d_ref, mo_ref, vo_ref):
        gv = g_ref[...]
        mn = ADAM_B1 * m_ref[...] + (1.0 - ADAM_B1) * gv
        vn = ADAM_B2 * v_ref[...] + (1.0 - ADAM_B2) * (gv * gv)
        mo_ref[...] = mn
        vo_ref[...] = vn
        d_ref[...] = -ADAM_LR * ((mn / c1) / (jnp.sqrt(vn / c2) + ADAM_EPS) + ADAM_WD * w_ref[...])

    spec = pl.BlockSpec((tr, cols), lambda i: (i, 0))
    return pl.pallas_call(
        body, name=name, grid=(rows // tr,), in_specs=[spec] * 4, out_specs=[spec] * 3,
        out_shape=[SDS((rows, cols), F32)] * 3, compiler_params=_params())(w, g, m, v)


BIG_SHARD_ROWS = {"w_in": 772, "w_gk_up": 1, "w_out": 256, "w_mq": 256, "w_mkv": 512, "w_mo": 256,
                  "w_gate_up": 1408, "w_down": 704}
SMALL_SIZES = {"mix_norm_w": 1024, "b_gk": 256, "gla_norm_w": 128, "sb_norm_w": 64, "xattn_norm_w": 1024,
               "mem_norm_w": 1024, "mq_norm_w": 256, "mk_norm_w": 256, "ffn_norm_w": 1024}
N_BIG_ROWS = sum(BIG_SHARD_ROWS.values())


def _pack_small(d):
    flat = jnp.concatenate([d[n].reshape(-1) for n in SMALL])
    return jnp.pad(flat, (0, SMALL_ROWS * 1024 - flat.shape[0])).reshape(SMALL_ROWS, 1024)


def _unpack_small(p):
    flat = p.reshape(-1)
    out, off = {}, 0
    for n in SMALL:
        out[n] = flat[off:off + SMALL_SIZES[n]].reshape(1, SMALL_SIZES[n])
        off += SMALL_SIZES[n]
    return out


def _pack_shards(d, small, dtype):
    rows = [d[n].astype(dtype).reshape(BIG_SHARD_ROWS[n], 1024) for n in BIG] + [small.astype(dtype)]
    pad = jnp.zeros((PACK_ROWS - N_BIG_ROWS - SMALL_ROWS, 1024), dtype)
    return jnp.concatenate(rows + [pad], axis=0)


def _big_offsets():
    off, out = 0, {}
    for n in BIG:
        out[n] = off
        off += BIG_SHARD_ROWS[n]
    return out


def _unpack_full(allw):
    off = _big_offsets()

    def seg(n):
        return allw[:, off[n]:off[n] + BIG_SHARD_ROWS[n]]

    def cols(n, k, c):
        return seg(n).reshape(4, k, c).transpose(1, 0, 2).reshape(k, 4 * c)

    def rows(n, c):
        return seg(n).reshape(-1, c)

    w_in = cols("w_in", 1024, 772)
    w_in = jnp.concatenate([w_in[:, :1536], w_in[:, 1552:], w_in[:, 1536:1552],
                            jnp.zeros((1024, DIN_P - DIN), w_in.dtype)], axis=1)
    return {"w_in": w_in, "w_gk_up": cols("w_gk_up", 16, 64), "w_out": rows("w_out", 1024),
            "w_mq": rows("w_mq", 1024), "w_mkv": cols("w_mkv", 1024, 512), "w_mo": rows("w_mo", 1024),
            "w_gate_up": cols("w_gate_up", 1024, 1408), "w_down": rows("w_down", 1024)}


def _pack_grads(g):
    def cols(a, k, c):
        return a.reshape(k, 4, c).transpose(1, 0, 2).reshape(4, -1, 1024)

    def rows(a):
        return a.reshape(4, -1, 1024)

    gi = g["w_in"]
    gi = jnp.concatenate([gi[:, :1536], gi[:, C_LR:C_LR + 16], gi[:, 1536:C_LR]], axis=1)
    parts = [cols(gi, 1024, 772), cols(g["w_gk_up"], 16, 64), rows(g["w_out"]), rows(g["w_mq"]),
             cols(g["w_mkv"], 1024, 512), rows(g["w_mo"]), cols(g["w_gate_up"], 1024, 1408), rows(g["w_down"])]
    small = jnp.broadcast_to(_pack_small(g)[None], (4, SMALL_ROWS, 1024))
    pad = jnp.zeros((4, PACK_ROWS - N_BIG_ROWS - SMALL_ROWS, 1024), F32)
    return jnp.concatenate(parts + [small, pad], axis=1)


def kernel(x, mem, mix_norm_w, w_in, w_gk_up, b_gk, gla_norm_w, sb_norm_w, w_out, xattn_norm_w, mem_norm_w, w_mq, w_mkv, mq_norm_w, mk_norm_w, w_mo, ffn_norm_w, w_gate_up, w_down, loss_target, m_mix_norm_w, m_w_in, m_w_gk_up, m_b_gk, m_gla_norm_w, m_sb_norm_w, m_w_out, m_xattn_norm_w, m_mem_norm_w, m_w_mq, m_w_mkv, m_mq_norm_w, m_mk_norm_w, m_w_mo, m_ffn_norm_w, m_w_gate_up, m_w_down, v_mix_norm_w, v_w_in, v_w_gk_up, v_b_gk, v_gla_norm_w, v_sb_norm_w, v_w_out, v_xattn_norm_w, v_mem_norm_w, v_w_mq, v_w_mkv, v_mq_norm_w, v_mk_norm_w, v_w_mo, v_ffn_norm_w, v_w_gate_up, v_w_down):
    args = dict(locals())
    wts = {n: args[n][0] if n in BIG else args[n] for n in WEIGHTS}
    mom = {n: args["m_" + n][0] if n in BIG else args["m_" + n] for n in WEIGHTS}
    var = {n: args["v_" + n][0] if n in BIG else args["v_" + n] for n in WEIGHTS}

    small_w = _pack_small(wts)
    allw = _ag_weights(_pack_shards(wts, small_w, BF16))
    full = _unpack_full(allw)
    full.update({n: wts[n] for n in SMALL})

    loss, grad_x, g = _local_step(x[0], mem[0], loss_target[0], full)
    loss = lax.psum(loss, ("x", "y", "c"))

    gp = _pack_grads(g)
    c_arr = lax.axis_index("c").astype(jnp.int32).reshape(1)
    part = _rs_add_halves(gp, _rs_swap_halves(gp), c_arr)
    total = _rs_share(_rs_add_chips(_rs_exchange(part)))

    off = _big_offsets()
    grads = {n: total[off[n]:off[n] + BIG_SHARD_ROWS[n]].reshape(wts[n].shape) for n in BIG}
    grads.update(_unpack_small(total[N_BIG_ROWS:N_BIG_ROWS + SMALL_ROWS]))

    delta, new_m, new_v = {}, {}, {}
    for n in BIG:
        w2 = wts[n].reshape(-1, wts[n].shape[-1])
        d_, m_, v_ = _adamw(w2, grads[n].reshape(w2.shape), mom[n].reshape(w2.shape), var[n].reshape(w2.shape),
                            "adamw_" + n)
        delta[n], new_m[n], new_v[n] = (t.reshape((1,) + wts[n].shape) for t in (d_, m_, v_))
        grads[n] = grads[n].reshape((1,) + wts[n].shape)
    ds, ms, vs_ = _adamw(small_w, total[N_BIG_ROWS:N_BIG_ROWS + SMALL_ROWS], _pack_small(mom), _pack_small(var),
                         "adamw_small")
    for dst, src in ((delta, ds), (new_m, ms), (new_v, vs_)):
        dst.update(_unpack_small(src))

    return (loss, grad_x[None], *[grads[n] for n in WEIGHTS], *[delta[n] for n in WEIGHTS],
            *[new_m[n] for n in WEIGHTS], *[new_v[n] for n in WEIGHTS])
```

```python
import functools
import math

import numpy as np
import jax
import jax.numpy as jnp
from jax import lax
from jax.experimental import pallas as pl
from jax.experimental.pallas import tpu as pltpu

F32 = jnp.float32
BF16 = jnp.bfloat16
SDS = jax.ShapeDtypeStruct
MESH = pl.DeviceIdType.MESH

D = 1024
EPS = 1e-6
D_FF = 2816
GLA_GATE_NORM = 16.0
GLA_C = 64
MEM_HEADS = 4
MEM_HD = 256
C_QG, C_KG, C_VG, C_GG, C_QS, C_KS, C_VS, C_LR = 0, 256, 512, 1024, 1536, 2048, 2560, 3072
DIN = 3088
DIN_P = 3200
TB = 512
SBQ = 256
VMEM_LIMIT = 56 * 1024 * 1024
HIGHEST = lax.Precision.HIGHEST

ADAM_LR, ADAM_B1, ADAM_B2, ADAM_EPS, ADAM_WD, ADAM_STEP = 0.001, 0.9, 0.999, 1e-08, 0.01, 10

BIG = ("w_in", "w_gk_up", "w_out", "w_mq", "w_mkv", "w_mo", "w_gate_up", "w_down")
SMALL = ("mix_norm_w", "b_gk", "gla_norm_w", "sb_norm_w", "xattn_norm_w", "mem_norm_w", "mq_norm_w",
         "mk_norm_w", "ffn_norm_w")
WEIGHTS = ("mix_norm_w", "w_in", "w_gk_up", "b_gk", "gla_norm_w", "sb_norm_w", "w_out", "xattn_norm_w",
           "mem_norm_w", "w_mq", "w_mkv", "mq_norm_w", "mk_norm_w", "w_mo", "ffn_norm_w", "w_gate_up", "w_down")
PACK_ROWS = 4224
PACK_HALF = PACK_ROWS // 2
SMALL_ROWS = 8


def _params(**kw):
    return pltpu.CompilerParams(vmem_limit_bytes=VMEM_LIMIT, **kw)


def _row(c, j=0):
    return pl.BlockSpec((TB, c), lambda i, j=j: (i, j))


def _const(shape):
    return pl.BlockSpec(shape, lambda i: (0,) * len(shape))


def _dot(a, b):
    return lax.dot_general(a, b, (((1,), (0,)), ((), ())), preferred_element_type=F32)


def _dot_nt(a, b):
    return lax.dot_general(a, b, (((1,), (1,)), ((), ())), preferred_element_type=F32)


def _dot_tn(a, b):
    return lax.dot_general(a, b, (((0,), (0,)), ((), ())), preferred_element_type=F32)


def _dot_nt_f32(a, b):
    return lax.dot_general(a, b, (((1,), (1,)), ((), ())), precision=HIGHEST, preferred_element_type=F32)


def _split3(x):
    h = x.astype(BF16)
    r = x - h.astype(F32)
    m = r.astype(BF16)
    l = (r - m.astype(F32)).astype(BF16)
    return h, m, l


def _dot_exact(x, ones_mat):
    h, m, l = _split3(x)
    return _dot(h, ones_mat) + _dot(m, ones_mat) + _dot(l, ones_mat)


def _dot_hilo(x, ones_mat):
    h = x.astype(BF16)
    l = (x - h.astype(F32)).astype(BF16)
    return _dot(h, ones_mat) + _dot(l, ones_mat)


def _softplus(z):
    return jnp.maximum(z, 0.0) + jnp.log1p(jnp.exp(-jnp.abs(z)))


def _rsqrt_ms(x):
    return lax.rsqrt(jnp.mean(x * x, axis=-1, keepdims=True) + EPS)


def _colsum8(x):
    r, c = x.shape
    return jnp.sum(x.reshape(r // 8, 8, c), axis=0)


def _matmul(a, b, *, mode, tm, tn, tk=None, res=None, out_dtype=F32, name):
    if mode == "tn":
        K, M = a.shape
    else:
        M, K = a.shape
    N = b.shape[0] if mode == "nt" else b.shape[1]
    tk = K if tk is None else tk
    assert M % tm == 0 and N % tn == 0 and K % tk == 0, (name, M, N, K, tm, tn, tk)
    nk = K // tk
    if mode == "tn":
        a_spec = pl.BlockSpec((tk, tm), lambda j, i, k: (k, i))
    else:
        a_spec = pl.BlockSpec((tm, tk), lambda j, i, k: (i, k))
    if mode == "nt":
        b_spec = pl.BlockSpec((tn, tk), lambda j, i, k: (j, k))
    else:
        b_spec = pl.BlockSpec((tk, tn), lambda j, i, k: (k, j))
    o_spec = pl.BlockSpec((tm, tn), lambda j, i, k: (i, j))
    dot = {"nn": _dot, "nt": _dot_nt, "tn": _dot_tn}[mode]
    has_res = res is not None

    def body(*refs):
        a_ref, b_ref = refs[0], refs[1]
        res_ref = refs[2] if has_res else None
        o_ref = refs[2 + has_res]
        p = dot(a_ref[...].astype(BF16), b_ref[...].astype(BF16))
        if nk == 1:
            if has_res:
                p = p + res_ref[...]
            o_ref[...] = p.astype(out_dtype)
        else:
            acc_ref = refs[3 + has_res]
            k = pl.program_id(2)

            @pl.when(k == 0)
            def _():
                acc_ref[...] = p

            @pl.when(k > 0)
            def _():
                acc_ref[...] += p

            @pl.when(k == nk - 1)
            def _():
                t = acc_ref[...]
                if has_res:
                    t = t + res_ref[...]
                o_ref[...] = t.astype(out_dtype)

    in_specs = [a_spec, b_spec] + ([o_spec] if has_res else [])
    args = (a, b) + ((res,) if has_res else ())
    return pl.pallas_call(
        body, name=name, grid=(N // tn, M // tm, nk), in_specs=in_specs, out_specs=o_spec,
        out_shape=SDS((M, N), out_dtype),
        scratch_shapes=[pltpu.VMEM((tm, tn), F32)] if nk > 1 else [],
        compiler_params=_params(dimension_semantics=("parallel", "parallel", "arbitrary")),
    )(*args)


def _norm_fwd(x, w, name):
    T, dm = x.shape

    def body(x_ref, w_ref, h_ref):
        xv = x_ref[...]
        h_ref[...] = (xv * _rsqrt_ms(xv) * w_ref[...]).astype(BF16)

    return pl.pallas_call(
        body, name=name, grid=(T // TB,), in_specs=[_row(dm), _const((1, dm))], out_specs=_row(dm),
        out_shape=SDS((T, dm), BF16), compiler_params=_params())(x, w)


def _norm_bwd(dy, x, w, dres, name):
    T, dm = x.shape

    def body(dy_ref, x_ref, w_ref, dres_ref, dx_ref, dw_ref):
        @pl.when(pl.program_id(0) == 0)
        def _():
            dw_ref[...] = jnp.zeros_like(dw_ref)

        xv = x_ref[...]
        r = _rsqrt_ms(xv)
        n = xv * r
        dyv = dy_ref[...]
        dn = dyv * w_ref[...]
        dx_ref[...] = dres_ref[...] + r * (dn - n * jnp.mean(dn * n, axis=-1, keepdims=True))
        dw_ref[...] += _colsum8(dyv * n)

    return pl.pallas_call(
        body, name=name, grid=(T // TB,),
        in_specs=[_row(dm), _row(dm), _const((1, dm)), _row(dm)],
        out_specs=[_row(dm), _const((8, dm))],
        out_shape=[SDS((T, dm), F32), SDS((8, dm), F32)], compiler_params=_params())(dy, x, w, dres)


def _proj_split(proj, wgk, bgk):
    T = proj.shape[0]

    def body(lr_ref, q_ref, k_ref, v_ref, wgk_ref, b_ref, gk_ref, qs_ref, ks_ref, vs_ref):
        u = _dot(lr_ref[...].astype(BF16), wgk_ref[...].astype(BF16)) + b_ref[...]
        gk_ref[...] = -_softplus(-u) / GLA_GATE_NORM
        qs_ref[...] = (q_ref[...] * 0.125).astype(BF16)
        ks_ref[...] = k_ref[...].astype(BF16)
        vs_ref[...] = v_ref[...].astype(BF16)

    return pl.pallas_call(
        body, name="proj_split", grid=(T // TB,),
        in_specs=[_row(128, C_LR // 128), _row(512, C_QS // 512), _row(512, C_KS // 512), _row(512, C_VS // 512),
                  _const((128, 256)), _const((1, 256))],
        out_specs=[_row(256), _row(512), _row(512), _row(512)],
        out_shape=[SDS((T, 256), F32), SDS((T, 512), BF16), SDS((T, 512), BF16), SDS((T, 512), BF16)],
        compiler_params=_params())(proj, proj, proj, proj, wgk, bgk)


def _dproj_assemble(proj, dq_g, dk_g, dv_g, dg_g, dq_s, dk_s, dv_s, dgk, wgk, bgk):
    T = proj.shape[0]

    def body(lr_ref, dqg_ref, dkg_ref, dvg_ref, dgg_ref, dqs_ref, dks_ref, dvs_ref, dgk_ref, wgk_ref, b_ref,
             dp_ref, dwgk_ref, dbgk_ref):
        @pl.when(pl.program_id(0) == 0)
        def _():
            dwgk_ref[...] = jnp.zeros_like(dwgk_ref)
            dbgk_ref[...] = jnp.zeros_like(dbgk_ref)

        lr = lr_ref[...].astype(BF16)
        wg = wgk_ref[...].astype(BF16)
        u = _dot(lr, wg) + b_ref[...]
        du = dgk_ref[...] * (jax.nn.sigmoid(-u) / GLA_GATE_NORM)
        dub = du.astype(BF16)
        dp_ref[:, C_QG:C_KG] = (dqg_ref[...] * 0.125).astype(BF16)
        dp_ref[:, C_KG:C_VG] = dkg_ref[...].astype(BF16)
        dp_ref[:, C_VG:C_GG] = dvg_ref[...].astype(BF16)
        dp_ref[:, C_GG:C_QS] = dgg_ref[...].astype(BF16)
        dp_ref[:, C_QS:C_KS] = (dqs_ref[...] * 0.125).astype(BF16)
        dp_ref[:, C_KS:C_VS] = dks_ref[...].astype(BF16)
        dp_ref[:, C_VS:C_LR] = dvs_ref[...].astype(BF16)
        dp_ref[:, C_LR:DIN_P] = _dot_nt(dub, wg).astype(BF16)
        dwgk_ref[...] += _dot_tn(lr, dub)
        dbgk_ref[...] += _colsum8(du)

    return pl.pallas_call(
        body, name="dproj_assemble", grid=(T // TB,),
        in_specs=[_row(128, C_LR // 128), _row(256), _row(256), _row(512), _row(512), _row(512), _row(512),
                  _row(512), _row(256), _const((128, 256)), _const((1, 256))],
        out_specs=[_row(DIN_P), _const((128, 256)), _const((8, 256))],
        out_shape=[SDS((T, DIN_P), BF16), SDS((128, 256), F32), SDS((8, 256), F32)],
        compiler_params=_params())(proj, dq_g, dk_g, dv_g, dg_g, dq_s, dk_s, dv_s, dgk, wgk, bgk)


def _group_ones(n, g):
    idx = np.arange(n) // g
    return jnp.asarray((idx[:, None] == idx[None, :]).astype(np.float32), dtype=BF16)


def _mix_cat(o_g, proj, o_s, wg512, ws512, bd64):
    T = o_g.shape[0]

    def body(og_ref, gg_ref, os_ref, wg_ref, ws_ref, bd_ref, cat_ref):
        og = og_ref[...]
        gg = gg_ref[...]
        s = gg * jax.nn.sigmoid(gg)
        for h in range(4):
            sl = slice(128 * h, 128 * (h + 1))
            x = og[:, sl]
            cat_ref[:, sl] = (x * _rsqrt_ms(x) * wg_ref[:, sl] * s[:, sl]).astype(BF16)
        osv = os_ref[...]
        ms = _dot_exact(osv * osv, bd_ref[...]) * (1.0 / 64.0)
        cat_ref[:, 512:1024] = (osv * lax.rsqrt(ms + EPS) * ws_ref[...]).astype(BF16)

    return pl.pallas_call(
        body, name="mix_cat", grid=(T // TB,),
        in_specs=[_row(512), _row(512, C_GG // 512), _row(512), _const((1, 512)), _const((1, 512)),
                  _const((512, 512))],
        out_specs=_row(1024), out_shape=SDS((T, 1024), BF16), compiler_params=_params())(
            o_g, proj, o_s, wg512, ws512, bd64)


def _mix_cat_bwd(dcat, o_g, proj, o_s, wg512, ws512, bd64):
    T = o_g.shape[0]

    def body(dc_ref, og_ref, gg_ref, os_ref, wg_ref, ws_ref, bd_ref, dog_ref, dgg_ref, dos_ref, dwg_ref, dws_ref):
        @pl.when(pl.program_id(0) == 0)
        def _():
            dwg_ref[...] = jnp.zeros_like(dwg_ref)
            dws_ref[...] = jnp.zeros_like(dws_ref)

        og = og_ref[...]
        gg = gg_ref[...]
        sg = jax.nn.sigmoid(gg)
        s = gg * sg
        ds = sg * (1.0 + gg * (1.0 - sg))
        for h in range(4):
            sl = slice(128 * h, 128 * (h + 1))
            x = og[:, sl]
            r = _rsqrt_ms(x)
            n = x * r
            w = wg_ref[:, sl]
            dc = dc_ref[:, sl]
            dy = dc * s[:, sl]
            dgg_ref[:, sl] = dc * (n * w) * ds[:, sl]
            dn = dy * w
            dog_ref[:, sl] = r * (dn - n * jnp.mean(dn * n, axis=-1, keepdims=True))
            dwg_ref[:, sl] += _colsum8(dy * n)
        osv = os_ref[...]
        bd = bd_ref[...]
        r = lax.rsqrt(_dot_exact(osv * osv, bd) * (1.0 / 64.0) + EPS)
        n = osv * r
        dc = dc_ref[:, 512:1024]
        dn = dc * ws_ref[...]
        dos_ref[...] = r * (dn - n * (_dot_exact(dn * n, bd) * (1.0 / 64.0)))
        dws_ref[...] += _colsum8(dc * n)

    return pl.pallas_call(
        body, name="mix_cat_bwd", grid=(T // TB,),
        in_specs=[_row(1024), _row(512), _row(512, C_GG // 512), _row(512), _const((1, 512)), _const((1, 512)),
                  _const((512, 512))],
        out_specs=[_row(512), _row(512), _row(512), _const((8, 512)), _const((8, 512))],
        out_shape=[SDS((T, 512), F32), SDS((T, 512), F32), SDS((T, 512), F32), SDS((8, 512), F32),
                   SDS((8, 512), F32)],
        compiler_params=_params())(dcat, o_g, proj, o_s, wg512, ws512, bd64)


def _swiglu_act(gu):
    T = gu.shape[0]

    def body(g_ref, u_ref, a_ref):
        g = g_ref[...]
        a_ref[...] = (g * jax.nn.sigmoid(g) * u_ref[...]).astype(BF16)

    return pl.pallas_call(
        body, name="swiglu_act", grid=(T // TB,), in_specs=[_row(D_FF, 0), _row(D_FF, 1)], out_specs=_row(D_FF),
        out_shape=SDS((T, D_FF), BF16), compiler_params=_params())(gu, gu)


def _swiglu_bwd(da, gu):
    T = gu.shape[0]

    def body(da_ref, g_ref, u_ref, dgu_ref):
        g = g_ref[...]
        sg = jax.nn.sigmoid(g)
        dav = da_ref[...]
        dgu_ref[:, :D_FF] = (dav * u_ref[...] * (sg * (1.0 + g * (1.0 - sg)))).astype(BF16)
        dgu_ref[:, D_FF:] = (dav * (g * sg)).astype(BF16)

    return pl.pallas_call(
        body, name="swiglu_bwd", grid=(T // TB,), in_specs=[_row(D_FF), _row(D_FF, 0), _row(D_FF, 1)],
        out_specs=_row(2 * D_FF), out_shape=SDS((T, 2 * D_FF), BF16), compiler_params=_params())(da, gu, gu)


def _loss_head(y, tgt):
    T = y.shape[0]

    def body(y_ref, t_ref, dy_ref, l_ref):
        @pl.when(pl.program_id(0) == 0)
        def _():
            l_ref[...] = jnp.zeros_like(l_ref)

        e = y_ref[...] - t_ref[...]
        dy_ref[...] = e * (1.0 / D)
        l_ref[...] += _colsum8(e * e) * (0.5 / D)

    return pl.pallas_call(
        body, name="loss_head", grid=(T // TB,), in_specs=[_row(D), _row(D)],
        out_specs=[_row(D), _const((8, D))], out_shape=[SDS((T, D), F32), SDS((8, D), F32)],
        compiler_params=_params())(y, tgt)


def _gla_consts():
    c = GLA_C
    L = np.tril(np.ones((c, c), np.float32))
    blocks = [L, L[(np.arange(c) // 16) * 16]]
    blocks += [np.repeat(L[16 * i:16 * i + 1], c, axis=0) for i in range(4)]
    blocks.append(np.repeat(L[c - 1:c], c, axis=0))
    return jnp.asarray(np.concatenate(blocks, axis=0))


@jax.custom_vjp
def _gla_lin(cm, g):
    y = jnp.dot(cm, g, precision=HIGHEST, preferred_element_type=F32)
    return tuple(y[GLA_C * n:GLA_C * (n + 1)] for n in range(7))


def _gla_lin_fwd(cm, g):
    return _gla_lin(cm, g), cm


def _gla_lin_bwd(cm, cts):
    ct = jnp.concatenate(cts, axis=0)
    dg = lax.dot_general(cm, ct, (((0,), (0,)), ((), ())), precision=HIGHEST, preferred_element_type=F32)
    return None, dg


_gla_lin.defvjp(_gla_lin_fwd, _gla_lin_bwd)


def _gla_chunk_pair(cm, q, k, g, v0, v1, st):
    c = GLA_C
    lane = lax.broadcasted_iota(jnp.int32, (c, 128), 1)
    m0 = (lane < 64).astype(F32)
    m1 = 1.0 - m0
    row = lax.broadcasted_iota(jnp.int32, (c, 128), 0)
    ri = lax.broadcasted_iota(jnp.int32, (c, c), 0)
    ci = lax.broadcasted_iota(jnp.int32, (c, c), 1)
    b, r, r0, r1, r2, r3, bl = _gla_lin(cm, g)
    qs = q * jnp.exp(b - r)
    a0 = jnp.zeros((c, c), F32)
    a1 = jnp.zeros((c, c), F32)
    for blk, rb in enumerate((r0, r1, r2, r3)):
        keep = row < 16 * (blk + 1)
        ksb = jnp.where(keep, k * jnp.exp(jnp.where(keep, rb - b, 0.0)), 0.0)
        qb = jnp.where((row >= 16 * blk) & keep, qs, 0.0)
        a0 = a0 + _dot_nt_f32(qb * m0, ksb)
        a1 = a1 + _dot_nt_f32(qb * m1, ksb)
    causal = ci <= ri
    a0 = jnp.where(causal, a0, 0.0)
    a1 = jnp.where(causal, a1, 0.0)
    qe = q * jnp.exp(b)
    o0 = _dot_nt(qe * m0, st) + _dot(a0, v0)
    o1 = _dot_nt(qe * m1, st) + _dot(a1, v1)
    kd = k * jnp.exp(bl - b)
    m0s = jnp.concatenate([m0, m0], axis=0)
    decay = jnp.exp(jnp.concatenate([bl, bl], axis=0))
    st_new = st * decay + m0s * _dot_tn(v0, kd) + (1.0 - m0s) * _dot_tn(v1, kd)
    return o0, o1, st_new


GLA_TB = 512


def _gla_fwd(proj, gk, cm):
    T = proj.shape[0]
    nc = GLA_TB // GLA_C

    def body(q_ref, k_ref, v_ref, g_ref, cm_ref, o_ref, st_ref, st_scr):
        @pl.when(pl.program_id(0) == 0)
        def _():
            st_scr[...] = jnp.zeros_like(st_scr)

        cmv = cm_ref[...]

        def chunk(ci, carry):
            rs = pl.ds(pl.multiple_of(ci * GLA_C, GLA_C), GLA_C)
            for p in range(2):
                ls = slice(128 * p, 128 * (p + 1))
                st = st_scr[p]
                st_ref[ci, p] = st
                o0, o1, st_new = _gla_chunk_pair(
                    cmv, q_ref[rs, ls] * 0.125, k_ref[rs, ls], g_ref[rs, ls],
                    v_ref[rs, 256 * p:256 * p + 128], v_ref[rs, 256 * p + 128:256 * p + 256], st)
                o_ref[rs, 256 * p:256 * p + 128] = o0
                o_ref[rs, 256 * p + 128:256 * p + 256] = o1
                st_scr[p] = st_new
            return carry

        lax.fori_loop(0, nc, chunk, 0)

    return pl.pallas_call(
        body, name="gla_fwd", grid=(T // GLA_TB,),
        in_specs=[pl.BlockSpec((GLA_TB, 256), lambda i: (i, C_QG // 256)),
                  pl.BlockSpec((GLA_TB, 256), lambda i: (i, C_KG // 256)),
                  pl.BlockSpec((GLA_TB, 512), lambda i: (i, C_VG // 512)),
                  pl.BlockSpec((GLA_TB, 256), lambda i: (i, 0)),
                  pl.BlockSpec((7 * GLA_C, GLA_C), lambda i: (0, 0))],
        out_specs=[pl.BlockSpec((GLA_TB, 512), lambda i: (i, 0)),
                   pl.BlockSpec((nc, 2, 128, 128), lambda i: (i, 0, 0, 0))],
        out_shape=[SDS((T, 512), F32), SDS((T // GLA_C, 2, 128, 128), F32)],
        scratch_shapes=[pltpu.VMEM((2, 128, 128), F32)],
        compiler_params=_params(dimension_semantics=("arbitrary",)))(proj, proj, proj, gk, cm)


def _gla_bwd(proj, gk, cm, states, do):
    T = proj.shape[0]
    nb = T // GLA_TB
    nc = GLA_TB // GLA_C

    def body(q_ref, k_ref, v_ref, g_ref, cm_ref, st_ref, do_ref, dq_ref, dk_ref, dv_ref, dg_ref, dst_scr):
        @pl.when(pl.program_id(0) == 0)
        def _():
            dst_scr[...] = jnp.zeros_like(dst_scr)

        cmv = cm_ref[...]

        def chunk(t, carry):
            ci = nc - 1 - t
            rs = pl.ds(pl.multiple_of(ci * GLA_C, GLA_C), GLA_C)
            for p in range(2):
                ls = slice(128 * p, 128 * (p + 1))
                _, vjp = jax.vjp(
                    functools.partial(_gla_chunk_pair, cmv),
                    q_ref[rs, ls] * 0.125, k_ref[rs, ls], g_ref[rs, ls],
                    v_ref[rs, 256 * p:256 * p + 128], v_ref[rs, 256 * p + 128:256 * p + 256], st_ref[ci, p])
                dq, dk, dg, dv0, dv1, dst = vjp((do_ref[rs, 256 * p:256 * p + 128],
                                                 do_ref[rs, 256 * p + 128:256 * p + 256], dst_scr[p]))
                dq_ref[rs, ls] = dq
                dk_ref[rs, ls] = dk
                dg_ref[rs, ls] = dg
                dv_ref[rs, 256 * p:256 * p + 128] = dv0
                dv_ref[rs, 256 * p + 128:256 * p + 256] = dv1
                dst_scr[p] = dst
            return carry

        lax.fori_loop(0, nc, chunk, 0)

    rev = lambda i: nb - 1 - i
    return pl.pallas_call(
        body, name="gla_bwd", grid=(nb,),
        in_specs=[pl.BlockSpec((GLA_TB, 256), lambda i: (rev(i), C_QG // 256)),
                  pl.BlockSpec((GLA_TB, 256), lambda i: (rev(i), C_KG // 256)),
                  pl.BlockSpec((GLA_TB, 512), lambda i: (rev(i), C_VG // 512)),
                  pl.BlockSpec((GLA_TB, 256), lambda i: (rev(i), 0)),
                  pl.BlockSpec((7 * GLA_C, GLA_C), lambda i: (0, 0)),
                  pl.BlockSpec((nc, 2, 128, 128), lambda i: (rev(i), 0, 0, 0)),
                  pl.BlockSpec((GLA_TB, 512), lambda i: (rev(i), 0))],
        out_specs=[pl.BlockSpec((GLA_TB, 256), lambda i: (rev(i), 0)),
                   pl.BlockSpec((GLA_TB, 256), lambda i: (rev(i), 0)),
                   pl.BlockSpec((GLA_TB, 512), lambda i: (rev(i), 0)),
                   pl.BlockSpec((GLA_TB, 256), lambda i: (rev(i), 0))],
        out_shape=[SDS((T, 256), F32), SDS((T, 256), F32), SDS((T, 512), F32), SDS((T, 256), F32)],
        scratch_shapes=[pltpu.VMEM((2, 128, 128), F32)],
        compiler_params=_params(dimension_semantics=("arbitrary",)))(proj, proj, proj, gk, cm, states, do)


SB_DEAD = 105.0
SB_COUNT_LANE = 127


def _sb_tri():
    i = np.arange(SBQ)
    return jnp.asarray((i[:, None] > i[None, :]).astype(np.float32), dtype=BF16)


def _sb_block_fwd(qh, kb, tri, carry, strict):
    z = _dot_nt(qh, kb)
    sp = _softplus(z)
    l1 = -sp
    if strict is not None:
        l1 = jnp.where(strict, l1, 0.0)
    log_a = (z - sp) + _dot_hilo(l1, tri) + carry
    a = jnp.exp(log_a)
    if strict is not None:
        a = jnp.where(strict, a, 0.0)
    return z - sp, l1, a


def _sb_fwd(qs, ks, vs, tri):
    T = qs.shape[0]
    nq = T // SBQ

    def body(q_ref, k_ref, v_ref, tri_ref, o_ref, c_ref):
        i = pl.program_id(1)
        lane = lax.broadcasted_iota(jnp.int32, (1, 128), 1)
        clane = lax.broadcasted_iota(jnp.int32, (SBQ, 128), 1)
        strict = (lax.broadcasted_iota(jnp.int32, (SBQ, SBQ), 1) < lax.broadcasted_iota(jnp.int32, (SBQ, SBQ), 0))
        tri_v = tri_ref[...]
        qv = q_ref[...]
        out = jnp.zeros((SBQ, 128), F32)
        for hh in range(2):
            hm = (lane // 64) == hh
            qh = jnp.where(hm, qv, jnp.zeros_like(qv))

            def block(j, carry, acc, masked):
                rs = pl.ds(pl.multiple_of(j * SBQ, SBQ), SBQ)
                _, l1, a = _sb_block_fwd(qh, k_ref[rs, :], tri_v, carry, strict if masked else None)
                acc = acc + _dot(a.astype(BF16), v_ref[rs, :])
                return carry + jnp.sum(l1, axis=1, keepdims=True), acc

            carry, acc = block(i, jnp.zeros((SBQ, 1), F32), jnp.zeros((SBQ, 128), F32), True)

            def more(state):
                return (state[0] <= i) & (jnp.max(state[1]) > -SB_DEAD)

            def step(state):
                jj, carry, acc, ct = state
                j = i - jj
                ct = jnp.where(clane == j, carry, ct)
                carry, acc = block(j, carry, acc, False)
                return jj + 1, carry, acc, ct

            jj, carry, acc, ct = lax.while_loop(
                more, step, (jnp.int32(1), carry, acc, jnp.zeros((SBQ, 128), F32)))
            out = out + jnp.where(hm, acc, 0.0)
            c_ref[0, :, 128 * hh:128 * (hh + 1)] = jnp.where(clane == SB_COUNT_LANE, (jj - 1).astype(F32), ct)
        o_ref[...] = out

    return pl.pallas_call(
        body, name="sb_fwd", grid=(4, nq),
        in_specs=[pl.BlockSpec((SBQ, 128), lambda h, i: (i, h)),
                  pl.BlockSpec((T, 128), lambda h, i: (0, h)),
                  pl.BlockSpec((T, 128), lambda h, i: (0, h)),
                  pl.BlockSpec((SBQ, SBQ), lambda h, i: (0, 0))],
        out_specs=[pl.BlockSpec((SBQ, 128), lambda h, i: (i, h)),
                   pl.BlockSpec((1, SBQ, 256), lambda h, i: (h, i, 0))],
        out_shape=[SDS((T, 512), F32), SDS((4, T, 256), F32)],
        compiler_params=_params(dimension_semantics=("parallel", "arbitrary")))(qs, ks, vs, tri)


def _sb_bwd(qs, ks, vs, do, carries, tri, tri_t):
    T = qs.shape[0]
    nq = T // SBQ

    def body(q_ref, k_ref, v_ref, do_ref, c_ref, tri_ref, trit_ref, dq_ref, dk_ref, dv_ref):
        i = pl.program_id(1)

        @pl.when(i == 0)
        def _():
            dk_ref[...] = jnp.zeros_like(dk_ref)
            dv_ref[...] = jnp.zeros_like(dv_ref)

        lane = lax.broadcasted_iota(jnp.int32, (1, 128), 1)
        clane = lax.broadcasted_iota(jnp.int32, (SBQ, 128), 1)
        strict = (lax.broadcasted_iota(jnp.int32, (SBQ, SBQ), 1) < lax.broadcasted_iota(jnp.int32, (SBQ, SBQ), 0))
        tri_v = tri_ref[...]
        trit_v = trit_ref[...]
        qv = q_ref[...]
        dov = do_ref[...].astype(BF16)
        dq_out = jnp.zeros((SBQ, 128), F32)
        for hh in range(2):
            hm = (lane // 64) == hh
            qh = jnp.where(hm, qv, jnp.zeros_like(qv))
            doh = jnp.where(hm, dov, jnp.zeros_like(dov))
            ct = c_ref[0, :, 128 * hh:128 * (hh + 1)]

            def block(j, pcarry, dq, masked):
                rs = pl.ds(pl.multiple_of(j * SBQ, SBQ), SBQ)
                kb = k_ref[rs, :]
                vb = v_ref[rs, :]
                carry = jnp.sum(jnp.where(clane == j, ct, 0.0), axis=1, keepdims=True)
                lb, _, a = _sb_block_fwd(qh, kb, tri_v, carry, strict if masked else None)
                g = a * _dot_nt(doh, vb)
                p = _dot_hilo(g, trit_v) + pcarry
                dz = g - (g + p) * jnp.exp(lb)
                if masked:
                    dz = jnp.where(strict, dz, 0.0)
                dzb = dz.astype(BF16)
                dk_ref[rs, :] += _dot_tn(dzb, qh)
                dv_ref[rs, :] += _dot_tn(a.astype(BF16), doh)
                return pcarry + jnp.sum(g, axis=1, keepdims=True), dq + _dot(dzb, kb)

            def step(j, state):
                return block(j, state[0], state[1], False)

            swept = jnp.max(jnp.where(clane == SB_COUNT_LANE, ct, 0.0)).astype(jnp.int32)
            first = i - jnp.clip(swept, 0, i)
            pcarry, dq = lax.fori_loop(first, i, step, (jnp.zeros((SBQ, 1), F32), jnp.zeros((SBQ, 128), F32)))
            _, dq = block(i, pcarry, dq, True)
            dq_out = dq_out + jnp.where(hm, dq, 0.0)
        dq_ref[...] = dq_out

    return pl.pallas_call(
        body, name="sb_bwd", grid=(4, nq),
        in_specs=[pl.BlockSpec((SBQ, 128), lambda h, i: (i, h)),
                  pl.BlockSpec((T, 128), lambda h, i: (0, h)),
                  pl.BlockSpec((T, 128), lambda h, i: (0, h)),
                  pl.BlockSpec((SBQ, 128), lambda h, i: (i, h)),
                  pl.BlockSpec((1, SBQ, 256), lambda h, i: (h, i, 0)),
                  pl.BlockSpec((SBQ, SBQ), lambda h, i: (0, 0)),
                  pl.BlockSpec((SBQ, SBQ), lambda h, i: (0, 0))],
        out_specs=[pl.BlockSpec((SBQ, 128), lambda h, i: (i, h)),
                   pl.BlockSpec((T, 128), lambda h, i: (0, h)),
                   pl.BlockSpec((T, 128), lambda h, i: (0, h))],
        out_shape=[SDS((T, 512), F32), SDS((T, 512), F32), SDS((T, 512), F32)],
        compiler_params=_params(dimension_semantics=("parallel", "arbitrary")))(qs, ks, vs, do, carries, tri, tri_t)


def _mem_fwd(mem, mem_norm_w, w_mkv, mk_norm_w):
    M = mem.shape[0]

    def body(mem_ref, wn_ref, w_ref, wk_ref, mn_ref, kraw_ref, k_ref, v_ref):
        mv = mem_ref[...]
        mn = (mv * _rsqrt_ms(mv) * wn_ref[...]).astype(BF16)
        mn_ref[...] = mn
        kv = _dot(mn, w_ref[...])
        kraw_ref[...] = kv[:, :D]
        v_ref[...] = kv[:, D:].astype(BF16)
        for h in range(MEM_HEADS):
            sl = slice(MEM_HD * h, MEM_HD * (h + 1))
            x = kv[:, sl]
            k_ref[:, sl] = (x * _rsqrt_ms(x) * wk_ref[...]).astype(BF16)

    vm = pl.BlockSpec(memory_space=pltpu.VMEM)
    return pl.pallas_call(
        body, name="mem_fwd", in_specs=[vm] * 4, out_specs=[vm] * 4,
        out_shape=[SDS((M, D), BF16), SDS((M, D), F32), SDS((M, D), BF16), SDS((M, D), BF16)],
        compiler_params=_params())(mem, mem_norm_w, w_mkv, mk_norm_w)


def _xattn_fwd(qraw, k, v, wq):
    T = qraw.shape[0]
    M = k.shape[0]

    def body(q_ref, k_ref, v_ref, wq_ref, o_ref):
        for h in range(MEM_HEADS):
            sl = slice(MEM_HD * h, MEM_HD * (h + 1))
            x = q_ref[:, sl]
            q = (x * _rsqrt_ms(x) * wq_ref[...]).astype(BF16)
            s = _dot_nt(q, k_ref[:, sl]) * (1.0 / math.sqrt(MEM_HD))
            s = s - jnp.max(s, axis=-1, keepdims=True)
            e = jnp.exp(s)
            p = e / jnp.sum(e, axis=-1, keepdims=True)
            o_ref[:, sl] = _dot(p.astype(BF16), v_ref[:, sl]).astype(BF16)

    return pl.pallas_call(
        body, name="xattn_fwd", grid=(T // TB,),
        in_specs=[_row(D), _const((M, D)), _const((M, D)), _const((1, MEM_HD))],
        out_specs=_row(D), out_shape=SDS((T, D), BF16), compiler_params=_params())(qraw, k, v, wq)


def _xattn_bwd(qraw, k, v, wq, do):
    T = qraw.shape[0]
    M = k.shape[0]

    def body(q_ref, k_ref, v_ref, wq_ref, do_ref, dq_ref, dk_ref, dv_ref, dw_ref):
        @pl.when(pl.program_id(0) == 0)
        def _():
            dk_ref[...] = jnp.zeros_like(dk_ref)
            dv_ref[...] = jnp.zeros_like(dv_ref)
            dw_ref[...] = jnp.zeros_like(dw_ref)

        w = wq_ref[...]
        for h in range(MEM_HEADS):
            sl = slice(MEM_HD * h, MEM_HD * (h + 1))
            x = q_ref[:, sl]
            r = _rsqrt_ms(x)
            n = x * r
            q = (n * w).astype(BF16)
            kb = k_ref[:, sl]
            s = _dot_nt(q, kb) * (1.0 / math.sqrt(MEM_HD))
            s = s - jnp.max(s, axis=-1, keepdims=True)
            e = jnp.exp(s)
            p = e / jnp.sum(e, axis=-1, keepdims=True)
            dob = do_ref[:, sl].astype(BF16)
            dp = _dot_nt(dob, v_ref[:, sl])
            ds = (p * (dp - jnp.sum(dp * p, axis=-1, keepdims=True)) * (1.0 / math.sqrt(MEM_HD))).astype(BF16)
            dv_ref[:, sl] += _dot_tn(p.astype(BF16), dob)
            dk_ref[:, sl] += _dot_tn(ds, q)
            dqn = _dot(ds, kb)
            dn = dqn * w
            dq_ref[:, sl] = r * (dn - n * jnp.mean(dn * n, axis=-1, keepdims=True))
            dw_ref[...] += _colsum8(dqn * n)

    return pl.pallas_call(
        body, name="xattn_bwd", grid=(T // TB,),
        in_specs=[_row(D), _const((M, D)), _const((M, D)), _const((1, MEM_HD)), _row(D)],
        out_specs=[_row(D), _const((M, D)), _const((M, D)), _const((8, MEM_HD))],
        out_shape=[SDS((T, D), F32), SDS((M, D), F32), SDS((M, D), F32), SDS((8, MEM_HD), F32)],
        compiler_params=_params())(qraw, k, v, wq, do)


def _mem_bwd(mem, mem_norm_w, w_mkv, mk_norm_w, mem_n, k_raw, dk, dv):
    M = mem.shape[0]

    def body(mem_ref, wn_ref, w_ref, wk_ref, mn_ref, kraw_ref, dk_ref, dv_ref, dw_ref, dwn_ref, dwk_ref, dkv_scr):
        wk = wk_ref[...]
        dwk = jnp.zeros((8, MEM_HD), F32)
        for h in range(MEM_HEADS):
            sl = slice(MEM_HD * h, MEM_HD * (h + 1))
            x = kraw_ref[:, sl]
            r = _rsqrt_ms(x)
            n = x * r
            dkh = dk_ref[:, sl]
            dn = dkh * wk
            dkv_scr[:, sl] = (r * (dn - n * jnp.mean(dn * n, axis=-1, keepdims=True))).astype(BF16)
            dwk = dwk + _colsum8(dkh * n)
        dwk_ref[...] = dwk
        dkv_scr[:, D:] = dv_ref[...].astype(BF16)
        dkv = dkv_scr[...]
        dw_ref[...] = _dot_tn(mn_ref[...], dkv)
        dmn = _dot_nt(dkv, w_ref[...])
        mv = mem_ref[...]
        dwn_ref[...] = _colsum8(dmn * (mv * _rsqrt_ms(mv)))

    vm = pl.BlockSpec(memory_space=pltpu.VMEM)
    return pl.pallas_call(
        body, name="mem_bwd", in_specs=[vm] * 8, out_specs=[vm] * 3,
        out_shape=[SDS((D, 2 * D), F32), SDS((8, D), F32), SDS((8, MEM_HD), F32)],
        scratch_shapes=[pltpu.VMEM((M, 2 * D), BF16)],
        compiler_params=_params())(mem, mem_norm_w, w_mkv, mk_norm_w, mem_n, k_raw, dk, dv)


def _local_step(x, mem, tgt, w):
    wgk = jnp.zeros((128, 256), F32).at[:16].set(w["w_gk_up"].astype(F32))
    wg512 = jnp.tile(w["gla_norm_w"], (1, 4))
    ws512 = jnp.tile(w["sb_norm_w"], (1, 8))
    bd64 = _group_ones(512, 64)
    cm = _gla_consts()
    tri = _sb_tri()
    tri_t = tri.T

    h1 = _norm_fwd(x, w["mix_norm_w"], "norm1_fwd")
    proj = _matmul(h1, w["w_in"], mode="nn", tm=TB, tn=DIN_P, name="mm_proj")
    gk, qs, ks, vs = _proj_split(proj, wgk, w["b_gk"])
    o_g, states = _gla_fwd(proj, gk, cm)
    o_s, carries = _sb_fwd(qs, ks, vs, tri)
    cat = _mix_cat(o_g, proj, o_s, wg512, ws512, bd64)
    x1 = _matmul(cat, w["w_out"], mode="nn", tm=TB, tn=D, res=x, name="mm_out")
    h2 = _norm_fwd(x1, w["xattn_norm_w"], "norm2_fwd")
    qraw = _matmul(h2, w["w_mq"], mode="nn", tm=TB, tn=D, name="mm_mq")
    mem_n, k_raw, k_n, v_m = _mem_fwd(mem, w["mem_norm_w"], w["w_mkv"], w["mk_norm_w"])
    om = _xattn_fwd(qraw, k_n, v_m, w["mq_norm_w"])
    x2 = _matmul(om, w["w_mo"], mode="nn", tm=TB, tn=D, res=x1, name="mm_mo")
    h3 = _norm_fwd(x2, w["ffn_norm_w"], "norm3_fwd")
    gu = _matmul(h3, w["w_gate_up"], mode="nn", tm=TB, tn=D_FF, name="mm_gate_up")
    act = _swiglu_act(gu)
    x3 = _matmul(act, w["w_down"], mode="nn", tm=TB, tn=D, res=x2, name="mm_down")
    dx3, loss_rows = _loss_head(x3, tgt)

    g = {}
    g["w_down"] = _matmul(act, dx3, mode="tn", tm=1408, tn=D, tk=TB, name="mm_dw_down")
    da = _matmul(dx3, w["w_down"], mode="nt", tm=TB, tn=1408, name="mm_da")
    dgu = _swiglu_bwd(da, gu)
    g["w_gate_up"] = _matmul(h3, dgu, mode="tn", tm=D, tn=1408, tk=TB, name="mm_dw_gate_up")
    dh3 = _matmul(dgu, w["w_gate_up"], mode="nt", tm=TB, tn=D, name="mm_dh3")
    dx2, g["ffn_norm_w"] = _norm_bwd(dh3, x2, w["ffn_norm_w"], dx3, "norm3_bwd")
    g["w_mo"] = _matmul(om, dx2, mode="tn", tm=D, tn=D, tk=TB, name="mm_dw_mo")
    dom = _matmul(dx2, w["w_mo"], mode="nt", tm=TB, tn=D, name="mm_dom")
    dqraw, dk_n, dv_m, g["mq_norm_w"] = _xattn_bwd(qraw, k_n, v_m, w["mq_norm_w"], dom)
    g["w_mkv"], g["mem_norm_w"], g["mk_norm_w"] = _mem_bwd(
        mem, w["mem_norm_w"], w["w_mkv"], w["mk_norm_w"], mem_n, k_raw, dk_n, dv_m)
    g["w_mq"] = _matmul(h2, dqraw, mode="tn", tm=D, tn=D, tk=TB, name="mm_dw_mq")
    dh2 = _matmul(dqraw, w["w_mq"], mode="nt", tm=TB, tn=D, name="mm_dh2")
    dx1, g["xattn_norm_w"] = _norm_bwd(dh2, x1, w["xattn_norm_w"], dx2, "norm2_bwd")
    g["w_out"] = _matmul(cat, dx1, mode="tn", tm=D, tn=D, tk=TB, name="mm_dw_out")
    dcat = _matmul(dx1, w["w_out"], mode="nt", tm=TB, tn=D, name="mm_dcat")
    do_g, dg_g, do_s, dwg, dws = _mix_cat_bwd(dcat, o_g, proj, o_s, wg512, ws512, bd64)
    dq_s, dk_s, dv_s = _sb_bwd(qs, ks, vs, do_s, carries, tri, tri_t)
    dq_g, dk_g, dv_g, dgk = _gla_bwd(proj, gk, cm, states, do_g)
    dproj, dwgk, g["b_gk"] = _dproj_assemble(proj, dq_g, dk_g, dv_g, dg_g, dq_s, dk_s, dv_s, dgk, wgk, w["b_gk"])
    g["w_in"] = _matmul(h1, dproj, mode="tn", tm=D, tn=640, tk=TB, name="mm_dw_in")
    dh1 = _matmul(dproj, w["w_in"], mode="nt", tm=TB, tn=D, name="mm_dh1")
    grad_x, g["mix_norm_w"] = _norm_bwd(dh1, x, w["mix_norm_w"], dx1, "norm1_bwd")

    g["w_gk_up"] = dwgk[:16]
    g["gla_norm_w"] = dwg.reshape(8, 4, 128).sum(axis=1)
    g["sb_norm_w"] = dws.reshape(8, 8, 64).sum(axis=1)
    for n in SMALL:
        g[n] = jnp.sum(g[n], axis=0, keepdims=True)
    return jnp.sum(loss_rows), grad_x, g


def _mesh_pos():
    return lax.axis_index("x"), lax.axis_index("y"), lax.axis_index("c")


def _other_chips(x, y):
    return [(1 - x, y), (x, 1 - y), (1 - x, 1 - y)]


def _ag_weights(wp):
    def body(w_ref, out_ref, send_sems, recv_sems, local_sem):
        x, y, c = _mesh_pos()
        me = 2 * x + y
        sibling = (x, y, 1 - c)
        chips = _other_chips(x, y)
        mine, theirs = c, 1 - c
        local = pltpu.make_async_copy(w_ref, out_ref.at[me], local_sem)
        local.start()

        def copy(k, src, dst, to):
            return pltpu.make_async_remote_copy(src_ref=src, dst_ref=dst, send_sem=send_sems.at[k],
                                                recv_sem=recv_sems.at[k], device_id=to, device_id_type=MESH)

        first = [copy(k, w_ref.at[mine], out_ref.at[me, mine], (cx, cy, c)) for k, (cx, cy) in enumerate(chips)]
        for cp in first:
            cp.start()
        passed = []
        for k, (cx, cy) in enumerate(chips):
            slot = out_ref.at[2 * cx + cy, mine]
            copy(k, slot, slot, (cx, cy, c)).wait_recv()
            fwd = copy(3 + k, slot, slot, sibling)
            fwd.start()
            passed.append(fwd)
        for k, (cx, cy) in enumerate(chips):
            slot = out_ref.at[2 * cx + cy, theirs]
            copy(3 + k, slot, slot, sibling).wait_recv()
        for cp in first + passed:
            cp.wait_send()
        local.wait()

    hbm = pl.BlockSpec(memory_space=pl.ANY)
    return pl.pallas_call(
        body, name="ag_weights", in_specs=[hbm], out_specs=hbm,
        out_shape=SDS((4, 2, PACK_HALF, 1024), BF16),
        scratch_shapes=[pltpu.SemaphoreType.DMA((6,)), pltpu.SemaphoreType.DMA((6,)), pltpu.SemaphoreType.DMA],
        compiler_params=pltpu.CompilerParams(has_side_effects=True))(wp)


def _rs_swap_halves(gp):
    def body(g_ref, out_ref, send_sem, recv_sem):
        x, y, c = _mesh_pos()
        cp = pltpu.make_async_remote_copy(
            src_ref=g_ref.at[:, 1 - c], dst_ref=out_ref, send_sem=send_sem, recv_sem=recv_sem,
            device_id=(x, y, 1 - c), device_id_type=MESH)
        cp.start()
        cp.wait()

    hbm = pl.BlockSpec(memory_space=pl.ANY)
    return pl.pallas_call(
        body, name="rs_swap_halves", in_specs=[hbm], out_specs=hbm,
        out_shape=SDS((4, PACK_HALF, 1024), F32),
        scratch_shapes=[pltpu.SemaphoreType.DMA, pltpu.SemaphoreType.DMA],
        compiler_params=pltpu.CompilerParams(has_side_effects=True))(gp)


RS_TR = 704


def _rs_add_halves(gp, other, c_arr):
    nt = PACK_HALF // RS_TR

    def body(c_ref, a_ref, b_ref, o_ref):
        o_ref[...] = a_ref[0] + b_ref[...]

    return pl.pallas_call(
        body, name="rs_add_halves",
        grid_spec=pltpu.PrefetchScalarGridSpec(
            num_scalar_prefetch=1, grid=(4, nt),
            in_specs=[pl.BlockSpec((1, 1, RS_TR, 1024), lambda s, t, c: (s, c[0], t, 0)),
                      pl.BlockSpec((1, RS_TR, 1024), lambda s, t, c: (s, t, 0))],
            out_specs=pl.BlockSpec((1, RS_TR, 1024), lambda s, t, c: (s, t, 0))),
        out_shape=SDS((4, PACK_HALF, 1024), F32), compiler_params=_params())(c_arr, gp, other)


def _rs_exchange(part):
    def body(p_ref, out_ref, send_sems, recv_sems, local_sem):
        x, y, c = _mesh_pos()
        me = 2 * x + y
        chips = _other_chips(x, y)
        local = pltpu.make_async_copy(p_ref.at[me], out_ref.at[me], local_sem)
        local.start()
        sends = []
        for k, (cx, cy) in enumerate(chips):
            cp = pltpu.make_async_remote_copy(
                src_ref=p_ref.at[2 * cx + cy], dst_ref=out_ref.at[me], send_sem=send_sems.at[k],
                recv_sem=recv_sems.at[k], device_id=(cx, cy, c), device_id_type=MESH)
            cp.start()
            sends.append(cp)
        for k, (cx, cy) in enumerate(chips):
            slot = out_ref.at[2 * cx + cy]
            pltpu.make_async_remote_copy(
                src_ref=slot, dst_ref=slot, send_sem=send_sems.at[k], recv_sem=recv_sems.at[k],
                device_id=(cx, cy, c), device_id_type=MESH).wait_recv()
        for cp in sends:
            cp.wait_send()
        local.wait()

    hbm = pl.BlockSpec(memory_space=pl.ANY)
    return pl.pallas_call(
        body, name="rs_exchange", in_specs=[hbm], out_specs=hbm,
        out_shape=SDS((4, PACK_HALF, 1024), F32),
        scratch_shapes=[pltpu.SemaphoreType.DMA((3,)), pltpu.SemaphoreType.DMA((3,)), pltpu.SemaphoreType.DMA],
        compiler_params=pltpu.CompilerParams(has_side_effects=True))(part)


def _rs_add_chips(parts):
    def body(p_ref, o_ref):
        o_ref[...] = ((p_ref[0] + p_ref[1]) + p_ref[2]) + p_ref[3]

    return pl.pallas_call(
        body, name="rs_add_chips", grid=(PACK_HALF // RS_TR,),
        in_specs=[pl.BlockSpec((4, RS_TR, 1024), lambda t: (0, t, 0))],
        out_specs=pl.BlockSpec((RS_TR, 1024), lambda t: (t, 0)),
        out_shape=SDS((PACK_HALF, 1024), F32), compiler_params=_params())(parts)


def _rs_share(half_sum):
    def body(h_ref, out_ref, send_sem, recv_sem, local_sem):
        x, y, c = _mesh_pos()
        local = pltpu.make_async_copy(h_ref, out_ref.at[c], local_sem)
        local.start()
        cp = pltpu.make_async_remote_copy(
            src_ref=h_ref, dst_ref=out_ref.at[c], send_sem=send_sem, recv_sem=recv_sem,
            device_id=(x, y, 1 - c), device_id_type=MESH)
        cp.start()
        theirs = out_ref.at[1 - c]
        pltpu.make_async_remote_copy(
            src_ref=theirs, dst_ref=theirs, send_sem=send_sem, recv_sem=recv_sem,
            device_id=(x, y, 1 - c), device_id_type=MESH).wait_recv()
        cp.wait_send()
        local.wait()

    hbm = pl.BlockSpec(memory_space=pl.ANY)
    return pl.pallas_call(
        body, name="rs_share", in_specs=[hbm], out_specs=hbm,
        out_shape=SDS((2, PACK_HALF, 1024), F32),
        scratch_shapes=[pltpu.SemaphoreType.DMA, pltpu.SemaphoreType.DMA, pltpu.SemaphoreType.DMA],
        compiler_params=pltpu.CompilerParams(has_side_effects=True))(half_sum)


def _adamw(w, g, m, v, name):
    rows, cols = w.shape
    tr = rows
    for cand in (512, 352, 256):
        if rows > cand and rows % cand == 0:
            tr = cand
            break
    c1 = 1.0 - ADAM_B1 ** ADAM_STEP
    c2 = 1.0 - ADAM_B2 ** ADAM_STEP

    def body(w_ref, g_ref, m_ref, v_ref, d_ref, mo_ref, vo_ref):
        gv = g_ref[...]
        mn = ADAM_B1 * m_ref[...] + (1.0 - ADAM_B1) * gv
        vn = ADAM_B2 * v_ref[...] + (1.0 - ADAM_B2) * (gv * gv)
        mo_ref[...] = mn
        vo_ref[...] = vn
        d_ref[...] = -ADAM_LR * ((mn / c1) / (jnp.sqrt(vn / c2) + ADAM_EPS) + ADAM_WD * w_ref[...])

    spec = pl.BlockSpec((tr, cols), lambda i: (i, 0))
    return pl.pallas_call(
        body, name=name, grid=(rows // tr,), in_specs=[spec] * 4, out_specs=[spec] * 3,
        out_shape=[SDS((rows, cols), F32)] * 3, compiler_params=_params())(w, g, m, v)


BIG_SHARD_ROWS = {"w_in": 772, "w_gk_up": 1, "w_out": 256, "w_mq": 256, "w_mkv": 512, "w_mo": 256,
                  "w_gate_up": 1408, "w_down": 704}
SMALL_SIZES = {"mix_norm_w": 1024, "b_gk": 256, "gla_norm_w": 128, "sb_norm_w": 64, "xattn_norm_w": 1024,
               "mem_norm_w": 1024, "mq_norm_w": 256, "mk_norm_w": 256, "ffn_norm_w": 1024}
N_BIG_ROWS = sum(BIG_SHARD_ROWS.values())


def _pack_small(d):
    flat = jnp.concatenate([d[n].reshape(-1) for n in SMALL])
    return jnp.pad(flat, (0, SMALL_ROWS * 1024 - flat.shape[0])).reshape(SMALL_ROWS, 1024)


def _unpack_small(p):
    flat = p.reshape(-1)
    out, off = {}, 0
    for n in SMALL:
        out[n] = flat[off:off + SMALL_SIZES[n]].reshape(1, SMALL_SIZES[n])
        off += SMALL_SIZES[n]
    return out


def _pack_shards(d, small, dtype):
    rows = [d[n].astype(dtype).reshape(BIG_SHARD_ROWS[n], 1024) for n in BIG] + [small.astype(dtype)]
    pad = jnp.zeros((PACK_ROWS - N_BIG_ROWS - SMALL_ROWS, 1024), dtype)
    return jnp.concatenate(rows + [pad], axis=0)


def _big_offsets():
    off, out = 0, {}
    for n in BIG:
        out[n] = off
        off += BIG_SHARD_ROWS[n]
    return out


def _unpack_full(allw):
    off = _big_offsets()

    def seg(n):
        return allw[:, off[n]:off[n] + BIG_SHARD_ROWS[n]]

    def cols(n, k, c):
        return seg(n).reshape(4, k, c).transpose(1, 0, 2).reshape(k, 4 * c)

    def rows(n, c):
        return seg(n).reshape(-1, c)

    w_in = cols("w_in", 1024, 772)
    w_in = jnp.concatenate([w_in[:, :1536], w_in[:, 1552:], w_in[:, 1536:1552],
                            jnp.zeros((1024, DIN_P - DIN), w_in.dtype)], axis=1)
    return {"w_in": w_in, "w_gk_up": cols("w_gk_up", 16, 64), "w_out": rows("w_out", 1024),
            "w_mq": rows("w_mq", 1024), "w_mkv": cols("w_mkv", 1024, 512), "w_mo": rows("w_mo", 1024),
            "w_gate_up": cols("w_gate_up", 1024, 1408), "w_down": rows("w_down", 1024)}


def _pack_grads(g):
    def cols(a, k, c):
        return a.reshape(k, 4, c).transpose(1, 0, 2).reshape(4, -1, 1024)

    def rows(a):
        return a.reshape(4, -1, 1024)

    gi = g["w_in"]
    gi = jnp.concatenate([gi[:, :1536], gi[:, C_LR:C_LR + 16], gi[:, 1536:C_LR]], axis=1)
    parts = [cols(gi, 1024, 772), cols(g["w_gk_up"], 16, 64), rows(g["w_out"]), rows(g["w_mq"]),
             cols(g["w_mkv"], 1024, 512), rows(g["w_mo"]), cols(g["w_gate_up"], 1024, 1408), rows(g["w_down"])]
    small = jnp.broadcast_to(_pack_small(g)[None], (4, SMALL_ROWS, 1024))
    pad = jnp.zeros((4, PACK_ROWS - N_BIG_ROWS - SMALL_ROWS, 1024), F32)
    return jnp.concatenate(parts + [small, pad], axis=1)


def kernel(x, mem, mix_norm_w, w_in, w_gk_up, b_gk, gla_norm_w, sb_norm_w, w_out, xattn_norm_w, mem_norm_w, w_mq, w_mkv, mq_norm_w, mk_norm_w, w_mo, ffn_norm_w, w_gate_up, w_down, loss_target, m_mix_norm_w, m_w_in, m_w_gk_up, m_b_gk, m_gla_norm_w, m_sb_norm_w, m_w_out, m_xattn_norm_w, m_mem_norm_w, m_w_mq, m_w_mkv, m_mq_norm_w, m_mk_norm_w, m_w_mo, m_ffn_norm_w, m_w_gate_up, m_w_down, v_mix_norm_w, v_w_in, v_w_gk_up, v_b_gk, v_gla_norm_w, v_sb_norm_w, v_w_out, v_xattn_norm_w, v_mem_norm_w, v_w_mq, v_w_mkv, v_mq_norm_w, v_mk_norm_w, v_w_mo, v_ffn_norm_w, v_w_gate_up, v_w_down):
    args = dict(locals())
    wts = {n: args[n][0] if n in BIG else args[n] for n in WEIGHTS}
    mom = {n: args["m_" + n][0] if n in BIG else args["m_" + n] for n in WEIGHTS}
    var = {n: args["v_" + n][0] if n in BIG else args["v_" + n] for n in WEIGHTS}

    small_w = _pack_small(wts)
    allw = _ag_weights(_pack_shards(wts, small_w, BF16).reshape(2, PACK_HALF, 1024))
    full = _unpack_full(allw.reshape(4, PACK_ROWS, 1024))
    full.update({n: wts[n] for n in SMALL})

    loss, grad_x, g = _local_step(x[0], mem[0], loss_target[0], full)
    loss = lax.psum(loss, ("x", "y", "c"))

    gp = _pack_grads(g).reshape(4, 2, PACK_HALF, 1024)
    c_arr = lax.axis_index("c").astype(jnp.int32).reshape(1)
    part = _rs_add_halves(gp, _rs_swap_halves(gp), c_arr)
    total = _rs_share(_rs_add_chips(_rs_exchange(part))).reshape(PACK_ROWS, 1024)

    off = _big_offsets()
    grads = {n: total[off[n]:off[n] + BIG_SHARD_ROWS[n]].reshape(wts[n].shape) for n in BIG}
    grads.update(_unpack_small(total[N_BIG_ROWS:N_BIG_ROWS + SMALL_ROWS]))

    delta, new_m, new_v = {}, {}, {}
    for n in BIG:
        w2 = wts[n].reshape(-1, wts[n].shape[-1])
        d_, m_, v_ = _adamw(w2, grads[n].reshape(w2.shape), mom[n].reshape(w2.shape), var[n].reshape(w2.shape),
                            "adamw_" + n)
        delta[n], new_m[n], new_v[n] = (t.reshape((1,) + wts[n].shape) for t in (d_, m_, v_))
        grads[n] = grads[n].reshape((1,) + wts[n].shape)
    ds, ms, vs_ = _adamw(small_w, total[N_BIG_ROWS:N_BIG_ROWS + SMALL_ROWS], _pack_small(mom), _pack_small(var),
                         "adamw_small")
    for dst, src in ((delta, ds), (new_m, ms), (new_v, vs_)):
        dst.update(_unpack_small(src))

    return (loss, grad_x[None], *[grads[n] for n in WEIGHTS], *[delta[n] for n in WEIGHTS],
            *[new_m[n] for n in WEIGHTS], *[new_v[n] for n in WEIGHTS])
```

```python
import functools
import math

import numpy as np
import jax
import jax.numpy as jnp
from jax import lax
from jax.experimental import pallas as pl
from jax.experimental.pallas import tpu as pltpu

F32 = jnp.float32
BF16 = jnp.bfloat16
SDS = jax.ShapeDtypeStruct
MESH = pl.DeviceIdType.MESH

D = 1024
EPS = 1e-6
D_FF = 2816
GLA_GATE_NORM = 16.0
GLA_C = 64
MEM_HEADS = 4
MEM_HD = 256
C_QG, C_KG, C_VG, C_GG, C_QS, C_KS, C_VS, C_LR = 0, 256, 512, 1024, 1536, 2048, 2560, 3072
DIN = 3088
DIN_P = 3200
TB = 512
SBQ = 256
VMEM_LIMIT = 56 * 1024 * 1024
HIGHEST = lax.Precision.HIGHEST

ADAM_LR, ADAM_B1, ADAM_B2, ADAM_EPS, ADAM_WD, ADAM_STEP = 0.001, 0.9, 0.999, 1e-08, 0.01, 10

BIG = ("w_in", "w_gk_up", "w_out", "w_mq", "w_mkv", "w_mo", "w_gate_up", "w_down")
SMALL = ("mix_norm_w", "b_gk", "gla_norm_w", "sb_norm_w", "xattn_norm_w", "mem_norm_w", "mq_norm_w",
         "mk_norm_w", "ffn_norm_w")
WEIGHTS = ("mix_norm_w", "w_in", "w_gk_up", "b_gk", "gla_norm_w", "sb_norm_w", "w_out", "xattn_norm_w",
           "mem_norm_w", "w_mq", "w_mkv", "mq_norm_w", "mk_norm_w", "w_mo", "ffn_norm_w", "w_gate_up", "w_down")
PACK_ROWS = 4224
PACK_HALF = PACK_ROWS // 2
SMALL_ROWS = 8


def _params(**kw):
    return pltpu.CompilerParams(vmem_limit_bytes=VMEM_LIMIT, **kw)


def _row(c, j=0):
    return pl.BlockSpec((TB, c), lambda i, j=j: (i, j))


def _const(shape):
    return pl.BlockSpec(shape, lambda i: (0,) * len(shape))


def _dot(a, b):
    return lax.dot_general(a, b, (((1,), (0,)), ((), ())), preferred_element_type=F32)


def _dot_nt(a, b):
    return lax.dot_general(a, b, (((1,), (1,)), ((), ())), preferred_element_type=F32)


def _dot_tn(a, b):
    return lax.dot_general(a, b, (((0,), (0,)), ((), ())), preferred_element_type=F32)


def _dot_nt_f32(a, b):
    return lax.dot_general(a, b, (((1,), (1,)), ((), ())), precision=HIGHEST, preferred_element_type=F32)


def _split3(x):
    h = x.astype(BF16)
    r = x - h.astype(F32)
    m = r.astype(BF16)
    l = (r - m.astype(F32)).astype(BF16)
    return h, m, l


def _dot_exact(x, ones_mat):
    h, m, l = _split3(x)
    return _dot(h, ones_mat) + _dot(m, ones_mat) + _dot(l, ones_mat)


def _dot_hilo(x, ones_mat):
    h = x.astype(BF16)
    l = (x - h.astype(F32)).astype(BF16)
    return _dot(h, ones_mat) + _dot(l, ones_mat)


def _softplus(z):
    return jnp.maximum(z, 0.0) + jnp.log1p(jnp.exp(-jnp.abs(z)))


def _rsqrt_ms(x):
    return lax.rsqrt(jnp.mean(x * x, axis=-1, keepdims=True) + EPS)


def _colsum8(x):
    r, c = x.shape
    return jnp.sum(x.reshape(r // 8, 8, c), axis=0)


def _matmul(a, b, *, mode, tm, tn, tk=None, res=None, out_dtype=F32, name):
    if mode == "tn":
        K, M = a.shape
    else:
        M, K = a.shape
    N = b.shape[0] if mode == "nt" else b.shape[1]
    tk = K if tk is None else tk
    assert M % tm == 0 and N % tn == 0 and K % tk == 0, (name, M, N, K, tm, tn, tk)
    nk = K // tk
    if mode == "tn":
        a_spec = pl.BlockSpec((tk, tm), lambda j, i, k: (k, i))
    else:
        a_spec = pl.BlockSpec((tm, tk), lambda j, i, k: (i, k))
    if mode == "nt":
        b_spec = pl.BlockSpec((tn, tk), lambda j, i, k: (j, k))
    else:
        b_spec = pl.BlockSpec((tk, tn), lambda j, i, k: (k, j))
    o_spec = pl.BlockSpec((tm, tn), lambda j, i, k: (i, j))
    dot = {"nn": _dot, "nt": _dot_nt, "tn": _dot_tn}[mode]
    has_res = res is not None

    def body(*refs):
        a_ref, b_ref = refs[0], refs[1]
        res_ref = refs[2] if has_res else None
        o_ref = refs[2 + has_res]
        p = dot(a_ref[...].astype(BF16), b_ref[...].astype(BF16))
        if nk == 1:
            if has_res:
                p = p + res_ref[...]
            o_ref[...] = p.astype(out_dtype)
        else:
            acc_ref = refs[3 + has_res]
            k = pl.program_id(2)

            @pl.when(k == 0)
            def _():
                acc_ref[...] = p

            @pl.when(k > 0)
            def _():
                acc_ref[...] += p

            @pl.when(k == nk - 1)
            def _():
                t = acc_ref[...]
                if has_res:
                    t = t + res_ref[...]
                o_ref[...] = t.astype(out_dtype)

    in_specs = [a_spec, b_spec] + ([o_spec] if has_res else [])
    args = (a, b) + ((res,) if has_res else ())
    return pl.pallas_call(
        body, name=name, grid=(N // tn, M // tm, nk), in_specs=in_specs, out_specs=o_spec,
        out_shape=SDS((M, N), out_dtype),
        scratch_shapes=[pltpu.VMEM((tm, tn), F32)] if nk > 1 else [],
        compiler_params=_params(dimension_semantics=("parallel", "parallel", "arbitrary")),
    )(*args)


def _norm_fwd(x, w, name):
    T, dm = x.shape

    def body(x_ref, w_ref, h_ref):
        xv = x_ref[...]
        h_ref[...] = (xv * _rsqrt_ms(xv) * w_ref[...]).astype(BF16)

    return pl.pallas_call(
        body, name=name, grid=(T // TB,), in_specs=[_row(dm), _const((1, dm))], out_specs=_row(dm),
        out_shape=SDS((T, dm), BF16), compiler_params=_params())(x, w)


def _norm_bwd(dy, x, w, dres, name):
    T, dm = x.shape

    def body(dy_ref, x_ref, w_ref, dres_ref, dx_ref, dw_ref):
        @pl.when(pl.program_id(0) == 0)
        def _():
            dw_ref[...] = jnp.zeros_like(dw_ref)

        xv = x_ref[...]
        r = _rsqrt_ms(xv)
        n = xv * r
        dyv = dy_ref[...]
        dn = dyv * w_ref[...]
        dx_ref[...] = dres_ref[...] + r * (dn - n * jnp.mean(dn * n, axis=-1, keepdims=True))
        dw_ref[...] += _colsum8(dyv * n)

    return pl.pallas_call(
        body, name=name, grid=(T // TB,),
        in_specs=[_row(dm), _row(dm), _const((1, dm)), _row(dm)],
        out_specs=[_row(dm), _const((8, dm))],
        out_shape=[SDS((T, dm), F32), SDS((8, dm), F32)], compiler_params=_params())(dy, x, w, dres)


def _proj_split(proj, wgk, bgk):
    T = proj.shape[0]

    def body(lr_ref, q_ref, k_ref, v_ref, wgk_ref, b_ref, gk_ref, qs_ref, ks_ref, vs_ref):
        u = _dot(lr_ref[...].astype(BF16), wgk_ref[...].astype(BF16)) + b_ref[...]
        gk_ref[...] = -_softplus(-u) / GLA_GATE_NORM
        qs_ref[...] = (q_ref[...] * 0.125).astype(BF16)
        ks_ref[...] = k_ref[...].astype(BF16)
        vs_ref[...] = v_ref[...].astype(BF16)

    return pl.pallas_call(
        body, name="proj_split", grid=(T // TB,),
        in_specs=[_row(128, C_LR // 128), _row(512, C_QS // 512), _row(512, C_KS // 512), _row(512, C_VS // 512),
                  _const((128, 256)), _const((1, 256))],
        out_specs=[_row(256), _row(512), _row(512), _row(512)],
        out_shape=[SDS((T, 256), F32), SDS((T, 512), BF16), SDS((T, 512), BF16), SDS((T, 512), BF16)],
        compiler_params=_params())(proj, proj, proj, proj, wgk, bgk)


def _dproj_assemble(proj, dq_g, dk_g, dv_g, dg_g, dq_s, dk_s, dv_s, dgk, wgk, bgk):
    T = proj.shape[0]

    def body(lr_ref, dqg_ref, dkg_ref, dvg_ref, dgg_ref, dqs_ref, dks_ref, dvs_ref, dgk_ref, wgk_ref, b_ref,
             dp_ref, dwgk_ref, dbgk_ref):
        @pl.when(pl.program_id(0) == 0)
        def _():
            dwgk_ref[...] = jnp.zeros_like(dwgk_ref)
            dbgk_ref[...] = jnp.zeros_like(dbgk_ref)

        lr = lr_ref[...].astype(BF16)
        wg = wgk_ref[...].astype(BF16)
        u = _dot(lr, wg) + b_ref[...]
        du = dgk_ref[...] * (jax.nn.sigmoid(-u) / GLA_GATE_NORM)
        dub = du.astype(BF16)
        dp_ref[:, C_QG:C_KG] = (dqg_ref[...] * 0.125).astype(BF16)
        dp_ref[:, C_KG:C_VG] = dkg_ref[...].astype(BF16)
        dp_ref[:, C_VG:C_GG] = dvg_ref[...].astype(BF16)
        dp_ref[:, C_GG:C_QS] = dgg_ref[...].astype(BF16)
        dp_ref[:, C_QS:C_KS] = (dqs_ref[...] * 0.125).astype(BF16)
        dp_ref[:, C_KS:C_VS] = dks_ref[...].astype(BF16)
        dp_ref[:, C_VS:C_LR] = dvs_ref[...].astype(BF16)
        dp_ref[:, C_LR:DIN_P] = _dot_nt(dub, wg).astype(BF16)
        dwgk_ref[...] += _dot_tn(lr, dub)
        dbgk_ref[...] += _colsum8(du)

    return pl.pallas_call(
        body, name="dproj_assemble", grid=(T // TB,),
        in_specs=[_row(128, C_LR // 128), _row(256), _row(256), _row(512), _row(512), _row(512), _row(512),
                  _row(512), _row(256), _const((128, 256)), _const((1, 256))],
        out_specs=[_row(DIN_P), _const((128, 256)), _const((8, 256))],
        out_shape=[SDS((T, DIN_P), BF16), SDS((128, 256), F32), SDS((8, 256), F32)],
        compiler_params=_params())(proj, dq_g, dk_g, dv_g, dg_g, dq_s, dk_s, dv_s, dgk, wgk, bgk)


def _group_ones(n, g):
    idx = np.arange(n) // g
    return jnp.asarray((idx[:, None] == idx[None, :]).astype(np.float32), dtype=BF16)


def _mix_cat(o_g, proj, o_s, wg512, ws512, bd64):
    T = o_g.shape[0]

    def body(og_ref, gg_ref, os_ref, wg_ref, ws_ref, bd_ref, cat_ref):
        og = og_ref[...]
        gg = gg_ref[...]
        s = gg * jax.nn.sigmoid(gg)
        for h in range(4):
            sl = slice(128 * h, 128 * (h + 1))
            x = og[:, sl]
            cat_ref[:, sl] = (x * _rsqrt_ms(x) * wg_ref[:, sl] * s[:, sl]).astype(BF16)
        osv = os_ref[...]
        ms = _dot_exact(osv * osv, bd_ref[...]) * (1.0 / 64.0)
        cat_ref[:, 512:1024] = (osv * lax.rsqrt(ms + EPS) * ws_ref[...]).astype(BF16)

    return pl.pallas_call(
        body, name="mix_cat", grid=(T // TB,),
        in_specs=[_row(512), _row(512, C_GG // 512), _row(512), _const((1, 512)), _const((1, 512)),
                  _const((512, 512))],
        out_specs=_row(1024), out_shape=SDS((T, 1024), BF16), compiler_params=_params())(
            o_g, proj, o_s, wg512, ws512, bd64)


def _mix_cat_bwd(dcat, o_g, proj, o_s, wg512, ws512, bd64):
    T = o_g.shape[0]

    def body(dc_ref, og_ref, gg_ref, os_ref, wg_ref, ws_ref, bd_ref, dog_ref, dgg_ref, dos_ref, dwg_ref, dws_ref):
        @pl.when(pl.program_id(0) == 0)
        def _():
            dwg_ref[...] = jnp.zeros_like(dwg_ref)
            dws_ref[...] = jnp.zeros_like(dws_ref)

        og = og_ref[...]
        gg = gg_ref[...]
        sg = jax.nn.sigmoid(gg)
        s = gg * sg
        ds = sg * (1.0 + gg * (1.0 - sg))
        for h in range(4):
            sl = slice(128 * h, 128 * (h + 1))
            x = og[:, sl]
            r = _rsqrt_ms(x)
            n = x * r
            w = wg_ref[:, sl]
            dc = dc_ref[:, sl]
            dy = dc * s[:, sl]
            dgg_ref[:, sl] = dc * (n * w) * ds[:, sl]
            dn = dy * w
            dog_ref[:, sl] = r * (dn - n * jnp.mean(dn * n, axis=-1, keepdims=True))
            dwg_ref[:, sl] += _colsum8(dy * n)
        osv = os_ref[...]
        bd = bd_ref[...]
        r = lax.rsqrt(_dot_exact(osv * osv, bd) * (1.0 / 64.0) + EPS)
        n = osv * r
        dc = dc_ref[:, 512:1024]
        dn = dc * ws_ref[...]
        dos_ref[...] = r * (dn - n * (_dot_exact(dn * n, bd) * (1.0 / 64.0)))
        dws_ref[...] += _colsum8(dc * n)

    return pl.pallas_call(
        body, name="mix_cat_bwd", grid=(T // TB,),
        in_specs=[_row(1024), _row(512), _row(512, C_GG // 512), _row(512), _const((1, 512)), _const((1, 512)),
                  _const((512, 512))],
        out_specs=[_row(512), _row(512), _row(512), _const((8, 512)), _const((8, 512))],
        out_shape=[SDS((T, 512), F32), SDS((T, 512), F32), SDS((T, 512), F32), SDS((8, 512), F32),
                   SDS((8, 512), F32)],
        compiler_params=_params())(dcat, o_g, proj, o_s, wg512, ws512, bd64)


def _swiglu_act(gu):
    T = gu.shape[0]

    def body(g_ref, u_ref, a_ref):
        g = g_ref[...]
        a_ref[...] = (g * jax.nn.sigmoid(g) * u_ref[...]).astype(BF16)

    return pl.pallas_call(
        body, name="swiglu_act", grid=(T // TB,), in_specs=[_row(D_FF, 0), _row(D_FF, 1)], out_specs=_row(D_FF),
        out_shape=SDS((T, D_FF), BF16), compiler_params=_params())(gu, gu)


def _swiglu_bwd(da, gu):
    T = gu.shape[0]

    def body(da_ref, g_ref, u_ref, dgu_ref):
        g = g_ref[...]
        sg = jax.nn.sigmoid(g)
        dav = da_ref[...]
        dgu_ref[:, :D_FF] = (dav * u_ref[...] * (sg * (1.0 + g * (1.0 - sg)))).astype(BF16)
        dgu_ref[:, D_FF:] = (dav * (g * sg)).astype(BF16)

    return pl.pallas_call(
        body, name="swiglu_bwd", grid=(T // TB,), in_specs=[_row(D_FF), _row(D_FF, 0), _row(D_FF, 1)],
        out_specs=_row(2 * D_FF), out_shape=SDS((T, 2 * D_FF), BF16), compiler_params=_params())(da, gu, gu)


def _loss_head(y, tgt):
    T = y.shape[0]

    def body(y_ref, t_ref, dy_ref, l_ref):
        @pl.when(pl.program_id(0) == 0)
        def _():
            l_ref[...] = jnp.zeros_like(l_ref)

        e = y_ref[...] - t_ref[...]
        dy_ref[...] = e * (1.0 / D)
        l_ref[...] += _colsum8(e * e) * (0.5 / D)

    return pl.pallas_call(
        body, name="loss_head", grid=(T // TB,), in_specs=[_row(D), _row(D)],
        out_specs=[_row(D), _const((8, D))], out_shape=[SDS((T, D), F32), SDS((8, D), F32)],
        compiler_params=_params())(y, tgt)


def _gla_consts():
    c = GLA_C
    L = np.tril(np.ones((c, c), np.float32))
    blocks = [L, L[(np.arange(c) // 16) * 16]]
    blocks += [np.repeat(L[16 * i:16 * i + 1], c, axis=0) for i in range(4)]
    blocks.append(np.repeat(L[c - 1:c], c, axis=0))
    return jnp.asarray(np.concatenate(blocks, axis=0))


@jax.custom_vjp
def _gla_lin(cm, g):
    cb = cm.astype(BF16)
    h, m, l = _split3(g)
    y = _dot(cb, h) + _dot(cb, m) + _dot(cb, l)
    return tuple(y[GLA_C * n:GLA_C * (n + 1)] for n in range(7))


def _gla_lin_fwd(cm, g):
    return _gla_lin(cm, g), cm


def _gla_lin_bwd(cm, cts):
    cb = cm.astype(BF16)
    h, m, l = _split3(jnp.concatenate(cts, axis=0))
    return None, _dot_tn(cb, h) + _dot_tn(cb, m) + _dot_tn(cb, l)


_gla_lin.defvjp(_gla_lin_fwd, _gla_lin_bwd)


def _gla_chunk_pair(cm, q, k, g, v0, v1, st):
    c = GLA_C
    lane = lax.broadcasted_iota(jnp.int32, (c, 128), 1)
    m0 = (lane < 64).astype(F32)
    m1 = 1.0 - m0
    row = lax.broadcasted_iota(jnp.int32, (c, 128), 0)
    ri = lax.broadcasted_iota(jnp.int32, (c, c), 0)
    ci = lax.broadcasted_iota(jnp.int32, (c, c), 1)
    b, r, r0, r1, r2, r3, bl = _gla_lin(cm, g)
    qs = q * jnp.exp(b - r)
    a0 = jnp.zeros((c, c), F32)
    a1 = jnp.zeros((c, c), F32)
    for blk, rb in enumerate((r0, r1, r2, r3)):
        keep = row < 16 * (blk + 1)
        ksb = jnp.where(keep, k * jnp.exp(jnp.where(keep, rb - b, 0.0)), 0.0)
        qb = jnp.where((row >= 16 * blk) & keep, qs, 0.0)
        a0 = a0 + _dot_nt_f32(qb * m0, ksb)
        a1 = a1 + _dot_nt_f32(qb * m1, ksb)
    causal = ci <= ri
    a0 = jnp.where(causal, a0, 0.0)
    a1 = jnp.where(causal, a1, 0.0)
    qe = q * jnp.exp(b)
    o0 = _dot_nt(qe * m0, st) + _dot(a0, v0)
    o1 = _dot_nt(qe * m1, st) + _dot(a1, v1)
    kd = k * jnp.exp(bl - b)
    m0s = jnp.concatenate([m0, m0], axis=0)
    decay = jnp.exp(jnp.concatenate([bl, bl], axis=0))
    st_new = st * decay + m0s * _dot_tn(v0, kd) + (1.0 - m0s) * _dot_tn(v1, kd)
    return o0, o1, st_new


GLA_TB = 512


def _gla_fwd(proj, gk, cm):
    T = proj.shape[0]
    nc = GLA_TB // GLA_C

    def body(q_ref, k_ref, v_ref, g_ref, cm_ref, o_ref, st_ref, st_scr):
        @pl.when(pl.program_id(0) == 0)
        def _():
            st_scr[...] = jnp.zeros_like(st_scr)

        cmv = cm_ref[...]

        def chunk(ci, carry):
            rs = pl.ds(pl.multiple_of(ci * GLA_C, GLA_C), GLA_C)
            for p in range(2):
                ls = slice(128 * p, 128 * (p + 1))
                st = st_scr[p]
                st_ref[ci, p] = st
                o0, o1, st_new = _gla_chunk_pair(
                    cmv, q_ref[rs, ls] * 0.125, k_ref[rs, ls], g_ref[rs, ls],
                    v_ref[rs, 256 * p:256 * p + 128], v_ref[rs, 256 * p + 128:256 * p + 256], st)
                o_ref[rs, 256 * p:256 * p + 128] = o0
                o_ref[rs, 256 * p + 128:256 * p + 256] = o1
                st_scr[p] = st_new
            return carry

        lax.fori_loop(0, nc, chunk, 0)

    return pl.pallas_call(
        body, name="gla_fwd", grid=(T // GLA_TB,),
        in_specs=[pl.BlockSpec((GLA_TB, 256), lambda i: (i, C_QG // 256)),
                  pl.BlockSpec((GLA_TB, 256), lambda i: (i, C_KG // 256)),
                  pl.BlockSpec((GLA_TB, 512), lambda i: (i, C_VG // 512)),
                  pl.BlockSpec((GLA_TB, 256), lambda i: (i, 0)),
                  pl.BlockSpec((7 * GLA_C, GLA_C), lambda i: (0, 0))],
        out_specs=[pl.BlockSpec((GLA_TB, 512), lambda i: (i, 0)),
                   pl.BlockSpec((nc, 2, 128, 128), lambda i: (i, 0, 0, 0))],
        out_shape=[SDS((T, 512), F32), SDS((T // GLA_C, 2, 128, 128), F32)],
        scratch_shapes=[pltpu.VMEM((2, 128, 128), F32)],
        compiler_params=_params(dimension_semantics=("arbitrary",)))(proj, proj, proj, gk, cm)


def _gla_bwd(proj, gk, cm, states, do):
    T = proj.shape[0]
    nb = T // GLA_TB
    nc = GLA_TB // GLA_C

    def body(q_ref, k_ref, v_ref, g_ref, cm_ref, st_ref, do_ref, dq_ref, dk_ref, dv_ref, dg_ref, dst_scr):
        @pl.when(pl.program_id(0) == 0)
        def _():
            dst_scr[...] = jnp.zeros_like(dst_scr)

        cmv = cm_ref[...]

        def chunk(t, carry):
            ci = nc - 1 - t
            rs = pl.ds(pl.multiple_of(ci * GLA_C, GLA_C), GLA_C)
            for p in range(2):
                ls = slice(128 * p, 128 * (p + 1))
                _, vjp = jax.vjp(
                    functools.partial(_gla_chunk_pair, cmv),
                    q_ref[rs, ls] * 0.125, k_ref[rs, ls], g_ref[rs, ls],
                    v_ref[rs, 256 * p:256 * p + 128], v_ref[rs, 256 * p + 128:256 * p + 256], st_ref[ci, p])
                dq, dk, dg, dv0, dv1, dst = vjp((do_ref[rs, 256 * p:256 * p + 128],
                                                 do_ref[rs, 256 * p + 128:256 * p + 256], dst_scr[p]))
                dq_ref[rs, ls] = dq
                dk_ref[rs, ls] = dk
                dg_ref[rs, ls] = dg
                dv_ref[rs, 256 * p:256 * p + 128] = dv0
                dv_ref[rs, 256 * p + 128:256 * p + 256] = dv1
                dst_scr[p] = dst
            return carry

        lax.fori_loop(0, nc, chunk, 0)

    rev = lambda i: nb - 1 - i
    return pl.pallas_call(
        body, name="gla_bwd", grid=(nb,),
        in_specs=[pl.BlockSpec((GLA_TB, 256), lambda i: (rev(i), C_QG // 256)),
                  pl.BlockSpec((GLA_TB, 256), lambda i: (rev(i), C_KG // 256)),
                  pl.BlockSpec((GLA_TB, 512), lambda i: (rev(i), C_VG // 512)),
                  pl.BlockSpec((GLA_TB, 256), lambda i: (rev(i), 0)),
                  pl.BlockSpec((7 * GLA_C, GLA_C), lambda i: (0, 0)),
                  pl.BlockSpec((nc, 2, 128, 128), lambda i: (rev(i), 0, 0, 0)),
                  pl.BlockSpec((GLA_TB, 512), lambda i: (rev(i), 0))],
        out_specs=[pl.BlockSpec((GLA_TB, 256), lambda i: (rev(i), 0)),
                   pl.BlockSpec((GLA_TB, 256), lambda i: (rev(i), 0)),
                   pl.BlockSpec((GLA_TB, 512), lambda i: (rev(i), 0)),
                   pl.BlockSpec((GLA_TB, 256), lambda i: (rev(i), 0))],
        out_shape=[SDS((T, 256), F32), SDS((T, 256), F32), SDS((T, 512), F32), SDS((T, 256), F32)],
        scratch_shapes=[pltpu.VMEM((2, 128, 128), F32)],
        compiler_params=_params(dimension_semantics=("arbitrary",)))(proj, proj, proj, gk, cm, states, do)


SB_DEAD = 105.0
SB_COUNT_LANE = 127


def _sb_tri():
    i = np.arange(SBQ)
    return jnp.asarray((i[:, None] > i[None, :]).astype(np.float32), dtype=BF16)


def _sb_block_fwd(qh, kb, tri, carry, strict):
    z = _dot_nt(qh, kb)
    sp = _softplus(z)
    l1 = -sp
    if strict is not None:
        l1 = jnp.where(strict, l1, 0.0)
    log_a = (z - sp) + _dot_hilo(l1, tri) + carry
    a = jnp.exp(log_a)
    if strict is not None:
        a = jnp.where(strict, a, 0.0)
    return z - sp, l1, a


def _sb_fwd(qs, ks, vs, tri):
    T = qs.shape[0]
    nq = T // SBQ

    def body(q_ref, k_ref, v_ref, tri_ref, o_ref, c_ref):
        i = pl.program_id(1)
        lane = lax.broadcasted_iota(jnp.int32, (1, 128), 1)
        clane = lax.broadcasted_iota(jnp.int32, (SBQ, 128), 1)
        strict = (lax.broadcasted_iota(jnp.int32, (SBQ, SBQ), 1) < lax.broadcasted_iota(jnp.int32, (SBQ, SBQ), 0))
        tri_v = tri_ref[...]
        qv = q_ref[...]
        first_head = lane < 64
        qhs = (jnp.where(first_head, qv, jnp.zeros_like(qv)), jnp.where(first_head, jnp.zeros_like(qv), qv))

        def block(j, carries, accs, masked):
            rs = pl.ds(pl.multiple_of(j * SBQ, SBQ), SBQ)
            kb = k_ref[rs, :]
            vb = v_ref[rs, :]
            out_c, out_a = [], []
            for hh in range(2):
                _, l1, a = _sb_block_fwd(qhs[hh], kb, tri_v, carries[hh], strict if masked else None)
                out_a.append(accs[hh] + _dot(a.astype(BF16), vb))
                out_c.append(carries[hh] + jnp.sum(l1, axis=1, keepdims=True))
            return out_c, out_a

        zero1 = jnp.zeros((SBQ, 1), F32)
        zero128 = jnp.zeros((SBQ, 128), F32)
        (c0, c1), (a0, a1) = block(i, (zero1, zero1), (zero128, zero128), True)

        def more(state):
            return (state[0] <= i) & (jnp.maximum(jnp.max(state[1]), jnp.max(state[2])) > -SB_DEAD)

        def step(state):
            jj, c0, c1, a0, a1, t0, t1 = state
            j = i - jj
            t0 = jnp.where(clane == j, c0, t0)
            t1 = jnp.where(clane == j, c1, t1)
            (c0, c1), (a0, a1) = block(j, (c0, c1), (a0, a1), False)
            return jj + 1, c0, c1, a0, a1, t0, t1

        jj, c0, c1, a0, a1, t0, t1 = lax.while_loop(
            more, step, (jnp.int32(1), c0, c1, a0, a1, zero128, zero128))
        o_ref[...] = jnp.where(first_head, a0, a1)
        swept = (jj - 1).astype(F32)
        c_ref[0, :, 0:128] = jnp.where(clane == SB_COUNT_LANE, swept, t0)
        c_ref[0, :, 128:256] = jnp.where(clane == SB_COUNT_LANE, swept, t1)

    return pl.pallas_call(
        body, name="sb_fwd", grid=(4, nq),
        in_specs=[pl.BlockSpec((SBQ, 128), lambda h, i: (i, h)),
                  pl.BlockSpec((T, 128), lambda h, i: (0, h)),
                  pl.BlockSpec((T, 128), lambda h, i: (0, h)),
                  pl.BlockSpec((SBQ, SBQ), lambda h, i: (0, 0))],
        out_specs=[pl.BlockSpec((SBQ, 128), lambda h, i: (i, h)),
                   pl.BlockSpec((1, SBQ, 256), lambda h, i: (h, i, 0))],
        out_shape=[SDS((T, 512), F32), SDS((4, T, 256), F32)],
        compiler_params=_params(dimension_semantics=("parallel", "arbitrary")))(qs, ks, vs, tri)


def _sb_bwd(qs, ks, vs, do, carries, tri, tri_t):
    T = qs.shape[0]
    nq = T // SBQ

    def body(q_ref, k_ref, v_ref, do_ref, c_ref, tri_ref, trit_ref, dq_ref, dk_ref, dv_ref):
        i = pl.program_id(1)

        @pl.when(i == 0)
        def _():
            dk_ref[...] = jnp.zeros_like(dk_ref)
            dv_ref[...] = jnp.zeros_like(dv_ref)

        lane = lax.broadcasted_iota(jnp.int32, (1, 128), 1)
        clane = lax.broadcasted_iota(jnp.int32, (SBQ, 128), 1)
        strict = (lax.broadcasted_iota(jnp.int32, (SBQ, SBQ), 1) < lax.broadcasted_iota(jnp.int32, (SBQ, SBQ), 0))
        tri_v = tri_ref[...]
        trit_v = trit_ref[...]
        qv = q_ref[...]
        dov = do_ref[...].astype(BF16)
        first_head = lane < 64
        qhs = (jnp.where(first_head, qv, jnp.zeros_like(qv)), jnp.where(first_head, jnp.zeros_like(qv), qv))
        dohs = (jnp.where(first_head, dov, jnp.zeros_like(dov)), jnp.where(first_head, jnp.zeros_like(dov), dov))
        cts = (c_ref[0, :, 0:128], c_ref[0, :, 128:256])

        def block(j, pcarries, dqs, masked):
            rs = pl.ds(pl.multiple_of(j * SBQ, SBQ), SBQ)
            kb = k_ref[rs, :]
            vb = v_ref[rs, :]
            out_p, out_q = [], []
            dk = jnp.zeros((SBQ, 128), F32)
            dv = jnp.zeros((SBQ, 128), F32)
            for hh in range(2):
                carry = jnp.sum(jnp.where(clane == j, cts[hh], 0.0), axis=1, keepdims=True)
                lb, _, a = _sb_block_fwd(qhs[hh], kb, tri_v, carry, strict if masked else None)
                g = a * _dot_nt(dohs[hh], vb)
                p = _dot_hilo(g, trit_v) + pcarries[hh]
                dz = g - (g + p) * jnp.exp(lb)
                if masked:
                    dz = jnp.where(strict, dz, 0.0)
                dzb = dz.astype(BF16)
                dk = dk + _dot_tn(dzb, qhs[hh])
                dv = dv + _dot_tn(a.astype(BF16), dohs[hh])
                out_p.append(pcarries[hh] + jnp.sum(g, axis=1, keepdims=True))
                out_q.append(dqs[hh] + _dot(dzb, kb))
            dk_ref[rs, :] += dk
            dv_ref[rs, :] += dv
            return out_p, out_q

        def step(j, state):
            (p0, p1), (q0, q1) = block(j, (state[0], state[1]), (state[2], state[3]), False)
            return p0, p1, q0, q1

        swept = jnp.max(jnp.where(clane == SB_COUNT_LANE, cts[0], 0.0)).astype(jnp.int32)
        first = i - jnp.clip(swept, 0, i)
        zero1 = jnp.zeros((SBQ, 1), F32)
        zero128 = jnp.zeros((SBQ, 128), F32)
        p0, p1, q0, q1 = lax.fori_loop(first, i, step, (zero1, zero1, zero128, zero128))
        _, (q0, q1) = block(i, (p0, p1), (q0, q1), True)
        dq_ref[...] = jnp.where(first_head, q0, q1)

    return pl.pallas_call(
        body, name="sb_bwd", grid=(4, nq),
        in_specs=[pl.BlockSpec((SBQ, 128), lambda h, i: (i, h)),
                  pl.BlockSpec((T, 128), lambda h, i: (0, h)),
                  pl.BlockSpec((T, 128), lambda h, i: (0, h)),
                  pl.BlockSpec((SBQ, 128), lambda h, i: (i, h)),
                  pl.BlockSpec((1, SBQ, 256), lambda h, i: (h, i, 0)),
                  pl.BlockSpec((SBQ, SBQ), lambda h, i: (0, 0)),
                  pl.BlockSpec((SBQ, SBQ), lambda h, i: (0, 0))],
        out_specs=[pl.BlockSpec((SBQ, 128), lambda h, i: (i, h)),
                   pl.BlockSpec((T, 128), lambda h, i: (0, h)),
                   pl.BlockSpec((T, 128), lambda h, i: (0, h))],
        out_shape=[SDS((T, 512), F32), SDS((T, 512), F32), SDS((T, 512), F32)],
        compiler_params=_params(dimension_semantics=("parallel", "arbitrary")))(qs, ks, vs, do, carries, tri, tri_t)


def _mem_fwd(mem, mem_norm_w, w_mkv, mk_norm_w):
    M = mem.shape[0]

    def body(mem_ref, wn_ref, w_ref, wk_ref, mn_ref, kraw_ref, k_ref, v_ref):
        mv = mem_ref[...]
        mn = (mv * _rsqrt_ms(mv) * wn_ref[...]).astype(BF16)
        mn_ref[...] = mn
        kv = _dot(mn, w_ref[...])
        kraw_ref[...] = kv[:, :D]
        v_ref[...] = kv[:, D:].astype(BF16)
        for h in range(MEM_HEADS):
            sl = slice(MEM_HD * h, MEM_HD * (h + 1))
            x = kv[:, sl]
            k_ref[:, sl] = (x * _rsqrt_ms(x) * wk_ref[...]).astype(BF16)

    vm = pl.BlockSpec(memory_space=pltpu.VMEM)
    return pl.pallas_call(
        body, name="mem_fwd", in_specs=[vm] * 4, out_specs=[vm] * 4,
        out_shape=[SDS((M, D), BF16), SDS((M, D), F32), SDS((M, D), BF16), SDS((M, D), BF16)],
        compiler_params=_params())(mem, mem_norm_w, w_mkv, mk_norm_w)


def _xattn_fwd(qraw, k, v, wq):
    T = qraw.shape[0]
    M = k.shape[0]

    def body(q_ref, k_ref, v_ref, wq_ref, o_ref):
        for h in range(MEM_HEADS):
            sl = slice(MEM_HD * h, MEM_HD * (h + 1))
            x = q_ref[:, sl]
            q = (x * _rsqrt_ms(x) * wq_ref[...]).astype(BF16)
            s = _dot_nt(q, k_ref[:, sl]) * (1.0 / math.sqrt(MEM_HD))
            s = s - jnp.max(s, axis=-1, keepdims=True)
            e = jnp.exp(s)
            p = e / jnp.sum(e, axis=-1, keepdims=True)
            o_ref[:, sl] = _dot(p.astype(BF16), v_ref[:, sl]).astype(BF16)

    return pl.pallas_call(
        body, name="xattn_fwd", grid=(T // TB,),
        in_specs=[_row(D), _const((M, D)), _const((M, D)), _const((1, MEM_HD))],
        out_specs=_row(D), out_shape=SDS((T, D), BF16), compiler_params=_params())(qraw, k, v, wq)


def _xattn_bwd(qraw, k, v, wq, do):
    T = qraw.shape[0]
    M = k.shape[0]

    def body(q_ref, k_ref, v_ref, wq_ref, do_ref, dq_ref, dk_ref, dv_ref, dw_ref):
        @pl.when(pl.program_id(0) == 0)
        def _():
            dk_ref[...] = jnp.zeros_like(dk_ref)
            dv_ref[...] = jnp.zeros_like(dv_ref)
            dw_ref[...] = jnp.zeros_like(dw_ref)

        w = wq_ref[...]
        for h in range(MEM_HEADS):
            sl = slice(MEM_HD * h, MEM_HD * (h + 1))
            x = q_ref[:, sl]
            r = _rsqrt_ms(x)
            n = x * r
            q = (n * w).astype(BF16)
            kb = k_ref[:, sl]
            s = _dot_nt(q, kb) * (1.0 / math.sqrt(MEM_HD))
            s = s - jnp.max(s, axis=-1, keepdims=True)
            e = jnp.exp(s)
            p = e / jnp.sum(e, axis=-1, keepdims=True)
            dob = do_ref[:, sl].astype(BF16)
            dp = _dot_nt(dob, v_ref[:, sl])
            ds = (p * (dp - jnp.sum(dp * p, axis=-1, keepdims=True)) * (1.0 / math.sqrt(MEM_HD))).astype(BF16)
            dv_ref[:, sl] += _dot_tn(p.astype(BF16), dob)
            dk_ref[:, sl] += _dot_tn(ds, q)
            dqn = _dot(ds, kb)
            dn = dqn * w
            dq_ref[:, sl] = r * (dn - n * jnp.mean(dn * n, axis=-1, keepdims=True))
            dw_ref[...] += _colsum8(dqn * n)

    return pl.pallas_call(
        body, name="xattn_bwd", grid=(T // TB,),
        in_specs=[_row(D), _const((M, D)), _const((M, D)), _const((1, MEM_HD)), _row(D)],
        out_specs=[_row(D), _const((M, D)), _const((M, D)), _const((8, MEM_HD))],
        out_shape=[SDS((T, D), F32), SDS((M, D), F32), SDS((M, D), F32), SDS((8, MEM_HD), F32)],
        compiler_params=_params())(qraw, k, v, wq, do)


def _mem_bwd(mem, mem_norm_w, w_mkv, mk_norm_w, mem_n, k_raw, dk, dv):
    M = mem.shape[0]

    def body(mem_ref, wn_ref, w_ref, wk_ref, mn_ref, kraw_ref, dk_ref, dv_ref, dw_ref, dwn_ref, dwk_ref, dkv_scr):
        wk = wk_ref[...]
        dwk = jnp.zeros((8, MEM_HD), F32)
        for h in range(MEM_HEADS):
            sl = slice(MEM_HD * h, MEM_HD * (h + 1))
            x = kraw_ref[:, sl]
            r = _rsqrt_ms(x)
            n = x * r
            dkh = dk_ref[:, sl]
            dn = dkh * wk
            dkv_scr[:, sl] = (r * (dn - n * jnp.mean(dn * n, axis=-1, keepdims=True))).astype(BF16)
            dwk = dwk + _colsum8(dkh * n)
        dwk_ref[...] = dwk
        dkv_scr[:, D:] = dv_ref[...].astype(BF16)
        dkv = dkv_scr[...]
        dw_ref[...] = _dot_tn(mn_ref[...], dkv)
        dmn = _dot_nt(dkv, w_ref[...])
        mv = mem_ref[...]
        dwn_ref[...] = _colsum8(dmn * (mv * _rsqrt_ms(mv)))

    vm = pl.BlockSpec(memory_space=pltpu.VMEM)
    return pl.pallas_call(
        body, name="mem_bwd", in_specs=[vm] * 8, out_specs=[vm] * 3,
        out_shape=[SDS((D, 2 * D), F32), SDS((8, D), F32), SDS((8, MEM_HD), F32)],
        scratch_shapes=[pltpu.VMEM((M, 2 * D), BF16)],
        compiler_params=_params())(mem, mem_norm_w, w_mkv, mk_norm_w, mem_n, k_raw, dk, dv)


def _local_step(x, mem, tgt, w):
    wgk = jnp.zeros((128, 256), F32).at[:16].set(w["w_gk_up"].astype(F32))
    wg512 = jnp.tile(w["gla_norm_w"], (1, 4))
    ws512 = jnp.tile(w["sb_norm_w"], (1, 8))
    bd64 = _group_ones(512, 64)
    cm = _gla_consts()
    tri = _sb_tri()
    tri_t = tri.T

    h1 = _norm_fwd(x, w["mix_norm_w"], "norm1_fwd")
    proj = _matmul(h1, w["w_in"], mode="nn", tm=TB, tn=DIN_P, name="mm_proj")
    gk, qs, ks, vs = _proj_split(proj, wgk, w["b_gk"])
    o_g, states = _gla_fwd(proj, gk, cm)
    o_s, carries = _sb_fwd(qs, ks, vs, tri)
    cat = _mix_cat(o_g, proj, o_s, wg512, ws512, bd64)
    x1 = _matmul(cat, w["w_out"], mode="nn", tm=TB, tn=D, res=x, name="mm_out")
    h2 = _norm_fwd(x1, w["xattn_norm_w"], "norm2_fwd")
    qraw = _matmul(h2, w["w_mq"], mode="nn", tm=TB, tn=D, name="mm_mq")
    mem_n, k_raw, k_n, v_m = _mem_fwd(mem, w["mem_norm_w"], w["w_mkv"], w["mk_norm_w"])
    om = _xattn_fwd(qraw, k_n, v_m, w["mq_norm_w"])
    x2 = _matmul(om, w["w_mo"], mode="nn", tm=TB, tn=D, res=x1, name="mm_mo")
    h3 = _norm_fwd(x2, w["ffn_norm_w"], "norm3_fwd")
    gu = _matmul(h3, w["w_gate_up"], mode="nn", tm=TB, tn=D_FF, name="mm_gate_up")
    act = _swiglu_act(gu)
    x3 = _matmul(act, w["w_down"], mode="nn", tm=TB, tn=D, res=x2, name="mm_down")
    dx3, loss_rows = _loss_head(x3, tgt)

    g = {}
    g["w_down"] = _matmul(act, dx3, mode="tn", tm=1408, tn=D, tk=TB, name="mm_dw_down")
    da = _matmul(dx3, w["w_down"], mode="nt", tm=TB, tn=1408, name="mm_da")
    dgu = _swiglu_bwd(da, gu)
    g["w_gate_up"] = _matmul(h3, dgu, mode="tn", tm=D, tn=1408, tk=TB, name="mm_dw_gate_up")
    dh3 = _matmul(dgu, w["w_gate_up"], mode="nt", tm=TB, tn=D, name="mm_dh3")
    dx2, g["ffn_norm_w"] = _norm_bwd(dh3, x2, w["ffn_norm_w"], dx3, "norm3_bwd")
    g["w_mo"] = _matmul(om, dx2, mode="tn", tm=D, tn=D, tk=TB, name="mm_dw_mo")
    dom = _matmul(dx2, w["w_mo"], mode="nt", tm=TB, tn=D, name="mm_dom")
    dqraw, dk_n, dv_m, g["mq_norm_w"] = _xattn_bwd(qraw, k_n, v_m, w["mq_norm_w"], dom)
    g["w_mkv"], g["mem_norm_w"], g["mk_norm_w"] = _mem_bwd(
        mem, w["mem_norm_w"], w["w_mkv"], w["mk_norm_w"], mem_n, k_raw, dk_n, dv_m)
    g["w_mq"] = _matmul(h2, dqraw, mode="tn", tm=D, tn=D, tk=TB, name="mm_dw_mq")
    dh2 = _matmul(dqraw, w["w_mq"], mode="nt", tm=TB, tn=D, name="mm_dh2")
    dx1, g["xattn_norm_w"] = _norm_bwd(dh2, x1, w["xattn_norm_w"], dx2, "norm2_bwd")
    g["w_out"] = _matmul(cat, dx1, mode="tn", tm=D, tn=D, tk=TB, name="mm_dw_out")
    dcat = _matmul(dx1, w["w_out"], mode="nt", tm=TB, tn=D, name="mm_dcat")
    do_g, dg_g, do_s, dwg, dws = _mix_cat_bwd(dcat, o_g, proj, o_s, wg512, ws512, bd64)
    dq_s, dk_s, dv_s = _sb_bwd(qs, ks, vs, do_s, carries, tri, tri_t)
    dq_g, dk_g, dv_g, dgk = _gla_bwd(proj, gk, cm, states, do_g)
    dproj, dwgk, g["b_gk"] = _dproj_assemble(proj, dq_g, dk_g, dv_g, dg_g, dq_s, dk_s, dv_s, dgk, wgk, w["b_gk"])
    g["w_in"] = _matmul(h1, dproj, mode="tn", tm=D, tn=640, tk=TB, name="mm_dw_in")
    dh1 = _matmul(dproj, w["w_in"], mode="nt", tm=TB, tn=D, name="mm_dh1")
    grad_x, g["mix_norm_w"] = _norm_bwd(dh1, x, w["mix_norm_w"], dx1, "norm1_bwd")

    g["w_gk_up"] = dwgk[:16]
    g["gla_norm_w"] = dwg.reshape(8, 4, 128).sum(axis=1)
    g["sb_norm_w"] = dws.reshape(8, 8, 64).sum(axis=1)
    for n in SMALL:
        g[n] = jnp.sum(g[n], axis=0, keepdims=True)
    return jnp.sum(loss_rows), grad_x, g


def _mesh_pos():
    return lax.axis_index("x"), lax.axis_index("y"), lax.axis_index("c")


def _other_chips(x, y):
    return [(1 - x, y), (x, 1 - y), (1 - x, 1 - y)]


def _ag_weights(wp):
    def body(w_ref, out_ref, send_sems, recv_sems):
        x, y, c = _mesh_pos()
        me = 2 * x + y
        sibling = (x, y, 1 - c)
        chips = _other_chips(x, y)
        mine, theirs = c, 1 - c

        def copy(k, src, dst, to):
            return pltpu.make_async_remote_copy(src_ref=src, dst_ref=dst, send_sem=send_sems.at[k],
                                                recv_sem=recv_sems.at[k], device_id=to, device_id_type=MESH)

        first = [copy(k, w_ref.at[mine], out_ref.at[me, mine], (cx, cy, c)) for k, (cx, cy) in enumerate(chips)]
        for cp in first:
            cp.start()
        passed = []
        for k, (cx, cy) in enumerate(chips):
            slot = out_ref.at[2 * cx + cy, mine]
            copy(k, slot, slot, (cx, cy, c)).wait_recv()
            fwd = copy(3 + k, slot, slot, sibling)
            fwd.start()
            passed.append(fwd)
        for k, (cx, cy) in enumerate(chips):
            slot = out_ref.at[2 * cx + cy, theirs]
            copy(3 + k, slot, slot, sibling).wait_recv()
        for cp in first + passed:
            cp.wait_send()

    hbm = pl.BlockSpec(memory_space=pl.ANY)
    return pl.pallas_call(
        body, name="ag_weights", in_specs=[hbm], out_specs=hbm,
        out_shape=SDS((4, 2, PACK_HALF, 1024), BF16),
        scratch_shapes=[pltpu.SemaphoreType.DMA((6,)), pltpu.SemaphoreType.DMA((6,))],
        compiler_params=pltpu.CompilerParams(has_side_effects=True))(wp)


def _rs_swap_halves(gp):
    def body(g_ref, out_ref, send_sem, recv_sem):
        x, y, c = _mesh_pos()
        cp = pltpu.make_async_remote_copy(
            src_ref=g_ref.at[:, 1 - c], dst_ref=out_ref, send_sem=send_sem, recv_sem=recv_sem,
            device_id=(x, y, 1 - c), device_id_type=MESH)
        cp.start()
        cp.wait()

    hbm = pl.BlockSpec(memory_space=pl.ANY)
    return pl.pallas_call(
        body, name="rs_swap_halves", in_specs=[hbm], out_specs=hbm,
        out_shape=SDS((4, PACK_HALF, 1024), BF16),
        scratch_shapes=[pltpu.SemaphoreType.DMA, pltpu.SemaphoreType.DMA],
        compiler_params=pltpu.CompilerParams(has_side_effects=True))(gp)


RS_TR = 704


def _rs_add_halves(gp, other, c_arr):
    nt = PACK_HALF // RS_TR

    def body(c_ref, a_ref, b_ref, o_ref):
        o_ref[...] = (a_ref[0].astype(F32) + b_ref[...].astype(F32)).astype(BF16)

    return pl.pallas_call(
        body, name="rs_add_halves",
        grid_spec=pltpu.PrefetchScalarGridSpec(
            num_scalar_prefetch=1, grid=(4, nt),
            in_specs=[pl.BlockSpec((1, 1, RS_TR, 1024), lambda s, t, c: (s, c[0], t, 0)),
                      pl.BlockSpec((1, RS_TR, 1024), lambda s, t, c: (s, t, 0))],
            out_specs=pl.BlockSpec((1, RS_TR, 1024), lambda s, t, c: (s, t, 0))),
        out_shape=SDS((4, PACK_HALF, 1024), BF16), compiler_params=_params())(c_arr, gp, other)


def _rs_exchange(part):
    def body(p_ref, out_ref, send_sems, recv_sems):
        x, y, c = _mesh_pos()
        me = 2 * x + y
        chips = _other_chips(x, y)
        sends = []
        for k, (cx, cy) in enumerate(chips):
            cp = pltpu.make_async_remote_copy(
                src_ref=p_ref.at[2 * cx + cy], dst_ref=out_ref.at[me], send_sem=send_sems.at[k],
                recv_sem=recv_sems.at[k], device_id=(cx, cy, c), device_id_type=MESH)
            cp.start()
            sends.append(cp)
        for k, (cx, cy) in enumerate(chips):
            slot = out_ref.at[2 * cx + cy]
            pltpu.make_async_remote_copy(
                src_ref=slot, dst_ref=slot, send_sem=send_sems.at[k], recv_sem=recv_sems.at[k],
                device_id=(cx, cy, c), device_id_type=MESH).wait_recv()
        for cp in sends:
            cp.wait_send()

    hbm = pl.BlockSpec(memory_space=pl.ANY)
    return pl.pallas_call(
        body, name="rs_exchange", in_specs=[hbm], out_specs=hbm,
        out_shape=SDS((4, PACK_HALF, 1024), BF16),
        scratch_shapes=[pltpu.SemaphoreType.DMA((3,)), pltpu.SemaphoreType.DMA((3,))],
        compiler_params=pltpu.CompilerParams(has_side_effects=True))(part)


def _rs_add_chips(recv, part, me_arr):
    def body(me_ref, r_ref, p_ref, o_ref):
        me = me_ref[0]
        total = None
        for k in range(4):
            term = jnp.where(me == k, p_ref[k], r_ref[k]).astype(F32)
            total = term if total is None else total + term
        o_ref[...] = total

    spec = pl.BlockSpec((4, RS_TR, 1024), lambda t, me: (0, t, 0))
    return pl.pallas_call(
        body, name="rs_add_chips",
        grid_spec=pltpu.PrefetchScalarGridSpec(
            num_scalar_prefetch=1, grid=(PACK_HALF // RS_TR,), in_specs=[spec, spec],
            out_specs=pl.BlockSpec((RS_TR, 1024), lambda t, me: (t, 0))),
        out_shape=SDS((PACK_HALF, 1024), F32), compiler_params=_params())(me_arr, recv, part)


def _rs_share(half_sum):
    def body(h_ref, out_ref, send_sem, recv_sem):
        x, y, c = _mesh_pos()
        cp = pltpu.make_async_remote_copy(
            src_ref=h_ref, dst_ref=out_ref, send_sem=send_sem, recv_sem=recv_sem,
            device_id=(x, y, 1 - c), device_id_type=MESH)
        cp.start()
        cp.wait()

    hbm = pl.BlockSpec(memory_space=pl.ANY)
    return pl.pallas_call(
        body, name="rs_share", in_specs=[hbm], out_specs=hbm,
        out_shape=SDS((PACK_HALF, 1024), F32),
        scratch_shapes=[pltpu.SemaphoreType.DMA, pltpu.SemaphoreType.DMA],
        compiler_params=pltpu.CompilerParams(has_side_effects=True))(half_sum)


def _allreduce_small(s):
    def gather(s_ref, out_ref, send_sems, recv_sems, local_sem):
        x, y, c = _mesh_pos()
        me = 4 * x + 2 * y + c
        local = pltpu.make_async_copy(s_ref, out_ref.at[me], local_sem)
        local.start()
        peers = []
        for r in range(1, 8):
            px = 1 - x if r & 4 else x
            py = 1 - y if r & 2 else y
            pc = 1 - c if r & 1 else c
            peers.append((px, py, pc))
        sends = []
        for k, peer in enumerate(peers):
            cp = pltpu.make_async_remote_copy(
                src_ref=s_ref, dst_ref=out_ref.at[me], send_sem=send_sems.at[k], recv_sem=recv_sems.at[k],
                device_id=peer, device_id_type=MESH)
            cp.start()
            sends.append(cp)
        for k, (px, py, pc) in enumerate(peers):
            slot = out_ref.at[4 * px + 2 * py + pc]
            pltpu.make_async_remote_copy(
                src_ref=slot, dst_ref=slot, send_sem=send_sems.at[k], recv_sem=recv_sems.at[k],
                device_id=(px, py, pc), device_id_type=MESH).wait_recv()
        for cp in sends:
            cp.wait_send()
        local.wait()

    hbm = pl.BlockSpec(memory_space=pl.ANY)
    parts = pl.pallas_call(
        gather, name="small_gather", in_specs=[hbm], out_specs=hbm,
        out_shape=SDS((8, SMALL_ROWS, 1024), F32),
        scratch_shapes=[pltpu.SemaphoreType.DMA((7,)), pltpu.SemaphoreType.DMA((7,)), pltpu.SemaphoreType.DMA],
        compiler_params=pltpu.CompilerParams(has_side_effects=True))(s)

    def add(p_ref, o_ref):
        total = p_ref[0]
        for k in range(1, 8):
            total = total + p_ref[k]
        o_ref[...] = total

    vm = pl.BlockSpec(memory_space=pltpu.VMEM)
    return pl.pallas_call(add, name="small_sum", in_specs=[vm], out_specs=vm,
                          out_shape=SDS((SMALL_ROWS, 1024), F32))(parts)


def _adamw(w, g, m, v, name):
    rows, cols = w.shape
    tr = rows
    for cand in (512, 352, 256):
        if rows > cand and rows % cand == 0:
            tr = cand
            break
    c1 = 1.0 - ADAM_B1 ** ADAM_STEP
    c2 = 1.0 - ADAM_B2 ** ADAM_STEP

    def body(w_ref, g_ref, m_ref, v_ref, d_ref, mo_ref, vo_ref):
        gv = g_ref[...]
        mn = ADAM_B1 * m_ref[...] + (1.0 - ADAM_B1) * gv
        vn = ADAM_B2 * v_ref[...] + (1.0 - ADAM_B2) * (gv * gv)
        mo_ref[...] = mn
        vo_ref[...] = vn
        d_ref[...] = -ADAM_LR * ((mn / c1) / (jnp.sqrt(vn / c2) + ADAM_EPS) + ADAM_WD * w_ref[...])

    spec = pl.BlockSpec((tr, cols), lambda i: (i, 0))
    return pl.pallas_call(
        body, name=name, grid=(rows // tr,), in_specs=[spec] * 4, out_specs=[spec] * 3,
        out_shape=[SDS((rows, cols), F32)] * 3, compiler_params=_params())(w, g, m, v)


BIG_SHARD_ROWS = {"w_in": 772, "w_gk_up": 1, "w_out": 256, "w_mq": 256, "w_mkv": 512, "w_mo": 256,
                  "w_gate_up": 1408, "w_down": 704}
SMALL_SIZES = {"mix_norm_w": 1024, "b_gk": 256, "gla_norm_w": 128, "sb_norm_w": 64, "xattn_norm_w": 1024,
               "mem_norm_w": 1024, "mq_norm_w": 256, "mk_norm_w": 256, "ffn_norm_w": 1024}
N_BIG_ROWS = sum(BIG_SHARD_ROWS.values())


def _pack_small(d):
    flat = jnp.concatenate([d[n].reshape(-1) for n in SMALL])
    return jnp.pad(flat, (0, SMALL_ROWS * 1024 - flat.shape[0])).reshape(SMALL_ROWS, 1024)


def _unpack_small(p):
    flat = p.reshape(-1)
    out, off = {}, 0
    for n in SMALL:
        out[n] = flat[off:off + SMALL_SIZES[n]].reshape(1, SMALL_SIZES[n])
        off += SMALL_SIZES[n]
    return out


def _pack_shards(d, dtype):
    rows = [d[n].astype(dtype).reshape(BIG_SHARD_ROWS[n], 1024) for n in BIG]
    pad = jnp.zeros((PACK_ROWS - N_BIG_ROWS, 1024), dtype)
    return jnp.concatenate(rows + [pad], axis=0)


def _big_offsets():
    off, out = 0, {}
    for n in BIG:
        out[n] = off
        off += BIG_SHARD_ROWS[n]
    return out


def _unpack_full(allw):
    off = _big_offsets()

    def seg(n):
        return allw[:, off[n]:off[n] + BIG_SHARD_ROWS[n]]

    def cols(n, k, c):
        return seg(n).reshape(4, k, c).transpose(1, 0, 2).reshape(k, 4 * c)

    def rows(n, c):
        return seg(n).reshape(-1, c)

    w_in = cols("w_in", 1024, 772)
    w_in = jnp.concatenate([w_in[:, :1536], w_in[:, 1552:], w_in[:, 1536:1552],
                            jnp.zeros((1024, DIN_P - DIN), w_in.dtype)], axis=1)
    return {"w_in": w_in, "w_gk_up": cols("w_gk_up", 16, 64), "w_out": rows("w_out", 1024),
            "w_mq": rows("w_mq", 1024), "w_mkv": cols("w_mkv", 1024, 512), "w_mo": rows("w_mo", 1024),
            "w_gate_up": cols("w_gate_up", 1024, 1408), "w_down": rows("w_down", 1024)}


def _pack_grads(g):
    def cols(a, k, c):
        return a.astype(BF16).reshape(k, 4, c).transpose(1, 0, 2).reshape(4, -1, 1024)

    def rows(a):
        return a.astype(BF16).reshape(4, -1, 1024)

    gi = g["w_in"]
    gi = jnp.concatenate([gi[:, :1536], gi[:, C_LR:C_LR + 16], gi[:, 1536:C_LR]], axis=1)
    parts = [cols(gi, 1024, 772), cols(g["w_gk_up"], 16, 64), rows(g["w_out"]), rows(g["w_mq"]),
             cols(g["w_mkv"], 1024, 512), rows(g["w_mo"]), cols(g["w_gate_up"], 1024, 1408), rows(g["w_down"])]
    pad = jnp.zeros((4, PACK_ROWS - N_BIG_ROWS, 1024), BF16)
    return jnp.concatenate(parts + [pad], axis=1)


def kernel(x, mem, mix_norm_w, w_in, w_gk_up, b_gk, gla_norm_w, sb_norm_w, w_out, xattn_norm_w, mem_norm_w, w_mq, w_mkv, mq_norm_w, mk_norm_w, w_mo, ffn_norm_w, w_gate_up, w_down, loss_target, m_mix_norm_w, m_w_in, m_w_gk_up, m_b_gk, m_gla_norm_w, m_sb_norm_w, m_w_out, m_xattn_norm_w, m_mem_norm_w, m_w_mq, m_w_mkv, m_mq_norm_w, m_mk_norm_w, m_w_mo, m_ffn_norm_w, m_w_gate_up, m_w_down, v_mix_norm_w, v_w_in, v_w_gk_up, v_b_gk, v_gla_norm_w, v_sb_norm_w, v_w_out, v_xattn_norm_w, v_mem_norm_w, v_w_mq, v_w_mkv, v_mq_norm_w, v_mk_norm_w, v_w_mo, v_ffn_norm_w, v_w_gate_up, v_w_down):
    args = dict(locals())
    wts = {n: args[n][0] if n in BIG else args[n] for n in WEIGHTS}
    mom = {n: args["m_" + n][0] if n in BIG else args["m_" + n] for n in WEIGHTS}
    var = {n: args["v_" + n][0] if n in BIG else args["v_" + n] for n in WEIGHTS}

    c = lax.axis_index("c")
    chip = 2 * lax.axis_index("x") + lax.axis_index("y")
    wp = _pack_shards(wts, BF16).reshape(2, PACK_HALF, 1024)
    allw = lax.dynamic_update_slice(_ag_weights(wp), wp[None], (chip, 0, 0, 0))
    full = _unpack_full(allw.reshape(4, PACK_ROWS, 1024))
    full.update({n: wts[n] for n in SMALL})

    loss, grad_x, g = _local_step(x[0], mem[0], loss_target[0], full)
    loss = lax.psum(loss, ("x", "y", "c"))

    gp = _pack_grads(g).reshape(4, 2, PACK_HALF, 1024)
    part = _rs_add_halves(gp, _rs_swap_halves(gp), c.astype(jnp.int32).reshape(1))
    mine = _rs_add_chips(_rs_exchange(part), part, chip.astype(jnp.int32).reshape(1))
    theirs = _rs_share(mine)
    total = jnp.concatenate([jnp.where(c == 0, mine, theirs), jnp.where(c == 0, theirs, mine)], axis=0)
    small_g = _allreduce_small(_pack_small(g))

    off = _big_offsets()
    grads = {n: total[off[n]:off[n] + BIG_SHARD_ROWS[n]].reshape(wts[n].shape) for n in BIG}
    grads.update(_unpack_small(small_g))

    delta, new_m, new_v = {}, {}, {}
    for n in BIG:
        w2 = wts[n].reshape(-1, wts[n].shape[-1])
        d_, m_, v_ = _adamw(w2, grads[n].reshape(w2.shape), mom[n].reshape(w2.shape), var[n].reshape(w2.shape),
                            "adamw_" + n)
        delta[n], new_m[n], new_v[n] = (t.reshape((1,) + wts[n].shape) for t in (d_, m_, v_))
        grads[n] = grads[n].reshape((1,) + wts[n].shape)
    ds, ms, vs_ = _adamw(_pack_small(wts), small_g, _pack_small(mom), _pack_small(var), "adamw_small")
    for dst, src in ((delta, ds), (new_m, ms), (new_v, vs_)):
        dst.update(_unpack_small(src))

    return (loss, grad_x[None], *[grads[n] for n in WEIGHTS], *[delta[n] for n in WEIGHTS],
            *[new_m[n] for n in WEIGHTS], *[new_v[n] for n in WEIGHTS])
```

```python
import functools
import math

import numpy as np
import jax
import jax.numpy as jnp
from jax import lax
from jax.experimental import pallas as pl
from jax.experimental.pallas import tpu as pltpu

F32 = jnp.float32
BF16 = jnp.bfloat16
SDS = jax.ShapeDtypeStruct
MESH = pl.DeviceIdType.MESH

D = 1024
EPS = 1e-6
D_FF = 2816
GLA_GATE_NORM = 16.0
GLA_C = 64
MEM_HEADS = 4
MEM_HD = 256
C_QG, C_KG, C_VG, C_GG, C_QS, C_KS, C_VS, C_LR = 0, 256, 512, 1024, 1536, 2048, 2560, 3072
DIN = 3088
DIN_P = 3200
TB = 512
SBQ = 256
VMEM_LIMIT = 56 * 1024 * 1024
HIGHEST = lax.Precision.HIGHEST

ADAM_LR, ADAM_B1, ADAM_B2, ADAM_EPS, ADAM_WD, ADAM_STEP = 0.001, 0.9, 0.999, 1e-08, 0.01, 10

BIG = ("w_in", "w_gk_up", "w_out", "w_mq", "w_mkv", "w_mo", "w_gate_up", "w_down")
SMALL = ("mix_norm_w", "b_gk", "gla_norm_w", "sb_norm_w", "xattn_norm_w", "mem_norm_w", "mq_norm_w",
         "mk_norm_w", "ffn_norm_w")
WEIGHTS = ("mix_norm_w", "w_in", "w_gk_up", "b_gk", "gla_norm_w", "sb_norm_w", "w_out", "xattn_norm_w",
           "mem_norm_w", "w_mq", "w_mkv", "mq_norm_w", "mk_norm_w", "w_mo", "ffn_norm_w", "w_gate_up", "w_down")
PACK_ROWS = 4224
PACK_HALF = PACK_ROWS // 2
SMALL_ROWS = 8


def _params(**kw):
    return pltpu.CompilerParams(vmem_limit_bytes=VMEM_LIMIT, **kw)


def _row(c, j=0):
    return pl.BlockSpec((TB, c), lambda i, j=j: (i, j))


def _const(shape):
    return pl.BlockSpec(shape, lambda i: (0,) * len(shape))


def _dot(a, b):
    return lax.dot_general(a, b, (((1,), (0,)), ((), ())), preferred_element_type=F32)


def _dot_nt(a, b):
    return lax.dot_general(a, b, (((1,), (1,)), ((), ())), preferred_element_type=F32)


def _dot_tn(a, b):
    return lax.dot_general(a, b, (((0,), (0,)), ((), ())), preferred_element_type=F32)


def _dot_nt_f32(a, b):
    return lax.dot_general(a, b, (((1,), (1,)), ((), ())), precision=HIGHEST, preferred_element_type=F32)


def _split3(x):
    h = x.astype(BF16)
    r = x - h.astype(F32)
    m = r.astype(BF16)
    l = (r - m.astype(F32)).astype(BF16)
    return h, m, l


def _dot_exact(x, ones_mat):
    h, m, l = _split3(x)
    return _dot(h, ones_mat) + _dot(m, ones_mat) + _dot(l, ones_mat)


def _dot_hilo(x, ones_mat):
    h = x.astype(BF16)
    l = (x - h.astype(F32)).astype(BF16)
    return _dot(h, ones_mat) + _dot(l, ones_mat)


def _softplus(z):
    return jnp.maximum(z, 0.0) + jnp.log1p(jnp.exp(-jnp.abs(z)))


def _rsqrt_ms(x):
    return lax.rsqrt(jnp.mean(x * x, axis=-1, keepdims=True) + EPS)


def _colsum8(x):
    r, c = x.shape
    return jnp.sum(x.reshape(r // 8, 8, c), axis=0)


def _matmul(a, b, *, mode, tm, tn, tk=None, res=None, out_dtype=F32, by_column_tile=False, a_spec=None,
            b_spec=None, mnk=None, name):
    if mnk is not None:
        M, N, K = mnk
    else:
        K, M = a.shape if mode == "tn" else a.shape[::-1]
        N = b.shape[0] if mode == "nt" else b.shape[1]
    tk = K if tk is None else tk
    assert M % tm == 0 and N % tn == 0 and K % tk == 0, (name, M, N, K, tm, tn, tk)
    nk = K // tk
    if a_spec is None:
        if mode == "tn":
            a_spec = pl.BlockSpec((tk, tm), lambda j, i, k: (k, i))
        else:
            a_spec = pl.BlockSpec((tm, tk), lambda j, i, k: (i, k))
    if b_spec is None:
        if mode == "nt":
            b_spec = pl.BlockSpec((tn, tk), lambda j, i, k: (j, k))
        else:
            b_spec = pl.BlockSpec((tk, tn), lambda j, i, k: (k, j))
    if by_column_tile:
        assert res is None
        o_spec = pl.BlockSpec((None, tm, tn), lambda j, i, k: (j, i, 0))
        o_shape = SDS((N // tn, M, tn), out_dtype)
    else:
        o_spec = pl.BlockSpec((tm, tn), lambda j, i, k: (i, j))
        o_shape = SDS((M, N), out_dtype)
    dot = {"nn": _dot, "nt": _dot_nt, "tn": _dot_tn}[mode]
    has_res = res is not None

    def body(*refs):
        a_ref, b_ref = refs[0], refs[1]
        res_ref = refs[2] if has_res else None
        o_ref = refs[2 + has_res]
        p = dot(a_ref[...].astype(BF16), b_ref[...].astype(BF16))
        if nk == 1:
            if has_res:
                p = p + res_ref[...]
            o_ref[...] = p.astype(out_dtype)
        else:
            acc_ref = refs[3 + has_res]
            k = pl.program_id(2)

            @pl.when(k == 0)
            def _():
                acc_ref[...] = p

            @pl.when(k > 0)
            def _():
                acc_ref[...] += p

            @pl.when(k == nk - 1)
            def _():
                t = acc_ref[...]
                if has_res:
                    t = t + res_ref[...]
                o_ref[...] = t.astype(out_dtype)

    in_specs = [a_spec, b_spec] + ([o_spec] if has_res else [])
    args = (a, b) + ((res,) if has_res else ())
    return pl.pallas_call(
        body, name=name, grid=(N // tn, M // tm, nk), in_specs=in_specs, out_specs=o_spec,
        out_shape=o_shape,
        scratch_shapes=[pltpu.VMEM((tm, tn), F32)] if nk > 1 else [],
        compiler_params=_params(dimension_semantics=("parallel", "parallel", "arbitrary")),
    )(*args)


def _norm_fwd(x, w, name):
    T, dm = x.shape

    def body(x_ref, w_ref, h_ref):
        xv = x_ref[...]
        h_ref[...] = (xv * _rsqrt_ms(xv) * w_ref[...]).astype(BF16)

    return pl.pallas_call(
        body, name=name, grid=(T // TB,), in_specs=[_row(dm), _const((1, dm))], out_specs=_row(dm),
        out_shape=SDS((T, dm), BF16), compiler_params=_params())(x, w)


def _norm_bwd(dy, x, w, dres, name):
    T, dm = x.shape

    def body(dy_ref, x_ref, w_ref, dres_ref, dx_ref, dw_ref):
        @pl.when(pl.program_id(0) == 0)
        def _():
            dw_ref[...] = jnp.zeros_like(dw_ref)

        xv = x_ref[...]
        r = _rsqrt_ms(xv)
        n = xv * r
        dyv = dy_ref[...]
        dn = dyv * w_ref[...]
        dx_ref[...] = dres_ref[...] + r * (dn - n * jnp.mean(dn * n, axis=-1, keepdims=True))
        dw_ref[...] += _colsum8(dyv * n)

    return pl.pallas_call(
        body, name=name, grid=(T // TB,),
        in_specs=[_row(dm), _row(dm), _const((1, dm)), _row(dm)],
        out_specs=[_row(dm), _const((8, dm))],
        out_shape=[SDS((T, dm), F32), SDS((8, dm), F32)], compiler_params=_params())(dy, x, w, dres)


def _proj_split(proj, wgk, bgk):
    T = proj.shape[0]

    def body(lr_ref, q_ref, k_ref, v_ref, wgk_ref, b_ref, gk_ref, qs_ref, ks_ref, vs_ref):
        u = _dot(lr_ref[...].astype(BF16), wgk_ref[...].astype(BF16)) + b_ref[...]
        gk_ref[...] = -_softplus(-u) / GLA_GATE_NORM
        qs_ref[...] = (q_ref[...] * 0.125).astype(BF16)
        ks_ref[...] = k_ref[...].astype(BF16)
        vs_ref[...] = v_ref[...].astype(BF16)

    return pl.pallas_call(
        body, name="proj_split", grid=(T // TB,),
        in_specs=[_row(128, C_LR // 128), _row(512, C_QS // 512), _row(512, C_KS // 512), _row(512, C_VS // 512),
                  _const((128, 256)), _const((1, 256))],
        out_specs=[_row(256), _row(512), _row(512), _row(512)],
        out_shape=[SDS((T, 256), F32), SDS((T, 512), BF16), SDS((T, 512), BF16), SDS((T, 512), BF16)],
        compiler_params=_params())(proj, proj, proj, proj, wgk, bgk)


def _dproj_assemble(proj, dq_g, dk_g, dv_g, dg_g, dq_s, dk_s, dv_s, dgk, wgk, bgk):
    T = proj.shape[0]

    def body(lr_ref, dqg_ref, dkg_ref, dvg_ref, dgg_ref, dqs_ref, dks_ref, dvs_ref, dgk_ref, wgk_ref, b_ref,
             dp_ref, dwgk_ref, dbgk_ref):
        @pl.when(pl.program_id(0) == 0)
        def _():
            dwgk_ref[...] = jnp.zeros_like(dwgk_ref)
            dbgk_ref[...] = jnp.zeros_like(dbgk_ref)

        lr = lr_ref[...].astype(BF16)
        wg = wgk_ref[...].astype(BF16)
        u = _dot(lr, wg) + b_ref[...]
        du = dgk_ref[...] * (jax.nn.sigmoid(-u) / GLA_GATE_NORM)
        dub = du.astype(BF16)
        dp_ref[:, C_QG:C_KG] = (dqg_ref[...] * 0.125).astype(BF16)
        dp_ref[:, C_KG:C_VG] = dkg_ref[...].astype(BF16)
        dp_ref[:, C_VG:C_GG] = dvg_ref[...].astype(BF16)
        dp_ref[:, C_GG:C_QS] = dgg_ref[...].astype(BF16)
        dp_ref[:, C_QS:C_KS] = (dqs_ref[...] * 0.125).astype(BF16)
        dp_ref[:, C_KS:C_VS] = dks_ref[...].astype(BF16)
        dp_ref[:, C_VS:C_LR] = dvs_ref[...].astype(BF16)
        dp_ref[:, C_LR:DIN_P] = _dot_nt(dub, wg).astype(BF16)
        dwgk_ref[...] += _dot_tn(lr, dub)
        dbgk_ref[...] += _colsum8(du)

    return pl.pallas_call(
        body, name="dproj_assemble", grid=(T // TB,),
        in_specs=[_row(128, C_LR // 128), _row(256), _row(256), _row(512), _row(512), _row(512), _row(512),
                  _row(512), _row(256), _const((128, 256)), _const((1, 256))],
        out_specs=[_row(DIN_P), _const((128, 256)), _const((8, 256))],
        out_shape=[SDS((T, DIN_P), BF16), SDS((128, 256), F32), SDS((8, 256), F32)],
        compiler_params=_params())(proj, dq_g, dk_g, dv_g, dg_g, dq_s, dk_s, dv_s, dgk, wgk, bgk)


def _group_ones(n, g):
    idx = np.arange(n) // g
    return jnp.asarray((idx[:, None] == idx[None, :]).astype(np.float32), dtype=BF16)


def _mix_cat(o_g, proj, o_s, wg512, ws512, bd64):
    T = o_g.shape[0]

    def body(og_ref, gg_ref, os_ref, wg_ref, ws_ref, bd_ref, cat_ref):
        og = og_ref[...]
        gg = gg_ref[...]
        s = gg * jax.nn.sigmoid(gg)
        for h in range(4):
            sl = slice(128 * h, 128 * (h + 1))
            x = og[:, sl]
            cat_ref[:, sl] = (x * _rsqrt_ms(x) * wg_ref[:, sl] * s[:, sl]).astype(BF16)
        osv = os_ref[...]
        ms = _dot_exact(osv * osv, bd_ref[...]) * (1.0 / 64.0)
        cat_ref[:, 512:1024] = (osv * lax.rsqrt(ms + EPS) * ws_ref[...]).astype(BF16)

    return pl.pallas_call(
        body, name="mix_cat", grid=(T // TB,),
        in_specs=[_row(512), _row(512, C_GG // 512), _row(512), _const((1, 512)), _const((1, 512)),
                  _const((512, 512))],
        out_specs=_row(1024), out_shape=SDS((T, 1024), BF16), compiler_params=_params())(
            o_g, proj, o_s, wg512, ws512, bd64)


def _mix_cat_bwd(dcat, o_g, proj, o_s, wg512, ws512, bd64):
    T = o_g.shape[0]

    def body(dc_ref, og_ref, gg_ref, os_ref, wg_ref, ws_ref, bd_ref, dog_ref, dgg_ref, dos_ref, dwg_ref, dws_ref):
        @pl.when(pl.program_id(0) == 0)
        def _():
            dwg_ref[...] = jnp.zeros_like(dwg_ref)
            dws_ref[...] = jnp.zeros_like(dws_ref)

        og = og_ref[...]
        gg = gg_ref[...]
        sg = jax.nn.sigmoid(gg)
        s = gg * sg
        ds = sg * (1.0 + gg * (1.0 - sg))
        for h in range(4):
            sl = slice(128 * h, 128 * (h + 1))
            x = og[:, sl]
            r = _rsqrt_ms(x)
            n = x * r
            w = wg_ref[:, sl]
            dc = dc_ref[:, sl]
            dy = dc * s[:, sl]
            dgg_ref[:, sl] = dc * (n * w) * ds[:, sl]
            dn = dy * w
            dog_ref[:, sl] = r * (dn - n * jnp.mean(dn * n, axis=-1, keepdims=True))
            dwg_ref[:, sl] += _colsum8(dy * n)
        osv = os_ref[...]
        bd = bd_ref[...]
        r = lax.rsqrt(_dot_exact(osv * osv, bd) * (1.0 / 64.0) + EPS)
        n = osv * r
        dc = dc_ref[:, 512:1024]
        dn = dc * ws_ref[...]
        dos_ref[...] = r * (dn - n * (_dot_exact(dn * n, bd) * (1.0 / 64.0)))
        dws_ref[...] += _colsum8(dc * n)

    return pl.pallas_call(
        body, name="mix_cat_bwd", grid=(T // TB,),
        in_specs=[_row(1024), _row(512), _row(512, C_GG // 512), _row(512), _const((1, 512)), _const((1, 512)),
                  _const((512, 512))],
        out_specs=[_row(512), _row(512), _row(512), _const((8, 512)), _const((8, 512))],
        out_shape=[SDS((T, 512), F32), SDS((T, 512), F32), SDS((T, 512), F32), SDS((8, 512), F32),
                   SDS((8, 512), F32)],
        compiler_params=_params())(dcat, o_g, proj, o_s, wg512, ws512, bd64)


FF_TN = 1408
DW_TK = 2048


def _gate_up_act(h, w):
    T = h.shape[0]
    nj = D_FF // FF_TN

    def body(h_ref, wg_ref, wu_ref, gu_ref, a_ref):
        hv = h_ref[...]
        g = _dot(hv, wg_ref[...])
        u = _dot(hv, wu_ref[...])
        gu_ref[0] = g
        gu_ref[1] = u
        a_ref[...] = (g * jax.nn.sigmoid(g) * u).astype(BF16)

    return pl.pallas_call(
        body, name="mm_gate_up_act", grid=(nj, T // TB),
        in_specs=[pl.BlockSpec((TB, D), lambda j, i: (i, 0)),
                  pl.BlockSpec((D, FF_TN), lambda j, i: (0, j)),
                  pl.BlockSpec((D, FF_TN), lambda j, i: (0, nj + j))],
        out_specs=[pl.BlockSpec((2, TB, FF_TN), lambda j, i: (0, i, j)),
                   pl.BlockSpec((TB, FF_TN), lambda j, i: (i, j))],
        out_shape=[SDS((2, T, D_FF), F32), SDS((T, D_FF), BF16)],
        compiler_params=_params(dimension_semantics=("parallel", "parallel")))(h, w, w)


def _down_bwd(dy, w_down, gu):
    T = dy.shape[0]
    nj = D_FF // FF_TN

    def body(dy_ref, w_ref, gu_ref, dgu_ref):
        da = _dot_nt(dy_ref[...].astype(BF16), w_ref[...])
        g = gu_ref[0]
        sg = jax.nn.sigmoid(g)
        dgu_ref[0] = (da * gu_ref[1] * (sg * (1.0 + g * (1.0 - sg)))).astype(BF16)
        dgu_ref[1] = (da * (g * sg)).astype(BF16)

    return pl.pallas_call(
        body, name="mm_down_bwd", grid=(nj, T // TB),
        in_specs=[pl.BlockSpec((TB, D), lambda j, i: (i, 0)),
                  pl.BlockSpec((FF_TN, D), lambda j, i: (j, 0)),
                  pl.BlockSpec((2, TB, FF_TN), lambda j, i: (0, i, j))],
        out_specs=pl.BlockSpec((2, TB, FF_TN), lambda j, i: (0, i, j)),
        out_shape=SDS((2, T, D_FF), BF16),
        compiler_params=_params(dimension_semantics=("parallel", "parallel")))(dy, w_down, gu)


def _loss_head(y, tgt):
    T = y.shape[0]

    def body(y_ref, t_ref, dy_ref, l_ref):
        @pl.when(pl.program_id(0) == 0)
        def _():
            l_ref[...] = jnp.zeros_like(l_ref)

        e = y_ref[...] - t_ref[...]
        dy_ref[...] = e * (1.0 / D)
        l_ref[...] += _colsum8(e * e) * (0.5 / D)

    return pl.pallas_call(
        body, name="loss_head", grid=(T // TB,), in_specs=[_row(D), _row(D)],
        out_specs=[_row(D), _const((8, D))], out_shape=[SDS((T, D), F32), SDS((8, D), F32)],
        compiler_params=_params())(y, tgt)


def _gla_consts():
    c = GLA_C
    L = np.tril(np.ones((c, c), np.float32))
    blocks = [L, L[(np.arange(c) // 16) * 16]]
    blocks += [np.repeat(L[16 * i:16 * i + 1], c, axis=0) for i in range(4)]
    blocks.append(np.repeat(L[c - 1:c], c, axis=0))
    return jnp.asarray(np.concatenate(blocks, axis=0))


@jax.custom_vjp
def _gla_lin(cm, g):
    cb = cm.astype(BF16)
    h, m, l = _split3(g)
    y = _dot(cb, h) + _dot(cb, m) + _dot(cb, l)
    return tuple(y[GLA_C * n:GLA_C * (n + 1)] for n in range(7))


def _gla_lin_fwd(cm, g):
    return _gla_lin(cm, g), cm


def _gla_lin_bwd(cm, cts):
    cb = cm.astype(BF16)
    h, m, l = _split3(jnp.concatenate(cts, axis=0))
    return None, _dot_tn(cb, h) + _dot_tn(cb, m) + _dot_tn(cb, l)


_gla_lin.defvjp(_gla_lin_fwd, _gla_lin_bwd)


GLA_SUB = 16


def _gla_scores_terms(qs, k, b, rs, blk):
    row = lax.broadcasted_iota(jnp.int32, (GLA_C, 128), 0)
    first_head = lax.broadcasted_iota(jnp.int32, (GLA_SUB, 128), 1) < 64
    keep = row < GLA_SUB * (blk + 1)
    e = jnp.where(keep, jnp.exp(jnp.where(keep, rs[blk] - b, 0.0)), 0.0)
    qb = qs[GLA_SUB * blk:GLA_SUB * (blk + 1)]
    lhs = jnp.concatenate([jnp.where(first_head, qb, 0.0), jnp.where(first_head, 0.0, qb)], axis=0)
    return lhs, e, first_head


@jax.custom_vjp
def _gla_scores(qs, k, b, r0, r1, r2, r3):
    rs = (r0, r1, r2, r3)
    tops, bottoms = [], []
    for blk in range(GLA_C // GLA_SUB):
        lhs, e, _ = _gla_scores_terms(qs, k, b, rs, blk)
        a = _dot_nt_f32(lhs, k * e)
        tops.append(a[:GLA_SUB])
        bottoms.append(a[GLA_SUB:])
    return jnp.concatenate(tops, axis=0), jnp.concatenate(bottoms, axis=0)


def _gla_scores_fwd(qs, k, b, r0, r1, r2, r3):
    return _gla_scores(qs, k, b, r0, r1, r2, r3), (qs, k, b, r0, r1, r2, r3)


def _gla_scores_bwd(saved, cts):
    qs, k, b = saved[:3]
    rs = saved[3:]
    da0, da1 = cts
    dqs, drs = [], []
    dk = jnp.zeros_like(k)
    db = jnp.zeros_like(b)
    for blk in range(GLA_C // GLA_SUB):
        lhs, e, first_head = _gla_scores_terms(qs, k, b, rs, blk)
        rows = slice(GLA_SUB * blk, GLA_SUB * (blk + 1))
        da = jnp.concatenate([da0[rows], da1[rows]], axis=0)
        dlhs = lax.dot_general(da, k * e, (((1,), (0,)), ((), ())), precision=HIGHEST, preferred_element_type=F32)
        dqs.append(jnp.where(first_head, dlhs[:GLA_SUB], dlhs[GLA_SUB:]))
        dks = lax.dot_general(da, lhs, (((0,), (0,)), ((), ())), precision=HIGHEST, preferred_element_type=F32)
        dk = dk + dks * e
        darg = dks * (k * e)
        db = db - darg
        drs.append(darg)
    return (jnp.concatenate(dqs, axis=0), dk, db, *drs)


_gla_scores.defvjp(_gla_scores_fwd, _gla_scores_bwd)


def _gla_chunk_pair(cm, q, k, g, v0, v1, st):
    c = GLA_C
    lane = lax.broadcasted_iota(jnp.int32, (c, 128), 1)
    m0 = (lane < 64).astype(F32)
    m1 = 1.0 - m0
    ri = lax.broadcasted_iota(jnp.int32, (c, c), 0)
    ci = lax.broadcasted_iota(jnp.int32, (c, c), 1)
    b, r, r0, r1, r2, r3, bl = _gla_lin(cm, g)
    a0, a1 = _gla_scores(q * jnp.exp(b - r), k, b, r0, r1, r2, r3)
    causal = ci <= ri
    a0 = jnp.where(causal, a0, 0.0)
    a1 = jnp.where(causal, a1, 0.0)
    qe = q * jnp.exp(b)
    o0 = _dot_nt(qe * m0, st) + _dot(a0, v0)
    o1 = _dot_nt(qe * m1, st) + _dot(a1, v1)
    kd = k * jnp.exp(bl - b)
    m0s = jnp.concatenate([m0, m0], axis=0)
    decay = jnp.exp(jnp.concatenate([bl, bl], axis=0))
    st_new = st * decay + m0s * _dot_tn(v0, kd) + (1.0 - m0s) * _dot_tn(v1, kd)
    return o0, o1, st_new


GLA_TB = 512


def _gla_fwd(proj, gk, cm):
    T = proj.shape[0]
    nc = GLA_TB // GLA_C

    def body(q_ref, k_ref, v_ref, g_ref, cm_ref, o_ref, st_ref, st_scr):
        @pl.when(pl.program_id(0) == 0)
        def _():
            st_scr[...] = jnp.zeros_like(st_scr)

        cmv = cm_ref[...]

        def chunk(ci, carry):
            rs = pl.ds(pl.multiple_of(ci * GLA_C, GLA_C), GLA_C)
            for p in range(2):
                ls = slice(128 * p, 128 * (p + 1))
                st = st_scr[p]
                st_ref[ci, p] = st
                o0, o1, st_new = _gla_chunk_pair(
                    cmv, q_ref[rs, ls] * 0.125, k_ref[rs, ls], g_ref[rs, ls],
                    v_ref[rs, 256 * p:256 * p + 128], v_ref[rs, 256 * p + 128:256 * p + 256], st)
                o_ref[rs, 256 * p:256 * p + 128] = o0
                o_ref[rs, 256 * p + 128:256 * p + 256] = o1
                st_scr[p] = st_new
            return carry

        lax.fori_loop(0, nc, chunk, 0)

    return pl.pallas_call(
        body, name="gla_fwd", grid=(T // GLA_TB,),
        in_specs=[pl.BlockSpec((GLA_TB, 256), lambda i: (i, C_QG // 256)),
                  pl.BlockSpec((GLA_TB, 256), lambda i: (i, C_KG // 256)),
                  pl.BlockSpec((GLA_TB, 512), lambda i: (i, C_VG // 512)),
                  pl.BlockSpec((GLA_TB, 256), lambda i: (i, 0)),
                  pl.BlockSpec((7 * GLA_C, GLA_C), lambda i: (0, 0))],
        out_specs=[pl.BlockSpec((GLA_TB, 512), lambda i: (i, 0)),
                   pl.BlockSpec((nc, 2, 128, 128), lambda i: (i, 0, 0, 0))],
        out_shape=[SDS((T, 512), F32), SDS((T // GLA_C, 2, 128, 128), F32)],
        scratch_shapes=[pltpu.VMEM((2, 128, 128), F32)],
        compiler_params=_params(dimension_semantics=("arbitrary",)))(proj, proj, proj, gk, cm)


def _gla_bwd(proj, gk, cm, states, do):
    T = proj.shape[0]
    nb = T // GLA_TB
    nc = GLA_TB // GLA_C

    def body(q_ref, k_ref, v_ref, g_ref, cm_ref, st_ref, do_ref, dq_ref, dk_ref, dv_ref, dg_ref, dst_scr):
        @pl.when(pl.program_id(0) == 0)
        def _():
            dst_scr[...] = jnp.zeros_like(dst_scr)

        cmv = cm_ref[...]

        def chunk(t, carry):
            ci = nc - 1 - t
            rs = pl.ds(pl.multiple_of(ci * GLA_C, GLA_C), GLA_C)
            for p in range(2):
                ls = slice(128 * p, 128 * (p + 1))
                _, vjp = jax.vjp(
                    functools.partial(_gla_chunk_pair, cmv),
                    q_ref[rs, ls] * 0.125, k_ref[rs, ls], g_ref[rs, ls],
                    v_ref[rs, 256 * p:256 * p + 128], v_ref[rs, 256 * p + 128:256 * p + 256], st_ref[ci, p])
                dq, dk, dg, dv0, dv1, dst = vjp((do_ref[rs, 256 * p:256 * p + 128],
                                                 do_ref[rs, 256 * p + 128:256 * p + 256], dst_scr[p]))
                dq_ref[rs, ls] = dq
                dk_ref[rs, ls] = dk
                dg_ref[rs, ls] = dg
                dv_ref[rs, 256 * p:256 * p + 128] = dv0
                dv_ref[rs, 256 * p + 128:256 * p + 256] = dv1
                dst_scr[p] = dst
            return carry

        lax.fori_loop(0, nc, chunk, 0)

    rev = lambda i: nb - 1 - i
    return pl.pallas_call(
        body, name="gla_bwd", grid=(nb,),
        in_specs=[pl.BlockSpec((GLA_TB, 256), lambda i: (rev(i), C_QG // 256)),
                  pl.BlockSpec((GLA_TB, 256), lambda i: (rev(i), C_KG // 256)),
                  pl.BlockSpec((GLA_TB, 512), lambda i: (rev(i), C_VG // 512)),
                  pl.BlockSpec((GLA_TB, 256), lambda i: (rev(i), 0)),
                  pl.BlockSpec((7 * GLA_C, GLA_C), lambda i: (0, 0)),
                  pl.BlockSpec((nc, 2, 128, 128), lambda i: (rev(i), 0, 0, 0)),
                  pl.BlockSpec((GLA_TB, 512), lambda i: (rev(i), 0))],
        out_specs=[pl.BlockSpec((GLA_TB, 256), lambda i: (rev(i), 0)),
                   pl.BlockSpec((GLA_TB, 256), lambda i: (rev(i), 0)),
                   pl.BlockSpec((GLA_TB, 512), lambda i: (rev(i), 0)),
                   pl.BlockSpec((GLA_TB, 256), lambda i: (rev(i), 0))],
        out_shape=[SDS((T, 256), F32), SDS((T, 256), F32), SDS((T, 512), F32), SDS((T, 256), F32)],
        scratch_shapes=[pltpu.VMEM((2, 128, 128), F32)],
        compiler_params=_params(dimension_semantics=("arbitrary",)))(proj, proj, proj, gk, cm, states, do)


SB_DEAD = 105.0
SB_COUNT_LANE = 127


def _sb_tri():
    i = np.arange(SBQ)
    return jnp.asarray((i[:, None] > i[None, :]).astype(np.float32), dtype=BF16)


def _sb_block_fwd(qh, kb, tri, carry, strict):
    z = _dot_nt(qh, kb)
    sp = _softplus(z)
    l1 = -sp
    if strict is not None:
        l1 = jnp.where(strict, l1, 0.0)
    log_a = (z - sp) + _dot_hilo(l1, tri) + carry
    a = jnp.exp(log_a)
    if strict is not None:
        a = jnp.where(strict, a, 0.0)
    return z - sp, l1, a


def _sb_fwd(qs, ks, vs, tri):
    T = qs.shape[0]
    nq = T // SBQ

    def body(q_ref, k_ref, v_ref, tri_ref, o_ref, c_ref):
        i = pl.program_id(1)
        lane = lax.broadcasted_iota(jnp.int32, (1, 128), 1)
        clane = lax.broadcasted_iota(jnp.int32, (SBQ, 128), 1)
        strict = (lax.broadcasted_iota(jnp.int32, (SBQ, SBQ), 1) < lax.broadcasted_iota(jnp.int32, (SBQ, SBQ), 0))
        tri_v = tri_ref[...]
        qv = q_ref[...]
        first_head = lane < 64
        qhs = (jnp.where(first_head, qv, jnp.zeros_like(qv)), jnp.where(first_head, jnp.zeros_like(qv), qv))

        def block(j, carries, accs, masked):
            rs = pl.ds(pl.multiple_of(j * SBQ, SBQ), SBQ)
            kb = k_ref[rs, :]
            vb = v_ref[rs, :]
            out_c, out_a = [], []
            for hh in range(2):
                _, l1, a = _sb_block_fwd(qhs[hh], kb, tri_v, carries[hh], strict if masked else None)
                out_a.append(accs[hh] + _dot(a.astype(BF16), vb))
                out_c.append(carries[hh] + jnp.sum(l1, axis=1, keepdims=True))
            return out_c, out_a

        zero1 = jnp.zeros((SBQ, 1), F32)
        zero128 = jnp.zeros((SBQ, 128), F32)
        (c0, c1), (a0, a1) = block(i, (zero1, zero1), (zero128, zero128), True)

        def more(state):
            return (state[0] <= i) & (jnp.maximum(jnp.max(state[1]), jnp.max(state[2])) > -SB_DEAD)

        def step(state):
            jj, c0, c1, a0, a1, t0, t1 = state
            j = i - jj
            t0 = jnp.where(clane == j, c0, t0)
            t1 = jnp.where(clane == j, c1, t1)
            (c0, c1), (a0, a1) = block(j, (c0, c1), (a0, a1), False)
            return jj + 1, c0, c1, a0, a1, t0, t1

        jj, c0, c1, a0, a1, t0, t1 = lax.while_loop(
            more, step, (jnp.int32(1), c0, c1, a0, a1, zero128, zero128))
        o_ref[...] = jnp.where(first_head, a0, a1)
        swept = (jj - 1).astype(F32)
        c_ref[0, :, 0:128] = jnp.where(clane == SB_COUNT_LANE, swept, t0)
        c_ref[0, :, 128:256] = jnp.where(clane == SB_COUNT_LANE, swept, t1)

    return pl.pallas_call(
        body, name="sb_fwd", grid=(4, nq),
        in_specs=[pl.BlockSpec((SBQ, 128), lambda h, i: (i, h)),
                  pl.BlockSpec((T, 128), lambda h, i: (0, h)),
                  pl.BlockSpec((T, 128), lambda h, i: (0, h)),
                  pl.BlockSpec((SBQ, SBQ), lambda h, i: (0, 0))],
        out_specs=[pl.BlockSpec((SBQ, 128), lambda h, i: (i, h)),
                   pl.BlockSpec((1, SBQ, 256), lambda h, i: (h, i, 0))],
        out_shape=[SDS((T, 512), F32), SDS((4, T, 256), F32)],
        compiler_params=_params(dimension_semantics=("parallel", "arbitrary")))(qs, ks, vs, tri)


def _sb_bwd(qs, ks, vs, do, carries, tri, tri_t):
    T = qs.shape[0]
    nq = T // SBQ

    def body(q_ref, k_ref, v_ref, do_ref, c_ref, tri_ref, trit_ref, dq_ref, dk_ref, dv_ref):
        i = pl.program_id(1)

        @pl.when(i == 0)
        def _():
            dk_ref[...] = jnp.zeros_like(dk_ref)
            dv_ref[...] = jnp.zeros_like(dv_ref)

        lane = lax.broadcasted_iota(jnp.int32, (1, 128), 1)
        clane = lax.broadcasted_iota(jnp.int32, (SBQ, 128), 1)
        strict = (lax.broadcasted_iota(jnp.int32, (SBQ, SBQ), 1) < lax.broadcasted_iota(jnp.int32, (SBQ, SBQ), 0))
        tri_v = tri_ref[...]
        trit_v = trit_ref[...]
        qv = q_ref[...]
        dov = do_ref[...].astype(BF16)
        first_head = lane < 64
        qhs = (jnp.where(first_head, qv, jnp.zeros_like(qv)), jnp.where(first_head, jnp.zeros_like(qv), qv))
        dohs = (jnp.where(first_head, dov, jnp.zeros_like(dov)), jnp.where(first_head, jnp.zeros_like(dov), dov))
        cts = (c_ref[0, :, 0:128], c_ref[0, :, 128:256])

        def block(j, pcarries, dqs, masked):
            rs = pl.ds(pl.multiple_of(j * SBQ, SBQ), SBQ)
            kb = k_ref[rs, :]
            vb = v_ref[rs, :]
            out_p, out_q = [], []
            dk = jnp.zeros((SBQ, 128), F32)
            dv = jnp.zeros((SBQ, 128), F32)
            for hh in range(2):
                carry = jnp.sum(jnp.where(clane == j, cts[hh], 0.0), axis=1, keepdims=True)
                lb, _, a = _sb_block_fwd(qhs[hh], kb, tri_v, carry, strict if masked else None)
                g = a * _dot_nt(dohs[hh], vb)
                p = _dot_hilo(g, trit_v) + pcarries[hh]
                dz = g - (g + p) * jnp.exp(lb)
                if masked:
                    dz = jnp.where(strict, dz, 0.0)
                dzb = dz.astype(BF16)
                dk = dk + _dot_tn(dzb, qhs[hh])
                dv = dv + _dot_tn(a.astype(BF16), dohs[hh])
                out_p.append(pcarries[hh] + jnp.sum(g, axis=1, keepdims=True))
                out_q.append(dqs[hh] + _dot(dzb, kb))
            dk_ref[rs, :] += dk
            dv_ref[rs, :] += dv
            return out_p, out_q

        def step(j, state):
            (p0, p1), (q0, q1) = block(j, (state[0], state[1]), (state[2], state[3]), False)
            return p0, p1, q0, q1

        swept = jnp.max(jnp.where(clane == SB_COUNT_LANE, cts[0], 0.0)).astype(jnp.int32)
        first = i - jnp.clip(swept, 0, i)
        zero1 = jnp.zeros((SBQ, 1), F32)
        zero128 = jnp.zeros((SBQ, 128), F32)
        p0, p1, q0, q1 = lax.fori_loop(first, i, step, (zero1, zero1, zero128, zero128))
        _, (q0, q1) = block(i, (p0, p1), (q0, q1), True)
        dq_ref[...] = jnp.where(first_head, q0, q1)

    return pl.pallas_call(
        body, name="sb_bwd", grid=(4, nq),
        in_specs=[pl.BlockSpec((SBQ, 128), lambda h, i: (i, h)),
                  pl.BlockSpec((T, 128), lambda h, i: (0, h)),
                  pl.BlockSpec((T, 128), lambda h, i: (0, h)),
                  pl.BlockSpec((SBQ, 128), lambda h, i: (i, h)),
                  pl.BlockSpec((1, SBQ, 256), lambda h, i: (h, i, 0)),
                  pl.BlockSpec((SBQ, SBQ), lambda h, i: (0, 0)),
                  pl.BlockSpec((SBQ, SBQ), lambda h, i: (0, 0))],
        out_specs=[pl.BlockSpec((SBQ, 128), lambda h, i: (i, h)),
                   pl.BlockSpec((T, 128), lambda h, i: (0, h)),
                   pl.BlockSpec((T, 128), lambda h, i: (0, h))],
        out_shape=[SDS((T, 512), F32), SDS((T, 512), F32), SDS((T, 512), F32)],
        compiler_params=_params(dimension_semantics=("parallel", "arbitrary")))(qs, ks, vs, do, carries, tri, tri_t)


def _mem_fwd(mem, mem_norm_w, w_mkv, mk_norm_w):
    M = mem.shape[0]

    def body(mem_ref, wn_ref, w_ref, wk_ref, mn_ref, kraw_ref, k_ref, v_ref):
        mv = mem_ref[...]
        mn = (mv * _rsqrt_ms(mv) * wn_ref[...]).astype(BF16)
        mn_ref[...] = mn
        kv = _dot(mn, w_ref[...])
        kraw_ref[...] = kv[:, :D]
        v_ref[...] = kv[:, D:].astype(BF16)
        for h in range(MEM_HEADS):
            sl = slice(MEM_HD * h, MEM_HD * (h + 1))
            x = kv[:, sl]
            k_ref[:, sl] = (x * _rsqrt_ms(x) * wk_ref[...]).astype(BF16)

    vm = pl.BlockSpec(memory_space=pltpu.VMEM)
    return pl.pallas_call(
        body, name="mem_fwd", in_specs=[vm] * 4, out_specs=[vm] * 4,
        out_shape=[SDS((M, D), BF16), SDS((M, D), F32), SDS((M, D), BF16), SDS((M, D), BF16)],
        compiler_params=_params())(mem, mem_norm_w, w_mkv, mk_norm_w)


def _xattn_fwd(qraw, k, v, wq):
    T = qraw.shape[0]
    M = k.shape[0]

    def body(q_ref, k_ref, v_ref, wq_ref, o_ref):
        for h in range(MEM_HEADS):
            sl = slice(MEM_HD * h, MEM_HD * (h + 1))
            x = q_ref[:, sl]
            q = (x * _rsqrt_ms(x) * wq_ref[...]).astype(BF16)
            s = _dot_nt(q, k_ref[:, sl]) * (1.0 / math.sqrt(MEM_HD))
            s = s - jnp.max(s, axis=-1, keepdims=True)
            e = jnp.exp(s)
            p = e / jnp.sum(e, axis=-1, keepdims=True)
            o_ref[:, sl] = _dot(p.astype(BF16), v_ref[:, sl]).astype(BF16)

    return pl.pallas_call(
        body, name="xattn_fwd", grid=(T // TB,),
        in_specs=[_row(D), _const((M, D)), _const((M, D)), _const((1, MEM_HD))],
        out_specs=_row(D), out_shape=SDS((T, D), BF16), compiler_params=_params())(qraw, k, v, wq)


def _xattn_bwd(qraw, k, v, wq, do):
    T = qraw.shape[0]
    M = k.shape[0]

    def body(q_ref, k_ref, v_ref, wq_ref, do_ref, dq_ref, dk_ref, dv_ref, dw_ref):
        @pl.when(pl.program_id(0) == 0)
        def _():
            dk_ref[...] = jnp.zeros_like(dk_ref)
            dv_ref[...] = jnp.zeros_like(dv_ref)
            dw_ref[...] = jnp.zeros_like(dw_ref)

        w = wq_ref[...]
        for h in range(MEM_HEADS):
            sl = slice(MEM_HD * h, MEM_HD * (h + 1))
            x = q_ref[:, sl]
            r = _rsqrt_ms(x)
            n = x * r
            q = (n * w).astype(BF16)
            kb = k_ref[:, sl]
            s = _dot_nt(q, kb) * (1.0 / math.sqrt(MEM_HD))
            s = s - jnp.max(s, axis=-1, keepdims=True)
            e = jnp.exp(s)
            p = e / jnp.sum(e, axis=-1, keepdims=True)
            dob = do_ref[:, sl].astype(BF16)
            dp = _dot_nt(dob, v_ref[:, sl])
            ds = (p * (dp - jnp.sum(dp * p, axis=-1, keepdims=True)) * (1.0 / math.sqrt(MEM_HD))).astype(BF16)
            dv_ref[:, sl] += _dot_tn(p.astype(BF16), dob)
            dk_ref[:, sl] += _dot_tn(ds, q)
            dqn = _dot(ds, kb)
            dn = dqn * w
            dq_ref[:, sl] = r * (dn - n * jnp.mean(dn * n, axis=-1, keepdims=True))
            dw_ref[...] += _colsum8(dqn * n)

    return pl.pallas_call(
        body, name="xattn_bwd", grid=(T // TB,),
        in_specs=[_row(D), _const((M, D)), _const((M, D)), _const((1, MEM_HD)), _row(D)],
        out_specs=[_row(D), _const((M, D)), _const((M, D)), _const((8, MEM_HD))],
        out_shape=[SDS((T, D), F32), SDS((M, D), F32), SDS((M, D), F32), SDS((8, MEM_HD), F32)],
        compiler_params=_params())(qraw, k, v, wq, do)


def _mem_bwd(mem, mem_norm_w, w_mkv, mk_norm_w, mem_n, k_raw, dk, dv):
    M = mem.shape[0]

    def body(mem_ref, wn_ref, w_ref, wk_ref, mn_ref, kraw_ref, dk_ref, dv_ref, dw_ref, dwn_ref, dwk_ref, dkv_scr):
        wk = wk_ref[...]
        dwk = jnp.zeros((8, MEM_HD), F32)
        for h in range(MEM_HEADS):
            sl = slice(MEM_HD * h, MEM_HD * (h + 1))
            x = kraw_ref[:, sl]
            r = _rsqrt_ms(x)
            n = x * r
            dkh = dk_ref[:, sl]
            dn = dkh * wk
            dkv_scr[:, sl] = (r * (dn - n * jnp.mean(dn * n, axis=-1, keepdims=True))).astype(BF16)
            dwk = dwk + _colsum8(dkh * n)
        dwk_ref[...] = dwk
        dkv_scr[:, D:] = dv_ref[...].astype(BF16)
        dkv = dkv_scr[...]
        mn = mn_ref[...]
        for s in range(4):
            dw_ref[s] = _dot_tn(mn, dkv[:, 512 * s:512 * (s + 1)]).astype(BF16)
        dmn = _dot_nt(dkv, w_ref[...])
        mv = mem_ref[...]
        dwn_ref[...] = _colsum8(dmn * (mv * _rsqrt_ms(mv)))

    vm = pl.BlockSpec(memory_space=pltpu.VMEM)
    return pl.pallas_call(
        body, name="mem_bwd", in_specs=[vm] * 8, out_specs=[vm] * 3,
        out_shape=[SDS((4, D, 512), BF16), SDS((8, D), F32), SDS((8, MEM_HD), F32)],
        scratch_shapes=[pltpu.VMEM((M, 2 * D), BF16)],
        compiler_params=_params())(mem, mem_norm_w, w_mkv, mk_norm_w, mem_n, k_raw, dk, dv)


def _local_step(x, mem, tgt, w):
    T = x.shape[0]
    wgk = jnp.zeros((128, 256), F32).at[:16].set(w["w_gk_up"].astype(F32))
    wg512 = jnp.tile(w["gla_norm_w"], (1, 4))
    ws512 = jnp.tile(w["sb_norm_w"], (1, 8))
    bd64 = _group_ones(512, 64)
    cm = _gla_consts()
    tri = _sb_tri()
    tri_t = tri.T

    h1 = _norm_fwd(x, w["mix_norm_w"], "norm1_fwd")
    proj = _matmul(h1, w["w_in"], mode="nn", tm=TB, tn=DIN_P, name="mm_proj")
    gk, qs, ks, vs = _proj_split(proj, wgk, w["b_gk"])
    o_g, states = _gla_fwd(proj, gk, cm)
    o_s, carries = _sb_fwd(qs, ks, vs, tri)
    cat = _mix_cat(o_g, proj, o_s, wg512, ws512, bd64)
    x1 = _matmul(cat, w["w_out"], mode="nn", tm=TB, tn=D, res=x, name="mm_out")
    h2 = _norm_fwd(x1, w["xattn_norm_w"], "norm2_fwd")
    qraw = _matmul(h2, w["w_mq"], mode="nn", tm=TB, tn=D, name="mm_mq")
    mem_n, k_raw, k_n, v_m = _mem_fwd(mem, w["mem_norm_w"], w["w_mkv"], w["mk_norm_w"])
    om = _xattn_fwd(qraw, k_n, v_m, w["mq_norm_w"])
    x2 = _matmul(om, w["w_mo"], mode="nn", tm=TB, tn=D, res=x1, name="mm_mo")
    h3 = _norm_fwd(x2, w["ffn_norm_w"], "norm3_fwd")
    gu, act = _gate_up_act(h3, w["w_gate_up"])
    x3 = _matmul(act, w["w_down"], mode="nn", tm=TB, tn=D, res=x2, name="mm_down")
    dx3, loss_rows = _loss_head(x3, tgt)

    g = {}
    dw_tk = min(DW_TK, T)
    dw = dict(mode="tn", tk=dw_tk, out_dtype=BF16)
    g["w_down"] = _matmul(act, dx3, tm=1408, tn=D, name="mm_dw_down", **dw)
    dgu = _down_bwd(dx3, w["w_down"], gu)
    g["w_gate_up"] = _matmul(
        h3, dgu, tm=D, tn=FF_TN, by_column_tile=True, mnk=(D, 2 * D_FF, T), name="mm_dw_gate_up",
        b_spec=pl.BlockSpec((None, dw_tk, FF_TN), lambda j, i, k: (j // 2, k, j % 2)), **dw)
    dh3 = _matmul(dgu, w["w_gate_up"], mode="nt", tm=TB, tn=D, tk=D_FF, mnk=(T, D, 2 * D_FF), name="mm_dh3",
                  a_spec=pl.BlockSpec((None, TB, D_FF), lambda j, i, k: (k, i, 0)))
    dx2, g["ffn_norm_w"] = _norm_bwd(dh3, x2, w["ffn_norm_w"], dx3, "norm3_bwd")
    g["w_mo"] = _matmul(om, dx2, tm=D, tn=D, name="mm_dw_mo", **dw)
    dom = _matmul(dx2, w["w_mo"], mode="nt", tm=TB, tn=D, name="mm_dom")
    dqraw, dk_n, dv_m, g["mq_norm_w"] = _xattn_bwd(qraw, k_n, v_m, w["mq_norm_w"], dom)
    g["w_mkv"], g["mem_norm_w"], g["mk_norm_w"] = _mem_bwd(
        mem, w["mem_norm_w"], w["w_mkv"], w["mk_norm_w"], mem_n, k_raw, dk_n, dv_m)
    g["w_mq"] = _matmul(h2, dqraw, tm=D, tn=D, name="mm_dw_mq", **dw)
    dh2 = _matmul(dqraw, w["w_mq"], mode="nt", tm=TB, tn=D, name="mm_dh2")
    dx1, g["xattn_norm_w"] = _norm_bwd(dh2, x1, w["xattn_norm_w"], dx2, "norm2_bwd")
    g["w_out"] = _matmul(cat, dx1, tm=D, tn=D, name="mm_dw_out", **dw)
    dcat = _matmul(dx1, w["w_out"], mode="nt", tm=TB, tn=D, name="mm_dcat")
    do_g, dg_g, do_s, dwg, dws = _mix_cat_bwd(dcat, o_g, proj, o_s, wg512, ws512, bd64)
    dq_s, dk_s, dv_s = _sb_bwd(qs, ks, vs, do_s, carries, tri, tri_t)
    dq_g, dk_g, dv_g, dgk = _gla_bwd(proj, gk, cm, states, do_g)
    dproj, dwgk, g["b_gk"] = _dproj_assemble(proj, dq_g, dk_g, dv_g, dg_g, dq_s, dk_s, dv_s, dgk, wgk, w["b_gk"])
    g["w_in"] = _matmul(h1, dproj, tm=D, tn=640, name="mm_dw_in", **dw)
    dh1 = _matmul(dproj, w["w_in"], mode="nt", tm=TB, tn=D, name="mm_dh1")
    grad_x, g["mix_norm_w"] = _norm_bwd(dh1, x, w["mix_norm_w"], dx1, "norm1_bwd")

    g["w_gk_up"] = dwgk[:16]
    g["gla_norm_w"] = dwg.reshape(8, 4, 128).sum(axis=1)
    g["sb_norm_w"] = dws.reshape(8, 8, 64).sum(axis=1)
    for n in SMALL:
        g[n] = jnp.sum(g[n], axis=0, keepdims=True)
    return jnp.sum(loss_rows), grad_x, g


def _mesh_pos():
    return lax.axis_index("x"), lax.axis_index("y"), lax.axis_index("c")


def _other_chips(x, y):
    return [(1 - x, y), (x, 1 - y), (1 - x, 1 - y)]


def _ag_weights(wp):
    def body(w_ref, out_ref, send_sems, recv_sems):
        x, y, c = _mesh_pos()
        me = 2 * x + y
        sibling = (x, y, 1 - c)
        chips = _other_chips(x, y)
        mine, theirs = c, 1 - c

        def copy(k, src, dst, to):
            return pltpu.make_async_remote_copy(src_ref=src, dst_ref=dst, send_sem=send_sems.at[k],
                                                recv_sem=recv_sems.at[k], device_id=to, device_id_type=MESH)

        first = [copy(k, w_ref.at[mine], out_ref.at[me, mine], (cx, cy, c)) for k, (cx, cy) in enumerate(chips)]
        for cp in first:
            cp.start()
        passed = []
        for k, (cx, cy) in enumerate(chips):
            slot = out_ref.at[2 * cx + cy, mine]
            copy(k, slot, slot, (cx, cy, c)).wait_recv()
            fwd = copy(3 + k, slot, slot, sibling)
            fwd.start()
            passed.append(fwd)
        for k, (cx, cy) in enumerate(chips):
            slot = out_ref.at[2 * cx + cy, theirs]
            copy(3 + k, slot, slot, sibling).wait_recv()
        for cp in first + passed:
            cp.wait_send()

    hbm = pl.BlockSpec(memory_space=pl.ANY)
    return pl.pallas_call(
        body, name="ag_weights", in_specs=[hbm], out_specs=hbm,
        out_shape=SDS((4, 2, PACK_HALF, 1024), BF16),
        scratch_shapes=[pltpu.SemaphoreType.DMA((6,)), pltpu.SemaphoreType.DMA((6,))],
        compiler_params=pltpu.CompilerParams(has_side_effects=True))(wp)


def _rs_swap_halves(gp):
    def body(g_ref, out_ref, send_sem, recv_sem):
        x, y, c = _mesh_pos()
        cp = pltpu.make_async_remote_copy(
            src_ref=g_ref.at[:, 1 - c], dst_ref=out_ref, send_sem=send_sem, recv_sem=recv_sem,
            device_id=(x, y, 1 - c), device_id_type=MESH)
        cp.start()
        cp.wait()

    hbm = pl.BlockSpec(memory_space=pl.ANY)
    return pl.pallas_call(
        body, name="rs_swap_halves", in_specs=[hbm], out_specs=hbm,
        out_shape=SDS((4, PACK_HALF, 1024), BF16),
        scratch_shapes=[pltpu.SemaphoreType.DMA, pltpu.SemaphoreType.DMA],
        compiler_params=pltpu.CompilerParams(has_side_effects=True))(gp)


RS_TR = 704


def _rs_add_halves(gp, other, c_arr):
    nt = PACK_HALF // RS_TR

    def body(c_ref, a_ref, b_ref, o_ref):
        o_ref[...] = (a_ref[0].astype(F32) + b_ref[...].astype(F32)).astype(BF16)

    return pl.pallas_call(
        body, name="rs_add_halves",
        grid_spec=pltpu.PrefetchScalarGridSpec(
            num_scalar_prefetch=1, grid=(4, nt),
            in_specs=[pl.BlockSpec((1, 1, RS_TR, 1024), lambda s, t, c: (s, c[0], t, 0)),
                      pl.BlockSpec((1, RS_TR, 1024), lambda s, t, c: (s, t, 0))],
            out_specs=pl.BlockSpec((1, RS_TR, 1024), lambda s, t, c: (s, t, 0))),
        out_shape=SDS((4, PACK_HALF, 1024), BF16), compiler_params=_params())(c_arr, gp, other)


def _rs_exchange(part):
    def body(p_ref, out_ref, send_sems, recv_sems):
        x, y, c = _mesh_pos()
        me = 2 * x + y
        chips = _other_chips(x, y)
        sends = []
        for k, (cx, cy) in enumerate(chips):
            cp = pltpu.make_async_remote_copy(
                src_ref=p_ref.at[2 * cx + cy], dst_ref=out_ref.at[me], send_sem=send_sems.at[k],
                recv_sem=recv_sems.at[k], device_id=(cx, cy, c), device_id_type=MESH)
            cp.start()
            sends.append(cp)
        for k, (cx, cy) in enumerate(chips):
            slot = out_ref.at[2 * cx + cy]
            pltpu.make_async_remote_copy(
                src_ref=slot, dst_ref=slot, send_sem=send_sems.at[k], recv_sem=recv_sems.at[k],
                device_id=(cx, cy, c), device_id_type=MESH).wait_recv()
        for cp in sends:
            cp.wait_send()

    hbm = pl.BlockSpec(memory_space=pl.ANY)
    return pl.pallas_call(
        body, name="rs_exchange", in_specs=[hbm], out_specs=hbm,
        out_shape=SDS((4, PACK_HALF, 1024), BF16),
        scratch_shapes=[pltpu.SemaphoreType.DMA((3,)), pltpu.SemaphoreType.DMA((3,))],
        compiler_params=pltpu.CompilerParams(has_side_effects=True))(part)


def _rs_add_chips(recv, part, me_arr):
    def body(me_ref, r_ref, p_ref, o_ref):
        me = me_ref[0]
        total = None
        for k in range(4):
            term = jnp.where(me == k, p_ref[k], r_ref[k]).astype(F32)
            total = term if total is None else total + term
        o_ref[...] = total

    spec = pl.BlockSpec((4, RS_TR, 1024), lambda t, me: (0, t, 0))
    return pl.pallas_call(
        body, name="rs_add_chips",
        grid_spec=pltpu.PrefetchScalarGridSpec(
            num_scalar_prefetch=1, grid=(PACK_HALF // RS_TR,), in_specs=[spec, spec],
            out_specs=pl.BlockSpec((RS_TR, 1024), lambda t, me: (t, 0))),
        out_shape=SDS((PACK_HALF, 1024), F32), compiler_params=_params())(me_arr, recv, part)


def _rs_share(half_sum):
    def body(h_ref, out_ref, send_sem, recv_sem):
        x, y, c = _mesh_pos()
        cp = pltpu.make_async_remote_copy(
            src_ref=h_ref, dst_ref=out_ref, send_sem=send_sem, recv_sem=recv_sem,
            device_id=(x, y, 1 - c), device_id_type=MESH)
        cp.start()
        cp.wait()

    hbm = pl.BlockSpec(memory_space=pl.ANY)
    return pl.pallas_call(
        body, name="rs_share", in_specs=[hbm], out_specs=hbm,
        out_shape=SDS((PACK_HALF, 1024), F32),
        scratch_shapes=[pltpu.SemaphoreType.DMA, pltpu.SemaphoreType.DMA],
        compiler_params=pltpu.CompilerParams(has_side_effects=True))(half_sum)


def _allreduce_small(s):
    def gather(s_ref, out_ref, send_sems, recv_sems, local_sem):
        x, y, c = _mesh_pos()
        me = 4 * x + 2 * y + c
        local = pltpu.make_async_copy(s_ref, out_ref.at[me], local_sem)
        local.start()
        peers = []
        for r in range(1, 8):
            px = 1 - x if r & 4 else x
            py = 1 - y if r & 2 else y
            pc = 1 - c if r & 1 else c
            peers.append((px, py, pc))
        sends = []
        for k, peer in enumerate(peers):
            cp = pltpu.make_async_remote_copy(
                src_ref=s_ref, dst_ref=out_ref.at[me], send_sem=send_sems.at[k], recv_sem=recv_sems.at[k],
                device_id=peer, device_id_type=MESH)
            cp.start()
            sends.append(cp)
        for k, (px, py, pc) in enumerate(peers):
            slot = out_ref.at[4 * px + 2 * py + pc]
            pltpu.make_async_remote_copy(
                src_ref=slot, dst_ref=slot, send_sem=send_sems.at[k], recv_sem=recv_sems.at[k],
                device_id=(px, py, pc), device_id_type=MESH).wait_recv()
        for cp in sends:
            cp.wait_send()
        local.wait()

    hbm = pl.BlockSpec(memory_space=pl.ANY)
    parts = pl.pallas_call(
        gather, name="small_gather", in_specs=[hbm], out_specs=hbm,
        out_shape=SDS((8, SMALL_ROWS, 1024), F32),
        scratch_shapes=[pltpu.SemaphoreType.DMA((7,)), pltpu.SemaphoreType.DMA((7,)), pltpu.SemaphoreType.DMA],
        compiler_params=pltpu.CompilerParams(has_side_effects=True))(s)

    def add(p_ref, o_ref):
        total = p_ref[0]
        for k in range(1, 8):
            total = total + p_ref[k]
        o_ref[...] = total

    vm = pl.BlockSpec(memory_space=pltpu.VMEM)
    return pl.pallas_call(add, name="small_sum", in_specs=[vm], out_specs=vm,
                          out_shape=SDS((SMALL_ROWS, 1024), F32))(parts)


def _adamw(w, g, m, v, name):
    rows, cols = w.shape
    tr = rows
    for cand in (512, 352, 256):
        if rows > cand and rows % cand == 0:
            tr = cand
            break
    c1 = 1.0 - ADAM_B1 ** ADAM_STEP
    c2 = 1.0 - ADAM_B2 ** ADAM_STEP

    def body(w_ref, g_ref, m_ref, v_ref, d_ref, mo_ref, vo_ref):
        gv = g_ref[...]
        mn = ADAM_B1 * m_ref[...] + (1.0 - ADAM_B1) * gv
        vn = ADAM_B2 * v_ref[...] + (1.0 - ADAM_B2) * (gv * gv)
        mo_ref[...] = mn
        vo_ref[...] = vn
        d_ref[...] = -ADAM_LR * ((mn / c1) / (jnp.sqrt(vn / c2) + ADAM_EPS) + ADAM_WD * w_ref[...])

    spec = pl.BlockSpec((tr, cols), lambda i: (i, 0))
    return pl.pallas_call(
        body, name=name, grid=(rows // tr,), in_specs=[spec] * 4, out_specs=[spec] * 3,
        out_shape=[SDS((rows, cols), F32)] * 3, compiler_params=_params())(w, g, m, v)


BIG_SHARD_ROWS = {"w_in": 772, "w_gk_up": 1, "w_out": 256, "w_mq": 256, "w_mkv": 512, "w_mo": 256,
                  "w_gate_up": 1408, "w_down": 704}
SMALL_SIZES = {"mix_norm_w": 1024, "b_gk": 256, "gla_norm_w": 128, "sb_norm_w": 64, "xattn_norm_w": 1024,
               "mem_norm_w": 1024, "mq_norm_w": 256, "mk_norm_w": 256, "ffn_norm_w": 1024}
N_BIG_ROWS = sum(BIG_SHARD_ROWS.values())


def _pack_small(d):
    flat = jnp.concatenate([d[n].reshape(-1) for n in SMALL])
    return jnp.pad(flat, (0, SMALL_ROWS * 1024 - flat.shape[0])).reshape(SMALL_ROWS, 1024)


def _unpack_small(p):
    flat = p.reshape(-1)
    out, off = {}, 0
    for n in SMALL:
        out[n] = flat[off:off + SMALL_SIZES[n]].reshape(1, SMALL_SIZES[n])
        off += SMALL_SIZES[n]
    return out


def _pack_shards(d, dtype):
    rows = [d[n].astype(dtype).reshape(BIG_SHARD_ROWS[n], 1024) for n in BIG]
    pad = jnp.zeros((PACK_ROWS - N_BIG_ROWS, 1024), dtype)
    return jnp.concatenate(rows + [pad], axis=0)


def _big_offsets():
    off, out = 0, {}
    for n in BIG:
        out[n] = off
        off += BIG_SHARD_ROWS[n]
    return out


def _unpack_full(allw):
    off = _big_offsets()

    def seg(n):
        return allw[:, off[n]:off[n] + BIG_SHARD_ROWS[n]]

    def cols(n, k, c):
        return seg(n).reshape(4, k, c).transpose(1, 0, 2).reshape(k, 4 * c)

    def rows(n, c):
        return seg(n).reshape(-1, c)

    w_in = cols("w_in", 1024, 772)
    w_in = jnp.concatenate([w_in[:, :1536], w_in[:, 1552:], w_in[:, 1536:1552],
                            jnp.zeros((1024, DIN_P - DIN), w_in.dtype)], axis=1)
    return {"w_in": w_in, "w_gk_up": cols("w_gk_up", 16, 64), "w_out": rows("w_out", 1024),
            "w_mq": rows("w_mq", 1024), "w_mkv": cols("w_mkv", 1024, 512), "w_mo": rows("w_mo", 1024),
            "w_gate_up": cols("w_gate_up", 1024, 1408), "w_down": rows("w_down", 1024)}


def _pack_grads(g):
    def cols(a, k, c):
        return a.astype(BF16).reshape(k, 4, c).transpose(1, 0, 2).reshape(4, -1, 1024)

    def rows(a):
        return a.astype(BF16).reshape(4, -1, 1024)

    gi = g["w_in"]
    gi = jnp.concatenate([gi[:, :1536], gi[:, C_LR:C_LR + 16], gi[:, 1536:C_LR]], axis=1)
    parts = [cols(gi, 1024, 772), cols(g["w_gk_up"], 16, 64), rows(g["w_out"]), rows(g["w_mq"]),
             rows(g["w_mkv"]), rows(g["w_mo"]), rows(g["w_gate_up"]), rows(g["w_down"])]
    pad = jnp.zeros((4, PACK_ROWS - N_BIG_ROWS, 1024), BF16)
    return jnp.concatenate(parts + [pad], axis=1)


def kernel(x, mem, mix_norm_w, w_in, w_gk_up, b_gk, gla_norm_w, sb_norm_w, w_out, xattn_norm_w, mem_norm_w, w_mq, w_mkv, mq_norm_w, mk_norm_w, w_mo, ffn_norm_w, w_gate_up, w_down, loss_target, m_mix_norm_w, m_w_in, m_w_gk_up, m_b_gk, m_gla_norm_w, m_sb_norm_w, m_w_out, m_xattn_norm_w, m_mem_norm_w, m_w_mq, m_w_mkv, m_mq_norm_w, m_mk_norm_w, m_w_mo, m_ffn_norm_w, m_w_gate_up, m_w_down, v_mix_norm_w, v_w_in, v_w_gk_up, v_b_gk, v_gla_norm_w, v_sb_norm_w, v_w_out, v_xattn_norm_w, v_mem_norm_w, v_w_mq, v_w_mkv, v_mq_norm_w, v_mk_norm_w, v_w_mo, v_ffn_norm_w, v_w_gate_up, v_w_down):
    args = dict(locals())
    wts = {n: args[n][0] if n in BIG else args[n] for n in WEIGHTS}
    mom = {n: args["m_" + n][0] if n in BIG else args["m_" + n] for n in WEIGHTS}
    var = {n: args["v_" + n][0] if n in BIG else args["v_" + n] for n in WEIGHTS}

    c = lax.axis_index("c")
    chip = 2 * lax.axis_index("x") + lax.axis_index("y")
    wp = _pack_shards(wts, BF16).reshape(2, PACK_HALF, 1024)
    allw = lax.dynamic_update_slice(_ag_weights(wp), wp[None], (chip, 0, 0, 0))
    full = _unpack_full(allw.reshape(4, PACK_ROWS, 1024))
    full.update({n: wts[n] for n in SMALL})

    loss, grad_x, g = _local_step(x[0], mem[0], loss_target[0], full)
    loss = lax.psum(loss, ("x", "y", "c"))

    gp = _pack_grads(g).reshape(4, 2, PACK_HALF, 1024)
    part = _rs_add_halves(gp, _rs_swap_halves(gp), c.astype(jnp.int32).reshape(1))
    mine = _rs_add_chips(_rs_exchange(part), part, chip.astype(jnp.int32).reshape(1))
    theirs = _rs_share(mine)
    total = jnp.concatenate([jnp.where(c == 0, mine, theirs), jnp.where(c == 0, theirs, mine)], axis=0)
    small_g = _allreduce_small(_pack_small(g))

    off = _big_offsets()
    grads = {n: total[off[n]:off[n] + BIG_SHARD_ROWS[n]].reshape(wts[n].shape) for n in BIG}
    grads.update(_unpack_small(small_g))

    delta, new_m, new_v = {}, {}, {}
    for n in BIG:
        w2 = wts[n].reshape(-1, wts[n].shape[-1])
        d_, m_, v_ = _adamw(w2, grads[n].reshape(w2.shape), mom[n].reshape(w2.shape), var[n].reshape(w2.shape),
                            "adamw_" + n)
        delta[n], new_m[n], new_v[n] = (t.reshape((1,) + wts[n].shape) for t in (d_, m_, v_))
        grads[n] = grads[n].reshape((1,) + wts[n].shape)
    ds, ms, vs_ = _adamw(_pack_small(wts), small_g, _pack_small(mom), _pack_small(var), "adamw_small")
    for dst, src in ((delta, ds), (new_m, ms), (new_v, vs_)):
        dst.update(_unpack_small(src))

    return (loss, grad_x[None], *[grads[n] for n in WEIGHTS], *[delta[n] for n in WEIGHTS],
            *[new_m[n] for n in WEIGHTS], *[new_v[n] for n in WEIGHTS])
```

```python
import functools
import math

import numpy as np
import jax
import jax.numpy as jnp
from jax import lax
from jax.experimental import pallas as pl
from jax.experimental.pallas import tpu as pltpu

F32 = jnp.float32
BF16 = jnp.bfloat16
SDS = jax.ShapeDtypeStruct
MESH = pl.DeviceIdType.MESH

D = 1024
EPS = 1e-6
D_FF = 2816
GLA_GATE_NORM = 16.0
GLA_C = 64
MEM_HEADS = 4
MEM_HD = 256
C_QG, C_KG, C_VG, C_GG, C_QS, C_KS, C_VS, C_LR = 0, 256, 512, 1024, 1536, 2048, 2560, 3072
DIN = 3088
DIN_P = 3200
TB = 512
SBQ = 256
VMEM_LIMIT = 56 * 1024 * 1024
HIGHEST = lax.Precision.HIGHEST

ADAM_LR, ADAM_B1, ADAM_B2, ADAM_EPS, ADAM_WD, ADAM_STEP = 0.001, 0.9, 0.999, 1e-08, 0.01, 10

BIG = ("w_in", "w_gk_up", "w_out", "w_mq", "w_mkv", "w_mo", "w_gate_up", "w_down")
SMALL = ("mix_norm_w", "b_gk", "gla_norm_w", "sb_norm_w", "xattn_norm_w", "mem_norm_w", "mq_norm_w",
         "mk_norm_w", "ffn_norm_w")
WEIGHTS = ("mix_norm_w", "w_in", "w_gk_up", "b_gk", "gla_norm_w", "sb_norm_w", "w_out", "xattn_norm_w",
           "mem_norm_w", "w_mq", "w_mkv", "mq_norm_w", "mk_norm_w", "w_mo", "ffn_norm_w", "w_gate_up", "w_down")
SMALL_ROWS = 8


def _params(**kw):
    return pltpu.CompilerParams(vmem_limit_bytes=VMEM_LIMIT, **kw)


def _row(c, j=0):
    return pl.BlockSpec((TB, c), lambda i, j=j: (i, j))


def _const(shape):
    return pl.BlockSpec(shape, lambda i: (0,) * len(shape))


def _dot(a, b):
    return lax.dot_general(a, b, (((1,), (0,)), ((), ())), preferred_element_type=F32)


def _dot_nt(a, b):
    return lax.dot_general(a, b, (((1,), (1,)), ((), ())), preferred_element_type=F32)


def _dot_tn(a, b):
    return lax.dot_general(a, b, (((0,), (0,)), ((), ())), preferred_element_type=F32)


def _dot_nt_f32(a, b):
    return lax.dot_general(a, b, (((1,), (1,)), ((), ())), precision=HIGHEST, preferred_element_type=F32)


def _split3(x):
    h = x.astype(BF16)
    r = x - h.astype(F32)
    m = r.astype(BF16)
    l = (r - m.astype(F32)).astype(BF16)
    return h, m, l


def _dot_exact(x, ones_mat):
    h, m, l = _split3(x)
    return _dot(h, ones_mat) + _dot(m, ones_mat) + _dot(l, ones_mat)


def _dot_hilo(x, ones_mat):
    h = x.astype(BF16)
    l = (x - h.astype(F32)).astype(BF16)
    return _dot(h, ones_mat) + _dot(l, ones_mat)


def _softplus(z):
    return jnp.maximum(z, 0.0) + jnp.log1p(jnp.exp(-jnp.abs(z)))


def _rsqrt_ms(x):
    return lax.rsqrt(jnp.mean(x * x, axis=-1, keepdims=True) + EPS)


def _colsum8(x):
    r, c = x.shape
    return jnp.sum(x.reshape(r // 8, 8, c), axis=0)


def _matmul(a, b, *, mode, tm, tn, tk=None, res=None, out_dtype=F32, by_column_tile=False, a_spec=None,
            b_spec=None, mnk=None, name):
    if mnk is not None:
        M, N, K = mnk
    else:
        K, M = a.shape if mode == "tn" else a.shape[::-1]
        N = b.shape[0] if mode == "nt" else b.shape[1]
    tk = K if tk is None else tk
    assert M % tm == 0 and N % tn == 0 and K % tk == 0, (name, M, N, K, tm, tn, tk)
    nk = K // tk
    if a_spec is None:
        if mode == "tn":
            a_spec = pl.BlockSpec((tk, tm), lambda j, i, k: (k, i))
        else:
            a_spec = pl.BlockSpec((tm, tk), lambda j, i, k: (i, k))
    if b_spec is None:
        if mode == "nt":
            b_spec = pl.BlockSpec((tn, tk), lambda j, i, k: (j, k))
        else:
            b_spec = pl.BlockSpec((tk, tn), lambda j, i, k: (k, j))
    if by_column_tile:
        assert res is None
        o_spec = pl.BlockSpec((None, tm, tn), lambda j, i, k: (j, i, 0))
        o_shape = SDS((N // tn, M, tn), out_dtype)
    else:
        o_spec = pl.BlockSpec((tm, tn), lambda j, i, k: (i, j))
        o_shape = SDS((M, N), out_dtype)
    dot = {"nn": _dot, "nt": _dot_nt, "tn": _dot_tn}[mode]
    has_res = res is not None

    def body(*refs):
        a_ref, b_ref = refs[0], refs[1]
        res_ref = refs[2] if has_res else None
        o_ref = refs[2 + has_res]
        p = dot(a_ref[...].astype(BF16), b_ref[...].astype(BF16))
        if nk == 1:
            if has_res:
                p = p + res_ref[...]
            o_ref[...] = p.astype(out_dtype)
        else:
            acc_ref = refs[3 + has_res]
            k = pl.program_id(2)

            @pl.when(k == 0)
            def _():
                acc_ref[...] = p

            @pl.when(k > 0)
            def _():
                acc_ref[...] += p

            @pl.when(k == nk - 1)
            def _():
                t = acc_ref[...]
                if has_res:
                    t = t + res_ref[...]
                o_ref[...] = t.astype(out_dtype)

    in_specs = [a_spec, b_spec] + ([o_spec] if has_res else [])
    args = (a, b) + ((res,) if has_res else ())
    return pl.pallas_call(
        body, name=name, grid=(N // tn, M // tm, nk), in_specs=in_specs, out_specs=o_spec,
        out_shape=o_shape,
        scratch_shapes=[pltpu.VMEM((tm, tn), F32)] if nk > 1 else [],
        compiler_params=_params(dimension_semantics=("parallel", "parallel", "arbitrary")),
    )(*args)


def _norm_fwd(x, w, name):
    T, dm = x.shape

    def body(x_ref, w_ref, h_ref):
        xv = x_ref[...]
        h_ref[...] = (xv * _rsqrt_ms(xv) * w_ref[...]).astype(BF16)

    return pl.pallas_call(
        body, name=name, grid=(T // TB,), in_specs=[_row(dm), _const((1, dm))], out_specs=_row(dm),
        out_shape=SDS((T, dm), BF16), compiler_params=_params())(x, w)


def _norm_bwd(dy, x, w, dres, name):
    T, dm = x.shape

    def body(dy_ref, x_ref, w_ref, dres_ref, dx_ref, dw_ref):
        @pl.when(pl.program_id(0) == 0)
        def _():
            dw_ref[...] = jnp.zeros_like(dw_ref)

        xv = x_ref[...]
        r = _rsqrt_ms(xv)
        n = xv * r
        dyv = dy_ref[...]
        dn = dyv * w_ref[...]
        dx_ref[...] = dres_ref[...] + r * (dn - n * jnp.mean(dn * n, axis=-1, keepdims=True))
        dw_ref[...] += _colsum8(dyv * n)

    return pl.pallas_call(
        body, name=name, grid=(T // TB,),
        in_specs=[_row(dm), _row(dm), _const((1, dm)), _row(dm)],
        out_specs=[_row(dm), _const((8, dm))],
        out_shape=[SDS((T, dm), F32), SDS((8, dm), F32)], compiler_params=_params())(dy, x, w, dres)


def _proj_split(proj, wgk, bgk):
    T = proj.shape[0]

    def body(lr_ref, q_ref, k_ref, v_ref, wgk_ref, b_ref, gk_ref, qs_ref, ks_ref, vs_ref):
        u = _dot(lr_ref[...].astype(BF16), wgk_ref[...].astype(BF16)) + b_ref[...]
        gk_ref[...] = -_softplus(-u) / GLA_GATE_NORM
        qs_ref[...] = (q_ref[...] * 0.125).astype(BF16)
        ks_ref[...] = k_ref[...].astype(BF16)
        vs_ref[...] = v_ref[...].astype(BF16)

    return pl.pallas_call(
        body, name="proj_split", grid=(T // TB,),
        in_specs=[_row(128, C_LR // 128), _row(512, C_QS // 512), _row(512, C_KS // 512), _row(512, C_VS // 512),
                  _const((128, 256)), _const((1, 256))],
        out_specs=[_row(256), _row(512), _row(512), _row(512)],
        out_shape=[SDS((T, 256), F32), SDS((T, 512), BF16), SDS((T, 512), BF16), SDS((T, 512), BF16)],
        compiler_params=_params())(proj, proj, proj, proj, wgk, bgk)


def _dproj_assemble(proj, dq_g, dk_g, dv_g, dg_g, dq_s, dk_s, dv_s, dgk, wgk, bgk):
    T = proj.shape[0]

    def body(lr_ref, dqg_ref, dkg_ref, dvg_ref, dgg_ref, dqs_ref, dks_ref, dvs_ref, dgk_ref, wgk_ref, b_ref,
             dp_ref, dwgk_ref, dbgk_ref):
        @pl.when(pl.program_id(0) == 0)
        def _():
            dwgk_ref[...] = jnp.zeros_like(dwgk_ref)
            dbgk_ref[...] = jnp.zeros_like(dbgk_ref)

        lr = lr_ref[...].astype(BF16)
        wg = wgk_ref[...].astype(BF16)
        u = _dot(lr, wg) + b_ref[...]
        du = dgk_ref[...] * (jax.nn.sigmoid(-u) / GLA_GATE_NORM)
        dub = du.astype(BF16)
        dp_ref[:, C_QG:C_KG] = (dqg_ref[...] * 0.125).astype(BF16)
        dp_ref[:, C_KG:C_VG] = dkg_ref[...].astype(BF16)
        dp_ref[:, C_VG:C_GG] = dvg_ref[...].astype(BF16)
        dp_ref[:, C_GG:C_QS] = dgg_ref[...].astype(BF16)
        dp_ref[:, C_QS:C_KS] = (dqs_ref[...] * 0.125).astype(BF16)
        dp_ref[:, C_KS:C_VS] = dks_ref[...].astype(BF16)
        dp_ref[:, C_VS:C_LR] = dvs_ref[...].astype(BF16)
        dp_ref[:, C_LR:DIN_P] = _dot_nt(dub, wg).astype(BF16)
        dwgk_ref[...] += _dot_tn(lr, dub)
        dbgk_ref[...] += _colsum8(du)

    return pl.pallas_call(
        body, name="dproj_assemble", grid=(T // TB,),
        in_specs=[_row(128, C_LR // 128), _row(256), _row(256), _row(512), _row(512), _row(512), _row(512),
                  _row(512), _row(256), _const((128, 256)), _const((1, 256))],
        out_specs=[_row(DIN_P), _const((128, 256)), _const((8, 256))],
        out_shape=[SDS((T, DIN_P), BF16), SDS((128, 256), F32), SDS((8, 256), F32)],
        compiler_params=_params())(proj, dq_g, dk_g, dv_g, dg_g, dq_s, dk_s, dv_s, dgk, wgk, bgk)


def _group_ones(n, g):
    idx = np.arange(n) // g
    return jnp.asarray((idx[:, None] == idx[None, :]).astype(np.float32), dtype=BF16)


def _mix_cat(o_g, proj, o_s, wg512, ws512, bd64):
    T = o_g.shape[0]

    def body(og_ref, gg_ref, os_ref, wg_ref, ws_ref, bd_ref, cat_ref):
        og = og_ref[...]
        gg = gg_ref[...]
        s = gg * jax.nn.sigmoid(gg)
        for h in range(4):
            sl = slice(128 * h, 128 * (h + 1))
            x = og[:, sl]
            cat_ref[:, sl] = (x * _rsqrt_ms(x) * wg_ref[:, sl] * s[:, sl]).astype(BF16)
        osv = os_ref[...]
        ms = _dot_exact(osv * osv, bd_ref[...]) * (1.0 / 64.0)
        cat_ref[:, 512:1024] = (osv * lax.rsqrt(ms + EPS) * ws_ref[...]).astype(BF16)

    return pl.pallas_call(
        body, name="mix_cat", grid=(T // TB,),
        in_specs=[_row(512), _row(512, C_GG // 512), _row(512), _const((1, 512)), _const((1, 512)),
                  _const((512, 512))],
        out_specs=_row(1024), out_shape=SDS((T, 1024), BF16), compiler_params=_params())(
            o_g, proj, o_s, wg512, ws512, bd64)


def _mix_cat_bwd(dcat, o_g, proj, o_s, wg512, ws512, bd64):
    T = o_g.shape[0]

    def body(dc_ref, og_ref, gg_ref, os_ref, wg_ref, ws_ref, bd_ref, dog_ref, dgg_ref, dos_ref, dwg_ref, dws_ref):
        @pl.when(pl.program_id(0) == 0)
        def _():
            dwg_ref[...] = jnp.zeros_like(dwg_ref)
            dws_ref[...] = jnp.zeros_like(dws_ref)

        og = og_ref[...]
        gg = gg_ref[...]
        sg = jax.nn.sigmoid(gg)
        s = gg * sg
        ds = sg * (1.0 + gg * (1.0 - sg))
        for h in range(4):
            sl = slice(128 * h, 128 * (h + 1))
            x = og[:, sl]
            r = _rsqrt_ms(x)
            n = x * r
            w = wg_ref[:, sl]
            dc = dc_ref[:, sl]
            dy = dc * s[:, sl]
            dgg_ref[:, sl] = dc * (n * w) * ds[:, sl]
            dn = dy * w
            dog_ref[:, sl] = r * (dn - n * jnp.mean(dn * n, axis=-1, keepdims=True))
            dwg_ref[:, sl] += _colsum8(dy * n)
        osv = os_ref[...]
        bd = bd_ref[...]
        r = lax.rsqrt(_dot_exact(osv * osv, bd) * (1.0 / 64.0) + EPS)
        n = osv * r
        dc = dc_ref[:, 512:1024]
        dn = dc * ws_ref[...]
        dos_ref[...] = r * (dn - n * (_dot_exact(dn * n, bd) * (1.0 / 64.0)))
        dws_ref[...] += _colsum8(dc * n)

    return pl.pallas_call(
        body, name="mix_cat_bwd", grid=(T // TB,),
        in_specs=[_row(1024), _row(512), _row(512, C_GG // 512), _row(512), _const((1, 512)), _const((1, 512)),
                  _const((512, 512))],
        out_specs=[_row(512), _row(512), _row(512), _const((8, 512)), _const((8, 512))],
        out_shape=[SDS((T, 512), F32), SDS((T, 512), F32), SDS((T, 512), F32), SDS((8, 512), F32),
                   SDS((8, 512), F32)],
        compiler_params=_params())(dcat, o_g, proj, o_s, wg512, ws512, bd64)


FF_TN = 1408
DW_TK = 2048


def _gate_up_act(h, w):
    T = h.shape[0]
    nj = D_FF // FF_TN

    def body(h_ref, wg_ref, wu_ref, gu_ref, a_ref):
        hv = h_ref[...]
        g = _dot(hv, wg_ref[...])
        u = _dot(hv, wu_ref[...])
        gu_ref[0] = g
        gu_ref[1] = u
        a_ref[...] = (g * jax.nn.sigmoid(g) * u).astype(BF16)

    return pl.pallas_call(
        body, name="mm_gate_up_act", grid=(nj, T // TB),
        in_specs=[pl.BlockSpec((TB, D), lambda j, i: (i, 0)),
                  pl.BlockSpec((None, D, FF_TN), lambda j, i: (j, 0, 0)),
                  pl.BlockSpec((None, D, FF_TN), lambda j, i: (nj + j, 0, 0))],
        out_specs=[pl.BlockSpec((2, TB, FF_TN), lambda j, i: (0, i, j)),
                   pl.BlockSpec((TB, FF_TN), lambda j, i: (i, j))],
        out_shape=[SDS((2, T, D_FF), F32), SDS((T, D_FF), BF16)],
        compiler_params=_params(dimension_semantics=("parallel", "parallel")))(h, w, w)


def _down_bwd(dy, w_down, gu):
    T = dy.shape[0]
    nj = D_FF // FF_TN

    def body(dy_ref, w_ref, gu_ref, dgu_ref):
        da = _dot_nt(dy_ref[...].astype(BF16), w_ref[...])
        g = gu_ref[0]
        sg = jax.nn.sigmoid(g)
        dgu_ref[0] = (da * gu_ref[1] * (sg * (1.0 + g * (1.0 - sg)))).astype(BF16)
        dgu_ref[1] = (da * (g * sg)).astype(BF16)

    return pl.pallas_call(
        body, name="mm_down_bwd", grid=(nj, T // TB),
        in_specs=[pl.BlockSpec((TB, D), lambda j, i: (i, 0)),
                  pl.BlockSpec((FF_TN, D), lambda j, i: (j, 0)),
                  pl.BlockSpec((2, TB, FF_TN), lambda j, i: (0, i, j))],
        out_specs=pl.BlockSpec((2, TB, FF_TN), lambda j, i: (0, i, j)),
        out_shape=SDS((2, T, D_FF), BF16),
        compiler_params=_params(dimension_semantics=("parallel", "parallel")))(dy, w_down, gu)


def _loss_head(y, tgt):
    T = y.shape[0]

    def body(y_ref, t_ref, dy_ref, l_ref):
        @pl.when(pl.program_id(0) == 0)
        def _():
            l_ref[...] = jnp.zeros_like(l_ref)

        e = y_ref[...] - t_ref[...]
        dy_ref[...] = e * (1.0 / D)
        l_ref[...] += _colsum8(e * e) * (0.5 / D)

    return pl.pallas_call(
        body, name="loss_head", grid=(T // TB,), in_specs=[_row(D), _row(D)],
        out_specs=[_row(D), _const((8, D))], out_shape=[SDS((T, D), F32), SDS((8, D), F32)],
        compiler_params=_params())(y, tgt)


def _gla_consts():
    c = GLA_C
    L = np.tril(np.ones((c, c), np.float32))
    blocks = [L, L[(np.arange(c) // 16) * 16]]
    blocks += [np.repeat(L[16 * i:16 * i + 1], c, axis=0) for i in range(4)]
    blocks.append(np.repeat(L[c - 1:c], c, axis=0))
    return jnp.asarray(np.concatenate(blocks, axis=0))


@jax.custom_vjp
def _gla_lin(cm, g):
    cb = cm.astype(BF16)
    h, m, l = _split3(g)
    y = _dot(cb, h) + _dot(cb, m) + _dot(cb, l)
    return tuple(y[GLA_C * n:GLA_C * (n + 1)] for n in range(7))


def _gla_lin_fwd(cm, g):
    return _gla_lin(cm, g), cm


def _gla_lin_bwd(cm, cts):
    cb = cm.astype(BF16)
    h, m, l = _split3(jnp.concatenate(cts, axis=0))
    return None, _dot_tn(cb, h) + _dot_tn(cb, m) + _dot_tn(cb, l)


_gla_lin.defvjp(_gla_lin_fwd, _gla_lin_bwd)


GLA_SUB = 16


def _gla_scores_terms(qs, k, b, rs, blk):
    row = lax.broadcasted_iota(jnp.int32, (GLA_C, 128), 0)
    first_head = lax.broadcasted_iota(jnp.int32, (GLA_SUB, 128), 1) < 64
    keep = row < GLA_SUB * (blk + 1)
    e = jnp.where(keep, jnp.exp(jnp.where(keep, rs[blk] - b, 0.0)), 0.0)
    qb = qs[GLA_SUB * blk:GLA_SUB * (blk + 1)]
    lhs = jnp.concatenate([jnp.where(first_head, qb, 0.0), jnp.where(first_head, 0.0, qb)], axis=0)
    return lhs, e, first_head


@jax.custom_vjp
def _gla_scores(qs, k, b, r0, r1, r2, r3):
    rs = (r0, r1, r2, r3)
    tops, bottoms = [], []
    for blk in range(GLA_C // GLA_SUB):
        lhs, e, _ = _gla_scores_terms(qs, k, b, rs, blk)
        a = _dot_nt_f32(lhs, k * e)
        tops.append(a[:GLA_SUB])
        bottoms.append(a[GLA_SUB:])
    return jnp.concatenate(tops, axis=0), jnp.concatenate(bottoms, axis=0)


def _gla_scores_fwd(qs, k, b, r0, r1, r2, r3):
    return _gla_scores(qs, k, b, r0, r1, r2, r3), (qs, k, b, r0, r1, r2, r3)


def _gla_scores_bwd(saved, cts):
    qs, k, b = saved[:3]
    rs = saved[3:]
    da0, da1 = cts
    dqs, drs = [], []
    dk = jnp.zeros_like(k)
    db = jnp.zeros_like(b)
    for blk in range(GLA_C // GLA_SUB):
        lhs, e, first_head = _gla_scores_terms(qs, k, b, rs, blk)
        rows = slice(GLA_SUB * blk, GLA_SUB * (blk + 1))
        da = jnp.concatenate([da0[rows], da1[rows]], axis=0)
        dlhs = lax.dot_general(da, k * e, (((1,), (0,)), ((), ())), precision=HIGHEST, preferred_element_type=F32)
        dqs.append(jnp.where(first_head, dlhs[:GLA_SUB], dlhs[GLA_SUB:]))
        dks = lax.dot_general(da, lhs, (((0,), (0,)), ((), ())), precision=HIGHEST, preferred_element_type=F32)
        dk = dk + dks * e
        darg = dks * (k * e)
        db = db - darg
        drs.append(darg)
    return (jnp.concatenate(dqs, axis=0), dk, db, *drs)


_gla_scores.defvjp(_gla_scores_fwd, _gla_scores_bwd)


def _gla_chunk_pair(cm, q, k, g, v0, v1, st):
    c = GLA_C
    lane = lax.broadcasted_iota(jnp.int32, (c, 128), 1)
    m0 = (lane < 64).astype(F32)
    m1 = 1.0 - m0
    ri = lax.broadcasted_iota(jnp.int32, (c, c), 0)
    ci = lax.broadcasted_iota(jnp.int32, (c, c), 1)
    b, r, r0, r1, r2, r3, bl = _gla_lin(cm, g)
    a0, a1 = _gla_scores(q * jnp.exp(b - r), k, b, r0, r1, r2, r3)
    causal = ci <= ri
    a0 = jnp.where(causal, a0, 0.0)
    a1 = jnp.where(causal, a1, 0.0)
    qe = q * jnp.exp(b)
    o0 = _dot_nt(qe * m0, st) + _dot(a0, v0)
    o1 = _dot_nt(qe * m1, st) + _dot(a1, v1)
    kd = k * jnp.exp(bl - b)
    m0s = jnp.concatenate([m0, m0], axis=0)
    decay = jnp.exp(jnp.concatenate([bl, bl], axis=0))
    st_new = st * decay + m0s * _dot_tn(v0, kd) + (1.0 - m0s) * _dot_tn(v1, kd)
    return o0, o1, st_new


GLA_TB = 512


def _gla_fwd(proj, gk, cm):
    T = proj.shape[0]
    nc = GLA_TB // GLA_C

    def body(q_ref, k_ref, v_ref, g_ref, cm_ref, o_ref, st_ref, st_scr):
        @pl.when(pl.program_id(0) == 0)
        def _():
            st_scr[...] = jnp.zeros_like(st_scr)

        cmv = cm_ref[...]

        def chunk(ci, carry):
            rs = pl.ds(pl.multiple_of(ci * GLA_C, GLA_C), GLA_C)
            for p in range(2):
                ls = slice(128 * p, 128 * (p + 1))
                st = st_scr[p]
                st_ref[ci, p] = st
                o0, o1, st_new = _gla_chunk_pair(
                    cmv, q_ref[rs, ls] * 0.125, k_ref[rs, ls], g_ref[rs, ls],
                    v_ref[rs, 256 * p:256 * p + 128], v_ref[rs, 256 * p + 128:256 * p + 256], st)
                o_ref[rs, 256 * p:256 * p + 128] = o0
                o_ref[rs, 256 * p + 128:256 * p + 256] = o1
                st_scr[p] = st_new
            return carry

        lax.fori_loop(0, nc, chunk, 0)

    return pl.pallas_call(
        body, name="gla_fwd", grid=(T // GLA_TB,),
        in_specs=[pl.BlockSpec((GLA_TB, 256), lambda i: (i, C_QG // 256)),
                  pl.BlockSpec((GLA_TB, 256), lambda i: (i, C_KG // 256)),
                  pl.BlockSpec((GLA_TB, 512), lambda i: (i, C_VG // 512)),
                  pl.BlockSpec((GLA_TB, 256), lambda i: (i, 0)),
                  pl.BlockSpec((7 * GLA_C, GLA_C), lambda i: (0, 0))],
        out_specs=[pl.BlockSpec((GLA_TB, 512), lambda i: (i, 0)),
                   pl.BlockSpec((nc, 2, 128, 128), lambda i: (i, 0, 0, 0))],
        out_shape=[SDS((T, 512), F32), SDS((T // GLA_C, 2, 128, 128), F32)],
        scratch_shapes=[pltpu.VMEM((2, 128, 128), F32)],
        compiler_params=_params(dimension_semantics=("arbitrary",)))(proj, proj, proj, gk, cm)


def _gla_bwd(proj, gk, cm, states, do):
    T = proj.shape[0]
    nb = T // GLA_TB
    nc = GLA_TB // GLA_C

    def body(q_ref, k_ref, v_ref, g_ref, cm_ref, st_ref, do_ref, dq_ref, dk_ref, dv_ref, dg_ref, dst_scr):
        @pl.when(pl.program_id(0) == 0)
        def _():
            dst_scr[...] = jnp.zeros_like(dst_scr)

        cmv = cm_ref[...]

        def chunk(t, carry):
            ci = nc - 1 - t
            rs = pl.ds(pl.multiple_of(ci * GLA_C, GLA_C), GLA_C)
            for p in range(2):
                ls = slice(128 * p, 128 * (p + 1))
                _, vjp = jax.vjp(
                    functools.partial(_gla_chunk_pair, cmv),
                    q_ref[rs, ls] * 0.125, k_ref[rs, ls], g_ref[rs, ls],
                    v_ref[rs, 256 * p:256 * p + 128], v_ref[rs, 256 * p + 128:256 * p + 256], st_ref[ci, p])
                dq, dk, dg, dv0, dv1, dst = vjp((do_ref[rs, 256 * p:256 * p + 128],
                                                 do_ref[rs, 256 * p + 128:256 * p + 256], dst_scr[p]))
                dq_ref[rs, ls] = dq
                dk_ref[rs, ls] = dk
                dg_ref[rs, ls] = dg
                dv_ref[rs, 256 * p:256 * p + 128] = dv0
                dv_ref[rs, 256 * p + 128:256 * p + 256] = dv1
                dst_scr[p] = dst
            return carry

        lax.fori_loop(0, nc, chunk, 0)

    rev = lambda i: nb - 1 - i
    return pl.pallas_call(
        body, name="gla_bwd", grid=(nb,),
        in_specs=[pl.BlockSpec((GLA_TB, 256), lambda i: (rev(i), C_QG // 256)),
                  pl.BlockSpec((GLA_TB, 256), lambda i: (rev(i), C_KG // 256)),
                  pl.BlockSpec((GLA_TB, 512), lambda i: (rev(i), C_VG // 512)),
                  pl.BlockSpec((GLA_TB, 256), lambda i: (rev(i), 0)),
                  pl.BlockSpec((7 * GLA_C, GLA_C), lambda i: (0, 0)),
                  pl.BlockSpec((nc, 2, 128, 128), lambda i: (rev(i), 0, 0, 0)),
                  pl.BlockSpec((GLA_TB, 512), lambda i: (rev(i), 0))],
        out_specs=[pl.BlockSpec((GLA_TB, 256), lambda i: (rev(i), 0)),
                   pl.BlockSpec((GLA_TB, 256), lambda i: (rev(i), 0)),
                   pl.BlockSpec((GLA_TB, 512), lambda i: (rev(i), 0)),
                   pl.BlockSpec((GLA_TB, 256), lambda i: (rev(i), 0))],
        out_shape=[SDS((T, 256), F32), SDS((T, 256), F32), SDS((T, 512), F32), SDS((T, 256), F32)],
        scratch_shapes=[pltpu.VMEM((2, 128, 128), F32)],
        compiler_params=_params(dimension_semantics=("arbitrary",)))(proj, proj, proj, gk, cm, states, do)


SB_DEAD = 105.0
SB_COUNT_LANE = 127


def _sb_tri():
    i = np.arange(SBQ)
    return jnp.asarray((i[:, None] > i[None, :]).astype(np.float32), dtype=BF16)


def _sb_block_fwd(qh, kb, tri, carry, strict):
    z = _dot_nt(qh, kb)
    sp = _softplus(z)
    l1 = -sp
    if strict is not None:
        l1 = jnp.where(strict, l1, 0.0)
    log_a = (z - sp) + _dot_hilo(l1, tri) + carry
    a = jnp.exp(log_a)
    if strict is not None:
        a = jnp.where(strict, a, 0.0)
    return z - sp, l1, a


def _sb_fwd(qs, ks, vs, tri):
    T = qs.shape[0]
    nq = T // SBQ

    def body(q_ref, k_ref, v_ref, tri_ref, o_ref, c_ref):
        i = pl.program_id(1)
        lane = lax.broadcasted_iota(jnp.int32, (1, 128), 1)
        clane = lax.broadcasted_iota(jnp.int32, (SBQ, 128), 1)
        strict = (lax.broadcasted_iota(jnp.int32, (SBQ, SBQ), 1) < lax.broadcasted_iota(jnp.int32, (SBQ, SBQ), 0))
        tri_v = tri_ref[...]
        qv = q_ref[...]
        first_head = lane < 64
        qhs = (jnp.where(first_head, qv, jnp.zeros_like(qv)), jnp.where(first_head, jnp.zeros_like(qv), qv))

        def block(j, carries, accs, masked):
            rs = pl.ds(pl.multiple_of(j * SBQ, SBQ), SBQ)
            kb = k_ref[rs, :]
            vb = v_ref[rs, :]
            out_c, out_a = [], []
            for hh in range(2):
                _, l1, a = _sb_block_fwd(qhs[hh], kb, tri_v, carries[hh], strict if masked else None)
                out_a.append(accs[hh] + _dot(a.astype(BF16), vb))
                out_c.append(carries[hh] + jnp.sum(l1, axis=1, keepdims=True))
            return out_c, out_a

        zero1 = jnp.zeros((SBQ, 1), F32)
        zero128 = jnp.zeros((SBQ, 128), F32)
        (c0, c1), (a0, a1) = block(i, (zero1, zero1), (zero128, zero128), True)

        def more(state):
            return (state[0] <= i) & (jnp.maximum(jnp.max(state[1]), jnp.max(state[2])) > -SB_DEAD)

        def step(state):
            jj, c0, c1, a0, a1, t0, t1 = state
            j = i - jj
            t0 = jnp.where(clane == j, c0, t0)
            t1 = jnp.where(clane == j, c1, t1)
            (c0, c1), (a0, a1) = block(j, (c0, c1), (a0, a1), False)
            return jj + 1, c0, c1, a0, a1, t0, t1

        jj, c0, c1, a0, a1, t0, t1 = lax.while_loop(
            more, step, (jnp.int32(1), c0, c1, a0, a1, zero128, zero128))
        o_ref[...] = jnp.where(first_head, a0, a1)
        swept = (jj - 1).astype(F32)
        c_ref[0, :, 0:128] = jnp.where(clane == SB_COUNT_LANE, swept, t0)
        c_ref[0, :, 128:256] = jnp.where(clane == SB_COUNT_LANE, swept, t1)

    return pl.pallas_call(
        body, name="sb_fwd", grid=(4, nq),
        in_specs=[pl.BlockSpec((SBQ, 128), lambda h, i: (i, h)),
                  pl.BlockSpec((T, 128), lambda h, i: (0, h)),
                  pl.BlockSpec((T, 128), lambda h, i: (0, h)),
                  pl.BlockSpec((SBQ, SBQ), lambda h, i: (0, 0))],
        out_specs=[pl.BlockSpec((SBQ, 128), lambda h, i: (i, h)),
                   pl.BlockSpec((1, SBQ, 256), lambda h, i: (h, i, 0))],
        out_shape=[SDS((T, 512), F32), SDS((4, T, 256), F32)],
        compiler_params=_params(dimension_semantics=("parallel", "arbitrary")))(qs, ks, vs, tri)


def _sb_bwd(qs, ks, vs, do, carries, tri, tri_t):
    T = qs.shape[0]
    nq = T // SBQ

    def body(q_ref, k_ref, v_ref, do_ref, c_ref, tri_ref, trit_ref, dq_ref, dk_ref, dv_ref):
        i = pl.program_id(1)

        @pl.when(i == 0)
        def _():
            dk_ref[...] = jnp.zeros_like(dk_ref)
            dv_ref[...] = jnp.zeros_like(dv_ref)

        lane = lax.broadcasted_iota(jnp.int32, (1, 128), 1)
        clane = lax.broadcasted_iota(jnp.int32, (SBQ, 128), 1)
        strict = (lax.broadcasted_iota(jnp.int32, (SBQ, SBQ), 1) < lax.broadcasted_iota(jnp.int32, (SBQ, SBQ), 0))
        tri_v = tri_ref[...]
        trit_v = trit_ref[...]
        qv = q_ref[...]
        dov = do_ref[...].astype(BF16)
        first_head = lane < 64
        qhs = (jnp.where(first_head, qv, jnp.zeros_like(qv)), jnp.where(first_head, jnp.zeros_like(qv), qv))
        dohs = (jnp.where(first_head, dov, jnp.zeros_like(dov)), jnp.where(first_head, jnp.zeros_like(dov), dov))
        cts = (c_ref[0, :, 0:128], c_ref[0, :, 128:256])

        def block(j, pcarries, dqs, masked):
            rs = pl.ds(pl.multiple_of(j * SBQ, SBQ), SBQ)
            kb = k_ref[rs, :]
            vb = v_ref[rs, :]
            out_p, out_q = [], []
            dk = jnp.zeros((SBQ, 128), F32)
            dv = jnp.zeros((SBQ, 128), F32)
            for hh in range(2):
                carry = jnp.sum(jnp.where(clane == j, cts[hh], 0.0), axis=1, keepdims=True)
                lb, _, a = _sb_block_fwd(qhs[hh], kb, tri_v, carry, strict if masked else None)
                g = a * _dot_nt(dohs[hh], vb)
                p = _dot_hilo(g, trit_v) + pcarries[hh]
                dz = g - (g + p) * jnp.exp(lb)
                if masked:
                    dz = jnp.where(strict, dz, 0.0)
                dzb = dz.astype(BF16)
                dk = dk + _dot_tn(dzb, qhs[hh])
                dv = dv + _dot_tn(a.astype(BF16), dohs[hh])
                out_p.append(pcarries[hh] + jnp.sum(g, axis=1, keepdims=True))
                out_q.append(dqs[hh] + _dot(dzb, kb))
            dk_ref[rs, :] += dk
            dv_ref[rs, :] += dv
            return out_p, out_q

        def step(j, state):
            (p0, p1), (q0, q1) = block(j, (state[0], state[1]), (state[2], state[3]), False)
            return p0, p1, q0, q1

        swept = jnp.max(jnp.where(clane == SB_COUNT_LANE, cts[0], 0.0)).astype(jnp.int32)
        first = i - jnp.clip(swept, 0, i)
        zero1 = jnp.zeros((SBQ, 1), F32)
        zero128 = jnp.zeros((SBQ, 128), F32)
        p0, p1, q0, q1 = lax.fori_loop(first, i, step, (zero1, zero1, zero128, zero128))
        _, (q0, q1) = block(i, (p0, p1), (q0, q1), True)
        dq_ref[...] = jnp.where(first_head, q0, q1)

    return pl.pallas_call(
        body, name="sb_bwd", grid=(4, nq),
        in_specs=[pl.BlockSpec((SBQ, 128), lambda h, i: (i, h)),
                  pl.BlockSpec((T, 128), lambda h, i: (0, h)),
                  pl.BlockSpec((T, 128), lambda h, i: (0, h)),
                  pl.BlockSpec((SBQ, 128), lambda h, i: (i, h)),
                  pl.BlockSpec((1, SBQ, 256), lambda h, i: (h, i, 0)),
                  pl.BlockSpec((SBQ, SBQ), lambda h, i: (0, 0)),
                  pl.BlockSpec((SBQ, SBQ), lambda h, i: (0, 0))],
        out_specs=[pl.BlockSpec((SBQ, 128), lambda h, i: (i, h)),
                   pl.BlockSpec((T, 128), lambda h, i: (0, h)),
                   pl.BlockSpec((T, 128), lambda h, i: (0, h))],
        out_shape=[SDS((T, 512), F32), SDS((T, 512), F32), SDS((T, 512), F32)],
        compiler_params=_params(dimension_semantics=("parallel", "arbitrary")))(qs, ks, vs, do, carries, tri, tri_t)


def _mem_fwd(mem, mem_norm_w, w_mkv, mk_norm_w):
    M = mem.shape[0]

    def body(mem_ref, wn_ref, w_ref, wk_ref, mn_ref, kraw_ref, k_ref, v_ref):
        mv = mem_ref[...]
        mn = (mv * _rsqrt_ms(mv) * wn_ref[...]).astype(BF16)
        mn_ref[...] = mn
        for s in range(2):
            cols = slice(512 * s, 512 * (s + 1))
            ks = _dot(mn, w_ref[s, :D, :])
            kraw_ref[:, cols] = ks
            v_ref[:, cols] = _dot(mn, w_ref[2 + s, :D, :]).astype(BF16)
            for h in range(2):
                x = ks[:, MEM_HD * h:MEM_HD * (h + 1)]
                k_ref[:, 512 * s + MEM_HD * h:512 * s + MEM_HD * (h + 1)] = (
                    x * _rsqrt_ms(x) * wk_ref[...]).astype(BF16)

    vm = pl.BlockSpec(memory_space=pltpu.VMEM)
    return pl.pallas_call(
        body, name="mem_fwd", in_specs=[vm] * 4, out_specs=[vm] * 4,
        out_shape=[SDS((M, D), BF16), SDS((M, D), F32), SDS((M, D), BF16), SDS((M, D), BF16)],
        compiler_params=_params())(mem, mem_norm_w, w_mkv, mk_norm_w)


def _xattn_fwd(qraw, k, v, wq):
    T = qraw.shape[0]
    M = k.shape[0]

    def body(q_ref, k_ref, v_ref, wq_ref, o_ref):
        for h in range(MEM_HEADS):
            sl = slice(MEM_HD * h, MEM_HD * (h + 1))
            x = q_ref[:, sl]
            q = (x * _rsqrt_ms(x) * wq_ref[...]).astype(BF16)
            s = _dot_nt(q, k_ref[:, sl]) * (1.0 / math.sqrt(MEM_HD))
            s = s - jnp.max(s, axis=-1, keepdims=True)
            e = jnp.exp(s)
            p = e / jnp.sum(e, axis=-1, keepdims=True)
            o_ref[:, sl] = _dot(p.astype(BF16), v_ref[:, sl]).astype(BF16)

    return pl.pallas_call(
        body, name="xattn_fwd", grid=(T // TB,),
        in_specs=[_row(D), _const((M, D)), _const((M, D)), _const((1, MEM_HD))],
        out_specs=_row(D), out_shape=SDS((T, D), BF16), compiler_params=_params())(qraw, k, v, wq)


def _xattn_bwd(qraw, k, v, wq, do):
    T = qraw.shape[0]
    M = k.shape[0]

    def body(q_ref, k_ref, v_ref, wq_ref, do_ref, dq_ref, dk_ref, dv_ref, dw_ref):
        @pl.when(pl.program_id(0) == 0)
        def _():
            dk_ref[...] = jnp.zeros_like(dk_ref)
            dv_ref[...] = jnp.zeros_like(dv_ref)
            dw_ref[...] = jnp.zeros_like(dw_ref)

        w = wq_ref[...]
        for h in range(MEM_HEADS):
            sl = slice(MEM_HD * h, MEM_HD * (h + 1))
            x = q_ref[:, sl]
            r = _rsqrt_ms(x)
            n = x * r
            q = (n * w).astype(BF16)
            kb = k_ref[:, sl]
            s = _dot_nt(q, kb) * (1.0 / math.sqrt(MEM_HD))
            s = s - jnp.max(s, axis=-1, keepdims=True)
            e = jnp.exp(s)
            p = e / jnp.sum(e, axis=-1, keepdims=True)
            dob = do_ref[:, sl].astype(BF16)
            dp = _dot_nt(dob, v_ref[:, sl])
            ds = (p * (dp - jnp.sum(dp * p, axis=-1, keepdims=True)) * (1.0 / math.sqrt(MEM_HD))).astype(BF16)
            dv_ref[:, sl] += _dot_tn(p.astype(BF16), dob)
            dk_ref[:, sl] += _dot_tn(ds, q)
            dqn = _dot(ds, kb)
            dn = dqn * w
            dq_ref[:, sl] = r * (dn - n * jnp.mean(dn * n, axis=-1, keepdims=True))
            dw_ref[...] += _colsum8(dqn * n)

    return pl.pallas_call(
        body, name="xattn_bwd", grid=(T // TB,),
        in_specs=[_row(D), _const((M, D)), _const((M, D)), _const((1, MEM_HD)), _row(D)],
        out_specs=[_row(D), _const((M, D)), _const((M, D)), _const((8, MEM_HD))],
        out_shape=[SDS((T, D), F32), SDS((M, D), F32), SDS((M, D), F32), SDS((8, MEM_HD), F32)],
        compiler_params=_params())(qraw, k, v, wq, do)


def _mem_bwd(mem, mem_norm_w, w_mkv, mk_norm_w, mem_n, k_raw, dk, dv):
    M = mem.shape[0]

    def body(mem_ref, wn_ref, w_ref, wk_ref, mn_ref, kraw_ref, dk_ref, dv_ref, dw_ref, dwn_ref, dwk_ref, dkv_scr):
        wk = wk_ref[...]
        dwk = jnp.zeros((8, MEM_HD), F32)
        for h in range(MEM_HEADS):
            sl = slice(MEM_HD * h, MEM_HD * (h + 1))
            x = kraw_ref[:, sl]
            r = _rsqrt_ms(x)
            n = x * r
            dkh = dk_ref[:, sl]
            dn = dkh * wk
            dkv_scr[:, sl] = (r * (dn - n * jnp.mean(dn * n, axis=-1, keepdims=True))).astype(BF16)
            dwk = dwk + _colsum8(dkh * n)
        dwk_ref[...] = dwk
        dkv_scr[:, D:] = dv_ref[...].astype(BF16)
        dkv = dkv_scr[...]
        mn = mn_ref[...]
        dmn = jnp.zeros((M, D), F32)
        for s in range(4):
            part = dkv[:, 512 * s:512 * (s + 1)]
            dw_ref[s] = _dot_tn(mn, part).astype(BF16)
            dmn = dmn + _dot_nt(part, w_ref[s, :D, :])
        mv = mem_ref[...]
        dwn_ref[...] = _colsum8(dmn * (mv * _rsqrt_ms(mv)))

    vm = pl.BlockSpec(memory_space=pltpu.VMEM)
    return pl.pallas_call(
        body, name="mem_bwd", in_specs=[vm] * 8, out_specs=[vm] * 3,
        out_shape=[SDS((4, D, 512), BF16), SDS((8, D), F32), SDS((8, MEM_HD), F32)],
        scratch_shapes=[pltpu.VMEM((M, 2 * D), BF16)],
        compiler_params=_params())(mem, mem_norm_w, w_mkv, mk_norm_w, mem_n, k_raw, dk, dv)


def _local_step(x, mem, tgt, w):
    T = x.shape[0]
    wgk = jnp.zeros((128, 256), F32).at[:16].set(w["w_gk_up"].astype(F32))
    wg512 = jnp.tile(w["gla_norm_w"], (1, 4))
    ws512 = jnp.tile(w["sb_norm_w"], (1, 8))
    bd64 = _group_ones(512, 64)
    cm = _gla_consts()
    tri = _sb_tri()
    tri_t = tri.T

    h1 = _norm_fwd(x, w["mix_norm_w"], "norm1_fwd")
    proj = _matmul(h1, w["w_in"], mode="nn", tm=TB, tn=DIN_P, name="mm_proj")
    gk, qs, ks, vs = _proj_split(proj, wgk, w["b_gk"])
    o_g, states = _gla_fwd(proj, gk, cm)
    o_s, carries = _sb_fwd(qs, ks, vs, tri)
    cat = _mix_cat(o_g, proj, o_s, wg512, ws512, bd64)
    x1 = _matmul(cat, w["w_out"], mode="nn", tm=TB, tn=D, res=x, name="mm_out")
    h2 = _norm_fwd(x1, w["xattn_norm_w"], "norm2_fwd")
    qraw = _matmul(h2, w["w_mq"], mode="nn", tm=TB, tn=D, name="mm_mq")
    mem_n, k_raw, k_n, v_m = _mem_fwd(mem, w["mem_norm_w"], w["w_mkv"], w["mk_norm_w"])
    om = _xattn_fwd(qraw, k_n, v_m, w["mq_norm_w"])
    x2 = _matmul(om, w["w_mo"], mode="nn", tm=TB, tn=D, res=x1, name="mm_mo")
    h3 = _norm_fwd(x2, w["ffn_norm_w"], "norm3_fwd")
    gu, act = _gate_up_act(h3, w["w_gate_up"])
    x3 = _matmul(act, w["w_down"], mode="nn", tm=TB, tn=D, res=x2, name="mm_down")
    dx3, loss_rows = _loss_head(x3, tgt)

    g = {}
    dw_tk = min(DW_TK, T)
    dw = dict(mode="tn", tk=dw_tk, out_dtype=BF16)
    g["w_down"] = _matmul(act, dx3, tm=1408, tn=D, name="mm_dw_down", **dw)
    dgu = _down_bwd(dx3, w["w_down"], gu)
    g["w_gate_up"] = _matmul(
        h3, dgu, tm=D, tn=FF_TN, by_column_tile=True, mnk=(D, 2 * D_FF, T), name="mm_dw_gate_up",
        b_spec=pl.BlockSpec((None, dw_tk, FF_TN), lambda j, i, k: (j // 2, k, j % 2)), **dw)
    tm3 = min(2 * TB, T)
    dh3 = _matmul(dgu, w["w_gate_up"], mode="nt", tm=tm3, tn=D, tk=FF_TN, mnk=(T, D, 2 * D_FF), name="mm_dh3",
                  a_spec=pl.BlockSpec((None, tm3, FF_TN), lambda j, i, k: (k // 2, i, k % 2)),
                  b_spec=pl.BlockSpec((None, D, FF_TN), lambda j, i, k: (k, 0, 0)))
    dx2, g["ffn_norm_w"] = _norm_bwd(dh3, x2, w["ffn_norm_w"], dx3, "norm3_bwd")
    g["w_mo"] = _matmul(om, dx2, tm=D, tn=D, name="mm_dw_mo", **dw)
    dom = _matmul(dx2, w["w_mo"], mode="nt", tm=TB, tn=D, name="mm_dom")
    dqraw, dk_n, dv_m, g["mq_norm_w"] = _xattn_bwd(qraw, k_n, v_m, w["mq_norm_w"], dom)
    g["w_mkv"], g["mem_norm_w"], g["mk_norm_w"] = _mem_bwd(
        mem, w["mem_norm_w"], w["w_mkv"], w["mk_norm_w"], mem_n, k_raw, dk_n, dv_m)
    g["w_mq"] = _matmul(h2, dqraw, tm=D, tn=D, name="mm_dw_mq", **dw)
    dh2 = _matmul(dqraw, w["w_mq"], mode="nt", tm=TB, tn=D, name="mm_dh2")
    dx1, g["xattn_norm_w"] = _norm_bwd(dh2, x1, w["xattn_norm_w"], dx2, "norm2_bwd")
    g["w_out"] = _matmul(cat, dx1, tm=D, tn=D, name="mm_dw_out", **dw)
    dcat = _matmul(dx1, w["w_out"], mode="nt", tm=TB, tn=D, name="mm_dcat")
    do_g, dg_g, do_s, dwg, dws = _mix_cat_bwd(dcat, o_g, proj, o_s, wg512, ws512, bd64)
    dq_s, dk_s, dv_s = _sb_bwd(qs, ks, vs, do_s, carries, tri, tri_t)
    dq_g, dk_g, dv_g, dgk = _gla_bwd(proj, gk, cm, states, do_g)
    dproj, dwgk, g["b_gk"] = _dproj_assemble(proj, dq_g, dk_g, dv_g, dg_g, dq_s, dk_s, dv_s, dgk, wgk, w["b_gk"])
    g["w_in"] = _matmul(h1, dproj, tm=D, tn=640, name="mm_dw_in", **dw)
    dh1 = _matmul(dproj, w["w_in"], mode="nt", tm=TB, tn=D, name="mm_dh1")
    grad_x, g["mix_norm_w"] = _norm_bwd(dh1, x, w["mix_norm_w"], dx1, "norm1_bwd")

    g["w_gk_up"] = dwgk[:16]
    g["gla_norm_w"] = dwg.reshape(8, 4, 128).sum(axis=1)
    g["sb_norm_w"] = dws.reshape(8, 8, 64).sum(axis=1)
    for n in SMALL:
        g[n] = jnp.sum(g[n], axis=0, keepdims=True)
    return jnp.sum(loss_rows), grad_x, g


def _mesh_pos():
    return lax.axis_index("x"), lax.axis_index("y"), lax.axis_index("c")


def _other_chips(x, y):
    return [(1 - x, y), (x, 1 - y), (1 - x, 1 - y)]


HBM = pl.BlockSpec(memory_space=pl.ANY)


def _ag_weights(wps):
    n = len(wps)

    def body(*refs):
        w_refs, out_refs = refs[:n], refs[n:2 * n]
        send_sems, recv_sems = refs[2 * n:]
        x, y, c = _mesh_pos()
        me = 2 * x + y
        sibling = (x, y, 1 - c)
        chips = _other_chips(x, y)
        mine, theirs = c, 1 - c

        def copy(a, k, src, dst, to):
            return pltpu.make_async_remote_copy(src_ref=src, dst_ref=dst, send_sem=send_sems.at[6 * a + k],
                                                recv_sem=recv_sems.at[6 * a + k], device_id=to, device_id_type=MESH)

        first = [copy(a, k, w_refs[a].at[mine], out_refs[a].at[me, mine], (cx, cy, c))
                 for a in range(n) for k, (cx, cy) in enumerate(chips)]
        for cp in first:
            cp.start()
        passed = []
        for a in range(n):
            for k, (cx, cy) in enumerate(chips):
                slot = out_refs[a].at[2 * cx + cy, mine]
                copy(a, k, slot, slot, (cx, cy, c)).wait_recv()
                fwd = copy(a, 3 + k, slot, slot, sibling)
                fwd.start()
                passed.append(fwd)
        for a in range(n):
            for k, (cx, cy) in enumerate(chips):
                slot = out_refs[a].at[2 * cx + cy, theirs]
                copy(a, 3 + k, slot, slot, sibling).wait_recv()
        for cp in first + passed:
            cp.wait_send()

    return pl.pallas_call(
        body, name="ag_weights", in_specs=[HBM] * n, out_specs=[HBM] * n,
        out_shape=[SDS((4,) + wp.shape, wp.dtype) for wp in wps],
        scratch_shapes=[pltpu.SemaphoreType.DMA((6 * n,)), pltpu.SemaphoreType.DMA((6 * n,))],
        compiler_params=pltpu.CompilerParams(has_side_effects=True))(*wps)


def _rs_swap_halves(gps):
    n = len(gps)

    def body(*refs):
        g_refs, out_refs = refs[:n], refs[n:2 * n]
        send_sems, recv_sems = refs[2 * n:]
        x, y, c = _mesh_pos()
        copies = [pltpu.make_async_remote_copy(
            src_ref=g_refs[a].at[:, 1 - c], dst_ref=out_refs[a], send_sem=send_sems.at[a], recv_sem=recv_sems.at[a],
            device_id=(x, y, 1 - c), device_id_type=MESH) for a in range(n)]
        for cp in copies:
            cp.start()
        for cp in copies:
            cp.wait()

    return pl.pallas_call(
        body, name="rs_swap_halves", in_specs=[HBM] * n, out_specs=[HBM] * n,
        out_shape=[SDS((4,) + gp.shape[2:], gp.dtype) for gp in gps],
        scratch_shapes=[pltpu.SemaphoreType.DMA((n,)), pltpu.SemaphoreType.DMA((n,))],
        compiler_params=pltpu.CompilerParams(has_side_effects=True))(*gps)


def _rs_add_halves(gp, other, c_arr, name):
    h, w = other.shape[1:]

    def body(c_ref, a_ref, b_ref, o_ref):
        o_ref[...] = (a_ref[0].astype(F32) + b_ref[...].astype(F32)).astype(BF16)

    return pl.pallas_call(
        body, name=name,
        grid_spec=pltpu.PrefetchScalarGridSpec(
            num_scalar_prefetch=1, grid=(4,),
            in_specs=[pl.BlockSpec((1, 1, h, w), lambda s, c: (s, c[0], 0, 0)),
                      pl.BlockSpec((1, h, w), lambda s, c: (s, 0, 0))],
            out_specs=pl.BlockSpec((1, h, w), lambda s, c: (s, 0, 0))),
        out_shape=SDS((4, h, w), BF16), compiler_params=_params())(c_arr, gp, other)


def _rs_exchange(parts):
    n = len(parts)

    def body(*refs):
        p_refs, out_refs = refs[:n], refs[n:2 * n]
        send_sems, recv_sems = refs[2 * n:]
        x, y, c = _mesh_pos()
        me = 2 * x + y
        chips = _other_chips(x, y)
        sends = []
        for a in range(n):
            for k, (cx, cy) in enumerate(chips):
                cp = pltpu.make_async_remote_copy(
                    src_ref=p_refs[a].at[2 * cx + cy], dst_ref=out_refs[a].at[me], send_sem=send_sems.at[3 * a + k],
                    recv_sem=recv_sems.at[3 * a + k], device_id=(cx, cy, c), device_id_type=MESH)
                cp.start()
                sends.append(cp)
        for a in range(n):
            for k, (cx, cy) in enumerate(chips):
                slot = out_refs[a].at[2 * cx + cy]
                pltpu.make_async_remote_copy(
                    src_ref=slot, dst_ref=slot, send_sem=send_sems.at[3 * a + k], recv_sem=recv_sems.at[3 * a + k],
                    device_id=(cx, cy, c), device_id_type=MESH).wait_recv()
        for cp in sends:
            cp.wait_send()

    return pl.pallas_call(
        body, name="rs_exchange", in_specs=[HBM] * n, out_specs=[HBM] * n,
        out_shape=[SDS(p.shape, p.dtype) for p in parts],
        scratch_shapes=[pltpu.SemaphoreType.DMA((3 * n,)), pltpu.SemaphoreType.DMA((3 * n,))],
        compiler_params=pltpu.CompilerParams(has_side_effects=True))(*parts)


def _rs_add_chips(recv, part, me_arr, name):
    h, w = part.shape[1:]
    th = h // 2 if (h // 2) % 16 == 0 else h

    def body(me_ref, r_ref, p_ref, o_ref):
        me = me_ref[0]
        total = None
        for k in range(4):
            term = jnp.where(me == k, p_ref[k], r_ref[k]).astype(F32)
            total = term if total is None else total + term
        o_ref[...] = total

    spec = pl.BlockSpec((4, th, w), lambda t, me: (0, t, 0))
    return pl.pallas_call(
        body, name=name,
        grid_spec=pltpu.PrefetchScalarGridSpec(
            num_scalar_prefetch=1, grid=(h // th,), in_specs=[spec, spec],
            out_specs=pl.BlockSpec((th, w), lambda t, me: (t, 0))),
        out_shape=SDS((h, w), F32), compiler_params=_params())(me_arr, recv, part)


def _rs_share(halves):
    n = len(halves)

    def body(*refs):
        h_refs, out_refs = refs[:n], refs[n:2 * n]
        send_sems, recv_sems = refs[2 * n:]
        x, y, c = _mesh_pos()
        copies = [pltpu.make_async_remote_copy(
            src_ref=h_refs[a], dst_ref=out_refs[a], send_sem=send_sems.at[a], recv_sem=recv_sems.at[a],
            device_id=(x, y, 1 - c), device_id_type=MESH) for a in range(n)]
        for cp in copies:
            cp.start()
        for cp in copies:
            cp.wait()

    return pl.pallas_call(
        body, name="rs_share", in_specs=[HBM] * n, out_specs=[HBM] * n,
        out_shape=[SDS(hs.shape, hs.dtype) for hs in halves],
        scratch_shapes=[pltpu.SemaphoreType.DMA((n,)), pltpu.SemaphoreType.DMA((n,))],
        compiler_params=pltpu.CompilerParams(has_side_effects=True))(*halves)


def _allreduce_small(s):
    def gather(s_ref, out_ref, send_sems, recv_sems, local_sem):
        x, y, c = _mesh_pos()
        me = 4 * x + 2 * y + c
        local = pltpu.make_async_copy(s_ref, out_ref.at[me], local_sem)
        local.start()
        peers = []
        for r in range(1, 8):
            px = 1 - x if r & 4 else x
            py = 1 - y if r & 2 else y
            pc = 1 - c if r & 1 else c
            peers.append((px, py, pc))
        sends = []
        for k, peer in enumerate(peers):
            cp = pltpu.make_async_remote_copy(
                src_ref=s_ref, dst_ref=out_ref.at[me], send_sem=send_sems.at[k], recv_sem=recv_sems.at[k],
                device_id=peer, device_id_type=MESH)
            cp.start()
            sends.append(cp)
        for k, (px, py, pc) in enumerate(peers):
            slot = out_ref.at[4 * px + 2 * py + pc]
            pltpu.make_async_remote_copy(
                src_ref=slot, dst_ref=slot, send_sem=send_sems.at[k], recv_sem=recv_sems.at[k],
                device_id=(px, py, pc), device_id_type=MESH).wait_recv()
        for cp in sends:
            cp.wait_send()
        local.wait()

    hbm = pl.BlockSpec(memory_space=pl.ANY)
    parts = pl.pallas_call(
        gather, name="small_gather", in_specs=[hbm], out_specs=hbm,
        out_shape=SDS((8, SMALL_ROWS, 1024), F32),
        scratch_shapes=[pltpu.SemaphoreType.DMA((7,)), pltpu.SemaphoreType.DMA((7,)), pltpu.SemaphoreType.DMA],
        compiler_params=pltpu.CompilerParams(has_side_effects=True))(s)

    def add(p_ref, o_ref):
        total = p_ref[0]
        for k in range(1, 8):
            total = total + p_ref[k]
        o_ref[...] = total

    vm = pl.BlockSpec(memory_space=pltpu.VMEM)
    return pl.pallas_call(add, name="small_sum", in_specs=[vm], out_specs=vm,
                          out_shape=SDS((SMALL_ROWS, 1024), F32))(parts)


def _adamw(w, g, m, v, name):
    rows, cols = w.shape
    tr = rows
    for cand in (512, 352, 256):
        if rows > cand and rows % cand == 0:
            tr = cand
            break
    c1 = 1.0 - ADAM_B1 ** ADAM_STEP
    c2 = 1.0 - ADAM_B2 ** ADAM_STEP

    def body(w_ref, g_ref, m_ref, v_ref, d_ref, mo_ref, vo_ref):
        gv = g_ref[...]
        mn = ADAM_B1 * m_ref[...] + (1.0 - ADAM_B1) * gv
        vn = ADAM_B2 * v_ref[...] + (1.0 - ADAM_B2) * (gv * gv)
        mo_ref[...] = mn
        vo_ref[...] = vn
        d_ref[...] = -ADAM_LR * ((mn / c1) / (jnp.sqrt(vn / c2) + ADAM_EPS) + ADAM_WD * w_ref[...])

    spec = pl.BlockSpec((tr, cols), lambda i: (i, 0))
    return pl.pallas_call(
        body, name=name, grid=(rows // tr,), in_specs=[spec] * 4, out_specs=[spec] * 3,
        out_shape=[SDS((rows, cols), F32)] * 3, compiler_params=_params())(w, g, m, v)


SMALL_SIZES = {"mix_norm_w": 1024, "b_gk": 256, "gla_norm_w": 128, "sb_norm_w": 64, "xattn_norm_w": 1024,
               "mem_norm_w": 1024, "mq_norm_w": 256, "mk_norm_w": 256, "ffn_norm_w": 1024}


def _pack_small(d):
    flat = jnp.concatenate([d[n].reshape(-1) for n in SMALL])
    return jnp.pad(flat, (0, SMALL_ROWS * 1024 - flat.shape[0])).reshape(SMALL_ROWS, 1024)


def _unpack_small(p):
    flat = p.reshape(-1)
    out, off = {}, 0
    for n in SMALL:
        out[n] = flat[off:off + SMALL_SIZES[n]].reshape(1, SMALL_SIZES[n])
        off += SMALL_SIZES[n]
    return out


ROWS_OF = (("w_out", 256), ("w_mq", 256), ("w_mo", 256), ("w_down", 704))
MKV_ROWS = 1056


def _shard_buffers(d, dtype):
    rows = jnp.concatenate([d[n] for n, _ in ROWS_OF], axis=0).astype(dtype)
    gk = jnp.pad(d["w_gk_up"], ((0, MKV_ROWS - D - 16), (0, 512 - 64)))
    mkv = jnp.concatenate([d["w_mkv"], gk], axis=0).astype(dtype)
    return rows, d["w_gate_up"].astype(dtype), mkv, d["w_in"].astype(dtype)


def _in_halves(a):
    return a.reshape(a.shape[:-2] + (2, a.shape[-2] // 2, a.shape[-1]))


def _whole(a):
    return a.reshape(a.shape[:-3] + (2 * a.shape[-2], a.shape[-1]))


def _full_weights(rows, gate_up, mkv, win):
    out, off = {}, 0
    for n, r in ROWS_OF:
        out[n] = rows[:, off:off + r].reshape(4 * r, 1024)
        off += r
    w_in = win.transpose(1, 0, 2).reshape(D, DIN)
    out["w_in"] = jnp.concatenate([w_in[:, :1536], w_in[:, 1552:], w_in[:, 1536:1552],
                                   jnp.zeros((D, DIN_P - DIN), w_in.dtype)], axis=1)
    out["w_gk_up"] = mkv[:, D:D + 16, :64].transpose(1, 0, 2).reshape(16, 256)
    out["w_gate_up"] = gate_up
    out["w_mkv"] = mkv
    return out


def _grad_buffers(g):
    rows = jnp.concatenate([g[n].reshape(4, r, 1024) for n, r in ROWS_OF], axis=1)
    gk = g["w_gk_up"].astype(BF16).reshape(16, 4, 64).transpose(1, 0, 2)
    gk = jnp.pad(gk, ((0, 0), (0, MKV_ROWS - D - 16), (0, 512 - 64)))
    mkv = jnp.concatenate([g["w_mkv"], gk], axis=1)
    gi = g["w_in"]
    gi = jnp.concatenate([gi[:, :1536], gi[:, C_LR:C_LR + 16], gi[:, 1536:C_LR]], axis=1)
    return rows, g["w_gate_up"], mkv, gi.reshape(D, 4, DIN // 4).transpose(1, 0, 2)


def _shard_grads(rows, gate_up, mkv, win):
    out, off = {}, 0
    for n, r in ROWS_OF:
        out[n] = rows[off:off + r]
        off += r
    out["w_gate_up"] = gate_up
    out["w_mkv"] = mkv[:D]
    out["w_gk_up"] = mkv[D:D + 16, :64]
    out["w_in"] = win
    return out


def kernel(x, mem, mix_norm_w, w_in, w_gk_up, b_gk, gla_norm_w, sb_norm_w, w_out, xattn_norm_w, mem_norm_w, w_mq, w_mkv, mq_norm_w, mk_norm_w, w_mo, ffn_norm_w, w_gate_up, w_down, loss_target, m_mix_norm_w, m_w_in, m_w_gk_up, m_b_gk, m_gla_norm_w, m_sb_norm_w, m_w_out, m_xattn_norm_w, m_mem_norm_w, m_w_mq, m_w_mkv, m_mq_norm_w, m_mk_norm_w, m_w_mo, m_ffn_norm_w, m_w_gate_up, m_w_down, v_mix_norm_w, v_w_in, v_w_gk_up, v_b_gk, v_gla_norm_w, v_sb_norm_w, v_w_out, v_xattn_norm_w, v_mem_norm_w, v_w_mq, v_w_mkv, v_mq_norm_w, v_mk_norm_w, v_w_mo, v_ffn_norm_w, v_w_gate_up, v_w_down):
    args = dict(locals())
    wts = {n: args[n][0] if n in BIG else args[n] for n in WEIGHTS}
    mom = {n: args["m_" + n][0] if n in BIG else args["m_" + n] for n in WEIGHTS}
    var = {n: args["v_" + n][0] if n in BIG else args["v_" + n] for n in WEIGHTS}

    c = lax.axis_index("c")
    chip = 2 * lax.axis_index("x") + lax.axis_index("y")
    c_arr = c.astype(jnp.int32).reshape(1)
    chip_arr = chip.astype(jnp.int32).reshape(1)
    names = ("rows", "gate_up", "mkv", "win")
    wps = [_in_halves(b) for b in _shard_buffers(wts, BF16)]
    gathered = [_whole(lax.dynamic_update_slice(got, wp[None], (chip, 0, 0, 0)))
                for got, wp in zip(_ag_weights(wps), wps)]
    full = _full_weights(*gathered)
    full.update({n: wts[n] for n in SMALL})

    loss, grad_x, g = _local_step(x[0], mem[0], loss_target[0], full)
    loss = lax.psum(loss, ("x", "y", "c"))

    gps = [_in_halves(b) for b in _grad_buffers(g)]
    parts = [_rs_add_halves(gp, other, c_arr, "rs_add_halves_" + n)
             for n, gp, other in zip(names, gps, _rs_swap_halves(gps))]
    mine = [_rs_add_chips(recv, part, chip_arr, "rs_add_chips_" + n)
            for n, recv, part in zip(names, _rs_exchange(parts), parts)]
    totals = [jnp.concatenate([jnp.where(c == 0, m, t), jnp.where(c == 0, t, m)], axis=0)
              for m, t in zip(mine, _rs_share(mine))]
    small_g = _allreduce_small(_pack_small(g))

    grads = _shard_grads(*totals)
    grads.update(_unpack_small(small_g))

    delta, new_m, new_v = {}, {}, {}
    for n in BIG:
        w2 = wts[n].reshape(-1, wts[n].shape[-1])
        d_, m_, v_ = _adamw(w2, grads[n].reshape(w2.shape), mom[n].reshape(w2.shape), var[n].reshape(w2.shape),
                            "adamw_" + n)
        delta[n], new_m[n], new_v[n] = (t.reshape((1,) + wts[n].shape) for t in (d_, m_, v_))
        grads[n] = grads[n].reshape((1,) + wts[n].shape)
    ds, ms, vs_ = _adamw(_pack_small(wts), small_g, _pack_small(mom), _pack_small(var), "adamw_small")
    for dst, src in ((delta, ds), (new_m, ms), (new_v, vs_)):
        dst.update(_unpack_small(src))

    return (loss, grad_x[None], *[grads[n] for n in WEIGHTS], *[delta[n] for n in WEIGHTS],
            *[new_m[n] for n in WEIGHTS], *[new_v[n] for n in WEIGHTS])
```

```python
import functools
import math

import numpy as np
import jax
import jax.numpy as jnp
from jax import lax
from jax.experimental import pallas as pl
from jax.experimental.pallas import tpu as pltpu

F32 = jnp.float32
BF16 = jnp.bfloat16
SDS = jax.ShapeDtypeStruct
MESH = pl.DeviceIdType.MESH

D = 1024
EPS = 1e-6
D_FF = 2816
GLA_GATE_NORM = 16.0
GLA_C = 64
MEM_HEADS = 4
MEM_HD = 256
C_QG, C_KG, C_VG, C_GG, C_QS, C_KS, C_VS, C_LR = 0, 256, 512, 1024, 1536, 2048, 2560, 3072
DIN = 3088
DIN_P = 3200
TB = 512
SBQ = 256
VMEM_LIMIT = 56 * 1024 * 1024
HIGHEST = lax.Precision.HIGHEST

ADAM_LR, ADAM_B1, ADAM_B2, ADAM_EPS, ADAM_WD, ADAM_STEP = 0.001, 0.9, 0.999, 1e-08, 0.01, 10

BIG = ("w_in", "w_gk_up", "w_out", "w_mq", "w_mkv", "w_mo", "w_gate_up", "w_down")
SMALL = ("mix_norm_w", "b_gk", "gla_norm_w", "sb_norm_w", "xattn_norm_w", "mem_norm_w", "mq_norm_w",
         "mk_norm_w", "ffn_norm_w")
WEIGHTS = ("mix_norm_w", "w_in", "w_gk_up", "b_gk", "gla_norm_w", "sb_norm_w", "w_out", "xattn_norm_w",
           "mem_norm_w", "w_mq", "w_mkv", "mq_norm_w", "mk_norm_w", "w_mo", "ffn_norm_w", "w_gate_up", "w_down")
SMALL_ROWS = 8


def _params(**kw):
    return pltpu.CompilerParams(vmem_limit_bytes=VMEM_LIMIT, **kw)


def _row(c, j=0):
    return pl.BlockSpec((TB, c), lambda i, j=j: (i, j))


def _const(shape):
    return pl.BlockSpec(shape, lambda i: (0,) * len(shape))


def _dot(a, b):
    return lax.dot_general(a, b, (((1,), (0,)), ((), ())), preferred_element_type=F32)


def _dot_nt(a, b):
    return lax.dot_general(a, b, (((1,), (1,)), ((), ())), preferred_element_type=F32)


def _dot_tn(a, b):
    return lax.dot_general(a, b, (((0,), (0,)), ((), ())), preferred_element_type=F32)


def _dot_nt_f32(a, b):
    return lax.dot_general(a, b, (((1,), (1,)), ((), ())), precision=HIGHEST, preferred_element_type=F32)


def _split3(x):
    h = x.astype(BF16)
    r = x - h.astype(F32)
    m = r.astype(BF16)
    l = (r - m.astype(F32)).astype(BF16)
    return h, m, l


def _dot_exact(x, ones_mat):
    h, m, l = _split3(x)
    return _dot(h, ones_mat) + _dot(m, ones_mat) + _dot(l, ones_mat)


def _dot_hilo(x, ones_mat):
    h = x.astype(BF16)
    l = (x - h.astype(F32)).astype(BF16)
    return _dot(h, ones_mat) + _dot(l, ones_mat)


def _softplus(z):
    return jnp.maximum(z, 0.0) + jnp.log1p(jnp.exp(-jnp.abs(z)))


def _rsqrt_ms(x):
    return lax.rsqrt(jnp.mean(x * x, axis=-1, keepdims=True) + EPS)


def _colsum8(x):
    r, c = x.shape
    return jnp.sum(x.reshape(r // 8, 8, c), axis=0)


def _matmul(a, b, *, mode, tm, tn, tk=None, res=None, out_dtype=F32, by_column_tile=False, a_spec=None,
            b_spec=None, mnk=None, name):
    if mnk is not None:
        M, N, K = mnk
    else:
        K, M = a.shape if mode == "tn" else a.shape[::-1]
        N = b.shape[0] if mode == "nt" else b.shape[1]
    tk = K if tk is None else tk
    assert M % tm == 0 and N % tn == 0 and K % tk == 0, (name, M, N, K, tm, tn, tk)
    nk = K // tk
    if a_spec is None:
        if mode == "tn":
            a_spec = pl.BlockSpec((tk, tm), lambda j, i, k: (k, i))
        else:
            a_spec = pl.BlockSpec((tm, tk), lambda j, i, k: (i, k))
    if b_spec is None:
        if mode == "nt":
            b_spec = pl.BlockSpec((tn, tk), lambda j, i, k: (j, k))
        else:
            b_spec = pl.BlockSpec((tk, tn), lambda j, i, k: (k, j))
    if by_column_tile:
        assert res is None
        o_spec = pl.BlockSpec((None, tm, tn), lambda j, i, k: (j, i, 0))
        o_shape = SDS((N // tn, M, tn), out_dtype)
    else:
        o_spec = pl.BlockSpec((tm, tn), lambda j, i, k: (i, j))
        o_shape = SDS((M, N), out_dtype)
    dot = {"nn": _dot, "nt": _dot_nt, "tn": _dot_tn}[mode]
    has_res = res is not None

    def body(*refs):
        a_ref, b_ref = refs[0], refs[1]
        res_ref = refs[2] if has_res else None
        o_ref = refs[2 + has_res]
        p = dot(a_ref[...].astype(BF16), b_ref[...].astype(BF16))
        if nk == 1:
            if has_res:
                p = p + res_ref[...]
            o_ref[...] = p.astype(out_dtype)
        else:
            acc_ref = refs[3 + has_res]
            k = pl.program_id(2)

            @pl.when(k == 0)
            def _():
                acc_ref[...] = p

            @pl.when(k > 0)
            def _():
                acc_ref[...] += p

            @pl.when(k == nk - 1)
            def _():
                t = acc_ref[...]
                if has_res:
                    t = t + res_ref[...]
                o_ref[...] = t.astype(out_dtype)

    in_specs = [a_spec, b_spec] + ([o_spec] if has_res else [])
    args = (a, b) + ((res,) if has_res else ())
    return pl.pallas_call(
        body, name=name, grid=(N // tn, M // tm, nk), in_specs=in_specs, out_specs=o_spec,
        out_shape=o_shape,
        scratch_shapes=[pltpu.VMEM((tm, tn), F32)] if nk > 1 else [],
        compiler_params=_params(dimension_semantics=("parallel", "parallel", "arbitrary")),
    )(*args)


def _norm_fwd(x, w, name):
    T, dm = x.shape

    def body(x_ref, w_ref, h_ref):
        xv = x_ref[...]
        h_ref[...] = (xv * _rsqrt_ms(xv) * w_ref[...]).astype(BF16)

    return pl.pallas_call(
        body, name=name, grid=(T // TB,), in_specs=[_row(dm), _const((1, dm))], out_specs=_row(dm),
        out_shape=SDS((T, dm), BF16), compiler_params=_params())(x, w)


def _norm_bwd(dy, x, w, dres, name):
    T, dm = x.shape

    def body(dy_ref, x_ref, w_ref, dres_ref, dx_ref, dw_ref):
        @pl.when(pl.program_id(0) == 0)
        def _():
            dw_ref[...] = jnp.zeros_like(dw_ref)

        xv = x_ref[...]
        r = _rsqrt_ms(xv)
        n = xv * r
        dyv = dy_ref[...]
        dn = dyv * w_ref[...]
        dx_ref[...] = dres_ref[...] + r * (dn - n * jnp.mean(dn * n, axis=-1, keepdims=True))
        dw_ref[...] += _colsum8(dyv * n)

    return pl.pallas_call(
        body, name=name, grid=(T // TB,),
        in_specs=[_row(dm), _row(dm), _const((1, dm)), _row(dm)],
        out_specs=[_row(dm), _const((8, dm))],
        out_shape=[SDS((T, dm), F32), SDS((8, dm), F32)], compiler_params=_params())(dy, x, w, dres)


def _proj_split(proj, wgk, bgk):
    T = proj.shape[0]

    def body(lr_ref, q_ref, k_ref, v_ref, wgk_ref, b_ref, gk_ref, qs_ref, ks_ref, vs_ref):
        u = _dot(lr_ref[...].astype(BF16), wgk_ref[...].astype(BF16)) + b_ref[...]
        gk_ref[...] = -_softplus(-u) / GLA_GATE_NORM
        qs_ref[...] = (q_ref[...] * 0.125).astype(BF16)
        ks_ref[...] = k_ref[...].astype(BF16)
        vs_ref[...] = v_ref[...].astype(BF16)

    return pl.pallas_call(
        body, name="proj_split", grid=(T // TB,),
        in_specs=[_row(128, C_LR // 128), _row(512, C_QS // 512), _row(512, C_KS // 512), _row(512, C_VS // 512),
                  _const((128, 256)), _const((1, 256))],
        out_specs=[_row(256), _row(512), _row(512), _row(512)],
        out_shape=[SDS((T, 256), F32), SDS((T, 512), BF16), SDS((T, 512), BF16), SDS((T, 512), BF16)],
        compiler_params=_params())(proj, proj, proj, proj, wgk, bgk)


def _dproj_assemble(proj, dq_g, dk_g, dv_g, dg_g, dq_s, dk_s, dv_s, dgk, wgk, bgk):
    T = proj.shape[0]

    def body(lr_ref, dqg_ref, dkg_ref, dvg_ref, dgg_ref, dqs_ref, dks_ref, dvs_ref, dgk_ref, wgk_ref, b_ref,
             dp_ref, dwgk_ref, dbgk_ref):
        @pl.when(pl.program_id(0) == 0)
        def _():
            dwgk_ref[...] = jnp.zeros_like(dwgk_ref)
            dbgk_ref[...] = jnp.zeros_like(dbgk_ref)

        lr = lr_ref[...].astype(BF16)
        wg = wgk_ref[...].astype(BF16)
        u = _dot(lr, wg) + b_ref[...]
        du = dgk_ref[...] * (jax.nn.sigmoid(-u) / GLA_GATE_NORM)
        dub = du.astype(BF16)
        dp_ref[:, C_QG:C_KG] = (dqg_ref[...] * 0.125).astype(BF16)
        dp_ref[:, C_KG:C_VG] = dkg_ref[...].astype(BF16)
        dp_ref[:, C_VG:C_GG] = dvg_ref[...].astype(BF16)
        dp_ref[:, C_GG:C_QS] = dgg_ref[...].astype(BF16)
        dp_ref[:, C_QS:C_KS] = (dqs_ref[...] * 0.125).astype(BF16)
        dp_ref[:, C_KS:C_VS] = dks_ref[...].astype(BF16)
        dp_ref[:, C_VS:C_LR] = dvs_ref[...].astype(BF16)
        dp_ref[:, C_LR:DIN_P] = _dot_nt(dub, wg).astype(BF16)
        dwgk_ref[...] += _dot_tn(lr, dub)
        dbgk_ref[...] += _colsum8(du)

    return pl.pallas_call(
        body, name="dproj_assemble", grid=(T // TB,),
        in_specs=[_row(128, C_LR // 128), _row(256), _row(256), _row(512), _row(512), _row(512), _row(512),
                  _row(512), _row(256), _const((128, 256)), _const((1, 256))],
        out_specs=[_row(DIN_P), _const((128, 256)), _const((8, 256))],
        out_shape=[SDS((T, DIN_P), BF16), SDS((128, 256), F32), SDS((8, 256), F32)],
        compiler_params=_params())(proj, dq_g, dk_g, dv_g, dg_g, dq_s, dk_s, dv_s, dgk, wgk, bgk)


def _group_ones(n, g):
    idx = np.arange(n) // g
    return jnp.asarray((idx[:, None] == idx[None, :]).astype(np.float32), dtype=BF16)


def _mix_cat(o_g, proj, o_s, wg512, ws512, bd64):
    T = o_g.shape[0]

    def body(og_ref, gg_ref, os_ref, wg_ref, ws_ref, bd_ref, cat_ref):
        og = og_ref[...]
        gg = gg_ref[...]
        s = gg * jax.nn.sigmoid(gg)
        for h in range(4):
            sl = slice(128 * h, 128 * (h + 1))
            x = og[:, sl]
            cat_ref[:, sl] = (x * _rsqrt_ms(x) * wg_ref[:, sl] * s[:, sl]).astype(BF16)
        osv = os_ref[...]
        ms = _dot_exact(osv * osv, bd_ref[...]) * (1.0 / 64.0)
        cat_ref[:, 512:1024] = (osv * lax.rsqrt(ms + EPS) * ws_ref[...]).astype(BF16)

    return pl.pallas_call(
        body, name="mix_cat", grid=(T // TB,),
        in_specs=[_row(512), _row(512, C_GG // 512), _row(512), _const((1, 512)), _const((1, 512)),
                  _const((512, 512))],
        out_specs=_row(1024), out_shape=SDS((T, 1024), BF16), compiler_params=_params())(
            o_g, proj, o_s, wg512, ws512, bd64)


def _mix_cat_bwd(dcat, o_g, proj, o_s, wg512, ws512, bd64):
    T = o_g.shape[0]

    def body(dc_ref, og_ref, gg_ref, os_ref, wg_ref, ws_ref, bd_ref, dog_ref, dgg_ref, dos_ref, dwg_ref, dws_ref):
        @pl.when(pl.program_id(0) == 0)
        def _():
            dwg_ref[...] = jnp.zeros_like(dwg_ref)
            dws_ref[...] = jnp.zeros_like(dws_ref)

        og = og_ref[...]
        gg = gg_ref[...]
        sg = jax.nn.sigmoid(gg)
        s = gg * sg
        ds = sg * (1.0 + gg * (1.0 - sg))
        for h in range(4):
            sl = slice(128 * h, 128 * (h + 1))
            x = og[:, sl]
            r = _rsqrt_ms(x)
            n = x * r
            w = wg_ref[:, sl]
            dc = dc_ref[:, sl]
            dy = dc * s[:, sl]
            dgg_ref[:, sl] = dc * (n * w) * ds[:, sl]
            dn = dy * w
            dog_ref[:, sl] = r * (dn - n * jnp.mean(dn * n, axis=-1, keepdims=True))
            dwg_ref[:, sl] += _colsum8(dy * n)
        osv = os_ref[...]
        bd = bd_ref[...]
        r = lax.rsqrt(_dot_exact(osv * osv, bd) * (1.0 / 64.0) + EPS)
        n = osv * r
        dc = dc_ref[:, 512:1024]
        dn = dc * ws_ref[...]
        dos_ref[...] = r * (dn - n * (_dot_exact(dn * n, bd) * (1.0 / 64.0)))
        dws_ref[...] += _colsum8(dc * n)

    return pl.pallas_call(
        body, name="mix_cat_bwd", grid=(T // TB,),
        in_specs=[_row(1024), _row(512), _row(512, C_GG // 512), _row(512), _const((1, 512)), _const((1, 512)),
                  _const((512, 512))],
        out_specs=[_row(512), _row(512), _row(512), _const((8, 512)), _const((8, 512))],
        out_shape=[SDS((T, 512), F32), SDS((T, 512), F32), SDS((T, 512), F32), SDS((8, 512), F32),
                   SDS((8, 512), F32)],
        compiler_params=_params())(dcat, o_g, proj, o_s, wg512, ws512, bd64)


FF_TN = 1408
DW_TK = 2048


def _gate_up_act(h, w):
    T = h.shape[0]
    nj = D_FF // FF_TN

    def body(h_ref, wg_ref, wu_ref, gu_ref, a_ref):
        hv = h_ref[...]
        g = _dot(hv, wg_ref[...])
        u = _dot(hv, wu_ref[...])
        gu_ref[0] = g
        gu_ref[1] = u
        a_ref[...] = (g * jax.nn.sigmoid(g) * u).astype(BF16)

    return pl.pallas_call(
        body, name="mm_gate_up_act", grid=(nj, T // TB),
        in_specs=[pl.BlockSpec((TB, D), lambda j, i: (i, 0)),
                  pl.BlockSpec((None, D, FF_TN), lambda j, i: (j, 0, 0)),
                  pl.BlockSpec((None, D, FF_TN), lambda j, i: (nj + j, 0, 0))],
        out_specs=[pl.BlockSpec((2, TB, FF_TN), lambda j, i: (0, i, j)),
                   pl.BlockSpec((TB, FF_TN), lambda j, i: (i, j))],
        out_shape=[SDS((2, T, D_FF), F32), SDS((T, D_FF), BF16)],
        compiler_params=_params(dimension_semantics=("parallel", "parallel")))(h, w, w)


def _down_bwd(dy, w_down, gu):
    T = dy.shape[0]
    nj = D_FF // FF_TN

    def body(dy_ref, w_ref, gu_ref, dgu_ref):
        da = _dot_nt(dy_ref[...].astype(BF16), w_ref[...])
        g = gu_ref[0]
        sg = jax.nn.sigmoid(g)
        dgu_ref[0] = (da * gu_ref[1] * (sg * (1.0 + g * (1.0 - sg)))).astype(BF16)
        dgu_ref[1] = (da * (g * sg)).astype(BF16)

    return pl.pallas_call(
        body, name="mm_down_bwd", grid=(nj, T // TB),
        in_specs=[pl.BlockSpec((TB, D), lambda j, i: (i, 0)),
                  pl.BlockSpec((FF_TN, D), lambda j, i: (j, 0)),
                  pl.BlockSpec((2, TB, FF_TN), lambda j, i: (0, i, j))],
        out_specs=pl.BlockSpec((2, TB, FF_TN), lambda j, i: (0, i, j)),
        out_shape=SDS((2, T, D_FF), BF16),
        compiler_params=_params(dimension_semantics=("parallel", "parallel")))(dy, w_down, gu)


def _loss_head(y, tgt):
    T = y.shape[0]

    def body(y_ref, t_ref, dy_ref, l_ref):
        @pl.when(pl.program_id(0) == 0)
        def _():
            l_ref[...] = jnp.zeros_like(l_ref)

        e = y_ref[...] - t_ref[...]
        dy_ref[...] = e * (1.0 / D)
        l_ref[...] += _colsum8(e * e) * (0.5 / D)

    return pl.pallas_call(
        body, name="loss_head", grid=(T // TB,), in_specs=[_row(D), _row(D)],
        out_specs=[_row(D), _const((8, D))], out_shape=[SDS((T, D), F32), SDS((8, D), F32)],
        compiler_params=_params())(y, tgt)


def _gla_consts():
    c = GLA_C
    L = np.tril(np.ones((c, c), np.float32))
    blocks = [L, L[(np.arange(c) // 16) * 16]]
    blocks += [np.repeat(L[16 * i:16 * i + 1], c, axis=0) for i in range(4)]
    blocks.append(np.repeat(L[c - 1:c], c, axis=0))
    return jnp.asarray(np.concatenate(blocks, axis=0))


@jax.custom_vjp
def _gla_lin(cm, g):
    cb = cm.astype(BF16)
    h, m, l = _split3(g)
    y = _dot(cb, h) + _dot(cb, m) + _dot(cb, l)
    return tuple(y[GLA_C * n:GLA_C * (n + 1)] for n in range(7))


def _gla_lin_fwd(cm, g):
    return _gla_lin(cm, g), cm


def _gla_lin_bwd(cm, cts):
    cb = cm.astype(BF16)
    h, m, l = _split3(jnp.concatenate(cts, axis=0))
    return None, _dot_tn(cb, h) + _dot_tn(cb, m) + _dot_tn(cb, l)


_gla_lin.defvjp(_gla_lin_fwd, _gla_lin_bwd)


GLA_SUB = 16


def _gla_scores_terms(qs, k, b, rs, blk):
    row = lax.broadcasted_iota(jnp.int32, (GLA_C, 128), 0)
    first_head = lax.broadcasted_iota(jnp.int32, (GLA_SUB, 128), 1) < 64
    keep = row < GLA_SUB * (blk + 1)
    e = jnp.where(keep, jnp.exp(jnp.where(keep, rs[blk] - b, 0.0)), 0.0)
    qb = qs[GLA_SUB * blk:GLA_SUB * (blk + 1)]
    lhs = jnp.concatenate([jnp.where(first_head, qb, 0.0), jnp.where(first_head, 0.0, qb)], axis=0)
    return lhs, e, first_head


@jax.custom_vjp
def _gla_scores(qs, k, b, r0, r1, r2, r3):
    rs = (r0, r1, r2, r3)
    tops, bottoms = [], []
    for blk in range(GLA_C // GLA_SUB):
        lhs, e, _ = _gla_scores_terms(qs, k, b, rs, blk)
        a = _dot_nt_f32(lhs, k * e)
        tops.append(a[:GLA_SUB])
        bottoms.append(a[GLA_SUB:])
    return jnp.concatenate(tops, axis=0), jnp.concatenate(bottoms, axis=0)


def _gla_scores_fwd(qs, k, b, r0, r1, r2, r3):
    return _gla_scores(qs, k, b, r0, r1, r2, r3), (qs, k, b, r0, r1, r2, r3)


def _gla_scores_bwd(saved, cts):
    qs, k, b = saved[:3]
    rs = saved[3:]
    da0, da1 = cts
    dqs, drs = [], []
    dk = jnp.zeros_like(k)
    db = jnp.zeros_like(b)
    for blk in range(GLA_C // GLA_SUB):
        lhs, e, first_head = _gla_scores_terms(qs, k, b, rs, blk)
        rows = slice(GLA_SUB * blk, GLA_SUB * (blk + 1))
        da = jnp.concatenate([da0[rows], da1[rows]], axis=0)
        dlhs = lax.dot_general(da, k * e, (((1,), (0,)), ((), ())), precision=HIGHEST, preferred_element_type=F32)
        dqs.append(jnp.where(first_head, dlhs[:GLA_SUB], dlhs[GLA_SUB:]))
        dks = lax.dot_general(da, lhs, (((0,), (0,)), ((), ())), precision=HIGHEST, preferred_element_type=F32)
        dk = dk + dks * e
        darg = dks * (k * e)
        db = db - darg
        drs.append(darg)
    return (jnp.concatenate(dqs, axis=0), dk, db, *drs)


_gla_scores.defvjp(_gla_scores_fwd, _gla_scores_bwd)


def _gla_chunk_pair(cm, q, k, g, v0, v1, st):
    c = GLA_C
    lane = lax.broadcasted_iota(jnp.int32, (c, 128), 1)
    m0 = (lane < 64).astype(F32)
    m1 = 1.0 - m0
    ri = lax.broadcasted_iota(jnp.int32, (c, c), 0)
    ci = lax.broadcasted_iota(jnp.int32, (c, c), 1)
    b, r, r0, r1, r2, r3, bl = _gla_lin(cm, g)
    a0, a1 = _gla_scores(q * jnp.exp(b - r), k, b, r0, r1, r2, r3)
    causal = ci <= ri
    a0 = jnp.where(causal, a0, 0.0)
    a1 = jnp.where(causal, a1, 0.0)
    qe = q * jnp.exp(b)
    o0 = _dot_nt(qe * m0, st) + _dot(a0, v0)
    o1 = _dot_nt(qe * m1, st) + _dot(a1, v1)
    kd = k * jnp.exp(bl - b)
    m0s = jnp.concatenate([m0, m0], axis=0)
    decay = jnp.exp(jnp.concatenate([bl, bl], axis=0))
    st_new = st * decay + m0s * _dot_tn(v0, kd) + (1.0 - m0s) * _dot_tn(v1, kd)
    return o0, o1, st_new


GLA_TB = 512


def _gla_fwd(proj, gk, cm, wps=()):
    T = proj.shape[0]
    nb = T // GLA_TB
    nc = GLA_TB // GLA_C
    n = len(wps)

    def body(q_ref, k_ref, v_ref, g_ref, cm_ref, *rest):
        w_refs, (o_ref, st_ref), out_refs = rest[:n], rest[n:n + 2], rest[n + 2:2 * n + 2]
        st_scr = rest[2 * n + 2]
        step = pl.program_id(0)
        if n:
            ag_start, ag_pass_on, ag_finish = _ag_phases(w_refs, out_refs, *rest[2 * n + 3:])

        @pl.when(step == 0)
        def _():
            st_scr[...] = jnp.zeros_like(st_scr)
            if n:
                ag_start()

        cmv = cm_ref[...]

        def chunk(ci, carry):
            rs = pl.ds(pl.multiple_of(ci * GLA_C, GLA_C), GLA_C)
            for p in range(2):
                ls = slice(128 * p, 128 * (p + 1))
                st = st_scr[p]
                st_ref[ci, p] = st
                o0, o1, st_new = _gla_chunk_pair(
                    cmv, q_ref[rs, ls] * 0.125, k_ref[rs, ls], g_ref[rs, ls],
                    v_ref[rs, 256 * p:256 * p + 128], v_ref[rs, 256 * p + 128:256 * p + 256], st)
                o_ref[rs, 256 * p:256 * p + 128] = o0
                o_ref[rs, 256 * p + 128:256 * p + 256] = o1
                st_scr[p] = st_new
            return carry

        lax.fori_loop(0, nc, chunk, 0)

        if n:
            pl.when(step == max(nb - 3, 0))(ag_pass_on)
            pl.when(step == nb - 1)(ag_finish)

    return pl.pallas_call(
        body, name="gla_fwd", grid=(nb,),
        in_specs=[pl.BlockSpec((GLA_TB, 256), lambda i: (i, C_QG // 256)),
                  pl.BlockSpec((GLA_TB, 256), lambda i: (i, C_KG // 256)),
                  pl.BlockSpec((GLA_TB, 512), lambda i: (i, C_VG // 512)),
                  pl.BlockSpec((GLA_TB, 256), lambda i: (i, 0)),
                  pl.BlockSpec((7 * GLA_C, GLA_C), lambda i: (0, 0))] + [HBM] * n,
        out_specs=[pl.BlockSpec((GLA_TB, 512), lambda i: (i, 0)),
                   pl.BlockSpec((nc, 2, 128, 128), lambda i: (i, 0, 0, 0))] + [HBM] * n,
        out_shape=[SDS((T, 512), F32), SDS((T // GLA_C, 2, 128, 128), F32)]
        + [SDS((4,) + wp.shape, wp.dtype) for wp in wps],
        scratch_shapes=[pltpu.VMEM((2, 128, 128), F32)] + (_ag_sems(n) if n else []),
        compiler_params=_params(dimension_semantics=("arbitrary",)))(proj, proj, proj, gk, cm, *wps)


def _gla_bwd(proj, gk, cm, states, do, parts=()):
    T = proj.shape[0]
    nb = T // GLA_TB
    nc = GLA_TB // GLA_C
    n = len(parts)

    def body(q_ref, k_ref, v_ref, g_ref, cm_ref, st_ref, do_ref, *rest):
        p_refs, (dq_ref, dk_ref, dv_ref, dg_ref), out_refs = rest[:n], rest[n:n + 4], rest[n + 4:2 * n + 4]
        dst_scr = rest[2 * n + 4]
        step = pl.program_id(0)
        if n:
            rs_start, rs_finish = _rs_exchange_phases(p_refs, out_refs, *rest[2 * n + 5:])

        @pl.when(step == 0)
        def _():
            dst_scr[...] = jnp.zeros_like(dst_scr)
            if n:
                rs_start()

        cmv = cm_ref[...]

        def chunk(t, carry):
            ci = nc - 1 - t
            rs = pl.ds(pl.multiple_of(ci * GLA_C, GLA_C), GLA_C)
            for p in range(2):
                ls = slice(128 * p, 128 * (p + 1))
                _, vjp = jax.vjp(
                    functools.partial(_gla_chunk_pair, cmv),
                    q_ref[rs, ls] * 0.125, k_ref[rs, ls], g_ref[rs, ls],
                    v_ref[rs, 256 * p:256 * p + 128], v_ref[rs, 256 * p + 128:256 * p + 256], st_ref[ci, p])
                dq, dk, dg, dv0, dv1, dst = vjp((do_ref[rs, 256 * p:256 * p + 128],
                                                 do_ref[rs, 256 * p + 128:256 * p + 256], dst_scr[p]))
                dq_ref[rs, ls] = dq
                dk_ref[rs, ls] = dk
                dg_ref[rs, ls] = dg
                dv_ref[rs, 256 * p:256 * p + 128] = dv0
                dv_ref[rs, 256 * p + 128:256 * p + 256] = dv1
                dst_scr[p] = dst
            return carry

        lax.fori_loop(0, nc, chunk, 0)

        if n:
            pl.when(step == nb - 1)(rs_finish)

    rev = lambda i: nb - 1 - i
    return pl.pallas_call(
        body, name="gla_bwd", grid=(nb,),
        in_specs=[pl.BlockSpec((GLA_TB, 256), lambda i: (rev(i), C_QG // 256)),
                  pl.BlockSpec((GLA_TB, 256), lambda i: (rev(i), C_KG // 256)),
                  pl.BlockSpec((GLA_TB, 512), lambda i: (rev(i), C_VG // 512)),
                  pl.BlockSpec((GLA_TB, 256), lambda i: (rev(i), 0)),
                  pl.BlockSpec((7 * GLA_C, GLA_C), lambda i: (0, 0)),
                  pl.BlockSpec((nc, 2, 128, 128), lambda i: (rev(i), 0, 0, 0)),
                  pl.BlockSpec((GLA_TB, 512), lambda i: (rev(i), 0))] + [HBM] * n,
        out_specs=[pl.BlockSpec((GLA_TB, 256), lambda i: (rev(i), 0)),
                   pl.BlockSpec((GLA_TB, 256), lambda i: (rev(i), 0)),
                   pl.BlockSpec((GLA_TB, 512), lambda i: (rev(i), 0)),
                   pl.BlockSpec((GLA_TB, 256), lambda i: (rev(i), 0))] + [HBM] * n,
        out_shape=[SDS((T, 256), F32), SDS((T, 256), F32), SDS((T, 512), F32), SDS((T, 256), F32)]
        + [SDS(p.shape, p.dtype) for p in parts],
        scratch_shapes=[pltpu.VMEM((2, 128, 128), F32)] + (_rs_exchange_sems(n) if n else []),
        compiler_params=_params(dimension_semantics=("arbitrary",)))(proj, proj, proj, gk, cm, states, do, *parts)


SB_DEAD = 105.0
SB_COUNT_LANE = 127


def _sb_tri():
    i = np.arange(SBQ)
    return jnp.asarray((i[:, None] > i[None, :]).astype(np.float32), dtype=BF16)


def _sb_block_fwd(qh, kb, tri, carry, strict):
    z = _dot_nt(qh, kb)
    sp = _softplus(z)
    l1 = -sp
    if strict is not None:
        l1 = jnp.where(strict, l1, 0.0)
    log_a = (z - sp) + _dot_hilo(l1, tri) + carry
    a = jnp.exp(log_a)
    if strict is not None:
        a = jnp.where(strict, a, 0.0)
    return z - sp, l1, a


def _sb_fwd(qs, ks, vs, tri):
    T = qs.shape[0]
    nq = T // SBQ

    def body(q_ref, k_ref, v_ref, tri_ref, o_ref, c_ref):
        i = pl.program_id(1)
        lane = lax.broadcasted_iota(jnp.int32, (1, 128), 1)
        clane = lax.broadcasted_iota(jnp.int32, (SBQ, 128), 1)
        strict = (lax.broadcasted_iota(jnp.int32, (SBQ, SBQ), 1) < lax.broadcasted_iota(jnp.int32, (SBQ, SBQ), 0))
        tri_v = tri_ref[...]
        qv = q_ref[...]
        first_head = lane < 64
        qhs = (jnp.where(first_head, qv, jnp.zeros_like(qv)), jnp.where(first_head, jnp.zeros_like(qv), qv))

        def block(j, carries, accs, masked):
            rs = pl.ds(pl.multiple_of(j * SBQ, SBQ), SBQ)
            kb = k_ref[rs, :]
            vb = v_ref[rs, :]
            out_c, out_a = [], []
            for hh in range(2):
                _, l1, a = _sb_block_fwd(qhs[hh], kb, tri_v, carries[hh], strict if masked else None)
                out_a.append(accs[hh] + _dot(a.astype(BF16), vb))
                out_c.append(carries[hh] + jnp.sum(l1, axis=1, keepdims=True))
            return out_c, out_a

        zero1 = jnp.zeros((SBQ, 1), F32)
        zero128 = jnp.zeros((SBQ, 128), F32)
        (c0, c1), (a0, a1) = block(i, (zero1, zero1), (zero128, zero128), True)

        def more(state):
            return (state[0] <= i) & (jnp.maximum(jnp.max(state[1]), jnp.max(state[2])) > -SB_DEAD)

        def step(state):
            jj, c0, c1, a0, a1, t0, t1 = state
            j = i - jj
            t0 = jnp.where(clane == j, c0, t0)
            t1 = jnp.where(clane == j, c1, t1)
            (c0, c1), (a0, a1) = block(j, (c0, c1), (a0, a1), False)
            return jj + 1, c0, c1, a0, a1, t0, t1

        jj, c0, c1, a0, a1, t0, t1 = lax.while_loop(
            more, step, (jnp.int32(1), c0, c1, a0, a1, zero128, zero128))
        o_ref[...] = jnp.where(first_head, a0, a1)
        swept = (jj - 1).astype(F32)
        c_ref[0, :, 0:128] = jnp.where(clane == SB_COUNT_LANE, swept, t0)
        c_ref[0, :, 128:256] = jnp.where(clane == SB_COUNT_LANE, swept, t1)

    return pl.pallas_call(
        body, name="sb_fwd", grid=(4, nq),
        in_specs=[pl.BlockSpec((SBQ, 128), lambda h, i: (i, h)),
                  pl.BlockSpec((T, 128), lambda h, i: (0, h)),
                  pl.BlockSpec((T, 128), lambda h, i: (0, h)),
                  pl.BlockSpec((SBQ, SBQ), lambda h, i: (0, 0))],
        out_specs=[pl.BlockSpec((SBQ, 128), lambda h, i: (i, h)),
                   pl.BlockSpec((1, SBQ, 256), lambda h, i: (h, i, 0))],
        out_shape=[SDS((T, 512), F32), SDS((4, T, 256), F32)],
        compiler_params=_params(dimension_semantics=("parallel", "arbitrary")))(qs, ks, vs, tri)


def _sb_bwd(qs, ks, vs, do, carries, tri, tri_t):
    T = qs.shape[0]
    nq = T // SBQ

    def body(q_ref, k_ref, v_ref, do_ref, c_ref, tri_ref, trit_ref, dq_ref, dk_ref, dv_ref):
        i = pl.program_id(1)

        @pl.when(i == 0)
        def _():
            dk_ref[...] = jnp.zeros_like(dk_ref)
            dv_ref[...] = jnp.zeros_like(dv_ref)

        lane = lax.broadcasted_iota(jnp.int32, (1, 128), 1)
        clane = lax.broadcasted_iota(jnp.int32, (SBQ, 128), 1)
        strict = (lax.broadcasted_iota(jnp.int32, (SBQ, SBQ), 1) < lax.broadcasted_iota(jnp.int32, (SBQ, SBQ), 0))
        tri_v = tri_ref[...]
        trit_v = trit_ref[...]
        qv = q_ref[...]
        dov = do_ref[...].astype(BF16)
        first_head = lane < 64
        qhs = (jnp.where(first_head, qv, jnp.zeros_like(qv)), jnp.where(first_head, jnp.zeros_like(qv), qv))
        dohs = (jnp.where(first_head, dov, jnp.zeros_like(dov)), jnp.where(first_head, jnp.zeros_like(dov), dov))
        cts = (c_ref[0, :, 0:128], c_ref[0, :, 128:256])

        def block(j, pcarries, dqs, masked):
            rs = pl.ds(pl.multiple_of(j * SBQ, SBQ), SBQ)
            kb = k_ref[rs, :]
            vb = v_ref[rs, :]
            out_p, out_q = [], []
            dk = jnp.zeros((SBQ, 128), F32)
            dv = jnp.zeros((SBQ, 128), F32)
            for hh in range(2):
                carry = jnp.sum(jnp.where(clane == j, cts[hh], 0.0), axis=1, keepdims=True)
                lb, _, a = _sb_block_fwd(qhs[hh], kb, tri_v, carry, strict if masked else None)
                g = a * _dot_nt(dohs[hh], vb)
                p = _dot_hilo(g, trit_v) + pcarries[hh]
                dz = g - (g + p) * jnp.exp(lb)
                if masked:
                    dz = jnp.where(strict, dz, 0.0)
                dzb = dz.astype(BF16)
                dk = dk + _dot_tn(dzb, qhs[hh])
                dv = dv + _dot_tn(a.astype(BF16), dohs[hh])
                out_p.append(pcarries[hh] + jnp.sum(g, axis=1, keepdims=True))
                out_q.append(dqs[hh] + _dot(dzb, kb))
            dk_ref[rs, :] += dk
            dv_ref[rs, :] += dv
            return out_p, out_q

        def step(j, state):
            (p0, p1), (q0, q1) = block(j, (state[0], state[1]), (state[2], state[3]), False)
            return p0, p1, q0, q1

        swept = jnp.max(jnp.where(clane == SB_COUNT_LANE, cts[0], 0.0)).astype(jnp.int32)
        first = i - jnp.clip(swept, 0, i)
        zero1 = jnp.zeros((SBQ, 1), F32)
        zero128 = jnp.zeros((SBQ, 128), F32)
        p0, p1, q0, q1 = lax.fori_loop(first, i, step, (zero1, zero1, zero128, zero128))
        _, (q0, q1) = block(i, (p0, p1), (q0, q1), True)
        dq_ref[...] = jnp.where(first_head, q0, q1)

    return pl.pallas_call(
        body, name="sb_bwd", grid=(4, nq),
        in_specs=[pl.BlockSpec((SBQ, 128), lambda h, i: (i, h)),
                  pl.BlockSpec((T, 128), lambda h, i: (0, h)),
                  pl.BlockSpec((T, 128), lambda h, i: (0, h)),
                  pl.BlockSpec((SBQ, 128), lambda h, i: (i, h)),
                  pl.BlockSpec((1, SBQ, 256), lambda h, i: (h, i, 0)),
                  pl.BlockSpec((SBQ, SBQ), lambda h, i: (0, 0)),
                  pl.BlockSpec((SBQ, SBQ), lambda h, i: (0, 0))],
        out_specs=[pl.BlockSpec((SBQ, 128), lambda h, i: (i, h)),
                   pl.BlockSpec((T, 128), lambda h, i: (0, h)),
                   pl.BlockSpec((T, 128), lambda h, i: (0, h))],
        out_shape=[SDS((T, 512), F32), SDS((T, 512), F32), SDS((T, 512), F32)],
        compiler_params=_params(dimension_semantics=("parallel", "arbitrary")))(qs, ks, vs, do, carries, tri, tri_t)


def _mem_fwd(mem, mem_norm_w, w_mkv, mk_norm_w):
    M = mem.shape[0]

    def body(mem_ref, wn_ref, w_ref, wk_ref, mn_ref, kraw_ref, k_ref, v_ref):
        mv = mem_ref[...]
        mn = (mv * _rsqrt_ms(mv) * wn_ref[...]).astype(BF16)
        mn_ref[...] = mn
        for s in range(2):
            cols = slice(512 * s, 512 * (s + 1))
            ks = _dot(mn, w_ref[s, :D, :])
            kraw_ref[:, cols] = ks
            v_ref[:, cols] = _dot(mn, w_ref[2 + s, :D, :]).astype(BF16)
            for h in range(2):
                x = ks[:, MEM_HD * h:MEM_HD * (h + 1)]
                k_ref[:, 512 * s + MEM_HD * h:512 * s + MEM_HD * (h + 1)] = (
                    x * _rsqrt_ms(x) * wk_ref[...]).astype(BF16)

    vm = pl.BlockSpec(memory_space=pltpu.VMEM)
    return pl.pallas_call(
        body, name="mem_fwd", in_specs=[vm] * 4, out_specs=[vm] * 4,
        out_shape=[SDS((M, D), BF16), SDS((M, D), F32), SDS((M, D), BF16), SDS((M, D), BF16)],
        compiler_params=_params())(mem, mem_norm_w, w_mkv, mk_norm_w)


def _xattn_fwd(qraw, k, v, wq):
    T = qraw.shape[0]
    M = k.shape[0]

    def body(q_ref, k_ref, v_ref, wq_ref, o_ref):
        for h in range(MEM_HEADS):
            sl = slice(MEM_HD * h, MEM_HD * (h + 1))
            x = q_ref[:, sl]
            q = (x * _rsqrt_ms(x) * wq_ref[...]).astype(BF16)
            s = _dot_nt(q, k_ref[:, sl]) * (1.0 / math.sqrt(MEM_HD))
            s = s - jnp.max(s, axis=-1, keepdims=True)
            e = jnp.exp(s)
            p = e / jnp.sum(e, axis=-1, keepdims=True)
            o_ref[:, sl] = _dot(p.astype(BF16), v_ref[:, sl]).astype(BF16)

    return pl.pallas_call(
        body, name="xattn_fwd", grid=(T // TB,),
        in_specs=[_row(D), _const((M, D)), _const((M, D)), _const((1, MEM_HD))],
        out_specs=_row(D), out_shape=SDS((T, D), BF16), compiler_params=_params())(qraw, k, v, wq)


def _xattn_bwd(qraw, k, v, wq, do):
    T = qraw.shape[0]
    M = k.shape[0]

    def body(q_ref, k_ref, v_ref, wq_ref, do_ref, dq_ref, dk_ref, dv_ref, dw_ref):
        @pl.when(pl.program_id(0) == 0)
        def _():
            dk_ref[...] = jnp.zeros_like(dk_ref)
            dv_ref[...] = jnp.zeros_like(dv_ref)
            dw_ref[...] = jnp.zeros_like(dw_ref)

        w = wq_ref[...]
        for h in range(MEM_HEADS):
            sl = slice(MEM_HD * h, MEM_HD * (h + 1))
            x = q_ref[:, sl]
            r = _rsqrt_ms(x)
            n = x * r
            q = (n * w).astype(BF16)
            kb = k_ref[:, sl]
            s = _dot_nt(q, kb) * (1.0 / math.sqrt(MEM_HD))
            s = s - jnp.max(s, axis=-1, keepdims=True)
            e = jnp.exp(s)
            p = e / jnp.sum(e, axis=-1, keepdims=True)
            dob = do_ref[:, sl].astype(BF16)
            dp = _dot_nt(dob, v_ref[:, sl])
            ds = (p * (dp - jnp.sum(dp * p, axis=-1, keepdims=True)) * (1.0 / math.sqrt(MEM_HD))).astype(BF16)
            dv_ref[:, sl] += _dot_tn(p.astype(BF16), dob)
            dk_ref[:, sl] += _dot_tn(ds, q)
            dqn = _dot(ds, kb)
            dn = dqn * w
            dq_ref[:, sl] = r * (dn - n * jnp.mean(dn * n, axis=-1, keepdims=True))
            dw_ref[...] += _colsum8(dqn * n)

    return pl.pallas_call(
        body, name="xattn_bwd", grid=(T // TB,),
        in_specs=[_row(D), _const((M, D)), _const((M, D)), _const((1, MEM_HD)), _row(D)],
        out_specs=[_row(D), _const((M, D)), _const((M, D)), _const((8, MEM_HD))],
        out_shape=[SDS((T, D), F32), SDS((M, D), F32), SDS((M, D), F32), SDS((8, MEM_HD), F32)],
        compiler_params=_params())(qraw, k, v, wq, do)


def _mem_bwd(mem, mem_norm_w, w_mkv, mk_norm_w, mem_n, k_raw, dk, dv):
    M = mem.shape[0]

    def body(mem_ref, wn_ref, w_ref, wk_ref, mn_ref, kraw_ref, dk_ref, dv_ref, dw_ref, dwn_ref, dwk_ref, dkv_scr):
        wk = wk_ref[...]
        dwk = jnp.zeros((8, MEM_HD), F32)
        for h in range(MEM_HEADS):
            sl = slice(MEM_HD * h, MEM_HD * (h + 1))
            x = kraw_ref[:, sl]
            r = _rsqrt_ms(x)
            n = x * r
            dkh = dk_ref[:, sl]
            dn = dkh * wk
            dkv_scr[:, sl] = (r * (dn - n * jnp.mean(dn * n, axis=-1, keepdims=True))).astype(BF16)
            dwk = dwk + _colsum8(dkh * n)
        dwk_ref[...] = dwk
        dkv_scr[:, D:] = dv_ref[...].astype(BF16)
        dkv = dkv_scr[...]
        mn = mn_ref[...]
        dmn = jnp.zeros((M, D), F32)
        for s in range(4):
            part = dkv[:, 512 * s:512 * (s + 1)]
            dw_ref[s] = _dot_tn(mn, part).astype(BF16)
            dmn = dmn + _dot_nt(part, w_ref[s, :D, :])
        mv = mem_ref[...]
        dwn_ref[...] = _colsum8(dmn * (mv * _rsqrt_ms(mv)))

    vm = pl.BlockSpec(memory_space=pltpu.VMEM)
    return pl.pallas_call(
        body, name="mem_bwd", in_specs=[vm] * 8, out_specs=[vm] * 3,
        out_shape=[SDS((4, D, 512), BF16), SDS((8, D), F32), SDS((8, MEM_HD), F32)],
        scratch_shapes=[pltpu.VMEM((M, 2 * D), BF16)],
        compiler_params=_params())(mem, mem_norm_w, w_mkv, mk_norm_w, mem_n, k_raw, dk, dv)


def _local_step(x, mem, tgt, w, later_weights=None, later_partials=None):
    T = x.shape[0]
    wgk = jnp.zeros((128, 256), F32).at[:16].set(w["w_gk_up"].astype(F32))
    wg512 = jnp.tile(w["gla_norm_w"], (1, 4))
    ws512 = jnp.tile(w["sb_norm_w"], (1, 8))
    bd64 = _group_ones(512, 64)
    cm = _gla_consts()
    tri = _sb_tri()
    tri_t = tri.T

    h1 = _norm_fwd(x, w["mix_norm_w"], "norm1_fwd")
    proj = _matmul(h1, w["w_in"], mode="nn", tm=TB, tn=DIN_P, name="mm_proj")
    gk, qs, ks, vs = _proj_split(proj, wgk, w["b_gk"])
    if later_weights is None:
        o_g, states = _gla_fwd(proj, gk, cm)
    else:
        o_g, states, *gathered = _gla_fwd(proj, gk, cm, later_weights[0])
        w = {**w, **later_weights[1](gathered)}
    o_s, carries = _sb_fwd(qs, ks, vs, tri)
    cat = _mix_cat(o_g, proj, o_s, wg512, ws512, bd64)
    x1 = _matmul(cat, w["w_out"], mode="nn", tm=TB, tn=D, res=x, name="mm_out")
    h2 = _norm_fwd(x1, w["xattn_norm_w"], "norm2_fwd")
    qraw = _matmul(h2, w["w_mq"], mode="nn", tm=TB, tn=D, name="mm_mq")
    mem_n, k_raw, k_n, v_m = _mem_fwd(mem, w["mem_norm_w"], w["w_mkv"], w["mk_norm_w"])
    om = _xattn_fwd(qraw, k_n, v_m, w["mq_norm_w"])
    x2 = _matmul(om, w["w_mo"], mode="nn", tm=TB, tn=D, res=x1, name="mm_mo")
    h3 = _norm_fwd(x2, w["ffn_norm_w"], "norm3_fwd")
    gu, act = _gate_up_act(h3, w["w_gate_up"])
    x3 = _matmul(act, w["w_down"], mode="nn", tm=TB, tn=D, res=x2, name="mm_down")
    dx3, loss_rows = _loss_head(x3, tgt)

    g = {}
    dw_tk = min(DW_TK, T)
    dw = dict(mode="tn", tk=dw_tk, out_dtype=BF16)
    g["w_down"] = _matmul(act, dx3, tm=1408, tn=D, name="mm_dw_down", **dw)
    dgu = _down_bwd(dx3, w["w_down"], gu)
    g["w_gate_up"] = _matmul(
        h3, dgu, tm=D, tn=FF_TN, by_column_tile=True, mnk=(D, 2 * D_FF, T), name="mm_dw_gate_up",
        b_spec=pl.BlockSpec((None, dw_tk, FF_TN), lambda j, i, k: (j // 2, k, j % 2)), **dw)
    tm3 = min(2 * TB, T)
    dh3 = _matmul(dgu, w["w_gate_up"], mode="nt", tm=tm3, tn=D, tk=FF_TN, mnk=(T, D, 2 * D_FF), name="mm_dh3",
                  a_spec=pl.BlockSpec((None, tm3, FF_TN), lambda j, i, k: (k // 2, i, k % 2)),
                  b_spec=pl.BlockSpec((None, D, FF_TN), lambda j, i, k: (k, 0, 0)))
    dx2, g["ffn_norm_w"] = _norm_bwd(dh3, x2, w["ffn_norm_w"], dx3, "norm3_bwd")
    g["w_mo"] = _matmul(om, dx2, tm=D, tn=D, name="mm_dw_mo", **dw)
    dom = _matmul(dx2, w["w_mo"], mode="nt", tm=TB, tn=D, name="mm_dom")
    dqraw, dk_n, dv_m, g["mq_norm_w"] = _xattn_bwd(qraw, k_n, v_m, w["mq_norm_w"], dom)
    g["w_mkv"], g["mem_norm_w"], g["mk_norm_w"] = _mem_bwd(
        mem, w["mem_norm_w"], w["w_mkv"], w["mk_norm_w"], mem_n, k_raw, dk_n, dv_m)
    g["w_mq"] = _matmul(h2, dqraw, tm=D, tn=D, name="mm_dw_mq", **dw)
    dh2 = _matmul(dqraw, w["w_mq"], mode="nt", tm=TB, tn=D, name="mm_dh2")
    dx1, g["xattn_norm_w"] = _norm_bwd(dh2, x1, w["xattn_norm_w"], dx2, "norm2_bwd")
    g["w_out"] = _matmul(cat, dx1, tm=D, tn=D, name="mm_dw_out", **dw)
    dcat = _matmul(dx1, w["w_out"], mode="nt", tm=TB, tn=D, name="mm_dcat")
    do_g, dg_g, do_s, dwg, dws = _mix_cat_bwd(dcat, o_g, proj, o_s, wg512, ws512, bd64)
    dq_s, dk_s, dv_s = _sb_bwd(qs, ks, vs, do_s, carries, tri, tri_t)
    parts = () if later_partials is None else later_partials(g)
    dq_g, dk_g, dv_g, dgk, *received = _gla_bwd(proj, gk, cm, states, do_g, parts)
    dproj, dwgk, g["b_gk"] = _dproj_assemble(proj, dq_g, dk_g, dv_g, dg_g, dq_s, dk_s, dv_s, dgk, wgk, w["b_gk"])
    g["w_in"] = _matmul(h1, dproj, tm=D, tn=640, name="mm_dw_in", **dw)
    dh1 = _matmul(dproj, w["w_in"], mode="nt", tm=TB, tn=D, name="mm_dh1")
    grad_x, g["mix_norm_w"] = _norm_bwd(dh1, x, w["mix_norm_w"], dx1, "norm1_bwd")

    g["w_gk_up"] = dwgk[:16]
    g["gla_norm_w"] = dwg.reshape(8, 4, 128).sum(axis=1)
    g["sb_norm_w"] = dws.reshape(8, 8, 64).sum(axis=1)
    for n in SMALL:
        g[n] = jnp.sum(g[n], axis=0, keepdims=True)
    if later_partials is None:
        return jnp.sum(loss_rows), grad_x, g
    return jnp.sum(loss_rows), grad_x, g, (parts, received)


def _mesh_pos():
    return lax.axis_index("x"), lax.axis_index("y"), lax.axis_index("c")


def _other_chips(x, y):
    return [(1 - x, y), (x, 1 - y), (1 - x, 1 - y)]


HBM = pl.BlockSpec(memory_space=pl.ANY)


def _ag_phases(w_refs, out_refs, send_sems, recv_sems):
    n = len(w_refs)
    x, y, c = _mesh_pos()
    me = 2 * x + y
    sibling = (x, y, 1 - c)
    chips = _other_chips(x, y)
    mine, theirs = c, 1 - c

    def copy(a, k, src, dst, to):
        return pltpu.make_async_remote_copy(src_ref=src, dst_ref=dst, send_sem=send_sems.at[6 * a + k],
                                            recv_sem=recv_sems.at[6 * a + k], device_id=to, device_id_type=MESH)

    def firsts():
        return [copy(a, k, w_refs[a].at[mine], out_refs[a].at[me, mine], (cx, cy, c))
                for a in range(n) for k, (cx, cy) in enumerate(chips)]

    def landed(a, k, half):
        cx, cy = chips[k]
        return out_refs[a].at[2 * cx + cy, half]

    def passes():
        return [copy(a, 3 + k, landed(a, k, mine), landed(a, k, mine), sibling) for a in range(n) for k in range(3)]

    def start():
        for cp in firsts():
            cp.start()

    def pass_on():
        for a in range(n):
            for k, (cx, cy) in enumerate(chips):
                copy(a, k, landed(a, k, mine), landed(a, k, mine), (cx, cy, c)).wait_recv()
                copy(a, 3 + k, landed(a, k, mine), landed(a, k, mine), sibling).start()

    def finish():
        for a in range(n):
            for k in range(3):
                copy(a, 3 + k, landed(a, k, theirs), landed(a, k, theirs), sibling).wait_recv()
        for cp in firsts() + passes():
            cp.wait_send()

    return start, pass_on, finish


def _ag_sems(n):
    return [pltpu.SemaphoreType.DMA((6 * n,)), pltpu.SemaphoreType.DMA((6 * n,))]


def _ag_weights(wps):
    n = len(wps)

    def body(*refs):
        for phase in _ag_phases(refs[:n], refs[n:2 * n], *refs[2 * n:]):
            phase()

    return pl.pallas_call(
        body, name="ag_weights", in_specs=[HBM] * n, out_specs=[HBM] * n,
        out_shape=[SDS((4,) + wp.shape, wp.dtype) for wp in wps], scratch_shapes=_ag_sems(n),
        compiler_params=pltpu.CompilerParams(has_side_effects=True))(*wps)


def _rs_swap_halves(gps, name):
    n = len(gps)

    def body(*refs):
        g_refs, out_refs = refs[:n], refs[n:2 * n]
        send_sems, recv_sems = refs[2 * n:]
        x, y, c = _mesh_pos()
        copies = [pltpu.make_async_remote_copy(
            src_ref=g_refs[a].at[:, 1 - c], dst_ref=out_refs[a], send_sem=send_sems.at[a], recv_sem=recv_sems.at[a],
            device_id=(x, y, 1 - c), device_id_type=MESH) for a in range(n)]
        for cp in copies:
            cp.start()
        for cp in copies:
            cp.wait()

    return pl.pallas_call(
        body, name=name, in_specs=[HBM] * n, out_specs=[HBM] * n,
        out_shape=[SDS((4,) + gp.shape[2:], gp.dtype) for gp in gps],
        scratch_shapes=[pltpu.SemaphoreType.DMA((n,)), pltpu.SemaphoreType.DMA((n,))],
        compiler_params=pltpu.CompilerParams(has_side_effects=True))(*gps)


def _rs_add_halves(gp, other, c_arr, name):
    h, w = other.shape[1:]

    def body(c_ref, a_ref, b_ref, o_ref):
        o_ref[...] = (a_ref[0].astype(F32) + b_ref[...].astype(F32)).astype(BF16)

    return pl.pallas_call(
        body, name=name,
        grid_spec=pltpu.PrefetchScalarGridSpec(
            num_scalar_prefetch=1, grid=(4,),
            in_specs=[pl.BlockSpec((1, 1, h, w), lambda s, c: (s, c[0], 0, 0)),
                      pl.BlockSpec((1, h, w), lambda s, c: (s, 0, 0))],
            out_specs=pl.BlockSpec((1, h, w), lambda s, c: (s, 0, 0))),
        out_shape=SDS((4, h, w), BF16), compiler_params=_params())(c_arr, gp, other)


def _rs_exchange_phases(p_refs, out_refs, send_sems, recv_sems):
    n = len(p_refs)
    x, y, c = _mesh_pos()
    me = 2 * x + y
    chips = _other_chips(x, y)

    def sends():
        return [pltpu.make_async_remote_copy(
            src_ref=p_refs[a].at[2 * cx + cy], dst_ref=out_refs[a].at[me], send_sem=send_sems.at[3 * a + k],
            recv_sem=recv_sems.at[3 * a + k], device_id=(cx, cy, c), device_id_type=MESH)
            for a in range(n) for k, (cx, cy) in enumerate(chips)]

    def start():
        for cp in sends():
            cp.start()

    def finish():
        for a in range(n):
            for k, (cx, cy) in enumerate(chips):
                slot = out_refs[a].at[2 * cx + cy]
                pltpu.make_async_remote_copy(
                    src_ref=slot, dst_ref=slot, send_sem=send_sems.at[3 * a + k], recv_sem=recv_sems.at[3 * a + k],
                    device_id=(cx, cy, c), device_id_type=MESH).wait_recv()
        for cp in sends():
            cp.wait_send()

    return start, finish


def _rs_exchange_sems(n):
    return [pltpu.SemaphoreType.DMA((3 * n,)), pltpu.SemaphoreType.DMA((3 * n,))]


def _rs_exchange(parts):
    n = len(parts)

    def body(*refs):
        for phase in _rs_exchange_phases(refs[:n], refs[n:2 * n], *refs[2 * n:]):
            phase()

    return pl.pallas_call(
        body, name="rs_exchange", in_specs=[HBM] * n, out_specs=[HBM] * n,
        out_shape=[SDS(p.shape, p.dtype) for p in parts], scratch_shapes=_rs_exchange_sems(n),
        compiler_params=pltpu.CompilerParams(has_side_effects=True))(*parts)


def _rs_add_chips(recv, part, me_arr, name):
    h, w = part.shape[1:]
    th = h // 2 if (h // 2) % 16 == 0 else h

    def body(me_ref, r_ref, p_ref, o_ref):
        me = me_ref[0]
        total = None
        for k in range(4):
            term = jnp.where(me == k, p_ref[k], r_ref[k]).astype(F32)
            total = term if total is None else total + term
        o_ref[...] = total

    spec = pl.BlockSpec((4, th, w), lambda t, me: (0, t, 0))
    return pl.pallas_call(
        body, name=name,
        grid_spec=pltpu.PrefetchScalarGridSpec(
            num_scalar_prefetch=1, grid=(h // th,), in_specs=[spec, spec],
            out_specs=pl.BlockSpec((th, w), lambda t, me: (t, 0))),
        out_shape=SDS((h, w), F32), compiler_params=_params())(me_arr, recv, part)


def _rs_share(halves):
    n = len(halves)

    def body(*refs):
        h_refs, out_refs = refs[:n], refs[n:2 * n]
        send_sems, recv_sems = refs[2 * n:]
        x, y, c = _mesh_pos()
        copies = [pltpu.make_async_remote_copy(
            src_ref=h_refs[a], dst_ref=out_refs[a], send_sem=send_sems.at[a], recv_sem=recv_sems.at[a],
            device_id=(x, y, 1 - c), device_id_type=MESH) for a in range(n)]
        for cp in copies:
            cp.start()
        for cp in copies:
            cp.wait()

    return pl.pallas_call(
        body, name="rs_share", in_specs=[HBM] * n, out_specs=[HBM] * n,
        out_shape=[SDS(hs.shape, hs.dtype) for hs in halves],
        scratch_shapes=[pltpu.SemaphoreType.DMA((n,)), pltpu.SemaphoreType.DMA((n,))],
        compiler_params=pltpu.CompilerParams(has_side_effects=True))(*halves)


def _allreduce_small(s):
    def gather(s_ref, out_ref, send_sems, recv_sems, local_sem):
        x, y, c = _mesh_pos()
        me = 4 * x + 2 * y + c
        local = pltpu.make_async_copy(s_ref, out_ref.at[me], local_sem)
        local.start()
        peers = []
        for r in range(1, 8):
            px = 1 - x if r & 4 else x
            py = 1 - y if r & 2 else y
            pc = 1 - c if r & 1 else c
            peers.append((px, py, pc))
        sends = []
        for k, peer in enumerate(peers):
            cp = pltpu.make_async_remote_copy(
                src_ref=s_ref, dst_ref=out_ref.at[me], send_sem=send_sems.at[k], recv_sem=recv_sems.at[k],
                device_id=peer, device_id_type=MESH)
            cp.start()
            sends.append(cp)
        for k, (px, py, pc) in enumerate(peers):
            slot = out_ref.at[4 * px + 2 * py + pc]
            pltpu.make_async_remote_copy(
                src_ref=slot, dst_ref=slot, send_sem=send_sems.at[k], recv_sem=recv_sems.at[k],
                device_id=(px, py, pc), device_id_type=MESH).wait_recv()
        for cp in sends:
            cp.wait_send()
        local.wait()

    hbm = pl.BlockSpec(memory_space=pl.ANY)
    parts = pl.pallas_call(
        gather, name="small_gather", in_specs=[hbm], out_specs=hbm,
        out_shape=SDS((8, SMALL_ROWS, 1024), F32),
        scratch_shapes=[pltpu.SemaphoreType.DMA((7,)), pltpu.SemaphoreType.DMA((7,)), pltpu.SemaphoreType.DMA],
        compiler_params=pltpu.CompilerParams(has_side_effects=True))(s)

    def add(p_ref, o_ref):
        total = p_ref[0]
        for k in range(1, 8):
            total = total + p_ref[k]
        o_ref[...] = total

    vm = pl.BlockSpec(memory_space=pltpu.VMEM)
    return pl.pallas_call(add, name="small_sum", in_specs=[vm], out_specs=vm,
                          out_shape=SDS((SMALL_ROWS, 1024), F32))(parts)


def _adamw(w, g, m, v, name):
    rows, cols = w.shape
    tr = rows
    for cand in (512, 352, 256):
        if rows > cand and rows % cand == 0:
            tr = cand
            break
    c1 = 1.0 - ADAM_B1 ** ADAM_STEP
    c2 = 1.0 - ADAM_B2 ** ADAM_STEP

    def body(w_ref, g_ref, m_ref, v_ref, d_ref, mo_ref, vo_ref):
        gv = g_ref[...]
        mn = ADAM_B1 * m_ref[...] + (1.0 - ADAM_B1) * gv
        vn = ADAM_B2 * v_ref[...] + (1.0 - ADAM_B2) * (gv * gv)
        mo_ref[...] = mn
        vo_ref[...] = vn
        d_ref[...] = -ADAM_LR * ((mn / c1) / (jnp.sqrt(vn / c2) + ADAM_EPS) + ADAM_WD * w_ref[...])

    spec = pl.BlockSpec((tr, cols), lambda i: (i, 0))
    return pl.pallas_call(
        body, name=name, grid=(rows // tr,), in_specs=[spec] * 4, out_specs=[spec] * 3,
        out_shape=[SDS((rows, cols), F32)] * 3, compiler_params=_params())(w, g, m, v)


SMALL_SIZES = {"mix_norm_w": 1024, "b_gk": 256, "gla_norm_w": 128, "sb_norm_w": 64, "xattn_norm_w": 1024,
               "mem_norm_w": 1024, "mq_norm_w": 256, "mk_norm_w": 256, "ffn_norm_w": 1024}


def _pack_small(d):
    flat = jnp.concatenate([d[n].reshape(-1) for n in SMALL])
    return jnp.pad(flat, (0, SMALL_ROWS * 1024 - flat.shape[0])).reshape(SMALL_ROWS, 1024)


def _unpack_small(p):
    flat = p.reshape(-1)
    out, off = {}, 0
    for n in SMALL:
        out[n] = flat[off:off + SMALL_SIZES[n]].reshape(1, SMALL_SIZES[n])
        off += SMALL_SIZES[n]
    return out


ROWS_OF = (("w_out", 256), ("w_mq", 256), ("w_mo", 256), ("w_down", 704))
WIN_ROWS = 1056
LATER = ("rows", "gate_up", "mkv")


def _shard_buffers(d, dtype):
    rows = jnp.concatenate([d[n] for n, _ in ROWS_OF], axis=0).astype(dtype)
    gk = jnp.pad(d["w_gk_up"], ((0, WIN_ROWS - D - 16), (0, DIN // 4 - 64)))
    win = jnp.concatenate([d["w_in"], gk], axis=0).astype(dtype)
    return [rows, d["w_gate_up"].astype(dtype), d["w_mkv"].astype(dtype)], win


def _in_halves(a):
    return a.reshape(a.shape[:-2] + (2, a.shape[-2] // 2, a.shape[-1]))


def _whole(a):
    return a.reshape(a.shape[:-3] + (2 * a.shape[-2], a.shape[-1]))


def _first_weights(win):
    w_in = win[:, :D].transpose(1, 0, 2).reshape(D, DIN)
    w_in = jnp.concatenate([w_in[:, :1536], w_in[:, 1552:], w_in[:, 1536:1552],
                            jnp.zeros((D, DIN_P - DIN), w_in.dtype)], axis=1)
    return {"w_in": w_in, "w_gk_up": win[:, D:D + 16, :64].transpose(1, 0, 2).reshape(16, 256)}


def _later_weights(rows, gate_up, mkv):
    out, off = {"w_gate_up": gate_up, "w_mkv": mkv}, 0
    for n, r in ROWS_OF:
        out[n] = rows[:, off:off + r].reshape(4 * r, 1024)
        off += r
    return out


def _later_grad_buffers(g):
    rows = jnp.concatenate([g[n].reshape(4, r, 1024) for n, r in ROWS_OF], axis=1)
    return [rows, g["w_gate_up"], g["w_mkv"]]


def _win_grad_buffer(g):
    gk = g["w_gk_up"].astype(BF16).reshape(16, 4, 64).transpose(1, 0, 2)
    gk = jnp.pad(gk, ((0, 0), (0, WIN_ROWS - D - 16), (0, DIN // 4 - 64)))
    gi = g["w_in"]
    gi = jnp.concatenate([gi[:, :1536], gi[:, C_LR:C_LR + 16], gi[:, 1536:C_LR]], axis=1)
    return jnp.concatenate([gi.reshape(D, 4, DIN // 4).transpose(1, 0, 2), gk], axis=1)


def _shard_grads(rows, gate_up, mkv, win):
    out, off = {"w_gate_up": gate_up, "w_mkv": mkv, "w_in": win[:D], "w_gk_up": win[D:D + 16, :64]}, 0
    for n, r in ROWS_OF:
        out[n] = rows[off:off + r]
        off += r
    return out


def kernel(x, mem, mix_norm_w, w_in, w_gk_up, b_gk, gla_norm_w, sb_norm_w, w_out, xattn_norm_w, mem_norm_w, w_mq, w_mkv, mq_norm_w, mk_norm_w, w_mo, ffn_norm_w, w_gate_up, w_down, loss_target, m_mix_norm_w, m_w_in, m_w_gk_up, m_b_gk, m_gla_norm_w, m_sb_norm_w, m_w_out, m_xattn_norm_w, m_mem_norm_w, m_w_mq, m_w_mkv, m_mq_norm_w, m_mk_norm_w, m_w_mo, m_ffn_norm_w, m_w_gate_up, m_w_down, v_mix_norm_w, v_w_in, v_w_gk_up, v_b_gk, v_gla_norm_w, v_sb_norm_w, v_w_out, v_xattn_norm_w, v_mem_norm_w, v_w_mq, v_w_mkv, v_mq_norm_w, v_mk_norm_w, v_w_mo, v_ffn_norm_w, v_w_gate_up, v_w_down):
    args = dict(locals())
    wts = {n: args[n][0] if n in BIG else args[n] for n in WEIGHTS}
    mom = {n: args["m_" + n][0] if n in BIG else args["m_" + n] for n in WEIGHTS}
    var = {n: args["v_" + n][0] if n in BIG else args["v_" + n] for n in WEIGHTS}

    c = lax.axis_index("c")
    chip = 2 * lax.axis_index("x") + lax.axis_index("y")
    c_arr = c.astype(jnp.int32).reshape(1)
    chip_arr = chip.astype(jnp.int32).reshape(1)
    def own_slot_filled(gathered, mine):
        return [_whole(lax.dynamic_update_slice(got, wp[None], (chip, 0, 0, 0))) for got, wp in zip(gathered, mine)]

    def chip_partials(names, buffers, tag):
        gps = [_in_halves(b) for b in buffers]
        return [_rs_add_halves(gp, other, c_arr, "rs_add_halves_" + n)
                for n, gp, other in zip(names, gps, _rs_swap_halves(gps, "rs_swap_halves_" + tag))]

    later_wps, win_wp = _shard_buffers(wts, BF16)
    later_wps, win_wp = [_in_halves(b) for b in later_wps], _in_halves(win_wp)
    first = _first_weights(*own_slot_filled(_ag_weights([win_wp]), [win_wp]))
    first.update({n: wts[n] for n in SMALL})

    loss, grad_x, g, (later_parts, later_recv) = _local_step(
        x[0], mem[0], loss_target[0], first,
        later_weights=(later_wps, lambda gathered: _later_weights(*own_slot_filled(gathered, later_wps))),
        later_partials=lambda g: chip_partials(LATER, _later_grad_buffers(g), "later"))
    loss = lax.psum(loss, ("x", "y", "c"))

    win_parts = chip_partials(("win",), [_win_grad_buffer(g)], "win")
    names = LATER + ("win",)
    mine = [_rs_add_chips(recv, part, chip_arr, "rs_add_chips_" + n)
            for n, recv, part in zip(names, list(later_recv) + list(_rs_exchange(win_parts)),
                                     list(later_parts) + win_parts)]
    totals = [jnp.concatenate([jnp.where(c == 0, m, t), jnp.where(c == 0, t, m)], axis=0)
              for m, t in zip(mine, _rs_share(mine))]
    small_g = _allreduce_small(_pack_small(g))

    grads = _shard_grads(*totals)
    grads.update(_unpack_small(small_g))

    delta, new_m, new_v = {}, {}, {}
    for n in BIG:
        w2 = wts[n].reshape(-1, wts[n].shape[-1])
        d_, m_, v_ = _adamw(w2, grads[n].reshape(w2.shape), mom[n].reshape(w2.shape), var[n].reshape(w2.shape),
                            "adamw_" + n)
        delta[n], new_m[n], new_v[n] = (t.reshape((1,) + wts[n].shape) for t in (d_, m_, v_))
        grads[n] = grads[n].reshape((1,) + wts[n].shape)
    ds, ms, vs_ = _adamw(_pack_small(wts), small_g, _pack_small(mom), _pack_small(var), "adamw_small")
    for dst, src in ((delta, ds), (new_m, ms), (new_v, vs_)):
        dst.update(_unpack_small(src))

    return (loss, grad_x[None], *[grads[n] for n in WEIGHTS], *[delta[n] for n in WEIGHTS],
            *[new_m[n] for n in WEIGHTS], *[new_v[n] for n in WEIGHTS])
```

```python
import functools
import math

import numpy as np
import jax
import jax.numpy as jnp
from jax import lax
from jax.experimental import pallas as pl
from jax.experimental.pallas import tpu as pltpu

F32 = jnp.float32
BF16 = jnp.bfloat16
SDS = jax.ShapeDtypeStruct
MESH = pl.DeviceIdType.MESH

D = 1024
EPS = 1e-6
D_FF = 2816
GLA_GATE_NORM = 16.0
GLA_C = 64
MEM_HEADS = 4
MEM_HD = 256
C_QG, C_KG, C_VG, C_GG, C_QS, C_KS, C_VS, C_LR = 0, 256, 512, 1024, 1536, 2048, 2560, 3072
DIN = 3088
DIN_P = 3200
TB = 512
SBQ = 256
VMEM_LIMIT = 56 * 1024 * 1024
HIGHEST = lax.Precision.HIGHEST

ADAM_LR, ADAM_B1, ADAM_B2, ADAM_EPS, ADAM_WD, ADAM_STEP = 0.001, 0.9, 0.999, 1e-08, 0.01, 10

BIG = ("w_in", "w_gk_up", "w_out", "w_mq", "w_mkv", "w_mo", "w_gate_up", "w_down")
SMALL = ("mix_norm_w", "b_gk", "gla_norm_w", "sb_norm_w", "xattn_norm_w", "mem_norm_w", "mq_norm_w",
         "mk_norm_w", "ffn_norm_w")
WEIGHTS = ("mix_norm_w", "w_in", "w_gk_up", "b_gk", "gla_norm_w", "sb_norm_w", "w_out", "xattn_norm_w",
           "mem_norm_w", "w_mq", "w_mkv", "mq_norm_w", "mk_norm_w", "w_mo", "ffn_norm_w", "w_gate_up", "w_down")
SMALL_ROWS = 8


def _params(**kw):
    return pltpu.CompilerParams(vmem_limit_bytes=VMEM_LIMIT, **kw)


def _row(c, j=0):
    return pl.BlockSpec((TB, c), lambda i, j=j: (i, j))


def _const(shape):
    return pl.BlockSpec(shape, lambda i: (0,) * len(shape))


def _dot(a, b):
    return lax.dot_general(a, b, (((1,), (0,)), ((), ())), preferred_element_type=F32)


def _dot_nt(a, b):
    return lax.dot_general(a, b, (((1,), (1,)), ((), ())), preferred_element_type=F32)


def _dot_tn(a, b):
    return lax.dot_general(a, b, (((0,), (0,)), ((), ())), preferred_element_type=F32)


def _dot_nt_f32(a, b):
    return lax.dot_general(a, b, (((1,), (1,)), ((), ())), precision=HIGHEST, preferred_element_type=F32)


def _split3(x):
    h = x.astype(BF16)
    r = x - h.astype(F32)
    m = r.astype(BF16)
    l = (r - m.astype(F32)).astype(BF16)
    return h, m, l


def _dot_exact(x, ones_mat):
    h, m, l = _split3(x)
    return _dot(h, ones_mat) + _dot(m, ones_mat) + _dot(l, ones_mat)


def _dot_hilo(x, ones_mat):
    h = x.astype(BF16)
    l = (x - h.astype(F32)).astype(BF16)
    return _dot(h, ones_mat) + _dot(l, ones_mat)


def _softplus(z):
    return jnp.maximum(z, 0.0) + jnp.log1p(jnp.exp(-jnp.abs(z)))


def _rsqrt_ms(x):
    return lax.rsqrt(jnp.mean(x * x, axis=-1, keepdims=True) + EPS)


def _colsum8(x):
    r, c = x.shape
    return jnp.sum(x.reshape(r // 8, 8, c), axis=0)


def _matmul(a, b, *, mode, tm, tn, tk=None, res=None, out_dtype=F32, by_column_tile=False, a_spec=None,
            b_spec=None, mnk=None, epilogue=None, name):
    if mnk is not None:
        M, N, K = mnk
    else:
        K, M = a.shape if mode == "tn" else a.shape[::-1]
        N = b.shape[0] if mode == "nt" else b.shape[1]
    tk = K if tk is None else tk
    assert M % tm == 0 and N % tn == 0 and K % tk == 0, (name, M, N, K, tm, tn, tk)
    nk = K // tk
    if a_spec is None:
        if mode == "tn":
            a_spec = pl.BlockSpec((tk, tm), lambda j, i, k: (k, i))
        else:
            a_spec = pl.BlockSpec((tm, tk), lambda j, i, k: (i, k))
    if b_spec is None:
        if mode == "nt":
            b_spec = pl.BlockSpec((tn, tk), lambda j, i, k: (j, k))
        else:
            b_spec = pl.BlockSpec((tk, tn), lambda j, i, k: (k, j))
    if by_column_tile:
        assert res is None
        o_spec = pl.BlockSpec((None, tm, tn), lambda j, i, k: (j, i, 0))
        o_shape = SDS((N // tn, M, tn), out_dtype)
    else:
        o_spec = pl.BlockSpec((tm, tn), lambda j, i, k: (i, j))
        o_shape = SDS((M, N), out_dtype)
    dot = {"nn": _dot, "nt": _dot_nt, "tn": _dot_tn}[mode]
    has_res = res is not None
    extra_in, extra_out, finish = epilogue if epilogue is not None else ((), (), None)
    n_in, n_out = 2 + has_res + len(extra_in), 1 + len(extra_out)

    def body(*refs):
        a_ref, b_ref = refs[0], refs[1]
        res_ref = refs[2] if has_res else None
        in_refs, out_refs = refs[2 + has_res:n_in], refs[n_in:n_in + n_out]
        first_row_tile = pl.program_id(1) == 0

        def done(t):
            if has_res:
                t = t + res_ref[...]
            if finish is None:
                out_refs[0][...] = t.astype(out_dtype)
            else:
                finish(t, first_row_tile, in_refs, out_refs)

        p = dot(a_ref[...].astype(BF16), b_ref[...].astype(BF16))
        if nk == 1:
            done(p)
        else:
            acc_ref = refs[n_in + n_out]
            k = pl.program_id(2)

            @pl.when(k == 0)
            def _():
                acc_ref[...] = p

            @pl.when(k > 0)
            def _():
                acc_ref[...] += p

            @pl.when(k == nk - 1)
            def _():
                done(acc_ref[...])

    in_specs = [a_spec, b_spec] + ([o_spec] if has_res else []) + [s for _, s in extra_in]
    args = (a, b) + ((res,) if has_res else ()) + tuple(x for x, _ in extra_in)
    outs = pl.pallas_call(
        body, name=name, grid=(N // tn, M // tm, nk), in_specs=in_specs,
        out_specs=[o_spec] + [s for _, s in extra_out], out_shape=[o_shape] + [s for s, _ in extra_out],
        scratch_shapes=[pltpu.VMEM((tm, tn), F32)] if nk > 1 else [],
        compiler_params=_params(dimension_semantics=("parallel", "parallel", "arbitrary")),
    )(*args)
    return outs[0] if epilogue is None else outs


def _full_row(tm, c):
    return pl.BlockSpec((tm, c), lambda j, i, k: (i, 0))


def _kept(shape):
    return pl.BlockSpec(shape, lambda j, i, k: (0,) * len(shape))


def _then_norm_fwd(w, T, tm):
    dm = w.shape[1]

    def finish(t, first, ins, outs):
        outs[0][...] = t
        outs[1][...] = (t * _rsqrt_ms(t) * ins[0][...]).astype(BF16)

    return [(w, _kept((1, dm)))], [(SDS((T, dm), BF16), _full_row(tm, dm))], finish


def _then_norm_bwd(x, w, dres, tm):
    T, dm = x.shape

    def finish(t, first, ins, outs):
        x_ref, w_ref, dres_ref = ins

        @pl.when(first)
        def _():
            outs[1][...] = jnp.zeros_like(outs[1])

        xv = x_ref[...]
        r = _rsqrt_ms(xv)
        n = xv * r
        dn = t * w_ref[...]
        outs[0][...] = dres_ref[...] + r * (dn - n * jnp.mean(dn * n, axis=-1, keepdims=True))
        outs[1][...] += _colsum8(t * n)

    return ([(x, _full_row(tm, dm)), (w, _kept((1, dm))), (dres, _full_row(tm, dm))],
            [(SDS((8, dm), F32), _kept((8, dm)))], finish)


def _then_loss(tgt, tm):
    def finish(t, first, ins, outs):
        @pl.when(first)
        def _():
            outs[1][...] = jnp.zeros_like(outs[1])

        e = t - ins[0][...]
        outs[0][...] = e * (1.0 / D)
        outs[1][...] += _colsum8(e * e) * (0.5 / D)

    return [(tgt, _full_row(tm, D))], [(SDS((8, D), F32), _kept((8, D)))], finish


def _norm_fwd(x, w, name):
    T, dm = x.shape

    def body(x_ref, w_ref, h_ref):
        xv = x_ref[...]
        h_ref[...] = (xv * _rsqrt_ms(xv) * w_ref[...]).astype(BF16)

    return pl.pallas_call(
        body, name=name, grid=(T // TB,), in_specs=[_row(dm), _const((1, dm))], out_specs=_row(dm),
        out_shape=SDS((T, dm), BF16), compiler_params=_params())(x, w)


def _then_proj_split(wgk, bgk, T, tm):
    def finish(t, first, ins, outs):
        wgk_ref, b_ref = ins
        proj_ref, gk_ref, qs_ref, ks_ref, vs_ref = outs
        proj_ref[...] = t
        u = _dot(t[:, C_LR:DIN_P].astype(BF16), wgk_ref[...].astype(BF16)) + b_ref[...]
        gk_ref[...] = -_softplus(-u) / GLA_GATE_NORM
        qs_ref[...] = (t[:, C_QS:C_KS] * 0.125).astype(BF16)
        ks_ref[...] = t[:, C_KS:C_VS].astype(BF16)
        vs_ref[...] = t[:, C_VS:C_LR].astype(BF16)

    return ([(wgk, _kept((128, 256))), (bgk, _kept((1, 256)))],
            [(SDS((T, 256), F32), _full_row(tm, 256))] + [(SDS((T, 512), BF16), _full_row(tm, 512))] * 3, finish)


def _dproj_assemble(proj, dq_g, dk_g, dv_g, dg_g, dq_s, dk_s, dv_s, dgk, wgk, bgk):
    T = proj.shape[0]

    def body(lr_ref, dqg_ref, dkg_ref, dvg_ref, dgg_ref, dqs_ref, dks_ref, dvs_ref, dgk_ref, wgk_ref, b_ref,
             dp_ref, dwgk_ref, dbgk_ref):
        @pl.when(pl.program_id(0) == 0)
        def _():
            dwgk_ref[...] = jnp.zeros_like(dwgk_ref)
            dbgk_ref[...] = jnp.zeros_like(dbgk_ref)

        lr = lr_ref[...].astype(BF16)
        wg = wgk_ref[...].astype(BF16)
        u = _dot(lr, wg) + b_ref[...]
        du = dgk_ref[...] * (jax.nn.sigmoid(-u) / GLA_GATE_NORM)
        dub = du.astype(BF16)
        dp_ref[:, C_QG:C_KG] = (dqg_ref[...] * 0.125).astype(BF16)
        dp_ref[:, C_KG:C_VG] = dkg_ref[...].astype(BF16)
        dp_ref[:, C_VG:C_GG] = dvg_ref[...].astype(BF16)
        dp_ref[:, C_GG:C_QS] = dgg_ref[...].astype(BF16)
        dp_ref[:, C_QS:C_KS] = (dqs_ref[...] * 0.125).astype(BF16)
        dp_ref[:, C_KS:C_VS] = dks_ref[...].astype(BF16)
        dp_ref[:, C_VS:C_LR] = dvs_ref[...].astype(BF16)
        dp_ref[:, C_LR:DIN_P] = _dot_nt(dub, wg).astype(BF16)
        dwgk_ref[...] += _dot_tn(lr, dub)
        dbgk_ref[...] += _colsum8(du)

    return pl.pallas_call(
        body, name="dproj_assemble", grid=(T // TB,),
        in_specs=[_row(128, C_LR // 128), _row(256), _row(256), _row(512), _row(512), _row(512), _row(512),
                  _row(512), _row(256), _const((128, 256)), _const((1, 256))],
        out_specs=[_row(DIN_P), _const((128, 256)), _const((8, 256))],
        out_shape=[SDS((T, DIN_P), BF16), SDS((128, 256), F32), SDS((8, 256), F32)],
        compiler_params=_params())(proj, dq_g, dk_g, dv_g, dg_g, dq_s, dk_s, dv_s, dgk, wgk, bgk)


def _group_ones(n, g):
    idx = np.arange(n) // g
    return jnp.asarray((idx[:, None] == idx[None, :]).astype(np.float32), dtype=BF16)


def _mix_cat(o_g, proj, o_s, wg512, ws512, bd64):
    T = o_g.shape[0]

    def body(og_ref, gg_ref, os_ref, wg_ref, ws_ref, bd_ref, cat_ref):
        og = og_ref[...]
        gg = gg_ref[...]
        s = gg * jax.nn.sigmoid(gg)
        for h in range(4):
            sl = slice(128 * h, 128 * (h + 1))
            x = og[:, sl]
            cat_ref[:, sl] = (x * _rsqrt_ms(x) * wg_ref[:, sl] * s[:, sl]).astype(BF16)
        osv = os_ref[...]
        ms = _dot_exact(osv * osv, bd_ref[...]) * (1.0 / 64.0)
        cat_ref[:, 512:1024] = (osv * lax.rsqrt(ms + EPS) * ws_ref[...]).astype(BF16)

    return pl.pallas_call(
        body, name="mix_cat", grid=(T // TB,),
        in_specs=[_row(512), _row(512, C_GG // 512), _row(512), _const((1, 512)), _const((1, 512)),
                  _const((512, 512))],
        out_specs=_row(1024), out_shape=SDS((T, 1024), BF16), compiler_params=_params())(
            o_g, proj, o_s, wg512, ws512, bd64)


def _mix_cat_bwd(dcat, o_g, proj, o_s, wg512, ws512, bd64):
    T = o_g.shape[0]

    def body(dc_ref, og_ref, gg_ref, os_ref, wg_ref, ws_ref, bd_ref, dog_ref, dgg_ref, dos_ref, dwg_ref, dws_ref):
        @pl.when(pl.program_id(0) == 0)
        def _():
            dwg_ref[...] = jnp.zeros_like(dwg_ref)
            dws_ref[...] = jnp.zeros_like(dws_ref)

        og = og_ref[...]
        gg = gg_ref[...]
        sg = jax.nn.sigmoid(gg)
        s = gg * sg
        ds = sg * (1.0 + gg * (1.0 - sg))
        for h in range(4):
            sl = slice(128 * h, 128 * (h + 1))
            x = og[:, sl]
            r = _rsqrt_ms(x)
            n = x * r
            w = wg_ref[:, sl]
            dc = dc_ref[:, sl]
            dy = dc * s[:, sl]
            dgg_ref[:, sl] = dc * (n * w) * ds[:, sl]
            dn = dy * w
            dog_ref[:, sl] = r * (dn - n * jnp.mean(dn * n, axis=-1, keepdims=True))
            dwg_ref[:, sl] += _colsum8(dy * n)
        osv = os_ref[...]
        bd = bd_ref[...]
        r = lax.rsqrt(_dot_exact(osv * osv, bd) * (1.0 / 64.0) + EPS)
        n = osv * r
        dc = dc_ref[:, 512:1024]
        dn = dc * ws_ref[...]
        dos_ref[...] = r * (dn - n * (_dot_exact(dn * n, bd) * (1.0 / 64.0)))
        dws_ref[...] += _colsum8(dc * n)

    return pl.pallas_call(
        body, name="mix_cat_bwd", grid=(T // TB,),
        in_specs=[_row(1024), _row(512), _row(512, C_GG // 512), _row(512), _const((1, 512)), _const((1, 512)),
                  _const((512, 512))],
        out_specs=[_row(512), _row(512), _row(512), _const((8, 512)), _const((8, 512))],
        out_shape=[SDS((T, 512), F32), SDS((T, 512), F32), SDS((T, 512), F32), SDS((8, 512), F32),
                   SDS((8, 512), F32)],
        compiler_params=_params())(dcat, o_g, proj, o_s, wg512, ws512, bd64)


FF_TN = 1408
DW_TK = 2048


def _gate_up_act(h, w):
    T = h.shape[0]
    nj = D_FF // FF_TN

    def body(h_ref, wg_ref, wu_ref, gu_ref, a_ref):
        hv = h_ref[...]
        g = _dot(hv, wg_ref[...])
        u = _dot(hv, wu_ref[...])
        gu_ref[0] = g.astype(BF16)
        gu_ref[1] = u.astype(BF16)
        a_ref[...] = (g * jax.nn.sigmoid(g) * u).astype(BF16)

    return pl.pallas_call(
        body, name="mm_gate_up_act", grid=(nj, T // TB),
        in_specs=[pl.BlockSpec((TB, D), lambda j, i: (i, 0)),
                  pl.BlockSpec((None, D, FF_TN), lambda j, i: (j, 0, 0)),
                  pl.BlockSpec((None, D, FF_TN), lambda j, i: (nj + j, 0, 0))],
        out_specs=[pl.BlockSpec((2, TB, FF_TN), lambda j, i: (0, i, j)),
                   pl.BlockSpec((TB, FF_TN), lambda j, i: (i, j))],
        out_shape=[SDS((2, T, D_FF), BF16), SDS((T, D_FF), BF16)],
        compiler_params=_params(dimension_semantics=("parallel", "parallel")))(h, w, w)


def _down_bwd(dy, w_down, gu):
    T = dy.shape[0]
    nj = D_FF // FF_TN

    def body(dy_ref, w_ref, gu_ref, dgu_ref):
        da = _dot_nt(dy_ref[...].astype(BF16), w_ref[...])
        g = gu_ref[0].astype(F32)
        sg = jax.nn.sigmoid(g)
        dgu_ref[0] = (da * gu_ref[1].astype(F32) * (sg * (1.0 + g * (1.0 - sg)))).astype(BF16)
        dgu_ref[1] = (da * (g * sg)).astype(BF16)

    return pl.pallas_call(
        body, name="mm_down_bwd", grid=(nj, T // TB),
        in_specs=[pl.BlockSpec((TB, D), lambda j, i: (i, 0)),
                  pl.BlockSpec((FF_TN, D), lambda j, i: (j, 0)),
                  pl.BlockSpec((2, TB, FF_TN), lambda j, i: (0, i, j))],
        out_specs=pl.BlockSpec((2, TB, FF_TN), lambda j, i: (0, i, j)),
        out_shape=SDS((2, T, D_FF), BF16),
        compiler_params=_params(dimension_semantics=("parallel", "parallel")))(dy, w_down, gu)


def _gla_consts():
    c = GLA_C
    L = np.tril(np.ones((c, c), np.float32))
    blocks = [L, L[(np.arange(c) // 16) * 16]]
    blocks += [np.repeat(L[16 * i:16 * i + 1], c, axis=0) for i in range(4)]
    blocks.append(np.repeat(L[c - 1:c], c, axis=0))
    return jnp.asarray(np.concatenate(blocks, axis=0))


@jax.custom_vjp
def _gla_lin(cm, g):
    cb = cm.astype(BF16)
    h, m, l = _split3(g)
    y = _dot(cb, h) + _dot(cb, m) + _dot(cb, l)
    return tuple(y[GLA_C * n:GLA_C * (n + 1)] for n in range(7))


def _gla_lin_fwd(cm, g):
    return _gla_lin(cm, g), cm


def _gla_lin_bwd(cm, cts):
    cb = cm.astype(BF16)
    h, m, l = _split3(jnp.concatenate(cts, axis=0))
    return None, _dot_tn(cb, h) + _dot_tn(cb, m) + _dot_tn(cb, l)


_gla_lin.defvjp(_gla_lin_fwd, _gla_lin_bwd)


GLA_SUB = 16


def _gla_scores_terms(qs, k, b, rs, blk):
    row = lax.broadcasted_iota(jnp.int32, (GLA_C, 128), 0)
    first_head = lax.broadcasted_iota(jnp.int32, (GLA_SUB, 128), 1) < 64
    keep = row < GLA_SUB * (blk + 1)
    e = jnp.where(keep, jnp.exp(jnp.where(keep, rs[blk] - b, 0.0)), 0.0)
    qb = qs[GLA_SUB * blk:GLA_SUB * (blk + 1)]
    lhs = jnp.concatenate([jnp.where(first_head, qb, 0.0), jnp.where(first_head, 0.0, qb)], axis=0)
    return lhs, e, first_head


@jax.custom_vjp
def _gla_scores(qs, k, b, r0, r1, r2, r3):
    rs = (r0, r1, r2, r3)
    tops, bottoms = [], []
    for blk in range(GLA_C // GLA_SUB):
        lhs, e, _ = _gla_scores_terms(qs, k, b, rs, blk)
        a = _dot_nt_f32(lhs, k * e)
        tops.append(a[:GLA_SUB])
        bottoms.append(a[GLA_SUB:])
    return jnp.concatenate(tops, axis=0), jnp.concatenate(bottoms, axis=0)


def _gla_scores_fwd(qs, k, b, r0, r1, r2, r3):
    return _gla_scores(qs, k, b, r0, r1, r2, r3), (qs, k, b, r0, r1, r2, r3)


def _gla_scores_bwd(saved, cts):
    qs, k, b = saved[:3]
    rs = saved[3:]
    da0, da1 = cts
    dqs, drs = [], []
    dk = jnp.zeros_like(k)
    db = jnp.zeros_like(b)
    for blk in range(GLA_C // GLA_SUB):
        lhs, e, first_head = _gla_scores_terms(qs, k, b, rs, blk)
        rows = slice(GLA_SUB * blk, GLA_SUB * (blk + 1))
        da = jnp.concatenate([da0[rows], da1[rows]], axis=0)
        dlhs = lax.dot_general(da, k * e, (((1,), (0,)), ((), ())), precision=HIGHEST, preferred_element_type=F32)
        dqs.append(jnp.where(first_head, dlhs[:GLA_SUB], dlhs[GLA_SUB:]))
        dks = lax.dot_general(da, lhs, (((0,), (0,)), ((), ())), precision=HIGHEST, preferred_element_type=F32)
        dk = dk + dks * e
        darg = dks * (k * e)
        db = db - darg
        drs.append(darg)
    return (jnp.concatenate(dqs, axis=0), dk, db, *drs)


_gla_scores.defvjp(_gla_scores_fwd, _gla_scores_bwd)


def _gla_chunk_pair(cm, q, k, g, v0, v1, st):
    c = GLA_C
    lane = lax.broadcasted_iota(jnp.int32, (c, 128), 1)
    m0 = (lane < 64).astype(F32)
    m1 = 1.0 - m0
    ri = lax.broadcasted_iota(jnp.int32, (c, c), 0)
    ci = lax.broadcasted_iota(jnp.int32, (c, c), 1)
    b, r, r0, r1, r2, r3, bl = _gla_lin(cm, g)
    a0, a1 = _gla_scores(q * jnp.exp(b - r), k, b, r0, r1, r2, r3)
    causal = ci <= ri
    a0 = jnp.where(causal, a0, 0.0)
    a1 = jnp.where(causal, a1, 0.0)
    qe = q * jnp.exp(b)
    o0 = _dot_nt(qe * m0, st) + _dot(a0, v0)
    o1 = _dot_nt(qe * m1, st) + _dot(a1, v1)
    kd = k * jnp.exp(bl - b)
    m0s = jnp.concatenate([m0, m0], axis=0)
    decay = jnp.exp(jnp.concatenate([bl, bl], axis=0))
    st_new = st * decay + m0s * _dot_tn(v0, kd) + (1.0 - m0s) * _dot_tn(v1, kd)
    return o0, o1, st_new


GLA_TB = 512


def _gla_fwd(proj, gk, cm, wps=()):
    T = proj.shape[0]
    nb = T // GLA_TB
    nc = GLA_TB // GLA_C
    n = len(wps)

    def body(q_ref, k_ref, v_ref, g_ref, cm_ref, *rest):
        w_refs, (o_ref, st_ref), out_refs = rest[:n], rest[n:n + 2], rest[n + 2:2 * n + 2]
        st_scr = rest[2 * n + 2]
        step = pl.program_id(0)
        if n:
            ag_start, ag_pass_on, ag_finish = _ag_phases(w_refs, out_refs, *rest[2 * n + 3:])

        @pl.when(step == 0)
        def _():
            st_scr[...] = jnp.zeros_like(st_scr)
            if n:
                ag_start()

        cmv = cm_ref[...]

        def chunk(ci, carry):
            rs = pl.ds(pl.multiple_of(ci * GLA_C, GLA_C), GLA_C)
            for p in range(2):
                ls = slice(128 * p, 128 * (p + 1))
                st = st_scr[p]
                st_ref[ci, p] = st
                o0, o1, st_new = _gla_chunk_pair(
                    cmv, q_ref[rs, ls] * 0.125, k_ref[rs, ls], g_ref[rs, ls],
                    v_ref[rs, 256 * p:256 * p + 128], v_ref[rs, 256 * p + 128:256 * p + 256], st)
                o_ref[rs, 256 * p:256 * p + 128] = o0
                o_ref[rs, 256 * p + 128:256 * p + 256] = o1
                st_scr[p] = st_new
            return carry

        lax.fori_loop(0, nc, chunk, 0)

        if n:
            pl.when(step == max(nb - 3, 0))(ag_pass_on)
            pl.when(step == nb - 1)(ag_finish)

    return pl.pallas_call(
        body, name="gla_fwd", grid=(nb,),
        in_specs=[pl.BlockSpec((GLA_TB, 256), lambda i: (i, C_QG // 256)),
                  pl.BlockSpec((GLA_TB, 256), lambda i: (i, C_KG // 256)),
                  pl.BlockSpec((GLA_TB, 512), lambda i: (i, C_VG // 512)),
                  pl.BlockSpec((GLA_TB, 256), lambda i: (i, 0)),
                  pl.BlockSpec((7 * GLA_C, GLA_C), lambda i: (0, 0))] + [HBM] * n,
        out_specs=[pl.BlockSpec((GLA_TB, 512), lambda i: (i, 0)),
                   pl.BlockSpec((nc, 2, 128, 128), lambda i: (i, 0, 0, 0))] + [HBM] * n,
        out_shape=[SDS((T, 512), F32), SDS((T // GLA_C, 2, 128, 128), F32)]
        + [SDS((4,) + wp.shape, wp.dtype) for wp in wps],
        scratch_shapes=[pltpu.VMEM((2, 128, 128), F32)] + (_ag_sems(n) if n else []),
        compiler_params=_params(dimension_semantics=("arbitrary",)))(proj, proj, proj, gk, cm, *wps)


def _gla_bwd(proj, gk, cm, states, do, parts=()):
    T = proj.shape[0]
    nb = T // GLA_TB
    nc = GLA_TB // GLA_C
    n = len(parts)

    def body(q_ref, k_ref, v_ref, g_ref, cm_ref, st_ref, do_ref, *rest):
        p_refs, (dq_ref, dk_ref, dv_ref, dg_ref), out_refs = rest[:n], rest[n:n + 4], rest[n + 4:2 * n + 4]
        dst_scr = rest[2 * n + 4]
        step = pl.program_id(0)
        if n:
            rs_start, rs_finish = _rs_exchange_phases(p_refs, out_refs, *rest[2 * n + 5:])

        @pl.when(step == 0)
        def _():
            dst_scr[...] = jnp.zeros_like(dst_scr)
            if n:
                rs_start()

        cmv = cm_ref[...]

        def chunk(t, carry):
            ci = nc - 1 - t
            rs = pl.ds(pl.multiple_of(ci * GLA_C, GLA_C), GLA_C)
            for p in range(2):
                ls = slice(128 * p, 128 * (p + 1))
                _, vjp = jax.vjp(
                    functools.partial(_gla_chunk_pair, cmv),
                    q_ref[rs, ls] * 0.125, k_ref[rs, ls], g_ref[rs, ls],
                    v_ref[rs, 256 * p:256 * p + 128], v_ref[rs, 256 * p + 128:256 * p + 256], st_ref[ci, p])
                dq, dk, dg, dv0, dv1, dst = vjp((do_ref[rs, 256 * p:256 * p + 128],
                                                 do_ref[rs, 256 * p + 128:256 * p + 256], dst_scr[p]))
                dq_ref[rs, ls] = dq
                dk_ref[rs, ls] = dk
                dg_ref[rs, ls] = dg
                dv_ref[rs, 256 * p:256 * p + 128] = dv0
                dv_ref[rs, 256 * p + 128:256 * p + 256] = dv1
                dst_scr[p] = dst
            return carry

        lax.fori_loop(0, nc, chunk, 0)

        if n:
            pl.when(step == nb - 1)(rs_finish)

    rev = lambda i: nb - 1 - i
    return pl.pallas_call(
        body, name="gla_bwd", grid=(nb,),
        in_specs=[pl.BlockSpec((GLA_TB, 256), lambda i: (rev(i), C_QG // 256)),
                  pl.BlockSpec((GLA_TB, 256), lambda i: (rev(i), C_KG // 256)),
                  pl.BlockSpec((GLA_TB, 512), lambda i: (rev(i), C_VG // 512)),
                  pl.BlockSpec((GLA_TB, 256), lambda i: (rev(i), 0)),
                  pl.BlockSpec((7 * GLA_C, GLA_C), lambda i: (0, 0)),
                  pl.BlockSpec((nc, 2, 128, 128), lambda i: (rev(i), 0, 0, 0)),
                  pl.BlockSpec((GLA_TB, 512), lambda i: (rev(i), 0))] + [HBM] * n,
        out_specs=[pl.BlockSpec((GLA_TB, 256), lambda i: (rev(i), 0)),
                   pl.BlockSpec((GLA_TB, 256), lambda i: (rev(i), 0)),
                   pl.BlockSpec((GLA_TB, 512), lambda i: (rev(i), 0)),
                   pl.BlockSpec((GLA_TB, 256), lambda i: (rev(i), 0))] + [HBM] * n,
        out_shape=[SDS((T, 256), F32), SDS((T, 256), F32), SDS((T, 512), F32), SDS((T, 256), F32)]
        + [SDS(p.shape, p.dtype) for p in parts],
        scratch_shapes=[pltpu.VMEM((2, 128, 128), F32)] + (_rs_exchange_sems(n) if n else []),
        compiler_params=_params(dimension_semantics=("arbitrary",)))(proj, proj, proj, gk, cm, states, do, *parts)


SB_DEAD = 105.0
SB_COUNT_LANE = 127


def _sb_tri():
    i = np.arange(SBQ)
    return jnp.asarray((i[:, None] > i[None, :]).astype(np.float32), dtype=BF16)


def _sb_block_fwd(qh, kb, tri, carry, strict):
    z = _dot_nt(qh, kb)
    sp = _softplus(z)
    l1 = -sp
    if strict is not None:
        l1 = jnp.where(strict, l1, 0.0)
    log_a = (z - sp) + _dot_hilo(l1, tri) + carry
    a = jnp.exp(log_a)
    if strict is not None:
        a = jnp.where(strict, a, 0.0)
    return z - sp, l1, a


def _sb_fwd(qs, ks, vs, tri):
    T = qs.shape[0]
    nq = T // SBQ

    def body(q_ref, k_ref, v_ref, tri_ref, o_ref, c_ref):
        i = pl.program_id(1)
        lane = lax.broadcasted_iota(jnp.int32, (1, 128), 1)
        clane = lax.broadcasted_iota(jnp.int32, (SBQ, 128), 1)
        strict = (lax.broadcasted_iota(jnp.int32, (SBQ, SBQ), 1) < lax.broadcasted_iota(jnp.int32, (SBQ, SBQ), 0))
        tri_v = tri_ref[...]
        qv = q_ref[...]
        first_head = lane < 64
        qhs = (jnp.where(first_head, qv, jnp.zeros_like(qv)), jnp.where(first_head, jnp.zeros_like(qv), qv))

        def block(j, carries, accs, masked):
            rs = pl.ds(pl.multiple_of(j * SBQ, SBQ), SBQ)
            kb = k_ref[rs, :]
            vb = v_ref[rs, :]
            out_c, out_a = [], []
            for hh in range(2):
                _, l1, a = _sb_block_fwd(qhs[hh], kb, tri_v, carries[hh], strict if masked else None)
                out_a.append(accs[hh] + _dot(a.astype(BF16), vb))
                out_c.append(carries[hh] + jnp.sum(l1, axis=1, keepdims=True))
            return out_c, out_a

        zero1 = jnp.zeros((SBQ, 1), F32)
        zero128 = jnp.zeros((SBQ, 128), F32)
        (c0, c1), (a0, a1) = block(i, (zero1, zero1), (zero128, zero128), True)

        def more(state):
            return (state[0] <= i) & (jnp.maximum(jnp.max(state[1]), jnp.max(state[2])) > -SB_DEAD)

        def step(state):
            jj, c0, c1, a0, a1, t0, t1 = state
            j = i - jj
            t0 = jnp.where(clane == j, c0, t0)
            t1 = jnp.where(clane == j, c1, t1)
            (c0, c1), (a0, a1) = block(j, (c0, c1), (a0, a1), False)
            return jj + 1, c0, c1, a0, a1, t0, t1

        jj, c0, c1, a0, a1, t0, t1 = lax.while_loop(
            more, step, (jnp.int32(1), c0, c1, a0, a1, zero128, zero128))
        o_ref[...] = jnp.where(first_head, a0, a1)
        swept = (jj - 1).astype(F32)
        c_ref[0, :, 0:128] = jnp.where(clane == SB_COUNT_LANE, swept, t0)
        c_ref[0, :, 128:256] = jnp.where(clane == SB_COUNT_LANE, swept, t1)

    return pl.pallas_call(
        body, name="sb_fwd", grid=(4, nq),
        in_specs=[pl.BlockSpec((SBQ, 128), lambda h, i: (i, h)),
                  pl.BlockSpec((T, 128), lambda h, i: (0, h)),
                  pl.BlockSpec((T, 128), lambda h, i: (0, h)),
                  pl.BlockSpec((SBQ, SBQ), lambda h, i: (0, 0))],
        out_specs=[pl.BlockSpec((SBQ, 128), lambda h, i: (i, h)),
                   pl.BlockSpec((1, SBQ, 256), lambda h, i: (h, i, 0))],
        out_shape=[SDS((T, 512), F32), SDS((4, T, 256), F32)],
        compiler_params=_params(dimension_semantics=("parallel", "arbitrary")))(qs, ks, vs, tri)


def _sb_bwd(qs, ks, vs, do, carries, tri, tri_t):
    T = qs.shape[0]
    nq = T // SBQ

    def body(q_ref, k_ref, v_ref, do_ref, c_ref, tri_ref, trit_ref, dq_ref, dk_ref, dv_ref):
        i = pl.program_id(1)

        @pl.when(i == 0)
        def _():
            dk_ref[...] = jnp.zeros_like(dk_ref)
            dv_ref[...] = jnp.zeros_like(dv_ref)

        lane = lax.broadcasted_iota(jnp.int32, (1, 128), 1)
        clane = lax.broadcasted_iota(jnp.int32, (SBQ, 128), 1)
        strict = (lax.broadcasted_iota(jnp.int32, (SBQ, SBQ), 1) < lax.broadcasted_iota(jnp.int32, (SBQ, SBQ), 0))
        tri_v = tri_ref[...]
        trit_v = trit_ref[...]
        qv = q_ref[...]
        dov = do_ref[...].astype(BF16)
        first_head = lane < 64
        qhs = (jnp.where(first_head, qv, jnp.zeros_like(qv)), jnp.where(first_head, jnp.zeros_like(qv), qv))
        dohs = (jnp.where(first_head, dov, jnp.zeros_like(dov)), jnp.where(first_head, jnp.zeros_like(dov), dov))
        cts = (c_ref[0, :, 0:128], c_ref[0, :, 128:256])

        def block(j, pcarries, dqs, masked):
            rs = pl.ds(pl.multiple_of(j * SBQ, SBQ), SBQ)
            kb = k_ref[rs, :]
            vb = v_ref[rs, :]
            out_p, out_q = [], []
            dk = jnp.zeros((SBQ, 128), F32)
            dv = jnp.zeros((SBQ, 128), F32)
            for hh in range(2):
                carry = jnp.sum(jnp.where(clane == j, cts[hh], 0.0), axis=1, keepdims=True)
                lb, _, a = _sb_block_fwd(qhs[hh], kb, tri_v, carry, strict if masked else None)
                g = a * _dot_nt(dohs[hh], vb)
                p = _dot_hilo(g, trit_v) + pcarries[hh]
                dz = g - (g + p) * jnp.exp(lb)
                if masked:
                    dz = jnp.where(strict, dz, 0.0)
                dzb = dz.astype(BF16)
                dk = dk + _dot_tn(dzb, qhs[hh])
                dv = dv + _dot_tn(a.astype(BF16), dohs[hh])
                out_p.append(pcarries[hh] + jnp.sum(g, axis=1, keepdims=True))
                out_q.append(dqs[hh] + _dot(dzb, kb))
            dk_ref[rs, :] += dk
            dv_ref[rs, :] += dv
            return out_p, out_q

        def step(j, state):
            (p0, p1), (q0, q1) = block(j, (state[0], state[1]), (state[2], state[3]), False)
            return p0, p1, q0, q1

        swept = jnp.max(jnp.where(clane == SB_COUNT_LANE, cts[0], 0.0)).astype(jnp.int32)
        first = i - jnp.clip(swept, 0, i)
        zero1 = jnp.zeros((SBQ, 1), F32)
        zero128 = jnp.zeros((SBQ, 128), F32)
        p0, p1, q0, q1 = lax.fori_loop(first, i, step, (zero1, zero1, zero128, zero128))
        _, (q0, q1) = block(i, (p0, p1), (q0, q1), True)
        dq_ref[...] = jnp.where(first_head, q0, q1)

    return pl.pallas_call(
        body, name="sb_bwd", grid=(4, nq),
        in_specs=[pl.BlockSpec((SBQ, 128), lambda h, i: (i, h)),
                  pl.BlockSpec((T, 128), lambda h, i: (0, h)),
                  pl.BlockSpec((T, 128), lambda h, i: (0, h)),
                  pl.BlockSpec((SBQ, 128), lambda h, i: (i, h)),
                  pl.BlockSpec((1, SBQ, 256), lambda h, i: (h, i, 0)),
                  pl.BlockSpec((SBQ, SBQ), lambda h, i: (0, 0)),
                  pl.BlockSpec((SBQ, SBQ), lambda h, i: (0, 0))],
        out_specs=[pl.BlockSpec((SBQ, 128), lambda h, i: (i, h)),
                   pl.BlockSpec((T, 128), lambda h, i: (0, h)),
                   pl.BlockSpec((T, 128), lambda h, i: (0, h))],
        out_shape=[SDS((T, 512), F32), SDS((T, 512), F32), SDS((T, 512), F32)],
        compiler_params=_params(dimension_semantics=("parallel", "arbitrary")))(qs, ks, vs, do, carries, tri, tri_t)


def _mem_fwd(mem, mem_norm_w, w_mkv, mk_norm_w):
    M = mem.shape[0]

    def body(mem_ref, wn_ref, w_ref, wk_ref, mn_ref, kraw_ref, k_ref, v_ref):
        mv = mem_ref[...]
        mn = (mv * _rsqrt_ms(mv) * wn_ref[...]).astype(BF16)
        mn_ref[...] = mn
        for s in range(2):
            cols = slice(512 * s, 512 * (s + 1))
            ks = _dot(mn, w_ref[s, :D, :])
            kraw_ref[:, cols] = ks
            v_ref[:, cols] = _dot(mn, w_ref[2 + s, :D, :]).astype(BF16)
            for h in range(2):
                x = ks[:, MEM_HD * h:MEM_HD * (h + 1)]
                k_ref[:, 512 * s + MEM_HD * h:512 * s + MEM_HD * (h + 1)] = (
                    x * _rsqrt_ms(x) * wk_ref[...]).astype(BF16)

    vm = pl.BlockSpec(memory_space=pltpu.VMEM)
    return pl.pallas_call(
        body, name="mem_fwd", in_specs=[vm] * 4, out_specs=[vm] * 4,
        out_shape=[SDS((M, D), BF16), SDS((M, D), F32), SDS((M, D), BF16), SDS((M, D), BF16)],
        compiler_params=_params())(mem, mem_norm_w, w_mkv, mk_norm_w)


def _xattn_fwd(qraw, k, v, wq):
    T = qraw.shape[0]
    M = k.shape[0]

    def body(q_ref, k_ref, v_ref, wq_ref, o_ref):
        for h in range(MEM_HEADS):
            sl = slice(MEM_HD * h, MEM_HD * (h + 1))
            x = q_ref[:, sl]
            q = (x * _rsqrt_ms(x) * wq_ref[...]).astype(BF16)
            s = _dot_nt(q, k_ref[:, sl]) * (1.0 / math.sqrt(MEM_HD))
            s = s - jnp.max(s, axis=-1, keepdims=True)
            e = jnp.exp(s)
            p = e / jnp.sum(e, axis=-1, keepdims=True)
            o_ref[:, sl] = _dot(p.astype(BF16), v_ref[:, sl]).astype(BF16)

    return pl.pallas_call(
        body, name="xattn_fwd", grid=(T // TB,),
        in_specs=[_row(D), _const((M, D)), _const((M, D)), _const((1, MEM_HD))],
        out_specs=_row(D), out_shape=SDS((T, D), BF16), compiler_params=_params())(qraw, k, v, wq)


def _xattn_bwd(qraw, k, v, wq, do):
    T = qraw.shape[0]
    M = k.shape[0]

    def body(q_ref, k_ref, v_ref, wq_ref, do_ref, dq_ref, dk_ref, dv_ref, dw_ref):
        @pl.when(pl.program_id(0) == 0)
        def _():
            dk_ref[...] = jnp.zeros_like(dk_ref)
            dv_ref[...] = jnp.zeros_like(dv_ref)
            dw_ref[...] = jnp.zeros_like(dw_ref)

        w = wq_ref[...]
        for h in range(MEM_HEADS):
            sl = slice(MEM_HD * h, MEM_HD * (h + 1))
            x = q_ref[:, sl]
            r = _rsqrt_ms(x)
            n = x * r
            q = (n * w).astype(BF16)
            kb = k_ref[:, sl]
            s = _dot_nt(q, kb) * (1.0 / math.sqrt(MEM_HD))
            s = s - jnp.max(s, axis=-1, keepdims=True)
            e = jnp.exp(s)
            p = e / jnp.sum(e, axis=-1, keepdims=True)
            dob = do_ref[:, sl].astype(BF16)
            dp = _dot_nt(dob, v_ref[:, sl])
            ds = (p * (dp - jnp.sum(dp * p, axis=-1, keepdims=True)) * (1.0 / math.sqrt(MEM_HD))).astype(BF16)
            dv_ref[:, sl] += _dot_tn(p.astype(BF16), dob)
            dk_ref[:, sl] += _dot_tn(ds, q)
            dqn = _dot(ds, kb)
            dn = dqn * w
            dq_ref[:, sl] = r * (dn - n * jnp.mean(dn * n, axis=-1, keepdims=True))
            dw_ref[...] += _colsum8(dqn * n)

    return pl.pallas_call(
        body, name="xattn_bwd", grid=(T // TB,),
        in_specs=[_row(D), _const((M, D)), _const((M, D)), _const((1, MEM_HD)), _row(D)],
        out_specs=[_row(D), _const((M, D)), _const((M, D)), _const((8, MEM_HD))],
        out_shape=[SDS((T, D), F32), SDS((M, D), F32), SDS((M, D), F32), SDS((8, MEM_HD), F32)],
        compiler_params=_params())(qraw, k, v, wq, do)


def _mem_bwd(mem, mem_norm_w, w_mkv, mk_norm_w, mem_n, k_raw, dk, dv):
    M = mem.shape[0]

    def body(mem_ref, wn_ref, w_ref, wk_ref, mn_ref, kraw_ref, dk_ref, dv_ref, dw_ref, dwn_ref, dwk_ref, dkv_scr):
        wk = wk_ref[...]
        dwk = jnp.zeros((8, MEM_HD), F32)
        for h in range(MEM_HEADS):
            sl = slice(MEM_HD * h, MEM_HD * (h + 1))
            x = kraw_ref[:, sl]
            r = _rsqrt_ms(x)
            n = x * r
            dkh = dk_ref[:, sl]
            dn = dkh * wk
            dkv_scr[:, sl] = (r * (dn - n * jnp.mean(dn * n, axis=-1, keepdims=True))).astype(BF16)
            dwk = dwk + _colsum8(dkh * n)
        dwk_ref[...] = dwk
        dkv_scr[:, D:] = dv_ref[...].astype(BF16)
        dkv = dkv_scr[...]
        mn = mn_ref[...]
        dmn = jnp.zeros((M, D), F32)
        for s in range(4):
            part = dkv[:, 512 * s:512 * (s + 1)]
            dw_ref[s] = _dot_tn(mn, part).astype(BF16)
            dmn = dmn + _dot_nt(part, w_ref[s, :D, :])
        mv = mem_ref[...]
        dwn_ref[...] = _colsum8(dmn * (mv * _rsqrt_ms(mv)))

    vm = pl.BlockSpec(memory_space=pltpu.VMEM)
    return pl.pallas_call(
        body, name="mem_bwd", in_specs=[vm] * 8, out_specs=[vm] * 3,
        out_shape=[SDS((4, D, 512), BF16), SDS((8, D), F32), SDS((8, MEM_HD), F32)],
        scratch_shapes=[pltpu.VMEM((M, 2 * D), BF16)],
        compiler_params=_params())(mem, mem_norm_w, w_mkv, mk_norm_w, mem_n, k_raw, dk, dv)


def _local_step(x, mem, tgt, w, later_weights=None, later_partials=None):
    T = x.shape[0]
    wgk = jnp.zeros((128, 256), F32).at[:16].set(w["w_gk_up"].astype(F32))
    wg512 = jnp.tile(w["gla_norm_w"], (1, 4))
    ws512 = jnp.tile(w["sb_norm_w"], (1, 8))
    bd64 = _group_ones(512, 64)
    cm = _gla_consts()
    tri = _sb_tri()
    tri_t = tri.T

    h1 = _norm_fwd(x, w["mix_norm_w"], "norm1_fwd")
    proj, gk, qs, ks, vs = _matmul(h1, w["w_in"], mode="nn", tm=TB, tn=DIN_P, name="mm_proj",
                                   epilogue=_then_proj_split(wgk, w["b_gk"], T, TB))
    if later_weights is None:
        o_g, states = _gla_fwd(proj, gk, cm)
    else:
        o_g, states, *gathered = _gla_fwd(proj, gk, cm, later_weights[0])
        w = {**w, **later_weights[1](gathered)}
    o_s, carries = _sb_fwd(qs, ks, vs, tri)
    cat = _mix_cat(o_g, proj, o_s, wg512, ws512, bd64)
    x1, h2 = _matmul(cat, w["w_out"], mode="nn", tm=TB, tn=D, res=x, name="mm_out",
                     epilogue=_then_norm_fwd(w["xattn_norm_w"], T, TB))
    qraw = _matmul(h2, w["w_mq"], mode="nn", tm=TB, tn=D, name="mm_mq")
    mem_n, k_raw, k_n, v_m = _mem_fwd(mem, w["mem_norm_w"], w["w_mkv"], w["mk_norm_w"])
    om = _xattn_fwd(qraw, k_n, v_m, w["mq_norm_w"])
    x2, h3 = _matmul(om, w["w_mo"], mode="nn", tm=TB, tn=D, res=x1, name="mm_mo",
                     epilogue=_then_norm_fwd(w["ffn_norm_w"], T, TB))
    gu, act = _gate_up_act(h3, w["w_gate_up"])
    dx3, loss_rows = _matmul(act, w["w_down"], mode="nn", tm=TB, tn=D, res=x2, name="mm_down",
                             epilogue=_then_loss(tgt, TB))

    g = {}
    dw_tk = min(DW_TK, T)
    dw = dict(mode="tn", tk=dw_tk, out_dtype=BF16)
    g["w_down"] = _matmul(act, dx3, tm=1408, tn=D, name="mm_dw_down", **dw)
    dgu = _down_bwd(dx3, w["w_down"], gu)
    g["w_gate_up"] = _matmul(
        h3, dgu, tm=D, tn=FF_TN, by_column_tile=True, mnk=(D, 2 * D_FF, T), name="mm_dw_gate_up",
        b_spec=pl.BlockSpec((None, dw_tk, FF_TN), lambda j, i, k: (j // 2, k, j % 2)), **dw)
    tm3 = min(2 * TB, T)
    dx2, g["ffn_norm_w"] = _matmul(
        dgu, w["w_gate_up"], mode="nt", tm=tm3, tn=D, tk=FF_TN, mnk=(T, D, 2 * D_FF), name="mm_dh3",
        a_spec=pl.BlockSpec((None, tm3, FF_TN), lambda j, i, k: (k // 2, i, k % 2)),
        b_spec=pl.BlockSpec((None, D, FF_TN), lambda j, i, k: (k, 0, 0)),
        epilogue=_then_norm_bwd(x2, w["ffn_norm_w"], dx3, tm3))
    g["w_mo"] = _matmul(om, dx2, tm=D, tn=D, name="mm_dw_mo", **dw)
    dom = _matmul(dx2, w["w_mo"], mode="nt", tm=TB, tn=D, name="mm_dom")
    dqraw, dk_n, dv_m, g["mq_norm_w"] = _xattn_bwd(qraw, k_n, v_m, w["mq_norm_w"], dom)
    g["w_mkv"], g["mem_norm_w"], g["mk_norm_w"] = _mem_bwd(
        mem, w["mem_norm_w"], w["w_mkv"], w["mk_norm_w"], mem_n, k_raw, dk_n, dv_m)
    g["w_mq"] = _matmul(h2, dqraw, tm=D, tn=D, name="mm_dw_mq", **dw)
    dx1, g["xattn_norm_w"] = _matmul(dqraw, w["w_mq"], mode="nt", tm=TB, tn=D, name="mm_dh2",
                                     epilogue=_then_norm_bwd(x1, w["xattn_norm_w"], dx2, TB))
    g["w_out"] = _matmul(cat, dx1, tm=D, tn=D, name="mm_dw_out", **dw)
    dcat = _matmul(dx1, w["w_out"], mode="nt", tm=TB, tn=D, name="mm_dcat")
    do_g, dg_g, do_s, dwg, dws = _mix_cat_bwd(dcat, o_g, proj, o_s, wg512, ws512, bd64)
    dq_s, dk_s, dv_s = _sb_bwd(qs, ks, vs, do_s, carries, tri, tri_t)
    parts = () if later_partials is None else later_partials(g)
    dq_g, dk_g, dv_g, dgk, *received = _gla_bwd(proj, gk, cm, states, do_g, parts)
    dproj, dwgk, g["b_gk"] = _dproj_assemble(proj, dq_g, dk_g, dv_g, dg_g, dq_s, dk_s, dv_s, dgk, wgk, w["b_gk"])
    g["w_in"] = _matmul(h1, dproj, tm=D, tn=640, name="mm_dw_in", **dw)
    grad_x, g["mix_norm_w"] = _matmul(dproj, w["w_in"], mode="nt", tm=TB, tn=D, name="mm_dh1",
                                      epilogue=_then_norm_bwd(x, w["mix_norm_w"], dx1, TB))

    g["w_gk_up"] = dwgk[:16]
    g["gla_norm_w"] = dwg.reshape(8, 4, 128).sum(axis=1)
    g["sb_norm_w"] = dws.reshape(8, 8, 64).sum(axis=1)
    for n in SMALL:
        g[n] = jnp.sum(g[n], axis=0, keepdims=True)
    if later_partials is None:
        return jnp.sum(loss_rows), grad_x, g
    return jnp.sum(loss_rows), grad_x, g, (parts, received)


def _mesh_pos():
    return lax.axis_index("x"), lax.axis_index("y"), lax.axis_index("c")


def _other_chips(x, y):
    return [(1 - x, y), (x, 1 - y), (1 - x, 1 - y)]


HBM = pl.BlockSpec(memory_space=pl.ANY)


def _ag_phases(w_refs, out_refs, send_sems, recv_sems):
    n = len(w_refs)
    x, y, c = _mesh_pos()
    me = 2 * x + y
    sibling = (x, y, 1 - c)
    chips = _other_chips(x, y)
    mine, theirs = c, 1 - c

    def copy(a, k, src, dst, to):
        return pltpu.make_async_remote_copy(src_ref=src, dst_ref=dst, send_sem=send_sems.at[6 * a + k],
                                            recv_sem=recv_sems.at[6 * a + k], device_id=to, device_id_type=MESH)

    def firsts():
        return [copy(a, k, w_refs[a].at[mine], out_refs[a].at[me, mine], (cx, cy, c))
                for a in range(n) for k, (cx, cy) in enumerate(chips)]

    def landed(a, k, half):
        cx, cy = chips[k]
        return out_refs[a].at[2 * cx + cy, half]

    def passes():
        return [copy(a, 3 + k, landed(a, k, mine), landed(a, k, mine), sibling) for a in range(n) for k in range(3)]

    def start():
        for cp in firsts():
            cp.start()

    def pass_on():
        for a in range(n):
            for k, (cx, cy) in enumerate(chips):
                copy(a, k, landed(a, k, mine), landed(a, k, mine), (cx, cy, c)).wait_recv()
                copy(a, 3 + k, landed(a, k, mine), landed(a, k, mine), sibling).start()

    def finish():
        for a in range(n):
            for k in range(3):
                copy(a, 3 + k, landed(a, k, theirs), landed(a, k, theirs), sibling).wait_recv()
        for cp in firsts() + passes():
            cp.wait_send()

    return start, pass_on, finish


def _ag_sems(n):
    return [pltpu.SemaphoreType.DMA((6 * n,)), pltpu.SemaphoreType.DMA((6 * n,))]


def _ag_weights(wps):
    n = len(wps)

    def body(*refs):
        for phase in _ag_phases(refs[:n], refs[n:2 * n], *refs[2 * n:]):
            phase()

    return pl.pallas_call(
        body, name="ag_weights", in_specs=[HBM] * n, out_specs=[HBM] * n,
        out_shape=[SDS((4,) + wp.shape, wp.dtype) for wp in wps], scratch_shapes=_ag_sems(n),
        compiler_params=pltpu.CompilerParams(has_side_effects=True))(*wps)


def _rs_swap_halves(gps, name):
    n = len(gps)

    def body(*refs):
        g_refs, out_refs = refs[:n], refs[n:2 * n]
        send_sems, recv_sems = refs[2 * n:]
        x, y, c = _mesh_pos()
        copies = [pltpu.make_async_remote_copy(
            src_ref=g_refs[a].at[:, 1 - c], dst_ref=out_refs[a], send_sem=send_sems.at[a], recv_sem=recv_sems.at[a],
            device_id=(x, y, 1 - c), device_id_type=MESH) for a in range(n)]
        for cp in copies:
            cp.start()
        for cp in copies:
            cp.wait()

    return pl.pallas_call(
        body, name=name, in_specs=[HBM] * n, out_specs=[HBM] * n,
        out_shape=[SDS((4,) + gp.shape[2:], gp.dtype) for gp in gps],
        scratch_shapes=[pltpu.SemaphoreType.DMA((n,)), pltpu.SemaphoreType.DMA((n,))],
        compiler_params=pltpu.CompilerParams(has_side_effects=True))(*gps)


def _rs_add_halves(gp, other, c_arr, name):
    h, w = other.shape[1:]

    def body(c_ref, a_ref, b_ref, o_ref):
        o_ref[...] = (a_ref[0].astype(F32) + b_ref[...].astype(F32)).astype(BF16)

    return pl.pallas_call(
        body, name=name,
        grid_spec=pltpu.PrefetchScalarGridSpec(
            num_scalar_prefetch=1, grid=(4,),
            in_specs=[pl.BlockSpec((1, 1, h, w), lambda s, c: (s, c[0], 0, 0)),
                      pl.BlockSpec((1, h, w), lambda s, c: (s, 0, 0))],
            out_specs=pl.BlockSpec((1, h, w), lambda s, c: (s, 0, 0))),
        out_shape=SDS((4, h, w), BF16), compiler_params=_params())(c_arr, gp, other)


def _rs_exchange_phases(p_refs, out_refs, send_sems, recv_sems):
    n = len(p_refs)
    x, y, c = _mesh_pos()
    me = 2 * x + y
    chips = _other_chips(x, y)

    def sends():
        return [pltpu.make_async_remote_copy(
            src_ref=p_refs[a].at[2 * cx + cy], dst_ref=out_refs[a].at[me], send_sem=send_sems.at[3 * a + k],
            recv_sem=recv_sems.at[3 * a + k], device_id=(cx, cy, c), device_id_type=MESH)
            for a in range(n) for k, (cx, cy) in enumerate(chips)]

    def start():
        for cp in sends():
            cp.start()

    def finish():
        for a in range(n):
            for k, (cx, cy) in enumerate(chips):
                slot = out_refs[a].at[2 * cx + cy]
                pltpu.make_async_remote_copy(
                    src_ref=slot, dst_ref=slot, send_sem=send_sems.at[3 * a + k], recv_sem=recv_sems.at[3 * a + k],
                    device_id=(cx, cy, c), device_id_type=MESH).wait_recv()
        for cp in sends():
            cp.wait_send()

    return start, finish


def _rs_exchange_sems(n):
    return [pltpu.SemaphoreType.DMA((3 * n,)), pltpu.SemaphoreType.DMA((3 * n,))]


def _rs_exchange(parts):
    n = len(parts)

    def body(*refs):
        for phase in _rs_exchange_phases(refs[:n], refs[n:2 * n], *refs[2 * n:]):
            phase()

    return pl.pallas_call(
        body, name="rs_exchange", in_specs=[HBM] * n, out_specs=[HBM] * n,
        out_shape=[SDS(p.shape, p.dtype) for p in parts], scratch_shapes=_rs_exchange_sems(n),
        compiler_params=pltpu.CompilerParams(has_side_effects=True))(*parts)


def _rs_add_chips(recv, part, me_arr, name):
    h, w = part.shape[1:]
    th = h // 2 if (h // 2) % 16 == 0 else h

    def body(me_ref, r_ref, p_ref, o_ref):
        me = me_ref[0]
        total = None
        for k in range(4):
            term = jnp.where(me == k, p_ref[k], r_ref[k]).astype(F32)
            total = term if total is None else total + term
        o_ref[...] = total

    spec = pl.BlockSpec((4, th, w), lambda t, me: (0, t, 0))
    return pl.pallas_call(
        body, name=name,
        grid_spec=pltpu.PrefetchScalarGridSpec(
            num_scalar_prefetch=1, grid=(h // th,), in_specs=[spec, spec],
            out_specs=pl.BlockSpec((th, w), lambda t, me: (t, 0))),
        out_shape=SDS((h, w), F32), compiler_params=_params())(me_arr, recv, part)


def _rs_share(halves):
    n = len(halves)

    def body(*refs):
        h_refs, out_refs = refs[:n], refs[n:2 * n]
        send_sems, recv_sems = refs[2 * n:]
        x, y, c = _mesh_pos()
        copies = [pltpu.make_async_remote_copy(
            src_ref=h_refs[a], dst_ref=out_refs[a], send_sem=send_sems.at[a], recv_sem=recv_sems.at[a],
            device_id=(x, y, 1 - c), device_id_type=MESH) for a in range(n)]
        for cp in copies:
            cp.start()
        for cp in copies:
            cp.wait()

    return pl.pallas_call(
        body, name="rs_share", in_specs=[HBM] * n, out_specs=[HBM] * n,
        out_shape=[SDS(hs.shape, hs.dtype) for hs in halves],
        scratch_shapes=[pltpu.SemaphoreType.DMA((n,)), pltpu.SemaphoreType.DMA((n,))],
        compiler_params=pltpu.CompilerParams(has_side_effects=True))(*halves)


def _allreduce_small(s):
    def gather(s_ref, out_ref, send_sems, recv_sems, local_sem):
        x, y, c = _mesh_pos()
        me = 4 * x + 2 * y + c
        local = pltpu.make_async_copy(s_ref, out_ref.at[me], local_sem)
        local.start()
        peers = []
        for r in range(1, 8):
            px = 1 - x if r & 4 else x
            py = 1 - y if r & 2 else y
            pc = 1 - c if r & 1 else c
            peers.append((px, py, pc))
        sends = []
        for k, peer in enumerate(peers):
            cp = pltpu.make_async_remote_copy(
                src_ref=s_ref, dst_ref=out_ref.at[me], send_sem=send_sems.at[k], recv_sem=recv_sems.at[k],
                device_id=peer, device_id_type=MESH)
            cp.start()
            sends.append(cp)
        for k, (px, py, pc) in enumerate(peers):
            slot = out_ref.at[4 * px + 2 * py + pc]
            pltpu.make_async_remote_copy(
                src_ref=slot, dst_ref=slot, send_sem=send_sems.at[k], recv_sem=recv_sems.at[k],
                device_id=(px, py, pc), device_id_type=MESH).wait_recv()
        for cp in sends:
            cp.wait_send()
        local.wait()

    hbm = pl.BlockSpec(memory_space=pl.ANY)
    parts = pl.pallas_call(
        gather, name="small_gather", in_specs=[hbm], out_specs=hbm,
        out_shape=SDS((8, SMALL_ROWS, 1024), F32),
        scratch_shapes=[pltpu.SemaphoreType.DMA((7,)), pltpu.SemaphoreType.DMA((7,)), pltpu.SemaphoreType.DMA],
        compiler_params=pltpu.CompilerParams(has_side_effects=True))(s)

    def add(p_ref, o_ref):
        total = p_ref[0]
        for k in range(1, 8):
            total = total + p_ref[k]
        o_ref[...] = total

    vm = pl.BlockSpec(memory_space=pltpu.VMEM)
    return pl.pallas_call(add, name="small_sum", in_specs=[vm], out_specs=vm,
                          out_shape=SDS((SMALL_ROWS, 1024), F32))(parts)


def _adamw(w, g, m, v, name):
    rows, cols = w.shape
    tr = rows
    for cand in (512, 352, 256):
        if rows > cand and rows % cand == 0:
            tr = cand
            break
    c1 = 1.0 - ADAM_B1 ** ADAM_STEP
    c2 = 1.0 - ADAM_B2 ** ADAM_STEP

    def body(w_ref, g_ref, m_ref, v_ref, d_ref, mo_ref, vo_ref):
        gv = g_ref[...]
        mn = ADAM_B1 * m_ref[...] + (1.0 - ADAM_B1) * gv
        vn = ADAM_B2 * v_ref[...] + (1.0 - ADAM_B2) * (gv * gv)
        mo_ref[...] = mn
        vo_ref[...] = vn
        d_ref[...] = -ADAM_LR * ((mn / c1) / (jnp.sqrt(vn / c2) + ADAM_EPS) + ADAM_WD * w_ref[...])

    spec = pl.BlockSpec((tr, cols), lambda i: (i, 0))
    return pl.pallas_call(
        body, name=name, grid=(rows // tr,), in_specs=[spec] * 4, out_specs=[spec] * 3,
        out_shape=[SDS((rows, cols), F32)] * 3, compiler_params=_params())(w, g, m, v)


SMALL_SIZES = {"mix_norm_w": 1024, "b_gk": 256, "gla_norm_w": 128, "sb_norm_w": 64, "xattn_norm_w": 1024,
               "mem_norm_w": 1024, "mq_norm_w": 256, "mk_norm_w": 256, "ffn_norm_w": 1024}


def _pack_small(d):
    flat = jnp.concatenate([d[n].reshape(-1) for n in SMALL])
    return jnp.pad(flat, (0, SMALL_ROWS * 1024 - flat.shape[0])).reshape(SMALL_ROWS, 1024)


def _unpack_small(p):
    flat = p.reshape(-1)
    out, off = {}, 0
    for n in SMALL:
        out[n] = flat[off:off + SMALL_SIZES[n]].reshape(1, SMALL_SIZES[n])
        off += SMALL_SIZES[n]
    return out


ROWS_OF = (("w_out", 256), ("w_mq", 256), ("w_mo", 256), ("w_down", 704))
WIN_ROWS = 1056
LATER = ("rows", "gate_up", "mkv")


def _shard_buffers(d, dtype):
    rows = jnp.concatenate([d[n] for n, _ in ROWS_OF], axis=0).astype(dtype)
    gk = jnp.pad(d["w_gk_up"], ((0, WIN_ROWS - D - 16), (0, DIN // 4 - 64)))
    win = jnp.concatenate([d["w_in"], gk], axis=0).astype(dtype)
    return [rows, d["w_gate_up"].astype(dtype), d["w_mkv"].astype(dtype)], win


def _in_halves(a):
    return a.reshape(a.shape[:-2] + (2, a.shape[-2] // 2, a.shape[-1]))


def _whole(a):
    return a.reshape(a.shape[:-3] + (2 * a.shape[-2], a.shape[-1]))


def _first_weights(win):
    w_in = win[:, :D].transpose(1, 0, 2).reshape(D, DIN)
    w_in = jnp.concatenate([w_in[:, :1536], w_in[:, 1552:], w_in[:, 1536:1552],
                            jnp.zeros((D, DIN_P - DIN), w_in.dtype)], axis=1)
    return {"w_in": w_in, "w_gk_up": win[:, D:D + 16, :64].transpose(1, 0, 2).reshape(16, 256)}


def _later_weights(rows, gate_up, mkv):
    out, off = {"w_gate_up": gate_up, "w_mkv": mkv}, 0
    for n, r in ROWS_OF:
        out[n] = rows[:, off:off + r].reshape(4 * r, 1024)
        off += r
    return out


def _later_grad_buffers(g):
    rows = jnp.concatenate([g[n].reshape(4, r, 1024) for n, r in ROWS_OF], axis=1)
    return [rows, g["w_gate_up"], g["w_mkv"]]


def _win_grad_buffer(g):
    gk = g["w_gk_up"].astype(BF16).reshape(16, 4, 64).transpose(1, 0, 2)
    gk = jnp.pad(gk, ((0, 0), (0, WIN_ROWS - D - 16), (0, DIN // 4 - 64)))
    gi = g["w_in"]
    gi = jnp.concatenate([gi[:, :1536], gi[:, C_LR:C_LR + 16], gi[:, 1536:C_LR]], axis=1)
    return jnp.concatenate([gi.reshape(D, 4, DIN // 4).transpose(1, 0, 2), gk], axis=1)


def _shard_grads(rows, gate_up, mkv, win):
    out, off = {"w_gate_up": gate_up, "w_mkv": mkv, "w_in": win[:D], "w_gk_up": win[D:D + 16, :64]}, 0
    for n, r in ROWS_OF:
        out[n] = rows[off:off + r]
        off += r
    return out


def kernel(x, mem, mix_norm_w, w_in, w_gk_up, b_gk, gla_norm_w, sb_norm_w, w_out, xattn_norm_w, mem_norm_w, w_mq, w_mkv, mq_norm_w, mk_norm_w, w_mo, ffn_norm_w, w_gate_up, w_down, loss_target, m_mix_norm_w, m_w_in, m_w_gk_up, m_b_gk, m_gla_norm_w, m_sb_norm_w, m_w_out, m_xattn_norm_w, m_mem_norm_w, m_w_mq, m_w_mkv, m_mq_norm_w, m_mk_norm_w, m_w_mo, m_ffn_norm_w, m_w_gate_up, m_w_down, v_mix_norm_w, v_w_in, v_w_gk_up, v_b_gk, v_gla_norm_w, v_sb_norm_w, v_w_out, v_xattn_norm_w, v_mem_norm_w, v_w_mq, v_w_mkv, v_mq_norm_w, v_mk_norm_w, v_w_mo, v_ffn_norm_w, v_w_gate_up, v_w_down):
    args = dict(locals())
    wts = {n: args[n][0] if n in BIG else args[n] for n in WEIGHTS}
    mom = {n: args["m_" + n][0] if n in BIG else args["m_" + n] for n in WEIGHTS}
    var = {n: args["v_" + n][0] if n in BIG else args["v_" + n] for n in WEIGHTS}

    c = lax.axis_index("c")
    chip = 2 * lax.axis_index("x") + lax.axis_index("y")
    c_arr = c.astype(jnp.int32).reshape(1)
    chip_arr = chip.astype(jnp.int32).reshape(1)
    def own_slot_filled(gathered, mine):
        return [_whole(lax.dynamic_update_slice(got, wp[None], (chip, 0, 0, 0))) for got, wp in zip(gathered, mine)]

    def chip_partials(names, buffers, tag):
        gps = [_in_halves(b) for b in buffers]
        return [_rs_add_halves(gp, other, c_arr, "rs_add_halves_" + n)
                for n, gp, other in zip(names, gps, _rs_swap_halves(gps, "rs_swap_halves_" + tag))]

    later_wps, win_wp = _shard_buffers(wts, BF16)
    later_wps, win_wp = [_in_halves(b) for b in later_wps], _in_halves(win_wp)
    first = _first_weights(*own_slot_filled(_ag_weights([win_wp]), [win_wp]))
    first.update({n: wts[n] for n in SMALL})

    loss, grad_x, g, (later_parts, later_recv) = _local_step(
        x[0], mem[0], loss_target[0], first,
        later_weights=(later_wps, lambda gathered: _later_weights(*own_slot_filled(gathered, later_wps))),
        later_partials=lambda g: chip_partials(LATER, _later_grad_buffers(g), "later"))
    loss = lax.psum(loss, ("x", "y", "c"))

    win_parts = chip_partials(("win",), [_win_grad_buffer(g)], "win")
    names = LATER + ("win",)
    mine = [_rs_add_chips(recv, part, chip_arr, "rs_add_chips_" + n)
            for n, recv, part in zip(names, list(later_recv) + list(_rs_exchange(win_parts)),
                                     list(later_parts) + win_parts)]
    totals = [jnp.concatenate([jnp.where(c == 0, m, t), jnp.where(c == 0, t, m)], axis=0)
              for m, t in zip(mine, _rs_share(mine))]
    small_g = _allreduce_small(_pack_small(g))

    grads = _shard_grads(*totals)
    grads.update(_unpack_small(small_g))

    delta, new_m, new_v = {}, {}, {}
    for n in BIG:
        w2 = wts[n].reshape(-1, wts[n].shape[-1])
        d_, m_, v_ = _adamw(w2, grads[n].reshape(w2.shape), mom[n].reshape(w2.shape), var[n].reshape(w2.shape),
                            "adamw_" + n)
        delta[n], new_m[n], new_v[n] = (t.reshape((1,) + wts[n].shape) for t in (d_, m_, v_))
        grads[n] = grads[n].reshape((1,) + wts[n].shape)
    ds, ms, vs_ = _adamw(_pack_small(wts), small_g, _pack_small(mom), _pack_small(var), "adamw_small")
    for dst, src in ((delta, ds), (new_m, ms), (new_v, vs_)):
        dst.update(_unpack_small(src))

    return (loss, grad_x[None], *[grads[n] for n in WEIGHTS], *[delta[n] for n in WEIGHTS],
            *[new_m[n] for n in WEIGHTS], *[new_v[n] for n in WEIGHTS])
```

```python
import functools
import math

import numpy as np
import jax
import jax.numpy as jnp
from jax import lax
from jax.experimental import pallas as pl
from jax.experimental.pallas import tpu as pltpu

F32 = jnp.float32
BF16 = jnp.bfloat16
SDS = jax.ShapeDtypeStruct
MESH = pl.DeviceIdType.MESH

D = 1024
EPS = 1e-6
D_FF = 2816
GLA_GATE_NORM = 16.0
GLA_C = 64
MEM_HEADS = 4
MEM_HD = 256
C_QG, C_KG, C_VG, C_GG, C_QS, C_KS, C_VS, C_LR = 0, 256, 512, 1024, 1536, 2048, 2560, 3072
DIN = 3088
DIN_P = 3200
TB = 512
SBQ = 256
VMEM_LIMIT = 56 * 1024 * 1024
HIGHEST = lax.Precision.HIGHEST

ADAM_LR, ADAM_B1, ADAM_B2, ADAM_EPS, ADAM_WD, ADAM_STEP = 0.001, 0.9, 0.999, 1e-08, 0.01, 10

BIG = ("w_in", "w_gk_up", "w_out", "w_mq", "w_mkv", "w_mo", "w_gate_up", "w_down")
SMALL = ("mix_norm_w", "b_gk", "gla_norm_w", "sb_norm_w", "xattn_norm_w", "mem_norm_w", "mq_norm_w",
         "mk_norm_w", "ffn_norm_w")
WEIGHTS = ("mix_norm_w", "w_in", "w_gk_up", "b_gk", "gla_norm_w", "sb_norm_w", "w_out", "xattn_norm_w",
           "mem_norm_w", "w_mq", "w_mkv", "mq_norm_w", "mk_norm_w", "w_mo", "ffn_norm_w", "w_gate_up", "w_down")
SMALL_ROWS = 16


def _params(**kw):
    return pltpu.CompilerParams(vmem_limit_bytes=VMEM_LIMIT, **kw)


def _row(c, j=0):
    return pl.BlockSpec((TB, c), lambda i, j=j: (i, j))


def _const(shape):
    return pl.BlockSpec(shape, lambda i: (0,) * len(shape))


def _dot(a, b):
    return lax.dot_general(a, b, (((1,), (0,)), ((), ())), preferred_element_type=F32)


def _dot_nt(a, b):
    return lax.dot_general(a, b, (((1,), (1,)), ((), ())), preferred_element_type=F32)


def _dot_tn(a, b):
    return lax.dot_general(a, b, (((0,), (0,)), ((), ())), preferred_element_type=F32)


def _dot_nt_f32(a, b):
    return lax.dot_general(a, b, (((1,), (1,)), ((), ())), precision=HIGHEST, preferred_element_type=F32)


def _split3(x):
    h = x.astype(BF16)
    r = x - h.astype(F32)
    m = r.astype(BF16)
    l = (r - m.astype(F32)).astype(BF16)
    return h, m, l


def _dot_exact(x, ones_mat):
    h, m, l = _split3(x)
    return _dot(h, ones_mat) + _dot(m, ones_mat) + _dot(l, ones_mat)


def _dot_hilo(x, ones_mat):
    h = x.astype(BF16)
    l = (x - h.astype(F32)).astype(BF16)
    return _dot(h, ones_mat) + _dot(l, ones_mat)


def _softplus(z):
    return jnp.maximum(z, 0.0) + jnp.log1p(jnp.exp(-jnp.abs(z)))


def _rsqrt_ms(x):
    return lax.rsqrt(jnp.mean(x * x, axis=-1, keepdims=True) + EPS)


def _colsum8(x):
    r, c = x.shape
    return jnp.sum(x.reshape(r // 8, 8, c), axis=0)


def _matmul(a, b, *, mode, tm, tn, tk=None, res=None, out_dtype=F32, by_column_tile=False, a_spec=None,
            b_spec=None, mnk=None, epilogue=None, name):
    if mnk is not None:
        M, N, K = mnk
    else:
        K, M = a.shape if mode == "tn" else a.shape[::-1]
        N = b.shape[0] if mode == "nt" else b.shape[1]
    tk = K if tk is None else tk
    assert M % tm == 0 and N % tn == 0 and K % tk == 0, (name, M, N, K, tm, tn, tk)
    nk = K // tk
    if a_spec is None:
        if mode == "tn":
            a_spec = pl.BlockSpec((tk, tm), lambda j, i, k: (k, i))
        else:
            a_spec = pl.BlockSpec((tm, tk), lambda j, i, k: (i, k))
    if b_spec is None:
        if mode == "nt":
            b_spec = pl.BlockSpec((tn, tk), lambda j, i, k: (j, k))
        else:
            b_spec = pl.BlockSpec((tk, tn), lambda j, i, k: (k, j))
    if by_column_tile:
        assert res is None
        o_spec = pl.BlockSpec((None, tm, tn), lambda j, i, k: (j, i, 0))
        o_shape = SDS((N // tn, M, tn), out_dtype)
    else:
        o_spec = pl.BlockSpec((tm, tn), lambda j, i, k: (i, j))
        o_shape = SDS((M, N), out_dtype)
    dot = {"nn": _dot, "nt": _dot_nt, "tn": _dot_tn}[mode]
    has_res = res is not None
    extra_in, extra_out, finish = epilogue if epilogue is not None else ((), (), None)
    n_in, n_out = 2 + has_res + len(extra_in), 1 + len(extra_out)

    def body(*refs):
        a_ref, b_ref = refs[0], refs[1]
        res_ref = refs[2] if has_res else None
        in_refs, out_refs = refs[2 + has_res:n_in], refs[n_in:n_in + n_out]
        first_row_tile = pl.program_id(1) == 0

        def done(t):
            if has_res:
                t = t + res_ref[...]
            if finish is None:
                out_refs[0][...] = t.astype(out_dtype)
            else:
                finish(t, first_row_tile, in_refs, out_refs)

        p = dot(a_ref[...].astype(BF16), b_ref[...].astype(BF16))
        if nk == 1:
            done(p)
        else:
            acc_ref = refs[n_in + n_out]
            k = pl.program_id(2)

            @pl.when(k == 0)
            def _():
                acc_ref[...] = p

            @pl.when(k > 0)
            def _():
                acc_ref[...] += p

            @pl.when(k == nk - 1)
            def _():
                done(acc_ref[...])

    in_specs = [a_spec, b_spec] + ([o_spec] if has_res else []) + [s for _, s in extra_in]
    args = (a, b) + ((res,) if has_res else ()) + tuple(x for x, _ in extra_in)
    outs = pl.pallas_call(
        body, name=name, grid=(N // tn, M // tm, nk), in_specs=in_specs,
        out_specs=[o_spec] + [s for _, s in extra_out], out_shape=[o_shape] + [s for s, _ in extra_out],
        scratch_shapes=[pltpu.VMEM((tm, tn), F32)] if nk > 1 else [],
        compiler_params=_params(dimension_semantics=("parallel", "parallel", "arbitrary")),
    )(*args)
    return outs[0] if epilogue is None else outs


def _full_row(tm, c):
    return pl.BlockSpec((tm, c), lambda j, i, k: (i, 0))


def _kept(shape):
    return pl.BlockSpec(shape, lambda j, i, k: (0,) * len(shape))


def _then_norm_fwd(w, T, tm):
    dm = w.shape[1]

    def finish(t, first, ins, outs):
        outs[0][...] = t
        outs[1][...] = (t * _rsqrt_ms(t) * ins[0][...]).astype(BF16)

    return [(w, _kept((1, dm)))], [(SDS((T, dm), BF16), _full_row(tm, dm))], finish


def _then_norm_bwd(x, w, dres, tm):
    T, dm = x.shape

    def finish(t, first, ins, outs):
        x_ref, w_ref, dres_ref = ins

        @pl.when(first)
        def _():
            outs[1][...] = jnp.zeros_like(outs[1])

        xv = x_ref[...]
        r = _rsqrt_ms(xv)
        n = xv * r
        dn = t * w_ref[...]
        outs[0][...] = dres_ref[...] + r * (dn - n * jnp.mean(dn * n, axis=-1, keepdims=True))
        outs[1][...] += _colsum8(t * n)

    return ([(x, _full_row(tm, dm)), (w, _kept((1, dm))), (dres, _full_row(tm, dm))],
            [(SDS((8, dm), F32), _kept((8, dm)))], finish)


def _then_loss(tgt, tm):
    def finish(t, first, ins, outs):
        @pl.when(first)
        def _():
            outs[1][...] = jnp.zeros_like(outs[1])

        e = t - ins[0][...]
        outs[0][...] = e * (1.0 / D)
        outs[1][...] += _colsum8(e * e) * (0.5 / D)

    return [(tgt, _full_row(tm, D))], [(SDS((8, D), F32), _kept((8, D)))], finish


def _norm_fwd(x, w, name):
    T, dm = x.shape

    def body(x_ref, w_ref, h_ref):
        xv = x_ref[...]
        h_ref[...] = (xv * _rsqrt_ms(xv) * w_ref[...]).astype(BF16)

    return pl.pallas_call(
        body, name=name, grid=(T // TB,), in_specs=[_row(dm), _const((1, dm))], out_specs=_row(dm),
        out_shape=SDS((T, dm), BF16), compiler_params=_params())(x, w)


def _then_proj_split(wgk, bgk, T, tm):
    def finish(t, first, ins, outs):
        wgk_ref, b_ref = ins
        proj_ref, gk_ref, qs_ref, ks_ref, vs_ref = outs
        proj_ref[...] = t
        u = _dot(t[:, C_LR:DIN_P].astype(BF16), wgk_ref[...].astype(BF16)) + b_ref[...]
        gk_ref[...] = -_softplus(-u) / GLA_GATE_NORM
        qs_ref[...] = (t[:, C_QS:C_KS] * 0.125).astype(BF16)
        ks_ref[...] = t[:, C_KS:C_VS].astype(BF16)
        vs_ref[...] = t[:, C_VS:C_LR].astype(BF16)

    return ([(wgk, _kept((128, 256))), (bgk, _kept((1, 256)))],
            [(SDS((T, 256), F32), _full_row(tm, 256))] + [(SDS((T, 512), BF16), _full_row(tm, 512))] * 3, finish)


def _dproj_assemble(proj, dq_g, dk_g, dv_g, dg_g, dq_s, dk_s, dv_s, dgk, wgk, bgk):
    T = proj.shape[0]

    def body(lr_ref, dqg_ref, dkg_ref, dvg_ref, dgg_ref, dqs_ref, dks_ref, dvs_ref, dgk_ref, wgk_ref, b_ref,
             dp_ref, dwgk_ref, dbgk_ref):
        @pl.when(pl.program_id(0) == 0)
        def _():
            dwgk_ref[...] = jnp.zeros_like(dwgk_ref)
            dbgk_ref[...] = jnp.zeros_like(dbgk_ref)

        lr = lr_ref[...].astype(BF16)
        wg = wgk_ref[...].astype(BF16)
        u = _dot(lr, wg) + b_ref[...]
        du = dgk_ref[...] * (jax.nn.sigmoid(-u) / GLA_GATE_NORM)
        dub = du.astype(BF16)
        dp_ref[:, C_QG:C_KG] = (dqg_ref[...] * 0.125).astype(BF16)
        dp_ref[:, C_KG:C_VG] = dkg_ref[...].astype(BF16)
        dp_ref[:, C_VG:C_GG] = dvg_ref[...].astype(BF16)
        dp_ref[:, C_GG:C_QS] = dgg_ref[...].astype(BF16)
        dp_ref[:, C_QS:C_KS] = (dqs_ref[...] * 0.125).astype(BF16)
        dp_ref[:, C_KS:C_VS] = dks_ref[...].astype(BF16)
        dp_ref[:, C_VS:C_LR] = dvs_ref[...].astype(BF16)
        dp_ref[:, C_LR:DIN_P] = _dot_nt(dub, wg).astype(BF16)
        dwgk_ref[...] += _dot_tn(lr, dub)
        dbgk_ref[...] += _colsum8(du)

    return pl.pallas_call(
        body, name="dproj_assemble", grid=(T // TB,),
        in_specs=[_row(128, C_LR // 128), _row(256), _row(256), _row(512), _row(512), _row(512), _row(512),
                  _row(512), _row(256), _const((128, 256)), _const((1, 256))],
        out_specs=[_row(DIN_P), _const((128, 256)), _const((8, 256))],
        out_shape=[SDS((T, DIN_P), BF16), SDS((128, 256), F32), SDS((8, 256), F32)],
        compiler_params=_params())(proj, dq_g, dk_g, dv_g, dg_g, dq_s, dk_s, dv_s, dgk, wgk, bgk)


def _group_ones(n, g):
    idx = np.arange(n) // g
    return jnp.asarray((idx[:, None] == idx[None, :]).astype(np.float32), dtype=BF16)


def _mix_cat(o_g, proj, o_s, wg512, ws512, bd64):
    T = o_g.shape[0]

    def body(og_ref, gg_ref, os_ref, wg_ref, ws_ref, bd_ref, cat_ref):
        og = og_ref[...]
        gg = gg_ref[...]
        s = gg * jax.nn.sigmoid(gg)
        for h in range(4):
            sl = slice(128 * h, 128 * (h + 1))
            x = og[:, sl]
            cat_ref[:, sl] = (x * _rsqrt_ms(x) * wg_ref[:, sl] * s[:, sl]).astype(BF16)
        osv = os_ref[...]
        ms = _dot_exact(osv * osv, bd_ref[...]) * (1.0 / 64.0)
        cat_ref[:, 512:1024] = (osv * lax.rsqrt(ms + EPS) * ws_ref[...]).astype(BF16)

    return pl.pallas_call(
        body, name="mix_cat", grid=(T // TB,),
        in_specs=[_row(512), _row(512, C_GG // 512), _row(512), _const((1, 512)), _const((1, 512)),
                  _const((512, 512))],
        out_specs=_row(1024), out_shape=SDS((T, 1024), BF16), compiler_params=_params())(
            o_g, proj, o_s, wg512, ws512, bd64)


def _mix_cat_bwd(dcat, o_g, proj, o_s, wg512, ws512, bd64):
    T = o_g.shape[0]

    def body(dc_ref, og_ref, gg_ref, os_ref, wg_ref, ws_ref, bd_ref, dog_ref, dgg_ref, dos_ref, dwg_ref, dws_ref):
        @pl.when(pl.program_id(0) == 0)
        def _():
            dwg_ref[...] = jnp.zeros_like(dwg_ref)
            dws_ref[...] = jnp.zeros_like(dws_ref)

        og = og_ref[...]
        gg = gg_ref[...]
        sg = jax.nn.sigmoid(gg)
        s = gg * sg
        ds = sg * (1.0 + gg * (1.0 - sg))
        for h in range(4):
            sl = slice(128 * h, 128 * (h + 1))
            x = og[:, sl]
            r = _rsqrt_ms(x)
            n = x * r
            w = wg_ref[:, sl]
            dc = dc_ref[:, sl]
            dy = dc * s[:, sl]
            dgg_ref[:, sl] = dc * (n * w) * ds[:, sl]
            dn = dy * w
            dog_ref[:, sl] = r * (dn - n * jnp.mean(dn * n, axis=-1, keepdims=True))
            dwg_ref[:, sl] += _colsum8(dy * n)
        osv = os_ref[...]
        bd = bd_ref[...]
        r = lax.rsqrt(_dot_exact(osv * osv, bd) * (1.0 / 64.0) + EPS)
        n = osv * r
        dc = dc_ref[:, 512:1024]
        dn = dc * ws_ref[...]
        dos_ref[...] = r * (dn - n * (_dot_exact(dn * n, bd) * (1.0 / 64.0)))
        dws_ref[...] += _colsum8(dc * n)

    return pl.pallas_call(
        body, name="mix_cat_bwd", grid=(T // TB,),
        in_specs=[_row(1024), _row(512), _row(512, C_GG // 512), _row(512), _const((1, 512)), _const((1, 512)),
                  _const((512, 512))],
        out_specs=[_row(512), _row(512), _row(512), _const((8, 512)), _const((8, 512))],
        out_shape=[SDS((T, 512), F32), SDS((T, 512), F32), SDS((T, 512), F32), SDS((8, 512), F32),
                   SDS((8, 512), F32)],
        compiler_params=_params())(dcat, o_g, proj, o_s, wg512, ws512, bd64)


FF_TN = 1408
DW_TK = 2048


def _gate_up_act(h, w):
    T = h.shape[0]
    nj = D_FF // FF_TN

    def body(h_ref, wg_ref, wu_ref, gu_ref, a_ref):
        hv = h_ref[...]
        g = _dot(hv, wg_ref[...])
        u = _dot(hv, wu_ref[...])
        gu_ref[0] = g.astype(BF16)
        gu_ref[1] = u.astype(BF16)
        a_ref[...] = (g * jax.nn.sigmoid(g) * u).astype(BF16)

    return pl.pallas_call(
        body, name="mm_gate_up_act", grid=(nj, T // TB),
        in_specs=[pl.BlockSpec((TB, D), lambda j, i: (i, 0)),
                  pl.BlockSpec((None, D, FF_TN), lambda j, i: (j, 0, 0)),
                  pl.BlockSpec((None, D, FF_TN), lambda j, i: (nj + j, 0, 0))],
        out_specs=[pl.BlockSpec((2, TB, FF_TN), lambda j, i: (0, i, j)),
                   pl.BlockSpec((TB, FF_TN), lambda j, i: (i, j))],
        out_shape=[SDS((2, T, D_FF), BF16), SDS((T, D_FF), BF16)],
        compiler_params=_params(dimension_semantics=("parallel", "parallel")))(h, w, w)


def _down_bwd(dy, w_down, gu):
    T = dy.shape[0]
    nj = D_FF // FF_TN

    def body(dy_ref, w_ref, gu_ref, dgu_ref):
        da = _dot_nt(dy_ref[...].astype(BF16), w_ref[...])
        g = gu_ref[0].astype(F32)
        sg = jax.nn.sigmoid(g)
        dgu_ref[0] = (da * gu_ref[1].astype(F32) * (sg * (1.0 + g * (1.0 - sg)))).astype(BF16)
        dgu_ref[1] = (da * (g * sg)).astype(BF16)

    return pl.pallas_call(
        body, name="mm_down_bwd", grid=(nj, T // TB),
        in_specs=[pl.BlockSpec((TB, D), lambda j, i: (i, 0)),
                  pl.BlockSpec((FF_TN, D), lambda j, i: (j, 0)),
                  pl.BlockSpec((2, TB, FF_TN), lambda j, i: (0, i, j))],
        out_specs=pl.BlockSpec((2, TB, FF_TN), lambda j, i: (0, i, j)),
        out_shape=SDS((2, T, D_FF), BF16),
        compiler_params=_params(dimension_semantics=("parallel", "parallel")))(dy, w_down, gu)


def _gla_consts():
    c = GLA_C
    L = np.tril(np.ones((c, c), np.float32))
    blocks = [L, L[(np.arange(c) // 16) * 16]]
    blocks += [np.repeat(L[16 * i:16 * i + 1], c, axis=0) for i in range(4)]
    blocks.append(np.repeat(L[c - 1:c], c, axis=0))
    return jnp.asarray(np.concatenate(blocks, axis=0))


@jax.custom_vjp
def _gla_lin(cm, g):
    cb = cm.astype(BF16)
    h, m, l = _split3(g)
    y = _dot(cb, h) + _dot(cb, m) + _dot(cb, l)
    return tuple(y[GLA_C * n:GLA_C * (n + 1)] for n in range(7))


def _gla_lin_fwd(cm, g):
    return _gla_lin(cm, g), cm


def _gla_lin_bwd(cm, cts):
    cb = cm.astype(BF16)
    h, m, l = _split3(jnp.concatenate(cts, axis=0))
    return None, _dot_tn(cb, h) + _dot_tn(cb, m) + _dot_tn(cb, l)


_gla_lin.defvjp(_gla_lin_fwd, _gla_lin_bwd)


GLA_SUB = 16
GLA_H = 4
GLA_W = 64 * GLA_H


def _head_lanes(rows, h):
    lane = lax.broadcasted_iota(jnp.int32, (rows, GLA_W), 1)
    return (lane >= 64 * h) & (lane < 64 * (h + 1))


def _gla_scores_terms(qs, k, b, rs, blk):
    row = lax.broadcasted_iota(jnp.int32, (GLA_C, GLA_W), 0)
    keep = row < GLA_SUB * (blk + 1)
    e = jnp.where(keep, jnp.exp(jnp.where(keep, rs[blk] - b, 0.0)), 0.0)
    qb = qs[GLA_SUB * blk:GLA_SUB * (blk + 1)]
    lhs = jnp.concatenate([jnp.where(_head_lanes(GLA_SUB, h), qb, 0.0) for h in range(GLA_H)], axis=0)
    return lhs, e


@jax.custom_vjp
def _gla_scores(qs, k, b, r0, r1, r2, r3):
    rs = (r0, r1, r2, r3)
    per_head = [[] for _ in range(GLA_H)]
    for blk in range(GLA_C // GLA_SUB):
        lhs, e = _gla_scores_terms(qs, k, b, rs, blk)
        a = _dot_nt_f32(lhs, k * e)
        for h in range(GLA_H):
            per_head[h].append(a[GLA_SUB * h:GLA_SUB * (h + 1)])
    return tuple(jnp.concatenate(p, axis=0) for p in per_head)


def _gla_scores_fwd(qs, k, b, r0, r1, r2, r3):
    return _gla_scores(qs, k, b, r0, r1, r2, r3), (qs, k, b, r0, r1, r2, r3)


def _gla_scores_bwd(saved, cts):
    qs, k, b = saved[:3]
    rs = saved[3:]
    dqs, drs = [], []
    dk = jnp.zeros_like(k)
    db = jnp.zeros_like(b)
    for blk in range(GLA_C // GLA_SUB):
        lhs, e = _gla_scores_terms(qs, k, b, rs, blk)
        rows = slice(GLA_SUB * blk, GLA_SUB * (blk + 1))
        da = jnp.concatenate([ct[rows] for ct in cts], axis=0)
        dlhs = lax.dot_general(da, k * e, (((1,), (0,)), ((), ())), precision=HIGHEST, preferred_element_type=F32)
        dq = jnp.zeros((GLA_SUB, GLA_W), F32)
        for h in range(GLA_H):
            dq = jnp.where(_head_lanes(GLA_SUB, h), dlhs[GLA_SUB * h:GLA_SUB * (h + 1)], dq)
        dqs.append(dq)
        dks = lax.dot_general(da, lhs, (((0,), (0,)), ((), ())), precision=HIGHEST, preferred_element_type=F32)
        dk = dk + dks * e
        darg = dks * (k * e)
        db = db - darg
        drs.append(darg)
    return (jnp.concatenate(dqs, axis=0), dk, db, *drs)


_gla_scores.defvjp(_gla_scores_fwd, _gla_scores_bwd)


def _gla_chunk(cm, q, k, g, v0, v1, v2, v3, st):
    c = GLA_C
    vs = (v0, v1, v2, v3)
    ri = lax.broadcasted_iota(jnp.int32, (c, c), 0)
    ci = lax.broadcasted_iota(jnp.int32, (c, c), 1)
    causal = ci <= ri
    b, r, r0, r1, r2, r3, bl = _gla_lin(cm, g)
    scores = _gla_scores(q * jnp.exp(b - r), k, b, r0, r1, r2, r3)
    qe = q * jnp.exp(b)
    kd = k * jnp.exp(bl - b)
    st_new = st * jnp.exp(jnp.concatenate([bl, bl], axis=0))
    outs = []
    for h in range(GLA_H):
        a = jnp.where(causal, scores[h], 0.0)
        outs.append(_dot_nt(jnp.where(_head_lanes(c, h), qe, 0.0), st) + _dot(a, vs[h]))
        st_new = st_new + jnp.where(_head_lanes(2 * c, h), _dot_tn(vs[h], kd), 0.0)
    return (*outs, st_new)


GLA_TB = 512


def _gla_fwd(proj, gk, cm, wps=()):
    T = proj.shape[0]
    nb = T // GLA_TB
    nc = GLA_TB // GLA_C
    n = len(wps)

    def body(q_ref, k_ref, v_ref, g_ref, cm_ref, *rest):
        w_refs, (o_ref, st_ref), out_refs = rest[:n], rest[n:n + 2], rest[n + 2:2 * n + 2]
        st_scr = rest[2 * n + 2]
        step = pl.program_id(0)
        if n:
            ag_start, ag_pass_on, ag_finish = _ag_phases(w_refs, out_refs, *rest[2 * n + 3:])

        @pl.when(step == 0)
        def _():
            st_scr[...] = jnp.zeros_like(st_scr)
            if n:
                ag_start()

        cmv = cm_ref[...]

        def chunk(ci, carry):
            rs = pl.ds(pl.multiple_of(ci * GLA_C, GLA_C), GLA_C)
            st = st_scr[...]
            st_ref[ci] = st
            *outs, st_new = _gla_chunk(cmv, q_ref[rs, :] * 0.125, k_ref[rs, :], g_ref[rs, :],
                                       *[v_ref[rs, 128 * h:128 * (h + 1)] for h in range(GLA_H)], st)
            for h in range(GLA_H):
                o_ref[rs, 128 * h:128 * (h + 1)] = outs[h]
            st_scr[...] = st_new
            return carry

        lax.fori_loop(0, nc, chunk, 0, unroll=True)

        if n:
            pl.when(step == max(nb - 3, 0))(ag_pass_on)
            pl.when(step == nb - 1)(ag_finish)

    return pl.pallas_call(
        body, name="gla_fwd", grid=(nb,),
        in_specs=[pl.BlockSpec((GLA_TB, 256), lambda i: (i, C_QG // 256)),
                  pl.BlockSpec((GLA_TB, 256), lambda i: (i, C_KG // 256)),
                  pl.BlockSpec((GLA_TB, 512), lambda i: (i, C_VG // 512)),
                  pl.BlockSpec((GLA_TB, 256), lambda i: (i, 0)),
                  pl.BlockSpec((7 * GLA_C, GLA_C), lambda i: (0, 0))] + [HBM] * n,
        out_specs=[pl.BlockSpec((GLA_TB, 512), lambda i: (i, 0)),
                   pl.BlockSpec((nc, 128, GLA_W), lambda i: (i, 0, 0))] + [HBM] * n,
        out_shape=[SDS((T, 512), F32), SDS((T // GLA_C, 128, GLA_W), F32)]
        + [SDS((4,) + wp.shape, wp.dtype) for wp in wps],
        scratch_shapes=[pltpu.VMEM((128, GLA_W), F32)] + (_ag_sems(n) if n else []),
        compiler_params=_params(dimension_semantics=("arbitrary",)))(proj, proj, proj, gk, cm, *wps)


def _gla_bwd(proj, gk, cm, states, do, parts=()):
    T = proj.shape[0]
    nb = T // GLA_TB
    nc = GLA_TB // GLA_C
    n = len(parts)

    def body(q_ref, k_ref, v_ref, g_ref, cm_ref, st_ref, do_ref, *rest):
        p_refs, (dq_ref, dk_ref, dv_ref, dg_ref), out_refs = rest[:n], rest[n:n + 4], rest[n + 4:2 * n + 4]
        dst_scr = rest[2 * n + 4]
        step = pl.program_id(0)
        if n:
            rs_start, rs_finish = _rs_exchange_phases(p_refs, out_refs, *rest[2 * n + 5:])

        @pl.when(step == 0)
        def _():
            dst_scr[...] = jnp.zeros_like(dst_scr)
            if n:
                rs_start()

        cmv = cm_ref[...]

        def chunk(t, carry):
            ci = nc - 1 - t
            rs = pl.ds(pl.multiple_of(ci * GLA_C, GLA_C), GLA_C)
            _, vjp = jax.vjp(
                functools.partial(_gla_chunk, cmv), q_ref[rs, :] * 0.125, k_ref[rs, :], g_ref[rs, :],
                *[v_ref[rs, 128 * h:128 * (h + 1)] for h in range(GLA_H)], st_ref[ci])
            dq, dk, dg, *dvs, dst = vjp((*[do_ref[rs, 128 * h:128 * (h + 1)] for h in range(GLA_H)], dst_scr[...]))
            dq_ref[rs, :] = dq
            dk_ref[rs, :] = dk
            dg_ref[rs, :] = dg
            for h in range(GLA_H):
                dv_ref[rs, 128 * h:128 * (h + 1)] = dvs[h]
            dst_scr[...] = dst
            return carry

        lax.fori_loop(0, nc, chunk, 0, unroll=True)

        if n:
            pl.when(step == nb - 1)(rs_finish)

    rev = lambda i: nb - 1 - i
    return pl.pallas_call(
        body, name="gla_bwd", grid=(nb,),
        in_specs=[pl.BlockSpec((GLA_TB, 256), lambda i: (rev(i), C_QG // 256)),
                  pl.BlockSpec((GLA_TB, 256), lambda i: (rev(i), C_KG // 256)),
                  pl.BlockSpec((GLA_TB, 512), lambda i: (rev(i), C_VG // 512)),
                  pl.BlockSpec((GLA_TB, 256), lambda i: (rev(i), 0)),
                  pl.BlockSpec((7 * GLA_C, GLA_C), lambda i: (0, 0)),
                  pl.BlockSpec((nc, 128, GLA_W), lambda i: (rev(i), 0, 0)),
                  pl.BlockSpec((GLA_TB, 512), lambda i: (rev(i), 0))] + [HBM] * n,
        out_specs=[pl.BlockSpec((GLA_TB, 256), lambda i: (rev(i), 0)),
                   pl.BlockSpec((GLA_TB, 256), lambda i: (rev(i), 0)),
                   pl.BlockSpec((GLA_TB, 512), lambda i: (rev(i), 0)),
                   pl.BlockSpec((GLA_TB, 256), lambda i: (rev(i), 0))] + [HBM] * n,
        out_shape=[SDS((T, 256), F32), SDS((T, 256), F32), SDS((T, 512), F32), SDS((T, 256), F32)]
        + [SDS(p.shape, p.dtype) for p in parts],
        scratch_shapes=[pltpu.VMEM((128, GLA_W), F32)] + (_rs_exchange_sems(n) if n else []),
        compiler_params=_params(dimension_semantics=("arbitrary",)))(proj, proj, proj, gk, cm, states, do, *parts)


SB_DEAD = 105.0
SB_COUNT_LANE = 127


def _sb_tri():
    i = np.arange(SBQ)
    return jnp.asarray((i[:, None] > i[None, :]).astype(np.float32), dtype=BF16)


def _sb_block_fwd(qh, kb, tri, carry, strict):
    z = _dot_nt(qh, kb)
    sp = _softplus(z)
    l1 = -sp
    if strict is not None:
        l1 = jnp.where(strict, l1, 0.0)
    log_a = (z - sp) + _dot_hilo(l1, tri) + carry
    a = jnp.exp(log_a)
    if strict is not None:
        a = jnp.where(strict, a, 0.0)
    return z - sp, l1, a


def _sb_fwd(qs, ks, vs, tri):
    T = qs.shape[0]
    nq = T // SBQ

    def body(q_ref, k_ref, v_ref, tri_ref, o_ref, c_ref):
        i = pl.program_id(1)
        lane = lax.broadcasted_iota(jnp.int32, (1, 128), 1)
        clane = lax.broadcasted_iota(jnp.int32, (SBQ, 128), 1)
        strict = (lax.broadcasted_iota(jnp.int32, (SBQ, SBQ), 1) < lax.broadcasted_iota(jnp.int32, (SBQ, SBQ), 0))
        tri_v = tri_ref[...]
        qv = q_ref[...]
        first_head = lane < 64
        qhs = (jnp.where(first_head, qv, jnp.zeros_like(qv)), jnp.where(first_head, jnp.zeros_like(qv), qv))

        def block(j, carries, accs, masked):
            rs = pl.ds(pl.multiple_of(j * SBQ, SBQ), SBQ)
            kb = k_ref[rs, :]
            vb = v_ref[rs, :]
            out_c, out_a = [], []
            for hh in range(2):
                _, l1, a = _sb_block_fwd(qhs[hh], kb, tri_v, carries[hh], strict if masked else None)
                out_a.append(accs[hh] + _dot(a.astype(BF16), vb))
                out_c.append(carries[hh] + jnp.sum(l1, axis=1, keepdims=True))
            return out_c, out_a

        zero1 = jnp.zeros((SBQ, 1), F32)
        zero128 = jnp.zeros((SBQ, 128), F32)
        (c0, c1), (a0, a1) = block(i, (zero1, zero1), (zero128, zero128), True)

        def more(state):
            return (state[0] <= i) & (jnp.maximum(jnp.max(state[1]), jnp.max(state[2])) > -SB_DEAD)

        def step(state):
            jj, c0, c1, a0, a1, t0, t1 = state
            j = i - jj
            t0 = jnp.where(clane == j, c0, t0)
            t1 = jnp.where(clane == j, c1, t1)
            (c0, c1), (a0, a1) = block(j, (c0, c1), (a0, a1), False)
            return jj + 1, c0, c1, a0, a1, t0, t1

        jj, c0, c1, a0, a1, t0, t1 = lax.while_loop(
            more, step, (jnp.int32(1), c0, c1, a0, a1, zero128, zero128))
        o_ref[...] = jnp.where(first_head, a0, a1)
        swept = (jj - 1).astype(F32)
        c_ref[0, :, 0:128] = jnp.where(clane == SB_COUNT_LANE, swept, t0)
        c_ref[0, :, 128:256] = jnp.where(clane == SB_COUNT_LANE, swept, t1)

    return pl.pallas_call(
        body, name="sb_fwd", grid=(4, nq),
        in_specs=[pl.BlockSpec((SBQ, 128), lambda h, i: (i, h)),
                  pl.BlockSpec((T, 128), lambda h, i: (0, h)),
                  pl.BlockSpec((T, 128), lambda h, i: (0, h)),
                  pl.BlockSpec((SBQ, SBQ), lambda h, i: (0, 0))],
        out_specs=[pl.BlockSpec((SBQ, 128), lambda h, i: (i, h)),
                   pl.BlockSpec((1, SBQ, 256), lambda h, i: (h, i, 0))],
        out_shape=[SDS((T, 512), F32), SDS((4, T, 256), F32)],
        compiler_params=_params(dimension_semantics=("parallel", "arbitrary")))(qs, ks, vs, tri)


def _sb_bwd(qs, ks, vs, do, carries, tri, tri_t):
    T = qs.shape[0]
    nq = T // SBQ

    def body(q_ref, k_ref, v_ref, do_ref, c_ref, tri_ref, trit_ref, dq_ref, dk_ref, dv_ref):
        i = pl.program_id(1)

        @pl.when(i == 0)
        def _():
            dk_ref[...] = jnp.zeros_like(dk_ref)
            dv_ref[...] = jnp.zeros_like(dv_ref)

        lane = lax.broadcasted_iota(jnp.int32, (1, 128), 1)
        clane = lax.broadcasted_iota(jnp.int32, (SBQ, 128), 1)
        strict = (lax.broadcasted_iota(jnp.int32, (SBQ, SBQ), 1) < lax.broadcasted_iota(jnp.int32, (SBQ, SBQ), 0))
        tri_v = tri_ref[...]
        trit_v = trit_ref[...]
        qv = q_ref[...]
        dov = do_ref[...].astype(BF16)
        first_head = lane < 64
        qhs = (jnp.where(first_head, qv, jnp.zeros_like(qv)), jnp.where(first_head, jnp.zeros_like(qv), qv))
        dohs = (jnp.where(first_head, dov, jnp.zeros_like(dov)), jnp.where(first_head, jnp.zeros_like(dov), dov))
        cts = (c_ref[0, :, 0:128], c_ref[0, :, 128:256])

        def block(j, pcarries, dqs, masked):
            rs = pl.ds(pl.multiple_of(j * SBQ, SBQ), SBQ)
            kb = k_ref[rs, :]
            vb = v_ref[rs, :]
            out_p, out_q = [], []
            dk = jnp.zeros((SBQ, 128), F32)
            dv = jnp.zeros((SBQ, 128), F32)
            for hh in range(2):
                carry = jnp.sum(jnp.where(clane == j, cts[hh], 0.0), axis=1, keepdims=True)
                lb, _, a = _sb_block_fwd(qhs[hh], kb, tri_v, carry, strict if masked else None)
                g = a * _dot_nt(dohs[hh], vb)
                p = _dot_hilo(g, trit_v) + pcarries[hh]
                dz = g - (g + p) * jnp.exp(lb)
                if masked:
                    dz = jnp.where(strict, dz, 0.0)
                dzb = dz.astype(BF16)
                dk = dk + _dot_tn(dzb, qhs[hh])
                dv = dv + _dot_tn(a.astype(BF16), dohs[hh])
                out_p.append(pcarries[hh] + jnp.sum(g, axis=1, keepdims=True))
                out_q.append(dqs[hh] + _dot(dzb, kb))
            dk_ref[rs, :] += dk
            dv_ref[rs, :] += dv
            return out_p, out_q

        def step(j, state):
            (p0, p1), (q0, q1) = block(j, (state[0], state[1]), (state[2], state[3]), False)
            return p0, p1, q0, q1

        swept = jnp.max(jnp.where(clane == SB_COUNT_LANE, cts[0], 0.0)).astype(jnp.int32)
        first = i - jnp.clip(swept, 0, i)
        zero1 = jnp.zeros((SBQ, 1), F32)
        zero128 = jnp.zeros((SBQ, 128), F32)
        p0, p1, q0, q1 = lax.fori_loop(first, i, step, (zero1, zero1, zero128, zero128))
        _, (q0, q1) = block(i, (p0, p1), (q0, q1), True)
        dq_ref[...] = jnp.where(first_head, q0, q1)

    return pl.pallas_call(
        body, name="sb_bwd", grid=(4, nq),
        in_specs=[pl.BlockSpec((SBQ, 128), lambda h, i: (i, h)),
                  pl.BlockSpec((T, 128), lambda h, i: (0, h)),
                  pl.BlockSpec((T, 128), lambda h, i: (0, h)),
                  pl.BlockSpec((SBQ, 128), lambda h, i: (i, h)),
                  pl.BlockSpec((1, SBQ, 256), lambda h, i: (h, i, 0)),
                  pl.BlockSpec((SBQ, SBQ), lambda h, i: (0, 0)),
                  pl.BlockSpec((SBQ, SBQ), lambda h, i: (0, 0))],
        out_specs=[pl.BlockSpec((SBQ, 128), lambda h, i: (i, h)),
                   pl.BlockSpec((T, 128), lambda h, i: (0, h)),
                   pl.BlockSpec((T, 128), lambda h, i: (0, h))],
        out_shape=[SDS((T, 512), F32), SDS((T, 512), F32), SDS((T, 512), F32)],
        compiler_params=_params(dimension_semantics=("parallel", "arbitrary")))(qs, ks, vs, do, carries, tri, tri_t)


def _mem_fwd(mem, mem_norm_w, w_mkv, mk_norm_w):
    M = mem.shape[0]

    def body(mem_ref, wn_ref, w_ref, wk_ref, mn_ref, kraw_ref, k_ref, v_ref):
        mv = mem_ref[...]
        mn = (mv * _rsqrt_ms(mv) * wn_ref[...]).astype(BF16)
        mn_ref[...] = mn
        for s in range(2):
            cols = slice(512 * s, 512 * (s + 1))
            ks = _dot(mn, w_ref[s, :D, :])
            kraw_ref[:, cols] = ks
            v_ref[:, cols] = _dot(mn, w_ref[2 + s, :D, :]).astype(BF16)
            for h in range(2):
                x = ks[:, MEM_HD * h:MEM_HD * (h + 1)]
                k_ref[:, 512 * s + MEM_HD * h:512 * s + MEM_HD * (h + 1)] = (
                    x * _rsqrt_ms(x) * wk_ref[...]).astype(BF16)

    vm = pl.BlockSpec(memory_space=pltpu.VMEM)
    return pl.pallas_call(
        body, name="mem_fwd", in_specs=[vm] * 4, out_specs=[vm] * 4,
        out_shape=[SDS((M, D), BF16), SDS((M, D), F32), SDS((M, D), BF16), SDS((M, D), BF16)],
        compiler_params=_params())(mem, mem_norm_w, w_mkv, mk_norm_w)


def _xattn_fwd(qraw, k, v, wq):
    T = qraw.shape[0]
    M = k.shape[0]

    def body(q_ref, k_ref, v_ref, wq_ref, o_ref):
        for h in range(MEM_HEADS):
            sl = slice(MEM_HD * h, MEM_HD * (h + 1))
            x = q_ref[:, sl]
            q = (x * _rsqrt_ms(x) * wq_ref[...]).astype(BF16)
            s = _dot_nt(q, k_ref[:, sl]) * (1.0 / math.sqrt(MEM_HD))
            s = s - jnp.max(s, axis=-1, keepdims=True)
            e = jnp.exp(s)
            p = e / jnp.sum(e, axis=-1, keepdims=True)
            o_ref[:, sl] = _dot(p.astype(BF16), v_ref[:, sl]).astype(BF16)

    return pl.pallas_call(
        body, name="xattn_fwd", grid=(T // TB,),
        in_specs=[_row(D), _const((M, D)), _const((M, D)), _const((1, MEM_HD))],
        out_specs=_row(D), out_shape=SDS((T, D), BF16), compiler_params=_params())(qraw, k, v, wq)


def _xattn_bwd(qraw, k, v, wq, do):
    T = qraw.shape[0]
    M = k.shape[0]

    def body(q_ref, k_ref, v_ref, wq_ref, do_ref, dq_ref, dk_ref, dv_ref, dw_ref):
        @pl.when(pl.program_id(0) == 0)
        def _():
            dk_ref[...] = jnp.zeros_like(dk_ref)
            dv_ref[...] = jnp.zeros_like(dv_ref)
            dw_ref[...] = jnp.zeros_like(dw_ref)

        w = wq_ref[...]
        for h in range(MEM_HEADS):
            sl = slice(MEM_HD * h, MEM_HD * (h + 1))
            x = q_ref[:, sl]
            r = _rsqrt_ms(x)
            n = x * r
            q = (n * w).astype(BF16)
            kb = k_ref[:, sl]
            s = _dot_nt(q, kb) * (1.0 / math.sqrt(MEM_HD))
            s = s - jnp.max(s, axis=-1, keepdims=True)
            e = jnp.exp(s)
            p = e / jnp.sum(e, axis=-1, keepdims=True)
            dob = do_ref[:, sl].astype(BF16)
            dp = _dot_nt(dob, v_ref[:, sl])
            ds = (p * (dp - jnp.sum(dp * p, axis=-1, keepdims=True)) * (1.0 / math.sqrt(MEM_HD))).astype(BF16)
            dv_ref[:, sl] += _dot_tn(p.astype(BF16), dob)
            dk_ref[:, sl] += _dot_tn(ds, q)
            dqn = _dot(ds, kb)
            dn = dqn * w
            dq_ref[:, sl] = r * (dn - n * jnp.mean(dn * n, axis=-1, keepdims=True))
            dw_ref[...] += _colsum8(dqn * n)

    return pl.pallas_call(
        body, name="xattn_bwd", grid=(T // TB,),
        in_specs=[_row(D), _const((M, D)), _const((M, D)), _const((1, MEM_HD)), _row(D)],
        out_specs=[_row(D), _const((M, D)), _const((M, D)), _const((8, MEM_HD))],
        out_shape=[SDS((T, D), F32), SDS((M, D), F32), SDS((M, D), F32), SDS((8, MEM_HD), F32)],
        compiler_params=_params())(qraw, k, v, wq, do)


def _mem_bwd(mem, mem_norm_w, w_mkv, mk_norm_w, mem_n, k_raw, dk, dv):
    M = mem.shape[0]

    def body(mem_ref, wn_ref, w_ref, wk_ref, mn_ref, kraw_ref, dk_ref, dv_ref, dw_ref, dwn_ref, dwk_ref, dkv_scr):
        wk = wk_ref[...]
        dwk = jnp.zeros((8, MEM_HD), F32)
        for h in range(MEM_HEADS):
            sl = slice(MEM_HD * h, MEM_HD * (h + 1))
            x = kraw_ref[:, sl]
            r = _rsqrt_ms(x)
            n = x * r
            dkh = dk_ref[:, sl]
            dn = dkh * wk
            dkv_scr[:, sl] = (r * (dn - n * jnp.mean(dn * n, axis=-1, keepdims=True))).astype(BF16)
            dwk = dwk + _colsum8(dkh * n)
        dwk_ref[...] = dwk
        dkv_scr[:, D:] = dv_ref[...].astype(BF16)
        dkv = dkv_scr[...]
        mn = mn_ref[...]
        dmn = jnp.zeros((M, D), F32)
        for s in range(4):
            part = dkv[:, 512 * s:512 * (s + 1)]
            dw_ref[s] = _dot_tn(mn, part).astype(BF16)
            dmn = dmn + _dot_nt(part, w_ref[s, :D, :])
        mv = mem_ref[...]
        dwn_ref[...] = _colsum8(dmn * (mv * _rsqrt_ms(mv)))

    vm = pl.BlockSpec(memory_space=pltpu.VMEM)
    return pl.pallas_call(
        body, name="mem_bwd", in_specs=[vm] * 8, out_specs=[vm] * 3,
        out_shape=[SDS((4, D, 512), BF16), SDS((8, D), F32), SDS((8, MEM_HD), F32)],
        scratch_shapes=[pltpu.VMEM((M, 2 * D), BF16)],
        compiler_params=_params())(mem, mem_norm_w, w_mkv, mk_norm_w, mem_n, k_raw, dk, dv)


def _local_step(x, mem, tgt, w, later_weights=None, later_partials=None):
    T = x.shape[0]
    wgk = jnp.zeros((128, 256), F32).at[:16].set(w["w_gk_up"].astype(F32))
    wg512 = jnp.tile(w["gla_norm_w"], (1, 4))
    ws512 = jnp.tile(w["sb_norm_w"], (1, 8))
    bd64 = _group_ones(512, 64)
    cm = _gla_consts()
    tri = _sb_tri()
    tri_t = tri.T

    h1 = _norm_fwd(x, w["mix_norm_w"], "norm1_fwd")
    proj, gk, qs, ks, vs = _matmul(h1, w["w_in"], mode="nn", tm=TB, tn=DIN_P, name="mm_proj",
                                   epilogue=_then_proj_split(wgk, w["b_gk"], T, TB))
    if later_weights is None:
        o_g, states = _gla_fwd(proj, gk, cm)
    else:
        o_g, states, *gathered = _gla_fwd(proj, gk, cm, later_weights[0])
        w = {**w, **later_weights[1](gathered)}
    o_s, carries = _sb_fwd(qs, ks, vs, tri)
    cat = _mix_cat(o_g, proj, o_s, wg512, ws512, bd64)
    x1, h2 = _matmul(cat, w["w_out"], mode="nn", tm=TB, tn=D, res=x, name="mm_out",
                     epilogue=_then_norm_fwd(w["xattn_norm_w"], T, TB))
    qraw = _matmul(h2, w["w_mq"], mode="nn", tm=TB, tn=D, name="mm_mq")
    mem_n, k_raw, k_n, v_m = _mem_fwd(mem, w["mem_norm_w"], w["w_mkv"], w["mk_norm_w"])
    om = _xattn_fwd(qraw, k_n, v_m, w["mq_norm_w"])
    x2, h3 = _matmul(om, w["w_mo"], mode="nn", tm=TB, tn=D, res=x1, name="mm_mo",
                     epilogue=_then_norm_fwd(w["ffn_norm_w"], T, TB))
    gu, act = _gate_up_act(h3, w["w_gate_up"])
    dx3, loss_rows = _matmul(act, w["w_down"], mode="nn", tm=TB, tn=D, res=x2, name="mm_down",
                             epilogue=_then_loss(tgt, TB))

    g = {}
    dw_tk = min(DW_TK, T)
    dw = dict(mode="tn", tk=dw_tk, out_dtype=BF16)
    g["w_down"] = _matmul(act, dx3, tm=1408, tn=D, name="mm_dw_down", **dw)
    dgu = _down_bwd(dx3, w["w_down"], gu)
    g["w_gate_up"] = _matmul(
        h3, dgu, tm=D, tn=FF_TN, by_column_tile=True, mnk=(D, 2 * D_FF, T), name="mm_dw_gate_up",
        b_spec=pl.BlockSpec((None, dw_tk, FF_TN), lambda j, i, k: (j // 2, k, j % 2)), **dw)
    tm3 = min(2 * TB, T)
    dx2, g["ffn_norm_w"] = _matmul(
        dgu, w["w_gate_up"], mode="nt", tm=tm3, tn=D, tk=FF_TN, mnk=(T, D, 2 * D_FF), name="mm_dh3",
        a_spec=pl.BlockSpec((None, tm3, FF_TN), lambda j, i, k: (k // 2, i, k % 2)),
        b_spec=pl.BlockSpec((None, D, FF_TN), lambda j, i, k: (k, 0, 0)),
        epilogue=_then_norm_bwd(x2, w["ffn_norm_w"], dx3, tm3))
    g["w_mo"] = _matmul(om, dx2, tm=D, tn=D, name="mm_dw_mo", **dw)
    dom = _matmul(dx2, w["w_mo"], mode="nt", tm=TB, tn=D, name="mm_dom")
    dqraw, dk_n, dv_m, g["mq_norm_w"] = _xattn_bwd(qraw, k_n, v_m, w["mq_norm_w"], dom)
    g["w_mkv"], g["mem_norm_w"], g["mk_norm_w"] = _mem_bwd(
        mem, w["mem_norm_w"], w["w_mkv"], w["mk_norm_w"], mem_n, k_raw, dk_n, dv_m)
    g["w_mq"] = _matmul(h2, dqraw, tm=D, tn=D, name="mm_dw_mq", **dw)
    dx1, g["xattn_norm_w"] = _matmul(dqraw, w["w_mq"], mode="nt", tm=TB, tn=D, name="mm_dh2",
                                     epilogue=_then_norm_bwd(x1, w["xattn_norm_w"], dx2, TB))
    g["w_out"] = _matmul(cat, dx1, tm=D, tn=D, name="mm_dw_out", **dw)
    dcat = _matmul(dx1, w["w_out"], mode="nt", tm=TB, tn=D, name="mm_dcat")
    do_g, dg_g, do_s, dwg, dws = _mix_cat_bwd(dcat, o_g, proj, o_s, wg512, ws512, bd64)
    dq_s, dk_s, dv_s = _sb_bwd(qs, ks, vs, do_s, carries, tri, tri_t)
    parts = () if later_partials is None else later_partials(g)
    dq_g, dk_g, dv_g, dgk, *received = _gla_bwd(proj, gk, cm, states, do_g, parts)
    dproj, dwgk, g["b_gk"] = _dproj_assemble(proj, dq_g, dk_g, dv_g, dg_g, dq_s, dk_s, dv_s, dgk, wgk, w["b_gk"])
    g["w_in"] = _matmul(h1, dproj, tm=D, tn=640, name="mm_dw_in", **dw)
    grad_x, g["mix_norm_w"] = _matmul(dproj, w["w_in"], mode="nt", tm=TB, tn=D, name="mm_dh1",
                                      epilogue=_then_norm_bwd(x, w["mix_norm_w"], dx1, TB))

    g["w_gk_up"] = dwgk[:16]
    g["gla_norm_w"], g["sb_norm_w"] = dwg, dws
    if later_partials is None:
        g["gla_norm_w"] = dwg.reshape(8, 4, 128).sum(axis=1)
        g["sb_norm_w"] = dws.reshape(8, 8, 64).sum(axis=1)
        for n in SMALL:
            g[n] = jnp.sum(g[n], axis=0, keepdims=True)
        return jnp.sum(loss_rows), grad_x, g
    return jnp.sum(loss_rows), grad_x, g, (parts, received)


def _mesh_pos():
    return lax.axis_index("x"), lax.axis_index("y"), lax.axis_index("c")


def _other_chips(x, y):
    return [(1 - x, y), (x, 1 - y), (1 - x, 1 - y)]


HBM = pl.BlockSpec(memory_space=pl.ANY)


def _ag_phases(w_refs, out_refs, send_sems, recv_sems):
    n = len(w_refs)
    x, y, c = _mesh_pos()
    me = 2 * x + y
    sibling = (x, y, 1 - c)
    chips = _other_chips(x, y)
    mine, theirs = c, 1 - c

    def copy(a, k, src, dst, to):
        return pltpu.make_async_remote_copy(src_ref=src, dst_ref=dst, send_sem=send_sems.at[6 * a + k],
                                            recv_sem=recv_sems.at[6 * a + k], device_id=to, device_id_type=MESH)

    def firsts():
        return [copy(a, k, w_refs[a].at[mine], out_refs[a].at[me, mine], (cx, cy, c))
                for a in range(n) for k, (cx, cy) in enumerate(chips)]

    def landed(a, k, half):
        cx, cy = chips[k]
        return out_refs[a].at[2 * cx + cy, half]

    def passes():
        return [copy(a, 3 + k, landed(a, k, mine), landed(a, k, mine), sibling) for a in range(n) for k in range(3)]

    def start():
        for cp in firsts():
            cp.start()

    def pass_on():
        for a in range(n):
            for k, (cx, cy) in enumerate(chips):
                copy(a, k, landed(a, k, mine), landed(a, k, mine), (cx, cy, c)).wait_recv()
                copy(a, 3 + k, landed(a, k, mine), landed(a, k, mine), sibling).start()

    def finish():
        for a in range(n):
            for k in range(3):
                copy(a, 3 + k, landed(a, k, theirs), landed(a, k, theirs), sibling).wait_recv()
        for cp in firsts() + passes():
            cp.wait_send()

    return start, pass_on, finish


def _ag_sems(n):
    return [pltpu.SemaphoreType.DMA((6 * n,)), pltpu.SemaphoreType.DMA((6 * n,))]


def _ag_weights(wps):
    n = len(wps)

    def body(*refs):
        for phase in _ag_phases(refs[:n], refs[n:2 * n], *refs[2 * n:]):
            phase()

    return pl.pallas_call(
        body, name="ag_weights", in_specs=[HBM] * n, out_specs=[HBM] * n,
        out_shape=[SDS((4,) + wp.shape, wp.dtype) for wp in wps], scratch_shapes=_ag_sems(n),
        compiler_params=pltpu.CompilerParams(has_side_effects=True))(*wps)


def _rs_swap_halves(gps, name):
    n = len(gps)

    def body(*refs):
        g_refs, out_refs = refs[:n], refs[n:2 * n]
        send_sems, recv_sems = refs[2 * n:]
        x, y, c = _mesh_pos()
        copies = [pltpu.make_async_remote_copy(
            src_ref=g_refs[a].at[:, 1 - c], dst_ref=out_refs[a], send_sem=send_sems.at[a], recv_sem=recv_sems.at[a],
            device_id=(x, y, 1 - c), device_id_type=MESH) for a in range(n)]
        for cp in copies:
            cp.start()
        for cp in copies:
            cp.wait()

    return pl.pallas_call(
        body, name=name, in_specs=[HBM] * n, out_specs=[HBM] * n,
        out_shape=[SDS((4,) + gp.shape[2:], gp.dtype) for gp in gps],
        scratch_shapes=[pltpu.SemaphoreType.DMA((n,)), pltpu.SemaphoreType.DMA((n,))],
        compiler_params=pltpu.CompilerParams(has_side_effects=True))(*gps)


def _rs_add_halves(gp, other, c_arr, name):
    h, w = other.shape[1:]

    def body(c_ref, a_ref, b_ref, o_ref):
        o_ref[...] = (a_ref[0].astype(F32) + b_ref[...].astype(F32)).astype(BF16)

    return pl.pallas_call(
        body, name=name,
        grid_spec=pltpu.PrefetchScalarGridSpec(
            num_scalar_prefetch=1, grid=(4,),
            in_specs=[pl.BlockSpec((1, 1, h, w), lambda s, c: (s, c[0], 0, 0)),
                      pl.BlockSpec((1, h, w), lambda s, c: (s, 0, 0))],
            out_specs=pl.BlockSpec((1, h, w), lambda s, c: (s, 0, 0))),
        out_shape=SDS((4, h, w), BF16), compiler_params=_params())(c_arr, gp, other)


def _rs_exchange_phases(p_refs, out_refs, send_sems, recv_sems):
    n = len(p_refs)
    x, y, c = _mesh_pos()
    me = 2 * x + y
    chips = _other_chips(x, y)

    def sends():
        return [pltpu.make_async_remote_copy(
            src_ref=p_refs[a].at[2 * cx + cy], dst_ref=out_refs[a].at[me], send_sem=send_sems.at[3 * a + k],
            recv_sem=recv_sems.at[3 * a + k], device_id=(cx, cy, c), device_id_type=MESH)
            for a in range(n) for k, (cx, cy) in enumerate(chips)]

    def start():
        for cp in sends():
            cp.start()

    def finish():
        for a in range(n):
            for k, (cx, cy) in enumerate(chips):
                slot = out_refs[a].at[2 * cx + cy]
                pltpu.make_async_remote_copy(
                    src_ref=slot, dst_ref=slot, send_sem=send_sems.at[3 * a + k], recv_sem=recv_sems.at[3 * a + k],
                    device_id=(cx, cy, c), device_id_type=MESH).wait_recv()
        for cp in sends():
            cp.wait_send()

    return start, finish


def _rs_exchange_sems(n):
    return [pltpu.SemaphoreType.DMA((3 * n,)), pltpu.SemaphoreType.DMA((3 * n,))]


def _rs_exchange(parts):
    n = len(parts)

    def body(*refs):
        for phase in _rs_exchange_phases(refs[:n], refs[n:2 * n], *refs[2 * n:]):
            phase()

    return pl.pallas_call(
        body, name="rs_exchange", in_specs=[HBM] * n, out_specs=[HBM] * n,
        out_shape=[SDS(p.shape, p.dtype) for p in parts], scratch_shapes=_rs_exchange_sems(n),
        compiler_params=pltpu.CompilerParams(has_side_effects=True))(*parts)


def _rs_add_chips(recv, part, me_arr, name):
    h, w = part.shape[1:]
    th = h // 2 if (h // 2) % 16 == 0 else h

    def body(me_ref, r_ref, p_ref, o_ref):
        me = me_ref[0]
        total = None
        for k in range(4):
            term = jnp.where(me == k, p_ref[k], r_ref[k]).astype(F32)
            total = term if total is None else total + term
        o_ref[...] = total

    spec = pl.BlockSpec((4, th, w), lambda t, me: (0, t, 0))
    return pl.pallas_call(
        body, name=name,
        grid_spec=pltpu.PrefetchScalarGridSpec(
            num_scalar_prefetch=1, grid=(h // th,), in_specs=[spec, spec],
            out_specs=pl.BlockSpec((th, w), lambda t, me: (t, 0))),
        out_shape=SDS((h, w), F32), compiler_params=_params())(me_arr, recv, part)


def _rs_share(halves):
    n = len(halves)

    def body(*refs):
        h_refs, out_refs = refs[:n], refs[n:2 * n]
        send_sems, recv_sems = refs[2 * n:]
        x, y, c = _mesh_pos()
        copies = [pltpu.make_async_remote_copy(
            src_ref=h_refs[a], dst_ref=out_refs[a], send_sem=send_sems.at[a], recv_sem=recv_sems.at[a],
            device_id=(x, y, 1 - c), device_id_type=MESH) for a in range(n)]
        for cp in copies:
            cp.start()
        for cp in copies:
            cp.wait()

    return pl.pallas_call(
        body, name="rs_share", in_specs=[HBM] * n, out_specs=[HBM] * n,
        out_shape=[SDS(hs.shape, hs.dtype) for hs in halves],
        scratch_shapes=[pltpu.SemaphoreType.DMA((n,)), pltpu.SemaphoreType.DMA((n,))],
        compiler_params=pltpu.CompilerParams(has_side_effects=True))(*halves)


def _small_rows(partials):
    lanes = np.arange(512)
    fold = jnp.asarray((lanes[:, None] % 64 == np.arange(128)[None, :]).astype(np.float32), dtype=BF16)
    n_small = len(SMALL)

    def body(*refs):
        fold_ref, o_ref = refs[n_small], refs[n_small + 1]
        o_ref[...] = jnp.zeros_like(o_ref)
        for i, name in enumerate(SMALL):
            v = refs[i][...]
            if name == "gla_norm_w":
                v = v[:, 0:128] + v[:, 128:256] + v[:, 256:384] + v[:, 384:512]
            if name == "sb_norm_w":
                v = _dot_exact(v, fold_ref[...])
            n = SMALL_SIZES[name]
            o_ref[i:i + 1, 0:n] = jnp.sum(v, axis=0, keepdims=True)[:, 0:n]

    vm = pl.BlockSpec(memory_space=pltpu.VMEM)
    return pl.pallas_call(body, name="small_rows", in_specs=[vm] * (n_small + 1), out_specs=vm,
                          out_shape=SDS((SMALL_ROWS, 1024), F32))(*[partials[n] for n in SMALL], fold)


def _small_gather(s):
    def gather(s_ref, out_ref, send_sems, recv_sems, local_sem):
        x, y, c = _mesh_pos()
        me = 4 * x + 2 * y + c
        local = pltpu.make_async_copy(s_ref, out_ref.at[me], local_sem)
        local.start()
        peers = []
        for r in range(1, 8):
            px = 1 - x if r & 4 else x
            py = 1 - y if r & 2 else y
            pc = 1 - c if r & 1 else c
            peers.append((px, py, pc))
        sends = []
        for k, peer in enumerate(peers):
            cp = pltpu.make_async_remote_copy(
                src_ref=s_ref, dst_ref=out_ref.at[me], send_sem=send_sems.at[k], recv_sem=recv_sems.at[k],
                device_id=peer, device_id_type=MESH)
            cp.start()
            sends.append(cp)
        for k, (px, py, pc) in enumerate(peers):
            slot = out_ref.at[4 * px + 2 * py + pc]
            pltpu.make_async_remote_copy(
                src_ref=slot, dst_ref=slot, send_sem=send_sems.at[k], recv_sem=recv_sems.at[k],
                device_id=(px, py, pc), device_id_type=MESH).wait_recv()
        for cp in sends:
            cp.wait_send()
        local.wait()

    return pl.pallas_call(
        gather, name="small_gather", in_specs=[HBM], out_specs=HBM,
        out_shape=SDS((8, SMALL_ROWS, 1024), F32),
        scratch_shapes=[pltpu.SemaphoreType.DMA((7,)), pltpu.SemaphoreType.DMA((7,)), pltpu.SemaphoreType.DMA],
        compiler_params=pltpu.CompilerParams(has_side_effects=True))(s)


def _adamw_update(w, g, m, v):
    mn = ADAM_B1 * m + (1.0 - ADAM_B1) * g
    vn = ADAM_B2 * v + (1.0 - ADAM_B2) * (g * g)
    c1 = 1.0 - ADAM_B1 ** ADAM_STEP
    c2 = 1.0 - ADAM_B2 ** ADAM_STEP
    return -ADAM_LR * ((mn / c1) / (jnp.sqrt(vn / c2) + ADAM_EPS) + ADAM_WD * w), mn, vn


def _small_update(parts, w, m, v):
    n_small = len(SMALL)

    def body(p_ref, *refs):
        ins, outs = refs[:3 * n_small], refs[3 * n_small:]
        total = p_ref[0]
        for k in range(1, 8):
            total = total + p_ref[k]
        for i, name in enumerate(SMALL):
            g = total[i:i + 1, 0:SMALL_SIZES[name]]
            d, mn, vn = _adamw_update(ins[i][...], g, ins[n_small + i][...], ins[2 * n_small + i][...])
            for slot, val in enumerate((g, d, mn, vn)):
                outs[slot * n_small + i][...] = val

    vm = pl.BlockSpec(memory_space=pltpu.VMEM)
    shapes = [SDS((1, SMALL_SIZES[n]), F32) for n in SMALL]
    outs = pl.pallas_call(
        body, name="small_adamw", in_specs=[vm] * (1 + 3 * n_small), out_specs=[vm] * (4 * n_small),
        out_shape=shapes * 4)(parts, *[w[n] for n in SMALL], *[m[n] for n in SMALL], *[v[n] for n in SMALL])
    return [dict(zip(SMALL, outs[s * n_small:(s + 1) * n_small])) for s in range(4)]


def _adamw(w, g, m, v, name):
    rows, cols = w.shape
    tr = rows
    for cand in (512, 352, 256):
        if rows > cand and rows % cand == 0:
            tr = cand
            break

    def body(w_ref, g_ref, m_ref, v_ref, d_ref, mo_ref, vo_ref):
        d_ref[...], mo_ref[...], vo_ref[...] = _adamw_update(w_ref[...], g_ref[...], m_ref[...], v_ref[...])

    spec = pl.BlockSpec((tr, cols), lambda i: (i, 0))
    return pl.pallas_call(
        body, name=name, grid=(rows // tr,), in_specs=[spec] * 4, out_specs=[spec] * 3,
        out_shape=[SDS((rows, cols), F32)] * 3, compiler_params=_params())(w, g, m, v)


SMALL_SIZES = {"mix_norm_w": 1024, "b_gk": 256, "gla_norm_w": 128, "sb_norm_w": 64, "xattn_norm_w": 1024,
               "mem_norm_w": 1024, "mq_norm_w": 256, "mk_norm_w": 256, "ffn_norm_w": 1024}


ROWS_OF = (("w_out", 256), ("w_mq", 256), ("w_mo", 256), ("w_down", 704))
WIN_ROWS = 1056
LATER = ("rows", "gate_up", "mkv")


def _shard_buffers(d, dtype):
    rows = jnp.concatenate([d[n] for n, _ in ROWS_OF], axis=0).astype(dtype)
    gk = jnp.pad(d["w_gk_up"], ((0, WIN_ROWS - D - 16), (0, DIN // 4 - 64)))
    win = jnp.concatenate([d["w_in"], gk], axis=0).astype(dtype)
    return [rows, d["w_gate_up"].astype(dtype), d["w_mkv"].astype(dtype)], win


def _in_halves(a):
    return a.reshape(a.shape[:-2] + (2, a.shape[-2] // 2, a.shape[-1]))


def _whole(a):
    return a.reshape(a.shape[:-3] + (2 * a.shape[-2], a.shape[-1]))


def _first_weights(win):
    w_in = win[:, :D].transpose(1, 0, 2).reshape(D, DIN)
    w_in = jnp.concatenate([w_in[:, :1536], w_in[:, 1552:], w_in[:, 1536:1552],
                            jnp.zeros((D, DIN_P - DIN), w_in.dtype)], axis=1)
    return {"w_in": w_in, "w_gk_up": win[:, D:D + 16, :64].transpose(1, 0, 2).reshape(16, 256)}


def _later_weights(rows, gate_up, mkv):
    out, off = {"w_gate_up": gate_up, "w_mkv": mkv}, 0
    for n, r in ROWS_OF:
        out[n] = rows[:, off:off + r].reshape(4 * r, 1024)
        off += r
    return out


def _later_grad_buffers(g):
    rows = jnp.concatenate([g[n].reshape(4, r, 1024) for n, r in ROWS_OF], axis=1)
    return [rows, g["w_gate_up"], g["w_mkv"]]


def _win_grad_buffer(g):
    gk = g["w_gk_up"].astype(BF16).reshape(16, 4, 64).transpose(1, 0, 2)
    gk = jnp.pad(gk, ((0, 0), (0, WIN_ROWS - D - 16), (0, DIN // 4 - 64)))
    gi = g["w_in"]
    gi = jnp.concatenate([gi[:, :1536], gi[:, C_LR:C_LR + 16], gi[:, 1536:C_LR]], axis=1)
    return jnp.concatenate([gi.reshape(D, 4, DIN // 4).transpose(1, 0, 2), gk], axis=1)


def _shard_grads(rows, gate_up, mkv, win):
    out, off = {"w_gate_up": gate_up, "w_mkv": mkv, "w_in": win[:D], "w_gk_up": win[D:D + 16, :64]}, 0
    for n, r in ROWS_OF:
        out[n] = rows[off:off + r]
        off += r
    return out


def kernel(x, mem, mix_norm_w, w_in, w_gk_up, b_gk, gla_norm_w, sb_norm_w, w_out, xattn_norm_w, mem_norm_w, w_mq, w_mkv, mq_norm_w, mk_norm_w, w_mo, ffn_norm_w, w_gate_up, w_down, loss_target, m_mix_norm_w, m_w_in, m_w_gk_up, m_b_gk, m_gla_norm_w, m_sb_norm_w, m_w_out, m_xattn_norm_w, m_mem_norm_w, m_w_mq, m_w_mkv, m_mq_norm_w, m_mk_norm_w, m_w_mo, m_ffn_norm_w, m_w_gate_up, m_w_down, v_mix_norm_w, v_w_in, v_w_gk_up, v_b_gk, v_gla_norm_w, v_sb_norm_w, v_w_out, v_xattn_norm_w, v_mem_norm_w, v_w_mq, v_w_mkv, v_mq_norm_w, v_mk_norm_w, v_w_mo, v_ffn_norm_w, v_w_gate_up, v_w_down):
    args = dict(locals())
    wts = {n: args[n][0] if n in BIG else args[n] for n in WEIGHTS}
    mom = {n: args["m_" + n][0] if n in BIG else args["m_" + n] for n in WEIGHTS}
    var = {n: args["v_" + n][0] if n in BIG else args["v_" + n] for n in WEIGHTS}

    c = lax.axis_index("c")
    chip = 2 * lax.axis_index("x") + lax.axis_index("y")
    c_arr = c.astype(jnp.int32).reshape(1)
    chip_arr = chip.astype(jnp.int32).reshape(1)
    def own_slot_filled(gathered, mine):
        return [_whole(lax.dynamic_update_slice(got, wp[None], (chip, 0, 0, 0))) for got, wp in zip(gathered, mine)]

    def chip_partials(names, buffers, tag):
        gps = [_in_halves(b) for b in buffers]
        return [_rs_add_halves(gp, other, c_arr, "rs_add_halves_" + n)
                for n, gp, other in zip(names, gps, _rs_swap_halves(gps, "rs_swap_halves_" + tag))]

    later_wps, win_wp = _shard_buffers(wts, BF16)
    later_wps, win_wp = [_in_halves(b) for b in later_wps], _in_halves(win_wp)
    first = _first_weights(*own_slot_filled(_ag_weights([win_wp]), [win_wp]))
    first.update({n: wts[n] for n in SMALL})

    loss, grad_x, g, (later_parts, later_recv) = _local_step(
        x[0], mem[0], loss_target[0], first,
        later_weights=(later_wps, lambda gathered: _later_weights(*own_slot_filled(gathered, later_wps))),
        later_partials=lambda g: chip_partials(LATER, _later_grad_buffers(g), "later"))
    loss = lax.psum(loss, ("x", "y", "c"))

    win_parts = chip_partials(("win",), [_win_grad_buffer(g)], "win")
    names = LATER + ("win",)
    mine = [_rs_add_chips(recv, part, chip_arr, "rs_add_chips_" + n)
            for n, recv, part in zip(names, list(later_recv) + list(_rs_exchange(win_parts)),
                                     list(later_parts) + win_parts)]
    totals = [jnp.concatenate([jnp.where(c == 0, m, t), jnp.where(c == 0, t, m)], axis=0)
              for m, t in zip(mine, _rs_share(mine))]
    grads = _shard_grads(*totals)

    small_grads, delta, new_m, new_v = _small_update(_small_gather(_small_rows(g)), wts, mom, var)
    grads.update(small_grads)
    for n in BIG:
        w2 = wts[n].reshape(-1, wts[n].shape[-1])
        d_, m_, v_ = _adamw(w2, grads[n].reshape(w2.shape), mom[n].reshape(w2.shape), var[n].reshape(w2.shape),
                            "adamw_" + n)
        delta[n], new_m[n], new_v[n] = (t.reshape((1,) + wts[n].shape) for t in (d_, m_, v_))
        grads[n] = grads[n].reshape((1,) + wts[n].shape)

    return (loss, grad_x[None], *[grads[n] for n in WEIGHTS], *[delta[n] for n in WEIGHTS],
            *[new_m[n] for n in WEIGHTS], *[new_v[n] for n in WEIGHTS])
```

```python
import functools
import math

import numpy as np
import jax
import jax.numpy as jnp
from jax import lax
from jax.experimental import pallas as pl
from jax.experimental.pallas import tpu as pltpu

F32 = jnp.float32
BF16 = jnp.bfloat16
SDS = jax.ShapeDtypeStruct
MESH = pl.DeviceIdType.MESH

D = 1024
EPS = 1e-6
D_FF = 2816
GLA_GATE_NORM = 16.0
GLA_C = 64
MEM_HEADS = 4
MEM_HD = 256
C_QG, C_KG, C_VG, C_GG, C_QS, C_KS, C_VS, C_LR = 0, 256, 512, 1024, 1536, 2048, 2560, 3072
DIN = 3088
DIN_P = 3200
TB = 512
SBQ = 256
VMEM_LIMIT = 56 * 1024 * 1024
HIGHEST = lax.Precision.HIGHEST

ADAM_LR, ADAM_B1, ADAM_B2, ADAM_EPS, ADAM_WD, ADAM_STEP = 0.001, 0.9, 0.999, 1e-08, 0.01, 10

BIG = ("w_in", "w_gk_up", "w_out", "w_mq", "w_mkv", "w_mo", "w_gate_up", "w_down")
SMALL = ("mix_norm_w", "b_gk", "gla_norm_w", "sb_norm_w", "xattn_norm_w", "mem_norm_w", "mq_norm_w",
         "mk_norm_w", "ffn_norm_w")
WEIGHTS = ("mix_norm_w", "w_in", "w_gk_up", "b_gk", "gla_norm_w", "sb_norm_w", "w_out", "xattn_norm_w",
           "mem_norm_w", "w_mq", "w_mkv", "mq_norm_w", "mk_norm_w", "w_mo", "ffn_norm_w", "w_gate_up", "w_down")
SMALL_ROWS = 16


def _params(**kw):
    return pltpu.CompilerParams(vmem_limit_bytes=VMEM_LIMIT, **kw)


def _row(c, j=0):
    return pl.BlockSpec((TB, c), lambda i, j=j: (i, j))


def _const(shape):
    return pl.BlockSpec(shape, lambda i: (0,) * len(shape))


def _dot(a, b):
    return lax.dot_general(a, b, (((1,), (0,)), ((), ())), preferred_element_type=F32)


def _dot_nt(a, b):
    return lax.dot_general(a, b, (((1,), (1,)), ((), ())), preferred_element_type=F32)


def _dot_tn(a, b):
    return lax.dot_general(a, b, (((0,), (0,)), ((), ())), preferred_element_type=F32)


def _dot_nt_f32(a, b):
    return lax.dot_general(a, b, (((1,), (1,)), ((), ())), precision=HIGHEST, preferred_element_type=F32)


def _split3(x):
    h = x.astype(BF16)
    r = x - h.astype(F32)
    m = r.astype(BF16)
    l = (r - m.astype(F32)).astype(BF16)
    return h, m, l


def _dot_exact(x, ones_mat):
    h, m, l = _split3(x)
    return _dot(h, ones_mat) + _dot(m, ones_mat) + _dot(l, ones_mat)


def _dot_hilo(x, ones_mat):
    h = x.astype(BF16)
    l = (x - h.astype(F32)).astype(BF16)
    return _dot(h, ones_mat) + _dot(l, ones_mat)


def _softplus(z):
    return jnp.maximum(z, 0.0) + jnp.log1p(jnp.exp(-jnp.abs(z)))


def _rsqrt_ms(x):
    return lax.rsqrt(jnp.mean(x * x, axis=-1, keepdims=True) + EPS)


def _colsum8(x):
    r, c = x.shape
    return jnp.sum(x.reshape(r // 8, 8, c), axis=0)


def _matmul(a, b, *, mode, tm, tn, tk=None, res=None, out_dtype=F32, by_column_tile=False, a_spec=None,
            b_spec=None, mnk=None, epilogue=None, exchange=None, name):
    if mnk is not None:
        M, N, K = mnk
    else:
        K, M = a.shape if mode == "tn" else a.shape[::-1]
        N = b.shape[0] if mode == "nt" else b.shape[1]
    tk = K if tk is None else tk
    assert M % tm == 0 and N % tn == 0 and K % tk == 0, (name, M, N, K, tm, tn, tk)
    nk = K // tk
    if a_spec is None:
        if mode == "tn":
            a_spec = pl.BlockSpec((tk, tm), lambda j, i, k: (k, i))
        else:
            a_spec = pl.BlockSpec((tm, tk), lambda j, i, k: (i, k))
    if b_spec is None:
        if mode == "nt":
            b_spec = pl.BlockSpec((tn, tk), lambda j, i, k: (j, k))
        else:
            b_spec = pl.BlockSpec((tk, tn), lambda j, i, k: (k, j))
    if by_column_tile:
        assert res is None
        o_spec = pl.BlockSpec((None, tm, tn), lambda j, i, k: (j, i, 0))
        o_shape = SDS((N // tn, M, tn), out_dtype)
    else:
        o_spec = pl.BlockSpec((tm, tn), lambda j, i, k: (i, j))
        o_shape = SDS((M, N), out_dtype)
    dot = {"nn": _dot, "nt": _dot_nt, "tn": _dot_tn}[mode]
    has_res = res is not None
    extra_in, extra_out, finish = epilogue if epilogue is not None else ((), (), None)
    n_in, n_out = 2 + has_res + len(extra_in), 1 + len(extra_out)
    sent = list(exchange) if exchange is not None else []
    ns = len(sent)
    grid = (N // tn, M // tm, nk)

    def body(*refs):
        a_ref, b_ref = refs[0], refs[1]
        res_ref = refs[2] if has_res else None
        in_refs, out_refs = refs[2 + has_res:n_in], refs[n_in + ns:n_in + ns + n_out]
        first_row_tile = pl.program_id(1) == 0
        scratch = refs[n_in + 2 * ns + n_out:]
        if ns:
            ids = [pl.program_id(d) for d in range(3)]
            rs_start, rs_finish = _rs_exchange_phases(
                refs[n_in:n_in + ns], refs[n_in + ns + n_out:n_in + 2 * ns + n_out], *scratch[nk > 1:])
            pl.when((ids[0] == 0) & (ids[1] == 0) & (ids[2] == 0))(rs_start)

        def done(t):
            if has_res:
                t = t + res_ref[...]
            if finish is None:
                out_refs[0][...] = t.astype(out_dtype)
            else:
                finish(t, first_row_tile, in_refs, out_refs)

        p = dot(a_ref[...].astype(BF16), b_ref[...].astype(BF16))
        if nk == 1:
            done(p)
        else:
            acc_ref = scratch[0]
            k = pl.program_id(2)

            @pl.when(k == 0)
            def _():
                acc_ref[...] = p

            @pl.when(k > 0)
            def _():
                acc_ref[...] += p

            @pl.when(k == nk - 1)
            def _():
                done(acc_ref[...])

        if ns:
            pl.when((ids[0] == grid[0] - 1) & (ids[1] == grid[1] - 1) & (ids[2] == grid[2] - 1))(rs_finish)

    in_specs = [a_spec, b_spec] + ([o_spec] if has_res else []) + [s for _, s in extra_in] + [HBM] * ns
    args = (a, b) + ((res,) if has_res else ()) + tuple(x for x, _ in extra_in) + tuple(sent)
    outs = pl.pallas_call(
        body, name=name, grid=grid, in_specs=in_specs,
        out_specs=[o_spec] + [s for _, s in extra_out] + [HBM] * ns,
        out_shape=[o_shape] + [s for s, _ in extra_out] + [SDS(p.shape, p.dtype) for p in sent],
        scratch_shapes=([pltpu.VMEM((tm, tn), F32)] if nk > 1 else []) + (_rs_exchange_sems(ns) if ns else []),
        compiler_params=_params(
            dimension_semantics=("arbitrary",) * 3 if ns else ("parallel", "parallel", "arbitrary")),
    )(*args)
    return outs[0] if epilogue is None and not ns else outs


def _full_row(tm, c):
    return pl.BlockSpec((tm, c), lambda j, i, k: (i, 0))


def _kept(shape):
    return pl.BlockSpec(shape, lambda j, i, k: (0,) * len(shape))


def _then_norm_fwd(w, T, tm):
    dm = w.shape[1]

    def finish(t, first, ins, outs):
        outs[0][...] = t
        outs[1][...] = (t * _rsqrt_ms(t) * ins[0][...]).astype(BF16)

    return [(w, _kept((1, dm)))], [(SDS((T, dm), BF16), _full_row(tm, dm))], finish


def _then_norm_bwd(x, w, dres, tm):
    T, dm = x.shape

    def finish(t, first, ins, outs):
        x_ref, w_ref, dres_ref = ins

        @pl.when(first)
        def _():
            outs[1][...] = jnp.zeros_like(outs[1])

        xv = x_ref[...]
        r = _rsqrt_ms(xv)
        n = xv * r
        dn = t * w_ref[...]
        outs[0][...] = dres_ref[...] + r * (dn - n * jnp.mean(dn * n, axis=-1, keepdims=True))
        outs[1][...] += _colsum8(t * n)

    return ([(x, _full_row(tm, dm)), (w, _kept((1, dm))), (dres, _full_row(tm, dm))],
            [(SDS((8, dm), F32), _kept((8, dm)))], finish)


def _then_loss(tgt, tm):
    def finish(t, first, ins, outs):
        @pl.when(first)
        def _():
            outs[1][...] = jnp.zeros_like(outs[1])

        e = t - ins[0][...]
        outs[0][...] = e * (1.0 / D)
        outs[1][...] += _colsum8(e * e) * (0.5 / D)

    return [(tgt, _full_row(tm, D))], [(SDS((8, D), F32), _kept((8, D)))], finish


def _norm_fwd(x, w, name):
    T, dm = x.shape

    def body(x_ref, w_ref, h_ref):
        xv = x_ref[...]
        h_ref[...] = (xv * _rsqrt_ms(xv) * w_ref[...]).astype(BF16)

    return pl.pallas_call(
        body, name=name, grid=(T // TB,), in_specs=[_row(dm), _const((1, dm))], out_specs=_row(dm),
        out_shape=SDS((T, dm), BF16), compiler_params=_params())(x, w)


def _then_proj_split(wgk, bgk, T, tm):
    def finish(t, first, ins, outs):
        wgk_ref, b_ref = ins
        proj_ref, gk_ref, qs_ref, ks_ref, vs_ref = outs
        proj_ref[...] = t
        u = _dot(t[:, C_LR:DIN_P].astype(BF16), wgk_ref[...].astype(BF16)) + b_ref[...]
        gk_ref[...] = -_softplus(-u) / GLA_GATE_NORM
        qs_ref[...] = (t[:, C_QS:C_KS] * 0.125).astype(BF16)
        ks_ref[...] = t[:, C_KS:C_VS].astype(BF16)
        vs_ref[...] = t[:, C_VS:C_LR].astype(BF16)

    return ([(wgk, _kept((128, 256))), (bgk, _kept((1, 256)))],
            [(SDS((T, 256), F32), _full_row(tm, 256))] + [(SDS((T, 512), BF16), _full_row(tm, 512))] * 3, finish)


def _dproj_assemble(proj, dq_g, dk_g, dv_g, dg_g, dq_s, dk_s, dv_s, dgk, wgk, bgk):
    T = proj.shape[0]

    def body(lr_ref, dqg_ref, dkg_ref, dvg_ref, dgg_ref, dqs_ref, dks_ref, dvs_ref, dgk_ref, wgk_ref, b_ref,
             dp_ref, dwgk_ref, dbgk_ref):
        @pl.when(pl.program_id(0) == 0)
        def _():
            dwgk_ref[...] = jnp.zeros_like(dwgk_ref)
            dbgk_ref[...] = jnp.zeros_like(dbgk_ref)

        lr = lr_ref[...].astype(BF16)
        wg = wgk_ref[...].astype(BF16)
        u = _dot(lr, wg) + b_ref[...]
        du = dgk_ref[...] * (jax.nn.sigmoid(-u) / GLA_GATE_NORM)
        dub = du.astype(BF16)
        dp_ref[:, C_QG:C_KG] = (dqg_ref[...] * 0.125).astype(BF16)
        dp_ref[:, C_KG:C_VG] = dkg_ref[...].astype(BF16)
        dp_ref[:, C_VG:C_GG] = dvg_ref[...].astype(BF16)
        dp_ref[:, C_GG:C_QS] = dgg_ref[...].astype(BF16)
        dp_ref[:, C_QS:C_KS] = (dqs_ref[...] * 0.125).astype(BF16)
        dp_ref[:, C_KS:C_VS] = dks_ref[...].astype(BF16)
        dp_ref[:, C_VS:C_LR] = dvs_ref[...].astype(BF16)
        dp_ref[:, C_LR:DIN_P] = _dot_nt(dub, wg).astype(BF16)
        dwgk_ref[...] += _dot_tn(lr, dub)
        dbgk_ref[...] += _colsum8(du)

    return pl.pallas_call(
        body, name="dproj_assemble", grid=(T // TB,),
        in_specs=[_row(128, C_LR // 128), _row(256), _row(256), _row(512), _row(512), _row(512), _row(512),
                  _row(512), _row(256), _const((128, 256)), _const((1, 256))],
        out_specs=[_row(DIN_P), _const((128, 256)), _const((8, 256))],
        out_shape=[SDS((T, DIN_P), BF16), SDS((128, 256), F32), SDS((8, 256), F32)],
        compiler_params=_params())(proj, dq_g, dk_g, dv_g, dg_g, dq_s, dk_s, dv_s, dgk, wgk, bgk)


def _group_ones(n, g):
    idx = np.arange(n) // g
    return jnp.asarray((idx[:, None] == idx[None, :]).astype(np.float32), dtype=BF16)


def _mix_cat(o_g, proj, o_s, wg512, ws512, bd64):
    T = o_g.shape[0]

    def body(og_ref, gg_ref, os_ref, wg_ref, ws_ref, bd_ref, cat_ref):
        og = og_ref[...]
        gg = gg_ref[...]
        s = gg * jax.nn.sigmoid(gg)
        for h in range(4):
            sl = slice(128 * h, 128 * (h + 1))
            x = og[:, sl]
            cat_ref[:, sl] = (x * _rsqrt_ms(x) * wg_ref[:, sl] * s[:, sl]).astype(BF16)
        osv = os_ref[...]
        ms = _dot_exact(osv * osv, bd_ref[...]) * (1.0 / 64.0)
        cat_ref[:, 512:1024] = (osv * lax.rsqrt(ms + EPS) * ws_ref[...]).astype(BF16)

    return pl.pallas_call(
        body, name="mix_cat", grid=(T // TB,),
        in_specs=[_row(512), _row(512, C_GG // 512), _row(512), _const((1, 512)), _const((1, 512)),
                  _const((512, 512))],
        out_specs=_row(1024), out_shape=SDS((T, 1024), BF16), compiler_params=_params())(
            o_g, proj, o_s, wg512, ws512, bd64)


def _mix_cat_bwd(dcat, o_g, proj, o_s, wg512, ws512, bd64):
    T = o_g.shape[0]

    def body(dc_ref, og_ref, gg_ref, os_ref, wg_ref, ws_ref, bd_ref, dog_ref, dgg_ref, dos_ref, dwg_ref, dws_ref):
        @pl.when(pl.program_id(0) == 0)
        def _():
            dwg_ref[...] = jnp.zeros_like(dwg_ref)
            dws_ref[...] = jnp.zeros_like(dws_ref)

        og = og_ref[...]
        gg = gg_ref[...]
        sg = jax.nn.sigmoid(gg)
        s = gg * sg
        ds = sg * (1.0 + gg * (1.0 - sg))
        for h in range(4):
            sl = slice(128 * h, 128 * (h + 1))
            x = og[:, sl]
            r = _rsqrt_ms(x)
            n = x * r
            w = wg_ref[:, sl]
            dc = dc_ref[:, sl]
            dy = dc * s[:, sl]
            dgg_ref[:, sl] = dc * (n * w) * ds[:, sl]
            dn = dy * w
            dog_ref[:, sl] = r * (dn - n * jnp.mean(dn * n, axis=-1, keepdims=True))
            dwg_ref[:, sl] += _colsum8(dy * n)
        osv = os_ref[...]
        bd = bd_ref[...]
        r = lax.rsqrt(_dot_exact(osv * osv, bd) * (1.0 / 64.0) + EPS)
        n = osv * r
        dc = dc_ref[:, 512:1024]
        dn = dc * ws_ref[...]
        dos_ref[...] = r * (dn - n * (_dot_exact(dn * n, bd) * (1.0 / 64.0)))
        dws_ref[...] += _colsum8(dc * n)

    return pl.pallas_call(
        body, name="mix_cat_bwd", grid=(T // TB,),
        in_specs=[_row(1024), _row(512), _row(512, C_GG // 512), _row(512), _const((1, 512)), _const((1, 512)),
                  _const((512, 512))],
        out_specs=[_row(512), _row(512), _row(512), _const((8, 512)), _const((8, 512))],
        out_shape=[SDS((T, 512), F32), SDS((T, 512), F32), SDS((T, 512), F32), SDS((8, 512), F32),
                   SDS((8, 512), F32)],
        compiler_params=_params())(dcat, o_g, proj, o_s, wg512, ws512, bd64)


FF_TN = 1408
DW_TK = 2048


def _gate_up_act(h, w):
    T = h.shape[0]
    nj = D_FF // FF_TN

    def body(h_ref, wg_ref, wu_ref, gu_ref, a_ref):
        hv = h_ref[...]
        g = _dot(hv, wg_ref[...])
        u = _dot(hv, wu_ref[...])
        gu_ref[0] = g.astype(BF16)
        gu_ref[1] = u.astype(BF16)
        a_ref[...] = (g * jax.nn.sigmoid(g) * u).astype(BF16)

    return pl.pallas_call(
        body, name="mm_gate_up_act", grid=(nj, T // TB),
        in_specs=[pl.BlockSpec((TB, D), lambda j, i: (i, 0)),
                  pl.BlockSpec((None, D, FF_TN), lambda j, i: (j, 0, 0)),
                  pl.BlockSpec((None, D, FF_TN), lambda j, i: (nj + j, 0, 0))],
        out_specs=[pl.BlockSpec((2, TB, FF_TN), lambda j, i: (0, i, j)),
                   pl.BlockSpec((TB, FF_TN), lambda j, i: (i, j))],
        out_shape=[SDS((2, T, D_FF), BF16), SDS((T, D_FF), BF16)],
        compiler_params=_params(dimension_semantics=("parallel", "parallel")))(h, w, w)


def _down_bwd(dy, w_down, gu):
    T = dy.shape[0]
    nj = D_FF // FF_TN

    def body(dy_ref, w_ref, gu_ref, dgu_ref):
        da = _dot_nt(dy_ref[...].astype(BF16), w_ref[...])
        g = gu_ref[0].astype(F32)
        sg = jax.nn.sigmoid(g)
        dgu_ref[0] = (da * gu_ref[1].astype(F32) * (sg * (1.0 + g * (1.0 - sg)))).astype(BF16)
        dgu_ref[1] = (da * (g * sg)).astype(BF16)

    return pl.pallas_call(
        body, name="mm_down_bwd", grid=(nj, T // TB),
        in_specs=[pl.BlockSpec((TB, D), lambda j, i: (i, 0)),
                  pl.BlockSpec((FF_TN, D), lambda j, i: (j, 0)),
                  pl.BlockSpec((2, TB, FF_TN), lambda j, i: (0, i, j))],
        out_specs=pl.BlockSpec((2, TB, FF_TN), lambda j, i: (0, i, j)),
        out_shape=SDS((2, T, D_FF), BF16),
        compiler_params=_params(dimension_semantics=("parallel", "parallel")))(dy, w_down, gu)


def _gla_consts():
    c = GLA_C
    L = np.tril(np.ones((c, c), np.float32))
    blocks = [L, L[(np.arange(c) // 16) * 16]]
    blocks += [np.repeat(L[16 * i:16 * i + 1], c, axis=0) for i in range(4)]
    blocks.append(np.repeat(L[c - 1:c], c, axis=0))
    return jnp.asarray(np.concatenate(blocks, axis=0))


@jax.custom_vjp
def _gla_lin(cm, g):
    cb = cm.astype(BF16)
    h, m, l = _split3(g)
    y = _dot(cb, h) + _dot(cb, m) + _dot(cb, l)
    return tuple(y[GLA_C * n:GLA_C * (n + 1)] for n in range(7))


def _gla_lin_fwd(cm, g):
    return _gla_lin(cm, g), cm


def _gla_lin_bwd(cm, cts):
    cb = cm.astype(BF16)
    h, m, l = _split3(jnp.concatenate(cts, axis=0))
    return None, _dot_tn(cb, h) + _dot_tn(cb, m) + _dot_tn(cb, l)


_gla_lin.defvjp(_gla_lin_fwd, _gla_lin_bwd)


GLA_SUB = 16
GLA_H = 4
GLA_W = 64 * GLA_H


def _head_lanes(rows, h):
    lane = lax.broadcasted_iota(jnp.int32, (rows, GLA_W), 1)
    return (lane >= 64 * h) & (lane < 64 * (h + 1))


def _gla_scores_terms(qs, k, b, rs, blk):
    row = lax.broadcasted_iota(jnp.int32, (GLA_C, GLA_W), 0)
    keep = row < GLA_SUB * (blk + 1)
    e = jnp.where(keep, jnp.exp(jnp.where(keep, rs[blk] - b, 0.0)), 0.0)
    qb = qs[GLA_SUB * blk:GLA_SUB * (blk + 1)]
    lhs = jnp.concatenate([jnp.where(_head_lanes(GLA_SUB, h), qb, 0.0) for h in range(GLA_H)], axis=0)
    return lhs, e


@jax.custom_vjp
def _gla_scores(qs, k, b, r0, r1, r2, r3):
    rs = (r0, r1, r2, r3)
    per_head = [[] for _ in range(GLA_H)]
    for blk in range(GLA_C // GLA_SUB):
        lhs, e = _gla_scores_terms(qs, k, b, rs, blk)
        a = _dot_nt_f32(lhs, k * e)
        for h in range(GLA_H):
            per_head[h].append(a[GLA_SUB * h:GLA_SUB * (h + 1)])
    return tuple(jnp.concatenate(p, axis=0) for p in per_head)


def _gla_scores_fwd(qs, k, b, r0, r1, r2, r3):
    return _gla_scores(qs, k, b, r0, r1, r2, r3), (qs, k, b, r0, r1, r2, r3)


def _gla_scores_bwd(saved, cts):
    qs, k, b = saved[:3]
    rs = saved[3:]
    dqs, drs = [], []
    dk = jnp.zeros_like(k)
    db = jnp.zeros_like(b)
    for blk in range(GLA_C // GLA_SUB):
        lhs, e = _gla_scores_terms(qs, k, b, rs, blk)
        rows = slice(GLA_SUB * blk, GLA_SUB * (blk + 1))
        da = jnp.concatenate([ct[rows] for ct in cts], axis=0)
        dlhs = lax.dot_general(da, k * e, (((1,), (0,)), ((), ())), precision=HIGHEST, preferred_element_type=F32)
        dq = jnp.zeros((GLA_SUB, GLA_W), F32)
        for h in range(GLA_H):
            dq = jnp.where(_head_lanes(GLA_SUB, h), dlhs[GLA_SUB * h:GLA_SUB * (h + 1)], dq)
        dqs.append(dq)
        dks = lax.dot_general(da, lhs, (((0,), (0,)), ((), ())), precision=HIGHEST, preferred_element_type=F32)
        dk = dk + dks * e
        darg = dks * (k * e)
        db = db - darg
        drs.append(darg)
    return (jnp.concatenate(dqs, axis=0), dk, db, *drs)


_gla_scores.defvjp(_gla_scores_fwd, _gla_scores_bwd)


def _gla_chunk(cm, q, k, g, v0, v1, v2, v3, st):
    c = GLA_C
    vs = (v0, v1, v2, v3)
    ri = lax.broadcasted_iota(jnp.int32, (c, c), 0)
    ci = lax.broadcasted_iota(jnp.int32, (c, c), 1)
    causal = ci <= ri
    b, r, r0, r1, r2, r3, bl = _gla_lin(cm, g)
    scores = _gla_scores(q * jnp.exp(b - r), k, b, r0, r1, r2, r3)
    qe = q * jnp.exp(b)
    kd = k * jnp.exp(bl - b)
    st_new = st * jnp.exp(jnp.concatenate([bl, bl], axis=0))
    outs = []
    for h in range(GLA_H):
        a = jnp.where(causal, scores[h], 0.0)
        outs.append(_dot_nt(jnp.where(_head_lanes(c, h), qe, 0.0), st) + _dot(a, vs[h]))
        st_new = st_new + jnp.where(_head_lanes(2 * c, h), _dot_tn(vs[h], kd), 0.0)
    return (*outs, st_new)


GLA_TB = 512


def _gla_fwd(proj, gk, cm):
    T = proj.shape[0]
    nb = T // GLA_TB
    nc = GLA_TB // GLA_C

    def body(q_ref, k_ref, v_ref, g_ref, cm_ref, o_ref, st_ref, st_scr):
        @pl.when(pl.program_id(0) == 0)
        def _():
            st_scr[...] = jnp.zeros_like(st_scr)

        cmv = cm_ref[...]

        def chunk(ci, carry):
            rs = pl.ds(pl.multiple_of(ci * GLA_C, GLA_C), GLA_C)
            st = st_scr[...]
            st_ref[ci] = st
            *outs, st_new = _gla_chunk(cmv, q_ref[rs, :] * 0.125, k_ref[rs, :], g_ref[rs, :],
                                       *[v_ref[rs, 128 * h:128 * (h + 1)] for h in range(GLA_H)], st)
            for h in range(GLA_H):
                o_ref[rs, 128 * h:128 * (h + 1)] = outs[h]
            st_scr[...] = st_new
            return carry

        lax.fori_loop(0, nc, chunk, 0, unroll=True)

    return pl.pallas_call(
        body, name="gla_fwd", grid=(nb,),
        in_specs=[pl.BlockSpec((GLA_TB, 256), lambda i: (i, C_QG // 256)),
                  pl.BlockSpec((GLA_TB, 256), lambda i: (i, C_KG // 256)),
                  pl.BlockSpec((GLA_TB, 512), lambda i: (i, C_VG // 512)),
                  pl.BlockSpec((GLA_TB, 256), lambda i: (i, 0)),
                  pl.BlockSpec((7 * GLA_C, GLA_C), lambda i: (0, 0))],
        out_specs=[pl.BlockSpec((GLA_TB, 512), lambda i: (i, 0)),
                   pl.BlockSpec((nc, 128, GLA_W), lambda i: (i, 0, 0))],
        out_shape=[SDS((T, 512), F32), SDS((T // GLA_C, 128, GLA_W), F32)],
        scratch_shapes=[pltpu.VMEM((128, GLA_W), F32)],
        compiler_params=_params(dimension_semantics=("arbitrary",)))(proj, proj, proj, gk, cm)


def _gla_bwd(proj, gk, cm, states, do, parts=()):
    T = proj.shape[0]
    nb = T // GLA_TB
    nc = GLA_TB // GLA_C
    n = len(parts)

    def body(q_ref, k_ref, v_ref, g_ref, cm_ref, st_ref, do_ref, *rest):
        p_refs, (dq_ref, dk_ref, dv_ref, dg_ref), out_refs = rest[:n], rest[n:n + 4], rest[n + 4:2 * n + 4]
        dst_scr = rest[2 * n + 4]
        step = pl.program_id(0)
        if n:
            rs_start, rs_finish = _rs_exchange_phases(p_refs, out_refs, *rest[2 * n + 5:])

        @pl.when(step == 0)
        def _():
            dst_scr[...] = jnp.zeros_like(dst_scr)
            if n:
                rs_start()

        cmv = cm_ref[...]

        def chunk(t, carry):
            ci = nc - 1 - t
            rs = pl.ds(pl.multiple_of(ci * GLA_C, GLA_C), GLA_C)
            _, vjp = jax.vjp(
                functools.partial(_gla_chunk, cmv), q_ref[rs, :] * 0.125, k_ref[rs, :], g_ref[rs, :],
                *[v_ref[rs, 128 * h:128 * (h + 1)] for h in range(GLA_H)], st_ref[ci])
            dq, dk, dg, *dvs, dst = vjp((*[do_ref[rs, 128 * h:128 * (h + 1)] for h in range(GLA_H)], dst_scr[...]))
            dq_ref[rs, :] = dq
            dk_ref[rs, :] = dk
            dg_ref[rs, :] = dg
            for h in range(GLA_H):
                dv_ref[rs, 128 * h:128 * (h + 1)] = dvs[h]
            dst_scr[...] = dst
            return carry

        lax.fori_loop(0, nc, chunk, 0, unroll=True)

        if n:
            pl.when(step == nb - 1)(rs_finish)

    rev = lambda i: nb - 1 - i
    return pl.pallas_call(
        body, name="gla_bwd", grid=(nb,),
        in_specs=[pl.BlockSpec((GLA_TB, 256), lambda i: (rev(i), C_QG // 256)),
                  pl.BlockSpec((GLA_TB, 256), lambda i: (rev(i), C_KG // 256)),
                  pl.BlockSpec((GLA_TB, 512), lambda i: (rev(i), C_VG // 512)),
                  pl.BlockSpec((GLA_TB, 256), lambda i: (rev(i), 0)),
                  pl.BlockSpec((7 * GLA_C, GLA_C), lambda i: (0, 0)),
                  pl.BlockSpec((nc, 128, GLA_W), lambda i: (rev(i), 0, 0)),
                  pl.BlockSpec((GLA_TB, 512), lambda i: (rev(i), 0))] + [HBM] * n,
        out_specs=[pl.BlockSpec((GLA_TB, 256), lambda i: (rev(i), 0)),
                   pl.BlockSpec((GLA_TB, 256), lambda i: (rev(i), 0)),
                   pl.BlockSpec((GLA_TB, 512), lambda i: (rev(i), 0)),
                   pl.BlockSpec((GLA_TB, 256), lambda i: (rev(i), 0))] + [HBM] * n,
        out_shape=[SDS((T, 256), F32), SDS((T, 256), F32), SDS((T, 512), F32), SDS((T, 256), F32)]
        + [SDS(p.shape, p.dtype) for p in parts],
        scratch_shapes=[pltpu.VMEM((128, GLA_W), F32)] + (_rs_exchange_sems(n) if n else []),
        compiler_params=_params(dimension_semantics=("arbitrary",)))(proj, proj, proj, gk, cm, states, do, *parts)


SB_DEAD = 105.0
SB_COUNT_LANE = 127


def _sb_tri():
    i = np.arange(SBQ)
    return jnp.asarray((i[:, None] > i[None, :]).astype(np.float32), dtype=BF16)


def _sb_block_fwd(qh, kb, tri, carry, strict):
    z = _dot_nt(qh, kb)
    sp = _softplus(z)
    l1 = -sp
    if strict is not None:
        l1 = jnp.where(strict, l1, 0.0)
    log_a = (z - sp) + _dot_hilo(l1, tri) + carry
    a = jnp.exp(log_a)
    if strict is not None:
        a = jnp.where(strict, a, 0.0)
    return z - sp, l1, a


def _sb_fwd(qs, ks, vs, tri, wps=()):
    T = qs.shape[0]
    nq = T // SBQ
    n = len(wps)

    def body(q_ref, k_ref, v_ref, tri_ref, *rest):
        w_refs, (o_ref, c_ref), out_refs = rest[:n], rest[n:n + 2], rest[n + 2:2 * n + 2]
        pair = pl.program_id(0)
        i = pl.program_id(1)
        if n:
            ag_start, ag_pass_on, ag_finish = _ag_phases(w_refs, out_refs, *rest[2 * n + 2:])
            pl.when((pair == 0) & (i == 0))(ag_start)
        lane = lax.broadcasted_iota(jnp.int32, (1, 128), 1)
        clane = lax.broadcasted_iota(jnp.int32, (SBQ, 128), 1)
        strict = (lax.broadcasted_iota(jnp.int32, (SBQ, SBQ), 1) < lax.broadcasted_iota(jnp.int32, (SBQ, SBQ), 0))
        tri_v = tri_ref[...]
        qv = q_ref[...]
        first_head = lane < 64
        qhs = (jnp.where(first_head, qv, jnp.zeros_like(qv)), jnp.where(first_head, jnp.zeros_like(qv), qv))

        def block(j, carries, accs, masked):
            rs = pl.ds(pl.multiple_of(j * SBQ, SBQ), SBQ)
            kb = k_ref[rs, :]
            vb = v_ref[rs, :]
            out_c, out_a = [], []
            for hh in range(2):
                _, l1, a = _sb_block_fwd(qhs[hh], kb, tri_v, carries[hh], strict if masked else None)
                out_a.append(accs[hh] + _dot(a.astype(BF16), vb))
                out_c.append(carries[hh] + jnp.sum(l1, axis=1, keepdims=True))
            return out_c, out_a

        zero1 = jnp.zeros((SBQ, 1), F32)
        zero128 = jnp.zeros((SBQ, 128), F32)
        (c0, c1), (a0, a1) = block(i, (zero1, zero1), (zero128, zero128), True)

        def more(state):
            return (state[0] <= i) & (jnp.maximum(jnp.max(state[1]), jnp.max(state[2])) > -SB_DEAD)

        def step(state):
            jj, c0, c1, a0, a1, t0, t1 = state
            j = i - jj
            t0 = jnp.where(clane == j, c0, t0)
            t1 = jnp.where(clane == j, c1, t1)
            (c0, c1), (a0, a1) = block(j, (c0, c1), (a0, a1), False)
            return jj + 1, c0, c1, a0, a1, t0, t1

        jj, c0, c1, a0, a1, t0, t1 = lax.while_loop(
            more, step, (jnp.int32(1), c0, c1, a0, a1, zero128, zero128))
        o_ref[...] = jnp.where(first_head, a0, a1)
        swept = (jj - 1).astype(F32)
        c_ref[0, :, 0:128] = jnp.where(clane == SB_COUNT_LANE, swept, t0)
        c_ref[0, :, 128:256] = jnp.where(clane == SB_COUNT_LANE, swept, t1)
        if n:
            pl.when((pair == 3) & (i == 0))(ag_pass_on)
            pl.when((pair == 3) & (i == nq - 1))(ag_finish)

    return pl.pallas_call(
        body, name="sb_fwd", grid=(4, nq),
        in_specs=[pl.BlockSpec((SBQ, 128), lambda h, i: (i, h)),
                  pl.BlockSpec((T, 128), lambda h, i: (0, h)),
                  pl.BlockSpec((T, 128), lambda h, i: (0, h)),
                  pl.BlockSpec((SBQ, SBQ), lambda h, i: (0, 0))] + [HBM] * n,
        out_specs=[pl.BlockSpec((SBQ, 128), lambda h, i: (i, h)),
                   pl.BlockSpec((1, SBQ, 256), lambda h, i: (h, i, 0))] + [HBM] * n,
        out_shape=[SDS((T, 512), F32), SDS((4, T, 256), F32)] + [SDS((4,) + wp.shape, wp.dtype) for wp in wps],
        scratch_shapes=_ag_sems(n) if n else [],
        compiler_params=_params(dimension_semantics=("arbitrary", "arbitrary")))(qs, ks, vs, tri, *wps)


def _sb_bwd(qs, ks, vs, do, carries, tri, tri_t):
    T = qs.shape[0]
    nq = T // SBQ

    def body(q_ref, k_ref, v_ref, do_ref, c_ref, tri_ref, trit_ref, dq_ref, dk_ref, dv_ref):
        i = pl.program_id(1)

        @pl.when(i == 0)
        def _():
            dk_ref[...] = jnp.zeros_like(dk_ref)
            dv_ref[...] = jnp.zeros_like(dv_ref)

        lane = lax.broadcasted_iota(jnp.int32, (1, 128), 1)
        clane = lax.broadcasted_iota(jnp.int32, (SBQ, 128), 1)
        strict = (lax.broadcasted_iota(jnp.int32, (SBQ, SBQ), 1) < lax.broadcasted_iota(jnp.int32, (SBQ, SBQ), 0))
        tri_v = tri_ref[...]
        trit_v = trit_ref[...]
        qv = q_ref[...]
        dov = do_ref[...].astype(BF16)
        first_head = lane < 64
        qhs = (jnp.where(first_head, qv, jnp.zeros_like(qv)), jnp.where(first_head, jnp.zeros_like(qv), qv))
        dohs = (jnp.where(first_head, dov, jnp.zeros_like(dov)), jnp.where(first_head, jnp.zeros_like(dov), dov))
        cts = (c_ref[0, :, 0:128], c_ref[0, :, 128:256])

        def block(j, pcarries, dqs, masked):
            rs = pl.ds(pl.multiple_of(j * SBQ, SBQ), SBQ)
            kb = k_ref[rs, :]
            vb = v_ref[rs, :]
            out_p, out_q = [], []
            dk = jnp.zeros((SBQ, 128), F32)
            dv = jnp.zeros((SBQ, 128), F32)
            for hh in range(2):
                carry = jnp.sum(jnp.where(clane == j, cts[hh], 0.0), axis=1, keepdims=True)
                lb, _, a = _sb_block_fwd(qhs[hh], kb, tri_v, carry, strict if masked else None)
                g = a * _dot_nt(dohs[hh], vb)
                p = _dot(g.astype(BF16), trit_v) + pcarries[hh]
                dz = g - (g + p) * jnp.exp(lb)
                if masked:
                    dz = jnp.where(strict, dz, 0.0)
                dzb = dz.astype(BF16)
                dk = dk + _dot_tn(dzb, qhs[hh])
                dv = dv + _dot_tn(a.astype(BF16), dohs[hh])
                out_p.append(pcarries[hh] + jnp.sum(g, axis=1, keepdims=True))
                out_q.append(dqs[hh] + _dot(dzb, kb))
            dk_ref[rs, :] += dk
            dv_ref[rs, :] += dv
            return out_p, out_q

        def step(j, state):
            (p0, p1), (q0, q1) = block(j, (state[0], state[1]), (state[2], state[3]), False)
            return p0, p1, q0, q1

        swept = jnp.max(jnp.where(clane == SB_COUNT_LANE, cts[0], 0.0)).astype(jnp.int32)
        first = i - jnp.clip(swept, 0, i)
        zero1 = jnp.zeros((SBQ, 1), F32)
        zero128 = jnp.zeros((SBQ, 128), F32)
        p0, p1, q0, q1 = lax.fori_loop(first, i, step, (zero1, zero1, zero128, zero128))
        _, (q0, q1) = block(i, (p0, p1), (q0, q1), True)
        dq_ref[...] = jnp.where(first_head, q0, q1)

    return pl.pallas_call(
        body, name="sb_bwd", grid=(4, nq),
        in_specs=[pl.BlockSpec((SBQ, 128), lambda h, i: (i, h)),
                  pl.BlockSpec((T, 128), lambda h, i: (0, h)),
                  pl.BlockSpec((T, 128), lambda h, i: (0, h)),
                  pl.BlockSpec((SBQ, 128), lambda h, i: (i, h)),
                  pl.BlockSpec((1, SBQ, 256), lambda h, i: (h, i, 0)),
                  pl.BlockSpec((SBQ, SBQ), lambda h, i: (0, 0)),
                  pl.BlockSpec((SBQ, SBQ), lambda h, i: (0, 0))],
        out_specs=[pl.BlockSpec((SBQ, 128), lambda h, i: (i, h)),
                   pl.BlockSpec((T, 128), lambda h, i: (0, h)),
                   pl.BlockSpec((T, 128), lambda h, i: (0, h))],
        out_shape=[SDS((T, 512), F32), SDS((T, 512), F32), SDS((T, 512), F32)],
        compiler_params=_params(dimension_semantics=("parallel", "arbitrary")))(qs, ks, vs, do, carries, tri, tri_t)


def _mem_fwd(mem, mem_norm_w, w_mkv, mk_norm_w):
    M = mem.shape[0]

    def body(mem_ref, wn_ref, w_ref, wk_ref, mn_ref, kraw_ref, k_ref, v_ref):
        mv = mem_ref[...]
        mn = (mv * _rsqrt_ms(mv) * wn_ref[...]).astype(BF16)
        mn_ref[...] = mn
        for s in range(2):
            cols = slice(512 * s, 512 * (s + 1))
            ks = _dot(mn, w_ref[s, :D, :])
            kraw_ref[:, cols] = ks
            v_ref[:, cols] = _dot(mn, w_ref[2 + s, :D, :]).astype(BF16)
            for h in range(2):
                x = ks[:, MEM_HD * h:MEM_HD * (h + 1)]
                k_ref[:, 512 * s + MEM_HD * h:512 * s + MEM_HD * (h + 1)] = (
                    x * _rsqrt_ms(x) * wk_ref[...]).astype(BF16)

    vm = pl.BlockSpec(memory_space=pltpu.VMEM)
    return pl.pallas_call(
        body, name="mem_fwd", in_specs=[vm] * 4, out_specs=[vm] * 4,
        out_shape=[SDS((M, D), BF16), SDS((M, D), F32), SDS((M, D), BF16), SDS((M, D), BF16)],
        compiler_params=_params())(mem, mem_norm_w, w_mkv, mk_norm_w)


def _xattn_fwd(qraw, k, v, wq):
    T = qraw.shape[0]
    M = k.shape[0]

    def body(q_ref, k_ref, v_ref, wq_ref, o_ref):
        for h in range(MEM_HEADS):
            sl = slice(MEM_HD * h, MEM_HD * (h + 1))
            x = q_ref[:, sl]
            q = (x * _rsqrt_ms(x) * wq_ref[...]).astype(BF16)
            s = _dot_nt(q, k_ref[:, sl]) * (1.0 / math.sqrt(MEM_HD))
            s = s - jnp.max(s, axis=-1, keepdims=True)
            e = jnp.exp(s)
            p = e / jnp.sum(e, axis=-1, keepdims=True)
            o_ref[:, sl] = _dot(p.astype(BF16), v_ref[:, sl]).astype(BF16)

    return pl.pallas_call(
        body, name="xattn_fwd", grid=(T // TB,),
        in_specs=[_row(D), _const((M, D)), _const((M, D)), _const((1, MEM_HD))],
        out_specs=_row(D), out_shape=SDS((T, D), BF16), compiler_params=_params())(qraw, k, v, wq)


def _xattn_bwd(qraw, k, v, wq, do):
    T = qraw.shape[0]
    M = k.shape[0]

    def body(q_ref, k_ref, v_ref, wq_ref, do_ref, dq_ref, dk_ref, dv_ref, dw_ref):
        @pl.when(pl.program_id(0) == 0)
        def _():
            dk_ref[...] = jnp.zeros_like(dk_ref)
            dv_ref[...] = jnp.zeros_like(dv_ref)
            dw_ref[...] = jnp.zeros_like(dw_ref)

        w = wq_ref[...]
        for h in range(MEM_HEADS):
            sl = slice(MEM_HD * h, MEM_HD * (h + 1))
            x = q_ref[:, sl]
            r = _rsqrt_ms(x)
            n = x * r
            q = (n * w).astype(BF16)
            kb = k_ref[:, sl]
            s = _dot_nt(q, kb) * (1.0 / math.sqrt(MEM_HD))
            s = s - jnp.max(s, axis=-1, keepdims=True)
            e = jnp.exp(s)
            p = e / jnp.sum(e, axis=-1, keepdims=True)
            dob = do_ref[:, sl].astype(BF16)
            dp = _dot_nt(dob, v_ref[:, sl])
            ds = (p * (dp - jnp.sum(dp * p, axis=-1, keepdims=True)) * (1.0 / math.sqrt(MEM_HD))).astype(BF16)
            dv_ref[:, sl] += _dot_tn(p.astype(BF16), dob)
            dk_ref[:, sl] += _dot_tn(ds, q)
            dqn = _dot(ds, kb)
            dn = dqn * w
            dq_ref[:, sl] = r * (dn - n * jnp.mean(dn * n, axis=-1, keepdims=True))
            dw_ref[...] += _colsum8(dqn * n)

    return pl.pallas_call(
        body, name="xattn_bwd", grid=(T // TB,),
        in_specs=[_row(D), _const((M, D)), _const((M, D)), _const((1, MEM_HD)), _row(D)],
        out_specs=[_row(D), _const((M, D)), _const((M, D)), _const((8, MEM_HD))],
        out_shape=[SDS((T, D), F32), SDS((M, D), F32), SDS((M, D), F32), SDS((8, MEM_HD), F32)],
        compiler_params=_params())(qraw, k, v, wq, do)


def _mem_bwd(mem, mem_norm_w, w_mkv, mk_norm_w, mem_n, k_raw, dk, dv):
    M = mem.shape[0]

    def body(mem_ref, wn_ref, w_ref, wk_ref, mn_ref, kraw_ref, dk_ref, dv_ref, dw_ref, dwn_ref, dwk_ref, dkv_scr):
        wk = wk_ref[...]
        dwk = jnp.zeros((8, MEM_HD), F32)
        for h in range(MEM_HEADS):
            sl = slice(MEM_HD * h, MEM_HD * (h + 1))
            x = kraw_ref[:, sl]
            r = _rsqrt_ms(x)
            n = x * r
            dkh = dk_ref[:, sl]
            dn = dkh * wk
            dkv_scr[:, sl] = (r * (dn - n * jnp.mean(dn * n, axis=-1, keepdims=True))).astype(BF16)
            dwk = dwk + _colsum8(dkh * n)
        dwk_ref[...] = dwk
        dkv_scr[:, D:] = dv_ref[...].astype(BF16)
        dkv = dkv_scr[...]
        mn = mn_ref[...]
        dmn = jnp.zeros((M, D), F32)
        for s in range(4):
            part = dkv[:, 512 * s:512 * (s + 1)]
            dw_ref[s] = _dot_tn(mn, part).astype(BF16)
            dmn = dmn + _dot_nt(part, w_ref[s, :D, :])
        mv = mem_ref[...]
        dwn_ref[...] = _colsum8(dmn * (mv * _rsqrt_ms(mv)))

    vm = pl.BlockSpec(memory_space=pltpu.VMEM)
    return pl.pallas_call(
        body, name="mem_bwd", in_specs=[vm] * 8, out_specs=[vm] * 3,
        out_shape=[SDS((4, D, 512), BF16), SDS((8, D), F32), SDS((8, MEM_HD), F32)],
        scratch_shapes=[pltpu.VMEM((M, 2 * D), BF16)],
        compiler_params=_params())(mem, mem_norm_w, w_mkv, mk_norm_w, mem_n, k_raw, dk, dv)


def _local_step(x, mem, tgt, w, later_weights=None, later_partials=None, last_partials=None):
    T = x.shape[0]
    wgk = jnp.zeros((128, 256), F32).at[:16].set(w["w_gk_up"].astype(F32))
    wg512 = jnp.tile(w["gla_norm_w"], (1, 4))
    ws512 = jnp.tile(w["sb_norm_w"], (1, 8))
    bd64 = _group_ones(512, 64)
    cm = _gla_consts()
    tri = _sb_tri()
    tri_t = tri.T

    h1 = _norm_fwd(x, w["mix_norm_w"], "norm1_fwd")
    proj, gk, qs, ks, vs = _matmul(h1, w["w_in"], mode="nn", tm=TB, tn=DIN_P, name="mm_proj",
                                   epilogue=_then_proj_split(wgk, w["b_gk"], T, TB))
    o_g, states = _gla_fwd(proj, gk, cm)
    if later_weights is None:
        o_s, carries = _sb_fwd(qs, ks, vs, tri)
    else:
        o_s, carries, *gathered = _sb_fwd(qs, ks, vs, tri, later_weights[0])
        w = {**w, **later_weights[1](gathered)}
    cat = _mix_cat(o_g, proj, o_s, wg512, ws512, bd64)
    x1, h2 = _matmul(cat, w["w_out"], mode="nn", tm=TB, tn=D, res=x, name="mm_out",
                     epilogue=_then_norm_fwd(w["xattn_norm_w"], T, TB))
    qraw = _matmul(h2, w["w_mq"], mode="nn", tm=TB, tn=D, name="mm_mq")
    mem_n, k_raw, k_n, v_m = _mem_fwd(mem, w["mem_norm_w"], w["w_mkv"], w["mk_norm_w"])
    om = _xattn_fwd(qraw, k_n, v_m, w["mq_norm_w"])
    x2, h3 = _matmul(om, w["w_mo"], mode="nn", tm=TB, tn=D, res=x1, name="mm_mo",
                     epilogue=_then_norm_fwd(w["ffn_norm_w"], T, TB))
    gu, act = _gate_up_act(h3, w["w_gate_up"])
    dx3, loss_rows = _matmul(act, w["w_down"], mode="nn", tm=TB, tn=D, res=x2, name="mm_down",
                             epilogue=_then_loss(tgt, TB))

    g = {}
    dw_tk = min(DW_TK, T)
    dw = dict(mode="tn", tk=dw_tk, out_dtype=BF16)
    g["w_down"] = _matmul(act, dx3, tm=1408, tn=D, name="mm_dw_down", **dw)
    dgu = _down_bwd(dx3, w["w_down"], gu)
    g["w_gate_up"] = _matmul(
        h3, dgu, tm=D, tn=FF_TN, by_column_tile=True, mnk=(D, 2 * D_FF, T), name="mm_dw_gate_up",
        b_spec=pl.BlockSpec((None, dw_tk, FF_TN), lambda j, i, k: (j // 2, k, j % 2)), **dw)
    tm3 = min(2 * TB, T)
    dx2, g["ffn_norm_w"] = _matmul(
        dgu, w["w_gate_up"], mode="nt", tm=tm3, tn=D, tk=FF_TN, mnk=(T, D, 2 * D_FF), name="mm_dh3",
        a_spec=pl.BlockSpec((None, tm3, FF_TN), lambda j, i, k: (k // 2, i, k % 2)),
        b_spec=pl.BlockSpec((None, D, FF_TN), lambda j, i, k: (k, 0, 0)),
        epilogue=_then_norm_bwd(x2, w["ffn_norm_w"], dx3, tm3))
    g["w_mo"] = _matmul(om, dx2, tm=D, tn=D, name="mm_dw_mo", **dw)
    dom = _matmul(dx2, w["w_mo"], mode="nt", tm=TB, tn=D, name="mm_dom")
    dqraw, dk_n, dv_m, g["mq_norm_w"] = _xattn_bwd(qraw, k_n, v_m, w["mq_norm_w"], dom)
    g["w_mkv"], g["mem_norm_w"], g["mk_norm_w"] = _mem_bwd(
        mem, w["mem_norm_w"], w["w_mkv"], w["mk_norm_w"], mem_n, k_raw, dk_n, dv_m)
    g["w_mq"] = _matmul(h2, dqraw, tm=D, tn=D, name="mm_dw_mq", **dw)
    dx1, g["xattn_norm_w"] = _matmul(dqraw, w["w_mq"], mode="nt", tm=TB, tn=D, name="mm_dh2",
                                     epilogue=_then_norm_bwd(x1, w["xattn_norm_w"], dx2, TB))
    g["w_out"] = _matmul(cat, dx1, tm=D, tn=D, name="mm_dw_out", **dw)
    dcat = _matmul(dx1, w["w_out"], mode="nt", tm=TB, tn=D, name="mm_dcat")
    do_g, dg_g, do_s, dwg, dws = _mix_cat_bwd(dcat, o_g, proj, o_s, wg512, ws512, bd64)
    dq_s, dk_s, dv_s = _sb_bwd(qs, ks, vs, do_s, carries, tri, tri_t)
    parts = () if later_partials is None else later_partials(g)
    dq_g, dk_g, dv_g, dgk, *received = _gla_bwd(proj, gk, cm, states, do_g, parts)
    dproj, dwgk, g["b_gk"] = _dproj_assemble(proj, dq_g, dk_g, dv_g, dg_g, dq_s, dk_s, dv_s, dgk, wgk, w["b_gk"])
    g["w_in"] = _matmul(h1, dproj, tm=D, tn=640, name="mm_dw_in", **dw)
    g["w_gk_up"] = dwgk[:16]
    last_parts = [] if last_partials is None else last_partials(g)
    grad_x, g["mix_norm_w"], *last_received = _matmul(
        dproj, w["w_in"], mode="nt", tm=TB, tn=D, name="mm_dh1",
        epilogue=_then_norm_bwd(x, w["mix_norm_w"], dx1, TB), exchange=last_parts)

    g["gla_norm_w"], g["sb_norm_w"] = dwg, dws
    if later_partials is None:
        g["gla_norm_w"] = dwg.reshape(8, 4, 128).sum(axis=1)
        g["sb_norm_w"] = dws.reshape(8, 8, 64).sum(axis=1)
        for n in SMALL:
            g[n] = jnp.sum(g[n], axis=0, keepdims=True)
        return jnp.sum(loss_rows), grad_x, g
    return jnp.sum(loss_rows), grad_x, g, (list(parts) + last_parts, list(received) + last_received)


def _mesh_pos():
    return lax.axis_index("x"), lax.axis_index("y"), lax.axis_index("c")


def _other_chips(x, y):
    return [(1 - x, y), (x, 1 - y), (1 - x, 1 - y)]


HBM = pl.BlockSpec(memory_space=pl.ANY)


def _ag_phases(w_refs, out_refs, send_sems, recv_sems):
    n = len(w_refs)
    x, y, c = _mesh_pos()
    me = 2 * x + y
    sibling = (x, y, 1 - c)
    chips = _other_chips(x, y)
    mine, theirs = c, 1 - c

    def copy(a, k, src, dst, to):
        return pltpu.make_async_remote_copy(src_ref=src, dst_ref=dst, send_sem=send_sems.at[6 * a + k],
                                            recv_sem=recv_sems.at[6 * a + k], device_id=to, device_id_type=MESH)

    def firsts():
        return [copy(a, k, w_refs[a].at[mine], out_refs[a].at[me, mine], (cx, cy, c))
                for a in range(n) for k, (cx, cy) in enumerate(chips)]

    def landed(a, k, half):
        cx, cy = chips[k]
        return out_refs[a].at[2 * cx + cy, half]

    def passes():
        return [copy(a, 3 + k, landed(a, k, mine), landed(a, k, mine), sibling) for a in range(n) for k in range(3)]

    def start():
        for cp in firsts():
            cp.start()

    def pass_on():
        for a in range(n):
            for k, (cx, cy) in enumerate(chips):
                copy(a, k, landed(a, k, mine), landed(a, k, mine), (cx, cy, c)).wait_recv()
                copy(a, 3 + k, landed(a, k, mine), landed(a, k, mine), sibling).start()

    def finish():
        for a in range(n):
            for k in range(3):
                copy(a, 3 + k, landed(a, k, theirs), landed(a, k, theirs), sibling).wait_recv()
        for cp in firsts() + passes():
            cp.wait_send()

    return start, pass_on, finish


def _ag_sems(n):
    return [pltpu.SemaphoreType.DMA((6 * n,)), pltpu.SemaphoreType.DMA((6 * n,))]


def _ag_weights(wps):
    n = len(wps)

    def body(*refs):
        for phase in _ag_phases(refs[:n], refs[n:2 * n], *refs[2 * n:]):
            phase()

    return pl.pallas_call(
        body, name="ag_weights", in_specs=[HBM] * n, out_specs=[HBM] * n,
        out_shape=[SDS((4,) + wp.shape, wp.dtype) for wp in wps], scratch_shapes=_ag_sems(n),
        compiler_params=pltpu.CompilerParams(has_side_effects=True))(*wps)


def _rs_swap_halves(gps, name):
    n = len(gps)

    def body(*refs):
        g_refs, out_refs = refs[:n], refs[n:2 * n]
        send_sems, recv_sems = refs[2 * n:]
        x, y, c = _mesh_pos()
        copies = [pltpu.make_async_remote_copy(
            src_ref=g_refs[a].at[:, 1 - c], dst_ref=out_refs[a], send_sem=send_sems.at[a], recv_sem=recv_sems.at[a],
            device_id=(x, y, 1 - c), device_id_type=MESH) for a in range(n)]
        for cp in copies:
            cp.start()
        for cp in copies:
            cp.wait()

    return pl.pallas_call(
        body, name=name, in_specs=[HBM] * n, out_specs=[HBM] * n,
        out_shape=[SDS((4,) + gp.shape[2:], gp.dtype) for gp in gps],
        scratch_shapes=[pltpu.SemaphoreType.DMA((n,)), pltpu.SemaphoreType.DMA((n,))],
        compiler_params=pltpu.CompilerParams(has_side_effects=True))(*gps)


def _rs_add_halves(gp, other, c_arr, name):
    h, w = other.shape[1:]

    def body(c_ref, a_ref, b_ref, o_ref):
        o_ref[...] = (a_ref[0].astype(F32) + b_ref[...].astype(F32)).astype(BF16)

    return pl.pallas_call(
        body, name=name,
        grid_spec=pltpu.PrefetchScalarGridSpec(
            num_scalar_prefetch=1, grid=(4,),
            in_specs=[pl.BlockSpec((1, 1, h, w), lambda s, c: (s, c[0], 0, 0)),
                      pl.BlockSpec((1, h, w), lambda s, c: (s, 0, 0))],
            out_specs=pl.BlockSpec((1, h, w), lambda s, c: (s, 0, 0))),
        out_shape=SDS((4, h, w), BF16), compiler_params=_params())(c_arr, gp, other)


def _rs_exchange_phases(p_refs, out_refs, send_sems, recv_sems):
    n = len(p_refs)
    x, y, c = _mesh_pos()
    me = 2 * x + y
    chips = _other_chips(x, y)

    def sends():
        return [pltpu.make_async_remote_copy(
            src_ref=p_refs[a].at[2 * cx + cy], dst_ref=out_refs[a].at[me], send_sem=send_sems.at[3 * a + k],
            recv_sem=recv_sems.at[3 * a + k], device_id=(cx, cy, c), device_id_type=MESH)
            for a in range(n) for k, (cx, cy) in enumerate(chips)]

    def start():
        for cp in sends():
            cp.start()

    def finish():
        for a in range(n):
            for k, (cx, cy) in enumerate(chips):
                slot = out_refs[a].at[2 * cx + cy]
                pltpu.make_async_remote_copy(
                    src_ref=slot, dst_ref=slot, send_sem=send_sems.at[3 * a + k], recv_sem=recv_sems.at[3 * a + k],
                    device_id=(cx, cy, c), device_id_type=MESH).wait_recv()
        for cp in sends():
            cp.wait_send()

    return start, finish


def _rs_exchange_sems(n):
    return [pltpu.SemaphoreType.DMA((3 * n,)), pltpu.SemaphoreType.DMA((3 * n,))]


def _rs_add_chips(recv, part, me_arr, name):
    h, w = part.shape[1:]
    th = h // 2 if (h // 2) % 16 == 0 else h

    def body(me_ref, r_ref, p_ref, o_ref):
        me = me_ref[0]
        total = None
        for k in range(4):
            term = jnp.where(me == k, p_ref[k], r_ref[k]).astype(F32)
            total = term if total is None else total + term
        o_ref[...] = total

    spec = pl.BlockSpec((4, th, w), lambda t, me: (0, t, 0))
    return pl.pallas_call(
        body, name=name,
        grid_spec=pltpu.PrefetchScalarGridSpec(
            num_scalar_prefetch=1, grid=(h // th,), in_specs=[spec, spec],
            out_specs=pl.BlockSpec((th, w), lambda t, me: (t, 0))),
        out_shape=SDS((h, w), F32), compiler_params=_params())(me_arr, recv, part)


def _rs_share(halves):
    n = len(halves)

    def body(*refs):
        h_refs, out_refs = refs[:n], refs[n:2 * n]
        send_sems, recv_sems = refs[2 * n:]
        x, y, c = _mesh_pos()
        copies = [pltpu.make_async_remote_copy(
            src_ref=h_refs[a], dst_ref=out_refs[a], send_sem=send_sems.at[a], recv_sem=recv_sems.at[a],
            device_id=(x, y, 1 - c), device_id_type=MESH) for a in range(n)]
        for cp in copies:
            cp.start()
        for cp in copies:
            cp.wait()

    return pl.pallas_call(
        body, name="rs_share", in_specs=[HBM] * n, out_specs=[HBM] * n,
        out_shape=[SDS(hs.shape, hs.dtype) for hs in halves],
        scratch_shapes=[pltpu.SemaphoreType.DMA((n,)), pltpu.SemaphoreType.DMA((n,))],
        compiler_params=pltpu.CompilerParams(has_side_effects=True))(*halves)


def _small_rows(partials):
    lanes = np.arange(512)
    fold = jnp.asarray((lanes[:, None] % 64 == np.arange(128)[None, :]).astype(np.float32), dtype=BF16)
    n_small = len(SMALL)

    def body(*refs):
        fold_ref, o_ref = refs[n_small], refs[n_small + 1]
        o_ref[...] = jnp.zeros_like(o_ref)
        for i, name in enumerate(SMALL):
            v = refs[i][...]
            if name == "gla_norm_w":
                v = v[:, 0:128] + v[:, 128:256] + v[:, 256:384] + v[:, 384:512]
            if name == "sb_norm_w":
                v = _dot_exact(v, fold_ref[...])
            n = SMALL_SIZES[name]
            o_ref[i:i + 1, 0:n] = jnp.sum(v, axis=0, keepdims=True)[:, 0:n]

    vm = pl.BlockSpec(memory_space=pltpu.VMEM)
    return pl.pallas_call(body, name="small_rows", in_specs=[vm] * (n_small + 1), out_specs=vm,
                          out_shape=SDS((SMALL_ROWS, 1024), F32))(*[partials[n] for n in SMALL], fold)


def _small_gather(s):
    def gather(s_ref, out_ref, send_sems, recv_sems, local_sem):
        x, y, c = _mesh_pos()
        me = 4 * x + 2 * y + c
        local = pltpu.make_async_copy(s_ref, out_ref.at[me], local_sem)
        local.start()
        peers = []
        for r in range(1, 8):
            px = 1 - x if r & 4 else x
            py = 1 - y if r & 2 else y
            pc = 1 - c if r & 1 else c
            peers.append((px, py, pc))
        sends = []
        for k, peer in enumerate(peers):
            cp = pltpu.make_async_remote_copy(
                src_ref=s_ref, dst_ref=out_ref.at[me], send_sem=send_sems.at[k], recv_sem=recv_sems.at[k],
                device_id=peer, device_id_type=MESH)
            cp.start()
            sends.append(cp)
        for k, (px, py, pc) in enumerate(peers):
            slot = out_ref.at[4 * px + 2 * py + pc]
            pltpu.make_async_remote_copy(
                src_ref=slot, dst_ref=slot, send_sem=send_sems.at[k], recv_sem=recv_sems.at[k],
                device_id=(px, py, pc), device_id_type=MESH).wait_recv()
        for cp in sends:
            cp.wait_send()
        local.wait()

    return pl.pallas_call(
        gather, name="small_gather", in_specs=[HBM], out_specs=HBM,
        out_shape=SDS((8, SMALL_ROWS, 1024), F32),
        scratch_shapes=[pltpu.SemaphoreType.DMA((7,)), pltpu.SemaphoreType.DMA((7,)), pltpu.SemaphoreType.DMA],
        compiler_params=pltpu.CompilerParams(has_side_effects=True))(s)


def _adamw_update(w, g, m, v):
    mn = ADAM_B1 * m + (1.0 - ADAM_B1) * g
    vn = ADAM_B2 * v + (1.0 - ADAM_B2) * (g * g)
    c1 = 1.0 - ADAM_B1 ** ADAM_STEP
    c2 = 1.0 - ADAM_B2 ** ADAM_STEP
    return -ADAM_LR * ((mn / c1) / (jnp.sqrt(vn / c2) + ADAM_EPS) + ADAM_WD * w), mn, vn


def _small_update(parts, w, m, v):
    n_small = len(SMALL)

    def body(p_ref, *refs):
        ins, outs = refs[:3 * n_small], refs[3 * n_small:]
        total = p_ref[0]
        for k in range(1, 8):
            total = total + p_ref[k]
        for i, name in enumerate(SMALL):
            g = total[i:i + 1, 0:SMALL_SIZES[name]]
            d, mn, vn = _adamw_update(ins[i][...], g, ins[n_small + i][...], ins[2 * n_small + i][...])
            for slot, val in enumerate((g, d, mn, vn)):
                outs[slot * n_small + i][...] = val

    vm = pl.BlockSpec(memory_space=pltpu.VMEM)
    shapes = [SDS((1, SMALL_SIZES[n]), F32) for n in SMALL]
    outs = pl.pallas_call(
        body, name="small_adamw", in_specs=[vm] * (1 + 3 * n_small), out_specs=[vm] * (4 * n_small),
        out_shape=shapes * 4)(parts, *[w[n] for n in SMALL], *[m[n] for n in SMALL], *[v[n] for n in SMALL])
    return [dict(zip(SMALL, outs[s * n_small:(s + 1) * n_small])) for s in range(4)]


def _adamw(w, g, m, v, name):
    rows, cols = w.shape
    tr = rows
    for cand in (512, 352, 256):
        if rows > cand and rows % cand == 0:
            tr = cand
            break

    def body(w_ref, g_ref, m_ref, v_ref, d_ref, mo_ref, vo_ref):
        d_ref[...], mo_ref[...], vo_ref[...] = _adamw_update(w_ref[...], g_ref[...], m_ref[...], v_ref[...])

    spec = pl.BlockSpec((tr, cols), lambda i: (i, 0))
    return pl.pallas_call(
        body, name=name, grid=(rows // tr,), in_specs=[spec] * 4, out_specs=[spec] * 3,
        out_shape=[SDS((rows, cols), F32)] * 3, compiler_params=_params())(w, g, m, v)


SMALL_SIZES = {"mix_norm_w": 1024, "b_gk": 256, "gla_norm_w": 128, "sb_norm_w": 64, "xattn_norm_w": 1024,
               "mem_norm_w": 1024, "mq_norm_w": 256, "mk_norm_w": 256, "ffn_norm_w": 1024}


ROWS_OF = (("w_out", 256), ("w_mq", 256), ("w_mo", 256), ("w_down", 704))
WIN_ROWS = 1056
LATER = ("rows", "gate_up", "mkv")


def _shard_buffers(d, dtype):
    rows = jnp.concatenate([d[n] for n, _ in ROWS_OF], axis=0).astype(dtype)
    gk = jnp.pad(d["w_gk_up"], ((0, WIN_ROWS - D - 16), (0, DIN // 4 - 64)))
    win = jnp.concatenate([d["w_in"], gk], axis=0).astype(dtype)
    return [rows, d["w_gate_up"].astype(dtype), d["w_mkv"].astype(dtype)], win


def _in_halves(a):
    return a.reshape(a.shape[:-2] + (2, a.shape[-2] // 2, a.shape[-1]))


def _whole(a):
    return a.reshape(a.shape[:-3] + (2 * a.shape[-2], a.shape[-1]))


def _first_weights(win):
    w_in = win[:, :D].transpose(1, 0, 2).reshape(D, DIN)
    w_in = jnp.concatenate([w_in[:, :1536], w_in[:, 1552:], w_in[:, 1536:1552],
                            jnp.zeros((D, DIN_P - DIN), w_in.dtype)], axis=1)
    return {"w_in": w_in, "w_gk_up": win[:, D:D + 16, :64].transpose(1, 0, 2).reshape(16, 256)}


def _later_weights(rows, gate_up, mkv):
    out, off = {"w_gate_up": gate_up, "w_mkv": mkv}, 0
    for n, r in ROWS_OF:
        out[n] = rows[:, off:off + r].reshape(4 * r, 1024)
        off += r
    return out


def _later_grad_buffers(g):
    rows = jnp.concatenate([g[n].reshape(4, r, 1024) for n, r in ROWS_OF], axis=1)
    return [rows, g["w_gate_up"], g["w_mkv"]]


def _win_grad_buffer(g):
    gk = g["w_gk_up"].astype(BF16).reshape(16, 4, 64).transpose(1, 0, 2)
    gk = jnp.pad(gk, ((0, 0), (0, WIN_ROWS - D - 16), (0, DIN // 4 - 64)))
    gi = g["w_in"]
    gi = jnp.concatenate([gi[:, :1536], gi[:, C_LR:C_LR + 16], gi[:, 1536:C_LR]], axis=1)
    return jnp.concatenate([gi.reshape(D, 4, DIN // 4).transpose(1, 0, 2), gk], axis=1)


def _shard_grads(rows, gate_up, mkv, win):
    out, off = {"w_gate_up": gate_up, "w_mkv": mkv, "w_in": win[:D], "w_gk_up": win[D:D + 16, :64]}, 0
    for n, r in ROWS_OF:
        out[n] = rows[off:off + r]
        off += r
    return out


def kernel(x, mem, mix_norm_w, w_in, w_gk_up, b_gk, gla_norm_w, sb_norm_w, w_out, xattn_norm_w, mem_norm_w, w_mq, w_mkv, mq_norm_w, mk_norm_w, w_mo, ffn_norm_w, w_gate_up, w_down, loss_target, m_mix_norm_w, m_w_in, m_w_gk_up, m_b_gk, m_gla_norm_w, m_sb_norm_w, m_w_out, m_xattn_norm_w, m_mem_norm_w, m_w_mq, m_w_mkv, m_mq_norm_w, m_mk_norm_w, m_w_mo, m_ffn_norm_w, m_w_gate_up, m_w_down, v_mix_norm_w, v_w_in, v_w_gk_up, v_b_gk, v_gla_norm_w, v_sb_norm_w, v_w_out, v_xattn_norm_w, v_mem_norm_w, v_w_mq, v_w_mkv, v_mq_norm_w, v_mk_norm_w, v_w_mo, v_ffn_norm_w, v_w_gate_up, v_w_down):
    args = dict(locals())
    wts = {n: args[n][0] if n in BIG else args[n] for n in WEIGHTS}
    mom = {n: args["m_" + n][0] if n in BIG else args["m_" + n] for n in WEIGHTS}
    var = {n: args["v_" + n][0] if n in BIG else args["v_" + n] for n in WEIGHTS}

    c = lax.axis_index("c")
    chip = 2 * lax.axis_index("x") + lax.axis_index("y")
    c_arr = c.astype(jnp.int32).reshape(1)
    chip_arr = chip.astype(jnp.int32).reshape(1)
    def own_slot_filled(gathered, mine):
        return [_whole(lax.dynamic_update_slice(got, wp[None], (chip, 0, 0, 0))) for got, wp in zip(gathered, mine)]

    def chip_partials(names, buffers, tag):
        gps = [_in_halves(b) for b in buffers]
        return [_rs_add_halves(gp, other, c_arr, "rs_add_halves_" + n)
                for n, gp, other in zip(names, gps, _rs_swap_halves(gps, "rs_swap_halves_" + tag))]

    later_wps, win_wp = _shard_buffers(wts, BF16)
    later_wps, win_wp = [_in_halves(b) for b in later_wps], _in_halves(win_wp)
    first = _first_weights(*own_slot_filled(_ag_weights([win_wp]), [win_wp]))
    first.update({n: wts[n] for n in SMALL})

    loss, grad_x, g, (parts, received) = _local_step(
        x[0], mem[0], loss_target[0], first,
        later_weights=(later_wps, lambda gathered: _later_weights(*own_slot_filled(gathered, later_wps))),
        later_partials=lambda g: chip_partials(LATER, _later_grad_buffers(g), "later"),
        last_partials=lambda g: chip_partials(("win",), [_win_grad_buffer(g)], "win"))
    loss = lax.psum(loss, ("x", "y", "c"))

    mine = [_rs_add_chips(recv, part, chip_arr, "rs_add_chips_" + n)
            for n, recv, part in zip(LATER + ("win",), received, parts)]
    totals = [jnp.concatenate([jnp.where(c == 0, m, t), jnp.where(c == 0, t, m)], axis=0)
              for m, t in zip(mine, _rs_share(mine))]
    grads = _shard_grads(*totals)

    small_grads, delta, new_m, new_v = _small_update(_small_gather(_small_rows(g)), wts, mom, var)
    grads.update(small_grads)
    for n in BIG:
        w2 = wts[n].reshape(-1, wts[n].shape[-1])
        d_, m_, v_ = _adamw(w2, grads[n].reshape(w2.shape), mom[n].reshape(w2.shape), var[n].reshape(w2.shape),
                            "adamw_" + n)
        delta[n], new_m[n], new_v[n] = (t.reshape((1,) + wts[n].shape) for t in (d_, m_, v_))
        grads[n] = grads[n].reshape((1,) + wts[n].shape)

    return (loss, grad_x[None], *[grads[n] for n in WEIGHTS], *[delta[n] for n in WEIGHTS],
            *[new_m[n] for n in WEIGHTS], *[new_v[n] for n in WEIGHTS])
```

```python
import functools
import math

import numpy as np
import jax
import jax.numpy as jnp
from jax import lax
from jax.experimental import pallas as pl
from jax.experimental.pallas import tpu as pltpu

F32 = jnp.float32
BF16 = jnp.bfloat16
SDS = jax.ShapeDtypeStruct
MESH = pl.DeviceIdType.MESH

D = 1024
EPS = 1e-6
D_FF = 2816
GLA_GATE_NORM = 16.0
GLA_C = 64
MEM_HEADS = 4
MEM_HD = 256
C_QG, C_KG, C_VG, C_GG, C_QS, C_KS, C_VS, C_LR = 0, 256, 512, 1024, 1536, 2048, 2560, 3072
DIN = 3088
DIN_P = 3200
TB = 512
SBQ = 256
VMEM_LIMIT = 56 * 1024 * 1024
HIGHEST = lax.Precision.HIGHEST

ADAM_LR, ADAM_B1, ADAM_B2, ADAM_EPS, ADAM_WD, ADAM_STEP = 0.001, 0.9, 0.999, 1e-08, 0.01, 10

BIG = ("w_in", "w_gk_up", "w_out", "w_mq", "w_mkv", "w_mo", "w_gate_up", "w_down")
SMALL = ("mix_norm_w", "b_gk", "gla_norm_w", "sb_norm_w", "xattn_norm_w", "mem_norm_w", "mq_norm_w",
         "mk_norm_w", "ffn_norm_w")
WEIGHTS = ("mix_norm_w", "w_in", "w_gk_up", "b_gk", "gla_norm_w", "sb_norm_w", "w_out", "xattn_norm_w",
           "mem_norm_w", "w_mq", "w_mkv", "mq_norm_w", "mk_norm_w", "w_mo", "ffn_norm_w", "w_gate_up", "w_down")
SMALL_ROWS = 16


def _params(**kw):
    return pltpu.CompilerParams(vmem_limit_bytes=VMEM_LIMIT, **kw)


def _row(c, j=0):
    return pl.BlockSpec((TB, c), lambda i, j=j: (i, j))


def _const(shape):
    return pl.BlockSpec(shape, lambda i: (0,) * len(shape))


def _dot(a, b):
    return lax.dot_general(a, b, (((1,), (0,)), ((), ())), preferred_element_type=F32)


def _dot_nt(a, b):
    return lax.dot_general(a, b, (((1,), (1,)), ((), ())), preferred_element_type=F32)


def _dot_tn(a, b):
    return lax.dot_general(a, b, (((0,), (0,)), ((), ())), preferred_element_type=F32)


def _dot_nt_f32(a, b):
    return lax.dot_general(a, b, (((1,), (1,)), ((), ())), precision=HIGHEST, preferred_element_type=F32)


def _split3(x):
    h = x.astype(BF16)
    r = x - h.astype(F32)
    m = r.astype(BF16)
    l = (r - m.astype(F32)).astype(BF16)
    return h, m, l


def _dot_exact(x, ones_mat):
    h, m, l = _split3(x)
    return _dot(h, ones_mat) + _dot(m, ones_mat) + _dot(l, ones_mat)


def _dot_hilo(x, ones_mat):
    h = x.astype(BF16)
    l = (x - h.astype(F32)).astype(BF16)
    return _dot(h, ones_mat) + _dot(l, ones_mat)


def _softplus(z):
    return jnp.maximum(z, 0.0) + jnp.log1p(jnp.exp(-jnp.abs(z)))


def _rsqrt_ms(x):
    return lax.rsqrt(jnp.mean(x * x, axis=-1, keepdims=True) + EPS)


def _colsum8(x):
    r, c = x.shape
    return jnp.sum(x.reshape(r // 8, 8, c), axis=0)


def _matmul(a, b, *, mode, tm, tn, tk=None, res=None, out_dtype=F32, by_column_tile=False, a_spec=None,
            b_spec=None, mnk=None, epilogue=None, exchange=None, name):
    if mnk is not None:
        M, N, K = mnk
    else:
        K, M = a.shape if mode == "tn" else a.shape[::-1]
        N = b.shape[0] if mode == "nt" else b.shape[1]
    tk = K if tk is None else tk
    assert M % tm == 0 and N % tn == 0 and K % tk == 0, (name, M, N, K, tm, tn, tk)
    nk = K // tk
    if a_spec is None:
        if mode == "tn":
            a_spec = pl.BlockSpec((tk, tm), lambda j, i, k: (k, i))
        else:
            a_spec = pl.BlockSpec((tm, tk), lambda j, i, k: (i, k))
    if b_spec is None:
        if mode == "nt":
            b_spec = pl.BlockSpec((tn, tk), lambda j, i, k: (j, k))
        else:
            b_spec = pl.BlockSpec((tk, tn), lambda j, i, k: (k, j))
    if by_column_tile:
        assert res is None
        o_spec = pl.BlockSpec((None, tm, tn), lambda j, i, k: (j, i, 0))
        o_shape = SDS((N // tn, M, tn), out_dtype)
    else:
        o_spec = pl.BlockSpec((tm, tn), lambda j, i, k: (i, j))
        o_shape = SDS((M, N), out_dtype)
    dot = {"nn": _dot, "nt": _dot_nt, "tn": _dot_tn}[mode]
    has_res = res is not None
    extra_in, extra_out, finish, *lhs_from = epilogue if epilogue is not None else ((), (), None)
    n_in, n_out = 2 + has_res + len(extra_in), 1 + len(extra_out)
    sent = list(exchange) if exchange is not None else []
    ns = len(sent)
    grid = (N // tn, M // tm, nk)

    def body(*refs):
        a_ref, b_ref = refs[0], refs[1]
        res_ref = refs[2] if has_res else None
        in_refs, out_refs = refs[2 + has_res:n_in], refs[n_in + ns:n_in + ns + n_out]
        first_row_tile = pl.program_id(1) == 0
        scratch = refs[n_in + 2 * ns + n_out:]
        if ns:
            ids = [pl.program_id(d) for d in range(3)]
            rs_start, rs_finish = _rs_exchange_phases(
                refs[n_in:n_in + ns], refs[n_in + ns + n_out:n_in + 2 * ns + n_out], *scratch[nk > 1:])
            pl.when((ids[0] == 0) & (ids[1] == 0) & (ids[2] == 0))(rs_start)

        def done(t):
            if has_res:
                t = t + res_ref[...]
            if finish is None:
                out_refs[0][...] = t.astype(out_dtype)
            else:
                finish(t, first_row_tile, in_refs, out_refs)

        lhs = lhs_from[0](a_ref[...], in_refs, out_refs) if lhs_from else a_ref[...].astype(BF16)
        p = dot(lhs, b_ref[...].astype(BF16))
        if nk == 1:
            done(p)
        else:
            acc_ref = scratch[0]
            k = pl.program_id(2)

            @pl.when(k == 0)
            def _():
                acc_ref[...] = p

            @pl.when(k > 0)
            def _():
                acc_ref[...] += p

            @pl.when(k == nk - 1)
            def _():
                done(acc_ref[...])

        if ns:
            pl.when((ids[0] == grid[0] - 1) & (ids[1] == grid[1] - 1) & (ids[2] == grid[2] - 1))(rs_finish)

    in_specs = [a_spec, b_spec] + ([o_spec] if has_res else []) + [s for _, s in extra_in] + [HBM] * ns
    args = (a, b) + ((res,) if has_res else ()) + tuple(x for x, _ in extra_in) + tuple(sent)
    outs = pl.pallas_call(
        body, name=name, grid=grid, in_specs=in_specs,
        out_specs=[o_spec] + [s for _, s in extra_out] + [HBM] * ns,
        out_shape=[o_shape] + [s for s, _ in extra_out] + [SDS(p.shape, p.dtype) for p in sent],
        scratch_shapes=([pltpu.VMEM((tm, tn), F32)] if nk > 1 else []) + (_rs_exchange_sems(ns) if ns else []),
        compiler_params=_params(
            dimension_semantics=("arbitrary",) * 3 if ns else ("parallel", "parallel", "arbitrary")),
    )(*args)
    return outs[0] if epilogue is None and not ns else outs


def _full_row(tm, c):
    return pl.BlockSpec((tm, c), lambda j, i, k: (i, 0))


def _kept(shape):
    return pl.BlockSpec(shape, lambda j, i, k: (0,) * len(shape))


def _then_norm_fwd(w, T, tm):
    dm = w.shape[1]

    def finish(t, first, ins, outs):
        outs[0][...] = t
        outs[1][...] = (t * _rsqrt_ms(t) * ins[0][...]).astype(BF16)

    return [(w, _kept((1, dm)))], [(SDS((T, dm), BF16), _full_row(tm, dm))], finish


def _then_norm_bwd(x, w, dres, tm):
    T, dm = x.shape

    def finish(t, first, ins, outs):
        x_ref, w_ref, dres_ref = ins

        @pl.when(first)
        def _():
            outs[1][...] = jnp.zeros_like(outs[1])

        xv = x_ref[...]
        r = _rsqrt_ms(xv)
        n = xv * r
        dn = t * w_ref[...]
        outs[0][...] = dres_ref[...] + r * (dn - n * jnp.mean(dn * n, axis=-1, keepdims=True))
        outs[1][...] += _colsum8(t * n)

    return ([(x, _full_row(tm, dm)), (w, _kept((1, dm))), (dres, _full_row(tm, dm))],
            [(SDS((8, dm), F32), _kept((8, dm)))], finish)


def _then_loss(tgt, tm):
    def finish(t, first, ins, outs):
        @pl.when(first)
        def _():
            outs[1][...] = jnp.zeros_like(outs[1])

        e = t - ins[0][...]
        outs[0][...] = e * (1.0 / D)
        outs[1][...] += _colsum8(e * e) * (0.5 / D)

    return [(tgt, _full_row(tm, D))], [(SDS((8, D), F32), _kept((8, D)))], finish


def _norm_then_proj_split(norm_w, wgk, bgk, T, tm):
    def lhs_from(xv, ins, outs):
        h = (xv * _rsqrt_ms(xv) * ins[2][...]).astype(BF16)
        outs[5][...] = h
        return h

    def finish(t, first, ins, outs):
        wgk_ref, b_ref = ins[:2]
        proj_ref, gk_ref, qs_ref, ks_ref, vs_ref = outs[:5]
        proj_ref[...] = t
        u = _dot(t[:, C_LR:DIN_P].astype(BF16), wgk_ref[...].astype(BF16)) + b_ref[...]
        gk_ref[...] = -_softplus(-u) / GLA_GATE_NORM
        qs_ref[...] = (t[:, C_QS:C_KS] * 0.125).astype(BF16)
        ks_ref[...] = t[:, C_KS:C_VS].astype(BF16)
        vs_ref[...] = t[:, C_VS:C_LR].astype(BF16)

    return ([(wgk, _kept((128, 256))), (bgk, _kept((1, 256))), (norm_w, _kept((1, D)))],
            [(SDS((T, 256), F32), _full_row(tm, 256))] + [(SDS((T, 512), BF16), _full_row(tm, 512))] * 3
            + [(SDS((T, D), BF16), _full_row(tm, D))], finish, lhs_from)


def _dproj_assemble(proj, dq_g, dk_g, dv_g, dg_g, dq_s, dk_s, dv_s, dgk, wgk, bgk):
    T = proj.shape[0]

    def body(lr_ref, dqg_ref, dkg_ref, dvg_ref, dgg_ref, dqs_ref, dks_ref, dvs_ref, dgk_ref, wgk_ref, b_ref,
             dp_ref, dwgk_ref, dbgk_ref):
        @pl.when(pl.program_id(0) == 0)
        def _():
            dwgk_ref[...] = jnp.zeros_like(dwgk_ref)
            dbgk_ref[...] = jnp.zeros_like(dbgk_ref)

        lr = lr_ref[...].astype(BF16)
        wg = wgk_ref[...].astype(BF16)
        u = _dot(lr, wg) + b_ref[...]
        du = dgk_ref[...] * (jax.nn.sigmoid(-u) / GLA_GATE_NORM)
        dub = du.astype(BF16)
        dp_ref[:, C_QG:C_KG] = (dqg_ref[...] * 0.125).astype(BF16)
        dp_ref[:, C_KG:C_VG] = dkg_ref[...].astype(BF16)
        dp_ref[:, C_VG:C_GG] = dvg_ref[...].astype(BF16)
        dp_ref[:, C_GG:C_QS] = dgg_ref[...].astype(BF16)
        dp_ref[:, C_QS:C_KS] = (dqs_ref[...] * 0.125).astype(BF16)
        dp_ref[:, C_KS:C_VS] = dks_ref[...].astype(BF16)
        dp_ref[:, C_VS:C_LR] = dvs_ref[...].astype(BF16)
        dp_ref[:, C_LR:DIN_P] = _dot_nt(dub, wg).astype(BF16)
        dwgk_ref[...] += _dot_tn(lr, dub)
        dbgk_ref[...] += _colsum8(du)

    return pl.pallas_call(
        body, name="dproj_assemble", grid=(T // TB,),
        in_specs=[_row(128, C_LR // 128), _row(256), _row(256), _row(512), _row(512), _row(512), _row(512),
                  _row(512), _row(256), _const((128, 256)), _const((1, 256))],
        out_specs=[_row(DIN_P), _const((128, 256)), _const((8, 256))],
        out_shape=[SDS((T, DIN_P), BF16), SDS((128, 256), F32), SDS((8, 256), F32)],
        compiler_params=_params())(proj, dq_g, dk_g, dv_g, dg_g, dq_s, dk_s, dv_s, dgk, wgk, bgk)


def _group_ones(n, g):
    idx = np.arange(n) // g
    return jnp.asarray((idx[:, None] == idx[None, :]).astype(np.float32), dtype=BF16)


def _mix_cat(o_g, proj, o_s, wg512, ws512, bd64):
    T = o_g.shape[0]

    def body(og_ref, gg_ref, os_ref, wg_ref, ws_ref, bd_ref, cat_ref):
        og = og_ref[...]
        gg = gg_ref[...]
        s = gg * jax.nn.sigmoid(gg)
        for h in range(4):
            sl = slice(128 * h, 128 * (h + 1))
            x = og[:, sl]
            cat_ref[:, sl] = (x * _rsqrt_ms(x) * wg_ref[:, sl] * s[:, sl]).astype(BF16)
        osv = os_ref[...]
        ms = _dot_exact(osv * osv, bd_ref[...]) * (1.0 / 64.0)
        cat_ref[:, 512:1024] = (osv * lax.rsqrt(ms + EPS) * ws_ref[...]).astype(BF16)

    return pl.pallas_call(
        body, name="mix_cat", grid=(T // TB,),
        in_specs=[_row(512), _row(512, C_GG // 512), _row(512), _const((1, 512)), _const((1, 512)),
                  _const((512, 512))],
        out_specs=_row(1024), out_shape=SDS((T, 1024), BF16), compiler_params=_params())(
            o_g, proj, o_s, wg512, ws512, bd64)


def _mix_cat_bwd(dcat, o_g, proj, o_s, wg512, ws512, bd64):
    T = o_g.shape[0]

    def body(dc_ref, og_ref, gg_ref, os_ref, wg_ref, ws_ref, bd_ref, dog_ref, dgg_ref, dos_ref, dwg_ref, dws_ref):
        @pl.when(pl.program_id(0) == 0)
        def _():
            dwg_ref[...] = jnp.zeros_like(dwg_ref)
            dws_ref[...] = jnp.zeros_like(dws_ref)

        og = og_ref[...]
        gg = gg_ref[...]
        sg = jax.nn.sigmoid(gg)
        s = gg * sg
        ds = sg * (1.0 + gg * (1.0 - sg))
        for h in range(4):
            sl = slice(128 * h, 128 * (h + 1))
            x = og[:, sl]
            r = _rsqrt_ms(x)
            n = x * r
            w = wg_ref[:, sl]
            dc = dc_ref[:, sl]
            dy = dc * s[:, sl]
            dgg_ref[:, sl] = dc * (n * w) * ds[:, sl]
            dn = dy * w
            dog_ref[:, sl] = r * (dn - n * jnp.mean(dn * n, axis=-1, keepdims=True))
            dwg_ref[:, sl] += _colsum8(dy * n)
        osv = os_ref[...]
        bd = bd_ref[...]
        r = lax.rsqrt(_dot_exact(osv * osv, bd) * (1.0 / 64.0) + EPS)
        n = osv * r
        dc = dc_ref[:, 512:1024]
        dn = dc * ws_ref[...]
        dos_ref[...] = r * (dn - n * (_dot_exact(dn * n, bd) * (1.0 / 64.0)))
        dws_ref[...] += _colsum8(dc * n)

    return pl.pallas_call(
        body, name="mix_cat_bwd", grid=(T // TB,),
        in_specs=[_row(1024), _row(512), _row(512, C_GG // 512), _row(512), _const((1, 512)), _const((1, 512)),
                  _const((512, 512))],
        out_specs=[_row(512), _row(512), _row(512), _const((8, 512)), _const((8, 512))],
        out_shape=[SDS((T, 512), F32), SDS((T, 512), F32), SDS((T, 512), F32), SDS((8, 512), F32),
                   SDS((8, 512), F32)],
        compiler_params=_params())(dcat, o_g, proj, o_s, wg512, ws512, bd64)


FF_TN = 1408
DW_TK = 2048


def _gate_up_act(h, w):
    T = h.shape[0]
    nj = D_FF // FF_TN

    def body(h_ref, wg_ref, wu_ref, gu_ref, a_ref):
        hv = h_ref[...]
        g = _dot(hv, wg_ref[...])
        u = _dot(hv, wu_ref[...])
        gu_ref[0] = g.astype(BF16)
        gu_ref[1] = u.astype(BF16)
        a_ref[...] = (g * jax.nn.sigmoid(g) * u).astype(BF16)

    return pl.pallas_call(
        body, name="mm_gate_up_act", grid=(nj, T // TB),
        in_specs=[pl.BlockSpec((TB, D), lambda j, i: (i, 0)),
                  pl.BlockSpec((None, D, FF_TN), lambda j, i: (j, 0, 0)),
                  pl.BlockSpec((None, D, FF_TN), lambda j, i: (nj + j, 0, 0))],
        out_specs=[pl.BlockSpec((2, TB, FF_TN), lambda j, i: (0, i, j)),
                   pl.BlockSpec((TB, FF_TN), lambda j, i: (i, j))],
        out_shape=[SDS((2, T, D_FF), BF16), SDS((T, D_FF), BF16)],
        compiler_params=_params(dimension_semantics=("parallel", "parallel")))(h, w, w)


def _down_bwd(dy, w_down, gu):
    T = dy.shape[0]
    nj = D_FF // FF_TN

    def body(dy_ref, w_ref, gu_ref, dgu_ref):
        da = _dot_nt(dy_ref[...].astype(BF16), w_ref[...])
        g = gu_ref[0].astype(F32)
        sg = jax.nn.sigmoid(g)
        dgu_ref[0] = (da * gu_ref[1].astype(F32) * (sg * (1.0 + g * (1.0 - sg)))).astype(BF16)
        dgu_ref[1] = (da * (g * sg)).astype(BF16)

    return pl.pallas_call(
        body, name="mm_down_bwd", grid=(nj, T // TB),
        in_specs=[pl.BlockSpec((TB, D), lambda j, i: (i, 0)),
                  pl.BlockSpec((FF_TN, D), lambda j, i: (j, 0)),
                  pl.BlockSpec((2, TB, FF_TN), lambda j, i: (0, i, j))],
        out_specs=pl.BlockSpec((2, TB, FF_TN), lambda j, i: (0, i, j)),
        out_shape=SDS((2, T, D_FF), BF16),
        compiler_params=_params(dimension_semantics=("parallel", "parallel")))(dy, w_down, gu)


def _gla_consts():
    c = GLA_C
    L = np.tril(np.ones((c, c), np.float32))
    blocks = [L, L[(np.arange(c) // 16) * 16]]
    blocks += [np.repeat(L[16 * i:16 * i + 1], c, axis=0) for i in range(4)]
    blocks.append(np.repeat(L[c - 1:c], c, axis=0))
    return jnp.asarray(np.concatenate(blocks, axis=0))


@jax.custom_vjp
def _gla_lin(cm, g):
    cb = cm.astype(BF16)
    h, m, l = _split3(g)
    y = _dot(cb, h) + _dot(cb, m) + _dot(cb, l)
    return tuple(y[GLA_C * n:GLA_C * (n + 1)] for n in range(7))


def _gla_lin_fwd(cm, g):
    return _gla_lin(cm, g), cm


def _gla_lin_bwd(cm, cts):
    cb = cm.astype(BF16)
    h, m, l = _split3(jnp.concatenate(cts, axis=0))
    return None, _dot_tn(cb, h) + _dot_tn(cb, m) + _dot_tn(cb, l)


_gla_lin.defvjp(_gla_lin_fwd, _gla_lin_bwd)


GLA_SUB = 16
GLA_H = 4
GLA_W = 64 * GLA_H


def _head_lanes(rows, h):
    lane = lax.broadcasted_iota(jnp.int32, (rows, GLA_W), 1)
    return (lane >= 64 * h) & (lane < 64 * (h + 1))


def _gla_scores_terms(qs, k, b, rs, blk):
    row = lax.broadcasted_iota(jnp.int32, (GLA_C, GLA_W), 0)
    keep = row < GLA_SUB * (blk + 1)
    e = jnp.where(keep, jnp.exp(jnp.where(keep, rs[blk] - b, 0.0)), 0.0)
    qb = qs[GLA_SUB * blk:GLA_SUB * (blk + 1)]
    lhs = jnp.concatenate([jnp.where(_head_lanes(GLA_SUB, h), qb, 0.0) for h in range(GLA_H)], axis=0)
    return lhs, e


@jax.custom_vjp
def _gla_scores(qs, k, b, r0, r1, r2, r3):
    rs = (r0, r1, r2, r3)
    per_head = [[] for _ in range(GLA_H)]
    for blk in range(GLA_C // GLA_SUB):
        lhs, e = _gla_scores_terms(qs, k, b, rs, blk)
        a = _dot_nt_f32(lhs, k * e)
        for h in range(GLA_H):
            per_head[h].append(a[GLA_SUB * h:GLA_SUB * (h + 1)])
    return tuple(jnp.concatenate(p, axis=0) for p in per_head)


def _gla_scores_fwd(qs, k, b, r0, r1, r2, r3):
    return _gla_scores(qs, k, b, r0, r1, r2, r3), (qs, k, b, r0, r1, r2, r3)


def _gla_scores_bwd(saved, cts):
    qs, k, b = saved[:3]
    rs = saved[3:]
    dqs, drs = [], []
    dk = jnp.zeros_like(k)
    db = jnp.zeros_like(b)
    for blk in range(GLA_C // GLA_SUB):
        lhs, e = _gla_scores_terms(qs, k, b, rs, blk)
        rows = slice(GLA_SUB * blk, GLA_SUB * (blk + 1))
        da = jnp.concatenate([ct[rows] for ct in cts], axis=0)
        dlhs = lax.dot_general(da, k * e, (((1,), (0,)), ((), ())), precision=HIGHEST, preferred_element_type=F32)
        dq = jnp.zeros((GLA_SUB, GLA_W), F32)
        for h in range(GLA_H):
            dq = jnp.where(_head_lanes(GLA_SUB, h), dlhs[GLA_SUB * h:GLA_SUB * (h + 1)], dq)
        dqs.append(dq)
        dks = lax.dot_general(da, lhs, (((0,), (0,)), ((), ())), precision=HIGHEST, preferred_element_type=F32)
        dk = dk + dks * e
        darg = dks * (k * e)
        db = db - darg
        drs.append(darg)
    return (jnp.concatenate(dqs, axis=0), dk, db, *drs)


_gla_scores.defvjp(_gla_scores_fwd, _gla_scores_bwd)


def _gla_chunk(cm, q, k, g, v0, v1, v2, v3, st):
    c = GLA_C
    vs = (v0, v1, v2, v3)
    ri = lax.broadcasted_iota(jnp.int32, (c, c), 0)
    ci = lax.broadcasted_iota(jnp.int32, (c, c), 1)
    causal = ci <= ri
    b, r, r0, r1, r2, r3, bl = _gla_lin(cm, g)
    scores = _gla_scores(q * jnp.exp(b - r), k, b, r0, r1, r2, r3)
    qe = q * jnp.exp(b)
    kd = k * jnp.exp(bl - b)
    st_new = st * jnp.exp(jnp.concatenate([bl, bl], axis=0))
    outs = []
    for h in range(GLA_H):
        a = jnp.where(causal, scores[h], 0.0)
        outs.append(_dot_nt(jnp.where(_head_lanes(c, h), qe, 0.0), st) + _dot(a, vs[h]))
        st_new = st_new + jnp.where(_head_lanes(2 * c, h), _dot_tn(vs[h], kd), 0.0)
    return (*outs, st_new)


GLA_TB = 512


def _gla_fwd(proj, gk, cm):
    T = proj.shape[0]
    nb = T // GLA_TB
    nc = GLA_TB // GLA_C

    def body(q_ref, k_ref, v_ref, g_ref, cm_ref, o_ref, st_ref, st_scr):
        @pl.when(pl.program_id(0) == 0)
        def _():
            st_scr[...] = jnp.zeros_like(st_scr)

        cmv = cm_ref[...]

        def chunk(ci, carry):
            rs = pl.ds(pl.multiple_of(ci * GLA_C, GLA_C), GLA_C)
            st = st_scr[...]
            st_ref[ci] = st
            *outs, st_new = _gla_chunk(cmv, q_ref[rs, :] * 0.125, k_ref[rs, :], g_ref[rs, :],
                                       *[v_ref[rs, 128 * h:128 * (h + 1)] for h in range(GLA_H)], st)
            for h in range(GLA_H):
                o_ref[rs, 128 * h:128 * (h + 1)] = outs[h]
            st_scr[...] = st_new
            return carry

        lax.fori_loop(0, nc, chunk, 0, unroll=True)

    return pl.pallas_call(
        body, name="gla_fwd", grid=(nb,),
        in_specs=[pl.BlockSpec((GLA_TB, 256), lambda i: (i, C_QG // 256)),
                  pl.BlockSpec((GLA_TB, 256), lambda i: (i, C_KG // 256)),
                  pl.BlockSpec((GLA_TB, 512), lambda i: (i, C_VG // 512)),
                  pl.BlockSpec((GLA_TB, 256), lambda i: (i, 0)),
                  pl.BlockSpec((7 * GLA_C, GLA_C), lambda i: (0, 0))],
        out_specs=[pl.BlockSpec((GLA_TB, 512), lambda i: (i, 0)),
                   pl.BlockSpec((nc, 128, GLA_W), lambda i: (i, 0, 0))],
        out_shape=[SDS((T, 512), F32), SDS((T // GLA_C, 128, GLA_W), F32)],
        scratch_shapes=[pltpu.VMEM((128, GLA_W), F32)],
        compiler_params=_params(dimension_semantics=("arbitrary",)))(proj, proj, proj, gk, cm)


def _gla_bwd(proj, gk, cm, states, do, parts=()):
    T = proj.shape[0]
    nb = T // GLA_TB
    nc = GLA_TB // GLA_C
    n = len(parts)

    def body(q_ref, k_ref, v_ref, g_ref, cm_ref, st_ref, do_ref, *rest):
        p_refs, (dq_ref, dk_ref, dv_ref, dg_ref), out_refs = rest[:n], rest[n:n + 4], rest[n + 4:2 * n + 4]
        dst_scr = rest[2 * n + 4]
        step = pl.program_id(0)
        if n:
            rs_start, rs_finish = _rs_exchange_phases(p_refs, out_refs, *rest[2 * n + 5:])

        @pl.when(step == 0)
        def _():
            dst_scr[...] = jnp.zeros_like(dst_scr)
            if n:
                rs_start()

        cmv = cm_ref[...]

        def chunk(t, carry):
            ci = nc - 1 - t
            rs = pl.ds(pl.multiple_of(ci * GLA_C, GLA_C), GLA_C)
            _, vjp = jax.vjp(
                functools.partial(_gla_chunk, cmv), q_ref[rs, :] * 0.125, k_ref[rs, :], g_ref[rs, :],
                *[v_ref[rs, 128 * h:128 * (h + 1)] for h in range(GLA_H)], st_ref[ci])
            dq, dk, dg, *dvs, dst = vjp((*[do_ref[rs, 128 * h:128 * (h + 1)] for h in range(GLA_H)], dst_scr[...]))
            dq_ref[rs, :] = dq
            dk_ref[rs, :] = dk
            dg_ref[rs, :] = dg
            for h in range(GLA_H):
                dv_ref[rs, 128 * h:128 * (h + 1)] = dvs[h]
            dst_scr[...] = dst
            return carry

        lax.fori_loop(0, nc, chunk, 0, unroll=True)

        if n:
            pl.when(step == nb - 1)(rs_finish)

    rev = lambda i: nb - 1 - i
    return pl.pallas_call(
        body, name="gla_bwd", grid=(nb,),
        in_specs=[pl.BlockSpec((GLA_TB, 256), lambda i: (rev(i), C_QG // 256)),
                  pl.BlockSpec((GLA_TB, 256), lambda i: (rev(i), C_KG // 256)),
                  pl.BlockSpec((GLA_TB, 512), lambda i: (rev(i), C_VG // 512)),
                  pl.BlockSpec((GLA_TB, 256), lambda i: (rev(i), 0)),
                  pl.BlockSpec((7 * GLA_C, GLA_C), lambda i: (0, 0)),
                  pl.BlockSpec((nc, 128, GLA_W), lambda i: (rev(i), 0, 0)),
                  pl.BlockSpec((GLA_TB, 512), lambda i: (rev(i), 0))] + [HBM] * n,
        out_specs=[pl.BlockSpec((GLA_TB, 256), lambda i: (rev(i), 0)),
                   pl.BlockSpec((GLA_TB, 256), lambda i: (rev(i), 0)),
                   pl.BlockSpec((GLA_TB, 512), lambda i: (rev(i), 0)),
                   pl.BlockSpec((GLA_TB, 256), lambda i: (rev(i), 0))] + [HBM] * n,
        out_shape=[SDS((T, 256), F32), SDS((T, 256), F32), SDS((T, 512), F32), SDS((T, 256), F32)]
        + [SDS(p.shape, p.dtype) for p in parts],
        scratch_shapes=[pltpu.VMEM((128, GLA_W), F32)] + (_rs_exchange_sems(n) if n else []),
        compiler_params=_params(dimension_semantics=("arbitrary",)))(proj, proj, proj, gk, cm, states, do, *parts)


SB_DEAD = 105.0
SB_COUNT_LANE = 127


def _sb_tri():
    i = np.arange(SBQ)
    return jnp.asarray((i[:, None] > i[None, :]).astype(np.float32), dtype=BF16)


def _sb_block_fwd(qh, kb, tri, carry, strict):
    z = _dot_nt(qh, kb)
    sp = _softplus(z)
    l1 = -sp
    if strict is not None:
        l1 = jnp.where(strict, l1, 0.0)
    log_a = (z - sp) + _dot_hilo(l1, tri) + carry
    a = jnp.exp(log_a)
    if strict is not None:
        a = jnp.where(strict, a, 0.0)
    return z - sp, l1, a


def _sb_fwd(qs, ks, vs, tri, wps=()):
    T = qs.shape[0]
    nq = T // SBQ
    n = len(wps)

    def body(q_ref, k_ref, v_ref, tri_ref, *rest):
        w_refs, (o_ref, c_ref), out_refs = rest[:n], rest[n:n + 2], rest[n + 2:2 * n + 2]
        pair = pl.program_id(0)
        i = pl.program_id(1)
        if n:
            ag_start, ag_pass_on, ag_finish = _ag_phases(w_refs, out_refs, *rest[2 * n + 2:])
            pl.when((pair == 0) & (i == 0))(ag_start)
        lane = lax.broadcasted_iota(jnp.int32, (1, 128), 1)
        clane = lax.broadcasted_iota(jnp.int32, (SBQ, 128), 1)
        strict = (lax.broadcasted_iota(jnp.int32, (SBQ, SBQ), 1) < lax.broadcasted_iota(jnp.int32, (SBQ, SBQ), 0))
        tri_v = tri_ref[...]
        qv = q_ref[...]
        first_head = lane < 64
        qhs = (jnp.where(first_head, qv, jnp.zeros_like(qv)), jnp.where(first_head, jnp.zeros_like(qv), qv))

        def block(j, carries, accs, masked):
            rs = pl.ds(pl.multiple_of(j * SBQ, SBQ), SBQ)
            kb = k_ref[rs, :]
            vb = v_ref[rs, :]
            out_c, out_a = [], []
            for hh in range(2):
                _, l1, a = _sb_block_fwd(qhs[hh], kb, tri_v, carries[hh], strict if masked else None)
                out_a.append(accs[hh] + _dot(a.astype(BF16), vb))
                out_c.append(carries[hh] + jnp.sum(l1, axis=1, keepdims=True))
            return out_c, out_a

        zero1 = jnp.zeros((SBQ, 1), F32)
        zero128 = jnp.zeros((SBQ, 128), F32)
        (c0, c1), (a0, a1) = block(i, (zero1, zero1), (zero128, zero128), True)

        def more(state):
            return (state[0] <= i) & (jnp.maximum(jnp.max(state[1]), jnp.max(state[2])) > -SB_DEAD)

        def step(state):
            jj, c0, c1, a0, a1, t0, t1 = state
            j = i - jj
            t0 = jnp.where(clane == j, c0, t0)
            t1 = jnp.where(clane == j, c1, t1)
            (c0, c1), (a0, a1) = block(j, (c0, c1), (a0, a1), False)
            return jj + 1, c0, c1, a0, a1, t0, t1

        jj, c0, c1, a0, a1, t0, t1 = lax.while_loop(
            more, step, (jnp.int32(1), c0, c1, a0, a1, zero128, zero128))
        o_ref[...] = jnp.where(first_head, a0, a1)
        swept = (jj - 1).astype(F32)
        c_ref[0, :, 0:128] = jnp.where(clane == SB_COUNT_LANE, swept, t0)
        c_ref[0, :, 128:256] = jnp.where(clane == SB_COUNT_LANE, swept, t1)
        if n:
            pl.when((pair == 3) & (i == 0))(ag_pass_on)
            pl.when((pair == 3) & (i == nq - 1))(ag_finish)

    return pl.pallas_call(
        body, name="sb_fwd", grid=(4, nq),
        in_specs=[pl.BlockSpec((SBQ, 128), lambda h, i: (i, h)),
                  pl.BlockSpec((T, 128), lambda h, i: (0, h)),
                  pl.BlockSpec((T, 128), lambda h, i: (0, h)),
                  pl.BlockSpec((SBQ, SBQ), lambda h, i: (0, 0))] + [HBM] * n,
        out_specs=[pl.BlockSpec((SBQ, 128), lambda h, i: (i, h)),
                   pl.BlockSpec((1, SBQ, 256), lambda h, i: (h, i, 0))] + [HBM] * n,
        out_shape=[SDS((T, 512), F32), SDS((4, T, 256), F32)] + [SDS((4,) + wp.shape, wp.dtype) for wp in wps],
        scratch_shapes=_ag_sems(n) if n else [],
        compiler_params=_params(dimension_semantics=("arbitrary", "arbitrary")))(qs, ks, vs, tri, *wps)


def _sb_bwd(qs, ks, vs, do, carries, tri, tri_t):
    T = qs.shape[0]
    nq = T // SBQ

    def body(q_ref, k_ref, v_ref, do_ref, c_ref, tri_ref, trit_ref, dq_ref, dk_ref, dv_ref):
        i = pl.program_id(1)

        @pl.when(i == 0)
        def _():
            dk_ref[...] = jnp.zeros_like(dk_ref)
            dv_ref[...] = jnp.zeros_like(dv_ref)

        lane = lax.broadcasted_iota(jnp.int32, (1, 128), 1)
        clane = lax.broadcasted_iota(jnp.int32, (SBQ, 128), 1)
        strict = (lax.broadcasted_iota(jnp.int32, (SBQ, SBQ), 1) < lax.broadcasted_iota(jnp.int32, (SBQ, SBQ), 0))
        tri_v = tri_ref[...]
        trit_v = trit_ref[...]
        qv = q_ref[...]
        dov = do_ref[...].astype(BF16)
        first_head = lane < 64
        qhs = (jnp.where(first_head, qv, jnp.zeros_like(qv)), jnp.where(first_head, jnp.zeros_like(qv), qv))
        dohs = (jnp.where(first_head, dov, jnp.zeros_like(dov)), jnp.where(first_head, jnp.zeros_like(dov), dov))
        cts = (c_ref[0, :, 0:128], c_ref[0, :, 128:256])

        def block(j, pcarries, dqs, masked):
            rs = pl.ds(pl.multiple_of(j * SBQ, SBQ), SBQ)
            kb = k_ref[rs, :]
            vb = v_ref[rs, :]
            out_p, out_q = [], []
            dk = jnp.zeros((SBQ, 128), F32)
            dv = jnp.zeros((SBQ, 128), F32)
            for hh in range(2):
                carry = jnp.sum(jnp.where(clane == j, cts[hh], 0.0), axis=1, keepdims=True)
                lb, _, a = _sb_block_fwd(qhs[hh], kb, tri_v, carry, strict if masked else None)
                g = a * _dot_nt(dohs[hh], vb)
                p = _dot(g.astype(BF16), trit_v) + pcarries[hh]
                dz = g - (g + p) * jnp.exp(lb)
                if masked:
                    dz = jnp.where(strict, dz, 0.0)
                dzb = dz.astype(BF16)
                dk = dk + _dot_tn(dzb, qhs[hh])
                dv = dv + _dot_tn(a.astype(BF16), dohs[hh])
                out_p.append(pcarries[hh] + jnp.sum(g, axis=1, keepdims=True))
                out_q.append(dqs[hh] + _dot(dzb, kb))
            dk_ref[rs, :] += dk
            dv_ref[rs, :] += dv
            return out_p, out_q

        def step(j, state):
            (p0, p1), (q0, q1) = block(j, (state[0], state[1]), (state[2], state[3]), False)
            return p0, p1, q0, q1

        swept = jnp.max(jnp.where(clane == SB_COUNT_LANE, cts[0], 0.0)).astype(jnp.int32)
        first = i - jnp.clip(swept, 0, i)
        zero1 = jnp.zeros((SBQ, 1), F32)
        zero128 = jnp.zeros((SBQ, 128), F32)
        p0, p1, q0, q1 = lax.fori_loop(first, i, step, (zero1, zero1, zero128, zero128))
        _, (q0, q1) = block(i, (p0, p1), (q0, q1), True)
        dq_ref[...] = jnp.where(first_head, q0, q1)

    return pl.pallas_call(
        body, name="sb_bwd", grid=(4, nq),
        in_specs=[pl.BlockSpec((SBQ, 128), lambda h, i: (i, h)),
                  pl.BlockSpec((T, 128), lambda h, i: (0, h)),
                  pl.BlockSpec((T, 128), lambda h, i: (0, h)),
                  pl.BlockSpec((SBQ, 128), lambda h, i: (i, h)),
                  pl.BlockSpec((1, SBQ, 256), lambda h, i: (h, i, 0)),
                  pl.BlockSpec((SBQ, SBQ), lambda h, i: (0, 0)),
                  pl.BlockSpec((SBQ, SBQ), lambda h, i: (0, 0))],
        out_specs=[pl.BlockSpec((SBQ, 128), lambda h, i: (i, h)),
                   pl.BlockSpec((T, 128), lambda h, i: (0, h)),
                   pl.BlockSpec((T, 128), lambda h, i: (0, h))],
        out_shape=[SDS((T, 512), F32), SDS((T, 512), F32), SDS((T, 512), F32)],
        compiler_params=_params(dimension_semantics=("parallel", "arbitrary")))(qs, ks, vs, do, carries, tri, tri_t)


def _mem_fwd(mem, mem_norm_w, w_mkv, mk_norm_w):
    M = mem.shape[0]

    def body(mem_ref, wn_ref, w_ref, wk_ref, mn_ref, kraw_ref, k_ref, v_ref):
        mv = mem_ref[...]
        mn = (mv * _rsqrt_ms(mv) * wn_ref[...]).astype(BF16)
        mn_ref[...] = mn
        for s in range(2):
            cols = slice(512 * s, 512 * (s + 1))
            ks = _dot(mn, w_ref[s, :D, :])
            kraw_ref[:, cols] = ks
            v_ref[:, cols] = _dot(mn, w_ref[2 + s, :D, :]).astype(BF16)
            for h in range(2):
                x = ks[:, MEM_HD * h:MEM_HD * (h + 1)]
                k_ref[:, 512 * s + MEM_HD * h:512 * s + MEM_HD * (h + 1)] = (
                    x * _rsqrt_ms(x) * wk_ref[...]).astype(BF16)

    vm = pl.BlockSpec(memory_space=pltpu.VMEM)
    return pl.pallas_call(
        body, name="mem_fwd", in_specs=[vm] * 4, out_specs=[vm] * 4,
        out_shape=[SDS((M, D), BF16), SDS((M, D), F32), SDS((M, D), BF16), SDS((M, D), BF16)],
        compiler_params=_params())(mem, mem_norm_w, w_mkv, mk_norm_w)


def _xattn_fwd(qraw, k, v, wq):
    T = qraw.shape[0]
    M = k.shape[0]

    def body(q_ref, k_ref, v_ref, wq_ref, o_ref):
        for h in range(MEM_HEADS):
            sl = slice(MEM_HD * h, MEM_HD * (h + 1))
            x = q_ref[:, sl]
            q = (x * _rsqrt_ms(x) * wq_ref[...]).astype(BF16)
            s = _dot_nt(q, k_ref[:, sl]) * (1.0 / math.sqrt(MEM_HD))
            s = s - jnp.max(s, axis=-1, keepdims=True)
            e = jnp.exp(s)
            p = e / jnp.sum(e, axis=-1, keepdims=True)
            o_ref[:, sl] = _dot(p.astype(BF16), v_ref[:, sl]).astype(BF16)

    return pl.pallas_call(
        body, name="xattn_fwd", grid=(T // TB,),
        in_specs=[_row(D), _const((M, D)), _const((M, D)), _const((1, MEM_HD))],
        out_specs=_row(D), out_shape=SDS((T, D), BF16), compiler_params=_params())(qraw, k, v, wq)


def _xattn_bwd(qraw, k, v, wq, do):
    T = qraw.shape[0]
    M = k.shape[0]

    def body(q_ref, k_ref, v_ref, wq_ref, do_ref, dq_ref, dk_ref, dv_ref, dw_ref):
        @pl.when(pl.program_id(0) == 0)
        def _():
            dk_ref[...] = jnp.zeros_like(dk_ref)
            dv_ref[...] = jnp.zeros_like(dv_ref)
            dw_ref[...] = jnp.zeros_like(dw_ref)

        w = wq_ref[...]
        for h in range(MEM_HEADS):
            sl = slice(MEM_HD * h, MEM_HD * (h + 1))
            x = q_ref[:, sl]
            r = _rsqrt_ms(x)
            n = x * r
            q = (n * w).astype(BF16)
            kb = k_ref[:, sl]
            s = _dot_nt(q, kb) * (1.0 / math.sqrt(MEM_HD))
            s = s - jnp.max(s, axis=-1, keepdims=True)
            e = jnp.exp(s)
            p = e / jnp.sum(e, axis=-1, keepdims=True)
            dob = do_ref[:, sl].astype(BF16)
            dp = _dot_nt(dob, v_ref[:, sl])
            ds = (p * (dp - jnp.sum(dp * p, axis=-1, keepdims=True)) * (1.0 / math.sqrt(MEM_HD))).astype(BF16)
            dv_ref[:, sl] += _dot_tn(p.astype(BF16), dob)
            dk_ref[:, sl] += _dot_tn(ds, q)
            dqn = _dot(ds, kb)
            dn = dqn * w
            dq_ref[:, sl] = r * (dn - n * jnp.mean(dn * n, axis=-1, keepdims=True))
            dw_ref[...] += _colsum8(dqn * n)

    return pl.pallas_call(
        body, name="xattn_bwd", grid=(T // TB,),
        in_specs=[_row(D), _const((M, D)), _const((M, D)), _const((1, MEM_HD)), _row(D)],
        out_specs=[_row(D), _const((M, D)), _const((M, D)), _const((8, MEM_HD))],
        out_shape=[SDS((T, D), F32), SDS((M, D), F32), SDS((M, D), F32), SDS((8, MEM_HD), F32)],
        compiler_params=_params())(qraw, k, v, wq, do)


def _mem_bwd(mem, mem_norm_w, w_mkv, mk_norm_w, mem_n, k_raw, dk, dv):
    M = mem.shape[0]

    def body(mem_ref, wn_ref, w_ref, wk_ref, mn_ref, kraw_ref, dk_ref, dv_ref, dw_ref, dwn_ref, dwk_ref, dkv_scr):
        wk = wk_ref[...]
        dwk = jnp.zeros((8, MEM_HD), F32)
        for h in range(MEM_HEADS):
            sl = slice(MEM_HD * h, MEM_HD * (h + 1))
            x = kraw_ref[:, sl]
            r = _rsqrt_ms(x)
            n = x * r
            dkh = dk_ref[:, sl]
            dn = dkh * wk
            dkv_scr[:, sl] = (r * (dn - n * jnp.mean(dn * n, axis=-1, keepdims=True))).astype(BF16)
            dwk = dwk + _colsum8(dkh * n)
        dwk_ref[...] = dwk
        dkv_scr[:, D:] = dv_ref[...].astype(BF16)
        dkv = dkv_scr[...]
        mn = mn_ref[...]
        dmn = jnp.zeros((M, D), F32)
        for s in range(4):
            part = dkv[:, 512 * s:512 * (s + 1)]
            dw_ref[s] = _dot_tn(mn, part).astype(BF16)
            dmn = dmn + _dot_nt(part, w_ref[s, :D, :])
        mv = mem_ref[...]
        dwn_ref[...] = _colsum8(dmn * (mv * _rsqrt_ms(mv)))

    vm = pl.BlockSpec(memory_space=pltpu.VMEM)
    return pl.pallas_call(
        body, name="mem_bwd", in_specs=[vm] * 8, out_specs=[vm] * 3,
        out_shape=[SDS((4, D, 512), BF16), SDS((8, D), F32), SDS((8, MEM_HD), F32)],
        scratch_shapes=[pltpu.VMEM((M, 2 * D), BF16)],
        compiler_params=_params())(mem, mem_norm_w, w_mkv, mk_norm_w, mem_n, k_raw, dk, dv)


def _local_step(x, mem, tgt, w, later_weights=None, later_partials=None, last_partials=None):
    T = x.shape[0]
    wgk = jnp.zeros((128, 256), F32).at[:16].set(w["w_gk_up"].astype(F32))
    wg512 = jnp.tile(w["gla_norm_w"], (1, 4))
    ws512 = jnp.tile(w["sb_norm_w"], (1, 8))
    bd64 = _group_ones(512, 64)
    cm = _gla_consts()
    tri = _sb_tri()
    tri_t = tri.T

    proj, gk, qs, ks, vs, h1 = _matmul(
        x, w["w_in"], mode="nn", tm=TB, tn=DIN_P, name="mm_proj",
        epilogue=_norm_then_proj_split(w["mix_norm_w"], wgk, w["b_gk"], T, TB))
    o_g, states = _gla_fwd(proj, gk, cm)
    if later_weights is None:
        o_s, carries = _sb_fwd(qs, ks, vs, tri)
    else:
        o_s, carries, *gathered = _sb_fwd(qs, ks, vs, tri, later_weights[0])
        w = {**w, **later_weights[1](gathered)}
    cat = _mix_cat(o_g, proj, o_s, wg512, ws512, bd64)
    tm3 = min(2 * TB, T)
    x1, h2 = _matmul(cat, w["w_out"], mode="nn", tm=tm3, tn=D, res=x, name="mm_out",
                     epilogue=_then_norm_fwd(w["xattn_norm_w"], T, tm3))
    qraw = _matmul(h2, w["w_mq"], mode="nn", tm=tm3, tn=D, name="mm_mq")
    mem_n, k_raw, k_n, v_m = _mem_fwd(mem, w["mem_norm_w"], w["w_mkv"], w["mk_norm_w"])
    om = _xattn_fwd(qraw, k_n, v_m, w["mq_norm_w"])
    x2, h3 = _matmul(om, w["w_mo"], mode="nn", tm=tm3, tn=D, res=x1, name="mm_mo",
                     epilogue=_then_norm_fwd(w["ffn_norm_w"], T, tm3))
    gu, act = _gate_up_act(h3, w["w_gate_up"])
    dx3, loss_rows = _matmul(act, w["w_down"], mode="nn", tm=TB, tn=D, res=x2, name="mm_down",
                             epilogue=_then_loss(tgt, TB))

    g = {}
    dw_tk = min(DW_TK, T)
    dw = dict(mode="tn", tk=dw_tk, out_dtype=BF16)
    g["w_down"] = _matmul(act, dx3, tm=1408, tn=D, name="mm_dw_down", **dw)
    dgu = _down_bwd(dx3, w["w_down"], gu)
    g["w_gate_up"] = _matmul(
        h3, dgu, tm=D, tn=FF_TN, by_column_tile=True, mnk=(D, 2 * D_FF, T), name="mm_dw_gate_up",
        b_spec=pl.BlockSpec((None, dw_tk, FF_TN), lambda j, i, k: (j // 2, k, j % 2)), **dw)
    dx2, g["ffn_norm_w"] = _matmul(
        dgu, w["w_gate_up"], mode="nt", tm=tm3, tn=D, tk=FF_TN, mnk=(T, D, 2 * D_FF), name="mm_dh3",
        a_spec=pl.BlockSpec((None, tm3, FF_TN), lambda j, i, k: (k // 2, i, k % 2)),
        b_spec=pl.BlockSpec((None, D, FF_TN), lambda j, i, k: (k, 0, 0)),
        epilogue=_then_norm_bwd(x2, w["ffn_norm_w"], dx3, tm3))
    g["w_mo"] = _matmul(om, dx2, tm=D, tn=D, name="mm_dw_mo", **dw)
    dom = _matmul(dx2, w["w_mo"], mode="nt", tm=tm3, tn=D, name="mm_dom")
    dqraw, dk_n, dv_m, g["mq_norm_w"] = _xattn_bwd(qraw, k_n, v_m, w["mq_norm_w"], dom)
    g["w_mkv"], g["mem_norm_w"], g["mk_norm_w"] = _mem_bwd(
        mem, w["mem_norm_w"], w["w_mkv"], w["mk_norm_w"], mem_n, k_raw, dk_n, dv_m)
    g["w_mq"] = _matmul(h2, dqraw, tm=D, tn=D, name="mm_dw_mq", **dw)
    dx1, g["xattn_norm_w"] = _matmul(dqraw, w["w_mq"], mode="nt", tm=tm3, tn=D, name="mm_dh2",
                                     epilogue=_then_norm_bwd(x1, w["xattn_norm_w"], dx2, tm3))
    g["w_out"] = _matmul(cat, dx1, tm=D, tn=D, name="mm_dw_out", **dw)
    dcat = _matmul(dx1, w["w_out"], mode="nt", tm=tm3, tn=D, name="mm_dcat")
    do_g, dg_g, do_s, dwg, dws = _mix_cat_bwd(dcat, o_g, proj, o_s, wg512, ws512, bd64)
    dq_s, dk_s, dv_s = _sb_bwd(qs, ks, vs, do_s, carries, tri, tri_t)
    parts = () if later_partials is None else later_partials(g)
    dq_g, dk_g, dv_g, dgk, *received = _gla_bwd(proj, gk, cm, states, do_g, parts)
    dproj, dwgk, g["b_gk"] = _dproj_assemble(proj, dq_g, dk_g, dv_g, dg_g, dq_s, dk_s, dv_s, dgk, wgk, w["b_gk"])
    g["w_in"] = _matmul(h1, dproj, tm=D, tn=640, name="mm_dw_in", **dw)
    g["w_gk_up"] = dwgk[:16]
    last_parts = [] if last_partials is None else last_partials(g)
    grad_x, g["mix_norm_w"], *last_received = _matmul(
        dproj, w["w_in"], mode="nt", tm=TB, tn=D, name="mm_dh1",
        epilogue=_then_norm_bwd(x, w["mix_norm_w"], dx1, TB), exchange=last_parts)

    g["gla_norm_w"], g["sb_norm_w"] = dwg, dws
    if later_partials is None:
        g["gla_norm_w"] = dwg.reshape(8, 4, 128).sum(axis=1)
        g["sb_norm_w"] = dws.reshape(8, 8, 64).sum(axis=1)
        for n in SMALL:
            g[n] = jnp.sum(g[n], axis=0, keepdims=True)
        return jnp.sum(loss_rows), grad_x, g
    return jnp.sum(loss_rows), grad_x, g, (list(parts) + last_parts, list(received) + last_received)


def _mesh_pos():
    return lax.axis_index("x"), lax.axis_index("y"), lax.axis_index("c")


def _other_chips(x, y):
    return [(1 - x, y), (x, 1 - y), (1 - x, 1 - y)]


HBM = pl.BlockSpec(memory_space=pl.ANY)


def _ag_phases(w_refs, out_refs, send_sems, recv_sems):
    n = len(w_refs)
    x, y, c = _mesh_pos()
    me = 2 * x + y
    sibling = (x, y, 1 - c)
    chips = _other_chips(x, y)
    mine, theirs = c, 1 - c

    def copy(a, k, src, dst, to):
        return pltpu.make_async_remote_copy(src_ref=src, dst_ref=dst, send_sem=send_sems.at[6 * a + k],
                                            recv_sem=recv_sems.at[6 * a + k], device_id=to, device_id_type=MESH)

    def firsts():
        return [copy(a, k, w_refs[a].at[mine], out_refs[a].at[me, mine], (cx, cy, c))
                for a in range(n) for k, (cx, cy) in enumerate(chips)]

    def landed(a, k, half):
        cx, cy = chips[k]
        return out_refs[a].at[2 * cx + cy, half]

    def passes():
        return [copy(a, 3 + k, landed(a, k, mine), landed(a, k, mine), sibling) for a in range(n) for k in range(3)]

    def start():
        for cp in firsts():
            cp.start()

    def pass_on():
        for a in range(n):
            for k, (cx, cy) in enumerate(chips):
                copy(a, k, landed(a, k, mine), landed(a, k, mine), (cx, cy, c)).wait_recv()
                copy(a, 3 + k, landed(a, k, mine), landed(a, k, mine), sibling).start()

    def finish():
        for a in range(n):
            for k in range(3):
                copy(a, 3 + k, landed(a, k, theirs), landed(a, k, theirs), sibling).wait_recv()
        for cp in firsts() + passes():
            cp.wait_send()

    return start, pass_on, finish


def _ag_sems(n):
    return [pltpu.SemaphoreType.DMA((6 * n,)), pltpu.SemaphoreType.DMA((6 * n,))]


def _ag_weights(wps):
    n = len(wps)

    def body(*refs):
        for phase in _ag_phases(refs[:n], refs[n:2 * n], *refs[2 * n:]):
            phase()

    return pl.pallas_call(
        body, name="ag_weights", in_specs=[HBM] * n, out_specs=[HBM] * n,
        out_shape=[SDS((4,) + wp.shape, wp.dtype) for wp in wps], scratch_shapes=_ag_sems(n),
        compiler_params=pltpu.CompilerParams(has_side_effects=True))(*wps)


def _rs_swap_halves(gps, name):
    n = len(gps)

    def body(*refs):
        g_refs, out_refs = refs[:n], refs[n:2 * n]
        send_sems, recv_sems = refs[2 * n:]
        x, y, c = _mesh_pos()
        copies = [pltpu.make_async_remote_copy(
            src_ref=g_refs[a].at[:, 1 - c], dst_ref=out_refs[a], send_sem=send_sems.at[a], recv_sem=recv_sems.at[a],
            device_id=(x, y, 1 - c), device_id_type=MESH) for a in range(n)]
        for cp in copies:
            cp.start()
        for cp in copies:
            cp.wait()

    return pl.pallas_call(
        body, name=name, in_specs=[HBM] * n, out_specs=[HBM] * n,
        out_shape=[SDS((4,) + gp.shape[2:], gp.dtype) for gp in gps],
        scratch_shapes=[pltpu.SemaphoreType.DMA((n,)), pltpu.SemaphoreType.DMA((n,))],
        compiler_params=pltpu.CompilerParams(has_side_effects=True))(*gps)


def _rs_add_halves(gp, other, c_arr, name):
    h, w = other.shape[1:]

    def body(c_ref, a_ref, b_ref, o_ref):
        o_ref[...] = (a_ref[0].astype(F32) + b_ref[...].astype(F32)).astype(BF16)

    return pl.pallas_call(
        body, name=name,
        grid_spec=pltpu.PrefetchScalarGridSpec(
            num_scalar_prefetch=1, grid=(4,),
            in_specs=[pl.BlockSpec((1, 1, h, w), lambda s, c: (s, c[0], 0, 0)),
                      pl.BlockSpec((1, h, w), lambda s, c: (s, 0, 0))],
            out_specs=pl.BlockSpec((1, h, w), lambda s, c: (s, 0, 0))),
        out_shape=SDS((4, h, w), BF16), compiler_params=_params())(c_arr, gp, other)


def _rs_exchange_phases(p_refs, out_refs, send_sems, recv_sems):
    n = len(p_refs)
    x, y, c = _mesh_pos()
    me = 2 * x + y
    chips = _other_chips(x, y)

    def sends():
        return [pltpu.make_async_remote_copy(
            src_ref=p_refs[a].at[2 * cx + cy], dst_ref=out_refs[a].at[me], send_sem=send_sems.at[3 * a + k],
            recv_sem=recv_sems.at[3 * a + k], device_id=(cx, cy, c), device_id_type=MESH)
            for a in range(n) for k, (cx, cy) in enumerate(chips)]

    def start():
        for cp in sends():
            cp.start()

    def finish():
        for a in range(n):
            for k, (cx, cy) in enumerate(chips):
                slot = out_refs[a].at[2 * cx + cy]
                pltpu.make_async_remote_copy(
                    src_ref=slot, dst_ref=slot, send_sem=send_sems.at[3 * a + k], recv_sem=recv_sems.at[3 * a + k],
                    device_id=(cx, cy, c), device_id_type=MESH).wait_recv()
        for cp in sends():
            cp.wait_send()

    return start, finish


def _rs_exchange_sems(n):
    return [pltpu.SemaphoreType.DMA((3 * n,)), pltpu.SemaphoreType.DMA((3 * n,))]


def _rs_add_chips(recv, part, me_arr, name):
    h, w = part.shape[1:]
    th = h // 2 if (h // 2) % 16 == 0 else h

    def body(me_ref, r_ref, p_ref, o_ref):
        me = me_ref[0]
        total = None
        for k in range(4):
            term = jnp.where(me == k, p_ref[k], r_ref[k]).astype(F32)
            total = term if total is None else total + term
        o_ref[...] = total

    spec = pl.BlockSpec((4, th, w), lambda t, me: (0, t, 0))
    return pl.pallas_call(
        body, name=name,
        grid_spec=pltpu.PrefetchScalarGridSpec(
            num_scalar_prefetch=1, grid=(h // th,), in_specs=[spec, spec],
            out_specs=pl.BlockSpec((th, w), lambda t, me: (t, 0))),
        out_shape=SDS((h, w), F32), compiler_params=_params())(me_arr, recv, part)


def _rs_share(halves):
    n = len(halves)

    def body(*refs):
        h_refs, out_refs = refs[:n], refs[n:2 * n]
        send_sems, recv_sems = refs[2 * n:]
        x, y, c = _mesh_pos()
        copies = [pltpu.make_async_remote_copy(
            src_ref=h_refs[a], dst_ref=out_refs[a], send_sem=send_sems.at[a], recv_sem=recv_sems.at[a],
            device_id=(x, y, 1 - c), device_id_type=MESH) for a in range(n)]
        for cp in copies:
            cp.start()
        for cp in copies:
            cp.wait()

    return pl.pallas_call(
        body, name="rs_share", in_specs=[HBM] * n, out_specs=[HBM] * n,
        out_shape=[SDS(hs.shape, hs.dtype) for hs in halves],
        scratch_shapes=[pltpu.SemaphoreType.DMA((n,)), pltpu.SemaphoreType.DMA((n,))],
        compiler_params=pltpu.CompilerParams(has_side_effects=True))(*halves)


def _small_rows(partials):
    lanes = np.arange(512)
    fold = jnp.asarray((lanes[:, None] % 64 == np.arange(128)[None, :]).astype(np.float32), dtype=BF16)
    n_small = len(SMALL)

    def body(*refs):
        fold_ref, o_ref = refs[n_small], refs[n_small + 1]
        o_ref[...] = jnp.zeros_like(o_ref)
        for i, name in enumerate(SMALL):
            v = refs[i][...]
            if name == "gla_norm_w":
                v = v[:, 0:128] + v[:, 128:256] + v[:, 256:384] + v[:, 384:512]
            if name == "sb_norm_w":
                v = _dot_exact(v, fold_ref[...])
            n = SMALL_SIZES[name]
            o_ref[i:i + 1, 0:n] = jnp.sum(v, axis=0, keepdims=True)[:, 0:n]

    vm = pl.BlockSpec(memory_space=pltpu.VMEM)
    return pl.pallas_call(body, name="small_rows", in_specs=[vm] * (n_small + 1), out_specs=vm,
                          out_shape=SDS((SMALL_ROWS, 1024), F32))(*[partials[n] for n in SMALL], fold)


def _small_gather(s):
    def gather(s_ref, out_ref, send_sems, recv_sems, local_sem):
        x, y, c = _mesh_pos()
        me = 4 * x + 2 * y + c
        local = pltpu.make_async_copy(s_ref, out_ref.at[me], local_sem)
        local.start()
        peers = []
        for r in range(1, 8):
            px = 1 - x if r & 4 else x
            py = 1 - y if r & 2 else y
            pc = 1 - c if r & 1 else c
            peers.append((px, py, pc))
        sends = []
        for k, peer in enumerate(peers):
            cp = pltpu.make_async_remote_copy(
                src_ref=s_ref, dst_ref=out_ref.at[me], send_sem=send_sems.at[k], recv_sem=recv_sems.at[k],
                device_id=peer, device_id_type=MESH)
            cp.start()
            sends.append(cp)
        for k, (px, py, pc) in enumerate(peers):
            slot = out_ref.at[4 * px + 2 * py + pc]
            pltpu.make_async_remote_copy(
                src_ref=slot, dst_ref=slot, send_sem=send_sems.at[k], recv_sem=recv_sems.at[k],
                device_id=(px, py, pc), device_id_type=MESH).wait_recv()
        for cp in sends:
            cp.wait_send()
        local.wait()

    return pl.pallas_call(
        gather, name="small_gather", in_specs=[HBM], out_specs=HBM,
        out_shape=SDS((8, SMALL_ROWS, 1024), F32),
        scratch_shapes=[pltpu.SemaphoreType.DMA((7,)), pltpu.SemaphoreType.DMA((7,)), pltpu.SemaphoreType.DMA],
        compiler_params=pltpu.CompilerParams(has_side_effects=True))(s)


def _adamw_update(w, g, m, v):
    mn = ADAM_B1 * m + (1.0 - ADAM_B1) * g
    vn = ADAM_B2 * v + (1.0 - ADAM_B2) * (g * g)
    c1 = 1.0 - ADAM_B1 ** ADAM_STEP
    c2 = 1.0 - ADAM_B2 ** ADAM_STEP
    return -ADAM_LR * ((mn / c1) / (jnp.sqrt(vn / c2) + ADAM_EPS) + ADAM_WD * w), mn, vn


def _small_update(parts, w, m, v):
    n_small = len(SMALL)

    def body(p_ref, *refs):
        ins, outs = refs[:3 * n_small], refs[3 * n_small:]
        total = p_ref[0]
        for k in range(1, 8):
            total = total + p_ref[k]
        for i, name in enumerate(SMALL):
            g = total[i:i + 1, 0:SMALL_SIZES[name]]
            d, mn, vn = _adamw_update(ins[i][...], g, ins[n_small + i][...], ins[2 * n_small + i][...])
            for slot, val in enumerate((g, d, mn, vn)):
                outs[slot * n_small + i][...] = val

    vm = pl.BlockSpec(memory_space=pltpu.VMEM)
    shapes = [SDS((1, SMALL_SIZES[n]), F32) for n in SMALL]
    outs = pl.pallas_call(
        body, name="small_adamw", in_specs=[vm] * (1 + 3 * n_small), out_specs=[vm] * (4 * n_small),
        out_shape=shapes * 4)(parts, *[w[n] for n in SMALL], *[m[n] for n in SMALL], *[v[n] for n in SMALL])
    return [dict(zip(SMALL, outs[s * n_small:(s + 1) * n_small])) for s in range(4)]


def _adamw(w, g, m, v, name):
    rows, cols = w.shape
    tr = rows
    for cand in (512, 352, 256):
        if rows > cand and rows % cand == 0:
            tr = cand
            break

    def body(w_ref, g_ref, m_ref, v_ref, d_ref, mo_ref, vo_ref):
        d_ref[...], mo_ref[...], vo_ref[...] = _adamw_update(w_ref[...], g_ref[...], m_ref[...], v_ref[...])

    spec = pl.BlockSpec((tr, cols), lambda i: (i, 0))
    return pl.pallas_call(
        body, name=name, grid=(rows // tr,), in_specs=[spec] * 4, out_specs=[spec] * 3,
        out_shape=[SDS((rows, cols), F32)] * 3, compiler_params=_params())(w, g, m, v)


SMALL_SIZES = {"mix_norm_w": 1024, "b_gk": 256, "gla_norm_w": 128, "sb_norm_w": 64, "xattn_norm_w": 1024,
               "mem_norm_w": 1024, "mq_norm_w": 256, "mk_norm_w": 256, "ffn_norm_w": 1024}


ROWS_OF = (("w_out", 256), ("w_mq", 256), ("w_mo", 256), ("w_down", 704))
WIN_ROWS = 1056
LATER = ("rows", "gate_up", "mkv")


def _shard_buffers(d, dtype):
    rows = jnp.concatenate([d[n] for n, _ in ROWS_OF], axis=0).astype(dtype)
    gk = jnp.pad(d["w_gk_up"], ((0, WIN_ROWS - D - 16), (0, DIN // 4 - 64)))
    win = jnp.concatenate([d["w_in"], gk], axis=0).astype(dtype)
    return [rows, d["w_gate_up"].astype(dtype), d["w_mkv"].astype(dtype)], win


def _in_halves(a):
    return a.reshape(a.shape[:-2] + (2, a.shape[-2] // 2, a.shape[-1]))


def _whole(a):
    return a.reshape(a.shape[:-3] + (2 * a.shape[-2], a.shape[-1]))


def _first_weights(win):
    w_in = win[:, :D].transpose(1, 0, 2).reshape(D, DIN)
    w_in = jnp.concatenate([w_in[:, :1536], w_in[:, 1552:], w_in[:, 1536:1552],
                            jnp.zeros((D, DIN_P - DIN), w_in.dtype)], axis=1)
    return {"w_in": w_in, "w_gk_up": win[:, D:D + 16, :64].transpose(1, 0, 2).reshape(16, 256)}


def _later_weights(rows, gate_up, mkv):
    out, off = {"w_gate_up": gate_up, "w_mkv": mkv}, 0
    for n, r in ROWS_OF:
        out[n] = rows[:, off:off + r].reshape(4 * r, 1024)
        off += r
    return out


def _later_grad_buffers(g):
    rows = jnp.concatenate([g[n].reshape(4, r, 1024) for n, r in ROWS_OF], axis=1)
    return [rows, g["w_gate_up"], g["w_mkv"]]


def _win_grad_buffer(g):
    gk = g["w_gk_up"].astype(BF16).reshape(16, 4, 64).transpose(1, 0, 2)
    gk = jnp.pad(gk, ((0, 0), (0, WIN_ROWS - D - 16), (0, DIN // 4 - 64)))
    gi = g["w_in"]
    gi = jnp.concatenate([gi[:, :1536], gi[:, C_LR:C_LR + 16], gi[:, 1536:C_LR]], axis=1)
    return jnp.concatenate([gi.reshape(D, 4, DIN // 4).transpose(1, 0, 2), gk], axis=1)


def _shard_grads(rows, gate_up, mkv, win):
    out, off = {"w_gate_up": gate_up, "w_mkv": mkv, "w_in": win[:D], "w_gk_up": win[D:D + 16, :64]}, 0
    for n, r in ROWS_OF:
        out[n] = rows[off:off + r]
        off += r
    return out


def kernel(x, mem, mix_norm_w, w_in, w_gk_up, b_gk, gla_norm_w, sb_norm_w, w_out, xattn_norm_w, mem_norm_w, w_mq, w_mkv, mq_norm_w, mk_norm_w, w_mo, ffn_norm_w, w_gate_up, w_down, loss_target, m_mix_norm_w, m_w_in, m_w_gk_up, m_b_gk, m_gla_norm_w, m_sb_norm_w, m_w_out, m_xattn_norm_w, m_mem_norm_w, m_w_mq, m_w_mkv, m_mq_norm_w, m_mk_norm_w, m_w_mo, m_ffn_norm_w, m_w_gate_up, m_w_down, v_mix_norm_w, v_w_in, v_w_gk_up, v_b_gk, v_gla_norm_w, v_sb_norm_w, v_w_out, v_xattn_norm_w, v_mem_norm_w, v_w_mq, v_w_mkv, v_mq_norm_w, v_mk_norm_w, v_w_mo, v_ffn_norm_w, v_w_gate_up, v_w_down):
    args = dict(locals())
    wts = {n: args[n][0] if n in BIG else args[n] for n in WEIGHTS}
    mom = {n: args["m_" + n][0] if n in BIG else args["m_" + n] for n in WEIGHTS}
    var = {n: args["v_" + n][0] if n in BIG else args["v_" + n] for n in WEIGHTS}

    c = lax.axis_index("c")
    chip = 2 * lax.axis_index("x") + lax.axis_index("y")
    c_arr = c.astype(jnp.int32).reshape(1)
    chip_arr = chip.astype(jnp.int32).reshape(1)
    def own_slot_filled(gathered, mine):
        return [_whole(lax.dynamic_update_slice(got, wp[None], (chip, 0, 0, 0))) for got, wp in zip(gathered, mine)]

    def chip_partials(names, buffers, tag):
        gps = [_in_halves(b) for b in buffers]
        return [_rs_add_halves(gp, other, c_arr, "rs_add_halves_" + n)
                for n, gp, other in zip(names, gps, _rs_swap_halves(gps, "rs_swap_halves_" + tag))]

    later_wps, win_wp = _shard_buffers(wts, BF16)
    later_wps, win_wp = [_in_halves(b) for b in later_wps], _in_halves(win_wp)
    first = _first_weights(*own_slot_filled(_ag_weights([win_wp]), [win_wp]))
    first.update({n: wts[n] for n in SMALL})

    loss, grad_x, g, (parts, received) = _local_step(
        x[0], mem[0], loss_target[0], first,
        later_weights=(later_wps, lambda gathered: _later_weights(*own_slot_filled(gathered, later_wps))),
        later_partials=lambda g: chip_partials(LATER, _later_grad_buffers(g), "later"),
        last_partials=lambda g: chip_partials(("win",), [_win_grad_buffer(g)], "win"))
    loss = lax.psum(loss, ("x", "y", "c"))

    mine = [_rs_add_chips(recv, part, chip_arr, "rs_add_chips_" + n)
            for n, recv, part in zip(LATER + ("win",), received, parts)]
    totals = [jnp.concatenate([jnp.where(c == 0, m, t), jnp.where(c == 0, t, m)], axis=0)
              for m, t in zip(mine, _rs_share(mine))]
    grads = _shard_grads(*totals)

    small_grads, delta, new_m, new_v = _small_update(_small_gather(_small_rows(g)), wts, mom, var)
    grads.update(small_grads)
    for n in BIG:
        w2 = wts[n].reshape(-1, wts[n].shape[-1])
        d_, m_, v_ = _adamw(w2, grads[n].reshape(w2.shape), mom[n].reshape(w2.shape), var[n].reshape(w2.shape),
                            "adamw_" + n)
        delta[n], new_m[n], new_v[n] = (t.reshape((1,) + wts[n].shape) for t in (d_, m_, v_))
        grads[n] = grads[n].reshape((1,) + wts[n].shape)

    return (loss, grad_x[None], *[grads[n] for n in WEIGHTS], *[delta[n] for n in WEIGHTS],
            *[new_m[n] for n in WEIGHTS], *[new_v[n] for n in WEIGHTS])
```

```python
import functools
import math

import numpy as np
import jax
import jax.numpy as jnp
from jax import lax
from jax.experimental import pallas as pl
from jax.experimental.pallas import tpu as pltpu

F32 = jnp.float32
BF16 = jnp.bfloat16
SDS = jax.ShapeDtypeStruct
MESH = pl.DeviceIdType.MESH

D = 1024
EPS = 1e-6
D_FF = 2816
GLA_GATE_NORM = 16.0
GLA_C = 64
MEM_HEADS = 4
MEM_HD = 256
C_QG, C_KG, C_VG, C_GG, C_QS, C_KS, C_VS, C_LR = 0, 256, 512, 1024, 1536, 2048, 2560, 3072
DIN = 3088
DIN_P = 3200
TB = 512
SBQ = 256
VMEM_LIMIT = 56 * 1024 * 1024
HIGHEST = lax.Precision.HIGHEST

ADAM_LR, ADAM_B1, ADAM_B2, ADAM_EPS, ADAM_WD, ADAM_STEP = 0.001, 0.9, 0.999, 1e-08, 0.01, 10

BIG = ("w_in", "w_gk_up", "w_out", "w_mq", "w_mkv", "w_mo", "w_gate_up", "w_down")
SMALL = ("mix_norm_w", "b_gk", "gla_norm_w", "sb_norm_w", "xattn_norm_w", "mem_norm_w", "mq_norm_w",
         "mk_norm_w", "ffn_norm_w")
WEIGHTS = ("mix_norm_w", "w_in", "w_gk_up", "b_gk", "gla_norm_w", "sb_norm_w", "w_out", "xattn_norm_w",
           "mem_norm_w", "w_mq", "w_mkv", "mq_norm_w", "mk_norm_w", "w_mo", "ffn_norm_w", "w_gate_up", "w_down")
SMALL_ROWS = 16


def _params(**kw):
    return pltpu.CompilerParams(vmem_limit_bytes=VMEM_LIMIT, **kw)


def _row(c, j=0):
    return pl.BlockSpec((TB, c), lambda i, j=j: (i, j))


def _const(shape):
    return pl.BlockSpec(shape, lambda i: (0,) * len(shape))


def _dot(a, b):
    return lax.dot_general(a, b, (((1,), (0,)), ((), ())), preferred_element_type=F32)


def _dot_nt(a, b):
    return lax.dot_general(a, b, (((1,), (1,)), ((), ())), preferred_element_type=F32)


def _dot_tn(a, b):
    return lax.dot_general(a, b, (((0,), (0,)), ((), ())), preferred_element_type=F32)


def _dot_nt_f32(a, b):
    return lax.dot_general(a, b, (((1,), (1,)), ((), ())), precision=HIGHEST, preferred_element_type=F32)


def _split3(x):
    h = x.astype(BF16)
    r = x - h.astype(F32)
    m = r.astype(BF16)
    l = (r - m.astype(F32)).astype(BF16)
    return h, m, l


def _dot_exact(x, ones_mat):
    h, m, l = _split3(x)
    return _dot(h, ones_mat) + _dot(m, ones_mat) + _dot(l, ones_mat)


def _softplus(z):
    return jnp.maximum(z, 0.0) + jnp.log1p(jnp.exp(-jnp.abs(z)))


def _rsqrt_ms(x):
    return lax.rsqrt(jnp.mean(x * x, axis=-1, keepdims=True) + EPS)


def _colsum8(x):
    r, c = x.shape
    return jnp.sum(x.reshape(r // 8, 8, c), axis=0)


def _matmul(a, b, *, mode, tm, tn, tk=None, res=None, out_dtype=F32, by_column_tile=False, a_spec=None,
            b_spec=None, mnk=None, epilogue=None, exchange=None, name):
    if mnk is not None:
        M, N, K = mnk
    else:
        K, M = a.shape if mode == "tn" else a.shape[::-1]
        N = b.shape[0] if mode == "nt" else b.shape[1]
    tk = K if tk is None else tk
    assert M % tm == 0 and N % tn == 0 and K % tk == 0, (name, M, N, K, tm, tn, tk)
    nk = K // tk
    if a_spec is None:
        if mode == "tn":
            a_spec = pl.BlockSpec((tk, tm), lambda j, i, k: (k, i))
        else:
            a_spec = pl.BlockSpec((tm, tk), lambda j, i, k: (i, k))
    if b_spec is None:
        if mode == "nt":
            b_spec = pl.BlockSpec((tn, tk), lambda j, i, k: (j, k))
        else:
            b_spec = pl.BlockSpec((tk, tn), lambda j, i, k: (k, j))
    if by_column_tile:
        assert res is None
        o_spec = pl.BlockSpec((None, tm, tn), lambda j, i, k: (j, i, 0))
        o_shape = SDS((N // tn, M, tn), out_dtype)
    else:
        o_spec = pl.BlockSpec((tm, tn), lambda j, i, k: (i, j))
        o_shape = SDS((M, N), out_dtype)
    dot = {"nn": _dot, "nt": _dot_nt, "tn": _dot_tn}[mode]
    has_res = res is not None
    extra_in, extra_out, finish, *lhs_from = epilogue if epilogue is not None else ((), (), None)
    n_in, n_out = 2 + has_res + len(extra_in), 1 + len(extra_out)
    sent = list(exchange) if exchange is not None else []
    ns = len(sent)
    grid = (N // tn, M // tm, nk)

    def body(*refs):
        a_ref, b_ref = refs[0], refs[1]
        res_ref = refs[2] if has_res else None
        in_refs, out_refs = refs[2 + has_res:n_in], refs[n_in + ns:n_in + ns + n_out]
        first_row_tile = pl.program_id(1) == 0
        scratch = refs[n_in + 2 * ns + n_out:]
        if ns:
            ids = [pl.program_id(d) for d in range(3)]
            rs_start, rs_finish = _rs_exchange_phases(
                refs[n_in:n_in + ns], refs[n_in + ns + n_out:n_in + 2 * ns + n_out], *scratch[nk > 1:])
            pl.when((ids[0] == 0) & (ids[1] == 0) & (ids[2] == 0))(rs_start)

        def done(t):
            if has_res:
                t = t + res_ref[...]
            if finish is None:
                out_refs[0][...] = t.astype(out_dtype)
            else:
                finish(t, first_row_tile, in_refs, out_refs)

        lhs = lhs_from[0](a_ref[...], in_refs, out_refs) if lhs_from else a_ref[...].astype(BF16)
        p = dot(lhs, b_ref[...].astype(BF16))
        if nk == 1:
            done(p)
        else:
            acc_ref = scratch[0]
            k = pl.program_id(2)

            @pl.when(k == 0)
            def _():
                acc_ref[...] = p

            @pl.when(k > 0)
            def _():
                acc_ref[...] += p

            @pl.when(k == nk - 1)
            def _():
                done(acc_ref[...])

        if ns:
            pl.when((ids[0] == grid[0] - 1) & (ids[1] == grid[1] - 1) & (ids[2] == grid[2] - 1))(rs_finish)

    in_specs = [a_spec, b_spec] + ([o_spec] if has_res else []) + [s for _, s in extra_in] + [HBM] * ns
    args = (a, b) + ((res,) if has_res else ()) + tuple(x for x, _ in extra_in) + tuple(sent)
    outs = pl.pallas_call(
        body, name=name, grid=grid, in_specs=in_specs,
        out_specs=[o_spec] + [s for _, s in extra_out] + [HBM] * ns,
        out_shape=[o_shape] + [s for s, _ in extra_out] + [SDS(p.shape, p.dtype) for p in sent],
        scratch_shapes=([pltpu.VMEM((tm, tn), F32)] if nk > 1 else []) + (_rs_exchange_sems(ns) if ns else []),
        compiler_params=_params(
            dimension_semantics=("arbitrary",) * 3 if ns else ("parallel", "parallel", "arbitrary")),
    )(*args)
    return outs[0] if epilogue is None and not ns else outs


def _full_row(tm, c):
    return pl.BlockSpec((tm, c), lambda j, i, k: (i, 0))


def _kept(shape):
    return pl.BlockSpec(shape, lambda j, i, k: (0,) * len(shape))


def _then_norm_fwd(w, T, tm):
    dm = w.shape[1]

    def finish(t, first, ins, outs):
        outs[0][...] = t
        outs[1][...] = (t * _rsqrt_ms(t) * ins[0][...]).astype(BF16)

    return [(w, _kept((1, dm)))], [(SDS((T, dm), BF16), _full_row(tm, dm))], finish


def _then_norm_bwd(x, w, dres, tm):
    T, dm = x.shape

    def finish(t, first, ins, outs):
        x_ref, w_ref, dres_ref = ins

        @pl.when(first)
        def _():
            outs[1][...] = jnp.zeros_like(outs[1])

        xv = x_ref[...]
        r = _rsqrt_ms(xv)
        n = xv * r
        dn = t * w_ref[...]
        outs[0][...] = dres_ref[...] + r * (dn - n * jnp.mean(dn * n, axis=-1, keepdims=True))
        outs[1][...] += _colsum8(t * n)

    return ([(x, _full_row(tm, dm)), (w, _kept((1, dm))), (dres, _full_row(tm, dm))],
            [(SDS((8, dm), F32), _kept((8, dm)))], finish)


def _then_loss(tgt, tm):
    def finish(t, first, ins, outs):
        @pl.when(first)
        def _():
            outs[1][...] = jnp.zeros_like(outs[1])

        e = t - ins[0][...]
        outs[0][...] = e * (1.0 / D)
        outs[1][...] += _colsum8(e * e) * (0.5 / D)

    return [(tgt, _full_row(tm, D))], [(SDS((8, D), F32), _kept((8, D)))], finish


def _norm_then_proj_split(norm_w, wgk, bgk, T, tm):
    def lhs_from(xv, ins, outs):
        h = (xv * _rsqrt_ms(xv) * ins[2][...]).astype(BF16)
        outs[5][...] = h
        return h

    def finish(t, first, ins, outs):
        wgk_ref, b_ref = ins[:2]
        proj_ref, gk_ref, qs_ref, ks_ref, vs_ref = outs[:5]
        proj_ref[...] = t
        u = _dot(t[:, C_LR:DIN_P].astype(BF16), wgk_ref[...].astype(BF16)) + b_ref[...]
        gk_ref[...] = -_softplus(-u) / GLA_GATE_NORM
        qs_ref[...] = (t[:, C_QS:C_KS] * 0.125).astype(BF16)
        ks_ref[...] = t[:, C_KS:C_VS].astype(BF16)
        vs_ref[...] = t[:, C_VS:C_LR].astype(BF16)

    return ([(wgk, _kept((128, 256))), (bgk, _kept((1, 256))), (norm_w, _kept((1, D)))],
            [(SDS((T, 256), F32), _full_row(tm, 256))] + [(SDS((T, 512), BF16), _full_row(tm, 512))] * 3
            + [(SDS((T, D), BF16), _full_row(tm, D))], finish, lhs_from)


def _dproj_assemble(proj, dq_g, dk_g, dv_g, dg_g, dq_s, dk_s, dv_s, dgk, wgk, bgk):
    T = proj.shape[0]

    def body(lr_ref, dqg_ref, dkg_ref, dvg_ref, dgg_ref, dqs_ref, dks_ref, dvs_ref, dgk_ref, wgk_ref, b_ref,
             dp_ref, dwgk_ref, dbgk_ref):
        @pl.when(pl.program_id(0) == 0)
        def _():
            dwgk_ref[...] = jnp.zeros_like(dwgk_ref)
            dbgk_ref[...] = jnp.zeros_like(dbgk_ref)

        lr = lr_ref[...].astype(BF16)
        wg = wgk_ref[...].astype(BF16)
        u = _dot(lr, wg) + b_ref[...]
        du = dgk_ref[...] * (jax.nn.sigmoid(-u) / GLA_GATE_NORM)
        dub = du.astype(BF16)
        dp_ref[:, C_QG:C_KG] = (dqg_ref[...] * 0.125).astype(BF16)
        dp_ref[:, C_KG:C_VG] = dkg_ref[...].astype(BF16)
        dp_ref[:, C_VG:C_GG] = dvg_ref[...].astype(BF16)
        dp_ref[:, C_GG:C_QS] = dgg_ref[...].astype(BF16)
        dp_ref[:, C_QS:C_KS] = (dqs_ref[...] * 0.125).astype(BF16)
        dp_ref[:, C_KS:C_VS] = dks_ref[...].astype(BF16)
        dp_ref[:, C_VS:C_LR] = dvs_ref[...].astype(BF16)
        dp_ref[:, C_LR:DIN_P] = _dot_nt(dub, wg).astype(BF16)
        dwgk_ref[...] += _dot_tn(lr, dub)
        dbgk_ref[...] += _colsum8(du)

    return pl.pallas_call(
        body, name="dproj_assemble", grid=(T // TB,),
        in_specs=[_row(128, C_LR // 128), _row(256), _row(256), _row(512), _row(512), _row(512), _row(512),
                  _row(512), _row(256), _const((128, 256)), _const((1, 256))],
        out_specs=[_row(DIN_P), _const((128, 256)), _const((8, 256))],
        out_shape=[SDS((T, DIN_P), BF16), SDS((128, 256), F32), SDS((8, 256), F32)],
        compiler_params=_params())(proj, dq_g, dk_g, dv_g, dg_g, dq_s, dk_s, dv_s, dgk, wgk, bgk)


def _group_ones(n, g):
    idx = np.arange(n) // g
    return jnp.asarray((idx[:, None] == idx[None, :]).astype(np.float32), dtype=BF16)


def _mix_cat(o_g, proj, o_s, wg512, ws512, bd64):
    T = o_g.shape[0]

    def body(og_ref, gg_ref, os_ref, wg_ref, ws_ref, bd_ref, cat_ref):
        og = og_ref[...]
        gg = gg_ref[...]
        s = gg * jax.nn.sigmoid(gg)
        for h in range(4):
            sl = slice(128 * h, 128 * (h + 1))
            x = og[:, sl]
            cat_ref[:, sl] = (x * _rsqrt_ms(x) * wg_ref[:, sl] * s[:, sl]).astype(BF16)
        osv = os_ref[...]
        ms = _dot_exact(osv * osv, bd_ref[...]) * (1.0 / 64.0)
        cat_ref[:, 512:1024] = (osv * lax.rsqrt(ms + EPS) * ws_ref[...]).astype(BF16)

    return pl.pallas_call(
        body, name="mix_cat", grid=(T // TB,),
        in_specs=[_row(512), _row(512, C_GG // 512), _row(512), _const((1, 512)), _const((1, 512)),
                  _const((512, 512))],
        out_specs=_row(1024), out_shape=SDS((T, 1024), BF16), compiler_params=_params())(
            o_g, proj, o_s, wg512, ws512, bd64)


def _mix_cat_bwd(dcat, o_g, proj, o_s, wg512, ws512, bd64):
    T = o_g.shape[0]

    def body(dc_ref, og_ref, gg_ref, os_ref, wg_ref, ws_ref, bd_ref, dog_ref, dgg_ref, dos_ref, dwg_ref, dws_ref):
        @pl.when(pl.program_id(0) == 0)
        def _():
            dwg_ref[...] = jnp.zeros_like(dwg_ref)
            dws_ref[...] = jnp.zeros_like(dws_ref)

        og = og_ref[...]
        gg = gg_ref[...]
        sg = jax.nn.sigmoid(gg)
        s = gg * sg
        ds = sg * (1.0 + gg * (1.0 - sg))
        for h in range(4):
            sl = slice(128 * h, 128 * (h + 1))
            x = og[:, sl]
            r = _rsqrt_ms(x)
            n = x * r
            w = wg_ref[:, sl]
            dc = dc_ref[:, sl]
            dy = dc * s[:, sl]
            dgg_ref[:, sl] = dc * (n * w) * ds[:, sl]
            dn = dy * w
            dog_ref[:, sl] = r * (dn - n * jnp.mean(dn * n, axis=-1, keepdims=True))
            dwg_ref[:, sl] += _colsum8(dy * n)
        osv = os_ref[...]
        bd = bd_ref[...]
        r = lax.rsqrt(_dot_exact(osv * osv, bd) * (1.0 / 64.0) + EPS)
        n = osv * r
        dc = dc_ref[:, 512:1024]
        dn = dc * ws_ref[...]
        dos_ref[...] = r * (dn - n * (_dot_exact(dn * n, bd) * (1.0 / 64.0)))
        dws_ref[...] += _colsum8(dc * n)

    return pl.pallas_call(
        body, name="mix_cat_bwd", grid=(T // TB,),
        in_specs=[_row(1024), _row(512), _row(512, C_GG // 512), _row(512), _const((1, 512)), _const((1, 512)),
                  _const((512, 512))],
        out_specs=[_row(512), _row(512), _row(512), _const((8, 512)), _const((8, 512))],
        out_shape=[SDS((T, 512), F32), SDS((T, 512), F32), SDS((T, 512), F32), SDS((8, 512), F32),
                   SDS((8, 512), F32)],
        compiler_params=_params())(dcat, o_g, proj, o_s, wg512, ws512, bd64)


FF_TN = 1408
DW_TK = 2048


def _gate_up_act(h, w):
    T = h.shape[0]
    nj = D_FF // FF_TN

    def body(h_ref, wg_ref, wu_ref, gu_ref, a_ref):
        hv = h_ref[...]
        g = _dot(hv, wg_ref[...])
        u = _dot(hv, wu_ref[...])
        gu_ref[0] = g.astype(BF16)
        gu_ref[1] = u.astype(BF16)
        a_ref[...] = (g * jax.nn.sigmoid(g) * u).astype(BF16)

    return pl.pallas_call(
        body, name="mm_gate_up_act", grid=(nj, T // TB),
        in_specs=[pl.BlockSpec((TB, D), lambda j, i: (i, 0)),
                  pl.BlockSpec((None, D, FF_TN), lambda j, i: (j, 0, 0)),
                  pl.BlockSpec((None, D, FF_TN), lambda j, i: (nj + j, 0, 0))],
        out_specs=[pl.BlockSpec((2, TB, FF_TN), lambda j, i: (0, i, j)),
                   pl.BlockSpec((TB, FF_TN), lambda j, i: (i, j))],
        out_shape=[SDS((2, T, D_FF), BF16), SDS((T, D_FF), BF16)],
        compiler_params=_params(dimension_semantics=("parallel", "parallel")))(h, w, w)


def _down_bwd(dy, w_down, gu):
    T = dy.shape[0]
    nj = D_FF // FF_TN

    def body(dy_ref, w_ref, gu_ref, dgu_ref):
        da = _dot_nt(dy_ref[...].astype(BF16), w_ref[...])
        g = gu_ref[0].astype(F32)
        sg = jax.nn.sigmoid(g)
        dgu_ref[0] = (da * gu_ref[1].astype(F32) * (sg * (1.0 + g * (1.0 - sg)))).astype(BF16)
        dgu_ref[1] = (da * (g * sg)).astype(BF16)

    return pl.pallas_call(
        body, name="mm_down_bwd", grid=(nj, T // TB),
        in_specs=[pl.BlockSpec((TB, D), lambda j, i: (i, 0)),
                  pl.BlockSpec((FF_TN, D), lambda j, i: (j, 0)),
                  pl.BlockSpec((2, TB, FF_TN), lambda j, i: (0, i, j))],
        out_specs=pl.BlockSpec((2, TB, FF_TN), lambda j, i: (0, i, j)),
        out_shape=SDS((2, T, D_FF), BF16),
        compiler_params=_params(dimension_semantics=("parallel", "parallel")))(dy, w_down, gu)


def _gla_consts():
    c = GLA_C
    L = np.tril(np.ones((c, c), np.float32))
    blocks = [L, L[(np.arange(c) // 16) * 16]]
    blocks += [np.repeat(L[16 * i:16 * i + 1], c, axis=0) for i in range(4)]
    blocks.append(np.repeat(L[c - 1:c], c, axis=0))
    return jnp.asarray(np.concatenate(blocks, axis=0))


@jax.custom_vjp
def _gla_lin(cm, g):
    cb = cm.astype(BF16)
    h, m, l = _split3(g)
    y = _dot(cb, h) + _dot(cb, m) + _dot(cb, l)
    return tuple(y[GLA_C * n:GLA_C * (n + 1)] for n in range(7))


def _gla_lin_fwd(cm, g):
    return _gla_lin(cm, g), cm


def _gla_lin_bwd(cm, cts):
    cb = cm.astype(BF16)
    h, m, l = _split3(jnp.concatenate(cts, axis=0))
    return None, _dot_tn(cb, h) + _dot_tn(cb, m) + _dot_tn(cb, l)


_gla_lin.defvjp(_gla_lin_fwd, _gla_lin_bwd)


GLA_SUB = 16
GLA_H = 4
GLA_W = 64 * GLA_H


def _head_lanes(rows, h):
    lane = lax.broadcasted_iota(jnp.int32, (rows, GLA_W), 1)
    return (lane >= 64 * h) & (lane < 64 * (h + 1))


def _gla_scores_terms(qs, k, b, rs, blk):
    row = lax.broadcasted_iota(jnp.int32, (GLA_C, GLA_W), 0)
    keep = row < GLA_SUB * (blk + 1)
    e = jnp.where(keep, jnp.exp(jnp.where(keep, rs[blk] - b, 0.0)), 0.0)
    qb = qs[GLA_SUB * blk:GLA_SUB * (blk + 1)]
    lhs = jnp.concatenate([jnp.where(_head_lanes(GLA_SUB, h), qb, 0.0) for h in range(GLA_H)], axis=0)
    return lhs, e


@jax.custom_vjp
def _gla_scores(qs, k, b, r0, r1, r2, r3):
    rs = (r0, r1, r2, r3)
    per_head = [[] for _ in range(GLA_H)]
    for blk in range(GLA_C // GLA_SUB):
        lhs, e = _gla_scores_terms(qs, k, b, rs, blk)
        a = _dot_nt_f32(lhs, k * e)
        for h in range(GLA_H):
            per_head[h].append(a[GLA_SUB * h:GLA_SUB * (h + 1)])
    return tuple(jnp.concatenate(p, axis=0) for p in per_head)


def _gla_scores_fwd(qs, k, b, r0, r1, r2, r3):
    return _gla_scores(qs, k, b, r0, r1, r2, r3), (qs, k, b, r0, r1, r2, r3)


def _gla_scores_bwd(saved, cts):
    qs, k, b = saved[:3]
    rs = saved[3:]
    dqs, drs = [], []
    dk = jnp.zeros_like(k)
    db = jnp.zeros_like(b)
    for blk in range(GLA_C // GLA_SUB):
        lhs, e = _gla_scores_terms(qs, k, b, rs, blk)
        rows = slice(GLA_SUB * blk, GLA_SUB * (blk + 1))
        da = jnp.concatenate([ct[rows] for ct in cts], axis=0)
        dlhs = lax.dot_general(da, k * e, (((1,), (0,)), ((), ())), precision=HIGHEST, preferred_element_type=F32)
        dq = jnp.zeros((GLA_SUB, GLA_W), F32)
        for h in range(GLA_H):
            dq = jnp.where(_head_lanes(GLA_SUB, h), dlhs[GLA_SUB * h:GLA_SUB * (h + 1)], dq)
        dqs.append(dq)
        dks = lax.dot_general(da, lhs, (((0,), (0,)), ((), ())), precision=HIGHEST, preferred_element_type=F32)
        dk = dk + dks * e
        darg = dks * (k * e)
        db = db - darg
        drs.append(darg)
    return (jnp.concatenate(dqs, axis=0), dk, db, *drs)


_gla_scores.defvjp(_gla_scores_fwd, _gla_scores_bwd)


def _gla_chunk(cm, q, k, g, v0, v1, v2, v3, st):
    c = GLA_C
    vs = (v0, v1, v2, v3)
    ri = lax.broadcasted_iota(jnp.int32, (c, c), 0)
    ci = lax.broadcasted_iota(jnp.int32, (c, c), 1)
    causal = ci <= ri
    b, r, r0, r1, r2, r3, bl = _gla_lin(cm, g)
    scores = _gla_scores(q * jnp.exp(b - r), k, b, r0, r1, r2, r3)
    qe = q * jnp.exp(b)
    kd = k * jnp.exp(bl - b)
    st_new = st * jnp.exp(jnp.concatenate([bl, bl], axis=0))
    outs = []
    for h in range(GLA_H):
        a = jnp.where(causal, scores[h], 0.0)
        outs.append(_dot_nt(jnp.where(_head_lanes(c, h), qe, 0.0), st) + _dot(a, vs[h]))
        st_new = st_new + jnp.where(_head_lanes(2 * c, h), _dot_tn(vs[h], kd), 0.0)
    return (*outs, st_new)


GLA_TB = 512


def _gla_fwd(proj, gk, cm):
    T = proj.shape[0]
    nb = T // GLA_TB
    nc = GLA_TB // GLA_C

    def body(q_ref, k_ref, v_ref, g_ref, cm_ref, o_ref, st_ref, st_scr):
        @pl.when(pl.program_id(0) == 0)
        def _():
            st_scr[...] = jnp.zeros_like(st_scr)

        cmv = cm_ref[...]

        def chunk(ci, carry):
            rs = pl.ds(pl.multiple_of(ci * GLA_C, GLA_C), GLA_C)
            st = st_scr[...]
            st_ref[ci] = st
            *outs, st_new = _gla_chunk(cmv, q_ref[rs, :] * 0.125, k_ref[rs, :], g_ref[rs, :],
                                       *[v_ref[rs, 128 * h:128 * (h + 1)] for h in range(GLA_H)], st)
            for h in range(GLA_H):
                o_ref[rs, 128 * h:128 * (h + 1)] = outs[h]
            st_scr[...] = st_new
            return carry

        lax.fori_loop(0, nc, chunk, 0, unroll=True)

    return pl.pallas_call(
        body, name="gla_fwd", grid=(nb,),
        in_specs=[pl.BlockSpec((GLA_TB, 256), lambda i: (i, C_QG // 256)),
                  pl.BlockSpec((GLA_TB, 256), lambda i: (i, C_KG // 256)),
                  pl.BlockSpec((GLA_TB, 512), lambda i: (i, C_VG // 512)),
                  pl.BlockSpec((GLA_TB, 256), lambda i: (i, 0)),
                  pl.BlockSpec((7 * GLA_C, GLA_C), lambda i: (0, 0))],
        out_specs=[pl.BlockSpec((GLA_TB, 512), lambda i: (i, 0)),
                   pl.BlockSpec((nc, 128, GLA_W), lambda i: (i, 0, 0))],
        out_shape=[SDS((T, 512), F32), SDS((T // GLA_C, 128, GLA_W), F32)],
        scratch_shapes=[pltpu.VMEM((128, GLA_W), F32)],
        compiler_params=_params(dimension_semantics=("arbitrary",)))(proj, proj, proj, gk, cm)


def _gla_bwd(proj, gk, cm, states, do, parts=()):
    T = proj.shape[0]
    nb = T // GLA_TB
    nc = GLA_TB // GLA_C
    n = len(parts)

    def body(q_ref, k_ref, v_ref, g_ref, cm_ref, st_ref, do_ref, *rest):
        p_refs, (dq_ref, dk_ref, dv_ref, dg_ref), out_refs = rest[:n], rest[n:n + 4], rest[n + 4:2 * n + 4]
        dst_scr = rest[2 * n + 4]
        step = pl.program_id(0)
        if n:
            rs_start, rs_finish = _rs_exchange_phases(p_refs, out_refs, *rest[2 * n + 5:])

        @pl.when(step == 0)
        def _():
            dst_scr[...] = jnp.zeros_like(dst_scr)
            if n:
                rs_start()

        cmv = cm_ref[...]

        def chunk(t, carry):
            ci = nc - 1 - t
            rs = pl.ds(pl.multiple_of(ci * GLA_C, GLA_C), GLA_C)
            _, vjp = jax.vjp(
                functools.partial(_gla_chunk, cmv), q_ref[rs, :] * 0.125, k_ref[rs, :], g_ref[rs, :],
                *[v_ref[rs, 128 * h:128 * (h + 1)] for h in range(GLA_H)], st_ref[ci])
            dq, dk, dg, *dvs, dst = vjp((*[do_ref[rs, 128 * h:128 * (h + 1)] for h in range(GLA_H)], dst_scr[...]))
            dq_ref[rs, :] = dq
            dk_ref[rs, :] = dk
            dg_ref[rs, :] = dg
            for h in range(GLA_H):
                dv_ref[rs, 128 * h:128 * (h + 1)] = dvs[h]
            dst_scr[...] = dst
            return carry

        lax.fori_loop(0, nc, chunk, 0, unroll=True)

        if n:
            pl.when(step == nb - 1)(rs_finish)

    rev = lambda i: nb - 1 - i
    return pl.pallas_call(
        body, name="gla_bwd", grid=(nb,),
        in_specs=[pl.BlockSpec((GLA_TB, 256), lambda i: (rev(i), C_QG // 256)),
                  pl.BlockSpec((GLA_TB, 256), lambda i: (rev(i), C_KG // 256)),
                  pl.BlockSpec((GLA_TB, 512), lambda i: (rev(i), C_VG // 512)),
                  pl.BlockSpec((GLA_TB, 256), lambda i: (rev(i), 0)),
                  pl.BlockSpec((7 * GLA_C, GLA_C), lambda i: (0, 0)),
                  pl.BlockSpec((nc, 128, GLA_W), lambda i: (rev(i), 0, 0)),
                  pl.BlockSpec((GLA_TB, 512), lambda i: (rev(i), 0))] + [HBM] * n,
        out_specs=[pl.BlockSpec((GLA_TB, 256), lambda i: (rev(i), 0)),
                   pl.BlockSpec((GLA_TB, 256), lambda i: (rev(i), 0)),
                   pl.BlockSpec((GLA_TB, 512), lambda i: (rev(i), 0)),
                   pl.BlockSpec((GLA_TB, 256), lambda i: (rev(i), 0))] + [HBM] * n,
        out_shape=[SDS((T, 256), F32), SDS((T, 256), F32), SDS((T, 512), F32), SDS((T, 256), F32)]
        + [SDS(p.shape, p.dtype) for p in parts],
        scratch_shapes=[pltpu.VMEM((128, GLA_W), F32)] + (_rs_exchange_sems(n) if n else []),
        compiler_params=_params(dimension_semantics=("arbitrary",)))(proj, proj, proj, gk, cm, states, do, *parts)


SB_DEAD = 105.0
SB_COUNT_LANE = 127


def _sb_tri():
    i = np.arange(SBQ)
    return jnp.asarray((i[:, None] > i[None, :]).astype(np.float32), dtype=BF16)


def _sb_block_fwd(qh, kb, tri, carry, strict):
    z = _dot_nt(qh, kb)
    sp = _softplus(z)
    l1 = -sp
    if strict is not None:
        l1 = jnp.where(strict, l1, 0.0)
    log_a = (z - sp) + _dot(l1.astype(BF16), tri) + carry
    a = jnp.exp(log_a)
    if strict is not None:
        a = jnp.where(strict, a, 0.0)
    return z - sp, l1, a


def _sb_fwd(qs, ks, vs, tri, wps=()):
    T = qs.shape[0]
    nq = T // SBQ
    n = len(wps)

    def body(q_ref, k_ref, v_ref, tri_ref, *rest):
        w_refs, (o_ref, c_ref), out_refs = rest[:n], rest[n:n + 2], rest[n + 2:2 * n + 2]
        pair = pl.program_id(0)
        i = pl.program_id(1)
        if n:
            ag_start, ag_pass_on, ag_finish = _ag_phases(w_refs, out_refs, *rest[2 * n + 2:])
            pl.when((pair == 0) & (i == 0))(ag_start)
        lane = lax.broadcasted_iota(jnp.int32, (1, 128), 1)
        clane = lax.broadcasted_iota(jnp.int32, (SBQ, 128), 1)
        strict = (lax.broadcasted_iota(jnp.int32, (SBQ, SBQ), 1) < lax.broadcasted_iota(jnp.int32, (SBQ, SBQ), 0))
        tri_v = tri_ref[...]
        qv = q_ref[...]
        first_head = lane < 64
        qhs = (jnp.where(first_head, qv, jnp.zeros_like(qv)), jnp.where(first_head, jnp.zeros_like(qv), qv))

        def block(j, carries, accs, masked):
            rs = pl.ds(pl.multiple_of(j * SBQ, SBQ), SBQ)
            kb = k_ref[rs, :]
            vb = v_ref[rs, :]
            out_c, out_a = [], []
            for hh in range(2):
                _, l1, a = _sb_block_fwd(qhs[hh], kb, tri_v, carries[hh], strict if masked else None)
                out_a.append(accs[hh] + _dot(a.astype(BF16), vb))
                out_c.append(carries[hh] + jnp.sum(l1, axis=1, keepdims=True))
            return out_c, out_a

        zero1 = jnp.zeros((SBQ, 1), F32)
        zero128 = jnp.zeros((SBQ, 128), F32)
        (c0, c1), (a0, a1) = block(i, (zero1, zero1), (zero128, zero128), True)

        def more(state):
            return (state[0] <= i) & (jnp.maximum(jnp.max(state[1]), jnp.max(state[2])) > -SB_DEAD)

        def step(state):
            jj, c0, c1, a0, a1, t0, t1 = state
            j = i - jj
            t0 = jnp.where(clane == j, c0, t0)
            t1 = jnp.where(clane == j, c1, t1)
            (c0, c1), (a0, a1) = block(j, (c0, c1), (a0, a1), False)
            return jj + 1, c0, c1, a0, a1, t0, t1

        jj, c0, c1, a0, a1, t0, t1 = lax.while_loop(
            more, step, (jnp.int32(1), c0, c1, a0, a1, zero128, zero128))
        o_ref[...] = jnp.where(first_head, a0, a1)
        swept = (jj - 1).astype(F32)
        c_ref[0, :, 0:128] = jnp.where(clane == SB_COUNT_LANE, swept, t0)
        c_ref[0, :, 128:256] = jnp.where(clane == SB_COUNT_LANE, swept, t1)
        if n:
            pl.when((pair == 3) & (i == 0))(ag_pass_on)
            pl.when((pair == 3) & (i == nq - 1))(ag_finish)

    return pl.pallas_call(
        body, name="sb_fwd", grid=(4, nq),
        in_specs=[pl.BlockSpec((SBQ, 128), lambda h, i: (i, h)),
                  pl.BlockSpec((T, 128), lambda h, i: (0, h)),
                  pl.BlockSpec((T, 128), lambda h, i: (0, h)),
                  pl.BlockSpec((SBQ, SBQ), lambda h, i: (0, 0))] + [HBM] * n,
        out_specs=[pl.BlockSpec((SBQ, 128), lambda h, i: (i, h)),
                   pl.BlockSpec((1, SBQ, 256), lambda h, i: (h, i, 0))] + [HBM] * n,
        out_shape=[SDS((T, 512), F32), SDS((4, T, 256), F32)] + [SDS((4,) + wp.shape, wp.dtype) for wp in wps],
        scratch_shapes=_ag_sems(n) if n else [],
        compiler_params=_params(dimension_semantics=("arbitrary", "arbitrary")))(qs, ks, vs, tri, *wps)


def _sb_bwd(qs, ks, vs, do, carries, tri, tri_t):
    T = qs.shape[0]
    nq = T // SBQ

    def body(q_ref, k_ref, v_ref, do_ref, c_ref, tri_ref, trit_ref, dq_ref, dk_ref, dv_ref):
        i = pl.program_id(1)

        @pl.when(i == 0)
        def _():
            dk_ref[...] = jnp.zeros_like(dk_ref)
            dv_ref[...] = jnp.zeros_like(dv_ref)

        lane = lax.broadcasted_iota(jnp.int32, (1, 128), 1)
        clane = lax.broadcasted_iota(jnp.int32, (SBQ, 128), 1)
        strict = (lax.broadcasted_iota(jnp.int32, (SBQ, SBQ), 1) < lax.broadcasted_iota(jnp.int32, (SBQ, SBQ), 0))
        tri_v = tri_ref[...]
        trit_v = trit_ref[...]
        qv = q_ref[...]
        dov = do_ref[...].astype(BF16)
        first_head = lane < 64
        qhs = (jnp.where(first_head, qv, jnp.zeros_like(qv)), jnp.where(first_head, jnp.zeros_like(qv), qv))
        dohs = (jnp.where(first_head, dov, jnp.zeros_like(dov)), jnp.where(first_head, jnp.zeros_like(dov), dov))
        cts = (c_ref[0, :, 0:128], c_ref[0, :, 128:256])

        def block(j, pcarries, dqs, masked):
            rs = pl.ds(pl.multiple_of(j * SBQ, SBQ), SBQ)
            kb = k_ref[rs, :]
            vb = v_ref[rs, :]
            out_p, out_q = [], []
            dk = jnp.zeros((SBQ, 128), F32)
            dv = jnp.zeros((SBQ, 128), F32)
            for hh in range(2):
                carry = jnp.sum(jnp.where(clane == j, cts[hh], 0.0), axis=1, keepdims=True)
                lb, _, a = _sb_block_fwd(qhs[hh], kb, tri_v, carry, strict if masked else None)
                g = a * _dot_nt(dohs[hh], vb)
                p = _dot(g.astype(BF16), trit_v) + pcarries[hh]
                dz = g - (g + p) * jnp.exp(lb)
                if masked:
                    dz = jnp.where(strict, dz, 0.0)
                dzb = dz.astype(BF16)
                dk = dk + _dot_tn(dzb, qhs[hh])
                dv = dv + _dot_tn(a.astype(BF16), dohs[hh])
                out_p.append(pcarries[hh] + jnp.sum(g, axis=1, keepdims=True))
                out_q.append(dqs[hh] + _dot(dzb, kb))
            dk_ref[rs, :] += dk
            dv_ref[rs, :] += dv
            return out_p, out_q

        def step(j, state):
            (p0, p1), (q0, q1) = block(j, (state[0], state[1]), (state[2], state[3]), False)
            return p0, p1, q0, q1

        swept = jnp.max(jnp.where(clane == SB_COUNT_LANE, cts[0], 0.0)).astype(jnp.int32)
        first = i - jnp.clip(swept, 0, i)
        zero1 = jnp.zeros((SBQ, 1), F32)
        zero128 = jnp.zeros((SBQ, 128), F32)
        p0, p1, q0, q1 = lax.fori_loop(first, i, step, (zero1, zero1, zero128, zero128))
        _, (q0, q1) = block(i, (p0, p1), (q0, q1), True)
        dq_ref[...] = jnp.where(first_head, q0, q1)

    return pl.pallas_call(
        body, name="sb_bwd", grid=(4, nq),
        in_specs=[pl.BlockSpec((SBQ, 128), lambda h, i: (i, h)),
                  pl.BlockSpec((T, 128), lambda h, i: (0, h)),
                  pl.BlockSpec((T, 128), lambda h, i: (0, h)),
                  pl.BlockSpec((SBQ, 128), lambda h, i: (i, h)),
                  pl.BlockSpec((1, SBQ, 256), lambda h, i: (h, i, 0)),
                  pl.BlockSpec((SBQ, SBQ), lambda h, i: (0, 0)),
                  pl.BlockSpec((SBQ, SBQ), lambda h, i: (0, 0))],
        out_specs=[pl.BlockSpec((SBQ, 128), lambda h, i: (i, h)),
                   pl.BlockSpec((T, 128), lambda h, i: (0, h)),
                   pl.BlockSpec((T, 128), lambda h, i: (0, h))],
        out_shape=[SDS((T, 512), F32), SDS((T, 512), F32), SDS((T, 512), F32)],
        compiler_params=_params(dimension_semantics=("parallel", "arbitrary")))(qs, ks, vs, do, carries, tri, tri_t)


def _mem_fwd(mem, mem_norm_w, w_mkv, mk_norm_w):
    M = mem.shape[0]

    def body(mem_ref, wn_ref, w_ref, wk_ref, mn_ref, kraw_ref, k_ref, v_ref):
        mv = mem_ref[...]
        mn = (mv * _rsqrt_ms(mv) * wn_ref[...]).astype(BF16)
        mn_ref[...] = mn
        for s in range(2):
            cols = slice(512 * s, 512 * (s + 1))
            ks = _dot(mn, w_ref[s, :D, :])
            kraw_ref[:, cols] = ks
            v_ref[:, cols] = _dot(mn, w_ref[2 + s, :D, :]).astype(BF16)
            for h in range(2):
                x = ks[:, MEM_HD * h:MEM_HD * (h + 1)]
                k_ref[:, 512 * s + MEM_HD * h:512 * s + MEM_HD * (h + 1)] = (
                    x * _rsqrt_ms(x) * wk_ref[...]).astype(BF16)

    vm = pl.BlockSpec(memory_space=pltpu.VMEM)
    return pl.pallas_call(
        body, name="mem_fwd", in_specs=[vm] * 4, out_specs=[vm] * 4,
        out_shape=[SDS((M, D), BF16), SDS((M, D), F32), SDS((M, D), BF16), SDS((M, D), BF16)],
        compiler_params=_params())(mem, mem_norm_w, w_mkv, mk_norm_w)


def _xattn_fwd(qraw, k, v, wq):
    T = qraw.shape[0]
    M = k.shape[0]

    def body(q_ref, k_ref, v_ref, wq_ref, o_ref):
        for h in range(MEM_HEADS):
            sl = slice(MEM_HD * h, MEM_HD * (h + 1))
            x = q_ref[:, sl]
            q = (x * _rsqrt_ms(x) * wq_ref[...]).astype(BF16)
            s = _dot_nt(q, k_ref[:, sl]) * (1.0 / math.sqrt(MEM_HD))
            s = s - jnp.max(s, axis=-1, keepdims=True)
            e = jnp.exp(s)
            p = e / jnp.sum(e, axis=-1, keepdims=True)
            o_ref[:, sl] = _dot(p.astype(BF16), v_ref[:, sl]).astype(BF16)

    return pl.pallas_call(
        body, name="xattn_fwd", grid=(T // TB,),
        in_specs=[_row(D), _const((M, D)), _const((M, D)), _const((1, MEM_HD))],
        out_specs=_row(D), out_shape=SDS((T, D), BF16), compiler_params=_params())(qraw, k, v, wq)


def _xattn_bwd(qraw, k, v, wq, do):
    T = qraw.shape[0]
    M = k.shape[0]

    def body(q_ref, k_ref, v_ref, wq_ref, do_ref, dq_ref, dk_ref, dv_ref, dw_ref):
        @pl.when(pl.program_id(0) == 0)
        def _():
            dk_ref[...] = jnp.zeros_like(dk_ref)
            dv_ref[...] = jnp.zeros_like(dv_ref)
            dw_ref[...] = jnp.zeros_like(dw_ref)

        w = wq_ref[...]
        for h in range(MEM_HEADS):
            sl = slice(MEM_HD * h, MEM_HD * (h + 1))
            x = q_ref[:, sl]
            r = _rsqrt_ms(x)
            n = x * r
            q = (n * w).astype(BF16)
            kb = k_ref[:, sl]
            s = _dot_nt(q, kb) * (1.0 / math.sqrt(MEM_HD))
            s = s - jnp.max(s, axis=-1, keepdims=True)
            e = jnp.exp(s)
            p = e / jnp.sum(e, axis=-1, keepdims=True)
            dob = do_ref[:, sl].astype(BF16)
            dp = _dot_nt(dob, v_ref[:, sl])
            ds = (p * (dp - jnp.sum(dp * p, axis=-1, keepdims=True)) * (1.0 / math.sqrt(MEM_HD))).astype(BF16)
            dv_ref[:, sl] += _dot_tn(p.astype(BF16), dob)
            dk_ref[:, sl] += _dot_tn(ds, q)
            dqn = _dot(ds, kb)
            dn = dqn * w
            dq_ref[:, sl] = r * (dn - n * jnp.mean(dn * n, axis=-1, keepdims=True))
            dw_ref[...] += _colsum8(dqn * n)

    return pl.pallas_call(
        body, name="xattn_bwd", grid=(T // TB,),
        in_specs=[_row(D), _const((M, D)), _const((M, D)), _const((1, MEM_HD)), _row(D)],
        out_specs=[_row(D), _const((M, D)), _const((M, D)), _const((8, MEM_HD))],
        out_shape=[SDS((T, D), F32), SDS((M, D), F32), SDS((M, D), F32), SDS((8, MEM_HD), F32)],
        compiler_params=_params())(qraw, k, v, wq, do)


def _mem_bwd(mem, mem_norm_w, w_mkv, mk_norm_w, mem_n, k_raw, dk, dv):
    M = mem.shape[0]

    def body(mem_ref, wn_ref, w_ref, wk_ref, mn_ref, kraw_ref, dk_ref, dv_ref, dw_ref, dwn_ref, dwk_ref, dkv_scr):
        wk = wk_ref[...]
        dwk = jnp.zeros((8, MEM_HD), F32)
        for h in range(MEM_HEADS):
            sl = slice(MEM_HD * h, MEM_HD * (h + 1))
            x = kraw_ref[:, sl]
            r = _rsqrt_ms(x)
            n = x * r
            dkh = dk_ref[:, sl]
            dn = dkh * wk
            dkv_scr[:, sl] = (r * (dn - n * jnp.mean(dn * n, axis=-1, keepdims=True))).astype(BF16)
            dwk = dwk + _colsum8(dkh * n)
        dwk_ref[...] = dwk
        dkv_scr[:, D:] = dv_ref[...].astype(BF16)
        dkv = dkv_scr[...]
        mn = mn_ref[...]
        dmn = jnp.zeros((M, D), F32)
        for s in range(4):
            part = dkv[:, 512 * s:512 * (s + 1)]
            dw_ref[s] = _dot_tn(mn, part).astype(BF16)
            dmn = dmn + _dot_nt(part, w_ref[s, :D, :])
        mv = mem_ref[...]
        dwn_ref[...] = _colsum8(dmn * (mv * _rsqrt_ms(mv)))

    vm = pl.BlockSpec(memory_space=pltpu.VMEM)
    return pl.pallas_call(
        body, name="mem_bwd", in_specs=[vm] * 8, out_specs=[vm] * 3,
        out_shape=[SDS((4, D, 512), BF16), SDS((8, D), F32), SDS((8, MEM_HD), F32)],
        scratch_shapes=[pltpu.VMEM((M, 2 * D), BF16)],
        compiler_params=_params())(mem, mem_norm_w, w_mkv, mk_norm_w, mem_n, k_raw, dk, dv)


def _local_step(x, mem, tgt, w, later_weights=None, later_partials=None, last_partials=None):
    T = x.shape[0]
    wgk = jnp.zeros((128, 256), F32).at[:16].set(w["w_gk_up"].astype(F32))
    wg512 = jnp.tile(w["gla_norm_w"], (1, 4))
    ws512 = jnp.tile(w["sb_norm_w"], (1, 8))
    bd64 = _group_ones(512, 64)
    cm = _gla_consts()
    tri = _sb_tri()
    tri_t = tri.T

    proj, gk, qs, ks, vs, h1 = _matmul(
        x, w["w_in"], mode="nn", tm=TB, tn=DIN_P, name="mm_proj",
        epilogue=_norm_then_proj_split(w["mix_norm_w"], wgk, w["b_gk"], T, TB))
    o_g, states = _gla_fwd(proj, gk, cm)
    if later_weights is None:
        o_s, carries = _sb_fwd(qs, ks, vs, tri)
    else:
        o_s, carries, *gathered = _sb_fwd(qs, ks, vs, tri, later_weights[0])
        w = {**w, **later_weights[1](gathered)}
    cat = _mix_cat(o_g, proj, o_s, wg512, ws512, bd64)
    tm3 = min(2 * TB, T)
    x1, h2 = _matmul(cat, w["w_out"], mode="nn", tm=tm3, tn=D, res=x, name="mm_out",
                     epilogue=_then_norm_fwd(w["xattn_norm_w"], T, tm3))
    qraw = _matmul(h2, w["w_mq"], mode="nn", tm=tm3, tn=D, name="mm_mq")
    mem_n, k_raw, k_n, v_m = _mem_fwd(mem, w["mem_norm_w"], w["w_mkv"], w["mk_norm_w"])
    om = _xattn_fwd(qraw, k_n, v_m, w["mq_norm_w"])
    x2, h3 = _matmul(om, w["w_mo"], mode="nn", tm=tm3, tn=D, res=x1, name="mm_mo",
                     epilogue=_then_norm_fwd(w["ffn_norm_w"], T, tm3))
    gu, act = _gate_up_act(h3, w["w_gate_up"])
    dx3, loss_rows = _matmul(act, w["w_down"], mode="nn", tm=TB, tn=D, res=x2, name="mm_down",
                             epilogue=_then_loss(tgt, TB))

    g = {}
    dw_tk = min(DW_TK, T)
    dw = dict(mode="tn", tk=dw_tk, out_dtype=BF16)
    g["w_down"] = _matmul(act, dx3, tm=1408, tn=D, name="mm_dw_down", **dw)
    dgu = _down_bwd(dx3, w["w_down"], gu)
    g["w_gate_up"] = _matmul(
        h3, dgu, tm=D, tn=FF_TN, by_column_tile=True, mnk=(D, 2 * D_FF, T), name="mm_dw_gate_up",
        b_spec=pl.BlockSpec((None, dw_tk, FF_TN), lambda j, i, k: (j // 2, k, j % 2)), **dw)
    dx2, g["ffn_norm_w"] = _matmul(
        dgu, w["w_gate_up"], mode="nt", tm=tm3, tn=D, tk=FF_TN, mnk=(T, D, 2 * D_FF), name="mm_dh3",
        a_spec=pl.BlockSpec((None, tm3, FF_TN), lambda j, i, k: (k // 2, i, k % 2)),
        b_spec=pl.BlockSpec((None, D, FF_TN), lambda j, i, k: (k, 0, 0)),
        epilogue=_then_norm_bwd(x2, w["ffn_norm_w"], dx3, tm3))
    g["w_mo"] = _matmul(om, dx2, tm=D, tn=D, name="mm_dw_mo", **dw)
    dom = _matmul(dx2, w["w_mo"], mode="nt", tm=tm3, tn=D, name="mm_dom")
    dqraw, dk_n, dv_m, g["mq_norm_w"] = _xattn_bwd(qraw, k_n, v_m, w["mq_norm_w"], dom)
    g["w_mkv"], g["mem_norm_w"], g["mk_norm_w"] = _mem_bwd(
        mem, w["mem_norm_w"], w["w_mkv"], w["mk_norm_w"], mem_n, k_raw, dk_n, dv_m)
    g["w_mq"] = _matmul(h2, dqraw, tm=D, tn=D, name="mm_dw_mq", **dw)
    dx1, g["xattn_norm_w"] = _matmul(dqraw, w["w_mq"], mode="nt", tm=tm3, tn=D, name="mm_dh2",
                                     epilogue=_then_norm_bwd(x1, w["xattn_norm_w"], dx2, tm3))
    g["w_out"] = _matmul(cat, dx1, tm=D, tn=D, name="mm_dw_out", **dw)
    dcat = _matmul(dx1, w["w_out"], mode="nt", tm=tm3, tn=D, name="mm_dcat")
    do_g, dg_g, do_s, dwg, dws = _mix_cat_bwd(dcat, o_g, proj, o_s, wg512, ws512, bd64)
    dq_s, dk_s, dv_s = _sb_bwd(qs, ks, vs, do_s, carries, tri, tri_t)
    parts = () if later_partials is None else later_partials(g)
    dq_g, dk_g, dv_g, dgk, *received = _gla_bwd(proj, gk, cm, states, do_g, parts)
    dproj, dwgk, g["b_gk"] = _dproj_assemble(proj, dq_g, dk_g, dv_g, dg_g, dq_s, dk_s, dv_s, dgk, wgk, w["b_gk"])
    g["w_in"] = _matmul(h1, dproj, tm=D, tn=640, name="mm_dw_in", **dw)
    g["w_gk_up"] = dwgk[:16]
    last_parts = [] if last_partials is None else last_partials(g)
    grad_x, g["mix_norm_w"], *last_received = _matmul(
        dproj, w["w_in"], mode="nt", tm=TB, tn=D, name="mm_dh1",
        epilogue=_then_norm_bwd(x, w["mix_norm_w"], dx1, TB), exchange=last_parts)

    g["gla_norm_w"], g["sb_norm_w"] = dwg, dws
    if later_partials is None:
        g["gla_norm_w"] = dwg.reshape(8, 4, 128).sum(axis=1)
        g["sb_norm_w"] = dws.reshape(8, 8, 64).sum(axis=1)
        for n in SMALL:
            g[n] = jnp.sum(g[n], axis=0, keepdims=True)
        return jnp.sum(loss_rows), grad_x, g
    return jnp.sum(loss_rows), grad_x, g, (list(parts) + last_parts, list(received) + last_received)


def _mesh_pos():
    return lax.axis_index("x"), lax.axis_index("y"), lax.axis_index("c")


def _other_chips(x, y):
    return [(1 - x, y), (x, 1 - y), (1 - x, 1 - y)]


HBM = pl.BlockSpec(memory_space=pl.ANY)


def _ag_phases(w_refs, out_refs, send_sems, recv_sems):
    n = len(w_refs)
    x, y, c = _mesh_pos()
    me = 2 * x + y
    sibling = (x, y, 1 - c)
    chips = _other_chips(x, y)
    mine, theirs = c, 1 - c

    def copy(a, k, src, dst, to):
        return pltpu.make_async_remote_copy(src_ref=src, dst_ref=dst, send_sem=send_sems.at[6 * a + k],
                                            recv_sem=recv_sems.at[6 * a + k], device_id=to, device_id_type=MESH)

    def firsts():
        return [copy(a, k, w_refs[a].at[mine], out_refs[a].at[me, mine], (cx, cy, c))
                for a in range(n) for k, (cx, cy) in enumerate(chips)]

    def landed(a, k, half):
        cx, cy = chips[k]
        return out_refs[a].at[2 * cx + cy, half]

    def passes():
        return [copy(a, 3 + k, landed(a, k, mine), landed(a, k, mine), sibling) for a in range(n) for k in range(3)]

    def start():
        for cp in firsts():
            cp.start()

    def pass_on():
        for a in range(n):
            for k, (cx, cy) in enumerate(chips):
                copy(a, k, landed(a, k, mine), landed(a, k, mine), (cx, cy, c)).wait_recv()
                copy(a, 3 + k, landed(a, k, mine), landed(a, k, mine), sibling).start()

    def finish():
        for a in range(n):
            for k in range(3):
                copy(a, 3 + k, landed(a, k, theirs), landed(a, k, theirs), sibling).wait_recv()
        for cp in firsts() + passes():
            cp.wait_send()

    return start, pass_on, finish


def _ag_sems(n):
    return [pltpu.SemaphoreType.DMA((6 * n,)), pltpu.SemaphoreType.DMA((6 * n,))]


def _ag_weights(wps):
    n = len(wps)

    def body(*refs):
        for phase in _ag_phases(refs[:n], refs[n:2 * n], *refs[2 * n:]):
            phase()

    return pl.pallas_call(
        body, name="ag_weights", in_specs=[HBM] * n, out_specs=[HBM] * n,
        out_shape=[SDS((4,) + wp.shape, wp.dtype) for wp in wps], scratch_shapes=_ag_sems(n),
        compiler_params=pltpu.CompilerParams(has_side_effects=True))(*wps)


def _rs_swap_halves(gps, name):
    n = len(gps)

    def body(*refs):
        g_refs, out_refs = refs[:n], refs[n:2 * n]
        send_sems, recv_sems = refs[2 * n:]
        x, y, c = _mesh_pos()
        copies = [pltpu.make_async_remote_copy(
            src_ref=g_refs[a].at[:, 1 - c], dst_ref=out_refs[a], send_sem=send_sems.at[a], recv_sem=recv_sems.at[a],
            device_id=(x, y, 1 - c), device_id_type=MESH) for a in range(n)]
        for cp in copies:
            cp.start()
        for cp in copies:
            cp.wait()

    return pl.pallas_call(
        body, name=name, in_specs=[HBM] * n, out_specs=[HBM] * n,
        out_shape=[SDS((4,) + gp.shape[2:], gp.dtype) for gp in gps],
        scratch_shapes=[pltpu.SemaphoreType.DMA((n,)), pltpu.SemaphoreType.DMA((n,))],
        compiler_params=pltpu.CompilerParams(has_side_effects=True))(*gps)


def _rs_add_halves(gp, other, c_arr, name):
    h, w = other.shape[1:]

    def body(c_ref, a_ref, b_ref, o_ref):
        o_ref[...] = (a_ref[0].astype(F32) + b_ref[...].astype(F32)).astype(BF16)

    return pl.pallas_call(
        body, name=name,
        grid_spec=pltpu.PrefetchScalarGridSpec(
            num_scalar_prefetch=1, grid=(4,),
            in_specs=[pl.BlockSpec((1, 1, h, w), lambda s, c: (s, c[0], 0, 0)),
                      pl.BlockSpec((1, h, w), lambda s, c: (s, 0, 0))],
            out_specs=pl.BlockSpec((1, h, w), lambda s, c: (s, 0, 0))),
        out_shape=SDS((4, h, w), BF16), compiler_params=_params())(c_arr, gp, other)


def _rs_exchange_phases(p_refs, out_refs, send_sems, recv_sems):
    n = len(p_refs)
    x, y, c = _mesh_pos()
    me = 2 * x + y
    chips = _other_chips(x, y)

    def sends():
        return [pltpu.make_async_remote_copy(
            src_ref=p_refs[a].at[2 * cx + cy], dst_ref=out_refs[a].at[me], send_sem=send_sems.at[3 * a + k],
            recv_sem=recv_sems.at[3 * a + k], device_id=(cx, cy, c), device_id_type=MESH)
            for a in range(n) for k, (cx, cy) in enumerate(chips)]

    def start():
        for cp in sends():
            cp.start()

    def finish():
        for a in range(n):
            for k, (cx, cy) in enumerate(chips):
                slot = out_refs[a].at[2 * cx + cy]
                pltpu.make_async_remote_copy(
                    src_ref=slot, dst_ref=slot, send_sem=send_sems.at[3 * a + k], recv_sem=recv_sems.at[3 * a + k],
                    device_id=(cx, cy, c), device_id_type=MESH).wait_recv()
        for cp in sends():
            cp.wait_send()

    return start, finish


def _rs_exchange_sems(n):
    return [pltpu.SemaphoreType.DMA((3 * n,)), pltpu.SemaphoreType.DMA((3 * n,))]


def _rs_add_chips(recv, part, me_arr, name):
    h, w = part.shape[1:]
    th = h // 2 if (h // 2) % 16 == 0 else h

    def body(me_ref, r_ref, p_ref, o_ref):
        me = me_ref[0]
        total = None
        for k in range(4):
            term = jnp.where(me == k, p_ref[k], r_ref[k]).astype(F32)
            total = term if total is None else total + term
        o_ref[...] = total

    spec = pl.BlockSpec((4, th, w), lambda t, me: (0, t, 0))
    return pl.pallas_call(
        body, name=name,
        grid_spec=pltpu.PrefetchScalarGridSpec(
            num_scalar_prefetch=1, grid=(h // th,), in_specs=[spec, spec],
            out_specs=pl.BlockSpec((th, w), lambda t, me: (t, 0))),
        out_shape=SDS((h, w), F32), compiler_params=_params())(me_arr, recv, part)


def _rs_share(halves):
    n = len(halves)

    def body(*refs):
        h_refs, out_refs = refs[:n], refs[n:2 * n]
        send_sems, recv_sems = refs[2 * n:]
        x, y, c = _mesh_pos()
        copies = [pltpu.make_async_remote_copy(
            src_ref=h_refs[a], dst_ref=out_refs[a], send_sem=send_sems.at[a], recv_sem=recv_sems.at[a],
            device_id=(x, y, 1 - c), device_id_type=MESH) for a in range(n)]
        for cp in copies:
            cp.start()
        for cp in copies:
            cp.wait()

    return pl.pallas_call(
        body, name="rs_share", in_specs=[HBM] * n, out_specs=[HBM] * n,
        out_shape=[SDS(hs.shape, hs.dtype) for hs in halves],
        scratch_shapes=[pltpu.SemaphoreType.DMA((n,)), pltpu.SemaphoreType.DMA((n,))],
        compiler_params=pltpu.CompilerParams(has_side_effects=True))(*halves)


def _small_rows(partials):
    lanes = np.arange(512)
    fold = jnp.asarray((lanes[:, None] % 64 == np.arange(128)[None, :]).astype(np.float32), dtype=BF16)
    n_small = len(SMALL)

    def body(*refs):
        fold_ref, o_ref = refs[n_small], refs[n_small + 1]
        o_ref[...] = jnp.zeros_like(o_ref)
        for i, name in enumerate(SMALL):
            v = refs[i][...]
            if name == "gla_norm_w":
                v = v[:, 0:128] + v[:, 128:256] + v[:, 256:384] + v[:, 384:512]
            if name == "sb_norm_w":
                v = _dot_exact(v, fold_ref[...])
            n = SMALL_SIZES[name]
            o_ref[i:i + 1, 0:n] = jnp.sum(v, axis=0, keepdims=True)[:, 0:n]

    vm = pl.BlockSpec(memory_space=pltpu.VMEM)
    return pl.pallas_call(body, name="small_rows", in_specs=[vm] * (n_small + 1), out_specs=vm,
                          out_shape=SDS((SMALL_ROWS, 1024), F32))(*[partials[n] for n in SMALL], fold)


def _small_gather(s):
    def gather(s_ref, out_ref, send_sems, recv_sems, local_sem):
        x, y, c = _mesh_pos()
        me = 4 * x + 2 * y + c
        local = pltpu.make_async_copy(s_ref, out_ref.at[me], local_sem)
        local.start()
        peers = []
        for r in range(1, 8):
            px = 1 - x if r & 4 else x
            py = 1 - y if r & 2 else y
            pc = 1 - c if r & 1 else c
            peers.append((px, py, pc))
        sends = []
        for k, peer in enumerate(peers):
            cp = pltpu.make_async_remote_copy(
                src_ref=s_ref, dst_ref=out_ref.at[me], send_sem=send_sems.at[k], recv_sem=recv_sems.at[k],
                device_id=peer, device_id_type=MESH)
            cp.start()
            sends.append(cp)
        for k, (px, py, pc) in enumerate(peers):
            slot = out_ref.at[4 * px + 2 * py + pc]
            pltpu.make_async_remote_copy(
                src_ref=slot, dst_ref=slot, send_sem=send_sems.at[k], recv_sem=recv_sems.at[k],
                device_id=(px, py, pc), device_id_type=MESH).wait_recv()
        for cp in sends:
            cp.wait_send()
        local.wait()

    return pl.pallas_call(
        gather, name="small_gather", in_specs=[HBM], out_specs=HBM,
        out_shape=SDS((8, SMALL_ROWS, 1024), F32),
        scratch_shapes=[pltpu.SemaphoreType.DMA((7,)), pltpu.SemaphoreType.DMA((7,)), pltpu.SemaphoreType.DMA],
        compiler_params=pltpu.CompilerParams(has_side_effects=True))(s)


def _adamw_update(w, g, m, v):
    mn = ADAM_B1 * m + (1.0 - ADAM_B1) * g
    vn = ADAM_B2 * v + (1.0 - ADAM_B2) * (g * g)
    c1 = 1.0 - ADAM_B1 ** ADAM_STEP
    c2 = 1.0 - ADAM_B2 ** ADAM_STEP
    return -ADAM_LR * ((mn / c1) / (jnp.sqrt(vn / c2) + ADAM_EPS) + ADAM_WD * w), mn, vn


def _small_update(parts, w, m, v):
    n_small = len(SMALL)

    def body(p_ref, *refs):
        ins, outs = refs[:3 * n_small], refs[3 * n_small:]
        total = p_ref[0]
        for k in range(1, 8):
            total = total + p_ref[k]
        for i, name in enumerate(SMALL):
            g = total[i:i + 1, 0:SMALL_SIZES[name]]
            d, mn, vn = _adamw_update(ins[i][...], g, ins[n_small + i][...], ins[2 * n_small + i][...])
            for slot, val in enumerate((g, d, mn, vn)):
                outs[slot * n_small + i][...] = val

    vm = pl.BlockSpec(memory_space=pltpu.VMEM)
    shapes = [SDS((1, SMALL_SIZES[n]), F32) for n in SMALL]
    outs = pl.pallas_call(
        body, name="small_adamw", in_specs=[vm] * (1 + 3 * n_small), out_specs=[vm] * (4 * n_small),
        out_shape=shapes * 4)(parts, *[w[n] for n in SMALL], *[m[n] for n in SMALL], *[v[n] for n in SMALL])
    return [dict(zip(SMALL, outs[s * n_small:(s + 1) * n_small])) for s in range(4)]


def _adamw(w, g, m, v, name):
    rows, cols = w.shape
    tr = rows
    for cand in (512, 352, 256):
        if rows > cand and rows % cand == 0:
            tr = cand
            break

    def body(w_ref, g_ref, m_ref, v_ref, d_ref, mo_ref, vo_ref):
        d_ref[...], mo_ref[...], vo_ref[...] = _adamw_update(w_ref[...], g_ref[...], m_ref[...], v_ref[...])

    spec = pl.BlockSpec((tr, cols), lambda i: (i, 0))
    return pl.pallas_call(
        body, name=name, grid=(rows // tr,), in_specs=[spec] * 4, out_specs=[spec] * 3,
        out_shape=[SDS((rows, cols), F32)] * 3, compiler_params=_params())(w, g, m, v)


SMALL_SIZES = {"mix_norm_w": 1024, "b_gk": 256, "gla_norm_w": 128, "sb_norm_w": 64, "xattn_norm_w": 1024,
               "mem_norm_w": 1024, "mq_norm_w": 256, "mk_norm_w": 256, "ffn_norm_w": 1024}


ROWS_OF = (("w_out", 256), ("w_mq", 256), ("w_mo", 256), ("w_down", 704))
WIN_ROWS = 1056
LATER = ("rows", "gate_up", "mkv")


def _shard_buffers(d, dtype):
    rows = jnp.concatenate([d[n] for n, _ in ROWS_OF], axis=0).astype(dtype)
    gk = jnp.pad(d["w_gk_up"], ((0, WIN_ROWS - D - 16), (0, DIN // 4 - 64)))
    win = jnp.concatenate([d["w_in"], gk], axis=0).astype(dtype)
    return [rows, d["w_gate_up"].astype(dtype), d["w_mkv"].astype(dtype)], win


def _in_halves(a):
    return a.reshape(a.shape[:-2] + (2, a.shape[-2] // 2, a.shape[-1]))


def _whole(a):
    return a.reshape(a.shape[:-3] + (2 * a.shape[-2], a.shape[-1]))


def _first_weights(win):
    w_in = win[:, :D].transpose(1, 0, 2).reshape(D, DIN)
    w_in = jnp.concatenate([w_in[:, :1536], w_in[:, 1552:], w_in[:, 1536:1552],
                            jnp.zeros((D, DIN_P - DIN), w_in.dtype)], axis=1)
    return {"w_in": w_in, "w_gk_up": win[:, D:D + 16, :64].transpose(1, 0, 2).reshape(16, 256)}


def _later_weights(rows, gate_up, mkv):
    out, off = {"w_gate_up": gate_up, "w_mkv": mkv}, 0
    for n, r in ROWS_OF:
        out[n] = rows[:, off:off + r].reshape(4 * r, 1024)
        off += r
    return out


def _later_grad_buffers(g):
    rows = jnp.concatenate([g[n].reshape(4, r, 1024) for n, r in ROWS_OF], axis=1)
    return [rows, g["w_gate_up"], g["w_mkv"]]


def _win_grad_buffer(g):
    gk = g["w_gk_up"].astype(BF16).reshape(16, 4, 64).transpose(1, 0, 2)
    gk = jnp.pad(gk, ((0, 0), (0, WIN_ROWS - D - 16), (0, DIN // 4 - 64)))
    gi = g["w_in"]
    gi = jnp.concatenate([gi[:, :1536], gi[:, C_LR:C_LR + 16], gi[:, 1536:C_LR]], axis=1)
    return jnp.concatenate([gi.reshape(D, 4, DIN // 4).transpose(1, 0, 2), gk], axis=1)


def _shard_grads(rows, gate_up, mkv, win):
    out, off = {"w_gate_up": gate_up, "w_mkv": mkv, "w_in": win[:D], "w_gk_up": win[D:D + 16, :64]}, 0
    for n, r in ROWS_OF:
        out[n] = rows[off:off + r]
        off += r
    return out


def kernel(x, mem, mix_norm_w, w_in, w_gk_up, b_gk, gla_norm_w, sb_norm_w, w_out, xattn_norm_w, mem_norm_w, w_mq, w_mkv, mq_norm_w, mk_norm_w, w_mo, ffn_norm_w, w_gate_up, w_down, loss_target, m_mix_norm_w, m_w_in, m_w_gk_up, m_b_gk, m_gla_norm_w, m_sb_norm_w, m_w_out, m_xattn_norm_w, m_mem_norm_w, m_w_mq, m_w_mkv, m_mq_norm_w, m_mk_norm_w, m_w_mo, m_ffn_norm_w, m_w_gate_up, m_w_down, v_mix_norm_w, v_w_in, v_w_gk_up, v_b_gk, v_gla_norm_w, v_sb_norm_w, v_w_out, v_xattn_norm_w, v_mem_norm_w, v_w_mq, v_w_mkv, v_mq_norm_w, v_mk_norm_w, v_w_mo, v_ffn_norm_w, v_w_gate_up, v_w_down):
    args = dict(locals())
    wts = {n: args[n][0] if n in BIG else args[n] for n in WEIGHTS}
    mom = {n: args["m_" + n][0] if n in BIG else args["m_" + n] for n in WEIGHTS}
    var = {n: args["v_" + n][0] if n in BIG else args["v_" + n] for n in WEIGHTS}

    c = lax.axis_index("c")
    chip = 2 * lax.axis_index("x") + lax.axis_index("y")
    c_arr = c.astype(jnp.int32).reshape(1)
    chip_arr = chip.astype(jnp.int32).reshape(1)
    def own_slot_filled(gathered, mine):
        return [_whole(lax.dynamic_update_slice(got, wp[None], (chip, 0, 0, 0))) for got, wp in zip(gathered, mine)]

    def chip_partials(names, buffers, tag):
        gps = [_in_halves(b) for b in buffers]
        return [_rs_add_halves(gp, other, c_arr, "rs_add_halves_" + n)
                for n, gp, other in zip(names, gps, _rs_swap_halves(gps, "rs_swap_halves_" + tag))]

    later_wps, win_wp = _shard_buffers(wts, BF16)
    later_wps, win_wp = [_in_halves(b) for b in later_wps], _in_halves(win_wp)
    first = _first_weights(*own_slot_filled(_ag_weights([win_wp]), [win_wp]))
    first.update({n: wts[n] for n in SMALL})

    loss, grad_x, g, (parts, received) = _local_step(
        x[0], mem[0], loss_target[0], first,
        later_weights=(later_wps, lambda gathered: _later_weights(*own_slot_filled(gathered, later_wps))),
        later_partials=lambda g: chip_partials(LATER, _later_grad_buffers(g), "later"),
        last_partials=lambda g: chip_partials(("win",), [_win_grad_buffer(g)], "win"))
    loss = lax.psum(loss, ("x", "y", "c"))

    mine = [_rs_add_chips(recv, part, chip_arr, "rs_add_chips_" + n)
            for n, recv, part in zip(LATER + ("win",), received, parts)]
    totals = [jnp.concatenate([jnp.where(c == 0, m, t), jnp.where(c == 0, t, m)], axis=0)
              for m, t in zip(mine, _rs_share(mine))]
    grads = _shard_grads(*totals)

    small_grads, delta, new_m, new_v = _small_update(_small_gather(_small_rows(g)), wts, mom, var)
    grads.update(small_grads)
    for n in BIG:
        w2 = wts[n].reshape(-1, wts[n].shape[-1])
        d_, m_, v_ = _adamw(w2, grads[n].reshape(w2.shape), mom[n].reshape(w2.shape), var[n].reshape(w2.shape),
                            "adamw_" + n)
        delta[n], new_m[n], new_v[n] = (t.reshape((1,) + wts[n].shape) for t in (d_, m_, v_))
        grads[n] = grads[n].reshape((1,) + wts[n].shape)

    return (loss, grad_x[None], *[grads[n] for n in WEIGHTS], *[delta[n] for n in WEIGHTS],
            *[new_m[n] for n in WEIGHTS], *[new_v[n] for n in WEIGHTS])
```

```python
import functools
import math

import numpy as np
import jax
import jax.numpy as jnp
from jax import lax
from jax.experimental import pallas as pl
from jax.experimental.pallas import tpu as pltpu

F32 = jnp.float32
BF16 = jnp.bfloat16
SDS = jax.ShapeDtypeStruct
MESH = pl.DeviceIdType.MESH

D = 1024
EPS = 1e-6
D_FF = 2816
GLA_GATE_NORM = 16.0
GLA_C = 64
MEM_HEADS = 4
MEM_HD = 256
C_QG, C_KG, C_VG, C_GG, C_QS, C_KS, C_VS, C_LR = 0, 256, 512, 1024, 1536, 2048, 2560, 3072
DIN = 3088
DIN_P = 3200
TB = 512
SBQ = 256
VMEM_LIMIT = 56 * 1024 * 1024
HIGHEST = lax.Precision.HIGHEST

ADAM_LR, ADAM_B1, ADAM_B2, ADAM_EPS, ADAM_WD, ADAM_STEP = 0.001, 0.9, 0.999, 1e-08, 0.01, 10

BIG = ("w_in", "w_gk_up", "w_out", "w_mq", "w_mkv", "w_mo", "w_gate_up", "w_down")
SMALL = ("mix_norm_w", "b_gk", "gla_norm_w", "sb_norm_w", "xattn_norm_w", "mem_norm_w", "mq_norm_w",
         "mk_norm_w", "ffn_norm_w")
WEIGHTS = ("mix_norm_w", "w_in", "w_gk_up", "b_gk", "gla_norm_w", "sb_norm_w", "w_out", "xattn_norm_w",
           "mem_norm_w", "w_mq", "w_mkv", "mq_norm_w", "mk_norm_w", "w_mo", "ffn_norm_w", "w_gate_up", "w_down")
SMALL_ROWS = 16


def _params(**kw):
    return pltpu.CompilerParams(vmem_limit_bytes=VMEM_LIMIT, **kw)


def _row(c, j=0):
    return pl.BlockSpec((TB, c), lambda i, j=j: (i, j))


def _const(shape):
    return pl.BlockSpec(shape, lambda i: (0,) * len(shape))


def _dot(a, b):
    return lax.dot_general(a, b, (((1,), (0,)), ((), ())), preferred_element_type=F32)


def _dot_nt(a, b):
    return lax.dot_general(a, b, (((1,), (1,)), ((), ())), preferred_element_type=F32)


def _dot_tn(a, b):
    return lax.dot_general(a, b, (((0,), (0,)), ((), ())), preferred_element_type=F32)


def _dot_nt_f32(a, b):
    return lax.dot_general(a, b, (((1,), (1,)), ((), ())), precision=HIGHEST, preferred_element_type=F32)


def _split3(x):
    h = x.astype(BF16)
    r = x - h.astype(F32)
    m = r.astype(BF16)
    l = (r - m.astype(F32)).astype(BF16)
    return h, m, l


def _dot_exact(x, ones_mat):
    h, m, l = _split3(x)
    return _dot(h, ones_mat) + _dot(m, ones_mat) + _dot(l, ones_mat)


def _softplus(z):
    return jnp.maximum(z, 0.0) + jnp.log1p(jnp.exp(-jnp.abs(z)))


def _rsqrt_ms(x):
    return lax.rsqrt(jnp.mean(x * x, axis=-1, keepdims=True) + EPS)


def _colsum8(x):
    r, c = x.shape
    return jnp.sum(x.reshape(r // 8, 8, c), axis=0)


def _matmul(a, b, *, mode, tm, tn, tk=None, res=None, out_dtype=F32, by_column_tile=False, a_spec=None,
            b_spec=None, mnk=None, epilogue=None, exchange=None, name):
    if mnk is not None:
        M, N, K = mnk
    else:
        K, M = a.shape if mode == "tn" else a.shape[::-1]
        N = b.shape[0] if mode == "nt" else b.shape[1]
    tk = K if tk is None else tk
    assert M % tm == 0 and N % tn == 0 and K % tk == 0, (name, M, N, K, tm, tn, tk)
    nk = K // tk
    if a_spec is None:
        if mode == "tn":
            a_spec = pl.BlockSpec((tk, tm), lambda j, i, k: (k, i))
        else:
            a_spec = pl.BlockSpec((tm, tk), lambda j, i, k: (i, k))
    if b_spec is None:
        if mode == "nt":
            b_spec = pl.BlockSpec((tn, tk), lambda j, i, k: (j, k))
        else:
            b_spec = pl.BlockSpec((tk, tn), lambda j, i, k: (k, j))
    if by_column_tile:
        assert res is None
        o_spec = pl.BlockSpec((None, tm, tn), lambda j, i, k: (j, i, 0))
        o_shape = SDS((N // tn, M, tn), out_dtype)
    else:
        o_spec = pl.BlockSpec((tm, tn), lambda j, i, k: (i, j))
        o_shape = SDS((M, N), out_dtype)
    dot = {"nn": _dot, "nt": _dot_nt, "tn": _dot_tn}[mode]
    has_res = res is not None
    extra_in, extra_out, finish, *lhs_from = epilogue if epilogue is not None else ((), (), None)
    n_in, n_out = 2 + has_res + len(extra_in), 1 + len(extra_out)
    sent = list(exchange) if exchange is not None else []
    ns = len(sent)
    grid = (N // tn, M // tm, nk)

    def body(*refs):
        a_ref, b_ref = refs[0], refs[1]
        res_ref = refs[2] if has_res else None
        in_refs, out_refs = refs[2 + has_res:n_in], refs[n_in + ns:n_in + ns + n_out]
        first_row_tile = pl.program_id(1) == 0
        scratch = refs[n_in + 2 * ns + n_out:]
        if ns:
            ids = [pl.program_id(d) for d in range(3)]
            rs_start, rs_finish = _rs_exchange_phases(
                refs[n_in:n_in + ns], refs[n_in + ns + n_out:n_in + 2 * ns + n_out], *scratch[nk > 1:])
            pl.when((ids[0] == 0) & (ids[1] == 0) & (ids[2] == 0))(rs_start)

        def done(t):
            if has_res:
                t = t + res_ref[...]
            if finish is None:
                out_refs[0][...] = t.astype(out_dtype)
            else:
                finish(t, first_row_tile, in_refs, out_refs)

        lhs = lhs_from[0](a_ref[...], in_refs, out_refs) if lhs_from else a_ref[...].astype(BF16)
        p = dot(lhs, b_ref[...].astype(BF16))
        if nk == 1:
            done(p)
        else:
            acc_ref = scratch[0]
            k = pl.program_id(2)

            @pl.when(k == 0)
            def _():
                acc_ref[...] = p

            @pl.when(k > 0)
            def _():
                acc_ref[...] += p

            @pl.when(k == nk - 1)
            def _():
                done(acc_ref[...])

        if ns:
            pl.when((ids[0] == grid[0] - 1) & (ids[1] == grid[1] - 1) & (ids[2] == grid[2] - 1))(rs_finish)

    in_specs = [a_spec, b_spec] + ([o_spec] if has_res else []) + [s for _, s in extra_in] + [HBM] * ns
    args = (a, b) + ((res,) if has_res else ()) + tuple(x for x, _ in extra_in) + tuple(sent)
    outs = pl.pallas_call(
        body, name=name, grid=grid, in_specs=in_specs,
        out_specs=[o_spec] + [s for _, s in extra_out] + [HBM] * ns,
        out_shape=[o_shape] + [s for s, _ in extra_out] + [SDS(p.shape, p.dtype) for p in sent],
        scratch_shapes=([pltpu.VMEM((tm, tn), F32)] if nk > 1 else []) + (_rs_exchange_sems(ns) if ns else []),
        compiler_params=_params(
            dimension_semantics=("arbitrary",) * 3 if ns else ("parallel", "parallel", "arbitrary")),
    )(*args)
    return outs[0] if epilogue is None and not ns else outs


def _full_row(tm, c):
    return pl.BlockSpec((tm, c), lambda j, i, k: (i, 0))


def _kept(shape):
    return pl.BlockSpec(shape, lambda j, i, k: (0,) * len(shape))


def _then_norm_fwd(w, T, tm):
    dm = w.shape[1]

    def finish(t, first, ins, outs):
        outs[0][...] = t
        outs[1][...] = (t * _rsqrt_ms(t) * ins[0][...]).astype(BF16)

    return [(w, _kept((1, dm)))], [(SDS((T, dm), BF16), _full_row(tm, dm))], finish


def _then_norm_bwd(x, w, dres, tm):
    T, dm = x.shape

    def finish(t, first, ins, outs):
        x_ref, w_ref, dres_ref = ins

        @pl.when(first)
        def _():
            outs[1][...] = jnp.zeros_like(outs[1])

        xv = x_ref[...]
        r = _rsqrt_ms(xv)
        n = xv * r
        dn = t * w_ref[...]
        outs[0][...] = dres_ref[...] + r * (dn - n * jnp.mean(dn * n, axis=-1, keepdims=True))
        outs[1][...] += _colsum8(t * n)

    return ([(x, _full_row(tm, dm)), (w, _kept((1, dm))), (dres, _full_row(tm, dm))],
            [(SDS((8, dm), F32), _kept((8, dm)))], finish)


def _then_loss(tgt, tm):
    def finish(t, first, ins, outs):
        @pl.when(first)
        def _():
            outs[1][...] = jnp.zeros_like(outs[1])

        e = t - ins[0][...]
        outs[0][...] = e * (1.0 / D)
        outs[1][...] += _colsum8(e * e) * (0.5 / D)

    return [(tgt, _full_row(tm, D))], [(SDS((8, D), F32), _kept((8, D)))], finish


def _norm_then_proj_split(norm_w, wgk, bgk, T, tm):
    def lhs_from(xv, ins, outs):
        h = (xv * _rsqrt_ms(xv) * ins[2][...]).astype(BF16)
        outs[5][...] = h
        return h

    def finish(t, first, ins, outs):
        wgk_ref, b_ref = ins[:2]
        proj_ref, gk_ref, qs_ref, ks_ref, vs_ref = outs[:5]
        proj_ref[...] = t
        u = _dot(t[:, C_LR:DIN_P].astype(BF16), wgk_ref[...].astype(BF16)) + b_ref[...]
        gk_ref[...] = -_softplus(-u) / GLA_GATE_NORM
        qs_ref[...] = (t[:, C_QS:C_KS] * 0.125).astype(BF16)
        ks_ref[...] = t[:, C_KS:C_VS].astype(BF16)
        vs_ref[...] = t[:, C_VS:C_LR].astype(BF16)

    return ([(wgk, _kept((128, 256))), (bgk, _kept((1, 256))), (norm_w, _kept((1, D)))],
            [(SDS((T, 256), F32), _full_row(tm, 256))] + [(SDS((T, 512), BF16), _full_row(tm, 512))] * 3
            + [(SDS((T, D), BF16), _full_row(tm, D))], finish, lhs_from)


def _dproj_assemble(proj, dq_g, dk_g, dv_g, dg_g, dq_s, dk_s, dv_s, dgk, wgk, bgk):
    T = proj.shape[0]

    def body(lr_ref, dqg_ref, dkg_ref, dvg_ref, dgg_ref, dqs_ref, dks_ref, dvs_ref, dgk_ref, wgk_ref, b_ref,
             dp_ref, dwgk_ref, dbgk_ref):
        @pl.when(pl.program_id(0) == 0)
        def _():
            dwgk_ref[...] = jnp.zeros_like(dwgk_ref)
            dbgk_ref[...] = jnp.zeros_like(dbgk_ref)

        lr = lr_ref[...].astype(BF16)
        wg = wgk_ref[...].astype(BF16)
        u = _dot(lr, wg) + b_ref[...]
        du = dgk_ref[...] * (jax.nn.sigmoid(-u) / GLA_GATE_NORM)
        dub = du.astype(BF16)
        dp_ref[:, C_QG:C_KG] = (dqg_ref[...] * 0.125).astype(BF16)
        dp_ref[:, C_KG:C_VG] = dkg_ref[...].astype(BF16)
        dp_ref[:, C_VG:C_GG] = dvg_ref[...].astype(BF16)
        dp_ref[:, C_GG:C_QS] = dgg_ref[...].astype(BF16)
        dp_ref[:, C_QS:C_KS] = (dqs_ref[...] * 0.125).astype(BF16)
        dp_ref[:, C_KS:C_VS] = dks_ref[...].astype(BF16)
        dp_ref[:, C_VS:C_LR] = dvs_ref[...].astype(BF16)
        dp_ref[:, C_LR:DIN_P] = _dot_nt(dub, wg).astype(BF16)
        dwgk_ref[...] += _dot_tn(lr, dub)
        dbgk_ref[...] += _colsum8(du)

    return pl.pallas_call(
        body, name="dproj_assemble", grid=(T // TB,),
        in_specs=[_row(128, C_LR // 128), _row(256), _row(256), _row(512), _row(512), _row(512), _row(512),
                  _row(512), _row(256), _const((128, 256)), _const((1, 256))],
        out_specs=[_row(DIN_P), _const((128, 256)), _const((8, 256))],
        out_shape=[SDS((T, DIN_P), BF16), SDS((128, 256), F32), SDS((8, 256), F32)],
        compiler_params=_params())(proj, dq_g, dk_g, dv_g, dg_g, dq_s, dk_s, dv_s, dgk, wgk, bgk)


def _group_ones(n, g):
    idx = np.arange(n) // g
    return jnp.asarray((idx[:, None] == idx[None, :]).astype(np.float32), dtype=BF16)


def _mix_cat(o_g, proj, o_s, wg512, ws512, bd64):
    T = o_g.shape[0]

    def body(og_ref, gg_ref, os_ref, wg_ref, ws_ref, bd_ref, cat_ref):
        og = og_ref[...]
        gg = gg_ref[...]
        s = gg * jax.nn.sigmoid(gg)
        for h in range(4):
            sl = slice(128 * h, 128 * (h + 1))
            x = og[:, sl]
            cat_ref[:, sl] = (x * _rsqrt_ms(x) * wg_ref[:, sl] * s[:, sl]).astype(BF16)
        osv = os_ref[...]
        ms = _dot_exact(osv * osv, bd_ref[...]) * (1.0 / 64.0)
        cat_ref[:, 512:1024] = (osv * lax.rsqrt(ms + EPS) * ws_ref[...]).astype(BF16)

    return pl.pallas_call(
        body, name="mix_cat", grid=(T // TB,),
        in_specs=[_row(512), _row(512, C_GG // 512), _row(512), _const((1, 512)), _const((1, 512)),
                  _const((512, 512))],
        out_specs=_row(1024), out_shape=SDS((T, 1024), BF16), compiler_params=_params())(
            o_g, proj, o_s, wg512, ws512, bd64)


def _then_mix_cat_bwd(o_g, proj, o_s, wg512, ws512, bd64, tm):
    T = o_g.shape[0]

    def finish(t, first, ins, outs):
        og_ref, gg_ref, os_ref, wg_ref, ws_ref, bd_ref = ins
        do_ref, dgg_ref, dwg_ref, dws_ref = outs

        @pl.when(first)
        def _():
            dwg_ref[...] = jnp.zeros_like(dwg_ref)
            dws_ref[...] = jnp.zeros_like(dws_ref)

        og = og_ref[...]
        gg = gg_ref[...]
        sg = jax.nn.sigmoid(gg)
        s = gg * sg
        ds = sg * (1.0 + gg * (1.0 - sg))
        for h in range(4):
            sl = slice(128 * h, 128 * (h + 1))
            x = og[:, sl]
            r = _rsqrt_ms(x)
            n = x * r
            w = wg_ref[:, sl]
            dc = t[:, sl]
            dy = dc * s[:, sl]
            dgg_ref[:, sl] = dc * (n * w) * ds[:, sl]
            dn = dy * w
            do_ref[:, sl] = r * (dn - n * jnp.mean(dn * n, axis=-1, keepdims=True))
            dwg_ref[:, sl] += _colsum8(dy * n)
        osv = os_ref[...]
        bd = bd_ref[...]
        r = lax.rsqrt(_dot_exact(osv * osv, bd) * (1.0 / 64.0) + EPS)
        n = osv * r
        dc = t[:, 512:1024]
        dn = dc * ws_ref[...]
        do_ref[:, 512:1024] = r * (dn - n * (_dot_exact(dn * n, bd) * (1.0 / 64.0)))
        dws_ref[...] += _colsum8(dc * n)

    half = pl.BlockSpec((tm, 512), lambda j, i, k: (i, 0))
    return ([(o_g, half), (proj, pl.BlockSpec((tm, 512), lambda j, i, k: (i, C_GG // 512))), (o_s, half),
             (wg512, _kept((1, 512))), (ws512, _kept((1, 512))), (bd64, _kept((512, 512)))],
            [(SDS((T, 512), F32), half), (SDS((8, 512), F32), _kept((8, 512))),
             (SDS((8, 512), F32), _kept((8, 512)))], finish)


FF_TN = 1408
DW_TK = 2048


def _gate_up_act(h, w):
    T = h.shape[0]
    nj = D_FF // FF_TN

    def body(h_ref, wg_ref, wu_ref, gu_ref, a_ref):
        hv = h_ref[...]
        g = _dot(hv, wg_ref[...])
        u = _dot(hv, wu_ref[...])
        gu_ref[0] = g.astype(BF16)
        gu_ref[1] = u.astype(BF16)
        a_ref[...] = (g * jax.nn.sigmoid(g) * u).astype(BF16)

    return pl.pallas_call(
        body, name="mm_gate_up_act", grid=(nj, T // TB),
        in_specs=[pl.BlockSpec((TB, D), lambda j, i: (i, 0)),
                  pl.BlockSpec((None, D, FF_TN), lambda j, i: (j, 0, 0)),
                  pl.BlockSpec((None, D, FF_TN), lambda j, i: (nj + j, 0, 0))],
        out_specs=[pl.BlockSpec((2, TB, FF_TN), lambda j, i: (0, i, j)),
                   pl.BlockSpec((TB, FF_TN), lambda j, i: (i, j))],
        out_shape=[SDS((2, T, D_FF), BF16), SDS((T, D_FF), BF16)],
        compiler_params=_params(dimension_semantics=("parallel", "parallel")))(h, w, w)


def _down_bwd(dy, w_down, gu):
    T = dy.shape[0]
    nj = D_FF // FF_TN

    def body(dy_ref, w_ref, gu_ref, dgu_ref):
        da = _dot_nt(dy_ref[...].astype(BF16), w_ref[...])
        g = gu_ref[0].astype(F32)
        sg = jax.nn.sigmoid(g)
        dgu_ref[0] = (da * gu_ref[1].astype(F32) * (sg * (1.0 + g * (1.0 - sg)))).astype(BF16)
        dgu_ref[1] = (da * (g * sg)).astype(BF16)

    return pl.pallas_call(
        body, name="mm_down_bwd", grid=(nj, T // TB),
        in_specs=[pl.BlockSpec((TB, D), lambda j, i: (i, 0)),
                  pl.BlockSpec((FF_TN, D), lambda j, i: (j, 0)),
                  pl.BlockSpec((2, TB, FF_TN), lambda j, i: (0, i, j))],
        out_specs=pl.BlockSpec((2, TB, FF_TN), lambda j, i: (0, i, j)),
        out_shape=SDS((2, T, D_FF), BF16),
        compiler_params=_params(dimension_semantics=("parallel", "parallel")))(dy, w_down, gu)


def _gla_consts():
    c = GLA_C
    L = np.tril(np.ones((c, c), np.float32))
    blocks = [L, L[(np.arange(c) // 16) * 16]]
    blocks += [np.repeat(L[16 * i:16 * i + 1], c, axis=0) for i in range(4)]
    blocks.append(np.repeat(L[c - 1:c], c, axis=0))
    return jnp.asarray(np.concatenate(blocks, axis=0))


@jax.custom_vjp
def _gla_lin(cm, g):
    cb = cm.astype(BF16)
    h, m, l = _split3(g)
    y = _dot(cb, h) + _dot(cb, m) + _dot(cb, l)
    return tuple(y[GLA_C * n:GLA_C * (n + 1)] for n in range(7))


def _gla_lin_fwd(cm, g):
    return _gla_lin(cm, g), cm


def _gla_lin_bwd(cm, cts):
    cb = cm.astype(BF16)
    h, m, l = _split3(jnp.concatenate(cts, axis=0))
    return None, _dot_tn(cb, h) + _dot_tn(cb, m) + _dot_tn(cb, l)


_gla_lin.defvjp(_gla_lin_fwd, _gla_lin_bwd)


GLA_SUB = 16
GLA_H = 4
GLA_W = 64 * GLA_H


def _head_lanes(rows, h):
    lane = lax.broadcasted_iota(jnp.int32, (rows, GLA_W), 1)
    return (lane >= 64 * h) & (lane < 64 * (h + 1))


def _gla_scores_terms(qs, k, b, rs, blk):
    row = lax.broadcasted_iota(jnp.int32, (GLA_C, GLA_W), 0)
    keep = row < GLA_SUB * (blk + 1)
    e = jnp.where(keep, jnp.exp(jnp.where(keep, rs[blk] - b, 0.0)), 0.0)
    qb = qs[GLA_SUB * blk:GLA_SUB * (blk + 1)]
    lhs = jnp.concatenate([jnp.where(_head_lanes(GLA_SUB, h), qb, 0.0) for h in range(GLA_H)], axis=0)
    return lhs, e


@jax.custom_vjp
def _gla_scores(qs, k, b, r0, r1, r2, r3):
    rs = (r0, r1, r2, r3)
    per_head = [[] for _ in range(GLA_H)]
    for blk in range(GLA_C // GLA_SUB):
        lhs, e = _gla_scores_terms(qs, k, b, rs, blk)
        a = _dot_nt_f32(lhs, k * e)
        for h in range(GLA_H):
            per_head[h].append(a[GLA_SUB * h:GLA_SUB * (h + 1)])
    return tuple(jnp.concatenate(p, axis=0) for p in per_head)


def _gla_scores_fwd(qs, k, b, r0, r1, r2, r3):
    return _gla_scores(qs, k, b, r0, r1, r2, r3), (qs, k, b, r0, r1, r2, r3)


def _gla_scores_bwd(saved, cts):
    qs, k, b = saved[:3]
    rs = saved[3:]
    dqs, drs = [], []
    dk = jnp.zeros_like(k)
    db = jnp.zeros_like(b)
    for blk in range(GLA_C // GLA_SUB):
        lhs, e = _gla_scores_terms(qs, k, b, rs, blk)
        rows = slice(GLA_SUB * blk, GLA_SUB * (blk + 1))
        da = jnp.concatenate([ct[rows] for ct in cts], axis=0)
        dlhs = lax.dot_general(da, k * e, (((1,), (0,)), ((), ())), precision=HIGHEST, preferred_element_type=F32)
        dq = jnp.zeros((GLA_SUB, GLA_W), F32)
        for h in range(GLA_H):
            dq = jnp.where(_head_lanes(GLA_SUB, h), dlhs[GLA_SUB * h:GLA_SUB * (h + 1)], dq)
        dqs.append(dq)
        dks = lax.dot_general(da, lhs, (((0,), (0,)), ((), ())), precision=HIGHEST, preferred_element_type=F32)
        dk = dk + dks * e
        darg = dks * (k * e)
        db = db - darg
        drs.append(darg)
    return (jnp.concatenate(dqs, axis=0), dk, db, *drs)


_gla_scores.defvjp(_gla_scores_fwd, _gla_scores_bwd)


def _gla_chunk(cm, q, k, g, v0, v1, v2, v3, st):
    c = GLA_C
    vs = (v0, v1, v2, v3)
    ri = lax.broadcasted_iota(jnp.int32, (c, c), 0)
    ci = lax.broadcasted_iota(jnp.int32, (c, c), 1)
    causal = ci <= ri
    b, r, r0, r1, r2, r3, bl = _gla_lin(cm, g)
    scores = _gla_scores(q * jnp.exp(b - r), k, b, r0, r1, r2, r3)
    qe = q * jnp.exp(b)
    kd = k * jnp.exp(bl - b)
    st_new = st * jnp.exp(jnp.concatenate([bl, bl], axis=0))
    outs = []
    for h in range(GLA_H):
        a = jnp.where(causal, scores[h], 0.0)
        outs.append(_dot_nt(jnp.where(_head_lanes(c, h), qe, 0.0), st) + _dot(a, vs[h]))
        st_new = st_new + jnp.where(_head_lanes(2 * c, h), _dot_tn(vs[h], kd), 0.0)
    return (*outs, st_new)


GLA_TB = 512


def _gla_fwd(proj, gk, cm):
    T = proj.shape[0]
    nb = T // GLA_TB
    nc = GLA_TB // GLA_C

    def body(q_ref, k_ref, v_ref, g_ref, cm_ref, o_ref, st_ref, st_scr):
        @pl.when(pl.program_id(0) == 0)
        def _():
            st_scr[...] = jnp.zeros_like(st_scr)

        cmv = cm_ref[...]

        def chunk(ci, carry):
            rs = pl.ds(pl.multiple_of(ci * GLA_C, GLA_C), GLA_C)
            st = st_scr[...]
            st_ref[ci] = st
            *outs, st_new = _gla_chunk(cmv, q_ref[rs, :] * 0.125, k_ref[rs, :], g_ref[rs, :],
                                       *[v_ref[rs, 128 * h:128 * (h + 1)] for h in range(GLA_H)], st)
            for h in range(GLA_H):
                o_ref[rs, 128 * h:128 * (h + 1)] = outs[h]
            st_scr[...] = st_new
            return carry

        lax.fori_loop(0, nc, chunk, 0, unroll=True)

    return pl.pallas_call(
        body, name="gla_fwd", grid=(nb,),
        in_specs=[pl.BlockSpec((GLA_TB, 256), lambda i: (i, C_QG // 256)),
                  pl.BlockSpec((GLA_TB, 256), lambda i: (i, C_KG // 256)),
                  pl.BlockSpec((GLA_TB, 512), lambda i: (i, C_VG // 512)),
                  pl.BlockSpec((GLA_TB, 256), lambda i: (i, 0)),
                  pl.BlockSpec((7 * GLA_C, GLA_C), lambda i: (0, 0))],
        out_specs=[pl.BlockSpec((GLA_TB, 512), lambda i: (i, 0)),
                   pl.BlockSpec((nc, 128, GLA_W), lambda i: (i, 0, 0))],
        out_shape=[SDS((T, 512), F32), SDS((T // GLA_C, 128, GLA_W), F32)],
        scratch_shapes=[pltpu.VMEM((128, GLA_W), F32)],
        compiler_params=_params(dimension_semantics=("arbitrary",)))(proj, proj, proj, gk, cm)


def _gla_bwd(proj, gk, cm, states, do, parts=()):
    T = proj.shape[0]
    nb = T // GLA_TB
    nc = GLA_TB // GLA_C
    n = len(parts)

    def body(q_ref, k_ref, v_ref, g_ref, cm_ref, st_ref, do_ref, *rest):
        p_refs, (dq_ref, dk_ref, dv_ref, dg_ref), out_refs = rest[:n], rest[n:n + 4], rest[n + 4:2 * n + 4]
        dst_scr = rest[2 * n + 4]
        step = pl.program_id(0)
        if n:
            rs_start, rs_finish = _rs_exchange_phases(p_refs, out_refs, *rest[2 * n + 5:])

        @pl.when(step == 0)
        def _():
            dst_scr[...] = jnp.zeros_like(dst_scr)
            if n:
                rs_start()

        cmv = cm_ref[...]

        def chunk(t, carry):
            ci = nc - 1 - t
            rs = pl.ds(pl.multiple_of(ci * GLA_C, GLA_C), GLA_C)
            _, vjp = jax.vjp(
                functools.partial(_gla_chunk, cmv), q_ref[rs, :] * 0.125, k_ref[rs, :], g_ref[rs, :],
                *[v_ref[rs, 128 * h:128 * (h + 1)] for h in range(GLA_H)], st_ref[ci])
            dq, dk, dg, *dvs, dst = vjp((*[do_ref[rs, 128 * h:128 * (h + 1)] for h in range(GLA_H)], dst_scr[...]))
            dq_ref[rs, :] = dq
            dk_ref[rs, :] = dk
            dg_ref[rs, :] = dg
            for h in range(GLA_H):
                dv_ref[rs, 128 * h:128 * (h + 1)] = dvs[h]
            dst_scr[...] = dst
            return carry

        lax.fori_loop(0, nc, chunk, 0, unroll=True)

        if n:
            pl.when(step == nb - 1)(rs_finish)

    rev = lambda i: nb - 1 - i
    return pl.pallas_call(
        body, name="gla_bwd", grid=(nb,),
        in_specs=[pl.BlockSpec((GLA_TB, 256), lambda i: (rev(i), C_QG // 256)),
                  pl.BlockSpec((GLA_TB, 256), lambda i: (rev(i), C_KG // 256)),
                  pl.BlockSpec((GLA_TB, 512), lambda i: (rev(i), C_VG // 512)),
                  pl.BlockSpec((GLA_TB, 256), lambda i: (rev(i), 0)),
                  pl.BlockSpec((7 * GLA_C, GLA_C), lambda i: (0, 0)),
                  pl.BlockSpec((nc, 128, GLA_W), lambda i: (rev(i), 0, 0)),
                  pl.BlockSpec((GLA_TB, 512), lambda i: (rev(i), 0))] + [HBM] * n,
        out_specs=[pl.BlockSpec((GLA_TB, 256), lambda i: (rev(i), 0)),
                   pl.BlockSpec((GLA_TB, 256), lambda i: (rev(i), 0)),
                   pl.BlockSpec((GLA_TB, 512), lambda i: (rev(i), 0)),
                   pl.BlockSpec((GLA_TB, 256), lambda i: (rev(i), 0))] + [HBM] * n,
        out_shape=[SDS((T, 256), F32), SDS((T, 256), F32), SDS((T, 512), F32), SDS((T, 256), F32)]
        + [SDS(p.shape, p.dtype) for p in parts],
        scratch_shapes=[pltpu.VMEM((128, GLA_W), F32)] + (_rs_exchange_sems(n) if n else []),
        compiler_params=_params(dimension_semantics=("arbitrary",)))(proj, proj, proj, gk, cm, states, do, *parts)


SB_DEAD = 105.0
SB_COUNT_LANE = 127


def _sb_tri():
    i = np.arange(SBQ)
    return jnp.asarray((i[:, None] > i[None, :]).astype(np.float32), dtype=BF16)


def _sb_block_fwd(qh, kb, tri, carry, strict):
    z = _dot_nt(qh, kb)
    sp = _softplus(z)
    l1 = -sp
    if strict is not None:
        l1 = jnp.where(strict, l1, 0.0)
    log_a = (z - sp) + _dot(l1.astype(BF16), tri) + carry
    a = jnp.exp(log_a)
    if strict is not None:
        a = jnp.where(strict, a, 0.0)
    return z - sp, l1, a


def _sb_fwd(qs, ks, vs, tri, wps=()):
    T = qs.shape[0]
    nq = T // SBQ
    n = len(wps)

    def body(q_ref, k_ref, v_ref, tri_ref, *rest):
        w_refs, (o_ref, c_ref), out_refs = rest[:n], rest[n:n + 2], rest[n + 2:2 * n + 2]
        pair = pl.program_id(0)
        i = pl.program_id(1)
        if n:
            ag_start, ag_pass_on, ag_finish = _ag_phases(w_refs, out_refs, *rest[2 * n + 2:])
            pl.when((pair == 0) & (i == 0))(ag_start)
        lane = lax.broadcasted_iota(jnp.int32, (1, 128), 1)
        clane = lax.broadcasted_iota(jnp.int32, (SBQ, 128), 1)
        strict = (lax.broadcasted_iota(jnp.int32, (SBQ, SBQ), 1) < lax.broadcasted_iota(jnp.int32, (SBQ, SBQ), 0))
        tri_v = tri_ref[...]
        qv = q_ref[...]
        first_head = lane < 64
        qhs = (jnp.where(first_head, qv, jnp.zeros_like(qv)), jnp.where(first_head, jnp.zeros_like(qv), qv))

        def block(j, carries, accs, masked):
            rs = pl.ds(pl.multiple_of(j * SBQ, SBQ), SBQ)
            kb = k_ref[rs, :]
            vb = v_ref[rs, :]
            out_c, out_a = [], []
            for hh in range(2):
                _, l1, a = _sb_block_fwd(qhs[hh], kb, tri_v, carries[hh], strict if masked else None)
                out_a.append(accs[hh] + _dot(a.astype(BF16), vb))
                out_c.append(carries[hh] + jnp.sum(l1, axis=1, keepdims=True))
            return out_c, out_a

        zero1 = jnp.zeros((SBQ, 1), F32)
        zero128 = jnp.zeros((SBQ, 128), F32)
        (c0, c1), (a0, a1) = block(i, (zero1, zero1), (zero128, zero128), True)

        def more(state):
            return (state[0] <= i) & (jnp.maximum(jnp.max(state[1]), jnp.max(state[2])) > -SB_DEAD)

        def step(state):
            jj, c0, c1, a0, a1, t0, t1 = state
            j = i - jj
            t0 = jnp.where(clane == j, c0, t0)
            t1 = jnp.where(clane == j, c1, t1)
            (c0, c1), (a0, a1) = block(j, (c0, c1), (a0, a1), False)
            return jj + 1, c0, c1, a0, a1, t0, t1

        jj, c0, c1, a0, a1, t0, t1 = lax.while_loop(
            more, step, (jnp.int32(1), c0, c1, a0, a1, zero128, zero128))
        o_ref[...] = jnp.where(first_head, a0, a1)
        swept = (jj - 1).astype(F32)
        c_ref[0, :, 0:128] = jnp.where(clane == SB_COUNT_LANE, swept, t0)
        c_ref[0, :, 128:256] = jnp.where(clane == SB_COUNT_LANE, swept, t1)
        if n:
            pl.when((pair == 3) & (i == 0))(ag_pass_on)
            pl.when((pair == 3) & (i == nq - 1))(ag_finish)

    return pl.pallas_call(
        body, name="sb_fwd", grid=(4, nq),
        in_specs=[pl.BlockSpec((SBQ, 128), lambda h, i: (i, h)),
                  pl.BlockSpec((T, 128), lambda h, i: (0, h)),
                  pl.BlockSpec((T, 128), lambda h, i: (0, h)),
                  pl.BlockSpec((SBQ, SBQ), lambda h, i: (0, 0))] + [HBM] * n,
        out_specs=[pl.BlockSpec((SBQ, 128), lambda h, i: (i, h)),
                   pl.BlockSpec((1, SBQ, 256), lambda h, i: (h, i, 0))] + [HBM] * n,
        out_shape=[SDS((T, 512), F32), SDS((4, T, 256), F32)] + [SDS((4,) + wp.shape, wp.dtype) for wp in wps],
        scratch_shapes=_ag_sems(n) if n else [],
        compiler_params=_params(dimension_semantics=("arbitrary", "arbitrary")))(qs, ks, vs, tri, *wps)


def _sb_bwd(qs, ks, vs, do, carries, tri, tri_t):
    T = qs.shape[0]
    nq = T // SBQ

    def body(q_ref, k_ref, v_ref, do_ref, c_ref, tri_ref, trit_ref, dq_ref, dk_ref, dv_ref):
        i = pl.program_id(1)

        @pl.when(i == 0)
        def _():
            dk_ref[...] = jnp.zeros_like(dk_ref)
            dv_ref[...] = jnp.zeros_like(dv_ref)

        lane = lax.broadcasted_iota(jnp.int32, (1, 128), 1)
        clane = lax.broadcasted_iota(jnp.int32, (SBQ, 128), 1)
        strict = (lax.broadcasted_iota(jnp.int32, (SBQ, SBQ), 1) < lax.broadcasted_iota(jnp.int32, (SBQ, SBQ), 0))
        tri_v = tri_ref[...]
        trit_v = trit_ref[...]
        qv = q_ref[...]
        dov = do_ref[...].astype(BF16)
        first_head = lane < 64
        qhs = (jnp.where(first_head, qv, jnp.zeros_like(qv)), jnp.where(first_head, jnp.zeros_like(qv), qv))
        dohs = (jnp.where(first_head, dov, jnp.zeros_like(dov)), jnp.where(first_head, jnp.zeros_like(dov), dov))
        cts = (c_ref[0, :, 0:128], c_ref[0, :, 128:256])

        def block(j, pcarries, dqs, masked):
            rs = pl.ds(pl.multiple_of(j * SBQ, SBQ), SBQ)
            kb = k_ref[rs, :]
            vb = v_ref[rs, :]
            out_p, out_q = [], []
            dk = jnp.zeros((SBQ, 128), F32)
            dv = jnp.zeros((SBQ, 128), F32)
            for hh in range(2):
                carry = jnp.sum(jnp.where(clane == j, cts[hh], 0.0), axis=1, keepdims=True)
                lb, _, a = _sb_block_fwd(qhs[hh], kb, tri_v, carry, strict if masked else None)
                g = a * _dot_nt(dohs[hh], vb)
                p = _dot(g.astype(BF16), trit_v) + pcarries[hh]
                dz = g - (g + p) * jnp.exp(lb)
                if masked:
                    dz = jnp.where(strict, dz, 0.0)
                dzb = dz.astype(BF16)
                dk = dk + _dot_tn(dzb, qhs[hh])
                dv = dv + _dot_tn(a.astype(BF16), dohs[hh])
                out_p.append(pcarries[hh] + jnp.sum(g, axis=1, keepdims=True))
                out_q.append(dqs[hh] + _dot(dzb, kb))
            dk_ref[rs, :] += dk
            dv_ref[rs, :] += dv
            return out_p, out_q

        def step(j, state):
            (p0, p1), (q0, q1) = block(j, (state[0], state[1]), (state[2], state[3]), False)
            return p0, p1, q0, q1

        swept = jnp.max(jnp.where(clane == SB_COUNT_LANE, cts[0], 0.0)).astype(jnp.int32)
        first = i - jnp.clip(swept, 0, i)
        zero1 = jnp.zeros((SBQ, 1), F32)
        zero128 = jnp.zeros((SBQ, 128), F32)
        p0, p1, q0, q1 = lax.fori_loop(first, i, step, (zero1, zero1, zero128, zero128))
        _, (q0, q1) = block(i, (p0, p1), (q0, q1), True)
        dq_ref[...] = jnp.where(first_head, q0, q1)

    return pl.pallas_call(
        body, name="sb_bwd", grid=(4, nq),
        in_specs=[pl.BlockSpec((SBQ, 128), lambda h, i: (i, h)),
                  pl.BlockSpec((T, 128), lambda h, i: (0, h)),
                  pl.BlockSpec((T, 128), lambda h, i: (0, h)),
                  pl.BlockSpec((SBQ, 128), lambda h, i: (i, 4 + h)),
                  pl.BlockSpec((1, SBQ, 256), lambda h, i: (h, i, 0)),
                  pl.BlockSpec((SBQ, SBQ), lambda h, i: (0, 0)),
                  pl.BlockSpec((SBQ, SBQ), lambda h, i: (0, 0))],
        out_specs=[pl.BlockSpec((SBQ, 128), lambda h, i: (i, h)),
                   pl.BlockSpec((T, 128), lambda h, i: (0, h)),
                   pl.BlockSpec((T, 128), lambda h, i: (0, h))],
        out_shape=[SDS((T, 512), F32), SDS((T, 512), F32), SDS((T, 512), F32)],
        compiler_params=_params(dimension_semantics=("parallel", "arbitrary")))(qs, ks, vs, do, carries, tri, tri_t)


def _mem_fwd(mem, mem_norm_w, w_mkv, mk_norm_w):
    M = mem.shape[0]

    def body(mem_ref, wn_ref, w_ref, wk_ref, mn_ref, kraw_ref, k_ref, v_ref):
        mv = mem_ref[...]
        mn = (mv * _rsqrt_ms(mv) * wn_ref[...]).astype(BF16)
        mn_ref[...] = mn
        for s in range(2):
            cols = slice(512 * s, 512 * (s + 1))
            ks = _dot(mn, w_ref[s, :D, :])
            kraw_ref[:, cols] = ks
            v_ref[:, cols] = _dot(mn, w_ref[2 + s, :D, :]).astype(BF16)
            for h in range(2):
                x = ks[:, MEM_HD * h:MEM_HD * (h + 1)]
                k_ref[:, 512 * s + MEM_HD * h:512 * s + MEM_HD * (h + 1)] = (
                    x * _rsqrt_ms(x) * wk_ref[...]).astype(BF16)

    vm = pl.BlockSpec(memory_space=pltpu.VMEM)
    return pl.pallas_call(
        body, name="mem_fwd", in_specs=[vm] * 4, out_specs=[vm] * 4,
        out_shape=[SDS((M, D), BF16), SDS((M, D), F32), SDS((M, D), BF16), SDS((M, D), BF16)],
        compiler_params=_params())(mem, mem_norm_w, w_mkv, mk_norm_w)


def _xattn_fwd(qraw, k, v, wq):
    T = qraw.shape[0]
    M = k.shape[0]

    def body(q_ref, k_ref, v_ref, wq_ref, o_ref):
        for h in range(MEM_HEADS):
            sl = slice(MEM_HD * h, MEM_HD * (h + 1))
            x = q_ref[:, sl]
            q = (x * _rsqrt_ms(x) * wq_ref[...]).astype(BF16)
            s = _dot_nt(q, k_ref[:, sl]) * (1.0 / math.sqrt(MEM_HD))
            s = s - jnp.max(s, axis=-1, keepdims=True)
            e = jnp.exp(s)
            p = e / jnp.sum(e, axis=-1, keepdims=True)
            o_ref[:, sl] = _dot(p.astype(BF16), v_ref[:, sl]).astype(BF16)

    return pl.pallas_call(
        body, name="xattn_fwd", grid=(T // TB,),
        in_specs=[_row(D), _const((M, D)), _const((M, D)), _const((1, MEM_HD))],
        out_specs=_row(D), out_shape=SDS((T, D), BF16), compiler_params=_params())(qraw, k, v, wq)


def _xattn_bwd(qraw, k, v, wq, do):
    T = qraw.shape[0]
    M = k.shape[0]

    def body(q_ref, k_ref, v_ref, wq_ref, do_ref, dq_ref, dk_ref, dv_ref, dw_ref):
        @pl.when(pl.program_id(0) == 0)
        def _():
            dk_ref[...] = jnp.zeros_like(dk_ref)
            dv_ref[...] = jnp.zeros_like(dv_ref)
            dw_ref[...] = jnp.zeros_like(dw_ref)

        w = wq_ref[...]
        for h in range(MEM_HEADS):
            sl = slice(MEM_HD * h, MEM_HD * (h + 1))
            x = q_ref[:, sl]
            r = _rsqrt_ms(x)
            n = x * r
            q = (n * w).astype(BF16)
            kb = k_ref[:, sl]
            s = _dot_nt(q, kb) * (1.0 / math.sqrt(MEM_HD))
            s = s - jnp.max(s, axis=-1, keepdims=True)
            e = jnp.exp(s)
            p = e / jnp.sum(e, axis=-1, keepdims=True)
            dob = do_ref[:, sl].astype(BF16)
            dp = _dot_nt(dob, v_ref[:, sl])
            ds = (p * (dp - jnp.sum(dp * p, axis=-1, keepdims=True)) * (1.0 / math.sqrt(MEM_HD))).astype(BF16)
            dv_ref[:, sl] += _dot_tn(p.astype(BF16), dob)
            dk_ref[:, sl] += _dot_tn(ds, q)
            dqn = _dot(ds, kb)
            dn = dqn * w
            dq_ref[:, sl] = r * (dn - n * jnp.mean(dn * n, axis=-1, keepdims=True))
            dw_ref[...] += _colsum8(dqn * n)

    return pl.pallas_call(
        body, name="xattn_bwd", grid=(T // TB,),
        in_specs=[_row(D), _const((M, D)), _const((M, D)), _const((1, MEM_HD)), _row(D)],
        out_specs=[_row(D), _const((M, D)), _const((M, D)), _const((8, MEM_HD))],
        out_shape=[SDS((T, D), F32), SDS((M, D), F32), SDS((M, D), F32), SDS((8, MEM_HD), F32)],
        compiler_params=_params())(qraw, k, v, wq, do)


def _mem_bwd(mem, mem_norm_w, w_mkv, mk_norm_w, mem_n, k_raw, dk, dv):
    M = mem.shape[0]

    def body(mem_ref, wn_ref, w_ref, wk_ref, mn_ref, kraw_ref, dk_ref, dv_ref, dw_ref, dwn_ref, dwk_ref, dkv_scr):
        wk = wk_ref[...]
        dwk = jnp.zeros((8, MEM_HD), F32)
        for h in range(MEM_HEADS):
            sl = slice(MEM_HD * h, MEM_HD * (h + 1))
            x = kraw_ref[:, sl]
            r = _rsqrt_ms(x)
            n = x * r
            dkh = dk_ref[:, sl]
            dn = dkh * wk
            dkv_scr[:, sl] = (r * (dn - n * jnp.mean(dn * n, axis=-1, keepdims=True))).astype(BF16)
            dwk = dwk + _colsum8(dkh * n)
        dwk_ref[...] = dwk
        dkv_scr[:, D:] = dv_ref[...].astype(BF16)
        dkv = dkv_scr[...]
        mn = mn_ref[...]
        dmn = jnp.zeros((M, D), F32)
        for s in range(4):
            part = dkv[:, 512 * s:512 * (s + 1)]
            dw_ref[s] = _dot_tn(mn, part).astype(BF16)
            dmn = dmn + _dot_nt(part, w_ref[s, :D, :])
        mv = mem_ref[...]
        dwn_ref[...] = _colsum8(dmn * (mv * _rsqrt_ms(mv)))

    vm = pl.BlockSpec(memory_space=pltpu.VMEM)
    return pl.pallas_call(
        body, name="mem_bwd", in_specs=[vm] * 8, out_specs=[vm] * 3,
        out_shape=[SDS((4, D, 512), BF16), SDS((8, D), F32), SDS((8, MEM_HD), F32)],
        scratch_shapes=[pltpu.VMEM((M, 2 * D), BF16)],
        compiler_params=_params())(mem, mem_norm_w, w_mkv, mk_norm_w, mem_n, k_raw, dk, dv)


def _local_step(x, mem, tgt, w, later_weights=None, later_partials=None, last_partials=None):
    T = x.shape[0]
    wgk = jnp.zeros((128, 256), F32).at[:16].set(w["w_gk_up"].astype(F32))
    wg512 = jnp.tile(w["gla_norm_w"], (1, 4))
    ws512 = jnp.tile(w["sb_norm_w"], (1, 8))
    bd64 = _group_ones(512, 64)
    cm = _gla_consts()
    tri = _sb_tri()
    tri_t = tri.T

    proj, gk, qs, ks, vs, h1 = _matmul(
        x, w["w_in"], mode="nn", tm=TB, tn=DIN_P, name="mm_proj",
        epilogue=_norm_then_proj_split(w["mix_norm_w"], wgk, w["b_gk"], T, TB))
    o_g, states = _gla_fwd(proj, gk, cm)
    if later_weights is None:
        o_s, carries = _sb_fwd(qs, ks, vs, tri)
    else:
        o_s, carries, *gathered = _sb_fwd(qs, ks, vs, tri, later_weights[0])
        w = {**w, **later_weights[1](gathered)}
    cat = _mix_cat(o_g, proj, o_s, wg512, ws512, bd64)
    tm3 = min(2 * TB, T)
    x1, h2 = _matmul(cat, w["w_out"], mode="nn", tm=tm3, tn=D, res=x, name="mm_out",
                     epilogue=_then_norm_fwd(w["xattn_norm_w"], T, tm3))
    qraw = _matmul(h2, w["w_mq"], mode="nn", tm=tm3, tn=D, name="mm_mq")
    mem_n, k_raw, k_n, v_m = _mem_fwd(mem, w["mem_norm_w"], w["w_mkv"], w["mk_norm_w"])
    om = _xattn_fwd(qraw, k_n, v_m, w["mq_norm_w"])
    x2, h3 = _matmul(om, w["w_mo"], mode="nn", tm=tm3, tn=D, res=x1, name="mm_mo",
                     epilogue=_then_norm_fwd(w["ffn_norm_w"], T, tm3))
    gu, act = _gate_up_act(h3, w["w_gate_up"])
    dx3, loss_rows = _matmul(act, w["w_down"], mode="nn", tm=TB, tn=D, res=x2, name="mm_down",
                             epilogue=_then_loss(tgt, TB))

    g = {}
    dw_tk = min(DW_TK, T)
    dw = dict(mode="tn", tk=dw_tk, out_dtype=BF16)
    g["w_down"] = _matmul(act, dx3, tm=1408, tn=D, name="mm_dw_down", **dw)
    dgu = _down_bwd(dx3, w["w_down"], gu)
    g["w_gate_up"] = _matmul(
        h3, dgu, tm=D, tn=FF_TN, by_column_tile=True, mnk=(D, 2 * D_FF, T), name="mm_dw_gate_up",
        b_spec=pl.BlockSpec((None, dw_tk, FF_TN), lambda j, i, k: (j // 2, k, j % 2)), **dw)
    dx2, g["ffn_norm_w"] = _matmul(
        dgu, w["w_gate_up"], mode="nt", tm=tm3, tn=D, tk=FF_TN, mnk=(T, D, 2 * D_FF), name="mm_dh3",
        a_spec=pl.BlockSpec((None, tm3, FF_TN), lambda j, i, k: (k // 2, i, k % 2)),
        b_spec=pl.BlockSpec((None, D, FF_TN), lambda j, i, k: (k, 0, 0)),
        epilogue=_then_norm_bwd(x2, w["ffn_norm_w"], dx3, tm3))
    g["w_mo"] = _matmul(om, dx2, tm=D, tn=D, name="mm_dw_mo", **dw)
    dom = _matmul(dx2, w["w_mo"], mode="nt", tm=tm3, tn=D, name="mm_dom")
    dqraw, dk_n, dv_m, g["mq_norm_w"] = _xattn_bwd(qraw, k_n, v_m, w["mq_norm_w"], dom)
    g["w_mkv"], g["mem_norm_w"], g["mk_norm_w"] = _mem_bwd(
        mem, w["mem_norm_w"], w["w_mkv"], w["mk_norm_w"], mem_n, k_raw, dk_n, dv_m)
    g["w_mq"] = _matmul(h2, dqraw, tm=D, tn=D, name="mm_dw_mq", **dw)
    dx1, g["xattn_norm_w"] = _matmul(dqraw, w["w_mq"], mode="nt", tm=tm3, tn=D, name="mm_dh2",
                                     epilogue=_then_norm_bwd(x1, w["xattn_norm_w"], dx2, tm3))
    g["w_out"] = _matmul(cat, dx1, tm=D, tn=D, name="mm_dw_out", **dw)
    do_gs, dg_g, dwg, dws = _matmul(dx1, w["w_out"], mode="nt", tm=TB, tn=D, name="mm_dcat",
                                    epilogue=_then_mix_cat_bwd(o_g, proj, o_s, wg512, ws512, bd64, TB))
    dq_s, dk_s, dv_s = _sb_bwd(qs, ks, vs, do_gs, carries, tri, tri_t)
    parts = () if later_partials is None else later_partials(g)
    dq_g, dk_g, dv_g, dgk, *received = _gla_bwd(proj, gk, cm, states, do_gs, parts)
    dproj, dwgk, g["b_gk"] = _dproj_assemble(proj, dq_g, dk_g, dv_g, dg_g, dq_s, dk_s, dv_s, dgk, wgk, w["b_gk"])
    g["w_in"] = _matmul(h1, dproj, tm=D, tn=640, name="mm_dw_in", **dw)
    g["w_gk_up"] = dwgk[:16]
    last_parts = [] if last_partials is None else last_partials(g)
    grad_x, g["mix_norm_w"], *last_received = _matmul(
        dproj, w["w_in"], mode="nt", tm=TB, tn=D, name="mm_dh1",
        epilogue=_then_norm_bwd(x, w["mix_norm_w"], dx1, TB), exchange=last_parts)

    g["gla_norm_w"], g["sb_norm_w"] = dwg, dws
    if later_partials is None:
        g["gla_norm_w"] = dwg.reshape(8, 4, 128).sum(axis=1)
        g["sb_norm_w"] = dws.reshape(8, 8, 64).sum(axis=1)
        for n in SMALL:
            g[n] = jnp.sum(g[n], axis=0, keepdims=True)
        return jnp.sum(loss_rows), grad_x, g
    return jnp.sum(loss_rows), grad_x, g, (list(parts) + last_parts, list(received) + last_received)


def _mesh_pos():
    return lax.axis_index("x"), lax.axis_index("y"), lax.axis_index("c")


def _other_chips(x, y):
    return [(1 - x, y), (x, 1 - y), (1 - x, 1 - y)]


HBM = pl.BlockSpec(memory_space=pl.ANY)


def _ag_phases(w_refs, out_refs, send_sems, recv_sems):
    n = len(w_refs)
    x, y, c = _mesh_pos()
    me = 2 * x + y
    sibling = (x, y, 1 - c)
    chips = _other_chips(x, y)
    mine, theirs = c, 1 - c

    def copy(a, k, src, dst, to):
        return pltpu.make_async_remote_copy(src_ref=src, dst_ref=dst, send_sem=send_sems.at[6 * a + k],
                                            recv_sem=recv_sems.at[6 * a + k], device_id=to, device_id_type=MESH)

    def firsts():
        return [copy(a, k, w_refs[a].at[mine], out_refs[a].at[me, mine], (cx, cy, c))
                for a in range(n) for k, (cx, cy) in enumerate(chips)]

    def landed(a, k, half):
        cx, cy = chips[k]
        return out_refs[a].at[2 * cx + cy, half]

    def passes():
        return [copy(a, 3 + k, landed(a, k, mine), landed(a, k, mine), sibling) for a in range(n) for k in range(3)]

    def start():
        for cp in firsts():
            cp.start()

    def pass_on():
        for a in range(n):
            for k, (cx, cy) in enumerate(chips):
                copy(a, k, landed(a, k, mine), landed(a, k, mine), (cx, cy, c)).wait_recv()
                copy(a, 3 + k, landed(a, k, mine), landed(a, k, mine), sibling).start()

    def finish():
        for a in range(n):
            for k in range(3):
                copy(a, 3 + k, landed(a, k, theirs), landed(a, k, theirs), sibling).wait_recv()
        for cp in firsts() + passes():
            cp.wait_send()

    return start, pass_on, finish


def _ag_sems(n):
    return [pltpu.SemaphoreType.DMA((6 * n,)), pltpu.SemaphoreType.DMA((6 * n,))]


def _ag_weights(wps):
    n = len(wps)

    def body(*refs):
        for phase in _ag_phases(refs[:n], refs[n:2 * n], *refs[2 * n:]):
            phase()

    return pl.pallas_call(
        body, name="ag_weights", in_specs=[HBM] * n, out_specs=[HBM] * n,
        out_shape=[SDS((4,) + wp.shape, wp.dtype) for wp in wps], scratch_shapes=_ag_sems(n),
        compiler_params=pltpu.CompilerParams(has_side_effects=True))(*wps)


def _rs_swap_halves(gps, name):
    n = len(gps)

    def body(*refs):
        g_refs, out_refs = refs[:n], refs[n:2 * n]
        send_sems, recv_sems = refs[2 * n:]
        x, y, c = _mesh_pos()
        copies = [pltpu.make_async_remote_copy(
            src_ref=g_refs[a].at[:, 1 - c], dst_ref=out_refs[a], send_sem=send_sems.at[a], recv_sem=recv_sems.at[a],
            device_id=(x, y, 1 - c), device_id_type=MESH) for a in range(n)]
        for cp in copies:
            cp.start()
        for cp in copies:
            cp.wait()

    return pl.pallas_call(
        body, name=name, in_specs=[HBM] * n, out_specs=[HBM] * n,
        out_shape=[SDS((4,) + gp.shape[2:], gp.dtype) for gp in gps],
        scratch_shapes=[pltpu.SemaphoreType.DMA((n,)), pltpu.SemaphoreType.DMA((n,))],
        compiler_params=pltpu.CompilerParams(has_side_effects=True))(*gps)


def _rs_add_halves(gp, other, c_arr, name):
    h, w = other.shape[1:]

    def body(c_ref, a_ref, b_ref, o_ref):
        o_ref[...] = (a_ref[0].astype(F32) + b_ref[...].astype(F32)).astype(BF16)

    return pl.pallas_call(
        body, name=name,
        grid_spec=pltpu.PrefetchScalarGridSpec(
            num_scalar_prefetch=1, grid=(4,),
            in_specs=[pl.BlockSpec((1, 1, h, w), lambda s, c: (s, c[0], 0, 0)),
                      pl.BlockSpec((1, h, w), lambda s, c: (s, 0, 0))],
            out_specs=pl.BlockSpec((1, h, w), lambda s, c: (s, 0, 0))),
        out_shape=SDS((4, h, w), BF16), compiler_params=_params())(c_arr, gp, other)


def _rs_exchange_phases(p_refs, out_refs, send_sems, recv_sems):
    n = len(p_refs)
    x, y, c = _mesh_pos()
    me = 2 * x + y
    chips = _other_chips(x, y)

    def sends():
        return [pltpu.make_async_remote_copy(
            src_ref=p_refs[a].at[2 * cx + cy], dst_ref=out_refs[a].at[me], send_sem=send_sems.at[3 * a + k],
            recv_sem=recv_sems.at[3 * a + k], device_id=(cx, cy, c), device_id_type=MESH)
            for a in range(n) for k, (cx, cy) in enumerate(chips)]

    def start():
        for cp in sends():
            cp.start()

    def finish():
        for a in range(n):
            for k, (cx, cy) in enumerate(chips):
                slot = out_refs[a].at[2 * cx + cy]
                pltpu.make_async_remote_copy(
                    src_ref=slot, dst_ref=slot, send_sem=send_sems.at[3 * a + k], recv_sem=recv_sems.at[3 * a + k],
                    device_id=(cx, cy, c), device_id_type=MESH).wait_recv()
        for cp in sends():
            cp.wait_send()

    return start, finish


def _rs_exchange_sems(n):
    return [pltpu.SemaphoreType.DMA((3 * n,)), pltpu.SemaphoreType.DMA((3 * n,))]


def _rs_add_chips(recv, part, me_arr, name):
    h, w = part.shape[1:]
    th = h // 2 if (h // 2) % 16 == 0 else h

    def body(me_ref, r_ref, p_ref, o_ref):
        me = me_ref[0]
        total = None
        for k in range(4):
            term = jnp.where(me == k, p_ref[k], r_ref[k]).astype(F32)
            total = term if total is None else total + term
        o_ref[...] = total

    spec = pl.BlockSpec((4, th, w), lambda t, me: (0, t, 0))
    return pl.pallas_call(
        body, name=name,
        grid_spec=pltpu.PrefetchScalarGridSpec(
            num_scalar_prefetch=1, grid=(h // th,), in_specs=[spec, spec],
            out_specs=pl.BlockSpec((th, w), lambda t, me: (t, 0))),
        out_shape=SDS((h, w), F32), compiler_params=_params())(me_arr, recv, part)


def _rs_share(halves):
    n = len(halves)

    def body(*refs):
        h_refs, out_refs = refs[:n], refs[n:2 * n]
        send_sems, recv_sems = refs[2 * n:]
        x, y, c = _mesh_pos()
        copies = [pltpu.make_async_remote_copy(
            src_ref=h_refs[a], dst_ref=out_refs[a], send_sem=send_sems.at[a], recv_sem=recv_sems.at[a],
            device_id=(x, y, 1 - c), device_id_type=MESH) for a in range(n)]
        for cp in copies:
            cp.start()
        for cp in copies:
            cp.wait()

    return pl.pallas_call(
        body, name="rs_share", in_specs=[HBM] * n, out_specs=[HBM] * n,
        out_shape=[SDS(hs.shape, hs.dtype) for hs in halves],
        scratch_shapes=[pltpu.SemaphoreType.DMA((n,)), pltpu.SemaphoreType.DMA((n,))],
        compiler_params=pltpu.CompilerParams(has_side_effects=True))(*halves)


def _small_rows(partials):
    lanes = np.arange(512)
    fold = jnp.asarray((lanes[:, None] % 64 == np.arange(128)[None, :]).astype(np.float32), dtype=BF16)
    n_small = len(SMALL)

    def body(*refs):
        fold_ref, o_ref = refs[n_small], refs[n_small + 1]
        o_ref[...] = jnp.zeros_like(o_ref)
        for i, name in enumerate(SMALL):
            v = refs[i][...]
            if name == "gla_norm_w":
                v = v[:, 0:128] + v[:, 128:256] + v[:, 256:384] + v[:, 384:512]
            if name == "sb_norm_w":
                v = _dot_exact(v, fold_ref[...])
            n = SMALL_SIZES[name]
            o_ref[i:i + 1, 0:n] = jnp.sum(v, axis=0, keepdims=True)[:, 0:n]

    vm = pl.BlockSpec(memory_space=pltpu.VMEM)
    return pl.pallas_call(body, name="small_rows", in_specs=[vm] * (n_small + 1), out_specs=vm,
                          out_shape=SDS((SMALL_ROWS, 1024), F32))(*[partials[n] for n in SMALL], fold)


def _small_gather(s):
    def gather(s_ref, out_ref, send_sems, recv_sems, local_sem):
        x, y, c = _mesh_pos()
        me = 4 * x + 2 * y + c
        local = pltpu.make_async_copy(s_ref, out_ref.at[me], local_sem)
        local.start()
        peers = []
        for r in range(1, 8):
            px = 1 - x if r & 4 else x
            py = 1 - y if r & 2 else y
            pc = 1 - c if r & 1 else c
            peers.append((px, py, pc))
        sends = []
        for k, peer in enumerate(peers):
            cp = pltpu.make_async_remote_copy(
                src_ref=s_ref, dst_ref=out_ref.at[me], send_sem=send_sems.at[k], recv_sem=recv_sems.at[k],
                device_id=peer, device_id_type=MESH)
            cp.start()
            sends.append(cp)
        for k, (px, py, pc) in enumerate(peers):
            slot = out_ref.at[4 * px + 2 * py + pc]
            pltpu.make_async_remote_copy(
                src_ref=slot, dst_ref=slot, send_sem=send_sems.at[k], recv_sem=recv_sems.at[k],
                device_id=(px, py, pc), device_id_type=MESH).wait_recv()
        for cp in sends:
            cp.wait_send()
        local.wait()

    return pl.pallas_call(
        gather, name="small_gather", in_specs=[HBM], out_specs=HBM,
        out_shape=SDS((8, SMALL_ROWS, 1024), F32),
        scratch_shapes=[pltpu.SemaphoreType.DMA((7,)), pltpu.SemaphoreType.DMA((7,)), pltpu.SemaphoreType.DMA],
        compiler_params=pltpu.CompilerParams(has_side_effects=True))(s)


def _adamw_update(w, g, m, v):
    mn = ADAM_B1 * m + (1.0 - ADAM_B1) * g
    vn = ADAM_B2 * v + (1.0 - ADAM_B2) * (g * g)
    c1 = 1.0 - ADAM_B1 ** ADAM_STEP
    c2 = 1.0 - ADAM_B2 ** ADAM_STEP
    return -ADAM_LR * ((mn / c1) / (jnp.sqrt(vn / c2) + ADAM_EPS) + ADAM_WD * w), mn, vn


def _small_update(parts, w, m, v):
    n_small = len(SMALL)

    def body(p_ref, *refs):
        ins, outs = refs[:3 * n_small], refs[3 * n_small:]
        total = p_ref[0]
        for k in range(1, 8):
            total = total + p_ref[k]
        for i, name in enumerate(SMALL):
            g = total[i:i + 1, 0:SMALL_SIZES[name]]
            d, mn, vn = _adamw_update(ins[i][...], g, ins[n_small + i][...], ins[2 * n_small + i][...])
            for slot, val in enumerate((g, d, mn, vn)):
                outs[slot * n_small + i][...] = val

    vm = pl.BlockSpec(memory_space=pltpu.VMEM)
    shapes = [SDS((1, SMALL_SIZES[n]), F32) for n in SMALL]
    outs = pl.pallas_call(
        body, name="small_adamw", in_specs=[vm] * (1 + 3 * n_small), out_specs=[vm] * (4 * n_small),
        out_shape=shapes * 4)(parts, *[w[n] for n in SMALL], *[m[n] for n in SMALL], *[v[n] for n in SMALL])
    return [dict(zip(SMALL, outs[s * n_small:(s + 1) * n_small])) for s in range(4)]


def _adamw(w, g, m, v, name):
    rows, cols = w.shape
    tr = rows
    for cand in (512, 352, 256):
        if rows > cand and rows % cand == 0:
            tr = cand
            break

    def body(w_ref, g_ref, m_ref, v_ref, d_ref, mo_ref, vo_ref):
        d_ref[...], mo_ref[...], vo_ref[...] = _adamw_update(w_ref[...], g_ref[...], m_ref[...], v_ref[...])

    spec = pl.BlockSpec((tr, cols), lambda i: (i, 0))
    return pl.pallas_call(
        body, name=name, grid=(rows // tr,), in_specs=[spec] * 4, out_specs=[spec] * 3,
        out_shape=[SDS((rows, cols), F32)] * 3, compiler_params=_params())(w, g, m, v)


SMALL_SIZES = {"mix_norm_w": 1024, "b_gk": 256, "gla_norm_w": 128, "sb_norm_w": 64, "xattn_norm_w": 1024,
               "mem_norm_w": 1024, "mq_norm_w": 256, "mk_norm_w": 256, "ffn_norm_w": 1024}


ROWS_OF = (("w_out", 256), ("w_mq", 256), ("w_mo", 256), ("w_down", 704))
WIN_ROWS = 1056
LATER = ("rows", "gate_up", "mkv")


def _shard_buffers(d, dtype):
    rows = jnp.concatenate([d[n] for n, _ in ROWS_OF], axis=0).astype(dtype)
    gk = jnp.pad(d["w_gk_up"], ((0, WIN_ROWS - D - 16), (0, DIN // 4 - 64)))
    win = jnp.concatenate([d["w_in"], gk], axis=0).astype(dtype)
    return [rows, d["w_gate_up"].astype(dtype), d["w_mkv"].astype(dtype)], win


def _in_halves(a):
    return a.reshape(a.shape[:-2] + (2, a.shape[-2] // 2, a.shape[-1]))


def _whole(a):
    return a.reshape(a.shape[:-3] + (2 * a.shape[-2], a.shape[-1]))


def _first_weights(win):
    w_in = win[:, :D].transpose(1, 0, 2).reshape(D, DIN)
    w_in = jnp.concatenate([w_in[:, :1536], w_in[:, 1552:], w_in[:, 1536:1552],
                            jnp.zeros((D, DIN_P - DIN), w_in.dtype)], axis=1)
    return {"w_in": w_in, "w_gk_up": win[:, D:D + 16, :64].transpose(1, 0, 2).reshape(16, 256)}


def _later_weights(rows, gate_up, mkv):
    out, off = {"w_gate_up": gate_up, "w_mkv": mkv}, 0
    for n, r in ROWS_OF:
        out[n] = rows[:, off:off + r].reshape(4 * r, 1024)
        off += r
    return out


def _later_grad_buffers(g):
    rows = jnp.concatenate([g[n].reshape(4, r, 1024) for n, r in ROWS_OF], axis=1)
    return [rows, g["w_gate_up"], g["w_mkv"]]


def _win_grad_buffer(g):
    gk = g["w_gk_up"].astype(BF16).reshape(16, 4, 64).transpose(1, 0, 2)
    gk = jnp.pad(gk, ((0, 0), (0, WIN_ROWS - D - 16), (0, DIN // 4 - 64)))
    gi = g["w_in"]
    gi = jnp.concatenate([gi[:, :1536], gi[:, C_LR:C_LR + 16], gi[:, 1536:C_LR]], axis=1)
    return jnp.concatenate([gi.reshape(D, 4, DIN // 4).transpose(1, 0, 2), gk], axis=1)


def _shard_grads(rows, gate_up, mkv, win):
    out, off = {"w_gate_up": gate_up, "w_mkv": mkv, "w_in": win[:D], "w_gk_up": win[D:D + 16, :64]}, 0
    for n, r in ROWS_OF:
        out[n] = rows[off:off + r]
        off += r
    return out


def kernel(x, mem, mix_norm_w, w_in, w_gk_up, b_gk, gla_norm_w, sb_norm_w, w_out, xattn_norm_w, mem_norm_w, w_mq, w_mkv, mq_norm_w, mk_norm_w, w_mo, ffn_norm_w, w_gate_up, w_down, loss_target, m_mix_norm_w, m_w_in, m_w_gk_up, m_b_gk, m_gla_norm_w, m_sb_norm_w, m_w_out, m_xattn_norm_w, m_mem_norm_w, m_w_mq, m_w_mkv, m_mq_norm_w, m_mk_norm_w, m_w_mo, m_ffn_norm_w, m_w_gate_up, m_w_down, v_mix_norm_w, v_w_in, v_w_gk_up, v_b_gk, v_gla_norm_w, v_sb_norm_w, v_w_out, v_xattn_norm_w, v_mem_norm_w, v_w_mq, v_w_mkv, v_mq_norm_w, v_mk_norm_w, v_w_mo, v_ffn_norm_w, v_w_gate_up, v_w_down):
    args = dict(locals())
    wts = {n: args[n][0] if n in BIG else args[n] for n in WEIGHTS}
    mom = {n: args["m_" + n][0] if n in BIG else args["m_" + n] for n in WEIGHTS}
    var = {n: args["v_" + n][0] if n in BIG else args["v_" + n] for n in WEIGHTS}

    c = lax.axis_index("c")
    chip = 2 * lax.axis_index("x") + lax.axis_index("y")
    c_arr = c.astype(jnp.int32).reshape(1)
    chip_arr = chip.astype(jnp.int32).reshape(1)
    def own_slot_filled(gathered, mine):
        return [_whole(lax.dynamic_update_slice(got, wp[None], (chip, 0, 0, 0))) for got, wp in zip(gathered, mine)]

    def chip_partials(names, buffers, tag):
        gps = [_in_halves(b) for b in buffers]
        return [_rs_add_halves(gp, other, c_arr, "rs_add_halves_" + n)
                for n, gp, other in zip(names, gps, _rs_swap_halves(gps, "rs_swap_halves_" + tag))]

    later_wps, win_wp = _shard_buffers(wts, BF16)
    later_wps, win_wp = [_in_halves(b) for b in later_wps], _in_halves(win_wp)
    first = _first_weights(*own_slot_filled(_ag_weights([win_wp]), [win_wp]))
    first.update({n: wts[n] for n in SMALL})

    loss, grad_x, g, (parts, received) = _local_step(
        x[0], mem[0], loss_target[0], first,
        later_weights=(later_wps, lambda gathered: _later_weights(*own_slot_filled(gathered, later_wps))),
        later_partials=lambda g: chip_partials(LATER, _later_grad_buffers(g), "later"),
        last_partials=lambda g: chip_partials(("win",), [_win_grad_buffer(g)], "win"))
    loss = lax.psum(loss, ("x", "y", "c"))

    mine = [_rs_add_chips(recv, part, chip_arr, "rs_add_chips_" + n)
            for n, recv, part in zip(LATER + ("win",), received, parts)]
    totals = [jnp.concatenate([jnp.where(c == 0, m, t), jnp.where(c == 0, t, m)], axis=0)
              for m, t in zip(mine, _rs_share(mine))]
    grads = _shard_grads(*totals)

    small_grads, delta, new_m, new_v = _small_update(_small_gather(_small_rows(g)), wts, mom, var)
    grads.update(small_grads)
    for n in BIG:
        w2 = wts[n].reshape(-1, wts[n].shape[-1])
        d_, m_, v_ = _adamw(w2, grads[n].reshape(w2.shape), mom[n].reshape(w2.shape), var[n].reshape(w2.shape),
                            "adamw_" + n)
        delta[n], new_m[n], new_v[n] = (t.reshape((1,) + wts[n].shape) for t in (d_, m_, v_))
        grads[n] = grads[n].reshape((1,) + wts[n].shape)

    return (loss, grad_x[None], *[grads[n] for n in WEIGHTS], *[delta[n] for n in WEIGHTS],
            *[new_m[n] for n in WEIGHTS], *[new_v[n] for n in WEIGHTS])
```

```python
import functools
import math

import numpy as np
import jax
import jax.numpy as jnp
from jax import lax
from jax.experimental import pallas as pl
from jax.experimental.pallas import tpu as pltpu

F32 = jnp.float32
BF16 = jnp.bfloat16
SDS = jax.ShapeDtypeStruct
MESH = pl.DeviceIdType.MESH

D = 1024
EPS = 1e-6
D_FF = 2816
GLA_GATE_NORM = 16.0
GLA_C = 64
MEM_HEADS = 4
MEM_HD = 256
C_QG, C_KG, C_VG, C_GG, C_QS, C_KS, C_VS, C_LR = 0, 256, 512, 1024, 1536, 2048, 2560, 3072
DIN = 3088
DIN_P = 3200
TB = 512
SBQ = 256
VMEM_LIMIT = 56 * 1024 * 1024
HIGHEST = lax.Precision.HIGHEST

ADAM_LR, ADAM_B1, ADAM_B2, ADAM_EPS, ADAM_WD, ADAM_STEP = 0.001, 0.9, 0.999, 1e-08, 0.01, 10

BIG = ("w_in", "w_gk_up", "w_out", "w_mq", "w_mkv", "w_mo", "w_gate_up", "w_down")
SMALL = ("mix_norm_w", "b_gk", "gla_norm_w", "sb_norm_w", "xattn_norm_w", "mem_norm_w", "mq_norm_w",
         "mk_norm_w", "ffn_norm_w")
WEIGHTS = ("mix_norm_w", "w_in", "w_gk_up", "b_gk", "gla_norm_w", "sb_norm_w", "w_out", "xattn_norm_w",
           "mem_norm_w", "w_mq", "w_mkv", "mq_norm_w", "mk_norm_w", "w_mo", "ffn_norm_w", "w_gate_up", "w_down")
SMALL_ROWS = 16


def _params(**kw):
    return pltpu.CompilerParams(vmem_limit_bytes=VMEM_LIMIT, **kw)


def _row(c, j=0):
    return pl.BlockSpec((TB, c), lambda i, j=j: (i, j))


def _const(shape):
    return pl.BlockSpec(shape, lambda i: (0,) * len(shape))


def _dot(a, b):
    return lax.dot_general(a, b, (((1,), (0,)), ((), ())), preferred_element_type=F32)


def _dot_nt(a, b):
    return lax.dot_general(a, b, (((1,), (1,)), ((), ())), preferred_element_type=F32)


def _dot_tn(a, b):
    return lax.dot_general(a, b, (((0,), (0,)), ((), ())), preferred_element_type=F32)


def _dot_nt_f32(a, b):
    return lax.dot_general(a, b, (((1,), (1,)), ((), ())), precision=HIGHEST, preferred_element_type=F32)


def _split3(x):
    h = x.astype(BF16)
    r = x - h.astype(F32)
    m = r.astype(BF16)
    l = (r - m.astype(F32)).astype(BF16)
    return h, m, l


def _dot_exact(x, ones_mat):
    h, m, l = _split3(x)
    return _dot(h, ones_mat) + _dot(m, ones_mat) + _dot(l, ones_mat)


def _softplus(z):
    return jnp.maximum(z, 0.0) + jnp.log1p(jnp.exp(-jnp.abs(z)))


def _rsqrt_ms(x):
    return lax.rsqrt(jnp.mean(x * x, axis=-1, keepdims=True) + EPS)


def _colsum8(x):
    r, c = x.shape
    return jnp.sum(x.reshape(r // 8, 8, c), axis=0)


def _matmul(a, b, *, mode, tm, tn, tk=None, res=None, out_dtype=F32, by_column_tile=False, a_spec=None,
            b_spec=None, mnk=None, epilogue=None, exchange=None, name):
    if mnk is not None:
        M, N, K = mnk
    else:
        K, M = a.shape if mode == "tn" else a.shape[::-1]
        N = b.shape[0] if mode == "nt" else b.shape[1]
    tk = K if tk is None else tk
    assert M % tm == 0 and N % tn == 0 and K % tk == 0, (name, M, N, K, tm, tn, tk)
    nk = K // tk
    if a_spec is None:
        if mode == "tn":
            a_spec = pl.BlockSpec((tk, tm), lambda j, i, k: (k, i))
        else:
            a_spec = pl.BlockSpec((tm, tk), lambda j, i, k: (i, k))
    if b_spec is None:
        if mode == "nt":
            b_spec = pl.BlockSpec((tn, tk), lambda j, i, k: (j, k))
        else:
            b_spec = pl.BlockSpec((tk, tn), lambda j, i, k: (k, j))
    if by_column_tile:
        assert res is None
        o_spec = pl.BlockSpec((None, tm, tn), lambda j, i, k: (j, i, 0))
        o_shape = SDS((N // tn, M, tn), out_dtype)
    else:
        o_spec = pl.BlockSpec((tm, tn), lambda j, i, k: (i, j))
        o_shape = SDS((M, N), out_dtype)
    dot = {"nn": _dot, "nt": _dot_nt, "tn": _dot_tn}[mode]
    has_res = res is not None
    extra_in, extra_out, finish, *lhs_from = epilogue if epilogue is not None else ((), (), None)
    n_in, n_out = 2 + has_res + len(extra_in), 1 + len(extra_out)
    sent = list(exchange) if exchange is not None else []
    ns = len(sent)
    grid = (N // tn, M // tm, nk)

    def body(*refs):
        a_ref, b_ref = refs[0], refs[1]
        res_ref = refs[2] if has_res else None
        in_refs, out_refs = refs[2 + has_res:n_in], refs[n_in + ns:n_in + ns + n_out]
        first_row_tile = pl.program_id(1) == 0
        scratch = refs[n_in + 2 * ns + n_out:]
        if ns:
            ids = [pl.program_id(d) for d in range(3)]
            rs_start, rs_finish = _rs_exchange_phases(
                refs[n_in:n_in + ns], refs[n_in + ns + n_out:n_in + 2 * ns + n_out], *scratch[nk > 1:])
            pl.when((ids[0] == 0) & (ids[1] == 0) & (ids[2] == 0))(rs_start)

        def done(t):
            if has_res:
                t = t + res_ref[...]
            if finish is None:
                out_refs[0][...] = t.astype(out_dtype)
            else:
                finish(t, first_row_tile, in_refs, out_refs)

        lhs = lhs_from[0](a_ref[...], in_refs, out_refs) if lhs_from else a_ref[...].astype(BF16)
        p = dot(lhs, b_ref[...].astype(BF16))
        if nk == 1:
            done(p)
        else:
            acc_ref = scratch[0]
            k = pl.program_id(2)

            @pl.when(k == 0)
            def _():
                acc_ref[...] = p

            @pl.when(k > 0)
            def _():
                acc_ref[...] += p

            @pl.when(k == nk - 1)
            def _():
                done(acc_ref[...])

        if ns:
            pl.when((ids[0] == grid[0] - 1) & (ids[1] == grid[1] - 1) & (ids[2] == grid[2] - 1))(rs_finish)

    in_specs = [a_spec, b_spec] + ([o_spec] if has_res else []) + [s for _, s in extra_in] + [HBM] * ns
    args = (a, b) + ((res,) if has_res else ()) + tuple(x for x, _ in extra_in) + tuple(sent)
    outs = pl.pallas_call(
        body, name=name, grid=grid, in_specs=in_specs,
        out_specs=[o_spec] + [s for _, s in extra_out] + [HBM] * ns,
        out_shape=[o_shape] + [s for s, _ in extra_out] + [SDS(p.shape, p.dtype) for p in sent],
        scratch_shapes=([pltpu.VMEM((tm, tn), F32)] if nk > 1 else []) + (_rs_exchange_sems(ns) if ns else []),
        compiler_params=_params(
            dimension_semantics=("arbitrary",) * 3 if ns else ("parallel", "parallel", "arbitrary")),
    )(*args)
    return outs[0] if epilogue is None and not ns else outs


def _full_row(tm, c):
    return pl.BlockSpec((tm, c), lambda j, i, k: (i, 0))


def _kept(shape):
    return pl.BlockSpec(shape, lambda j, i, k: (0,) * len(shape))


def _then_norm_fwd(w, T, tm):
    dm = w.shape[1]

    def finish(t, first, ins, outs):
        outs[0][...] = t
        outs[1][...] = (t * _rsqrt_ms(t) * ins[0][...]).astype(BF16)

    return [(w, _kept((1, dm)))], [(SDS((T, dm), BF16), _full_row(tm, dm))], finish


def _then_norm_bwd(x, w, dres, tm):
    T, dm = x.shape

    def finish(t, first, ins, outs):
        x_ref, w_ref, dres_ref = ins

        @pl.when(first)
        def _():
            outs[1][...] = jnp.zeros_like(outs[1])

        xv = x_ref[...]
        r = _rsqrt_ms(xv)
        n = xv * r
        dn = t * w_ref[...]
        outs[0][...] = dres_ref[...] + r * (dn - n * jnp.mean(dn * n, axis=-1, keepdims=True))
        outs[1][...] += _colsum8(t * n)

    return ([(x, _full_row(tm, dm)), (w, _kept((1, dm))), (dres, _full_row(tm, dm))],
            [(SDS((8, dm), F32), _kept((8, dm)))], finish)


def _then_loss(tgt, tm):
    def finish(t, first, ins, outs):
        @pl.when(first)
        def _():
            outs[1][...] = jnp.zeros_like(outs[1])

        e = t - ins[0][...]
        outs[0][...] = e * (1.0 / D)
        outs[1][...] += _colsum8(e * e) * (0.5 / D)

    return [(tgt, _full_row(tm, D))], [(SDS((8, D), F32), _kept((8, D)))], finish


def _norm_then_proj_split(norm_w, wgk, bgk, T, tm):
    def lhs_from(xv, ins, outs):
        h = (xv * _rsqrt_ms(xv) * ins[2][...]).astype(BF16)
        outs[5][...] = h
        return h

    def finish(t, first, ins, outs):
        wgk_ref, b_ref = ins[:2]
        proj_ref, gk_ref, qs_ref, ks_ref, vs_ref = outs[:5]
        proj_ref[...] = t
        u = _dot(t[:, C_LR:DIN_P].astype(BF16), wgk_ref[...].astype(BF16)) + b_ref[...]
        gk_ref[...] = -_softplus(-u) / GLA_GATE_NORM
        qs_ref[...] = (t[:, C_QS:C_KS] * 0.125).astype(BF16)
        ks_ref[...] = t[:, C_KS:C_VS].astype(BF16)
        vs_ref[...] = t[:, C_VS:C_LR].astype(BF16)

    return ([(wgk, _kept((128, 256))), (bgk, _kept((1, 256))), (norm_w, _kept((1, D)))],
            [(SDS((T, 256), F32), _full_row(tm, 256))] + [(SDS((T, 512), BF16), _full_row(tm, 512))] * 3
            + [(SDS((T, D), BF16), _full_row(tm, D))], finish, lhs_from)


def _dproj_assemble(proj, dq_g, dk_g, dv_g, dg_g, dq_s, dk_s, dv_s, dgk, wgk, bgk):
    T = proj.shape[0]

    def body(lr_ref, dqg_ref, dkg_ref, dvg_ref, dgg_ref, dqs_ref, dks_ref, dvs_ref, dgk_ref, wgk_ref, b_ref,
             dp_ref, dwgk_ref, dbgk_ref):
        @pl.when(pl.program_id(0) == 0)
        def _():
            dwgk_ref[...] = jnp.zeros_like(dwgk_ref)
            dbgk_ref[...] = jnp.zeros_like(dbgk_ref)

        lr = lr_ref[...].astype(BF16)
        wg = wgk_ref[...].astype(BF16)
        u = _dot(lr, wg) + b_ref[...]
        du = dgk_ref[...] * (jax.nn.sigmoid(-u) / GLA_GATE_NORM)
        dub = du.astype(BF16)
        dp_ref[:, C_QG:C_KG] = (dqg_ref[...] * 0.125).astype(BF16)
        dp_ref[:, C_KG:C_VG] = dkg_ref[...].astype(BF16)
        dp_ref[:, C_VG:C_GG] = dvg_ref[...].astype(BF16)
        dp_ref[:, C_GG:C_QS] = dgg_ref[...].astype(BF16)
        dp_ref[:, C_QS:C_KS] = (dqs_ref[...] * 0.125).astype(BF16)
        dp_ref[:, C_KS:C_VS] = dks_ref[...].astype(BF16)
        dp_ref[:, C_VS:C_LR] = dvs_ref[...].astype(BF16)
        dp_ref[:, C_LR:DIN_P] = _dot_nt(dub, wg).astype(BF16)
        dwgk_ref[...] += _dot_tn(lr, dub)
        dbgk_ref[...] += _colsum8(du)

    return pl.pallas_call(
        body, name="dproj_assemble", grid=(T // TB,),
        in_specs=[_row(128, C_LR // 128), _row(256), _row(256), _row(512), _row(512), _row(512), _row(512),
                  _row(512), _row(256), _const((128, 256)), _const((1, 256))],
        out_specs=[_row(DIN_P), _const((128, 256)), _const((8, 256))],
        out_shape=[SDS((T, DIN_P), BF16), SDS((128, 256), F32), SDS((8, 256), F32)],
        compiler_params=_params())(proj, dq_g, dk_g, dv_g, dg_g, dq_s, dk_s, dv_s, dgk, wgk, bgk)


def _group_ones(n, g):
    idx = np.arange(n) // g
    return jnp.asarray((idx[:, None] == idx[None, :]).astype(np.float32), dtype=BF16)


def _mix_cat(o_g, proj, o_s, wg512, ws512, bd64):
    T = o_g.shape[0]

    def body(og_ref, gg_ref, os_ref, wg_ref, ws_ref, bd_ref, cat_ref):
        og = og_ref[...]
        gg = gg_ref[...]
        s = gg * jax.nn.sigmoid(gg)
        for h in range(4):
            sl = slice(128 * h, 128 * (h + 1))
            x = og[:, sl]
            cat_ref[:, sl] = (x * _rsqrt_ms(x) * wg_ref[:, sl] * s[:, sl]).astype(BF16)
        osv = os_ref[...]
        ms = _dot_exact(osv * osv, bd_ref[...]) * (1.0 / 64.0)
        cat_ref[:, 512:1024] = (osv * lax.rsqrt(ms + EPS) * ws_ref[...]).astype(BF16)

    return pl.pallas_call(
        body, name="mix_cat", grid=(T // TB,),
        in_specs=[_row(512), _row(512, C_GG // 512), _row(512), _const((1, 512)), _const((1, 512)),
                  _const((512, 512))],
        out_specs=_row(1024), out_shape=SDS((T, 1024), BF16), compiler_params=_params())(
            o_g, proj, o_s, wg512, ws512, bd64)


def _then_mix_cat_bwd(o_g, proj, o_s, wg512, ws512, bd64, tm):
    T = o_g.shape[0]

    def finish(t, first, ins, outs):
        og_ref, gg_ref, os_ref, wg_ref, ws_ref, bd_ref = ins
        do_ref, dgg_ref, dwg_ref, dws_ref = outs

        @pl.when(first)
        def _():
            dwg_ref[...] = jnp.zeros_like(dwg_ref)
            dws_ref[...] = jnp.zeros_like(dws_ref)

        og = og_ref[...]
        gg = gg_ref[...]
        sg = jax.nn.sigmoid(gg)
        s = gg * sg
        ds = sg * (1.0 + gg * (1.0 - sg))
        for h in range(4):
            sl = slice(128 * h, 128 * (h + 1))
            x = og[:, sl]
            r = _rsqrt_ms(x)
            n = x * r
            w = wg_ref[:, sl]
            dc = t[:, sl]
            dy = dc * s[:, sl]
            dgg_ref[:, sl] = dc * (n * w) * ds[:, sl]
            dn = dy * w
            do_ref[:, sl] = r * (dn - n * jnp.mean(dn * n, axis=-1, keepdims=True))
            dwg_ref[:, sl] += _colsum8(dy * n)
        osv = os_ref[...]
        bd = bd_ref[...]
        r = lax.rsqrt(_dot_exact(osv * osv, bd) * (1.0 / 64.0) + EPS)
        n = osv * r
        dc = t[:, 512:1024]
        dn = dc * ws_ref[...]
        do_ref[:, 512:1024] = r * (dn - n * (_dot_exact(dn * n, bd) * (1.0 / 64.0)))
        dws_ref[...] += _colsum8(dc * n)

    half = pl.BlockSpec((tm, 512), lambda j, i, k: (i, 0))
    return ([(o_g, half), (proj, pl.BlockSpec((tm, 512), lambda j, i, k: (i, C_GG // 512))), (o_s, half),
             (wg512, _kept((1, 512))), (ws512, _kept((1, 512))), (bd64, _kept((512, 512)))],
            [(SDS((T, 512), F32), half), (SDS((8, 512), F32), _kept((8, 512))),
             (SDS((8, 512), F32), _kept((8, 512)))], finish)


FF_TN = 1408
DW_TK = 2048


def _gate_up_act(h, w):
    T = h.shape[0]
    nj = D_FF // FF_TN

    def body(h_ref, wg_ref, wu_ref, gu_ref, a_ref):
        hv = h_ref[...]
        g = _dot(hv, wg_ref[...])
        u = _dot(hv, wu_ref[...])
        gu_ref[0] = g.astype(BF16)
        gu_ref[1] = u.astype(BF16)
        a_ref[...] = (g * jax.nn.sigmoid(g) * u).astype(BF16)

    return pl.pallas_call(
        body, name="mm_gate_up_act", grid=(nj, T // TB),
        in_specs=[pl.BlockSpec((TB, D), lambda j, i: (i, 0)),
                  pl.BlockSpec((None, D, FF_TN), lambda j, i: (j, 0, 0)),
                  pl.BlockSpec((None, D, FF_TN), lambda j, i: (nj + j, 0, 0))],
        out_specs=[pl.BlockSpec((2, TB, FF_TN), lambda j, i: (0, i, j)),
                   pl.BlockSpec((TB, FF_TN), lambda j, i: (i, j))],
        out_shape=[SDS((2, T, D_FF), BF16), SDS((T, D_FF), BF16)],
        compiler_params=_params(dimension_semantics=("parallel", "parallel")))(h, w, w)


def _down_bwd(dy, w_down, gu):
    T = dy.shape[0]
    nj = D_FF // FF_TN

    def body(dy_ref, w_ref, gu_ref, dgu_ref):
        da = _dot_nt(dy_ref[...].astype(BF16), w_ref[...])
        g = gu_ref[0].astype(F32)
        sg = jax.nn.sigmoid(g)
        dgu_ref[0] = (da * gu_ref[1].astype(F32) * (sg * (1.0 + g * (1.0 - sg)))).astype(BF16)
        dgu_ref[1] = (da * (g * sg)).astype(BF16)

    return pl.pallas_call(
        body, name="mm_down_bwd", grid=(nj, T // TB),
        in_specs=[pl.BlockSpec((TB, D), lambda j, i: (i, 0)),
                  pl.BlockSpec((FF_TN, D), lambda j, i: (j, 0)),
                  pl.BlockSpec((2, TB, FF_TN), lambda j, i: (0, i, j))],
        out_specs=pl.BlockSpec((2, TB, FF_TN), lambda j, i: (0, i, j)),
        out_shape=SDS((2, T, D_FF), BF16),
        compiler_params=_params(dimension_semantics=("parallel", "parallel")))(dy, w_down, gu)


def _gla_consts():
    c = GLA_C
    L = np.tril(np.ones((c, c), np.float32))
    blocks = [L, L[(np.arange(c) // 16) * 16]]
    blocks += [np.repeat(L[16 * i:16 * i + 1], c, axis=0) for i in range(4)]
    blocks.append(np.repeat(L[c - 1:c], c, axis=0))
    return jnp.asarray(np.concatenate(blocks, axis=0))


@jax.custom_vjp
def _gla_lin(cm, g):
    cb = cm.astype(BF16)
    h, m, l = _split3(g)
    y = _dot(cb, h) + _dot(cb, m) + _dot(cb, l)
    return tuple(y[GLA_C * n:GLA_C * (n + 1)] for n in range(7))


def _gla_lin_fwd(cm, g):
    return _gla_lin(cm, g), cm


def _gla_lin_bwd(cm, cts):
    cb = cm.astype(BF16)
    h, m, l = _split3(jnp.concatenate(cts, axis=0))
    return None, _dot_tn(cb, h) + _dot_tn(cb, m) + _dot_tn(cb, l)


_gla_lin.defvjp(_gla_lin_fwd, _gla_lin_bwd)


GLA_SUB = 16
GLA_H = 4
GLA_W = 64 * GLA_H


def _head_lanes(rows, h):
    lane = lax.broadcasted_iota(jnp.int32, (rows, GLA_W), 1)
    return (lane >= 64 * h) & (lane < 64 * (h + 1))


def _gla_scores_terms(qs, k, b, rs, blk):
    row = lax.broadcasted_iota(jnp.int32, (GLA_C, GLA_W), 0)
    keep = row < GLA_SUB * (blk + 1)
    e = jnp.where(keep, jnp.exp(jnp.where(keep, rs[blk] - b, 0.0)), 0.0)
    qb = qs[GLA_SUB * blk:GLA_SUB * (blk + 1)]
    lhs = jnp.concatenate([jnp.where(_head_lanes(GLA_SUB, h), qb, 0.0) for h in range(GLA_H)], axis=0)
    return lhs, e


@jax.custom_vjp
def _gla_scores(qs, k, b, r0, r1, r2, r3):
    rs = (r0, r1, r2, r3)
    per_head = [[] for _ in range(GLA_H)]
    for blk in range(GLA_C // GLA_SUB):
        lhs, e = _gla_scores_terms(qs, k, b, rs, blk)
        a = _dot_nt_f32(lhs, k * e)
        for h in range(GLA_H):
            per_head[h].append(a[GLA_SUB * h:GLA_SUB * (h + 1)])
    return tuple(jnp.concatenate(p, axis=0) for p in per_head)


def _gla_scores_fwd(qs, k, b, r0, r1, r2, r3):
    return _gla_scores(qs, k, b, r0, r1, r2, r3), (qs, k, b, r0, r1, r2, r3)


def _gla_scores_bwd(saved, cts):
    qs, k, b = saved[:3]
    rs = saved[3:]
    dqs, drs = [], []
    dk = jnp.zeros_like(k)
    db = jnp.zeros_like(b)
    for blk in range(GLA_C // GLA_SUB):
        lhs, e = _gla_scores_terms(qs, k, b, rs, blk)
        rows = slice(GLA_SUB * blk, GLA_SUB * (blk + 1))
        da = jnp.concatenate([ct[rows] for ct in cts], axis=0)
        dlhs = lax.dot_general(da, k * e, (((1,), (0,)), ((), ())), precision=HIGHEST, preferred_element_type=F32)
        dq = jnp.zeros((GLA_SUB, GLA_W), F32)
        for h in range(GLA_H):
            dq = jnp.where(_head_lanes(GLA_SUB, h), dlhs[GLA_SUB * h:GLA_SUB * (h + 1)], dq)
        dqs.append(dq)
        dks = lax.dot_general(da, lhs, (((0,), (0,)), ((), ())), precision=HIGHEST, preferred_element_type=F32)
        dk = dk + dks * e
        darg = dks * (k * e)
        db = db - darg
        drs.append(darg)
    return (jnp.concatenate(dqs, axis=0), dk, db, *drs)


_gla_scores.defvjp(_gla_scores_fwd, _gla_scores_bwd)


def _gla_chunk(cm, q, k, g, v0, v1, v2, v3, st):
    c = GLA_C
    vs = (v0, v1, v2, v3)
    ri = lax.broadcasted_iota(jnp.int32, (c, c), 0)
    ci = lax.broadcasted_iota(jnp.int32, (c, c), 1)
    causal = ci <= ri
    b, r, r0, r1, r2, r3, bl = _gla_lin(cm, g)
    scores = _gla_scores(q * jnp.exp(b - r), k, b, r0, r1, r2, r3)
    qe = q * jnp.exp(b)
    kd = k * jnp.exp(bl - b)
    st_new = st * jnp.exp(jnp.concatenate([bl, bl], axis=0))
    outs = []
    for h in range(GLA_H):
        a = jnp.where(causal, scores[h], 0.0)
        outs.append(_dot_nt(jnp.where(_head_lanes(c, h), qe, 0.0), st) + _dot(a, vs[h]))
        st_new = st_new + jnp.where(_head_lanes(2 * c, h), _dot_tn(vs[h], kd), 0.0)
    return (*outs, st_new)


GLA_TB = 512


def _gla_fwd(proj, gk, cm):
    T = proj.shape[0]
    nb = T // GLA_TB
    nc = GLA_TB // GLA_C

    def body(q_ref, k_ref, v_ref, g_ref, cm_ref, o_ref, st_ref, st_scr):
        @pl.when(pl.program_id(0) == 0)
        def _():
            st_scr[...] = jnp.zeros_like(st_scr)

        cmv = cm_ref[...]

        def chunk(ci, carry):
            rs = pl.ds(pl.multiple_of(ci * GLA_C, GLA_C), GLA_C)
            st = st_scr[...]
            st_ref[ci] = st
            *outs, st_new = _gla_chunk(cmv, q_ref[rs, :] * 0.125, k_ref[rs, :], g_ref[rs, :],
                                       *[v_ref[rs, 128 * h:128 * (h + 1)] for h in range(GLA_H)], st)
            for h in range(GLA_H):
                o_ref[rs, 128 * h:128 * (h + 1)] = outs[h]
            st_scr[...] = st_new
            return carry

        lax.fori_loop(0, nc, chunk, 0, unroll=True)

    return pl.pallas_call(
        body, name="gla_fwd", grid=(nb,),
        in_specs=[pl.BlockSpec((GLA_TB, 256), lambda i: (i, C_QG // 256)),
                  pl.BlockSpec((GLA_TB, 256), lambda i: (i, C_KG // 256)),
                  pl.BlockSpec((GLA_TB, 512), lambda i: (i, C_VG // 512)),
                  pl.BlockSpec((GLA_TB, 256), lambda i: (i, 0)),
                  pl.BlockSpec((7 * GLA_C, GLA_C), lambda i: (0, 0))],
        out_specs=[pl.BlockSpec((GLA_TB, 512), lambda i: (i, 0)),
                   pl.BlockSpec((nc, 128, GLA_W), lambda i: (i, 0, 0))],
        out_shape=[SDS((T, 512), F32), SDS((T // GLA_C, 128, GLA_W), F32)],
        scratch_shapes=[pltpu.VMEM((128, GLA_W), F32)],
        compiler_params=_params(dimension_semantics=("arbitrary",)))(proj, proj, proj, gk, cm)


def _gla_bwd(proj, gk, cm, states, do, parts=()):
    T = proj.shape[0]
    nb = T // GLA_TB
    nc = GLA_TB // GLA_C
    n = len(parts)

    def body(q_ref, k_ref, v_ref, g_ref, cm_ref, st_ref, do_ref, *rest):
        p_refs, (dq_ref, dk_ref, dv_ref, dg_ref), out_refs = rest[:n], rest[n:n + 4], rest[n + 4:2 * n + 4]
        dst_scr = rest[2 * n + 4]
        step = pl.program_id(0)
        if n:
            rs_start, rs_finish = _rs_exchange_phases(p_refs, out_refs, *rest[2 * n + 5:])

        @pl.when(step == 0)
        def _():
            dst_scr[...] = jnp.zeros_like(dst_scr)
            if n:
                rs_start()

        cmv = cm_ref[...]

        def chunk(t, carry):
            ci = nc - 1 - t
            rs = pl.ds(pl.multiple_of(ci * GLA_C, GLA_C), GLA_C)
            _, vjp = jax.vjp(
                functools.partial(_gla_chunk, cmv), q_ref[rs, :] * 0.125, k_ref[rs, :], g_ref[rs, :],
                *[v_ref[rs, 128 * h:128 * (h + 1)] for h in range(GLA_H)], st_ref[ci])
            dq, dk, dg, *dvs, dst = vjp((*[do_ref[rs, 128 * h:128 * (h + 1)] for h in range(GLA_H)], dst_scr[...]))
            dq_ref[rs, :] = dq
            dk_ref[rs, :] = dk
            dg_ref[rs, :] = dg
            for h in range(GLA_H):
                dv_ref[rs, 128 * h:128 * (h + 1)] = dvs[h]
            dst_scr[...] = dst
            return carry

        lax.fori_loop(0, nc, chunk, 0, unroll=True)

        if n:
            pl.when(step == nb - 1)(rs_finish)

    rev = lambda i: nb - 1 - i
    return pl.pallas_call(
        body, name="gla_bwd", grid=(nb,),
        in_specs=[pl.BlockSpec((GLA_TB, 256), lambda i: (rev(i), C_QG // 256)),
                  pl.BlockSpec((GLA_TB, 256), lambda i: (rev(i), C_KG // 256)),
                  pl.BlockSpec((GLA_TB, 512), lambda i: (rev(i), C_VG // 512)),
                  pl.BlockSpec((GLA_TB, 256), lambda i: (rev(i), 0)),
                  pl.BlockSpec((7 * GLA_C, GLA_C), lambda i: (0, 0)),
                  pl.BlockSpec((nc, 128, GLA_W), lambda i: (rev(i), 0, 0)),
                  pl.BlockSpec((GLA_TB, 512), lambda i: (rev(i), 0))] + [HBM] * n,
        out_specs=[pl.BlockSpec((GLA_TB, 256), lambda i: (rev(i), 0)),
                   pl.BlockSpec((GLA_TB, 256), lambda i: (rev(i), 0)),
                   pl.BlockSpec((GLA_TB, 512), lambda i: (rev(i), 0)),
                   pl.BlockSpec((GLA_TB, 256), lambda i: (rev(i), 0))] + [HBM] * n,
        out_shape=[SDS((T, 256), F32), SDS((T, 256), F32), SDS((T, 512), F32), SDS((T, 256), F32)]
        + [SDS(p.shape, p.dtype) for p in parts],
        scratch_shapes=[pltpu.VMEM((128, GLA_W), F32)] + (_rs_exchange_sems(n) if n else []),
        compiler_params=_params(dimension_semantics=("arbitrary",)))(proj, proj, proj, gk, cm, states, do, *parts)


SB_DEAD = 105.0
SB_COUNT_LANE = 127


def _sb_tri():
    i = np.arange(SBQ)
    return jnp.asarray((i[:, None] > i[None, :]).astype(np.float32), dtype=BF16)


def _sb_block_fwd(qh, kb, tri, carry, strict):
    z = _dot_nt(qh, kb)
    sp = _softplus(z)
    l1 = -sp
    if strict is not None:
        l1 = jnp.where(strict, l1, 0.0)
    log_a = (z - sp) + _dot(l1.astype(BF16), tri) + carry
    a = jnp.exp(log_a)
    if strict is not None:
        a = jnp.where(strict, a, 0.0)
    return z - sp, l1, a


def _sb_fwd(qs, ks, vs, tri, wps=()):
    T = qs.shape[0]
    nq = T // SBQ
    n = len(wps)

    def body(q_ref, k_ref, v_ref, tri_ref, *rest):
        w_refs, (o_ref, c_ref), out_refs = rest[:n], rest[n:n + 2], rest[n + 2:2 * n + 2]
        pair = pl.program_id(0)
        i = pl.program_id(1)
        if n:
            ag_start, ag_pass_on, ag_finish = _ag_phases(w_refs, out_refs, *rest[2 * n + 2:])
            pl.when((pair == 0) & (i == 0))(ag_start)
        lane = lax.broadcasted_iota(jnp.int32, (1, 128), 1)
        clane = lax.broadcasted_iota(jnp.int32, (SBQ, 128), 1)
        strict = (lax.broadcasted_iota(jnp.int32, (SBQ, SBQ), 1) < lax.broadcasted_iota(jnp.int32, (SBQ, SBQ), 0))
        tri_v = tri_ref[...]
        qv = q_ref[...]
        first_head = lane < 64
        qhs = (jnp.where(first_head, qv, jnp.zeros_like(qv)), jnp.where(first_head, jnp.zeros_like(qv), qv))

        def block(j, carries, accs, masked):
            rs = pl.ds(pl.multiple_of(j * SBQ, SBQ), SBQ)
            kb = k_ref[rs, :]
            vb = v_ref[rs, :]
            out_c, out_a = [], []
            for hh in range(2):
                _, l1, a = _sb_block_fwd(qhs[hh], kb, tri_v, carries[hh], strict if masked else None)
                out_a.append(accs[hh] + _dot(a.astype(BF16), vb))
                out_c.append(carries[hh] + jnp.sum(l1, axis=1, keepdims=True))
            return out_c, out_a

        zero1 = jnp.zeros((SBQ, 1), F32)
        zero128 = jnp.zeros((SBQ, 128), F32)
        (c0, c1), (a0, a1) = block(i, (zero1, zero1), (zero128, zero128), True)

        def more(state):
            return (state[0] <= i) & (jnp.maximum(jnp.max(state[1]), jnp.max(state[2])) > -SB_DEAD)

        def step(state):
            jj, c0, c1, a0, a1, t0, t1 = state
            j = i - jj
            t0 = jnp.where(clane == j, c0, t0)
            t1 = jnp.where(clane == j, c1, t1)
            (c0, c1), (a0, a1) = block(j, (c0, c1), (a0, a1), False)
            return jj + 1, c0, c1, a0, a1, t0, t1

        jj, c0, c1, a0, a1, t0, t1 = lax.while_loop(
            more, step, (jnp.int32(1), c0, c1, a0, a1, zero128, zero128))
        o_ref[...] = jnp.where(first_head, a0, a1)
        swept = (jj - 1).astype(F32)
        c_ref[0, :, 0:128] = jnp.where(clane == SB_COUNT_LANE, swept, t0)
        c_ref[0, :, 128:256] = jnp.where(clane == SB_COUNT_LANE, swept, t1)
        if n:
            pl.when((pair == 3) & (i == 0))(ag_pass_on)
            pl.when((pair == 3) & (i == nq - 1))(ag_finish)

    return pl.pallas_call(
        body, name="sb_fwd", grid=(4, nq),
        in_specs=[pl.BlockSpec((SBQ, 128), lambda h, i: (i, h)),
                  pl.BlockSpec((T, 128), lambda h, i: (0, h)),
                  pl.BlockSpec((T, 128), lambda h, i: (0, h)),
                  pl.BlockSpec((SBQ, SBQ), lambda h, i: (0, 0))] + [HBM] * n,
        out_specs=[pl.BlockSpec((SBQ, 128), lambda h, i: (i, h)),
                   pl.BlockSpec((1, SBQ, 256), lambda h, i: (h, i, 0))] + [HBM] * n,
        out_shape=[SDS((T, 512), F32), SDS((4, T, 256), F32)] + [SDS((4,) + wp.shape, wp.dtype) for wp in wps],
        scratch_shapes=_ag_sems(n) if n else [],
        compiler_params=_params(dimension_semantics=("arbitrary", "arbitrary")))(qs, ks, vs, tri, *wps)


def _sb_bwd(qs, ks, vs, do, carries, tri, tri_t):
    T = qs.shape[0]
    nq = T // SBQ

    def body(q_ref, k_ref, v_ref, do_ref, c_ref, tri_ref, trit_ref, dq_ref, dk_ref, dv_ref):
        i = pl.program_id(1)

        @pl.when(i == 0)
        def _():
            dk_ref[...] = jnp.zeros_like(dk_ref)
            dv_ref[...] = jnp.zeros_like(dv_ref)

        lane = lax.broadcasted_iota(jnp.int32, (1, 128), 1)
        clane = lax.broadcasted_iota(jnp.int32, (SBQ, 128), 1)
        strict = (lax.broadcasted_iota(jnp.int32, (SBQ, SBQ), 1) < lax.broadcasted_iota(jnp.int32, (SBQ, SBQ), 0))
        tri_v = tri_ref[...]
        trit_v = trit_ref[...]
        qv = q_ref[...]
        dov = do_ref[...].astype(BF16)
        first_head = lane < 64
        qhs = (jnp.where(first_head, qv, jnp.zeros_like(qv)), jnp.where(first_head, jnp.zeros_like(qv), qv))
        dohs = (jnp.where(first_head, dov, jnp.zeros_like(dov)), jnp.where(first_head, jnp.zeros_like(dov), dov))
        cts = (c_ref[0, :, 0:128], c_ref[0, :, 128:256])

        def block(j, pcarries, dqs, masked):
            rs = pl.ds(pl.multiple_of(j * SBQ, SBQ), SBQ)
            kb = k_ref[rs, :]
            vb = v_ref[rs, :]
            out_p, out_q = [], []
            dk = jnp.zeros((SBQ, 128), F32)
            dv = jnp.zeros((SBQ, 128), F32)
            for hh in range(2):
                carry = jnp.sum(jnp.where(clane == j, cts[hh], 0.0), axis=1, keepdims=True)
                lb, _, a = _sb_block_fwd(qhs[hh], kb, tri_v, carry, strict if masked else None)
                g = a * _dot_nt(dohs[hh], vb)
                p = _dot(g.astype(BF16), trit_v) + pcarries[hh]
                dz = g - (g + p) * jnp.exp(lb)
                if masked:
                    dz = jnp.where(strict, dz, 0.0)
                dzb = dz.astype(BF16)
                dk = dk + _dot_tn(dzb, qhs[hh])
                dv = dv + _dot_tn(a.astype(BF16), dohs[hh])
                out_p.append(pcarries[hh] + jnp.sum(g, axis=1, keepdims=True))
                out_q.append(dqs[hh] + _dot(dzb, kb))
            dk_ref[rs, :] += dk
            dv_ref[rs, :] += dv
            return out_p, out_q

        def step(j, state):
            (p0, p1), (q0, q1) = block(j, (state[0], state[1]), (state[2], state[3]), False)
            return p0, p1, q0, q1

        swept = jnp.max(jnp.where(clane == SB_COUNT_LANE, cts[0], 0.0)).astype(jnp.int32)
        first = i - jnp.clip(swept, 0, i)
        zero1 = jnp.zeros((SBQ, 1), F32)
        zero128 = jnp.zeros((SBQ, 128), F32)
        p0, p1, q0, q1 = lax.fori_loop(first, i, step, (zero1, zero1, zero128, zero128))
        _, (q0, q1) = block(i, (p0, p1), (q0, q1), True)
        dq_ref[...] = jnp.where(first_head, q0, q1)

    return pl.pallas_call(
        body, name="sb_bwd", grid=(4, nq),
        in_specs=[pl.BlockSpec((SBQ, 128), lambda h, i: (i, h)),
                  pl.BlockSpec((T, 128), lambda h, i: (0, h)),
                  pl.BlockSpec((T, 128), lambda h, i: (0, h)),
                  pl.BlockSpec((SBQ, 128), lambda h, i: (i, 4 + h)),
                  pl.BlockSpec((1, SBQ, 256), lambda h, i: (h, i, 0)),
                  pl.BlockSpec((SBQ, SBQ), lambda h, i: (0, 0)),
                  pl.BlockSpec((SBQ, SBQ), lambda h, i: (0, 0))],
        out_specs=[pl.BlockSpec((SBQ, 128), lambda h, i: (i, h)),
                   pl.BlockSpec((T, 128), lambda h, i: (0, h)),
                   pl.BlockSpec((T, 128), lambda h, i: (0, h))],
        out_shape=[SDS((T, 512), F32), SDS((T, 512), F32), SDS((T, 512), F32)],
        compiler_params=_params(dimension_semantics=("parallel", "arbitrary")))(qs, ks, vs, do, carries, tri, tri_t)


def _mem_fwd(mem, mem_norm_w, w_mkv, mk_norm_w):
    M = mem.shape[0]

    def body(mem_ref, wn_ref, w_ref, wk_ref, mn_ref, kraw_ref, k_ref, v_ref):
        mv = mem_ref[...]
        mn = (mv * _rsqrt_ms(mv) * wn_ref[...]).astype(BF16)
        mn_ref[...] = mn
        for s in range(2):
            cols = slice(512 * s, 512 * (s + 1))
            ks = _dot(mn, w_ref[s, :D, :])
            kraw_ref[:, cols] = ks
            v_ref[:, cols] = _dot(mn, w_ref[2 + s, :D, :]).astype(BF16)
            for h in range(2):
                x = ks[:, MEM_HD * h:MEM_HD * (h + 1)]
                k_ref[:, 512 * s + MEM_HD * h:512 * s + MEM_HD * (h + 1)] = (
                    x * _rsqrt_ms(x) * wk_ref[...]).astype(BF16)

    vm = pl.BlockSpec(memory_space=pltpu.VMEM)
    return pl.pallas_call(
        body, name="mem_fwd", in_specs=[vm] * 4, out_specs=[vm] * 4,
        out_shape=[SDS((M, D), BF16), SDS((M, D), F32), SDS((M, D), BF16), SDS((M, D), BF16)],
        compiler_params=_params())(mem, mem_norm_w, w_mkv, mk_norm_w)


def _xattn_fwd(qraw, k, v, wq):
    T = qraw.shape[0]
    M = k.shape[0]

    def body(q_ref, k_ref, v_ref, wq_ref, o_ref):
        for h in range(MEM_HEADS):
            sl = slice(MEM_HD * h, MEM_HD * (h + 1))
            x = q_ref[:, sl]
            q = (x * _rsqrt_ms(x) * wq_ref[...]).astype(BF16)
            s = _dot_nt(q, k_ref[:, sl]) * (1.0 / math.sqrt(MEM_HD))
            s = s - jnp.max(s, axis=-1, keepdims=True)
            e = jnp.exp(s)
            p = e / jnp.sum(e, axis=-1, keepdims=True)
            o_ref[:, sl] = _dot(p.astype(BF16), v_ref[:, sl]).astype(BF16)

    return pl.pallas_call(
        body, name="xattn_fwd", grid=(T // TB,),
        in_specs=[_row(D), _const((M, D)), _const((M, D)), _const((1, MEM_HD))],
        out_specs=_row(D), out_shape=SDS((T, D), BF16), compiler_params=_params())(qraw, k, v, wq)


def _xattn_bwd(qraw, k, v, wq, do):
    T = qraw.shape[0]
    M = k.shape[0]

    def body(q_ref, k_ref, v_ref, wq_ref, do_ref, dq_ref, dk_ref, dv_ref, dw_ref):
        @pl.when(pl.program_id(0) == 0)
        def _():
            dk_ref[...] = jnp.zeros_like(dk_ref)
            dv_ref[...] = jnp.zeros_like(dv_ref)
            dw_ref[...] = jnp.zeros_like(dw_ref)

        w = wq_ref[...]
        for h in range(MEM_HEADS):
            sl = slice(MEM_HD * h, MEM_HD * (h + 1))
            x = q_ref[:, sl]
            r = _rsqrt_ms(x)
            n = x * r
            q = (n * w).astype(BF16)
            kb = k_ref[:, sl]
            s = _dot_nt(q, kb) * (1.0 / math.sqrt(MEM_HD))
            s = s - jnp.max(s, axis=-1, keepdims=True)
            e = jnp.exp(s)
            p = e / jnp.sum(e, axis=-1, keepdims=True)
            dob = do_ref[:, sl].astype(BF16)
            dp = _dot_nt(dob, v_ref[:, sl])
            ds = (p * (dp - jnp.sum(dp * p, axis=-1, keepdims=True)) * (1.0 / math.sqrt(MEM_HD))).astype(BF16)
            dv_ref[:, sl] += _dot_tn(p.astype(BF16), dob)
            dk_ref[:, sl] += _dot_tn(ds, q)
            dqn = _dot(ds, kb)
            dn = dqn * w
            dq_ref[:, sl] = r * (dn - n * jnp.mean(dn * n, axis=-1, keepdims=True))
            dw_ref[...] += _colsum8(dqn * n)

    return pl.pallas_call(
        body, name="xattn_bwd", grid=(T // TB,),
        in_specs=[_row(D), _const((M, D)), _const((M, D)), _const((1, MEM_HD)), _row(D)],
        out_specs=[_row(D), _const((M, D)), _const((M, D)), _const((8, MEM_HD))],
        out_shape=[SDS((T, D), F32), SDS((M, D), F32), SDS((M, D), F32), SDS((8, MEM_HD), F32)],
        compiler_params=_params())(qraw, k, v, wq, do)


def _mem_bwd(mem, mem_norm_w, w_mkv, mk_norm_w, mem_n, k_raw, dk, dv):
    M = mem.shape[0]

    def body(mem_ref, wn_ref, w_ref, wk_ref, mn_ref, kraw_ref, dk_ref, dv_ref, dw_ref, dwn_ref, dwk_ref, dkv_scr):
        wk = wk_ref[...]
        dwk = jnp.zeros((8, MEM_HD), F32)
        for h in range(MEM_HEADS):
            sl = slice(MEM_HD * h, MEM_HD * (h + 1))
            x = kraw_ref[:, sl]
            r = _rsqrt_ms(x)
            n = x * r
            dkh = dk_ref[:, sl]
            dn = dkh * wk
            dkv_scr[:, sl] = (r * (dn - n * jnp.mean(dn * n, axis=-1, keepdims=True))).astype(BF16)
            dwk = dwk + _colsum8(dkh * n)
        dwk_ref[...] = dwk
        dkv_scr[:, D:] = dv_ref[...].astype(BF16)
        dkv = dkv_scr[...]
        mn = mn_ref[...]
        dmn = jnp.zeros((M, D), F32)
        for s in range(4):
            part = dkv[:, 512 * s:512 * (s + 1)]
            dw_ref[s] = _dot_tn(mn, part).astype(BF16)
            dmn = dmn + _dot_nt(part, w_ref[s, :D, :])
        mv = mem_ref[...]
        dwn_ref[...] = _colsum8(dmn * (mv * _rsqrt_ms(mv)))

    vm = pl.BlockSpec(memory_space=pltpu.VMEM)
    return pl.pallas_call(
        body, name="mem_bwd", in_specs=[vm] * 8, out_specs=[vm] * 3,
        out_shape=[SDS((4, D, 512), BF16), SDS((8, D), F32), SDS((8, MEM_HD), F32)],
        scratch_shapes=[pltpu.VMEM((M, 2 * D), BF16)],
        compiler_params=_params())(mem, mem_norm_w, w_mkv, mk_norm_w, mem_n, k_raw, dk, dv)


def _local_step(x, mem, tgt, w, later_weights=None, later_partials=None, last_partials=None):
    T = x.shape[0]
    wgk = jnp.zeros((128, 256), F32).at[:16].set(w["w_gk_up"].astype(F32))
    wg512 = jnp.tile(w["gla_norm_w"], (1, 4))
    ws512 = jnp.tile(w["sb_norm_w"], (1, 8))
    bd64 = _group_ones(512, 64)
    cm = _gla_consts()
    tri = _sb_tri()
    tri_t = tri.T

    proj, gk, qs, ks, vs, h1 = _matmul(
        x, w["w_in"], mode="nn", tm=TB, tn=DIN_P, name="mm_proj",
        epilogue=_norm_then_proj_split(w["mix_norm_w"], wgk, w["b_gk"], T, TB))
    o_g, states = _gla_fwd(proj, gk, cm)
    if later_weights is None:
        o_s, carries = _sb_fwd(qs, ks, vs, tri)
    else:
        o_s, carries, *gathered = _sb_fwd(qs, ks, vs, tri, later_weights[0])
        w = {**w, **later_weights[1](gathered)}
    cat = _mix_cat(o_g, proj, o_s, wg512, ws512, bd64)
    tm3 = min(2 * TB, T)
    x1, h2 = _matmul(cat, w["w_out"], mode="nn", tm=tm3, tn=D, res=x, name="mm_out",
                     epilogue=_then_norm_fwd(w["xattn_norm_w"], T, tm3))
    qraw = _matmul(h2, w["w_mq"], mode="nn", tm=tm3, tn=D, name="mm_mq")
    mem_n, k_raw, k_n, v_m = _mem_fwd(mem, w["mem_norm_w"], w["w_mkv"], w["mk_norm_w"])
    om = _xattn_fwd(qraw, k_n, v_m, w["mq_norm_w"])
    x2, h3 = _matmul(om, w["w_mo"], mode="nn", tm=tm3, tn=D, res=x1, name="mm_mo",
                     epilogue=_then_norm_fwd(w["ffn_norm_w"], T, tm3))
    gu, act = _gate_up_act(h3, w["w_gate_up"])
    dx3, loss_rows = _matmul(act, w["w_down"], mode="nn", tm=TB, tn=D, res=x2, name="mm_down",
                             epilogue=_then_loss(tgt, TB))

    g = {}
    dw_tk = min(DW_TK, T)
    dw = dict(mode="tn", tk=dw_tk, out_dtype=BF16)
    g["w_down"] = _matmul(act, dx3, tm=1408, tn=D, name="mm_dw_down", **dw)
    dgu = _down_bwd(dx3, w["w_down"], gu)
    g["w_gate_up"] = _matmul(
        h3, dgu, tm=D, tn=FF_TN, by_column_tile=True, mnk=(D, 2 * D_FF, T), name="mm_dw_gate_up",
        b_spec=pl.BlockSpec((None, dw_tk, FF_TN), lambda j, i, k: (j // 2, k, j % 2)), **dw)
    dx2, g["ffn_norm_w"] = _matmul(
        dgu, w["w_gate_up"], mode="nt", tm=tm3, tn=D, tk=FF_TN, mnk=(T, D, 2 * D_FF), name="mm_dh3",
        a_spec=pl.BlockSpec((None, tm3, FF_TN), lambda j, i, k: (k // 2, i, k % 2)),
        b_spec=pl.BlockSpec((None, D, FF_TN), lambda j, i, k: (k, 0, 0)),
        epilogue=_then_norm_bwd(x2, w["ffn_norm_w"], dx3, tm3))
    g["w_mo"] = _matmul(om, dx2, tm=D, tn=D, name="mm_dw_mo", **dw)
    dom = _matmul(dx2, w["w_mo"], mode="nt", tm=tm3, tn=D, name="mm_dom")
    dqraw, dk_n, dv_m, g["mq_norm_w"] = _xattn_bwd(qraw, k_n, v_m, w["mq_norm_w"], dom)
    g["w_mkv"], g["mem_norm_w"], g["mk_norm_w"] = _mem_bwd(
        mem, w["mem_norm_w"], w["w_mkv"], w["mk_norm_w"], mem_n, k_raw, dk_n, dv_m)
    g["w_mq"] = _matmul(h2, dqraw, tm=D, tn=D, name="mm_dw_mq", **dw)
    dx1, g["xattn_norm_w"] = _matmul(dqraw, w["w_mq"], mode="nt", tm=tm3, tn=D, name="mm_dh2",
                                     epilogue=_then_norm_bwd(x1, w["xattn_norm_w"], dx2, tm3))
    g["w_out"] = _matmul(cat, dx1, tm=D, tn=D, name="mm_dw_out", **dw)
    do_gs, dg_g, dwg, dws = _matmul(dx1, w["w_out"], mode="nt", tm=TB, tn=D, name="mm_dcat",
                                    epilogue=_then_mix_cat_bwd(o_g, proj, o_s, wg512, ws512, bd64, TB))
    dq_s, dk_s, dv_s = _sb_bwd(qs, ks, vs, do_gs, carries, tri, tri_t)
    parts = () if later_partials is None else later_partials(g)
    dq_g, dk_g, dv_g, dgk, *received = _gla_bwd(proj, gk, cm, states, do_gs, parts)
    dproj, dwgk, g["b_gk"] = _dproj_assemble(proj, dq_g, dk_g, dv_g, dg_g, dq_s, dk_s, dv_s, dgk, wgk, w["b_gk"])
    g["w_in"] = _matmul(h1, dproj, tm=D, tn=640, name="mm_dw_in", **dw)
    g["w_gk_up"] = dwgk[:16]
    last_parts = [] if last_partials is None else last_partials(g)
    grad_x, g["mix_norm_w"], *last_received = _matmul(
        dproj, w["w_in"], mode="nt", tm=TB, tn=D, name="mm_dh1",
        epilogue=_then_norm_bwd(x, w["mix_norm_w"], dx1, TB), exchange=last_parts)

    g["gla_norm_w"], g["sb_norm_w"] = dwg, dws
    if later_partials is None:
        g["gla_norm_w"] = dwg.reshape(8, 4, 128).sum(axis=1)
        g["sb_norm_w"] = dws.reshape(8, 8, 64).sum(axis=1)
        for n in SMALL:
            g[n] = jnp.sum(g[n], axis=0, keepdims=True)
        return jnp.sum(loss_rows), grad_x, g
    return jnp.sum(loss_rows), grad_x, g, (list(parts) + last_parts, list(received) + last_received)


def _mesh_pos():
    return lax.axis_index("x"), lax.axis_index("y"), lax.axis_index("c")


def _other_chips(x, y):
    return [(1 - x, y), (x, 1 - y), (1 - x, 1 - y)]


HBM = pl.BlockSpec(memory_space=pl.ANY)


def _ag_phases(w_refs, out_refs, send_sems, recv_sems):
    n = len(w_refs)
    x, y, c = _mesh_pos()
    me = 2 * x + y
    sibling = (x, y, 1 - c)
    chips = _other_chips(x, y)
    mine, theirs = c, 1 - c

    def copy(a, k, src, dst, to):
        return pltpu.make_async_remote_copy(src_ref=src, dst_ref=dst, send_sem=send_sems.at[6 * a + k],
                                            recv_sem=recv_sems.at[6 * a + k], device_id=to, device_id_type=MESH)

    def firsts():
        return [copy(a, k, w_refs[a].at[mine], out_refs[a].at[me, mine], (cx, cy, c))
                for a in range(n) for k, (cx, cy) in enumerate(chips)]

    def landed(a, k, half):
        cx, cy = chips[k]
        return out_refs[a].at[2 * cx + cy, half]

    def passes():
        return [copy(a, 3 + k, landed(a, k, mine), landed(a, k, mine), sibling) for a in range(n) for k in range(3)]

    def start():
        for cp in firsts():
            cp.start()

    def pass_on():
        for a in range(n):
            for k, (cx, cy) in enumerate(chips):
                copy(a, k, landed(a, k, mine), landed(a, k, mine), (cx, cy, c)).wait_recv()
                copy(a, 3 + k, landed(a, k, mine), landed(a, k, mine), sibling).start()

    def finish():
        for a in range(n):
            for k in range(3):
                copy(a, 3 + k, landed(a, k, theirs), landed(a, k, theirs), sibling).wait_recv()
        for cp in firsts() + passes():
            cp.wait_send()

    return start, pass_on, finish


def _ag_sems(n):
    return [pltpu.SemaphoreType.DMA((6 * n,)), pltpu.SemaphoreType.DMA((6 * n,))]


def _ag_weights(wps):
    n = len(wps)

    def body(*refs):
        for phase in _ag_phases(refs[:n], refs[n:2 * n], *refs[2 * n:]):
            phase()

    return pl.pallas_call(
        body, name="ag_weights", in_specs=[HBM] * n, out_specs=[HBM] * n,
        out_shape=[SDS((4,) + wp.shape, wp.dtype) for wp in wps], scratch_shapes=_ag_sems(n),
        compiler_params=pltpu.CompilerParams(has_side_effects=True))(*wps)


def _rs_swap_halves(gps, name):
    n = len(gps)

    def body(*refs):
        g_refs, out_refs = refs[:n], refs[n:2 * n]
        send_sems, recv_sems = refs[2 * n:]
        x, y, c = _mesh_pos()
        copies = [pltpu.make_async_remote_copy(
            src_ref=g_refs[a].at[:, 1 - c], dst_ref=out_refs[a], send_sem=send_sems.at[a], recv_sem=recv_sems.at[a],
            device_id=(x, y, 1 - c), device_id_type=MESH) for a in range(n)]
        for cp in copies:
            cp.start()
        for cp in copies:
            cp.wait()

    return pl.pallas_call(
        body, name=name, in_specs=[HBM] * n, out_specs=[HBM] * n,
        out_shape=[SDS((4,) + gp.shape[2:], gp.dtype) for gp in gps],
        scratch_shapes=[pltpu.SemaphoreType.DMA((n,)), pltpu.SemaphoreType.DMA((n,))],
        compiler_params=pltpu.CompilerParams(has_side_effects=True))(*gps)


def _rs_add_halves(gp, other, c_arr, name):
    h, w = other.shape[1:]

    def body(c_ref, a_ref, b_ref, o_ref):
        o_ref[...] = (a_ref[0].astype(F32) + b_ref[...].astype(F32)).astype(BF16)

    return pl.pallas_call(
        body, name=name,
        grid_spec=pltpu.PrefetchScalarGridSpec(
            num_scalar_prefetch=1, grid=(4,),
            in_specs=[pl.BlockSpec((1, 1, h, w), lambda s, c: (s, c[0], 0, 0)),
                      pl.BlockSpec((1, h, w), lambda s, c: (s, 0, 0))],
            out_specs=pl.BlockSpec((1, h, w), lambda s, c: (s, 0, 0))),
        out_shape=SDS((4, h, w), BF16), compiler_params=_params())(c_arr, gp, other)


def _rs_exchange_phases(p_refs, out_refs, send_sems, recv_sems):
    n = len(p_refs)
    x, y, c = _mesh_pos()
    me = 2 * x + y
    chips = _other_chips(x, y)

    def sends():
        return [pltpu.make_async_remote_copy(
            src_ref=p_refs[a].at[2 * cx + cy], dst_ref=out_refs[a].at[me], send_sem=send_sems.at[3 * a + k],
            recv_sem=recv_sems.at[3 * a + k], device_id=(cx, cy, c), device_id_type=MESH)
            for a in range(n) for k, (cx, cy) in enumerate(chips)]

    def start():
        for cp in sends():
            cp.start()

    def finish():
        for a in range(n):
            for k, (cx, cy) in enumerate(chips):
                slot = out_refs[a].at[2 * cx + cy]
                pltpu.make_async_remote_copy(
                    src_ref=slot, dst_ref=slot, send_sem=send_sems.at[3 * a + k], recv_sem=recv_sems.at[3 * a + k],
                    device_id=(cx, cy, c), device_id_type=MESH).wait_recv()
        for cp in sends():
            cp.wait_send()

    return start, finish


def _rs_exchange_sems(n):
    return [pltpu.SemaphoreType.DMA((3 * n,)), pltpu.SemaphoreType.DMA((3 * n,))]


def _rs_add_chips(recv, part, me_arr, name):
    h, w = part.shape[1:]
    th = h // 2 if (h // 2) % 16 == 0 else h

    def body(me_ref, r_ref, p_ref, o_ref):
        me = me_ref[0]
        total = None
        for k in range(4):
            term = jnp.where(me == k, p_ref[k], r_ref[k]).astype(F32)
            total = term if total is None else total + term
        o_ref[...] = total

    spec = pl.BlockSpec((4, th, w), lambda t, me: (0, t, 0))
    return pl.pallas_call(
        body, name=name,
        grid_spec=pltpu.PrefetchScalarGridSpec(
            num_scalar_prefetch=1, grid=(h // th,), in_specs=[spec, spec],
            out_specs=pl.BlockSpec((th, w), lambda t, me: (t, 0))),
        out_shape=SDS((h, w), F32), compiler_params=_params())(me_arr, recv, part)


def _rs_share(halves):
    n = len(halves)

    def body(*refs):
        h_refs, out_refs = refs[:n], refs[n:2 * n]
        send_sems, recv_sems = refs[2 * n:]
        x, y, c = _mesh_pos()
        copies = [pltpu.make_async_remote_copy(
            src_ref=h_refs[a], dst_ref=out_refs[a], send_sem=send_sems.at[a], recv_sem=recv_sems.at[a],
            device_id=(x, y, 1 - c), device_id_type=MESH) for a in range(n)]
        for cp in copies:
            cp.start()
        for cp in copies:
            cp.wait()

    return pl.pallas_call(
        body, name="rs_share", in_specs=[HBM] * n, out_specs=[HBM] * n,
        out_shape=[SDS(hs.shape, hs.dtype) for hs in halves],
        scratch_shapes=[pltpu.SemaphoreType.DMA((n,)), pltpu.SemaphoreType.DMA((n,))],
        compiler_params=pltpu.CompilerParams(has_side_effects=True))(*halves)


def _small_rows(partials):
    lanes = np.arange(512)
    fold = jnp.asarray((lanes[:, None] % 64 == np.arange(128)[None, :]).astype(np.float32), dtype=BF16)
    n_small = len(SMALL)

    def body(*refs):
        fold_ref, o_ref = refs[n_small], refs[n_small + 1]
        o_ref[...] = jnp.zeros_like(o_ref)
        for i, name in enumerate(SMALL):
            v = refs[i][...]
            if name == "gla_norm_w":
                v = v[:, 0:128] + v[:, 128:256] + v[:, 256:384] + v[:, 384:512]
            if name == "sb_norm_w":
                v = _dot_exact(v, fold_ref[...])
            n = SMALL_SIZES[name]
            o_ref[i:i + 1, 0:n] = jnp.sum(v, axis=0, keepdims=True)[:, 0:n]

    vm = pl.BlockSpec(memory_space=pltpu.VMEM)
    return pl.pallas_call(body, name="small_rows", in_specs=[vm] * (n_small + 1), out_specs=vm,
                          out_shape=SDS((SMALL_ROWS, 1024), F32))(*[partials[n] for n in SMALL], fold)


def _small_gather(s):
    def gather(s_ref, out_ref, send_sems, recv_sems, local_sem):
        x, y, c = _mesh_pos()
        me = 4 * x + 2 * y + c
        local = pltpu.make_async_copy(s_ref, out_ref.at[me], local_sem)
        local.start()
        peers = []
        for r in range(1, 8):
            px = 1 - x if r & 4 else x
            py = 1 - y if r & 2 else y
            pc = 1 - c if r & 1 else c
            peers.append((px, py, pc))
        sends = []
        for k, peer in enumerate(peers):
            cp = pltpu.make_async_remote_copy(
                src_ref=s_ref, dst_ref=out_ref.at[me], send_sem=send_sems.at[k], recv_sem=recv_sems.at[k],
                device_id=peer, device_id_type=MESH)
            cp.start()
            sends.append(cp)
        for k, (px, py, pc) in enumerate(peers):
            slot = out_ref.at[4 * px + 2 * py + pc]
            pltpu.make_async_remote_copy(
                src_ref=slot, dst_ref=slot, send_sem=send_sems.at[k], recv_sem=recv_sems.at[k],
                device_id=(px, py, pc), device_id_type=MESH).wait_recv()
        for cp in sends:
            cp.wait_send()
        local.wait()

    return pl.pallas_call(
        gather, name="small_gather", in_specs=[HBM], out_specs=HBM,
        out_shape=SDS((8, SMALL_ROWS, 1024), F32),
        scratch_shapes=[pltpu.SemaphoreType.DMA((7,)), pltpu.SemaphoreType.DMA((7,)), pltpu.SemaphoreType.DMA],
        compiler_params=pltpu.CompilerParams(has_side_effects=True))(s)


def _adamw_update(w, g, m, v):
    mn = ADAM_B1 * m + (1.0 - ADAM_B1) * g
    vn = ADAM_B2 * v + (1.0 - ADAM_B2) * (g * g)
    c1 = 1.0 - ADAM_B1 ** ADAM_STEP
    c2 = 1.0 - ADAM_B2 ** ADAM_STEP
    return -ADAM_LR * ((mn / c1) / (jnp.sqrt(vn / c2) + ADAM_EPS) + ADAM_WD * w), mn, vn


def _small_update(parts, w, m, v):
    n_small = len(SMALL)

    def body(p_ref, *refs):
        ins, outs = refs[:3 * n_small], refs[3 * n_small:]
        total = p_ref[0]
        for k in range(1, 8):
            total = total + p_ref[k]
        for i, name in enumerate(SMALL):
            g = total[i:i + 1, 0:SMALL_SIZES[name]]
            d, mn, vn = _adamw_update(ins[i][...], g, ins[n_small + i][...], ins[2 * n_small + i][...])
            for slot, val in enumerate((g, d, mn, vn)):
                outs[slot * n_small + i][...] = val

    vm = pl.BlockSpec(memory_space=pltpu.VMEM)
    shapes = [SDS((1, SMALL_SIZES[n]), F32) for n in SMALL]
    outs = pl.pallas_call(
        body, name="small_adamw", in_specs=[vm] * (1 + 3 * n_small), out_specs=[vm] * (4 * n_small),
        out_shape=shapes * 4)(parts, *[w[n] for n in SMALL], *[m[n] for n in SMALL], *[v[n] for n in SMALL])
    return [dict(zip(SMALL, outs[s * n_small:(s + 1) * n_small])) for s in range(4)]


def _adamw(w, g, m, v, name):
    rows, cols = w.shape
    tr = rows
    for cand in (512, 352, 256):
        if rows > cand and rows % cand == 0:
            tr = cand
            break

    def body(w_ref, g_ref, m_ref, v_ref, d_ref, mo_ref, vo_ref):
        d_ref[...], mo_ref[...], vo_ref[...] = _adamw_update(w_ref[...], g_ref[...], m_ref[...], v_ref[...])

    spec = pl.BlockSpec((tr, cols), lambda i: (i, 0))
    return pl.pallas_call(
        body, name=name, grid=(rows // tr,), in_specs=[spec] * 4, out_specs=[spec] * 3,
        out_shape=[SDS((rows, cols), F32)] * 3, compiler_params=_params())(w, g, m, v)


SMALL_SIZES = {"mix_norm_w": 1024, "b_gk": 256, "gla_norm_w": 128, "sb_norm_w": 64, "xattn_norm_w": 1024,
               "mem_norm_w": 1024, "mq_norm_w": 256, "mk_norm_w": 256, "ffn_norm_w": 1024}


ROWS_OF = (("w_out", 256), ("w_mq", 256), ("w_mo", 256), ("w_down", 704))
WIN_ROWS = 1056
LATER = ("rows", "gate_up", "mkv")


def _shard_buffers(d, dtype):
    rows = jnp.concatenate([d[n] for n, _ in ROWS_OF], axis=0).astype(dtype)
    gk = jnp.pad(d["w_gk_up"], ((0, WIN_ROWS - D - 16), (0, DIN // 4 - 64)))
    win = jnp.concatenate([d["w_in"], gk], axis=0).astype(dtype)
    return [rows, d["w_gate_up"].astype(dtype), d["w_mkv"].astype(dtype)], win


def _in_halves(a):
    return a.reshape(a.shape[:-2] + (2, a.shape[-2] // 2, a.shape[-1]))


def _whole(a):
    return a.reshape(a.shape[:-3] + (2 * a.shape[-2], a.shape[-1]))


def _first_weights(win):
    s0, s1, s2, s3 = (win[j, :D] for j in range(4))
    w_in = jnp.concatenate([s0, s1[:, :764], s2[:, 8:], s3, s1[:, 764:], s2[:, :8],
                            jnp.zeros((D, DIN_P - DIN), win.dtype)], axis=1)
    return {"w_in": w_in, "w_gk_up": win[:, D:D + 16, :64].transpose(1, 0, 2).reshape(16, 256)}


def _later_weights(rows, gate_up, mkv):
    out, off = {"w_gate_up": gate_up, "w_mkv": mkv}, 0
    for n, r in ROWS_OF:
        out[n] = rows[:, off:off + r].reshape(4 * r, 1024)
        off += r
    return out


def _later_grad_buffers(g):
    rows = jnp.concatenate([g[n].reshape(4, r, 1024) for n, r in ROWS_OF], axis=1)
    return [rows, g["w_gate_up"], g["w_mkv"]]


def _win_grad_buffer(g):
    gk = g["w_gk_up"].astype(BF16).reshape(16, 4, 64).transpose(1, 0, 2)
    gk = jnp.pad(gk, ((0, 0), (0, WIN_ROWS - D - 16), (0, DIN // 4 - 64)))
    gi = g["w_in"]
    shards = jnp.stack([gi[:, :772], jnp.concatenate([gi[:, 772:1536], gi[:, C_LR:C_LR + 8]], axis=1),
                        jnp.concatenate([gi[:, C_LR + 8:C_LR + 16], gi[:, 1536:2300]], axis=1), gi[:, 2300:C_LR]])
    return jnp.concatenate([shards, gk], axis=1)


def _shard_grads(rows, gate_up, mkv, win):
    out, off = {"w_gate_up": gate_up, "w_mkv": mkv, "w_in": win[:D], "w_gk_up": win[D:D + 16, :64]}, 0
    for n, r in ROWS_OF:
        out[n] = rows[off:off + r]
        off += r
    return out


def kernel(x, mem, mix_norm_w, w_in, w_gk_up, b_gk, gla_norm_w, sb_norm_w, w_out, xattn_norm_w, mem_norm_w, w_mq, w_mkv, mq_norm_w, mk_norm_w, w_mo, ffn_norm_w, w_gate_up, w_down, loss_target, m_mix_norm_w, m_w_in, m_w_gk_up, m_b_gk, m_gla_norm_w, m_sb_norm_w, m_w_out, m_xattn_norm_w, m_mem_norm_w, m_w_mq, m_w_mkv, m_mq_norm_w, m_mk_norm_w, m_w_mo, m_ffn_norm_w, m_w_gate_up, m_w_down, v_mix_norm_w, v_w_in, v_w_gk_up, v_b_gk, v_gla_norm_w, v_sb_norm_w, v_w_out, v_xattn_norm_w, v_mem_norm_w, v_w_mq, v_w_mkv, v_mq_norm_w, v_mk_norm_w, v_w_mo, v_ffn_norm_w, v_w_gate_up, v_w_down):
    args = dict(locals())
    wts = {n: args[n][0] if n in BIG else args[n] for n in WEIGHTS}
    mom = {n: args["m_" + n][0] if n in BIG else args["m_" + n] for n in WEIGHTS}
    var = {n: args["v_" + n][0] if n in BIG else args["v_" + n] for n in WEIGHTS}

    c = lax.axis_index("c")
    chip = 2 * lax.axis_index("x") + lax.axis_index("y")
    c_arr = c.astype(jnp.int32).reshape(1)
    chip_arr = chip.astype(jnp.int32).reshape(1)
    def own_slot_filled(gathered, mine):
        return [_whole(lax.dynamic_update_slice(got, wp[None], (chip, 0, 0, 0))) for got, wp in zip(gathered, mine)]

    def chip_partials(names, buffers, tag):
        gps = [_in_halves(b) for b in buffers]
        return [_rs_add_halves(gp, other, c_arr, "rs_add_halves_" + n)
                for n, gp, other in zip(names, gps, _rs_swap_halves(gps, "rs_swap_halves_" + tag))]

    later_wps, win_wp = _shard_buffers(wts, BF16)
    later_wps, win_wp = [_in_halves(b) for b in later_wps], _in_halves(win_wp)
    first = _first_weights(*own_slot_filled(_ag_weights([win_wp]), [win_wp]))
    first.update({n: wts[n] for n in SMALL})

    loss, grad_x, g, (parts, received) = _local_step(
        x[0], mem[0], loss_target[0], first,
        later_weights=(later_wps, lambda gathered: _later_weights(*own_slot_filled(gathered, later_wps))),
        later_partials=lambda g: chip_partials(LATER, _later_grad_buffers(g), "later"),
        last_partials=lambda g: chip_partials(("win",), [_win_grad_buffer(g)], "win"))
    loss = lax.psum(loss, ("x", "y", "c"))

    mine = [_rs_add_chips(recv, part, chip_arr, "rs_add_chips_" + n)
            for n, recv, part in zip(LATER + ("win",), received, parts)]
    totals = [jnp.concatenate([jnp.where(c == 0, m, t), jnp.where(c == 0, t, m)], axis=0)
              for m, t in zip(mine, _rs_share(mine))]
    grads = _shard_grads(*totals)

    small_grads, delta, new_m, new_v = _small_update(_small_gather(_small_rows(g)), wts, mom, var)
    grads.update(small_grads)
    for n in BIG:
        w2 = wts[n].reshape(-1, wts[n].shape[-1])
        d_, m_, v_ = _adamw(w2, grads[n].reshape(w2.shape), mom[n].reshape(w2.shape), var[n].reshape(w2.shape),
                            "adamw_" + n)
        delta[n], new_m[n], new_v[n] = (t.reshape((1,) + wts[n].shape) for t in (d_, m_, v_))
        grads[n] = grads[n].reshape((1,) + wts[n].shape)

    return (loss, grad_x[None], *[grads[n] for n in WEIGHTS], *[delta[n] for n in WEIGHTS],
            *[new_m[n] for n in WEIGHTS], *[new_v[n] for n in WEIGHTS])
```
